```python
import jax, jax.numpy as jnp
from jax import lax
import numpy as np

D_MODEL = 2048
BATCH = 1
SEQ = 8192
DEPTH = 1

GRID_W = 64
CTX_LEN = 256
MIX_WIDTH = D_MODEL
POOL_WIDTH = MIX_WIDTH // 2
POOL_WINDOWS = (2, 4, 8, 16)
N_POOL_GROUPS = len(POOL_WINDOWS)
POOL_GROUP = POOL_WIDTH // N_POOL_GROUPS
MLSTM_WIDTH = MIX_WIDTH - POOL_WIDTH
MLSTM_HEADS = 4
MLSTM_HEAD_DIM = MLSTM_WIDTH // MLSTM_HEADS
CHUNK = 64
QK_CONV = 3
N_GROUPS = 4
EXPERTS_PER_GROUP = 8
N_EXPERTS = N_GROUPS * EXPERTS_PER_GROUP
TOP_K = 2
D_EXPERT = 512
EPS = 1e-6
Q0 = POOL_WIDTH
O0 = Q0 + MLSTM_WIDTH
K0 = O0 + MLSTM_WIDTH
V0 = K0 + MLSTM_WIDTH
G0 = V0 + MLSTM_WIDTH
IN_COLS = G0 + 4 * MLSTM_HEADS

kernel_name = "hymba_pool_mlstm_hmoe_dit_layer"


def rmsnorm(x, g):
    xf = x.astype(jnp.float32)
    y = xf * lax.rsqrt(jnp.mean(xf * xf, axis=-1, keepdims=True) + EPS)
    return (y * g).astype(x.dtype)


def modulate(h, shift, scale):
    return h * (1.0 + scale) + shift


def adaln(cond, w, b):
    return jax.nn.silu(cond) @ w + b


def box_mean(x, win, axis):
    L = x.shape[axis]
    t = np.arange(L)
    lo = np.clip(t - win // 2, 0, L)
    hi = np.clip(t - win // 2 + win, 0, L)
    xf = jnp.moveaxis(x.astype(jnp.float32), axis, 0)
    cs = jnp.concatenate([jnp.zeros_like(xf[:1]), jnp.cumsum(xf, axis=0)], axis=0)
    cnt = jnp.asarray(hi - lo, jnp.float32).reshape((L,) + (1,) * (xf.ndim - 1))
    return jnp.moveaxis((cs[hi] - cs[lo]) / cnt, 0, axis)


def pool_mix(u, w_pool, pool_scale, rows):
    B, T, _ = u.shape
    ug = u.reshape(B, T, N_POOL_GROUPS, POOL_GROUP)
    diffs = []
    for gi, win in enumerate(POOL_WINDOWS):
        ux = ug[:, :, gi]
        if rows is None:
            mean = box_mean(ux, win, 1)
        else:
            grid = ux.reshape(B, rows, GRID_W, POOL_GROUP)
            mean = box_mean(box_mean(grid, win, 1), win, 2).reshape(B, T, POOL_GROUP)
        diffs.append(mean.astype(u.dtype) - ux)
    d = jnp.stack(diffs, axis=2)
    y = jnp.einsum("btgc,gcd->btgd", d, w_pool).reshape(B, T, POOL_WIDTH)
    return y * pool_scale


def dwconv(x, w):
    K, C = w.shape
    return lax.conv_general_dilated(x, w[:, None, :].astype(x.dtype), (1,), [(K // 2, K // 2)],
                                    dimension_numbers=("NWC", "WIO", "NWC"), feature_group_count=C)


def split_heads(a):
    B, T, _ = a.shape
    return a.reshape(B, T, MLSTM_HEADS, MLSTM_HEAD_DIM).transpose(0, 2, 1, 3)


def mlstm_q(u_q, w_conv_q):
    return split_heads(jax.nn.silu(dwconv(u_q, w_conv_q)) * (MLSTM_HEAD_DIM ** -0.5))


def mlstm_kv_gates(p, w_conv_k, gate_bias):
    B, T, _ = p.shape
    k = jax.nn.silu(dwconv(p[..., :MLSTM_WIDTH], w_conv_k))
    v = p[..., MLSTM_WIDTH:2 * MLSTM_WIDTH]
    gates = p[..., 2 * MLSTM_WIDTH:].astype(jnp.float32) + gate_bias.reshape(-1).astype(jnp.float32)
    gates = gates.reshape(B, T, 4, MLSTM_HEADS).transpose(2, 0, 3, 1)
    return split_heads(k), split_heads(v), gates


def zero_state(B):
    H, Dh = MLSTM_HEADS, MLSTM_HEAD_DIM
    return (jnp.zeros((B, H, Dh, Dh), jnp.float32), jnp.zeros((B, H, Dh), jnp.float32),
            jnp.zeros((B, H), jnp.float32))


def mlstm_scan(q, k, v, i_pre, f_pre, state):
    need_out = q is not None
    B, H, T, Dh = k.shape
    nc = T // CHUNK

    def chunks(a):
        a = a.reshape((B, H, nc, CHUNK) + a.shape[3:])
        return jnp.moveaxis(a, 2, 0)

    logf = jax.nn.log_sigmoid(f_pre)
    xs = (chunks(k.astype(jnp.float32)), chunks(v.astype(jnp.float32)), chunks(i_pre), chunks(logf))
    if need_out:
        xs = xs + (chunks(q.astype(jnp.float32)),)
    tril = jnp.tril(jnp.ones((CHUNK, CHUNK), dtype=bool))

    def step(carry, inp):
        C, n, m = carry
        kc, vc, ic, lfc = inp[:4]
        b = jnp.cumsum(lfc, axis=-1)
        g = b[..., -1]
        a = g[..., None] - b + ic
        m_new = jnp.maximum(g + m, jnp.max(a, axis=-1))
        decay = jnp.exp(g + m - m_new)
        wk = jnp.exp(a - m_new[..., None])
        C_new = decay[..., None, None] * C + jnp.einsum("bhl,bhlv,bhlk->bhvk", wk, vc, kc)
        n_new = decay[..., None] * n + jnp.einsum("bhl,bhlk->bhk", wk, kc)
        if not need_out:
            return (C_new, n_new, m_new), None
        qc = inp[4]
        dmat = b[..., :, None] - b[..., None, :] + ic[..., None, :]
        dmat = jnp.where(tril, dmat, -jnp.inf)
        inter = b + m[..., None]
        m_t = jnp.maximum(inter, jnp.max(dmat, axis=-1))
        w_inter = jnp.exp(inter - m_t)
        wts = jnp.exp(dmat - m_t[..., None]) * jnp.einsum("bhtd,bhsd->bhts", qc, kc)
        num = (w_inter[..., None] * jnp.einsum("bhvk,bhtk->bhtv", C, qc)
               + jnp.einsum("bhts,bhsv->bhtv", wts, vc))
        den = w_inter * jnp.einsum("bhk,bhtk->bht", n, qc) + jnp.sum(wts, axis=-1)
        h = num / jnp.maximum(jnp.abs(den), jnp.exp(-m_t))[..., None]
        return (C_new, n_new, m_new), h

    state, hs = lax.scan(step, state, xs)
    if not need_out:
        return state, None
    return state, jnp.moveaxis(hs, 0, 2).reshape(B, H, T, Dh)


def bidir_mlstm(q, k, v, gates, init_f, init_b):
    rev = lambda a: jnp.flip(a, axis=2)
    st_f, h_f = mlstm_scan(q, k, v, gates[0], gates[1], init_f)
    st_b, h_b = mlstm_scan(None if q is None else rev(q), rev(k), rev(v), rev(gates[2]), rev(gates[3]), init_b)
    h = None if q is None else h_f + rev(h_b)
    return st_f, st_b, h


def mlstm_readout(h, o_pre, head_g):
    B, H, T, Dh = h.shape
    mu = jnp.mean(h, axis=-1, keepdims=True)
    var = jnp.mean(jnp.square(h - mu), axis=-1, keepdims=True)
    hn = ((h - mu) * lax.rsqrt(var + EPS)).transpose(0, 2, 1, 3).reshape(B, T, H * Dh)
    return (hn * head_g).astype(o_pre.dtype) * jax.nn.sigmoid(o_pre)


def mix_out(u, h_m, w_pool, pool_scale, head_g, w_out, rows):
    p = pool_mix(u[..., :POOL_WIDTH], w_pool, pool_scale, rows)
    m = mlstm_readout(h_m, u[..., O0:K0], head_g)
    return jnp.concatenate([p, m], axis=-1) @ w_out


def hier_moe(xn, w_group, b_group, w_router, b_router, w1, w3, w2):
    B, T, D = xn.shape
    xt = xn.reshape(B * T, D)
    grp_logits = (xt @ w_group).astype(jnp.float32) + b_group
    p_grp = jax.nn.softmax(grp_logits, axis=-1)
    _, g_sel = lax.top_k(grp_logits, 1)
    exp_logits = ((xt @ w_router).astype(jnp.float32) + b_router).reshape(-1, N_GROUPS, EXPERTS_PER_GROUP)
    sel_logits = jnp.take_along_axis(exp_logits, g_sel[:, :, None], axis=1)[:, 0]
    p_exp = jax.nn.softmax(sel_logits, axis=-1)
    top_p, top_i = lax.top_k(p_exp, TOP_K)
    top_p = top_p / jnp.sum(top_p, axis=-1, keepdims=True)
    w_tok = jnp.take_along_axis(p_grp, g_sel, axis=1) * top_p
    eid = g_sel * EXPERTS_PER_GROUP + top_i
    combine = jnp.einsum("nk,nke->ne", w_tok, jax.nn.one_hot(eid, N_EXPERTS, dtype=jnp.float32)).astype(xt.dtype)
    y = jnp.zeros_like(xt)
    for e in range(N_EXPERTS):
        he = jax.nn.silu(xt @ w1[e]) * (xt @ w3[e])
        y = y + combine[:, e:e + 1] * (he @ w2[e])
    return y.reshape(B, T, D)


def setup_inputs(seed: int = 0) -> dict:
    key = jax.random.key(seed)
    ks = jax.random.split(key, 24)
    nrm = lambda k, s: jax.random.normal(k, s, jnp.float32)
    D = D_MODEL
    fb = jnp.linspace(3.0, 6.0, MLSTM_HEADS)
    zb = jnp.zeros((MLSTM_HEADS,), jnp.float32)
    gate_base = jnp.stack([zb, fb, zb, fb])
    return {
        "x": nrm(ks[0], (BATCH, SEQ, D)),
        "c": nrm(ks[1], (BATCH, D)),
        "ctx": nrm(ks[2], (BATCH, CTX_LEN, D)),
        "c_ctx": nrm(ks[3], (D,)),
        "w_mod": nrm(ks[4], (DEPTH, D, 6 * D)) * (0.5 * D ** -0.5),
        "b_mod": 0.02 * nrm(ks[5], (DEPTH, 6 * D)),
        "norm1_g": 1.0 + 0.02 * nrm(ks[6], (DEPTH, D)),
        "w_in": nrm(ks[7], (DEPTH, D, IN_COLS)) * D ** -0.5,
        "w_conv_q": nrm(ks[8], (DEPTH, QK_CONV, MLSTM_WIDTH)) * QK_CONV ** -0.5,
        "w_conv_k": nrm(ks[9], (DEPTH, QK_CONV, MLSTM_WIDTH)) * QK_CONV ** -0.5,
        "gate_bias": gate_base + 0.1 * nrm(ks[10], (DEPTH, 4, MLSTM_HEADS)),
        "head_norm_g": 1.0 + 0.02 * nrm(ks[11], (DEPTH, MLSTM_WIDTH)),
        "w_pool": nrm(ks[12], (DEPTH, N_POOL_GROUPS, POOL_GROUP, POOL_GROUP)) * POOL_GROUP ** -0.5,
        "pool_scale": 1.0 + 0.02 * nrm(ks[13], (DEPTH, POOL_WIDTH)),
        "w_out": nrm(ks[14], (DEPTH, MIX_WIDTH, D)) * MIX_WIDTH ** -0.5,
        "norm2_g": 1.0 + 0.02 * nrm(ks[15], (DEPTH, D)),
        "w_group": nrm(ks[16], (DEPTH, D, N_GROUPS)) * D ** -0.5,
        "b_group": 0.01 * nrm(ks[17], (DEPTH, N_GROUPS)),
        "w_router": nrm(ks[18], (DEPTH, D, N_EXPERTS)) * D ** -0.5,
        "b_router": 0.01 * nrm(ks[19], (DEPTH, N_EXPERTS)),
        "w1": nrm(ks[20], (DEPTH, N_EXPERTS, D, D_EXPERT)) * D ** -0.5,
        "w3": nrm(ks[21], (DEPTH, N_EXPERTS, D, D_EXPERT)) * D ** -0.5,
        "w2": nrm(ks[22], (DEPTH, N_EXPERTS, D_EXPERT, D)) * D_EXPERT ** -0.5,
        "final_g": 1.0 + 0.02 * nrm(ks[23], (D,)),
    }


def reference(x, c, ctx, c_ctx, w_mod, b_mod, norm1_g, w_in, w_conv_q, w_conv_k, gate_bias,
              head_norm_g, w_pool, pool_scale, w_out, norm2_g, w_group, b_group, w_router,
              b_router, w1, w3, w2, final_g):
    B = x.shape[0]
    rows = x.shape[1] // GRID_W
    D = D_MODEL
    h_lat, h_ctx = x, ctx
    for layer in range(DEPTH):
        last = layer == DEPTH - 1
        sh1, sc1, g1, sh2, sc2, g2 = jnp.split(adaln(c, w_mod[layer], b_mod[layer])[:, None, :], 6, axis=-1)
        n_ctx_mod = 2 if last else 6
        mod_ctx = jnp.split(adaln(c_ctx, w_mod[layer][:, :n_ctx_mod * D], b_mod[layer][:n_ctx_mod * D]),
                            n_ctx_mod, axis=-1)

        cn = modulate(rmsnorm(h_ctx, norm1_g[layer]), mod_ctx[0], mod_ctx[1])
        if last:
            uc_state = cn @ w_in[layer][:, K0:]
            q_c = None
        else:
            uc = cn @ w_in[layer]
            uc_state = uc[..., K0:]
            q_c = mlstm_q(uc[..., Q0:O0], w_conv_q[layer])
        k_c, v_c, gt_c = mlstm_kv_gates(uc_state, w_conv_k[layer], gate_bias[layer])
        st_f, st_b, hm_c = bidir_mlstm(q_c, k_c, v_c, gt_c, zero_state(B), zero_state(B))

        xn = modulate(rmsnorm(h_lat, norm1_g[layer]), sh1, sc1)
        u = xn @ w_in[layer]
        q_l = mlstm_q(u[..., Q0:O0], w_conv_q[layer])
        k_l, v_l, gt_l = mlstm_kv_gates(u[..., K0:], w_conv_k[layer], gate_bias[layer])
        _, _, hm_l = bidir_mlstm(q_l, k_l, v_l, gt_l, st_f, st_b)
        mix_l = mix_out(u, hm_l, w_pool[layer], pool_scale[layer], head_norm_g[layer], w_out[layer], rows)
        h_lat = h_lat + g1 * mix_l
        fn = modulate(rmsnorm(h_lat, norm2_g[layer]), sh2, sc2)
        h_lat = h_lat + g2 * hier_moe(fn, w_group[layer], b_group[layer], w_router[layer],
                                      b_router[layer], w1[layer], w3[layer], w2[layer])

        if not last:
            c_sh1, c_sc1, c_g1, c_sh2, c_sc2, c_g2 = mod_ctx
            mix_c = mix_out(uc, hm_c, w_pool[layer], pool_scale[layer], head_norm_g[layer], w_out[layer], None)
            h_ctx = h_ctx + c_g1 * mix_c
            fc = modulate(rmsnorm(h_ctx, norm2_g[layer]), c_sh2, c_sc2)
            h_ctx = h_ctx + c_g2 * hier_moe(fc, w_group[layer], b_group[layer], w_router[layer],
                                            b_router[layer], w1[layer], w3[layer], w2[layer])
    return rmsnorm(h_lat, final_g)
```

```python
import functools

import jax
import jax.numpy as jnp
from jax import lax
from jax.experimental import pallas as pl
from jax.experimental.pallas import tpu as pltpu

F32 = jnp.float32
BF16 = jnp.bfloat16

D_MODEL = 2048
GRID_W = 64
GRID_SHIFT = 6
POOL_WINDOWS = (2, 4, 8, 16)
POOL_GROUP = 256
HEADS = 4
HEAD_DIM = 256
MIX_HALF = 1024
N_GATES = 16
N_GROUPS = 4
EXPERTS_PER_GROUP = 8
N_EXPERTS = 32
D_EXPERT = 512
EPS = 1e-6
LANES = 128
CHUNK = 256
ROUTE_LANE0 = N_GROUPS

VMEM_LIMIT = 56 * 1024 * 1024


def _cparams(sem, vmem=VMEM_LIMIT):
    return pltpu.CompilerParams(dimension_semantics=sem, vmem_limit_bytes=vmem)


def _split2(x):
    hi = x.astype(BF16)
    lo = (x - hi.astype(F32)).astype(BF16)
    return hi, lo


def _split3(x):
    hi = x.astype(BF16)
    r = x - hi.astype(F32)
    mid = r.astype(BF16)
    lo = (r - mid.astype(F32)).astype(BF16)
    return hi, mid, lo


def _dot(a, b):
    return jnp.dot(a, b, preferred_element_type=F32)


def _silu(x):
    return x * jax.nn.sigmoid(x)


def _log_sigmoid(x):
    return jnp.minimum(x, 0.0) - jnp.log(1.0 + jnp.exp(-jnp.abs(x)))


def _adaln_kernel(c_ref, w_ref, b_ref, o_ref):
    a = _silu(c_ref[...])
    a3 = jnp.concatenate(_split3(a), axis=0)
    w_hi, w_lo = _split2(w_ref[...])
    acc = _dot(a3, w_hi)
    acc_lo = _dot(a3[:32], w_lo)
    out = acc[0:16] + acc[16:32] + acc[32:48] + acc_lo[0:16] + acc_lo[16:32]
    o_ref[...] = out + b_ref[...]


def _adaln(c16, w_mod, b_mod):
    n = w_mod.shape[1]
    tn = 1536
    return pl.pallas_call(
        _adaln_kernel,
        grid=(n // tn,),
        in_specs=[pl.BlockSpec((16, D_MODEL), lambda j: (0, 0)),
                  pl.BlockSpec((D_MODEL, tn), lambda j: (0, j)),
                  pl.BlockSpec((1, tn), lambda j: (0, j))],
        out_specs=pl.BlockSpec((16, tn), lambda j: (0, j)),
        out_shape=jax.ShapeDtypeStruct((16, n), F32),
        compiler_params=_cparams(("arbitrary",)),
        name="adaln",
    )(c16, w_mod, b_mod)


def _inproj_kernel(x_ref, mod_ref, g_ref, w_ref, wgh_ref, wgl_ref, gb_ref,
                   pool_o, q_o, o_o, k_o, v_o, gc_o, gr_o):
    x = x_ref[...]
    ms = jnp.mean(x * x, axis=-1, keepdims=True)
    y = x * lax.rsqrt(ms + EPS) * g_ref[...]
    xn = y * (1.0 + mod_ref[1:2, :]) + mod_ref[0:1, :]
    xh, xl = _split2(xn)
    for ci, o in enumerate((pool_o, q_o, o_o, k_o, v_o)):
        o[...] = _dot(xh, w_ref[:, ci * MIX_HALF:(ci + 1) * MIX_HALF]).astype(o.dtype)
    wgh = wgh_ref[...]
    gates = _dot(xh, wgh) + _dot(xh, wgl_ref[...]) + _dot(xl, wgh) + gb_ref[...]
    gc_o[...] = gates[:, :N_GATES]
    gr_o[...] = gates.T[:N_GATES, :]


def _inproj(x2d, mod, g, w_main, wg_hi, wg_lo, gate_bias_row, tm):
    t = x2d.shape[0]
    nt = t // tm
    const = lambda i: (0, 0)
    row = lambda i: (i, 0)
    f32_out = jax.ShapeDtypeStruct((t, MIX_HALF), F32)
    return pl.pallas_call(
        _inproj_kernel,
        grid=(nt,),
        in_specs=[pl.BlockSpec((tm, D_MODEL), row),
                  pl.BlockSpec((2, D_MODEL), const),
                  pl.BlockSpec((1, D_MODEL), const),
                  pl.BlockSpec(w_main.shape, const, pipeline_mode=pl.Buffered(1)),
                  pl.BlockSpec((D_MODEL, LANES), const),
                  pl.BlockSpec((D_MODEL, LANES), const),
                  pl.BlockSpec((1, LANES), const)],
        out_specs=[pl.BlockSpec((tm, MIX_HALF), row)] * 5
        + [pl.BlockSpec((tm, N_GATES), row), pl.BlockSpec((N_GATES, tm), lambda i: (0, i))],
        out_shape=[f32_out, f32_out, f32_out, f32_out,
                   jax.ShapeDtypeStruct((t, MIX_HALF), BF16),
                   jax.ShapeDtypeStruct((t, N_GATES), F32),
                   jax.ShapeDtypeStruct((N_GATES, t), F32)],
        compiler_params=_cparams(("arbitrary",)),
        name="inproj",
    )(x2d, mod, g, w_main, wg_hi, wg_lo, gate_bias_row)


def _conv_kernel(q_ref, qp_ref, qn_ref, k_ref, kp_ref, kn_ref, wq_ref, wk_ref, qo, ko, *, nt):
    i = pl.program_id(0)
    has_prev = (i > 0).astype(F32)
    has_next = (i < nt - 1).astype(F32)

    def conv(x_ref, p_ref, n_ref, w_ref):
        x = x_ref[...]
        tm = x.shape[0]
        row = lax.broadcasted_iota(jnp.int32, x.shape, 0)
        prev_row = p_ref[7:8, :] * has_prev
        next_row = n_ref[0:1, :] * has_next
        xp = jnp.where(row == 0, prev_row, pltpu.roll(x, 1, 0))
        xn = jnp.where(row == tm - 1, next_row, pltpu.roll(x, tm - 1, 0))
        y = w_ref[0:1, :] * xp + w_ref[1:2, :] * x + w_ref[2:3, :] * xn
        return _silu(y)

    qo[...] = (conv(q_ref, qp_ref, qn_ref, wq_ref) * (HEAD_DIM ** -0.5)).astype(BF16)
    ko[...] = conv(k_ref, kp_ref, kn_ref, wk_ref).astype(BF16)


def _conv_qk(uq, uk, wq, wk, tm):
    t = uq.shape[0]
    nt = t // tm
    r8 = tm // 8
    last8 = t // 8 - 1
    main = pl.BlockSpec((tm, MIX_HALF), lambda i: (i, 0))
    prev = pl.BlockSpec((8, MIX_HALF), lambda i: (jnp.maximum(i * r8 - 1, 0), 0))
    nxt = pl.BlockSpec((8, MIX_HALF), lambda i: (jnp.minimum((i + 1) * r8, last8), 0))
    wspec = pl.BlockSpec((3, MIX_HALF), lambda i: (0, 0))
    out = jax.ShapeDtypeStruct((t, MIX_HALF), BF16)
    return pl.pallas_call(
        functools.partial(_conv_kernel, nt=nt),
        grid=(nt,),
        in_specs=[main, prev, nxt, main, prev, nxt, wspec, wspec],
        out_specs=[main, main],
        out_shape=[out, out],
        compiler_params=_cparams(("arbitrary",)),
        name="conv_qk",
    )(uq, uq, uq, uk, uk, uk, wq, wk)


def _mlstm_kernel(*refs, need_out):
    if need_out:
        (qf, kf, vf, gcf, grf, qb, kb, vb, gcb, grb, s0, n0, m0,
         hf_o, hb_o, s_s, n_s, m_s) = refs
        q_refs, h_outs = (qf, qb), (hf_o, hb_o)
    else:
        (kf, vf, gcf, grf, kb, vb, gcb, grb, s0, n0, m0,
         s_o, n_o, m_o, s_s, n_s, m_s) = refs
    k_refs, v_refs, gc_refs, gr_refs = (kf, kb), (vf, vb), (gcf, gcb), (grf, grb)
    j = pl.program_id(0)

    @pl.when(j == 0)
    def _():
        s_s[...] = s0[...]
        n_s[...] = n0[...]
        m_s[...] = m0[...]

    L = CHUNK
    row = lax.broadcasted_iota(jnp.int32, (L, L), 0)
    col = lax.broadcasted_iota(jnp.int32, (L, L), 1)
    tril = row >= col
    triu = row <= col
    tril_bf = jnp.where(tril, 1.0, 0.0).astype(BF16)
    triu_bf = jnp.where(triu, 1.0, 0.0).astype(BF16)

    for d in range(2):
        m_col, m_row, mask = (tril_bf, triu_bf, tril) if d == 0 else (triu_bf, tril_bf, triu)
        gc = gc_refs[d][...]
        gr = gr_refs[d][...]
        lf_c = _log_sigmoid(gc)
        lf_r = _log_sigmoid(gr)
        c_hi, c_mid, c_lo = _split3(lf_c)
        b_c = _dot(m_col, c_hi) + _dot(m_col, c_mid) + _dot(m_col, c_lo)
        r_hi, r_mid, r_lo = _split3(lf_r)
        b_r = _dot(r_hi, m_row) + _dot(r_mid, m_row) + _dot(r_lo, m_row)
        for h in range(HEADS):
            hd = d * HEADS + h
            ci, cf = 8 * d + h, 8 * d + 4 + h
            hs = slice(h * HEAD_DIM, (h + 1) * HEAD_DIM)
            i_c, bc = gc[:, ci:ci + 1], b_c[:, cf:cf + 1]
            i_r, br = gr[ci:ci + 1, :], b_r[cf:cf + 1, :]
            g = jnp.sum(lf_r[cf:cf + 1, :], axis=1, keepdims=True)
            m_old = m_s[hd][:, 0:1]
            s_old = s_s[hd]
            n_old = n_s[hd]
            a_r = g - br + i_r
            a_c = g - bc + i_c
            m_new = jnp.maximum(g + m_old, jnp.max(a_r, axis=1, keepdims=True))
            decay = jnp.exp(g + m_old - m_new)
            wk_c = jnp.exp(a_c - m_new)
            k_h = k_refs[d][:, hs]
            v_h = v_refs[d][:, hs]
            kw = k_h.astype(F32) * wk_c
            n_new = decay * n_old + jnp.sum(kw, axis=0, keepdims=True)
            s_new = decay * s_old + _dot(kw.T.astype(BF16), v_h)
            if need_out:
                q_h = q_refs[d][:, hs]
                dm = jnp.where(mask, bc - br + i_r, -jnp.inf)
                inter = bc + m_old
                m_t = jnp.maximum(inter, jnp.max(dm, axis=1, keepdims=True))
                w_inter = jnp.exp(inter - m_t)
                qk = lax.dot_general(q_h, k_h, (((1,), (1,)), ((), ())),
                                     preferred_element_type=F32)
                p = jnp.exp(dm - m_t) * qk
                num = w_inter * _dot(q_h, s_old.astype(BF16)) + _dot(p.astype(BF16), v_h)
                den = (w_inter * jnp.sum(q_h.astype(F32) * n_old, axis=1, keepdims=True)
                       + jnp.sum(p, axis=1, keepdims=True))
                h_outs[d][:, hs] = num / jnp.maximum(jnp.abs(den), jnp.exp(-m_t))
            s_s[hd] = s_new
            n_s[hd] = n_new
            m_s[hd] = jnp.broadcast_to(m_new, (1, LANES))

    if not need_out:
        s_o[...] = s_s[...]
        n_o[...] = n_s[...]
        m_o[...] = m_s[...]


def _mlstm(q, k, v, gc, gr, s0, n0, m0, need_out):
    t = k.shape[0]
    nc = t // CHUNK
    fwd = lambda j: (j, 0)
    bwd = lambda j: (nc - 1 - j, 0)
    fwd_t = lambda j: (0, j)
    bwd_t = lambda j: (0, nc - 1 - j)
    c3 = lambda j: (0, 0, 0)
    seq = lambda im: pl.BlockSpec((CHUNK, MIX_HALF), im)
    state_specs = [pl.BlockSpec(s0.shape, c3), pl.BlockSpec(n0.shape, c3), pl.BlockSpec(m0.shape, c3)]
    scratch = [pltpu.VMEM(s0.shape, F32), pltpu.VMEM(n0.shape, F32), pltpu.VMEM(m0.shape, F32)]

    def side(im, im_t):
        specs = ([seq(im)] if need_out else []) + [seq(im), seq(im)]
        return specs + [pl.BlockSpec((CHUNK, N_GATES), im), pl.BlockSpec((N_GATES, CHUNK), im_t)]

    in_specs = side(fwd, fwd_t) + side(bwd, bwd_t) + state_specs
    seq_in = ((q,) if need_out else ()) + (k, v, gc, gr)
    args = seq_in + seq_in + (s0, n0, m0)
    if need_out:
        out_specs = [seq(fwd), seq(bwd)]
        out_shape = [jax.ShapeDtypeStruct((t, MIX_HALF), F32)] * 2
    else:
        out_specs = state_specs
        out_shape = [jax.ShapeDtypeStruct(a.shape, F32) for a in (s0, n0, m0)]
    return pl.pallas_call(
        functools.partial(_mlstm_kernel, need_out=need_out),
        grid=(nc,),
        in_specs=in_specs,
        out_specs=out_specs,
        out_shape=out_shape,
        scratch_shapes=scratch,
        compiler_params=_cparams(("arbitrary",)),
        name="mlstm_out" if need_out else "mlstm_state",
    )(*args)


POOL_PAD = 512


def _pool_kernel(u_ref, w_ref, sc_ref, o_ref, pad_s, *, win, t):
    half = win // 2
    tile = 256
    zeros = jnp.zeros((POOL_PAD, POOL_GROUP), F32)
    pad_s[0:POOL_PAD, :] = zeros
    pad_s[POOL_PAD + t:POOL_PAD + t + POOL_PAD, :] = zeros

    def copy(r, carry):
        t0 = pl.multiple_of(r * tile, tile)
        pad_s[pl.ds(POOL_PAD + t0, tile), :] = u_ref[pl.ds(t0, tile), :]
        return carry

    lax.fori_loop(0, t // tile, copy, 0)

    row = lax.broadcasted_iota(jnp.int32, (tile, tile), 0)
    col = lax.broadcasted_iota(jnp.int32, (tile, tile), 1)
    same_row = (row >> GRID_SHIFT) == (col >> GRID_SHIFT)
    in_win = (col - row >= -half) & (col - row < half)
    band = jnp.where(same_row & in_win, 1.0, 0.0).astype(BF16)
    w = w_ref[...].astype(BF16)
    scale = sc_ref[...]
    n_rows = t // GRID_W

    def body(r, carry):
        t0 = pl.multiple_of(r * tile, tile)
        acc = jnp.zeros((tile, POOL_GROUP), F32)
        for dd in range(-half, half):
            acc = acc + pad_s[pl.ds(POOL_PAD + t0 + GRID_W * dd, tile), :]
        tok = t0 + lax.broadcasted_iota(jnp.int32, (tile, POOL_GROUP), 0)
        gr = tok >> GRID_SHIFT
        gc = tok & (GRID_W - 1)
        cnt_v = jnp.minimum(gr + half, n_rows) - jnp.maximum(gr - half, 0)
        cnt_h = jnp.minimum(gc + half, GRID_W) - jnp.maximum(gc - half, 0)
        mean_v = acc / cnt_v.astype(F32)
        hi, lo = _split2(mean_v)
        mean = (_dot(band, hi) + _dot(band, lo)) / cnt_h.astype(F32)
        x = pad_s[pl.ds(POOL_PAD + t0, tile), :]
        y = _dot((mean - x).astype(BF16), w) * scale
        o_ref[pl.ds(t0, tile), :] = y.astype(BF16)
        return carry

    lax.fori_loop(0, t // tile, body, 0)


def _pool(u_pool, w_pool_g, scale_row, gi, win):
    t = u_pool.shape[0]
    return pl.pallas_call(
        functools.partial(_pool_kernel, win=win, t=t),
        grid=(1,),
        in_specs=[pl.BlockSpec((t, POOL_GROUP), lambda i: (0, gi)),
                  pl.BlockSpec((POOL_GROUP, POOL_GROUP), lambda i: (0, 0)),
                  pl.BlockSpec((1, POOL_GROUP), lambda i: (0, gi))],
        out_specs=pl.BlockSpec((t, POOL_GROUP), lambda i: (0, 0)),
        out_shape=jax.ShapeDtypeStruct((t, POOL_GROUP), BF16),
        scratch_shapes=[pltpu.VMEM((t + 2 * POOL_PAD, POOL_GROUP), F32)],
        compiler_params=_cparams(("arbitrary",)),
        name=f"pool_w{win}",
    )(u_pool, w_pool_g, scale_row)


def _route(logits):
    lane = lax.broadcasted_iota(jnp.int32, logits.shape, 1).astype(F32)
    neg = -jnp.inf
    big = float(LANES)
    gl = jnp.where(lane < N_GROUPS, logits, neg)
    gmax = jnp.max(gl, axis=1, keepdims=True)
    gsel = jnp.min(jnp.where(gl == gmax, lane, big), axis=1, keepdims=True)
    p_grp = 1.0 / jnp.sum(jnp.exp(gl - gmax), axis=1, keepdims=True)
    lo = ROUTE_LANE0 + EXPERTS_PER_GROUP * gsel
    el = jnp.where((lane >= lo) & (lane < lo + EXPERTS_PER_GROUP), logits, neg)
    m1 = jnp.max(el, axis=1, keepdims=True)
    i1 = jnp.min(jnp.where(el == m1, lane, big), axis=1, keepdims=True)
    el2 = jnp.where(lane == i1, neg, el)
    m2 = jnp.max(el2, axis=1, keepdims=True)
    i2 = jnp.min(jnp.where(el2 == m2, lane, big), axis=1, keepdims=True)
    e2 = jnp.exp(m2 - m1)
    p1 = 1.0 / (1.0 + e2)
    p2 = e2 / (1.0 + e2)
    return jnp.where(lane == i1, p_grp * p1, 0.0) + jnp.where(lane == i2, p_grp * p2, 0.0)


def _outproj_kernel(p0, p1, p2, p3, hf_ref, hb_ref, uo_ref, x_ref, wout_ref, mod_ref,
                    hg_ref, n2g_ref, wrh_ref, wrl_ref, br_ref, h1_o, fn_o, comb_o):
    h = hf_ref[...] + hb_ref[...]
    parts = []
    for hh in range(HEADS):
        hs = h[:, hh * HEAD_DIM:(hh + 1) * HEAD_DIM]
        mu = jnp.mean(hs, axis=-1, keepdims=True)
        ctr = hs - mu
        var = jnp.mean(ctr * ctr, axis=-1, keepdims=True)
        parts.append(ctr * lax.rsqrt(var + EPS))
    hn = jnp.concatenate(parts, axis=1) * hg_ref[...]
    m = (hn * jax.nn.sigmoid(uo_ref[...])).astype(BF16)
    mix = _dot(m, wout_ref[MIX_HALF:, :])
    for gi, p_ref in enumerate((p0, p1, p2, p3)):
        mix = mix + _dot(p_ref[...], wout_ref[gi * POOL_GROUP:(gi + 1) * POOL_GROUP, :])
    h1 = x_ref[...] + mod_ref[2:3, :] * mix
    h1_o[...] = h1
    ms = jnp.mean(h1 * h1, axis=-1, keepdims=True)
    fn = h1 * lax.rsqrt(ms + EPS) * n2g_ref[...]
    fn = fn * (1.0 + mod_ref[4:5, :]) + mod_ref[3:4, :]
    fh, fl = _split2(fn)
    fn_o[...] = fh
    wrh = wrh_ref[...]
    logits = _dot(fh, wrh) + _dot(fh, wrl_ref[...]) + _dot(fl, wrh) + br_ref[...]
    comb_o[...] = _route(logits)


def _outproj(ps, hf, hb, uo, x2d, w_out, mod, head_g, norm2_g, wr_hi, wr_lo, b_route, tm):
    t = x2d.shape[0]
    const = lambda i: (0, 0)
    row = lambda i: (i, 0)
    in_specs = ([pl.BlockSpec((tm, POOL_GROUP), row)] * 4
                + [pl.BlockSpec((tm, MIX_HALF), row)] * 3
                + [pl.BlockSpec((tm, D_MODEL), row),
                   pl.BlockSpec(w_out.shape, const, pipeline_mode=pl.Buffered(1)),
                   pl.BlockSpec(mod.shape, const),
                   pl.BlockSpec((1, MIX_HALF), const),
                   pl.BlockSpec((1, D_MODEL), const),
                   pl.BlockSpec((D_MODEL, LANES), const),
                   pl.BlockSpec((D_MODEL, LANES), const),
                   pl.BlockSpec((1, LANES), const)])
    return pl.pallas_call(
        _outproj_kernel,
        grid=(t // tm,),
        in_specs=in_specs,
        out_specs=[pl.BlockSpec((tm, D_MODEL), row), pl.BlockSpec((tm, D_MODEL), row),
                   pl.BlockSpec((tm, LANES), row)],
        out_shape=[jax.ShapeDtypeStruct((t, D_MODEL), F32),
                   jax.ShapeDtypeStruct((t, D_MODEL), BF16),
                   jax.ShapeDtypeStruct((t, LANES), F32)],
        compiler_params=_cparams(("arbitrary",)),
        name="outproj_route",
    )(*ps, hf, hb, uo, x2d, w_out, mod, head_g, norm2_g, wr_hi, wr_lo, b_route)


def _moe_dense_kernel(x_ref, comb_ref, w1_ref, w3_ref, w2_ref, o_ref, acc):
    e = pl.program_id(1)

    @pl.when(e == 0)
    def _():
        acc[...] = jnp.zeros_like(acc)

    x = x_ref[...]
    a = _dot(x, w1_ref[...].astype(BF16))
    b = _dot(x, w3_ref[...].astype(BF16))
    y = _dot((_silu(a) * b).astype(BF16), w2_ref[...].astype(BF16))
    comb = comb_ref[...]
    lane = lax.broadcasted_iota(jnp.int32, comb.shape, 1)
    c = jnp.sum(jnp.where(lane == e + ROUTE_LANE0, comb, 0.0), axis=1, keepdims=True)
    acc[...] += c * y

    @pl.when(e == N_EXPERTS - 1)
    def _():
        o_ref[...] = acc[...]


def _moe_dense(fn, comb, w1, w3, w2, tm):
    t = fn.shape[0]
    return pl.pallas_call(
        _moe_dense_kernel,
        grid=(t // tm, N_EXPERTS),
        in_specs=[pl.BlockSpec((tm, D_MODEL), lambda i, e: (i, 0)),
                  pl.BlockSpec((tm, LANES), lambda i, e: (i, 0)),
                  pl.BlockSpec((None, D_MODEL, D_EXPERT), lambda i, e: (e, 0, 0)),
                  pl.BlockSpec((None, D_MODEL, D_EXPERT), lambda i, e: (e, 0, 0)),
                  pl.BlockSpec((None, D_EXPERT, D_MODEL), lambda i, e: (e, 0, 0))],
        out_specs=pl.BlockSpec((tm, D_MODEL), lambda i, e: (i, 0)),
        out_shape=jax.ShapeDtypeStruct((t, D_MODEL), F32),
        scratch_shapes=[pltpu.VMEM((tm, D_MODEL), F32)],
        compiler_params=_cparams(("arbitrary", "arbitrary")),
        name="moe_dense",
    )(fn, comb, w1, w3, w2)


def _final_kernel(h1_ref, y_ref, g2_ref, fg_ref, o_ref):
    h = h1_ref[...] + g2_ref[...] * y_ref[...]
    ms = jnp.mean(h * h, axis=-1, keepdims=True)
    o_ref[...] = h * lax.rsqrt(ms + EPS) * fg_ref[...]


def _final(h1, y, g2, final_g, tm):
    t = h1.shape[0]
    row = lambda i: (i, 0)
    const = lambda i: (0, 0)
    return pl.pallas_call(
        _final_kernel,
        grid=(t // tm,),
        in_specs=[pl.BlockSpec((tm, D_MODEL), row), pl.BlockSpec((tm, D_MODEL), row),
                  pl.BlockSpec((1, D_MODEL), const), pl.BlockSpec((1, D_MODEL), const)],
        out_specs=pl.BlockSpec((tm, D_MODEL), row),
        out_shape=jax.ShapeDtypeStruct((t, D_MODEL), F32),
        compiler_params=_cparams(("arbitrary",)),
        name="final_norm",
    )(h1, y, g2, final_g)


def _pad_lanes(a):
    return jnp.pad(a, ((0, 0), (0, LANES - a.shape[1])))


def kernel(x, c, ctx, c_ctx, w_mod, b_mod, norm1_g, w_in, w_conv_q, w_conv_k, gate_bias, head_norm_g, w_pool, pool_scale, w_out, norm2_g, w_group, b_group, w_router, b_router, w1, w3, w2, final_g):
    assert x.shape[0] == 1 and w_mod.shape[0] == 1
    seq = x.shape[1]
    x2d = x[0]
    ctx2d = ctx[0]

    n_main = 5 * MIX_HALF
    w_main = w_in[0][:, :n_main].astype(BF16)
    wg_hi, wg_lo = _split2(_pad_lanes(w_in[0][:, n_main:]))
    gate_bias_row = _pad_lanes(gate_bias[0].reshape(1, N_GATES))
    w_out_bf = w_out[0].astype(BF16)
    wr_hi, wr_lo = _split2(_pad_lanes(jnp.concatenate([w_group[0], w_router[0]], axis=1)))
    b_route = _pad_lanes(jnp.concatenate([b_group[0], b_router[0]]).reshape(1, -1))
    norm1 = norm1_g[0].reshape(1, D_MODEL)

    c16 = jnp.zeros((16, D_MODEL), F32).at[0].set(c[0]).at[1].set(c_ctx)
    mods = _adaln(c16, w_mod[0], b_mod[0].reshape(1, -1))
    mod_lat = mods[0].reshape(6, D_MODEL)
    mod_ctx = mods[1].reshape(6, D_MODEL)

    _, uq_c, _, uk_c, v_c, gc_c, gr_c = _inproj(ctx2d, mod_ctx[0:2], norm1, w_main, wg_hi, wg_lo,
                                                gate_bias_row, tm=256)
    _, k_c = _conv_qk(uq_c, uk_c, w_conv_q[0], w_conv_k[0], tm=256)
    zeros_state = (jnp.zeros((2 * HEADS, HEAD_DIM, HEAD_DIM), F32),
                   jnp.zeros((2 * HEADS, 1, HEAD_DIM), F32),
                   jnp.zeros((2 * HEADS, 1, LANES), F32))
    s0, n0, m0 = _mlstm(None, k_c, v_c, gc_c, gr_c, *zeros_state, need_out=False)

    u_pool, uq, uo, uk, v, gc, gr = _inproj(x2d, mod_lat[0:2], norm1, w_main, wg_hi, wg_lo,
                                            gate_bias_row, tm=256)
    q, k = _conv_qk(uq, uk, w_conv_q[0], w_conv_k[0], tm=512)
    hf, hb = _mlstm(q, k, v, gc, gr, s0, n0, m0, need_out=True)
    ps = [_pool(u_pool, w_pool[0, gi], pool_scale[0].reshape(1, -1), gi, win)
          for gi, win in enumerate(POOL_WINDOWS)]
    h1, fn, comb = _outproj(ps, hf, hb, uo, x2d, w_out_bf, mod_lat, head_norm_g[0].reshape(1, -1),
                            norm2_g[0].reshape(1, -1), wr_hi, wr_lo, b_route, tm=256)
    y = _moe_dense(fn, comb, w1[0], w3[0], w2[0], tm=512)
    out = _final(h1, y, mod_lat[5:6], final_g.reshape(1, -1), tm=512)
    return out.reshape(1, seq, D_MODEL)
```

```python
import functools

import jax
import jax.numpy as jnp
from jax import lax
from jax.experimental import pallas as pl
from jax.experimental.pallas import tpu as pltpu

F32 = jnp.float32
BF16 = jnp.bfloat16

D_MODEL = 2048
GRID_W = 64
GRID_SHIFT = 6
POOL_WINDOWS = (2, 4, 8, 16)
POOL_GROUP = 256
HEADS = 4
HEAD_DIM = 256
MIX_HALF = 1024
N_GATES = 16
N_GROUPS = 4
EXPERTS_PER_GROUP = 8
N_EXPERTS = 32
D_EXPERT = 512
EPS = 1e-6
LANES = 128
CHUNK = 256
ROUTE_LANE0 = N_GROUPS

VMEM_LIMIT = 56 * 1024 * 1024


def _cparams(sem, vmem=VMEM_LIMIT):
    return pltpu.CompilerParams(dimension_semantics=sem, vmem_limit_bytes=vmem)


def _split2(x):
    hi = x.astype(BF16)
    lo = (x - hi.astype(F32)).astype(BF16)
    return hi, lo


def _split3(x):
    hi = x.astype(BF16)
    r = x - hi.astype(F32)
    mid = r.astype(BF16)
    lo = (r - mid.astype(F32)).astype(BF16)
    return hi, mid, lo


def _dot(a, b):
    return jnp.dot(a, b, preferred_element_type=F32)


def _silu(x):
    return x * jax.nn.sigmoid(x)


def _log_sigmoid(x):
    return jnp.minimum(x, 0.0) - jnp.log(1.0 + jnp.exp(-jnp.abs(x)))


def _adaln_kernel(c_ref, w_ref, b_ref, o_ref):
    a = _silu(c_ref[...])
    a3 = jnp.concatenate(_split3(a), axis=0)
    w_hi, w_lo = _split2(w_ref[...])
    acc = _dot(a3, w_hi)
    acc_lo = _dot(a3[:32], w_lo)
    out = acc[0:16] + acc[16:32] + acc[32:48] + acc_lo[0:16] + acc_lo[16:32]
    o_ref[...] = out + b_ref[...]


def _adaln(c16, w_mod, b_mod):
    n = w_mod.shape[1]
    tn = 1536
    return pl.pallas_call(
        _adaln_kernel,
        grid=(n // tn,),
        in_specs=[pl.BlockSpec((16, D_MODEL), lambda j: (0, 0)),
                  pl.BlockSpec((D_MODEL, tn), lambda j: (0, j)),
                  pl.BlockSpec((1, tn), lambda j: (0, j))],
        out_specs=pl.BlockSpec((16, tn), lambda j: (0, j)),
        out_shape=jax.ShapeDtypeStruct((16, n), F32),
        compiler_params=_cparams(("arbitrary",)),
        name="adaln",
    )(c16, w_mod, b_mod)


def _inproj_kernel(x_ref, mod_ref, g_ref, w_ref, wgh_ref, wgl_ref, gb_ref,
                   pool_o, q_o, o_o, k_o, v_o, gc_o, gr_o):
    x = x_ref[...]
    ms = jnp.mean(x * x, axis=-1, keepdims=True)
    y = x * lax.rsqrt(ms + EPS) * g_ref[...]
    xn = y * (1.0 + mod_ref[1:2, :]) + mod_ref[0:1, :]
    xh, xl = _split2(xn)
    for ci, o in enumerate((pool_o, q_o, o_o, k_o, v_o)):
        o[...] = _dot(xh, w_ref[:, ci * MIX_HALF:(ci + 1) * MIX_HALF]).astype(o.dtype)
    wgh = wgh_ref[...]
    gates = _dot(xh, wgh) + _dot(xh, wgl_ref[...]) + _dot(xl, wgh) + gb_ref[...]
    gc_o[...] = gates[:, :N_GATES]
    gr_o[...] = gates.T[:N_GATES, :]


def _inproj(x2d, mod, g, w_main, wg_hi, wg_lo, gate_bias_row, tm):
    t = x2d.shape[0]
    nt = t // tm
    const = lambda i: (0, 0)
    row = lambda i: (i, 0)
    f32_out = jax.ShapeDtypeStruct((t, MIX_HALF), F32)
    return pl.pallas_call(
        _inproj_kernel,
        grid=(nt,),
        in_specs=[pl.BlockSpec((tm, D_MODEL), row),
                  pl.BlockSpec((2, D_MODEL), const),
                  pl.BlockSpec((1, D_MODEL), const),
                  pl.BlockSpec(w_main.shape, const, pipeline_mode=pl.Buffered(1)),
                  pl.BlockSpec((D_MODEL, LANES), const),
                  pl.BlockSpec((D_MODEL, LANES), const),
                  pl.BlockSpec((1, LANES), const)],
        out_specs=[pl.BlockSpec((tm, MIX_HALF), row)] * 5
        + [pl.BlockSpec((tm, N_GATES), row), pl.BlockSpec((N_GATES, tm), lambda i: (0, i))],
        out_shape=[f32_out, f32_out, f32_out, f32_out,
                   jax.ShapeDtypeStruct((t, MIX_HALF), BF16),
                   jax.ShapeDtypeStruct((t, N_GATES), F32),
                   jax.ShapeDtypeStruct((N_GATES, t), F32)],
        compiler_params=_cparams(("arbitrary",)),
        name="inproj",
    )(x2d, mod, g, w_main, wg_hi, wg_lo, gate_bias_row)


def _conv_kernel(q_ref, qp_ref, qn_ref, k_ref, kp_ref, kn_ref, wq_ref, wk_ref, qo, ko, *, nt):
    i = pl.program_id(0)
    has_prev = (i > 0).astype(F32)
    has_next = (i < nt - 1).astype(F32)

    def conv(x_ref, p_ref, n_ref, w_ref):
        x = x_ref[...]
        tm = x.shape[0]
        row = lax.broadcasted_iota(jnp.int32, x.shape, 0)
        prev_row = p_ref[7:8, :] * has_prev
        next_row = n_ref[0:1, :] * has_next
        xp = jnp.where(row == 0, prev_row, pltpu.roll(x, 1, 0))
        xn = jnp.where(row == tm - 1, next_row, pltpu.roll(x, tm - 1, 0))
        y = w_ref[0:1, :] * xp + w_ref[1:2, :] * x + w_ref[2:3, :] * xn
        return _silu(y)

    qo[...] = (conv(q_ref, qp_ref, qn_ref, wq_ref) * (HEAD_DIM ** -0.5)).astype(BF16)
    ko[...] = conv(k_ref, kp_ref, kn_ref, wk_ref).astype(BF16)


def _conv_qk(uq, uk, wq, wk, tm):
    t = uq.shape[0]
    nt = t // tm
    r8 = tm // 8
    last8 = t // 8 - 1
    main = pl.BlockSpec((tm, MIX_HALF), lambda i: (i, 0))
    prev = pl.BlockSpec((8, MIX_HALF), lambda i: (jnp.maximum(i * r8 - 1, 0), 0))
    nxt = pl.BlockSpec((8, MIX_HALF), lambda i: (jnp.minimum((i + 1) * r8, last8), 0))
    wspec = pl.BlockSpec((3, MIX_HALF), lambda i: (0, 0))
    out = jax.ShapeDtypeStruct((t, MIX_HALF), BF16)
    return pl.pallas_call(
        functools.partial(_conv_kernel, nt=nt),
        grid=(nt,),
        in_specs=[main, prev, nxt, main, prev, nxt, wspec, wspec],
        out_specs=[main, main],
        out_shape=[out, out],
        compiler_params=_cparams(("arbitrary",)),
        name="conv_qk",
    )(uq, uq, uq, uk, uk, uk, wq, wk)


def _mlstm_kernel(*refs, need_out):
    if need_out:
        (qf, kf, vf, gcf, grf, qb, kb, vb, gcb, grb, s0, n0, m0,
         hf_o, hb_o, s_s, n_s, m_s) = refs
        q_refs, h_outs = (qf, qb), (hf_o, hb_o)
    else:
        (kf, vf, gcf, grf, kb, vb, gcb, grb, s0, n0, m0,
         s_o, n_o, m_o, s_s, n_s, m_s) = refs
    k_refs, v_refs, gc_refs, gr_refs = (kf, kb), (vf, vb), (gcf, gcb), (grf, grb)
    j = pl.program_id(0)

    @pl.when(j == 0)
    def _():
        s_s[...] = s0[...]
        n_s[...] = n0[...]
        m_s[...] = m0[...]

    L = CHUNK
    row = lax.broadcasted_iota(jnp.int32, (L, L), 0)
    col = lax.broadcasted_iota(jnp.int32, (L, L), 1)
    tril = row >= col
    triu = row <= col
    tril_bf = jnp.where(tril, 1.0, 0.0).astype(BF16)
    triu_bf = jnp.where(triu, 1.0, 0.0).astype(BF16)

    for d in range(2):
        m_col, m_row, mask = (tril_bf, triu_bf, tril) if d == 0 else (triu_bf, tril_bf, triu)
        gc = gc_refs[d][...]
        gr = gr_refs[d][...]
        lf_c = _log_sigmoid(gc)
        lf_r = _log_sigmoid(gr)
        c_hi, c_mid, c_lo = _split3(lf_c)
        b_c = _dot(m_col, c_hi) + _dot(m_col, c_mid) + _dot(m_col, c_lo)
        r_hi, r_mid, r_lo = _split3(lf_r)
        b_r = _dot(r_hi, m_row) + _dot(r_mid, m_row) + _dot(r_lo, m_row)
        for h in range(HEADS):
            hd = d * HEADS + h
            ci, cf = 8 * d + h, 8 * d + 4 + h
            hs = slice(h * HEAD_DIM, (h + 1) * HEAD_DIM)
            i_c, bc = gc[:, ci:ci + 1], b_c[:, cf:cf + 1]
            i_r, br = gr[ci:ci + 1, :], b_r[cf:cf + 1, :]
            g = jnp.sum(lf_r[cf:cf + 1, :], axis=1, keepdims=True)
            m_old = m_s[hd][:, 0:1]
            s_old = s_s[hd]
            n_old = n_s[hd]
            a_r = g - br + i_r
            a_c = g - bc + i_c
            m_new = jnp.maximum(g + m_old, jnp.max(a_r, axis=1, keepdims=True))
            decay = jnp.exp(g + m_old - m_new)
            wk_c = jnp.exp(a_c - m_new)
            k_h = k_refs[d][:, hs]
            v_h = v_refs[d][:, hs]
            kw = k_h.astype(F32) * wk_c
            n_new = decay * n_old + jnp.sum(kw, axis=0, keepdims=True)
            s_new = decay * s_old + _dot(kw.T.astype(BF16), v_h)
            if need_out:
                q_h = q_refs[d][:, hs]
                dm = jnp.where(mask, bc - br + i_r, -jnp.inf)
                inter = bc + m_old
                m_t = jnp.maximum(inter, jnp.max(dm, axis=1, keepdims=True))
                w_inter = jnp.exp(inter - m_t)
                qk = lax.dot_general(q_h, k_h, (((1,), (1,)), ((), ())),
                                     preferred_element_type=F32)
                p = jnp.exp(dm - m_t) * qk
                num = w_inter * _dot(q_h, s_old.astype(BF16)) + _dot(p.astype(BF16), v_h)
                den = (w_inter * jnp.sum(q_h.astype(F32) * n_old, axis=1, keepdims=True)
                       + jnp.sum(p, axis=1, keepdims=True))
                h_outs[d][:, hs] = num / jnp.maximum(jnp.abs(den), jnp.exp(-m_t))
            s_s[hd] = s_new
            n_s[hd] = n_new
            m_s[hd] = jnp.broadcast_to(m_new, (1, LANES))

    if not need_out:
        s_o[...] = s_s[...]
        n_o[...] = n_s[...]
        m_o[...] = m_s[...]


def _mlstm(q, k, v, gc, gr, s0, n0, m0, need_out):
    t = k.shape[0]
    nc = t // CHUNK
    fwd = lambda j: (j, 0)
    bwd = lambda j: (nc - 1 - j, 0)
    fwd_t = lambda j: (0, j)
    bwd_t = lambda j: (0, nc - 1 - j)
    c3 = lambda j: (0, 0, 0)
    seq = lambda im: pl.BlockSpec((CHUNK, MIX_HALF), im)
    state_specs = [pl.BlockSpec(s0.shape, c3), pl.BlockSpec(n0.shape, c3), pl.BlockSpec(m0.shape, c3)]
    scratch = [pltpu.VMEM(s0.shape, F32), pltpu.VMEM(n0.shape, F32), pltpu.VMEM(m0.shape, F32)]

    def side(im, im_t):
        specs = ([seq(im)] if need_out else []) + [seq(im), seq(im)]
        return specs + [pl.BlockSpec((CHUNK, N_GATES), im), pl.BlockSpec((N_GATES, CHUNK), im_t)]

    in_specs = side(fwd, fwd_t) + side(bwd, bwd_t) + state_specs
    seq_in = ((q,) if need_out else ()) + (k, v, gc, gr)
    args = seq_in + seq_in + (s0, n0, m0)
    if need_out:
        out_specs = [seq(fwd), seq(bwd)]
        out_shape = [jax.ShapeDtypeStruct((t, MIX_HALF), F32)] * 2
    else:
        out_specs = state_specs
        out_shape = [jax.ShapeDtypeStruct(a.shape, F32) for a in (s0, n0, m0)]
    return pl.pallas_call(
        functools.partial(_mlstm_kernel, need_out=need_out),
        grid=(nc,),
        in_specs=in_specs,
        out_specs=out_specs,
        out_shape=out_shape,
        scratch_shapes=scratch,
        compiler_params=_cparams(("arbitrary",)),
        name="mlstm_out" if need_out else "mlstm_state",
    )(*args)


POOL_PAD = 512


def _pool_kernel(u_ref, w_ref, sc_ref, o_ref, pad_s, *, win, t):
    half = win // 2
    tile = 256
    zeros = jnp.zeros((POOL_PAD, POOL_GROUP), F32)
    pad_s[0:POOL_PAD, :] = zeros
    pad_s[POOL_PAD + t:POOL_PAD + t + POOL_PAD, :] = zeros

    def copy(r, carry):
        t0 = pl.multiple_of(r * tile, tile)
        pad_s[pl.ds(POOL_PAD + t0, tile), :] = u_ref[pl.ds(t0, tile), :]
        return carry

    lax.fori_loop(0, t // tile, copy, 0)

    row = lax.broadcasted_iota(jnp.int32, (tile, tile), 0)
    col = lax.broadcasted_iota(jnp.int32, (tile, tile), 1)
    same_row = (row >> GRID_SHIFT) == (col >> GRID_SHIFT)
    in_win = (col - row >= -half) & (col - row < half)
    band = jnp.where(same_row & in_win, 1.0, 0.0).astype(BF16)
    w = w_ref[...].astype(BF16)
    scale = sc_ref[...]
    n_rows = t // GRID_W

    def body(r, carry):
        t0 = pl.multiple_of(r * tile, tile)
        acc = jnp.zeros((tile, POOL_GROUP), F32)
        for dd in range(-half, half):
            acc = acc + pad_s[pl.ds(POOL_PAD + t0 + GRID_W * dd, tile), :]
        tok = t0 + lax.broadcasted_iota(jnp.int32, (tile, POOL_GROUP), 0)
        gr = tok >> GRID_SHIFT
        gc = tok & (GRID_W - 1)
        cnt_v = jnp.minimum(gr + half, n_rows) - jnp.maximum(gr - half, 0)
        cnt_h = jnp.minimum(gc + half, GRID_W) - jnp.maximum(gc - half, 0)
        mean_v = acc / cnt_v.astype(F32)
        hi, lo = _split2(mean_v)
        mean = (_dot(band, hi) + _dot(band, lo)) / cnt_h.astype(F32)
        x = pad_s[pl.ds(POOL_PAD + t0, tile), :]
        y = _dot((mean - x).astype(BF16), w) * scale
        o_ref[pl.ds(t0, tile), :] = y.astype(BF16)
        return carry

    lax.fori_loop(0, t // tile, body, 0)


def _pool(u_pool, w_pool_g, scale_row, gi, win):
    t = u_pool.shape[0]
    return pl.pallas_call(
        functools.partial(_pool_kernel, win=win, t=t),
        grid=(1,),
        in_specs=[pl.BlockSpec((t, POOL_GROUP), lambda i: (0, gi)),
                  pl.BlockSpec((POOL_GROUP, POOL_GROUP), lambda i: (0, 0)),
                  pl.BlockSpec((1, POOL_GROUP), lambda i: (0, gi))],
        out_specs=pl.BlockSpec((t, POOL_GROUP), lambda i: (0, 0)),
        out_shape=jax.ShapeDtypeStruct((t, POOL_GROUP), BF16),
        scratch_shapes=[pltpu.VMEM((t + 2 * POOL_PAD, POOL_GROUP), F32)],
        compiler_params=_cparams(("arbitrary",)),
        name=f"pool_w{win}",
    )(u_pool, w_pool_g, scale_row)


def _route(logits):
    lane = lax.broadcasted_iota(jnp.int32, logits.shape, 1).astype(F32)
    neg = -jnp.inf
    big = float(LANES)
    gl = jnp.where(lane < N_GROUPS, logits, neg)
    gmax = jnp.max(gl, axis=1, keepdims=True)
    gsel = jnp.min(jnp.where(gl == gmax, lane, big), axis=1, keepdims=True)
    p_grp = 1.0 / jnp.sum(jnp.exp(gl - gmax), axis=1, keepdims=True)
    lo = ROUTE_LANE0 + EXPERTS_PER_GROUP * gsel
    el = jnp.where((lane >= lo) & (lane < lo + EXPERTS_PER_GROUP), logits, neg)
    m1 = jnp.max(el, axis=1, keepdims=True)
    i1 = jnp.min(jnp.where(el == m1, lane, big), axis=1, keepdims=True)
    el2 = jnp.where(lane == i1, neg, el)
    m2 = jnp.max(el2, axis=1, keepdims=True)
    i2 = jnp.min(jnp.where(el2 == m2, lane, big), axis=1, keepdims=True)
    e2 = jnp.exp(m2 - m1)
    p1 = 1.0 / (1.0 + e2)
    p2 = e2 / (1.0 + e2)
    info = jnp.where(lane == 0.0, i1 - ROUTE_LANE0, 0.0)
    info = jnp.where(lane == 1.0, i2 - ROUTE_LANE0, info)
    info = jnp.where(lane == 2.0, p_grp * p1, info)
    return jnp.where(lane == 3.0, p_grp * p2, info)


def _outproj_kernel(p0, p1, p2, p3, hf_ref, hb_ref, uo_ref, x_ref, wout_ref, mod_ref,
                    hg_ref, n2g_ref, wrh_ref, wrl_ref, br_ref, h1_o, fn_o, info_o):
    h = hf_ref[...] + hb_ref[...]
    parts = []
    for hh in range(HEADS):
        hs = h[:, hh * HEAD_DIM:(hh + 1) * HEAD_DIM]
        mu = jnp.mean(hs, axis=-1, keepdims=True)
        ctr = hs - mu
        var = jnp.mean(ctr * ctr, axis=-1, keepdims=True)
        parts.append(ctr * lax.rsqrt(var + EPS))
    hn = jnp.concatenate(parts, axis=1) * hg_ref[...]
    m = (hn * jax.nn.sigmoid(uo_ref[...])).astype(BF16)
    mix = _dot(m, wout_ref[MIX_HALF:, :])
    for gi, p_ref in enumerate((p0, p1, p2, p3)):
        mix = mix + _dot(p_ref[...], wout_ref[gi * POOL_GROUP:(gi + 1) * POOL_GROUP, :])
    h1 = x_ref[...] + mod_ref[2:3, :] * mix
    h1_o[...] = h1
    ms = jnp.mean(h1 * h1, axis=-1, keepdims=True)
    fn = h1 * lax.rsqrt(ms + EPS) * n2g_ref[...]
    fn = fn * (1.0 + mod_ref[4:5, :]) + mod_ref[3:4, :]
    fh, fl = _split2(fn)
    fn_o[...] = fh
    wrh = wrh_ref[...]
    logits = _dot(fh, wrh) + _dot(fh, wrl_ref[...]) + _dot(fl, wrh) + br_ref[...]
    info_o[...] = _route(logits)


def _outproj(ps, hf, hb, uo, x2d, w_out, mod, head_g, norm2_g, wr_hi, wr_lo, b_route, tm):
    t = x2d.shape[0]
    const = lambda i: (0, 0)
    row = lambda i: (i, 0)
    in_specs = ([pl.BlockSpec((tm, POOL_GROUP), row)] * 4
                + [pl.BlockSpec((tm, MIX_HALF), row)] * 3
                + [pl.BlockSpec((tm, D_MODEL), row),
                   pl.BlockSpec(w_out.shape, const, pipeline_mode=pl.Buffered(1)),
                   pl.BlockSpec(mod.shape, const),
                   pl.BlockSpec((1, MIX_HALF), const),
                   pl.BlockSpec((1, D_MODEL), const),
                   pl.BlockSpec((D_MODEL, LANES), const),
                   pl.BlockSpec((D_MODEL, LANES), const),
                   pl.BlockSpec((1, LANES), const)])
    return pl.pallas_call(
        _outproj_kernel,
        grid=(t // tm,),
        in_specs=in_specs,
        out_specs=[pl.BlockSpec((tm, D_MODEL), row), pl.BlockSpec((tm, D_MODEL), row),
                   pl.BlockSpec((tm, LANES), row)],
        out_shape=[jax.ShapeDtypeStruct((t, D_MODEL), F32),
                   jax.ShapeDtypeStruct((t, D_MODEL), BF16),
                   jax.ShapeDtypeStruct((t, LANES), F32)],
        compiler_params=_cparams(("arbitrary",)),
        name="outproj_route",
    )(*ps, hf, hb, uo, x2d, w_out, mod, head_g, norm2_g, wr_hi, wr_lo, b_route)


DISPATCH_BLOCK = 256
GRANULE = 16
GRANULE_SHIFT = 4
LOCAL_CAP = 1024
LOCAL_GRANULES = LOCAL_CAP // GRANULE
EXPERT_TILE = 128
TILE_GRANULES = EXPERT_TILE // GRANULE


def _dispatch_kernel(fn_ref, info_ref, xs_o, ws_o, pt_o, cnt_o):
    tb = DISPATCH_BLOCK
    info = info_ref[...]
    e1, e2 = info[:, 0:1], info[:, 1:2]
    w1c, w2c = info[:, 2:3], info[:, 3:4]
    lane = lax.broadcasted_iota(jnp.int32, (tb, LANES), 1).astype(F32)
    o1 = jnp.where(lane == e1, 1.0, 0.0)
    o2 = jnp.where(lane == e2, 1.0, 0.0)
    onehot = o1 + o2
    cnt = jnp.sum(onehot, axis=0, keepdims=True)
    gran = jnp.floor((cnt + (GRANULE - 1)) * (1.0 / GRANULE))
    a = lax.broadcasted_iota(jnp.int32, (LANES, LANES), 0)
    b = lax.broadcasted_iota(jnp.int32, (LANES, LANES), 1)
    upper = jnp.where(a < b, 1.0, 0.0).astype(BF16)
    seg_off = _dot(jnp.broadcast_to(gran, (8, LANES)).astype(BF16), upper)[0:1, :] * GRANULE
    r = lax.broadcasted_iota(jnp.int32, (tb, tb), 0)
    c = lax.broadcasted_iota(jnp.int32, (tb, tb), 1)
    strict = jnp.where(r > c, 1.0, 0.0).astype(BF16)
    rank = _dot(strict, onehot.astype(BF16))
    slot = rank + seg_off
    pos1 = jnp.sum(o1 * slot, axis=1, keepdims=True)
    pos2 = jnp.sum(o2 * slot, axis=1, keepdims=True)
    rows = lax.broadcasted_iota(jnp.int32, (tb, LOCAL_CAP), 1).astype(F32)
    pt1 = jnp.where(rows == pos1, 1.0, 0.0)
    pt2 = jnp.where(rows == pos2, 1.0, 0.0)
    pt = pt1 + pt2
    pt_o[...] = pt.astype(BF16)
    perm = pt.T.astype(BF16)
    w_slot = jnp.sum((pt1 * w1c + pt2 * w2c).T, axis=1, keepdims=True)
    xs_o[...] = _dot(perm, fn_ref[...]).astype(BF16)
    ws_o[...] = jnp.broadcast_to(w_slot, (LOCAL_CAP, LANES))
    cnt_o[...] = jnp.broadcast_to(cnt, (8, LANES)).astype(jnp.int32)


def _dispatch(fn, info):
    t = fn.shape[0]
    nb = t // DISPATCH_BLOCK
    row = lambda i: (i, 0)
    return pl.pallas_call(
        _dispatch_kernel,
        grid=(nb,),
        in_specs=[pl.BlockSpec((DISPATCH_BLOCK, D_MODEL), row),
                  pl.BlockSpec((DISPATCH_BLOCK, LANES), row)],
        out_specs=[pl.BlockSpec((LOCAL_CAP, D_MODEL), row),
                   pl.BlockSpec((LOCAL_CAP, LANES), row),
                   pl.BlockSpec((DISPATCH_BLOCK, LOCAL_CAP), row),
                   pl.BlockSpec((8, LANES), row)],
        out_shape=[jax.ShapeDtypeStruct((nb * LOCAL_CAP, D_MODEL), BF16),
                   jax.ShapeDtypeStruct((nb * LOCAL_CAP, LANES), F32),
                   jax.ShapeDtypeStruct((t, LOCAL_CAP), BF16),
                   jax.ShapeDtypeStruct((nb * 8, LANES), jnp.int32)],
        compiler_params=_cparams(("arbitrary",)),
        name="moe_dispatch",
    )(fn, info)


def _plan_kernel(cnt_ref, gmap_o, texp_o, ntile_o, lrun, *, nb, max_tiles):
    def init(b, c):
        lrun[b] = 0
        return c

    lax.fori_loop(0, nb, init, 0)

    def per_expert(e, carry):
        g0, last_e = carry

        def per_block(b, g):
            k = (cnt_ref[b, e] + (GRANULE - 1)) >> GRANULE_SHIFT
            lo = lrun[b]
            lrun[b] = lo + k

            def put(j, c):
                gmap_o[g + j] = b * LOCAL_GRANULES + lo + j
                return c

            lax.fori_loop(0, k, put, 0)
            return g + k

        g1 = lax.fori_loop(0, nb, per_block, g0)
        pad = (-g1) & (TILE_GRANULES - 1)

        def put_pad(j, c):
            gmap_o[g1 + j] = -1
            return c

        lax.fori_loop(0, pad, put_pad, 0)
        g2 = g1 + pad

        def put_tile(tt, c):
            texp_o[tt] = e
            return c

        lax.fori_loop(g0 // TILE_GRANULES, g2 // TILE_GRANULES, put_tile, 0)
        return g2, jnp.where(g2 > g0, e, last_e)

    g_end, last_e = lax.fori_loop(0, N_EXPERTS, per_expert, (0, 0))
    n_tiles = g_end // TILE_GRANULES
    ntile_o[0] = n_tiles

    def fill(tt, c):
        texp_o[tt] = last_e
        return c

    lax.fori_loop(n_tiles, max_tiles, fill, 0)

    def fill_map(g, c):
        gmap_o[g] = -1
        return c

    lax.fori_loop(g_end, max_tiles * TILE_GRANULES, fill_map, 0)


def _max_tiles(t):
    nb = t // DISPATCH_BLOCK
    worst_rows = 2 * t + nb * N_EXPERTS * (GRANULE - 1) + N_EXPERTS * (EXPERT_TILE - GRANULE)
    return -(-worst_rows // EXPERT_TILE)


def _plan(cnt, t):
    nb = cnt.shape[0]
    max_tiles = _max_tiles(t)
    smem = pl.BlockSpec(memory_space=pltpu.SMEM)
    return pl.pallas_call(
        functools.partial(_plan_kernel, nb=nb, max_tiles=max_tiles),
        in_specs=[smem],
        out_specs=[smem, smem, smem],
        out_shape=[jax.ShapeDtypeStruct((max_tiles * TILE_GRANULES,), jnp.int32),
                   jax.ShapeDtypeStruct((max_tiles,), jnp.int32),
                   jax.ShapeDtypeStruct((1,), jnp.int32)],
        scratch_shapes=[pltpu.SMEM((nb,), jnp.int32)],
        name="moe_plan",
    )(cnt)


def _experts_kernel(gmap, texp, ntile, xy_in, ws_in, w1_ref, w3_ref, w2_ref, xy_out,
                    xbuf, wsbuf, ybuf, wb1, wb3, wb2, gsem, ssem, n_gather, n_scatter):
    t = pl.program_id(0)
    nt = ntile[0]
    slot = t % 2

    def rows(i):
        return pl.ds(pl.multiple_of(i * GRANULE, GRANULE), GRANULE)

    def gather_copies(gm, j, sl):
        return (pltpu.make_async_copy(xy_in.at[rows(gm), :], xbuf.at[sl, rows(j), :], gsem.at[sl]),
                pltpu.make_async_copy(ws_in.at[rows(gm), :], wsbuf.at[sl, rows(j), :], gsem.at[sl]))

    def scatter_copies(gm, j, sl):
        return (pltpu.make_async_copy(ybuf.at[sl, rows(j), :], xy_out.at[rows(gm), :], ssem.at[sl]),)

    def issue(tt, sl, copies, counter):
        def body(j, n):
            gm = gmap[tt * TILE_GRANULES + j]

            @pl.when(gm >= 0)
            def _():
                for cp in copies(gm, j, sl):
                    cp.start()

            return n + (gm >= 0).astype(jnp.int32)

        counter[sl] = lax.fori_loop(0, TILE_GRANULES, body, 0)

    def drain(sl, copies, counter):
        def body(j, c):
            for cp in copies(0, 0, sl):
                cp.wait()
            return c

        lax.fori_loop(0, counter[sl], body, 0)

    @pl.when(t == 0)
    def _():
        xbuf[...] = jnp.zeros_like(xbuf)
        wsbuf[...] = jnp.zeros_like(wsbuf)
        issue(0, 0, gather_copies, n_gather)

    @pl.when(t < nt)
    def _():
        drain(slot, gather_copies, n_gather)

        @pl.when(t + 1 < nt)
        def _():
            issue(t + 1, 1 - slot, gather_copies, n_gather)

        @pl.when((t == 0) | (texp[jnp.maximum(t - 1, 0)] != texp[t]))
        def _():
            wb1[...] = w1_ref[...].astype(BF16)
            wb3[...] = w3_ref[...].astype(BF16)
            wb2[...] = w2_ref[...].astype(BF16)

        x = xbuf[slot]
        w_row = wsbuf[slot][:, 0:1]
        a = _dot(x, wb1[...])
        b = _dot(x, wb3[...])
        y = _dot((_silu(a) * b).astype(BF16), wb2[...]) * w_row

        @pl.when(t >= 2)
        def _():
            drain(slot, scatter_copies, n_scatter)

        ybuf[slot] = y.astype(BF16)
        issue(t, slot, scatter_copies, n_scatter)

        @pl.when(t == nt - 1)
        def _():
            drain(slot, scatter_copies, n_scatter)

            @pl.when(t >= 1)
            def _():
                drain(1 - slot, scatter_copies, n_scatter)


def _experts(gmap, texp, ntile, xy, ws, w1, w3, w2):
    max_tiles = texp.shape[0]
    wmap = lambda t, gmap, texp, ntile: (texp[t], 0, 0)
    grid_spec = pltpu.PrefetchScalarGridSpec(
        num_scalar_prefetch=3,
        grid=(max_tiles,),
        in_specs=[pl.BlockSpec(memory_space=pl.ANY),
                  pl.BlockSpec(memory_space=pl.ANY),
                  pl.BlockSpec((None, D_MODEL, D_EXPERT), wmap),
                  pl.BlockSpec((None, D_MODEL, D_EXPERT), wmap),
                  pl.BlockSpec((None, D_EXPERT, D_MODEL), wmap)],
        out_specs=pl.BlockSpec(memory_space=pl.ANY),
        scratch_shapes=[pltpu.VMEM((2, EXPERT_TILE, D_MODEL), BF16),
                        pltpu.VMEM((2, EXPERT_TILE, LANES), F32),
                        pltpu.VMEM((2, EXPERT_TILE, D_MODEL), BF16),
                        pltpu.VMEM((D_MODEL, D_EXPERT), BF16),
                        pltpu.VMEM((D_MODEL, D_EXPERT), BF16),
                        pltpu.VMEM((D_EXPERT, D_MODEL), BF16),
                        pltpu.SemaphoreType.DMA((2,)),
                        pltpu.SemaphoreType.DMA((2,)),
                        pltpu.SMEM((2,), jnp.int32),
                        pltpu.SMEM((2,), jnp.int32)])
    return pl.pallas_call(
        _experts_kernel,
        grid_spec=grid_spec,
        out_shape=jax.ShapeDtypeStruct(xy.shape, xy.dtype),
        input_output_aliases={3: 0},
        compiler_params=_cparams(("arbitrary",)),
        name="moe_experts",
    )(gmap, texp, ntile, xy, ws, w1, w3, w2)


def _combine_kernel(y_ref, pt_ref, h1_ref, g2_ref, fg_ref, o_ref):
    moe = _dot(pt_ref[...], y_ref[...])
    h = h1_ref[...] + g2_ref[...] * moe
    ms = jnp.mean(h * h, axis=-1, keepdims=True)
    o_ref[...] = h * lax.rsqrt(ms + EPS) * fg_ref[...]


def _combine(xy, pt, h1, g2, final_g):
    t = h1.shape[0]
    row = lambda i: (i, 0)
    const = lambda i: (0, 0)
    return pl.pallas_call(
        _combine_kernel,
        grid=(t // DISPATCH_BLOCK,),
        in_specs=[pl.BlockSpec((LOCAL_CAP, D_MODEL), row),
                  pl.BlockSpec((DISPATCH_BLOCK, LOCAL_CAP), row),
                  pl.BlockSpec((DISPATCH_BLOCK, D_MODEL), row),
                  pl.BlockSpec((1, D_MODEL), const), pl.BlockSpec((1, D_MODEL), const)],
        out_specs=pl.BlockSpec((DISPATCH_BLOCK, D_MODEL), row),
        out_shape=jax.ShapeDtypeStruct((t, D_MODEL), F32),
        compiler_params=_cparams(("arbitrary",)),
        name="moe_combine_final",
    )(xy, pt, h1, g2, final_g)


def _pad_lanes(a):
    return jnp.pad(a, ((0, 0), (0, LANES - a.shape[1])))


def kernel(x, c, ctx, c_ctx, w_mod, b_mod, norm1_g, w_in, w_conv_q, w_conv_k, gate_bias, head_norm_g, w_pool, pool_scale, w_out, norm2_g, w_group, b_group, w_router, b_router, w1, w3, w2, final_g):
    assert x.shape[0] == 1 and w_mod.shape[0] == 1
    seq = x.shape[1]
    x2d = x[0]
    ctx2d = ctx[0]

    n_main = 5 * MIX_HALF
    w_main = w_in[0][:, :n_main].astype(BF16)
    wg_hi, wg_lo = _split2(_pad_lanes(w_in[0][:, n_main:]))
    gate_bias_row = _pad_lanes(gate_bias[0].reshape(1, N_GATES))
    w_out_bf = w_out[0].astype(BF16)
    wr_hi, wr_lo = _split2(_pad_lanes(jnp.concatenate([w_group[0], w_router[0]], axis=1)))
    b_route = _pad_lanes(jnp.concatenate([b_group[0], b_router[0]]).reshape(1, -1))
    norm1 = norm1_g[0].reshape(1, D_MODEL)

    c16 = jnp.zeros((16, D_MODEL), F32).at[0].set(c[0]).at[1].set(c_ctx)
    mods = _adaln(c16, w_mod[0], b_mod[0].reshape(1, -1))
    mod_lat = mods[0].reshape(6, D_MODEL)
    mod_ctx = mods[1].reshape(6, D_MODEL)

    _, uq_c, _, uk_c, v_c, gc_c, gr_c = _inproj(ctx2d, mod_ctx[0:2], norm1, w_main, wg_hi, wg_lo,
                                                gate_bias_row, tm=256)
    _, k_c = _conv_qk(uq_c, uk_c, w_conv_q[0], w_conv_k[0], tm=256)
    zeros_state = (jnp.zeros((2 * HEADS, HEAD_DIM, HEAD_DIM), F32),
                   jnp.zeros((2 * HEADS, 1, HEAD_DIM), F32),
                   jnp.zeros((2 * HEADS, 1, LANES), F32))
    s0, n0, m0 = _mlstm(None, k_c, v_c, gc_c, gr_c, *zeros_state, need_out=False)

    u_pool, uq, uo, uk, v, gc, gr = _inproj(x2d, mod_lat[0:2], norm1, w_main, wg_hi, wg_lo,
                                            gate_bias_row, tm=256)
    q, k = _conv_qk(uq, uk, w_conv_q[0], w_conv_k[0], tm=512)
    hf, hb = _mlstm(q, k, v, gc, gr, s0, n0, m0, need_out=True)
    ps = [_pool(u_pool, w_pool[0, gi], pool_scale[0].reshape(1, -1), gi, win)
          for gi, win in enumerate(POOL_WINDOWS)]
    h1, fn, info = _outproj(ps, hf, hb, uo, x2d, w_out_bf, mod_lat, head_norm_g[0].reshape(1, -1),
                            norm2_g[0].reshape(1, -1), wr_hi, wr_lo, b_route, tm=256)

    xs, ws, pt, cnt = _dispatch(fn, info)
    gmap, texp, ntile = _plan(cnt[::8, :N_EXPERTS], seq)
    xy = _experts(gmap, texp, ntile, xs, ws, w1[0], w3[0], w2[0])
    out = _combine(xy, pt, h1, mod_lat[5:6], final_g.reshape(1, -1))
    return out.reshape(1, seq, D_MODEL)
```

```python
import functools

import jax
import jax.numpy as jnp
from jax import lax
from jax.experimental import pallas as pl
from jax.experimental.pallas import tpu as pltpu

F32 = jnp.float32
BF16 = jnp.bfloat16

D_MODEL = 2048
GRID_W = 64
GRID_SHIFT = 6
POOL_WINDOWS = (2, 4, 8, 16)
POOL_GROUP = 256
HEADS = 4
HEAD_DIM = 256
MIX_HALF = 1024
N_GATES = 16
N_GROUPS = 4
EXPERTS_PER_GROUP = 8
N_EXPERTS = 32
D_EXPERT = 512
EPS = 1e-6
LANES = 128
CHUNK = 256
ROUTE_LANE0 = N_GROUPS

VMEM_LIMIT = 56 * 1024 * 1024


def _cparams(sem, vmem=VMEM_LIMIT):
    return pltpu.CompilerParams(dimension_semantics=sem, vmem_limit_bytes=vmem)


def _split2(x):
    hi = x.astype(BF16)
    lo = (x - hi.astype(F32)).astype(BF16)
    return hi, lo


def _split3(x):
    hi = x.astype(BF16)
    r = x - hi.astype(F32)
    mid = r.astype(BF16)
    lo = (r - mid.astype(F32)).astype(BF16)
    return hi, mid, lo


def _dot(a, b):
    return jnp.dot(a, b, preferred_element_type=F32)


def _silu(x):
    return x * jax.nn.sigmoid(x)


def _log_sigmoid(x):
    return jnp.minimum(x, 0.0) - jnp.log(1.0 + jnp.exp(-jnp.abs(x)))


def _adaln_kernel(c_ref, w_ref, b_ref, o_ref):
    a = _silu(c_ref[...])
    a3 = jnp.concatenate(_split3(a), axis=0)
    w_hi, w_lo = _split2(w_ref[...])
    acc = _dot(a3, w_hi)
    acc_lo = _dot(a3[:32], w_lo)
    out = acc[0:16] + acc[16:32] + acc[32:48] + acc_lo[0:16] + acc_lo[16:32]
    o_ref[...] = out + b_ref[...]


def _adaln(c16, w_mod, b_mod):
    n = w_mod.shape[1]
    tn = 1536
    return pl.pallas_call(
        _adaln_kernel,
        grid=(n // tn,),
        in_specs=[pl.BlockSpec((16, D_MODEL), lambda j: (0, 0)),
                  pl.BlockSpec((D_MODEL, tn), lambda j: (0, j)),
                  pl.BlockSpec((1, tn), lambda j: (0, j))],
        out_specs=pl.BlockSpec((16, tn), lambda j: (0, j)),
        out_shape=jax.ShapeDtypeStruct((16, n), F32),
        compiler_params=_cparams(("arbitrary",)),
        name="adaln",
    )(c16, w_mod, b_mod)


def _inproj_kernel(x_ref, mod_ref, g_ref, w_ref, wgh_ref, wgl_ref, gb_ref,
                   pool_o, q_o, o_o, k_o, v_o, gc_o, gr_o):
    x = x_ref[...]
    ms = jnp.mean(x * x, axis=-1, keepdims=True)
    y = x * lax.rsqrt(ms + EPS) * g_ref[...]
    xn = y * (1.0 + mod_ref[1:2, :]) + mod_ref[0:1, :]
    xh, xl = _split2(xn)
    for ci, o in enumerate((pool_o, q_o, o_o, k_o, v_o)):
        o[...] = _dot(xh, w_ref[:, ci * MIX_HALF:(ci + 1) * MIX_HALF]).astype(o.dtype)
    wgh = wgh_ref[...]
    gates = _dot(xh, wgh) + _dot(xh, wgl_ref[...]) + _dot(xl, wgh) + gb_ref[...]
    gc_o[...] = gates[:, :N_GATES]
    gr_o[...] = gates.T[:N_GATES, :]


def _inproj(x2d, mod, g, w_main, wg_hi, wg_lo, gate_bias_row, tm):
    t = x2d.shape[0]
    nt = t // tm
    const = lambda i: (0, 0)
    row = lambda i: (i, 0)
    f32_out = jax.ShapeDtypeStruct((t, MIX_HALF), F32)
    return pl.pallas_call(
        _inproj_kernel,
        grid=(nt,),
        in_specs=[pl.BlockSpec((tm, D_MODEL), row),
                  pl.BlockSpec((2, D_MODEL), const),
                  pl.BlockSpec((1, D_MODEL), const),
                  pl.BlockSpec(w_main.shape, const, pipeline_mode=pl.Buffered(1)),
                  pl.BlockSpec((D_MODEL, LANES), const),
                  pl.BlockSpec((D_MODEL, LANES), const),
                  pl.BlockSpec((1, LANES), const)],
        out_specs=[pl.BlockSpec((tm, MIX_HALF), row)] * 5
        + [pl.BlockSpec((tm, N_GATES), row), pl.BlockSpec((N_GATES, tm), lambda i: (0, i))],
        out_shape=[f32_out, f32_out, f32_out, f32_out,
                   jax.ShapeDtypeStruct((t, MIX_HALF), BF16),
                   jax.ShapeDtypeStruct((t, N_GATES), F32),
                   jax.ShapeDtypeStruct((N_GATES, t), F32)],
        compiler_params=_cparams(("arbitrary",)),
        name="inproj",
    )(x2d, mod, g, w_main, wg_hi, wg_lo, gate_bias_row)


def _conv_kernel(q_ref, qp_ref, qn_ref, k_ref, kp_ref, kn_ref, wq_ref, wk_ref, qo, ko, *, nt):
    i = pl.program_id(0)
    has_prev = (i > 0).astype(F32)
    has_next = (i < nt - 1).astype(F32)

    def conv(x_ref, p_ref, n_ref, w_ref):
        x = x_ref[...]
        tm = x.shape[0]
        row = lax.broadcasted_iota(jnp.int32, x.shape, 0)
        prev_row = p_ref[7:8, :] * has_prev
        next_row = n_ref[0:1, :] * has_next
        xp = jnp.where(row == 0, prev_row, pltpu.roll(x, 1, 0))
        xn = jnp.where(row == tm - 1, next_row, pltpu.roll(x, tm - 1, 0))
        y = w_ref[0:1, :] * xp + w_ref[1:2, :] * x + w_ref[2:3, :] * xn
        return _silu(y)

    qo[...] = (conv(q_ref, qp_ref, qn_ref, wq_ref) * (HEAD_DIM ** -0.5)).astype(BF16)
    ko[...] = conv(k_ref, kp_ref, kn_ref, wk_ref).astype(BF16)


def _conv_qk(uq, uk, wq, wk, tm):
    t = uq.shape[0]
    nt = t // tm
    r8 = tm // 8
    last8 = t // 8 - 1
    main = pl.BlockSpec((tm, MIX_HALF), lambda i: (i, 0))
    prev = pl.BlockSpec((8, MIX_HALF), lambda i: (jnp.maximum(i * r8 - 1, 0), 0))
    nxt = pl.BlockSpec((8, MIX_HALF), lambda i: (jnp.minimum((i + 1) * r8, last8), 0))
    wspec = pl.BlockSpec((3, MIX_HALF), lambda i: (0, 0))
    out = jax.ShapeDtypeStruct((t, MIX_HALF), BF16)
    return pl.pallas_call(
        functools.partial(_conv_kernel, nt=nt),
        grid=(nt,),
        in_specs=[main, prev, nxt, main, prev, nxt, wspec, wspec],
        out_specs=[main, main],
        out_shape=[out, out],
        compiler_params=_cparams(("arbitrary",)),
        name="conv_qk",
    )(uq, uq, uq, uk, uk, uk, wq, wk)


def _mlstm_kernel(*refs, need_out):
    if need_out:
        (qf, kf, vf, gcf, grf, qb, kb, vb, gcb, grb, s0, n0, m0,
         hf_o, hb_o, s_s, n_s, m_s) = refs
        q_refs, h_outs = (qf, qb), (hf_o, hb_o)
    else:
        (kf, vf, gcf, grf, kb, vb, gcb, grb, s0, n0, m0,
         s_o, n_o, m_o, s_s, n_s, m_s) = refs
    k_refs, v_refs, gc_refs, gr_refs = (kf, kb), (vf, vb), (gcf, gcb), (grf, grb)
    j = pl.program_id(0)

    @pl.when(j == 0)
    def _():
        s_s[...] = s0[...]
        n_s[...] = n0[...]
        m_s[...] = m0[...]

    L = CHUNK
    row = lax.broadcasted_iota(jnp.int32, (L, L), 0)
    col = lax.broadcasted_iota(jnp.int32, (L, L), 1)
    tril = row >= col
    triu = row <= col
    tril_bf = jnp.where(tril, 1.0, 0.0).astype(BF16)
    triu_bf = jnp.where(triu, 1.0, 0.0).astype(BF16)

    for d in range(2):
        m_col, m_row, mask = (tril_bf, triu_bf, tril) if d == 0 else (triu_bf, tril_bf, triu)
        gc = gc_refs[d][...]
        gr = gr_refs[d][...]
        lf_c = _log_sigmoid(gc)
        lf_r = _log_sigmoid(gr)
        c_hi, c_mid, c_lo = _split3(lf_c)
        b_c = _dot(m_col, c_hi) + _dot(m_col, c_mid) + _dot(m_col, c_lo)
        r_hi, r_mid, r_lo = _split3(lf_r)
        b_r = _dot(r_hi, m_row) + _dot(r_mid, m_row) + _dot(r_lo, m_row)
        for h in range(HEADS):
            hd = d * HEADS + h
            ci, cf = 8 * d + h, 8 * d + 4 + h
            hs = slice(h * HEAD_DIM, (h + 1) * HEAD_DIM)
            i_c, bc = gc[:, ci:ci + 1], b_c[:, cf:cf + 1]
            i_r, br = gr[ci:ci + 1, :], b_r[cf:cf + 1, :]
            g = jnp.sum(lf_r[cf:cf + 1, :], axis=1, keepdims=True)
            m_old = m_s[hd][:, 0:1]
            s_old = s_s[hd]
            n_old = n_s[hd]
            a_r = g - br + i_r
            a_c = g - bc + i_c
            m_new = jnp.maximum(g + m_old, jnp.max(a_r, axis=1, keepdims=True))
            decay = jnp.exp(g + m_old - m_new)
            wk_c = jnp.exp(a_c - m_new)
            k_h = k_refs[d][:, hs]
            v_h = v_refs[d][:, hs]
            kw = k_h.astype(F32) * wk_c
            n_new = decay * n_old + jnp.sum(kw, axis=0, keepdims=True)
            s_new = decay * s_old + _dot(kw.T.astype(BF16), v_h)
            if need_out:
                q_h = q_refs[d][:, hs]
                dm = jnp.where(mask, bc - br + i_r, -jnp.inf)
                inter = bc + m_old
                m_t = jnp.maximum(inter, jnp.max(dm, axis=1, keepdims=True))
                w_inter = jnp.exp(inter - m_t)
                qk = lax.dot_general(q_h, k_h, (((1,), (1,)), ((), ())),
                                     preferred_element_type=F32)
                p = jnp.exp(dm - m_t) * qk
                num = w_inter * _dot(q_h, s_old.astype(BF16)) + _dot(p.astype(BF16), v_h)
                den = (w_inter * jnp.sum(q_h.astype(F32) * n_old, axis=1, keepdims=True)
                       + jnp.sum(p, axis=1, keepdims=True))
                h_outs[d][:, hs] = num / jnp.maximum(jnp.abs(den), jnp.exp(-m_t))
            s_s[hd] = s_new
            n_s[hd] = n_new
            m_s[hd] = jnp.broadcast_to(m_new, (1, LANES))

    if not need_out:
        s_o[...] = s_s[...]
        n_o[...] = n_s[...]
        m_o[...] = m_s[...]


def _mlstm(q, k, v, gc, gr, s0, n0, m0, need_out):
    t = k.shape[0]
    nc = t // CHUNK
    fwd = lambda j: (j, 0)
    bwd = lambda j: (nc - 1 - j, 0)
    fwd_t = lambda j: (0, j)
    bwd_t = lambda j: (0, nc - 1 - j)
    c3 = lambda j: (0, 0, 0)
    seq = lambda im: pl.BlockSpec((CHUNK, MIX_HALF), im)
    state_specs = [pl.BlockSpec(s0.shape, c3), pl.BlockSpec(n0.shape, c3), pl.BlockSpec(m0.shape, c3)]
    scratch = [pltpu.VMEM(s0.shape, F32), pltpu.VMEM(n0.shape, F32), pltpu.VMEM(m0.shape, F32)]

    def side(im, im_t):
        specs = ([seq(im)] if need_out else []) + [seq(im), seq(im)]
        return specs + [pl.BlockSpec((CHUNK, N_GATES), im), pl.BlockSpec((N_GATES, CHUNK), im_t)]

    in_specs = side(fwd, fwd_t) + side(bwd, bwd_t) + state_specs
    seq_in = ((q,) if need_out else ()) + (k, v, gc, gr)
    args = seq_in + seq_in + (s0, n0, m0)
    if need_out:
        out_specs = [seq(fwd), seq(bwd)]
        out_shape = [jax.ShapeDtypeStruct((t, MIX_HALF), F32)] * 2
    else:
        out_specs = state_specs
        out_shape = [jax.ShapeDtypeStruct(a.shape, F32) for a in (s0, n0, m0)]
    return pl.pallas_call(
        functools.partial(_mlstm_kernel, need_out=need_out),
        grid=(nc,),
        in_specs=in_specs,
        out_specs=out_specs,
        out_shape=out_shape,
        scratch_shapes=scratch,
        compiler_params=_cparams(("arbitrary",)),
        name="mlstm_out" if need_out else "mlstm_state",
    )(*args)


POOL_PAD = 512


def _pool_kernel(u_ref, w_ref, sc_ref, o_ref, pad_s, *, win, t):
    half = win // 2
    tile = 256
    zeros = jnp.zeros((POOL_PAD, POOL_GROUP), F32)
    pad_s[0:POOL_PAD, :] = zeros
    pad_s[POOL_PAD + t:POOL_PAD + t + POOL_PAD, :] = zeros

    def copy(r, carry):
        t0 = pl.multiple_of(r * tile, tile)
        pad_s[pl.ds(POOL_PAD + t0, tile), :] = u_ref[pl.ds(t0, tile), :]
        return carry

    lax.fori_loop(0, t // tile, copy, 0)

    row = lax.broadcasted_iota(jnp.int32, (tile, tile), 0)
    col = lax.broadcasted_iota(jnp.int32, (tile, tile), 1)
    same_row = (row >> GRID_SHIFT) == (col >> GRID_SHIFT)
    in_win = (col - row >= -half) & (col - row < half)
    band = jnp.where(same_row & in_win, 1.0, 0.0).astype(BF16)
    w = w_ref[...].astype(BF16)
    scale = sc_ref[...]
    n_rows = t // GRID_W

    def body(r, carry):
        t0 = pl.multiple_of(r * tile, tile)
        acc = jnp.zeros((tile, POOL_GROUP), F32)
        for dd in range(-half, half):
            acc = acc + pad_s[pl.ds(POOL_PAD + t0 + GRID_W * dd, tile), :]
        tok = t0 + lax.broadcasted_iota(jnp.int32, (tile, POOL_GROUP), 0)
        gr = tok >> GRID_SHIFT
        gc = tok & (GRID_W - 1)
        cnt_v = jnp.minimum(gr + half, n_rows) - jnp.maximum(gr - half, 0)
        cnt_h = jnp.minimum(gc + half, GRID_W) - jnp.maximum(gc - half, 0)
        mean_v = acc / cnt_v.astype(F32)
        hi, lo = _split2(mean_v)
        mean = (_dot(band, hi) + _dot(band, lo)) / cnt_h.astype(F32)
        x = pad_s[pl.ds(POOL_PAD + t0, tile), :]
        y = _dot((mean - x).astype(BF16), w) * scale
        o_ref[pl.ds(t0, tile), :] = y.astype(BF16)
        return carry

    lax.fori_loop(0, t // tile, body, 0)


def _pool(u_pool, w_pool_g, scale_row, gi, win):
    t = u_pool.shape[0]
    return pl.pallas_call(
        functools.partial(_pool_kernel, win=win, t=t),
        grid=(1,),
        in_specs=[pl.BlockSpec((t, POOL_GROUP), lambda i: (0, gi)),
                  pl.BlockSpec((POOL_GROUP, POOL_GROUP), lambda i: (0, 0)),
                  pl.BlockSpec((1, POOL_GROUP), lambda i: (0, gi))],
        out_specs=pl.BlockSpec((t, POOL_GROUP), lambda i: (0, 0)),
        out_shape=jax.ShapeDtypeStruct((t, POOL_GROUP), BF16),
        scratch_shapes=[pltpu.VMEM((t + 2 * POOL_PAD, POOL_GROUP), F32)],
        compiler_params=_cparams(("arbitrary",)),
        name=f"pool_w{win}",
    )(u_pool, w_pool_g, scale_row)


def _route(logits):
    lane = lax.broadcasted_iota(jnp.int32, logits.shape, 1).astype(F32)
    neg = -jnp.inf
    big = float(LANES)
    gl = jnp.where(lane < N_GROUPS, logits, neg)
    gmax = jnp.max(gl, axis=1, keepdims=True)
    gsel = jnp.min(jnp.where(gl == gmax, lane, big), axis=1, keepdims=True)
    p_grp = 1.0 / jnp.sum(jnp.exp(gl - gmax), axis=1, keepdims=True)
    lo = ROUTE_LANE0 + EXPERTS_PER_GROUP * gsel
    el = jnp.where((lane >= lo) & (lane < lo + EXPERTS_PER_GROUP), logits, neg)
    m1 = jnp.max(el, axis=1, keepdims=True)
    i1 = jnp.min(jnp.where(el == m1, lane, big), axis=1, keepdims=True)
    el2 = jnp.where(lane == i1, neg, el)
    m2 = jnp.max(el2, axis=1, keepdims=True)
    i2 = jnp.min(jnp.where(el2 == m2, lane, big), axis=1, keepdims=True)
    e2 = jnp.exp(m2 - m1)
    p1 = 1.0 / (1.0 + e2)
    p2 = e2 / (1.0 + e2)
    info = jnp.where(lane == 0.0, i1 - ROUTE_LANE0, 0.0)
    info = jnp.where(lane == 1.0, i2 - ROUTE_LANE0, info)
    info = jnp.where(lane == 2.0, p_grp * p1, info)
    return jnp.where(lane == 3.0, p_grp * p2, info)


def _outproj_kernel(p0, p1, p2, p3, hf_ref, hb_ref, uo_ref, x_ref, wout_ref, mod_ref,
                    hg_ref, n2g_ref, wrh_ref, wrl_ref, br_ref, h1_o, fn_o, info_o):
    h = hf_ref[...] + hb_ref[...]
    parts = []
    for hh in range(HEADS):
        hs = h[:, hh * HEAD_DIM:(hh + 1) * HEAD_DIM]
        mu = jnp.mean(hs, axis=-1, keepdims=True)
        ctr = hs - mu
        var = jnp.mean(ctr * ctr, axis=-1, keepdims=True)
        parts.append(ctr * lax.rsqrt(var + EPS))
    hn = jnp.concatenate(parts, axis=1) * hg_ref[...]
    m = (hn * jax.nn.sigmoid(uo_ref[...])).astype(BF16)
    mix = _dot(m, wout_ref[MIX_HALF:, :])
    for gi, p_ref in enumerate((p0, p1, p2, p3)):
        mix = mix + _dot(p_ref[...], wout_ref[gi * POOL_GROUP:(gi + 1) * POOL_GROUP, :])
    h1 = x_ref[...] + mod_ref[2:3, :] * mix
    h1_o[...] = h1
    ms = jnp.mean(h1 * h1, axis=-1, keepdims=True)
    fn = h1 * lax.rsqrt(ms + EPS) * n2g_ref[...]
    fn = fn * (1.0 + mod_ref[4:5, :]) + mod_ref[3:4, :]
    fh, fl = _split2(fn)
    fn_o[...] = fh
    wrh = wrh_ref[...]
    logits = _dot(fh, wrh) + _dot(fh, wrl_ref[...]) + _dot(fl, wrh) + br_ref[...]
    info_o[...] = _route(logits)


def _outproj(ps, hf, hb, uo, x2d, w_out, mod, head_g, norm2_g, wr_hi, wr_lo, b_route, tm):
    t = x2d.shape[0]
    const = lambda i: (0, 0)
    row = lambda i: (i, 0)
    in_specs = ([pl.BlockSpec((tm, POOL_GROUP), row)] * 4
                + [pl.BlockSpec((tm, MIX_HALF), row)] * 3
                + [pl.BlockSpec((tm, D_MODEL), row),
                   pl.BlockSpec(w_out.shape, const, pipeline_mode=pl.Buffered(1)),
                   pl.BlockSpec(mod.shape, const),
                   pl.BlockSpec((1, MIX_HALF), const),
                   pl.BlockSpec((1, D_MODEL), const),
                   pl.BlockSpec((D_MODEL, LANES), const),
                   pl.BlockSpec((D_MODEL, LANES), const),
                   pl.BlockSpec((1, LANES), const)])
    return pl.pallas_call(
        _outproj_kernel,
        grid=(t // tm,),
        in_specs=in_specs,
        out_specs=[pl.BlockSpec((tm, D_MODEL), row), pl.BlockSpec((tm, D_MODEL), row),
                   pl.BlockSpec((tm, LANES), row)],
        out_shape=[jax.ShapeDtypeStruct((t, D_MODEL), F32),
                   jax.ShapeDtypeStruct((t, D_MODEL), BF16),
                   jax.ShapeDtypeStruct((t, LANES), F32)],
        compiler_params=_cparams(("arbitrary",)),
        name="outproj_route",
    )(*ps, hf, hb, uo, x2d, w_out, mod, head_g, norm2_g, wr_hi, wr_lo, b_route)


DISPATCH_BLOCK = 256
GRANULE = 16
GRANULE_SHIFT = 4
LOCAL_CAP = 1024
LOCAL_GRANULES = LOCAL_CAP // GRANULE
EXPERT_TILE = 128
TILE_GRANULES = EXPERT_TILE // GRANULE


def _dispatch_kernel(fn_ref, info_ref, xs_o, ws_o, pt_o, cnt_o):
    tb = DISPATCH_BLOCK
    info = info_ref[...]
    e1, e2 = info[:, 0:1], info[:, 1:2]
    w1c, w2c = info[:, 2:3], info[:, 3:4]
    lane = lax.broadcasted_iota(jnp.int32, (tb, LANES), 1).astype(F32)
    o1 = jnp.where(lane == e1, 1.0, 0.0)
    o2 = jnp.where(lane == e2, 1.0, 0.0)
    onehot = o1 + o2
    cnt = jnp.sum(onehot, axis=0, keepdims=True)
    gran = jnp.floor((cnt + (GRANULE - 1)) * (1.0 / GRANULE))
    a = lax.broadcasted_iota(jnp.int32, (LANES, LANES), 0)
    b = lax.broadcasted_iota(jnp.int32, (LANES, LANES), 1)
    upper = jnp.where(a < b, 1.0, 0.0).astype(BF16)
    seg_off = _dot(jnp.broadcast_to(gran, (8, LANES)).astype(BF16), upper)[0:1, :] * GRANULE
    r = lax.broadcasted_iota(jnp.int32, (tb, tb), 0)
    c = lax.broadcasted_iota(jnp.int32, (tb, tb), 1)
    strict = jnp.where(r > c, 1.0, 0.0).astype(BF16)
    rank = _dot(strict, onehot.astype(BF16))
    slot = rank + seg_off
    pos1 = jnp.sum(o1 * slot, axis=1, keepdims=True)
    pos2 = jnp.sum(o2 * slot, axis=1, keepdims=True)
    rows = lax.broadcasted_iota(jnp.int32, (tb, LOCAL_CAP), 1).astype(F32)
    pt1 = jnp.where(rows == pos1, 1.0, 0.0)
    pt2 = jnp.where(rows == pos2, 1.0, 0.0)
    pt = pt1 + pt2
    pt_o[...] = pt.astype(BF16)
    perm = pt.T.astype(BF16)
    w_slot = jnp.sum((pt1 * w1c + pt2 * w2c).T, axis=1, keepdims=True)
    xs_o[...] = _dot(perm, fn_ref[...]).astype(BF16)
    ws_o[...] = jnp.broadcast_to(w_slot, (LOCAL_CAP, LANES))
    cnt_o[...] = jnp.broadcast_to(cnt, (8, LANES)).astype(jnp.int32)


def _dispatch(fn, info):
    t = fn.shape[0]
    nb = t // DISPATCH_BLOCK
    row = lambda i: (i, 0)
    return pl.pallas_call(
        _dispatch_kernel,
        grid=(nb,),
        in_specs=[pl.BlockSpec((DISPATCH_BLOCK, D_MODEL), row),
                  pl.BlockSpec((DISPATCH_BLOCK, LANES), row)],
        out_specs=[pl.BlockSpec((LOCAL_CAP, D_MODEL), row),
                   pl.BlockSpec((LOCAL_CAP, LANES), row),
                   pl.BlockSpec((DISPATCH_BLOCK, LOCAL_CAP), row),
                   pl.BlockSpec((8, LANES), row)],
        out_shape=[jax.ShapeDtypeStruct((nb * LOCAL_CAP, D_MODEL), BF16),
                   jax.ShapeDtypeStruct((nb * LOCAL_CAP, LANES), F32),
                   jax.ShapeDtypeStruct((t, LOCAL_CAP), BF16),
                   jax.ShapeDtypeStruct((nb * 8, LANES), jnp.int32)],
        compiler_params=_cparams(("arbitrary",)),
        name="moe_dispatch",
    )(fn, info)


def _plan_kernel(cnt_ref, gmap_o, texp_o, tend_o, ntile_o, lrun, *, nb, max_tiles):
    def init(b, c):
        lrun[b] = 0
        return c

    lax.fori_loop(0, nb, init, 0)

    def per_expert(e, carry):
        g0, last_e = carry

        def per_block(b, g):
            k = (cnt_ref[b, e] + (GRANULE - 1)) >> GRANULE_SHIFT
            lo = lrun[b]
            lrun[b] = lo + k

            def put(j, c):
                gmap_o[g + j] = b * LOCAL_GRANULES + lo + j
                return c

            lax.fori_loop(0, k, put, 0)
            return g + k

        g1 = lax.fori_loop(0, nb, per_block, g0)
        pad = (-g1) & (TILE_GRANULES - 1)

        def put_pad(j, c):
            gmap_o[g1 + j] = -1
            return c

        lax.fori_loop(0, pad, put_pad, 0)
        g2 = g1 + pad

        def put_tile(tt, c):
            texp_o[tt] = e
            return c

        lax.fori_loop(g0 // TILE_GRANULES, g2 // TILE_GRANULES, put_tile, 0)
        tend_o[e] = g2 // TILE_GRANULES
        return g2, jnp.where(g2 > g0, e, last_e)

    g_end, last_e = lax.fori_loop(0, N_EXPERTS, per_expert, (0, 0))
    n_tiles = g_end // TILE_GRANULES
    ntile_o[0] = n_tiles

    def fill(tt, c):
        texp_o[tt] = last_e
        return c

    lax.fori_loop(n_tiles, max_tiles, fill, 0)

    def fill_map(g, c):
        gmap_o[g] = -1
        return c

    lax.fori_loop(g_end, max_tiles * TILE_GRANULES, fill_map, 0)


def _max_tiles(t):
    nb = t // DISPATCH_BLOCK
    worst_rows = 2 * t + nb * N_EXPERTS * (GRANULE - 1) + N_EXPERTS * (EXPERT_TILE - GRANULE)
    return -(-worst_rows // EXPERT_TILE)


def _plan(cnt, t):
    nb = cnt.shape[0]
    max_tiles = _max_tiles(t)
    smem = pl.BlockSpec(memory_space=pltpu.SMEM)
    return pl.pallas_call(
        functools.partial(_plan_kernel, nb=nb, max_tiles=max_tiles),
        in_specs=[smem],
        out_specs=[smem, smem, smem, smem],
        out_shape=[jax.ShapeDtypeStruct((max_tiles * TILE_GRANULES,), jnp.int32),
                   jax.ShapeDtypeStruct((max_tiles,), jnp.int32),
                   jax.ShapeDtypeStruct((N_EXPERTS,), jnp.int32),
                   jax.ShapeDtypeStruct((1,), jnp.int32)],
        scratch_shapes=[pltpu.SMEM((nb,), jnp.int32)],
        name="moe_plan",
    )(cnt)


def _experts_kernel(gmap, texp, tend, ntile, xy_in, ws_in, w1_hbm, w3_hbm, w2_hbm, xy_out,
                    xbuf, wsbuf, ybuf, st1, st3, st2, wb1, wb3, wb2,
                    gsem, ssem, wsem, n_gather, n_scatter):
    nt = ntile[0]

    def weight_copies(e, ws):
        return (pltpu.make_async_copy(w1_hbm.at[e], st1.at[ws], wsem.at[ws]),
                pltpu.make_async_copy(w3_hbm.at[e], st3.at[ws], wsem.at[ws]),
                pltpu.make_async_copy(w2_hbm.at[e], st2.at[ws], wsem.at[ws]))

    def rows(i):
        return pl.ds(pl.multiple_of(i * GRANULE, GRANULE), GRANULE)

    def gather_copies(gm, j, sl):
        return (pltpu.make_async_copy(xy_in.at[rows(gm), :], xbuf.at[sl, rows(j), :], gsem.at[sl]),
                pltpu.make_async_copy(ws_in.at[rows(gm), :], wsbuf.at[sl, rows(j), :], gsem.at[sl]))

    def scatter_copies(gm, j, sl):
        return (pltpu.make_async_copy(ybuf.at[sl, rows(j), :], xy_out.at[rows(gm), :], ssem.at[sl]),)

    def issue(tt, sl, copies, counter):
        def body(j, n):
            gm = gmap[tt * TILE_GRANULES + j]

            @pl.when(gm >= 0)
            def _():
                for cp in copies(gm, j, sl):
                    cp.start()

            return n + (gm >= 0).astype(jnp.int32)

        counter[sl] = lax.fori_loop(0, TILE_GRANULES, body, 0)

    def drain(sl, copies, counter):
        def body(j, c):
            for cp in copies(0, 0, sl):
                cp.wait()
            return c

        lax.fori_loop(0, counter[sl], body, 0)

    xbuf[...] = jnp.zeros_like(xbuf)
    wsbuf[...] = jnp.zeros_like(wsbuf)

    @pl.when(nt > 0)
    def _():
        issue(0, 0, gather_copies, n_gather)
        for cp in weight_copies(texp[0], 0):
            cp.start()

    def tile(t, wslot):
        slot = t % 2
        e = texp[t]
        first = (t == 0) | (texp[jnp.maximum(t - 1, 0)] != e)
        wslot = jnp.where(first & (t > 0), 1 - wslot, wslot)

        drain(slot, gather_copies, n_gather)

        @pl.when(t + 1 < nt)
        def _():
            issue(t + 1, 1 - slot, gather_copies, n_gather)

        @pl.when(first)
        def _():
            for cp in weight_copies(e, wslot):
                cp.wait()
            wb1[...] = st1[wslot].astype(BF16)
            wb3[...] = st3[wslot].astype(BF16)
            wb2[...] = st2[wslot].astype(BF16)
            nxt = tend[e]

            @pl.when(nxt < nt)
            def _():
                for cp in weight_copies(texp[jnp.minimum(nxt, nt - 1)], 1 - wslot):
                    cp.start()

        x = xbuf[slot]
        w_row = wsbuf[slot][:, 0:1]
        a = _dot(x, wb1[...])
        b = _dot(x, wb3[...])
        y = _dot((_silu(a) * b).astype(BF16), wb2[...]) * w_row

        @pl.when(t >= 2)
        def _():
            drain(slot, scatter_copies, n_scatter)

        ybuf[slot] = y.astype(BF16)
        issue(t, slot, scatter_copies, n_scatter)
        return wslot

    lax.fori_loop(0, nt, tile, 0)

    @pl.when(nt >= 1)
    def _():
        drain((nt - 1) % 2, scatter_copies, n_scatter)

    @pl.when(nt >= 2)
    def _():
        drain(nt % 2, scatter_copies, n_scatter)


def _experts(gmap, texp, tend, ntile, xy, ws, w1, w3, w2):
    smem = pl.BlockSpec(memory_space=pltpu.SMEM)
    hbm = pl.BlockSpec(memory_space=pl.ANY)
    return pl.pallas_call(
        _experts_kernel,
        in_specs=[smem, smem, smem, smem, hbm, hbm, hbm, hbm, hbm],
        out_specs=hbm,
        out_shape=jax.ShapeDtypeStruct(xy.shape, xy.dtype),
        scratch_shapes=[pltpu.VMEM((2, EXPERT_TILE, D_MODEL), BF16),
                        pltpu.VMEM((2, EXPERT_TILE, LANES), F32),
                        pltpu.VMEM((2, EXPERT_TILE, D_MODEL), BF16),
                        pltpu.VMEM((2, D_MODEL, D_EXPERT), F32),
                        pltpu.VMEM((2, D_MODEL, D_EXPERT), F32),
                        pltpu.VMEM((2, D_EXPERT, D_MODEL), F32),
                        pltpu.VMEM((D_MODEL, D_EXPERT), BF16),
                        pltpu.VMEM((D_MODEL, D_EXPERT), BF16),
                        pltpu.VMEM((D_EXPERT, D_MODEL), BF16),
                        pltpu.SemaphoreType.DMA((2,)),
                        pltpu.SemaphoreType.DMA((2,)),
                        pltpu.SemaphoreType.DMA((2,)),
                        pltpu.SMEM((2,), jnp.int32),
                        pltpu.SMEM((2,), jnp.int32)],
        input_output_aliases={4: 0},
        compiler_params=pltpu.CompilerParams(vmem_limit_bytes=VMEM_LIMIT),
        name="moe_experts",
    )(gmap, texp, tend, ntile, xy, ws, w1, w3, w2)


def _combine_kernel(y_ref, pt_ref, h1_ref, g2_ref, fg_ref, o_ref):
    moe = _dot(pt_ref[...], y_ref[...])
    h = h1_ref[...] + g2_ref[...] * moe
    ms = jnp.mean(h * h, axis=-1, keepdims=True)
    o_ref[...] = h * lax.rsqrt(ms + EPS) * fg_ref[...]


def _combine(xy, pt, h1, g2, final_g):
    t = h1.shape[0]
    row = lambda i: (i, 0)
    const = lambda i: (0, 0)
    return pl.pallas_call(
        _combine_kernel,
        grid=(t // DISPATCH_BLOCK,),
        in_specs=[pl.BlockSpec((LOCAL_CAP, D_MODEL), row),
                  pl.BlockSpec((DISPATCH_BLOCK, LOCAL_CAP), row),
                  pl.BlockSpec((DISPATCH_BLOCK, D_MODEL), row),
                  pl.BlockSpec((1, D_MODEL), const), pl.BlockSpec((1, D_MODEL), const)],
        out_specs=pl.BlockSpec((DISPATCH_BLOCK, D_MODEL), row),
        out_shape=jax.ShapeDtypeStruct((t, D_MODEL), F32),
        compiler_params=_cparams(("arbitrary",)),
        name="moe_combine_final",
    )(xy, pt, h1, g2, final_g)


def _pad_lanes(a):
    return jnp.pad(a, ((0, 0), (0, LANES - a.shape[1])))


def kernel(x, c, ctx, c_ctx, w_mod, b_mod, norm1_g, w_in, w_conv_q, w_conv_k, gate_bias, head_norm_g, w_pool, pool_scale, w_out, norm2_g, w_group, b_group, w_router, b_router, w1, w3, w2, final_g):
    assert x.shape[0] == 1 and w_mod.shape[0] == 1
    seq = x.shape[1]
    x2d = x[0]
    ctx2d = ctx[0]

    n_main = 5 * MIX_HALF
    w_main = w_in[0][:, :n_main].astype(BF16)
    wg_hi, wg_lo = _split2(_pad_lanes(w_in[0][:, n_main:]))
    gate_bias_row = _pad_lanes(gate_bias[0].reshape(1, N_GATES))
    w_out_bf = w_out[0].astype(BF16)
    wr_hi, wr_lo = _split2(_pad_lanes(jnp.concatenate([w_group[0], w_router[0]], axis=1)))
    b_route = _pad_lanes(jnp.concatenate([b_group[0], b_router[0]]).reshape(1, -1))
    norm1 = norm1_g[0].reshape(1, D_MODEL)

    c16 = jnp.zeros((16, D_MODEL), F32).at[0].set(c[0]).at[1].set(c_ctx)
    mods = _adaln(c16, w_mod[0], b_mod[0].reshape(1, -1))
    mod_lat = mods[0].reshape(6, D_MODEL)
    mod_ctx = mods[1].reshape(6, D_MODEL)

    _, uq_c, _, uk_c, v_c, gc_c, gr_c = _inproj(ctx2d, mod_ctx[0:2], norm1, w_main, wg_hi, wg_lo,
                                                gate_bias_row, tm=256)
    _, k_c = _conv_qk(uq_c, uk_c, w_conv_q[0], w_conv_k[0], tm=256)
    zeros_state = (jnp.zeros((2 * HEADS, HEAD_DIM, HEAD_DIM), F32),
                   jnp.zeros((2 * HEADS, 1, HEAD_DIM), F32),
                   jnp.zeros((2 * HEADS, 1, LANES), F32))
    s0, n0, m0 = _mlstm(None, k_c, v_c, gc_c, gr_c, *zeros_state, need_out=False)

    u_pool, uq, uo, uk, v, gc, gr = _inproj(x2d, mod_lat[0:2], norm1, w_main, wg_hi, wg_lo,
                                            gate_bias_row, tm=256)
    q, k = _conv_qk(uq, uk, w_conv_q[0], w_conv_k[0], tm=512)
    hf, hb = _mlstm(q, k, v, gc, gr, s0, n0, m0, need_out=True)
    ps = [_pool(u_pool, w_pool[0, gi], pool_scale[0].reshape(1, -1), gi, win)
          for gi, win in enumerate(POOL_WINDOWS)]
    h1, fn, info = _outproj(ps, hf, hb, uo, x2d, w_out_bf, mod_lat, head_norm_g[0].reshape(1, -1),
                            norm2_g[0].reshape(1, -1), wr_hi, wr_lo, b_route, tm=256)

    xs, ws, pt, cnt = _dispatch(fn, info)
    gmap, texp, tend, ntile = _plan(cnt[::8, :N_EXPERTS], seq)
    xy = _experts(gmap, texp, tend, ntile, xs, ws, w1[0], w3[0], w2[0])
    out = _combine(xy, pt, h1, mod_lat[5:6], final_g.reshape(1, -1))
    return out.reshape(1, seq, D_MODEL)
```

```python
import functools

import jax
import jax.numpy as jnp
from jax import lax
from jax.experimental import pallas as pl
from jax.experimental.pallas import tpu as pltpu

F32 = jnp.float32
BF16 = jnp.bfloat16

D_MODEL = 2048
GRID_W = 64
GRID_SHIFT = 6
POOL_WINDOWS = (2, 4, 8, 16)
POOL_GROUP = 256
HEADS = 4
HEAD_DIM = 256
MIX_HALF = 1024
N_GATES = 16
N_GROUPS = 4
EXPERTS_PER_GROUP = 8
N_EXPERTS = 32
D_EXPERT = 512
EPS = 1e-6
LANES = 128
CHUNK = 256
ROUTE_LANE0 = N_GROUPS

VMEM_LIMIT = 56 * 1024 * 1024


def _cparams(sem, vmem=VMEM_LIMIT):
    return pltpu.CompilerParams(dimension_semantics=sem, vmem_limit_bytes=vmem)


def _split2(x):
    hi = x.astype(BF16)
    lo = (x - hi.astype(F32)).astype(BF16)
    return hi, lo


def _split3(x):
    hi = x.astype(BF16)
    r = x - hi.astype(F32)
    mid = r.astype(BF16)
    lo = (r - mid.astype(F32)).astype(BF16)
    return hi, mid, lo


def _dot(a, b):
    return jnp.dot(a, b, preferred_element_type=F32)


SPLIT_LANE = 64


def _split_pack(w):
    hi, lo = _split2(w)
    n = w.shape[1]
    packed = jnp.zeros((w.shape[0], LANES), BF16).at[:, :n].set(hi).at[:, SPLIT_LANE:SPLIT_LANE + n].set(lo)
    return packed, jnp.pad(hi, ((0, 0), (0, LANES - n)))


def _split_dot(xh, xl, w_packed, w_hi):
    r = _dot(xh, w_packed)
    return r + pltpu.roll(r, SPLIT_LANE, 1) + _dot(xl, w_hi)


def _silu(x):
    return x * jax.nn.sigmoid(x)


def _log_sigmoid(x):
    return jnp.minimum(x, 0.0) - jnp.log(1.0 + jnp.exp(-jnp.abs(x)))


def _cast_kernel(w_ref, o_ref):
    o_ref[...] = w_ref[:, :o_ref.shape[1]].astype(BF16)


def _cast_bf16(w, n_cols, rows=256):
    k = w.shape[0]
    return pl.pallas_call(
        _cast_kernel,
        grid=(k // rows,),
        in_specs=[pl.BlockSpec((rows, w.shape[1]), lambda i: (i, 0))],
        out_specs=pl.BlockSpec((rows, n_cols), lambda i: (i, 0)),
        out_shape=jax.ShapeDtypeStruct((k, n_cols), BF16),
        compiler_params=_cparams(("arbitrary",)),
        name="cast_bf16",
    )(w)


def _adaln_kernel(c_ref, w_ref, b_ref, o_ref):
    a = _silu(c_ref[...])
    a3 = jnp.concatenate(_split3(a), axis=0)
    w_hi, w_lo = _split2(w_ref[...])
    acc = _dot(a3, w_hi)
    acc_lo = _dot(a3[:32], w_lo)
    out = acc[0:16] + acc[16:32] + acc[32:48] + acc_lo[0:16] + acc_lo[16:32]
    o_ref[...] = out + b_ref[...]


def _adaln(c16, w_mod, b_mod):
    n = w_mod.shape[1]
    tn = 1536
    return pl.pallas_call(
        _adaln_kernel,
        grid=(n // tn,),
        in_specs=[pl.BlockSpec((16, D_MODEL), lambda j: (0, 0)),
                  pl.BlockSpec((D_MODEL, tn), lambda j: (0, j)),
                  pl.BlockSpec((1, tn), lambda j: (0, j))],
        out_specs=pl.BlockSpec((16, tn), lambda j: (0, j)),
        out_shape=jax.ShapeDtypeStruct((16, n), F32),
        compiler_params=_cparams(("arbitrary",)),
        name="adaln",
    )(c16, w_mod, b_mod)


def _inproj_kernel(x_ref, mod_ref, g_ref, w_ref, wgp_ref, wgh_ref, gb_ref,
                   pool_o, q_o, o_o, k_o, v_o, gc_o, gr_o):
    x = x_ref[...]
    ms = jnp.mean(x * x, axis=-1, keepdims=True)
    y = x * lax.rsqrt(ms + EPS) * g_ref[...]
    xn = y * (1.0 + mod_ref[1:2, :]) + mod_ref[0:1, :]
    xh, xl = _split2(xn)
    for ci, o in enumerate((pool_o, q_o, o_o, k_o, v_o)):
        o[...] = _dot(xh, w_ref[:, ci * MIX_HALF:(ci + 1) * MIX_HALF]).astype(o.dtype)
    gates = _split_dot(xh, xl, wgp_ref[...], wgh_ref[...]) + gb_ref[...]
    gc_o[...] = gates[:, :N_GATES]
    gr_o[...] = gates.T[:N_GATES, :]


def _inproj(x2d, mod, g, w_main, wg_pack, wg_hi, gate_bias_row, tm):
    t = x2d.shape[0]
    nt = t // tm
    const = lambda i: (0, 0)
    row = lambda i: (i, 0)
    f32_out = jax.ShapeDtypeStruct((t, MIX_HALF), F32)
    return pl.pallas_call(
        _inproj_kernel,
        grid=(nt,),
        in_specs=[pl.BlockSpec((tm, D_MODEL), row),
                  pl.BlockSpec((2, D_MODEL), const),
                  pl.BlockSpec((1, D_MODEL), const),
                  pl.BlockSpec(w_main.shape, const, pipeline_mode=pl.Buffered(1)),
                  pl.BlockSpec((D_MODEL, LANES), const),
                  pl.BlockSpec((D_MODEL, LANES), const),
                  pl.BlockSpec((1, LANES), const)],
        out_specs=[pl.BlockSpec((tm, MIX_HALF), row)] * 5
        + [pl.BlockSpec((tm, N_GATES), row), pl.BlockSpec((N_GATES, tm), lambda i: (0, i))],
        out_shape=[f32_out, f32_out, f32_out, f32_out,
                   jax.ShapeDtypeStruct((t, MIX_HALF), BF16),
                   jax.ShapeDtypeStruct((t, N_GATES), F32),
                   jax.ShapeDtypeStruct((N_GATES, t), F32)],
        compiler_params=_cparams(("arbitrary",)),
        name="inproj",
    )(x2d, mod, g, w_main, wg_pack, wg_hi, gate_bias_row)


def _conv_kernel(q_ref, qp_ref, qn_ref, k_ref, kp_ref, kn_ref, wq_ref, wk_ref, qo, ko, *, nt):
    i = pl.program_id(0)
    has_prev = (i > 0).astype(F32)
    has_next = (i < nt - 1).astype(F32)

    def conv(x_ref, p_ref, n_ref, w_ref):
        x = x_ref[...]
        tm = x.shape[0]
        row = lax.broadcasted_iota(jnp.int32, x.shape, 0)
        prev_row = p_ref[7:8, :] * has_prev
        next_row = n_ref[0:1, :] * has_next
        xp = jnp.where(row == 0, prev_row, pltpu.roll(x, 1, 0))
        xn = jnp.where(row == tm - 1, next_row, pltpu.roll(x, tm - 1, 0))
        y = w_ref[0:1, :] * xp + w_ref[1:2, :] * x + w_ref[2:3, :] * xn
        return _silu(y)

    qo[...] = (conv(q_ref, qp_ref, qn_ref, wq_ref) * (HEAD_DIM ** -0.5)).astype(BF16)
    ko[...] = conv(k_ref, kp_ref, kn_ref, wk_ref).astype(BF16)


def _conv_qk(uq, uk, wq, wk, tm):
    t = uq.shape[0]
    nt = t // tm
    r8 = tm // 8
    last8 = t // 8 - 1
    main = pl.BlockSpec((tm, MIX_HALF), lambda i: (i, 0))
    prev = pl.BlockSpec((8, MIX_HALF), lambda i: (jnp.maximum(i * r8 - 1, 0), 0))
    nxt = pl.BlockSpec((8, MIX_HALF), lambda i: (jnp.minimum((i + 1) * r8, last8), 0))
    wspec = pl.BlockSpec((3, MIX_HALF), lambda i: (0, 0))
    out = jax.ShapeDtypeStruct((t, MIX_HALF), BF16)
    return pl.pallas_call(
        functools.partial(_conv_kernel, nt=nt),
        grid=(nt,),
        in_specs=[main, prev, nxt, main, prev, nxt, wspec, wspec],
        out_specs=[main, main],
        out_shape=[out, out],
        compiler_params=_cparams(("arbitrary",)),
        name="conv_qk",
    )(uq, uq, uq, uk, uk, uk, wq, wk)


def _mlstm_kernel(*refs, need_out):
    if need_out:
        (qf, kf, vf, gcf, grf, qb, kb, vb, gcb, grb, s0, n0, m0,
         hf_o, hb_o, s_s, n_s, m_s) = refs
        q_refs, h_outs = (qf, qb), (hf_o, hb_o)
    else:
        (kf, vf, gcf, grf, kb, vb, gcb, grb, s0, n0, m0,
         s_o, n_o, m_o, s_s, n_s, m_s) = refs
    k_refs, v_refs, gc_refs, gr_refs = (kf, kb), (vf, vb), (gcf, gcb), (grf, grb)
    j = pl.program_id(0)

    @pl.when(j == 0)
    def _():
        s_s[...] = s0[...]
        n_s[...] = n0[...]
        m_s[...] = m0[...]

    L = CHUNK
    row = lax.broadcasted_iota(jnp.int32, (L, L), 0)
    col = lax.broadcasted_iota(jnp.int32, (L, L), 1)
    tril = row >= col
    triu = row <= col
    tril_bf = jnp.where(tril, 1.0, 0.0).astype(BF16)
    triu_bf = jnp.where(triu, 1.0, 0.0).astype(BF16)

    for d in range(2):
        m_col, m_row, mask = (tril_bf, triu_bf, tril) if d == 0 else (triu_bf, tril_bf, triu)
        gc = gc_refs[d][...]
        gr = gr_refs[d][...]
        lf_c = _log_sigmoid(gc)
        lf_r = _log_sigmoid(gr)
        c_hi, c_mid, c_lo = _split3(lf_c)
        b_c = _dot(m_col, c_hi) + _dot(m_col, c_mid) + _dot(m_col, c_lo)
        r_hi, r_mid, r_lo = _split3(lf_r)
        b_r = _dot(r_hi, m_row) + _dot(r_mid, m_row) + _dot(r_lo, m_row)
        for h in range(HEADS):
            hd = d * HEADS + h
            ci, cf = 8 * d + h, 8 * d + 4 + h
            hs = slice(h * HEAD_DIM, (h + 1) * HEAD_DIM)
            i_c, bc = gc[:, ci:ci + 1], b_c[:, cf:cf + 1]
            i_r, br = gr[ci:ci + 1, :], b_r[cf:cf + 1, :]
            g = jnp.sum(lf_r[cf:cf + 1, :], axis=1, keepdims=True)
            m_old = m_s[hd][:, 0:1]
            s_old = s_s[hd]
            n_old = n_s[hd]
            a_r = g - br + i_r
            a_c = g - bc + i_c
            m_new = jnp.maximum(g + m_old, jnp.max(a_r, axis=1, keepdims=True))
            decay = jnp.exp(g + m_old - m_new)
            wk_c = jnp.exp(a_c - m_new)
            k_h = k_refs[d][:, hs]
            v_h = v_refs[d][:, hs]
            kw = k_h.astype(F32) * wk_c
            n_new = decay * n_old + jnp.sum(kw, axis=0, keepdims=True)
            s_new = decay * s_old + _dot(kw.T.astype(BF16), v_h)
            if need_out:
                q_h = q_refs[d][:, hs]
                dm = jnp.where(mask, bc - br + i_r, -jnp.inf)
                inter = bc + m_old
                m_t = jnp.maximum(inter, jnp.max(dm, axis=1, keepdims=True))
                w_inter = jnp.exp(inter - m_t)
                qk = lax.dot_general(q_h, k_h, (((1,), (1,)), ((), ())),
                                     preferred_element_type=F32)
                p = jnp.exp(dm - m_t) * qk
                num = w_inter * _dot(q_h, s_old.astype(BF16)) + _dot(p.astype(BF16), v_h)
                den = (w_inter * jnp.sum(q_h.astype(F32) * n_old, axis=1, keepdims=True)
                       + jnp.sum(p, axis=1, keepdims=True))
                h_outs[d][:, hs] = num / jnp.maximum(jnp.abs(den), jnp.exp(-m_t))
            s_s[hd] = s_new
            n_s[hd] = n_new
            m_s[hd] = jnp.broadcast_to(m_new, (1, LANES))

    if not need_out:
        s_o[...] = s_s[...]
        n_o[...] = n_s[...]
        m_o[...] = m_s[...]


def _mlstm(q, k, v, gc, gr, s0, n0, m0, need_out):
    t = k.shape[0]
    nc = t // CHUNK
    fwd = lambda j: (j, 0)
    bwd = lambda j: (nc - 1 - j, 0)
    fwd_t = lambda j: (0, j)
    bwd_t = lambda j: (0, nc - 1 - j)
    c3 = lambda j: (0, 0, 0)
    seq = lambda im: pl.BlockSpec((CHUNK, MIX_HALF), im)
    state_specs = [pl.BlockSpec(s0.shape, c3), pl.BlockSpec(n0.shape, c3), pl.BlockSpec(m0.shape, c3)]
    scratch = [pltpu.VMEM(s0.shape, F32), pltpu.VMEM(n0.shape, F32), pltpu.VMEM(m0.shape, F32)]

    def side(im, im_t):
        specs = ([seq(im)] if need_out else []) + [seq(im), seq(im)]
        return specs + [pl.BlockSpec((CHUNK, N_GATES), im), pl.BlockSpec((N_GATES, CHUNK), im_t)]

    in_specs = side(fwd, fwd_t) + side(bwd, bwd_t) + state_specs
    seq_in = ((q,) if need_out else ()) + (k, v, gc, gr)
    args = seq_in + seq_in + (s0, n0, m0)
    if need_out:
        out_specs = [seq(fwd), seq(bwd)]
        out_shape = [jax.ShapeDtypeStruct((t, MIX_HALF), F32)] * 2
    else:
        out_specs = state_specs
        out_shape = [jax.ShapeDtypeStruct(a.shape, F32) for a in (s0, n0, m0)]
    return pl.pallas_call(
        functools.partial(_mlstm_kernel, need_out=need_out),
        grid=(nc,),
        in_specs=in_specs,
        out_specs=out_specs,
        out_shape=out_shape,
        scratch_shapes=scratch,
        compiler_params=_cparams(("arbitrary",)),
        name="mlstm_out" if need_out else "mlstm_state",
    )(*args)


POOL_PAD = 512


def _pool_kernel(u_ref, w_ref, sc_ref, o_ref, pad_s, *, t):
    for gi, win in enumerate(POOL_WINDOWS):
        @pl.when(pl.program_id(0) == gi)
        def _():
            _pool_group(u_ref, w_ref, sc_ref, o_ref, pad_s, win=win, t=t)


def _pool_group(u_ref, w_ref, sc_ref, o_ref, pad_s, *, win, t):
    half = win // 2
    tile = 256
    zeros = jnp.zeros((POOL_PAD, POOL_GROUP), F32)
    pad_s[0:POOL_PAD, :] = zeros
    pad_s[POOL_PAD + t:POOL_PAD + t + POOL_PAD, :] = zeros

    def copy(r, carry):
        t0 = pl.multiple_of(r * tile, tile)
        pad_s[pl.ds(POOL_PAD + t0, tile), :] = u_ref[pl.ds(t0, tile), :]
        return carry

    lax.fori_loop(0, t // tile, copy, 0)

    row = lax.broadcasted_iota(jnp.int32, (tile, tile), 0)
    col = lax.broadcasted_iota(jnp.int32, (tile, tile), 1)
    same_row = (row >> GRID_SHIFT) == (col >> GRID_SHIFT)
    in_win = (col - row >= -half) & (col - row < half)
    band = jnp.where(same_row & in_win, 1.0, 0.0).astype(BF16)
    w = w_ref[...].astype(BF16)
    scale = sc_ref[...]
    n_rows = t // GRID_W

    def body(r, carry):
        t0 = pl.multiple_of(r * tile, tile)
        acc = jnp.zeros((tile, POOL_GROUP), F32)
        for dd in range(-half, half):
            acc = acc + pad_s[pl.ds(POOL_PAD + t0 + GRID_W * dd, tile), :]
        tok = t0 + lax.broadcasted_iota(jnp.int32, (tile, POOL_GROUP), 0)
        gr = tok >> GRID_SHIFT
        gc = tok & (GRID_W - 1)
        cnt_v = jnp.minimum(gr + half, n_rows) - jnp.maximum(gr - half, 0)
        cnt_h = jnp.minimum(gc + half, GRID_W) - jnp.maximum(gc - half, 0)
        mean_v = acc / cnt_v.astype(F32)
        hi, lo = _split2(mean_v)
        mean = (_dot(band, hi) + _dot(band, lo)) / cnt_h.astype(F32)
        x = pad_s[pl.ds(POOL_PAD + t0, tile), :]
        y = _dot((mean - x).astype(BF16), w) * scale
        o_ref[pl.ds(t0, tile), :] = y.astype(BF16)
        return carry

    lax.fori_loop(0, t // tile, body, 0)


def _pool(u_pool, w_pool, scale_row):
    t = u_pool.shape[0]
    return pl.pallas_call(
        functools.partial(_pool_kernel, t=t),
        grid=(len(POOL_WINDOWS),),
        in_specs=[pl.BlockSpec((t, POOL_GROUP), lambda g: (0, g)),
                  pl.BlockSpec((None, POOL_GROUP, POOL_GROUP), lambda g: (g, 0, 0)),
                  pl.BlockSpec((1, POOL_GROUP), lambda g: (0, g))],
        out_specs=pl.BlockSpec((t, POOL_GROUP), lambda g: (0, g)),
        out_shape=jax.ShapeDtypeStruct((t, MIX_HALF), BF16),
        scratch_shapes=[pltpu.VMEM((t + 2 * POOL_PAD, POOL_GROUP), F32)],
        compiler_params=_cparams(("arbitrary",)),
        name="pool_mix",
    )(u_pool, w_pool, scale_row)


def _route(logits):
    lane = lax.broadcasted_iota(jnp.int32, logits.shape, 1).astype(F32)
    neg = -jnp.inf
    big = float(LANES)
    gl = jnp.where(lane < N_GROUPS, logits, neg)
    gmax = jnp.max(gl, axis=1, keepdims=True)
    gsel = jnp.min(jnp.where(gl == gmax, lane, big), axis=1, keepdims=True)
    p_grp = 1.0 / jnp.sum(jnp.exp(gl - gmax), axis=1, keepdims=True)
    lo = ROUTE_LANE0 + EXPERTS_PER_GROUP * gsel
    el = jnp.where((lane >= lo) & (lane < lo + EXPERTS_PER_GROUP), logits, neg)
    m1 = jnp.max(el, axis=1, keepdims=True)
    i1 = jnp.min(jnp.where(el == m1, lane, big), axis=1, keepdims=True)
    el2 = jnp.where(lane == i1, neg, el)
    m2 = jnp.max(el2, axis=1, keepdims=True)
    i2 = jnp.min(jnp.where(el2 == m2, lane, big), axis=1, keepdims=True)
    e2 = jnp.exp(m2 - m1)
    p1 = 1.0 / (1.0 + e2)
    p2 = e2 / (1.0 + e2)
    info = jnp.where(lane == 0.0, i1 - ROUTE_LANE0, 0.0)
    info = jnp.where(lane == 1.0, i2 - ROUTE_LANE0, info)
    info = jnp.where(lane == 2.0, p_grp * p1, info)
    return jnp.where(lane == 3.0, p_grp * p2, info)


def _outproj_kernel(p_ref, hf_ref, hb_ref, uo_ref, x_ref, wout_ref, mod_ref,
                    hg_ref, n2g_ref, wrp_ref, wrh_ref, br_ref, h1_o, fn_o, info_o):
    h = hf_ref[...] + hb_ref[...]
    parts = []
    for hh in range(HEADS):
        hs = h[:, hh * HEAD_DIM:(hh + 1) * HEAD_DIM]
        mu = jnp.mean(hs, axis=-1, keepdims=True)
        ctr = hs - mu
        var = jnp.mean(ctr * ctr, axis=-1, keepdims=True)
        parts.append(ctr * lax.rsqrt(var + EPS))
    hn = jnp.concatenate(parts, axis=1) * hg_ref[...]
    m = (hn * jax.nn.sigmoid(uo_ref[...])).astype(BF16)
    mix = _dot(jnp.concatenate([p_ref[...], m], axis=1), wout_ref[...])
    h1 = x_ref[...] + mod_ref[2:3, :] * mix
    h1_o[...] = h1
    ms = jnp.mean(h1 * h1, axis=-1, keepdims=True)
    fn = h1 * lax.rsqrt(ms + EPS) * n2g_ref[...]
    fn = fn * (1.0 + mod_ref[4:5, :]) + mod_ref[3:4, :]
    fh, fl = _split2(fn)
    fn_o[...] = fh
    logits = _split_dot(fh, fl, wrp_ref[...], wrh_ref[...]) + br_ref[...]
    info_o[...] = _route(logits)


def _outproj(p, hf, hb, uo, x2d, w_out, mod, head_g, norm2_g, wr_pack, wr_hi, b_route, tm):
    t = x2d.shape[0]
    const = lambda i: (0, 0)
    row = lambda i: (i, 0)
    in_specs = ([pl.BlockSpec((tm, MIX_HALF), row)] * 4
                + [pl.BlockSpec((tm, D_MODEL), row),
                   pl.BlockSpec(w_out.shape, const, pipeline_mode=pl.Buffered(1)),
                   pl.BlockSpec(mod.shape, const),
                   pl.BlockSpec((1, MIX_HALF), const),
                   pl.BlockSpec((1, D_MODEL), const),
                   pl.BlockSpec((D_MODEL, LANES), const),
                   pl.BlockSpec((D_MODEL, LANES), const),
                   pl.BlockSpec((1, LANES), const)])
    return pl.pallas_call(
        _outproj_kernel,
        grid=(t // tm,),
        in_specs=in_specs,
        out_specs=[pl.BlockSpec((tm, D_MODEL), row), pl.BlockSpec((tm, D_MODEL), row),
                   pl.BlockSpec((tm, LANES), row)],
        out_shape=[jax.ShapeDtypeStruct((t, D_MODEL), F32),
                   jax.ShapeDtypeStruct((t, D_MODEL), BF16),
                   jax.ShapeDtypeStruct((t, LANES), F32)],
        compiler_params=_cparams(("arbitrary",)),
        name="outproj_route",
    )(p, hf, hb, uo, x2d, w_out, mod, head_g, norm2_g, wr_pack, wr_hi, b_route)


DISPATCH_BLOCK = 256
GRANULE = 16
GRANULE_SHIFT = 4
LOCAL_CAP = 1024
LOCAL_GRANULES = LOCAL_CAP // GRANULE
EXPERT_TILE = 128
TILE_GRANULES = EXPERT_TILE // GRANULE
PLAN_UNROLL = 4
PLAN_SLACK = 8
GATHER_DEPTH = 3


def _dispatch_kernel(fn_ref, info_ref, xs_o, ws_o, pt_o, cnt_o):
    tb = DISPATCH_BLOCK
    info = info_ref[...]
    e1, e2 = info[:, 0:1], info[:, 1:2]
    w1c, w2c = info[:, 2:3], info[:, 3:4]
    lane = lax.broadcasted_iota(jnp.int32, (tb, LANES), 1).astype(F32)
    o1 = jnp.where(lane == e1, 1.0, 0.0)
    o2 = jnp.where(lane == e2, 1.0, 0.0)
    onehot = o1 + o2
    cnt = jnp.sum(onehot, axis=0, keepdims=True)
    gran = jnp.floor((cnt + (GRANULE - 1)) * (1.0 / GRANULE))
    a = lax.broadcasted_iota(jnp.int32, (LANES, LANES), 0)
    b = lax.broadcasted_iota(jnp.int32, (LANES, LANES), 1)
    upper = jnp.where(a < b, 1.0, 0.0).astype(BF16)
    seg_off = _dot(jnp.broadcast_to(gran, (8, LANES)).astype(BF16), upper)[0:1, :] * GRANULE
    r = lax.broadcasted_iota(jnp.int32, (tb, tb), 0)
    c = lax.broadcasted_iota(jnp.int32, (tb, tb), 1)
    strict = jnp.where(r > c, 1.0, 0.0).astype(BF16)
    rank = _dot(strict, onehot.astype(BF16))
    slot = rank + seg_off
    pos1 = jnp.sum(o1 * slot, axis=1, keepdims=True)
    pos2 = jnp.sum(o2 * slot, axis=1, keepdims=True)
    rows = lax.broadcasted_iota(jnp.int32, (tb, LOCAL_CAP), 1).astype(F32)
    pt1 = jnp.where(rows == pos1, 1.0, 0.0)
    pt2 = jnp.where(rows == pos2, 1.0, 0.0)
    pt = pt1 + pt2
    pt_o[...] = pt.astype(BF16)
    perm = pt.T.astype(BF16)
    w_slot = jnp.sum((pt1 * w1c + pt2 * w2c).T, axis=1, keepdims=True)
    xs_o[...] = _dot(perm, fn_ref[...]).astype(BF16)
    ws_o[...] = jnp.broadcast_to(w_slot, (LOCAL_CAP, LANES))
    cnt_o[...] = jnp.broadcast_to(cnt, (8, LANES)).astype(jnp.int32)


def _dispatch(fn, info):
    t = fn.shape[0]
    nb = t // DISPATCH_BLOCK
    row = lambda i: (i, 0)
    return pl.pallas_call(
        _dispatch_kernel,
        grid=(nb,),
        in_specs=[pl.BlockSpec((DISPATCH_BLOCK, D_MODEL), row),
                  pl.BlockSpec((DISPATCH_BLOCK, LANES), row)],
        out_specs=[pl.BlockSpec((LOCAL_CAP, D_MODEL), row),
                   pl.BlockSpec((LOCAL_CAP, LANES), row),
                   pl.BlockSpec((DISPATCH_BLOCK, LOCAL_CAP), row),
                   pl.BlockSpec((8, LANES), row)],
        out_shape=[jax.ShapeDtypeStruct((nb * LOCAL_CAP, D_MODEL), BF16),
                   jax.ShapeDtypeStruct((nb * LOCAL_CAP, LANES), F32),
                   jax.ShapeDtypeStruct((t, LOCAL_CAP), BF16),
                   jax.ShapeDtypeStruct((nb * 8, LANES), jnp.int32)],
        compiler_params=_cparams(("arbitrary",)),
        name="moe_dispatch",
    )(fn, info)


def _plan_kernel(cnt_ref, gmap_o, texp_o, tend_o, ntile_o, lrun, *, nb, max_tiles):
    def init(b, c):
        lrun[b] = 0
        return c

    lax.fori_loop(0, nb, init, 0)

    def per_expert(e, carry):
        g0, last_e = carry

        def per_block(b, g):
            k = (cnt_ref[b, e] + (GRANULE - 1)) >> GRANULE_SHIFT
            lo = lrun[b]
            lrun[b] = lo + k

            base = b * LOCAL_GRANULES + lo

            for j in range(PLAN_UNROLL):
                gmap_o[g + j] = base + j

            @pl.when(k > PLAN_UNROLL)
            def _():
                def put(j, c):
                    gmap_o[g + j] = base + j
                    return c

                lax.fori_loop(PLAN_UNROLL, k, put, 0)

            return g + k

        g1 = lax.fori_loop(0, nb, per_block, g0)
        pad = (-g1) & (TILE_GRANULES - 1)

        def put_pad(j, c):
            gmap_o[g1 + j] = -1
            return c

        lax.fori_loop(0, pad, put_pad, 0)
        g2 = g1 + pad

        def put_tile(tt, c):
            texp_o[tt] = e
            return c

        lax.fori_loop(g0 // TILE_GRANULES, g2 // TILE_GRANULES, put_tile, 0)
        tend_o[e] = g2 // TILE_GRANULES
        return g2, jnp.where(g2 > g0, e, last_e)

    g_end, last_e = lax.fori_loop(0, N_EXPERTS, per_expert, (0, 0))
    n_tiles = g_end // TILE_GRANULES
    ntile_o[0] = n_tiles

    def fill(tt, c):
        texp_o[tt] = last_e
        return c

    lax.fori_loop(n_tiles, max_tiles, fill, 0)

    def fill_map(g, c):
        gmap_o[g] = -1
        return c

    lax.fori_loop(g_end, max_tiles * TILE_GRANULES + PLAN_SLACK, fill_map, 0)


def _max_tiles(t):
    nb = t // DISPATCH_BLOCK
    worst_rows = 2 * t + nb * N_EXPERTS * (GRANULE - 1) + N_EXPERTS * (EXPERT_TILE - GRANULE)
    return -(-worst_rows // EXPERT_TILE)


def _plan(cnt, t):
    nb = cnt.shape[0]
    max_tiles = _max_tiles(t)
    smem = pl.BlockSpec(memory_space=pltpu.SMEM)
    return pl.pallas_call(
        functools.partial(_plan_kernel, nb=nb, max_tiles=max_tiles),
        in_specs=[smem],
        out_specs=[smem, smem, smem, smem],
        out_shape=[jax.ShapeDtypeStruct((max_tiles * TILE_GRANULES + PLAN_SLACK,), jnp.int32),
                   jax.ShapeDtypeStruct((max_tiles,), jnp.int32),
                   jax.ShapeDtypeStruct((N_EXPERTS,), jnp.int32),
                   jax.ShapeDtypeStruct((1,), jnp.int32)],
        scratch_shapes=[pltpu.SMEM((nb,), jnp.int32)],
        name="moe_plan",
    )(cnt)


def _experts_kernel(gmap, texp, tend, ntile, xy_in, ws_in, w1_hbm, w3_hbm, w2_hbm, xy_out,
                    xbuf, wsbuf, ybuf, st1, st3, st2, wb1, wb3, wb2,
                    gsem, ssem, wsem, n_gather, n_scatter):
    nt = ntile[0]

    def weight_copies(e, ws):
        return (pltpu.make_async_copy(w1_hbm.at[e], st1.at[ws], wsem.at[ws]),
                pltpu.make_async_copy(w3_hbm.at[e], st3.at[ws], wsem.at[ws]),
                pltpu.make_async_copy(w2_hbm.at[e], st2.at[ws], wsem.at[ws]))

    def rows(i):
        return pl.ds(pl.multiple_of(i * GRANULE, GRANULE), GRANULE)

    def gather_copies(gm, j, sl):
        return (pltpu.make_async_copy(xy_in.at[rows(gm), :], xbuf.at[sl, rows(j), :], gsem.at[sl]),
                pltpu.make_async_copy(ws_in.at[rows(gm), :], wsbuf.at[sl, rows(j), :], gsem.at[sl]))

    def scatter_copies(gm, j, sl):
        return (pltpu.make_async_copy(ybuf.at[sl, rows(j), :], xy_out.at[rows(gm), :], ssem.at[sl]),)

    def issue(tt, sl, copies, counter):
        def body(j, n):
            gm = gmap[tt * TILE_GRANULES + j]

            @pl.when(gm >= 0)
            def _():
                for cp in copies(gm, j, sl):
                    cp.start()

            return n + (gm >= 0).astype(jnp.int32)

        counter[sl] = lax.fori_loop(0, TILE_GRANULES, body, 0)

    def drain(sl, copies, counter):
        def body(j, c):
            for cp in copies(0, 0, sl):
                cp.wait()
            return c

        lax.fori_loop(0, counter[sl], body, 0)

    xbuf[...] = jnp.zeros_like(xbuf)
    wsbuf[...] = jnp.zeros_like(wsbuf)

    @pl.when(nt > 0)
    def _():
        for cp in weight_copies(texp[0], 0):
            cp.start()

    for ahead in range(GATHER_DEPTH - 1):
        @pl.when(ahead < nt)
        def _():
            issue(ahead, ahead, gather_copies, n_gather)

    def tile(t, wslot):
        slot = t % 2
        gslot = t % GATHER_DEPTH
        e = texp[t]
        first = (t == 0) | (texp[jnp.maximum(t - 1, 0)] != e)
        wslot = jnp.where(first & (t > 0), 1 - wslot, wslot)

        drain(gslot, gather_copies, n_gather)

        @pl.when(t + GATHER_DEPTH - 1 < nt)
        def _():
            issue(t + GATHER_DEPTH - 1, (t + GATHER_DEPTH - 1) % GATHER_DEPTH, gather_copies, n_gather)

        @pl.when(first)
        def _():
            for cp in weight_copies(e, wslot):
                cp.wait()
            wb1[...] = st1[wslot].astype(BF16)
            wb3[...] = st3[wslot].astype(BF16)
            wb2[...] = st2[wslot].astype(BF16)
            nxt = tend[e]

            @pl.when(nxt < nt)
            def _():
                for cp in weight_copies(texp[jnp.minimum(nxt, nt - 1)], 1 - wslot):
                    cp.start()

        x = xbuf[gslot]
        w_row = wsbuf[gslot][:, 0:1]
        a = _dot(x, wb1[...])
        b = _dot(x, wb3[...])
        y = _dot((_silu(a) * b).astype(BF16), wb2[...]) * w_row

        @pl.when(t >= 2)
        def _():
            drain(slot, scatter_copies, n_scatter)

        ybuf[slot] = y.astype(BF16)
        issue(t, slot, scatter_copies, n_scatter)
        return wslot

    lax.fori_loop(0, nt, tile, 0)

    @pl.when(nt >= 1)
    def _():
        drain((nt - 1) % 2, scatter_copies, n_scatter)

    @pl.when(nt >= 2)
    def _():
        drain(nt % 2, scatter_copies, n_scatter)


def _experts(gmap, texp, tend, ntile, xy, ws, w1, w3, w2):
    smem = pl.BlockSpec(memory_space=pltpu.SMEM)
    hbm = pl.BlockSpec(memory_space=pl.ANY)
    return pl.pallas_call(
        _experts_kernel,
        in_specs=[smem, smem, smem, smem, hbm, hbm, hbm, hbm, hbm],
        out_specs=hbm,
        out_shape=jax.ShapeDtypeStruct(xy.shape, xy.dtype),
        scratch_shapes=[pltpu.VMEM((GATHER_DEPTH, EXPERT_TILE, D_MODEL), BF16),
                        pltpu.VMEM((GATHER_DEPTH, EXPERT_TILE, LANES), F32),
                        pltpu.VMEM((2, EXPERT_TILE, D_MODEL), BF16),
                        pltpu.VMEM((2, D_MODEL, D_EXPERT), F32),
                        pltpu.VMEM((2, D_MODEL, D_EXPERT), F32),
                        pltpu.VMEM((2, D_EXPERT, D_MODEL), F32),
                        pltpu.VMEM((D_MODEL, D_EXPERT), BF16),
                        pltpu.VMEM((D_MODEL, D_EXPERT), BF16),
                        pltpu.VMEM((D_EXPERT, D_MODEL), BF16),
                        pltpu.SemaphoreType.DMA((GATHER_DEPTH,)),
                        pltpu.SemaphoreType.DMA((2,)),
                        pltpu.SemaphoreType.DMA((2,)),
                        pltpu.SMEM((GATHER_DEPTH,), jnp.int32),
                        pltpu.SMEM((2,), jnp.int32)],
        input_output_aliases={4: 0},
        compiler_params=pltpu.CompilerParams(vmem_limit_bytes=VMEM_LIMIT),
        name="moe_experts",
    )(gmap, texp, tend, ntile, xy, ws, w1, w3, w2)


def _combine_kernel(y_ref, pt_ref, h1_ref, g2_ref, fg_ref, o_ref):
    moe = _dot(pt_ref[...], y_ref[...])
    h = h1_ref[...] + g2_ref[...] * moe
    ms = jnp.mean(h * h, axis=-1, keepdims=True)
    o_ref[...] = h * lax.rsqrt(ms + EPS) * fg_ref[...]


def _combine(xy, pt, h1, g2, final_g):
    t = h1.shape[0]
    row = lambda i: (i, 0)
    const = lambda i: (0, 0)
    return pl.pallas_call(
        _combine_kernel,
        grid=(t // DISPATCH_BLOCK,),
        in_specs=[pl.BlockSpec((LOCAL_CAP, D_MODEL), row),
                  pl.BlockSpec((DISPATCH_BLOCK, LOCAL_CAP), row),
                  pl.BlockSpec((DISPATCH_BLOCK, D_MODEL), row),
                  pl.BlockSpec((1, D_MODEL), const), pl.BlockSpec((1, D_MODEL), const)],
        out_specs=pl.BlockSpec((DISPATCH_BLOCK, D_MODEL), row),
        out_shape=jax.ShapeDtypeStruct((t, D_MODEL), F32),
        compiler_params=_cparams(("arbitrary",)),
        name="moe_combine_final",
    )(xy, pt, h1, g2, final_g)


def _pad_lanes(a):
    return jnp.pad(a, ((0, 0), (0, LANES - a.shape[1])))


def kernel(x, c, ctx, c_ctx, w_mod, b_mod, norm1_g, w_in, w_conv_q, w_conv_k, gate_bias, head_norm_g, w_pool, pool_scale, w_out, norm2_g, w_group, b_group, w_router, b_router, w1, w3, w2, final_g):
    assert x.shape[0] == 1 and w_mod.shape[0] == 1
    seq = x.shape[1]
    x2d = x[0]
    ctx2d = ctx[0]

    n_main = 5 * MIX_HALF
    w_main = _cast_bf16(w_in[0], n_main)
    wg_pack, wg_hi = _split_pack(w_in[0][:, n_main:])
    gate_bias_row = _pad_lanes(gate_bias[0].reshape(1, N_GATES))
    w_out_bf = _cast_bf16(w_out[0], D_MODEL)
    wr_pack, wr_hi = _split_pack(jnp.concatenate([w_group[0], w_router[0]], axis=1))
    b_route = _pad_lanes(jnp.concatenate([b_group[0], b_router[0]]).reshape(1, -1))
    norm1 = norm1_g[0].reshape(1, D_MODEL)

    c16 = jnp.zeros((16, D_MODEL), F32).at[0].set(c[0]).at[1].set(c_ctx)
    mods = _adaln(c16, w_mod[0], b_mod[0].reshape(1, -1))
    mod_lat = mods[0].reshape(6, D_MODEL)
    mod_ctx = mods[1].reshape(6, D_MODEL)

    _, uq_c, _, uk_c, v_c, gc_c, gr_c = _inproj(ctx2d, mod_ctx[0:2], norm1, w_main, wg_pack, wg_hi,
                                                gate_bias_row, tm=256)
    _, k_c = _conv_qk(uq_c, uk_c, w_conv_q[0], w_conv_k[0], tm=256)
    zeros_state = (jnp.zeros((2 * HEADS, HEAD_DIM, HEAD_DIM), F32),
                   jnp.zeros((2 * HEADS, 1, HEAD_DIM), F32),
                   jnp.zeros((2 * HEADS, 1, LANES), F32))
    s0, n0, m0 = _mlstm(None, k_c, v_c, gc_c, gr_c, *zeros_state, need_out=False)

    u_pool, uq, uo, uk, v, gc, gr = _inproj(x2d, mod_lat[0:2], norm1, w_main, wg_pack, wg_hi,
                                            gate_bias_row, tm=256)
    q, k = _conv_qk(uq, uk, w_conv_q[0], w_conv_k[0], tm=512)
    hf, hb = _mlstm(q, k, v, gc, gr, s0, n0, m0, need_out=True)
    p = _pool(u_pool, w_pool[0], pool_scale[0].reshape(1, -1))
    h1, fn, info = _outproj(p, hf, hb, uo, x2d, w_out_bf, mod_lat, head_norm_g[0].reshape(1, -1),
                            norm2_g[0].reshape(1, -1), wr_pack, wr_hi, b_route, tm=256)

    xs, ws, pt, cnt = _dispatch(fn, info)
    gmap, texp, tend, ntile = _plan(cnt[::8, :N_EXPERTS], seq)
    xy = _experts(gmap, texp, tend, ntile, xs, ws, w1[0], w3[0], w2[0])
    out = _combine(xy, pt, h1, mod_lat[5:6], final_g.reshape(1, -1))
    return out.reshape(1, seq, D_MODEL)
```

```python
import functools

import jax
import jax.numpy as jnp
from jax import lax
from jax.experimental import pallas as pl
from jax.experimental.pallas import tpu as pltpu

F32 = jnp.float32
BF16 = jnp.bfloat16

D_MODEL = 2048
GRID_W = 64
GRID_SHIFT = 6
POOL_WINDOWS = (2, 4, 8, 16)
POOL_GROUP = 256
HEADS = 4
HEAD_DIM = 256
MIX_HALF = 1024
N_GATES = 16
N_GROUPS = 4
EXPERTS_PER_GROUP = 8
N_EXPERTS = 32
D_EXPERT = 512
EPS = 1e-6
LANES = 128
CHUNK = 256
ROUTE_LANE0 = N_GROUPS

VMEM_LIMIT = 56 * 1024 * 1024


def _cparams(sem, vmem=VMEM_LIMIT):
    return pltpu.CompilerParams(dimension_semantics=sem, vmem_limit_bytes=vmem)


def _split2(x):
    hi = x.astype(BF16)
    lo = (x - hi.astype(F32)).astype(BF16)
    return hi, lo


def _split3(x):
    hi = x.astype(BF16)
    r = x - hi.astype(F32)
    mid = r.astype(BF16)
    lo = (r - mid.astype(F32)).astype(BF16)
    return hi, mid, lo


def _dot(a, b):
    return jnp.dot(a, b, preferred_element_type=F32)


SPLIT_LANE = 64


def _split_pack(w):
    hi, lo = _split2(w)
    n = w.shape[1]
    packed = jnp.zeros((w.shape[0], LANES), BF16).at[:, :n].set(hi).at[:, SPLIT_LANE:SPLIT_LANE + n].set(lo)
    return packed, jnp.pad(hi, ((0, 0), (0, LANES - n)))


def _split_dot(xh, xl, w_packed, w_hi):
    r = _dot(xh, w_packed)
    return r + pltpu.roll(r, SPLIT_LANE, 1) + _dot(xl, w_hi)


def _silu(x):
    return x * jax.nn.sigmoid(x)


def _log_sigmoid(x):
    return jnp.minimum(x, 0.0) - jnp.log(1.0 + jnp.exp(-jnp.abs(x)))


def _cast_kernel(w_ref, o_ref):
    o_ref[...] = w_ref[...].astype(BF16)


def _cast_bf16(w, rows=256):
    k, n = w.shape
    return pl.pallas_call(
        _cast_kernel,
        grid=(k // rows,),
        in_specs=[pl.BlockSpec((rows, n), lambda i: (i, 0))],
        out_specs=pl.BlockSpec((rows, n), lambda i: (i, 0)),
        out_shape=jax.ShapeDtypeStruct((k, n), BF16),
        compiler_params=_cparams(("arbitrary",)),
        name="cast_bf16",
    )(w)


def _copy_kernel(w_ref, o_ref):
    o_ref[...] = w_ref[...]


def _take_rows(w, start, n):
    return pl.pallas_call(
        _copy_kernel,
        grid=(1,),
        in_specs=[pl.BlockSpec((n, w.shape[1]), lambda i: (start // n, 0))],
        out_specs=pl.BlockSpec((n, w.shape[1]), lambda i: (0, 0)),
        out_shape=jax.ShapeDtypeStruct((n, w.shape[1]), w.dtype),
        name="take_rows",
    )(w)


def _cast_t_kernel(wt_ref, o_ref):
    o_ref[...] = wt_ref[...].T.astype(BF16)


def _cast_transposed_bf16(wt, n_cols, cols=256):
    k = wt.shape[1]
    return pl.pallas_call(
        _cast_t_kernel,
        grid=(n_cols // cols,),
        in_specs=[pl.BlockSpec((cols, k), lambda i: (i, 0))],
        out_specs=pl.BlockSpec((k, cols), lambda i: (0, i)),
        out_shape=jax.ShapeDtypeStruct((k, n_cols), BF16),
        compiler_params=_cparams(("arbitrary",)),
        name="cast_transposed_bf16",
    )(wt)


def _adaln_kernel(c_ref, w_ref, b_ref, o_ref):
    a = _silu(c_ref[...])
    a3 = jnp.concatenate(_split3(a), axis=0)
    w_hi, w_lo = _split2(w_ref[...])
    acc = _dot(a3, w_hi)
    acc_lo = _dot(a3[:32], w_lo)
    out = acc[0:16] + acc[16:32] + acc[32:48] + acc_lo[0:16] + acc_lo[16:32]
    o_ref[...] = out + b_ref[...]


def _adaln(c16, w_mod, b_mod):
    n = w_mod.shape[1]
    tn = 1536
    return pl.pallas_call(
        _adaln_kernel,
        grid=(n // tn,),
        in_specs=[pl.BlockSpec((16, D_MODEL), lambda j: (0, 0)),
                  pl.BlockSpec((D_MODEL, tn), lambda j: (0, j)),
                  pl.BlockSpec((1, tn), lambda j: (0, j))],
        out_specs=pl.BlockSpec((16, tn), lambda j: (0, j)),
        out_shape=jax.ShapeDtypeStruct((16, n), F32),
        compiler_params=_cparams(("arbitrary",)),
        name="adaln",
    )(c16, w_mod, b_mod)


HALO = 8
NEG_INF = float("-inf")


def _gate_scan_info(gates):
    n = gates.shape[0]
    lane = lax.broadcasted_iota(jnp.int32, gates.shape, 1)
    rows = lax.broadcasted_iota(jnp.int32, gates.shape, 0)
    lf = jnp.where(lane < N_GATES, _log_sigmoid(gates), 0.0)
    hi = lf.astype(BF16).astype(F32)
    rem = lf - hi
    mid = rem.astype(BF16).astype(F32)
    packed = (hi + pltpu.roll(mid, 32, 1) + pltpu.roll(rem - mid, 64, 1)).astype(BF16)
    r = lax.broadcasted_iota(jnp.int32, (n, n), 0)
    c = lax.broadcasted_iota(jnp.int32, (n, n), 1)
    pf = _dot(jnp.where(r >= c, 1.0, 0.0).astype(BF16), packed)
    pb = _dot(jnp.where(r <= c, 1.0, 0.0).astype(BF16), packed)
    bf = pf + pltpu.roll(pf, 96, 1) + pltpu.roll(pf, 64, 1)
    bb = pb + pltpu.roll(pb, 96, 1) + pltpu.roll(pb, 64, 1)
    b = jnp.where(lane < 8, bf, bb)
    cval = gates - pltpu.roll(b, LANES - 4, 1)
    pm = cval
    sm = cval
    step = 1
    while step < n:
        pm = jnp.maximum(pm, jnp.where(rows >= step, pltpu.roll(pm, step, 0), NEG_INF))
        sm = jnp.maximum(sm, jnp.where(rows < n - step, pltpu.roll(sm, n - step, 0), NEG_INF))
        step *= 2
    cm = jnp.where(lane < 8, pm, sm)
    return cval, jnp.where((lane & 4) == 0, cm, b)


def _inproj_kernel(x_ref, xp_ref, xn_ref, mod_ref, g_ref, w_ref, wgp_ref, wgh_ref, gb_ref,
                   wcq_ref, wck_ref, pool_o, q_o, o_o, k_o, kt_o, v_o, col_o, row_o, *, nt):
    i = pl.program_id(0)
    tm = x_ref.shape[0]
    x_all = jnp.concatenate([xp_ref[...], x_ref[...], xn_ref[...]], axis=0)
    ms = jnp.mean(x_all * x_all, axis=-1, keepdims=True)
    y = x_all * lax.rsqrt(ms + EPS) * g_ref[...]
    xn_all = y * (1.0 + mod_ref[1:2, :]) + mod_ref[0:1, :]
    xh_all = xn_all.astype(BF16)
    xh, xl = _split2(xn_all[HALO:HALO + tm])

    def cols(ci):
        return w_ref[:, ci * MIX_HALF:(ci + 1) * MIX_HALF]

    pool_o[...] = _dot(xh, cols(0))
    o_o[...] = _dot(xh, cols(2))
    v_o[...] = _dot(xh, cols(4)).astype(BF16)

    rowi = lax.broadcasted_iota(jnp.int32, (tm, MIX_HALF), 0)
    at_start = jnp.logical_and(rowi == 0, i == 0)
    at_end = jnp.logical_and(rowi == tm - 1, i == nt - 1)

    def conv_silu(ci, wc_ref):
        u = _dot(xh_all, cols(ci))
        n = u.shape[0]
        up = jnp.where(at_start, 0.0, pltpu.roll(u, 1, 0)[HALO:HALO + tm])
        un = jnp.where(at_end, 0.0, pltpu.roll(u, n - 1, 0)[HALO:HALO + tm])
        return _silu(wc_ref[0:1, :] * up + wc_ref[1:2, :] * u[HALO:HALO + tm] + wc_ref[2:3, :] * un)

    q_o[...] = (conv_silu(1, wcq_ref) * (HEAD_DIM ** -0.5)).astype(BF16)
    k = conv_silu(3, wck_ref)
    k_o[...] = k.astype(BF16)
    kt_o[...] = k.T.astype(BF16)
    gates = _split_dot(xh, xl, wgp_ref[...], wgh_ref[...]) + gb_ref[...]
    cval, col = _gate_scan_info(gates)
    col_o[...] = col[:, :N_GATES]
    row_o[...] = cval.T[:N_GATES, :]


def _inproj(x2d, mod, g, w_main, wg_pack, wg_hi, gate_bias_row, wcq, wck):
    t = x2d.shape[0]
    tm = CHUNK
    nt = t // tm
    r8 = tm // HALO
    last8 = t // HALO - 1
    const = lambda i: (0, 0)
    row = lambda i: (i, 0)
    f32_out = jax.ShapeDtypeStruct((t, MIX_HALF), F32)
    bf16_out = jax.ShapeDtypeStruct((t, MIX_HALF), BF16)
    seq = pl.BlockSpec((tm, MIX_HALF), row)
    return pl.pallas_call(
        functools.partial(_inproj_kernel, nt=nt),
        grid=(nt,),
        in_specs=[pl.BlockSpec((tm, D_MODEL), row),
                  pl.BlockSpec((HALO, D_MODEL), lambda i: (jnp.maximum(i * r8 - 1, 0), 0)),
                  pl.BlockSpec((HALO, D_MODEL), lambda i: (jnp.minimum((i + 1) * r8, last8), 0)),
                  pl.BlockSpec((2, D_MODEL), const),
                  pl.BlockSpec((1, D_MODEL), const),
                  pl.BlockSpec(w_main.shape, const, pipeline_mode=pl.Buffered(1)),
                  pl.BlockSpec((D_MODEL, LANES), const),
                  pl.BlockSpec((D_MODEL, LANES), const),
                  pl.BlockSpec((1, LANES), const),
                  pl.BlockSpec((3, MIX_HALF), const),
                  pl.BlockSpec((3, MIX_HALF), const)],
        out_specs=[seq, seq, seq, seq, pl.BlockSpec((MIX_HALF, tm), lambda i: (0, i)), seq,
                   pl.BlockSpec((tm, N_GATES), row), pl.BlockSpec((N_GATES, tm), lambda i: (0, i))],
        out_shape=[f32_out, bf16_out, f32_out, bf16_out,
                   jax.ShapeDtypeStruct((MIX_HALF, t), BF16), bf16_out,
                   jax.ShapeDtypeStruct((t, N_GATES), F32),
                   jax.ShapeDtypeStruct((N_GATES, t), F32)],
        compiler_params=_cparams(("arbitrary",)),
        name="inproj",
    )(x2d, x2d, x2d, mod, g, w_main, wg_pack, wg_hi, gate_bias_row, wcq, wck)


def _mlstm_kernel(*refs, need_out):
    if need_out:
        (qf, kf, ktf, vf, colf, rowf, qb, kb, ktb, vb, colb, rowb, s0, n0, m0,
         hf_o, hb_o, s_s, n_s, m_s) = refs
        q_refs, h_outs = (qf, qb), (hf_o, hb_o)
    else:
        (kf, ktf, vf, colf, rowf, kb, ktb, vb, colb, rowb, s0, n0, m0,
         s_o, n_o, m_o, s_s, n_s, m_s) = refs
    k_refs, kt_refs, v_refs = (kf, kb), (ktf, ktb), (vf, vb)
    col_refs, row_refs = (colf, colb), (rowf, rowb)
    j = pl.program_id(0)

    @pl.when(j == 0)
    def _():
        s_s[...] = s0[...]
        n_s[...] = n0[...]
        m_s[...] = m0[...]

    L = CHUNK
    row = lax.broadcasted_iota(jnp.int32, (L, L), 0)
    col = lax.broadcasted_iota(jnp.int32, (L, L), 1)

    for d in range(2):
        mask = (row >= col) if d == 0 else (row <= col)
        edge = L - 1 if d == 0 else 0
        colv = col_refs[d][...]
        rowv = row_refs[d][...]
        for h in range(HEADS):
            hd = d * HEADS + h
            lc, lb = 8 * d + h, 8 * d + 4 + h
            hs = slice(h * HEAD_DIM, (h + 1) * HEAD_DIM)
            cm_c, b_c = colv[:, lc:lc + 1], colv[:, lb:lb + 1]
            c_r = rowv[lc:lc + 1, :]
            g = b_c[edge:edge + 1, :]
            m_old = m_s[hd][:, 0:1]
            s_old = s_s[hd]
            n_old = n_s[hd]
            m_x = jnp.maximum(m_old, cm_c[edge:edge + 1, :])
            decay = jnp.exp(m_old - m_x)
            wk_r = jnp.exp(c_r - m_x)
            k_h = k_refs[d][:, hs]
            v_h = v_refs[d][:, hs]
            kwt = (kt_refs[d][hs, :].astype(F32) * wk_r).astype(BF16)
            n_new = decay * n_old + _dot(jnp.broadcast_to(wk_r, (8, L)).astype(BF16), k_h)[0:1, :]
            s_new = decay * s_old + _dot(kwt, v_h)
            if need_out:
                q_h = q_refs[d][:, hs]
                m_c = jnp.maximum(m_old, cm_c)
                w_inter = jnp.exp(m_old - m_c)
                qk = lax.dot_general(q_h, k_h, (((1,), (1,)), ((), ())),
                                     preferred_element_type=F32)
                p = jnp.where(mask, jnp.exp(c_r - m_c), 0.0) * qk
                num = w_inter * _dot(q_h, s_old.astype(BF16)) + _dot(p.astype(BF16), v_h)
                den = (w_inter * jnp.sum(q_h.astype(F32) * n_old, axis=1, keepdims=True)
                       + jnp.sum(p, axis=1, keepdims=True))
                h_outs[d][:, hs] = num / jnp.maximum(jnp.abs(den), jnp.exp(-(b_c + m_c)))
            s_s[hd] = s_new
            n_s[hd] = n_new
            m_s[hd] = jnp.broadcast_to(g + m_x, (1, LANES))

    if not need_out:
        s_o[...] = s_s[...]
        n_o[...] = n_s[...]
        m_o[...] = m_s[...]


def _mlstm(q, k, kt, v, col, rowi, s0, n0, m0, need_out):
    t = k.shape[0]
    nc = t // CHUNK
    fwd = lambda j: (j, 0)
    bwd = lambda j: (nc - 1 - j, 0)
    fwd_t = lambda j: (0, j)
    bwd_t = lambda j: (0, nc - 1 - j)
    c3 = lambda j: (0, 0, 0)
    seq = lambda im: pl.BlockSpec((CHUNK, MIX_HALF), im)
    state_specs = [pl.BlockSpec(s0.shape, c3), pl.BlockSpec(n0.shape, c3), pl.BlockSpec(m0.shape, c3)]
    scratch = [pltpu.VMEM(s0.shape, F32), pltpu.VMEM(n0.shape, F32), pltpu.VMEM(m0.shape, F32)]

    def side(im, im_t):
        specs = ([seq(im)] if need_out else []) + [seq(im), pl.BlockSpec((MIX_HALF, CHUNK), im_t), seq(im)]
        return specs + [pl.BlockSpec((CHUNK, N_GATES), im), pl.BlockSpec((N_GATES, CHUNK), im_t)]

    in_specs = side(fwd, fwd_t) + side(bwd, bwd_t) + state_specs
    seq_in = ((q,) if need_out else ()) + (k, kt, v, col, rowi)
    args = seq_in + seq_in + (s0, n0, m0)
    if need_out:
        out_specs = [seq(fwd), seq(bwd)]
        out_shape = [jax.ShapeDtypeStruct((t, MIX_HALF), F32)] * 2
    else:
        out_specs = state_specs
        out_shape = [jax.ShapeDtypeStruct(a.shape, F32) for a in (s0, n0, m0)]
    return pl.pallas_call(
        functools.partial(_mlstm_kernel, need_out=need_out),
        grid=(nc,),
        in_specs=in_specs,
        out_specs=out_specs,
        out_shape=out_shape,
        scratch_shapes=scratch,
        compiler_params=_cparams(("arbitrary",)),
        name="mlstm_out" if need_out else "mlstm_state",
    )(*args)


POOL_PAD = 512


def _pool_kernel(u_ref, w_ref, sc_ref, o_ref, pad_s, *, t):
    for gi, win in enumerate(POOL_WINDOWS):
        @pl.when(pl.program_id(0) == gi)
        def _():
            _pool_group(u_ref, w_ref, sc_ref, o_ref, pad_s, win=win, t=t)


def _pool_group(u_ref, w_ref, sc_ref, o_ref, pad_s, *, win, t):
    half = win // 2
    tile = 256
    zeros = jnp.zeros((POOL_PAD, POOL_GROUP), F32)
    pad_s[0:POOL_PAD, :] = zeros
    pad_s[POOL_PAD + t:POOL_PAD + t + POOL_PAD, :] = zeros

    def copy(r, carry):
        t0 = pl.multiple_of(r * tile, tile)
        pad_s[pl.ds(POOL_PAD + t0, tile), :] = u_ref[pl.ds(t0, tile), :]
        return carry

    lax.fori_loop(0, t // tile, copy, 0)

    row = lax.broadcasted_iota(jnp.int32, (tile, tile), 0)
    col = lax.broadcasted_iota(jnp.int32, (tile, tile), 1)
    same_row = (row >> GRID_SHIFT) == (col >> GRID_SHIFT)
    in_win = (col - row >= -half) & (col - row < half)
    band = jnp.where(same_row & in_win, 1.0, 0.0).astype(BF16)
    w = w_ref[...].astype(BF16)
    scale = sc_ref[...]
    n_rows = t // GRID_W

    def body(r, carry):
        t0 = pl.multiple_of(r * tile, tile)
        acc = jnp.zeros((tile, POOL_GROUP), F32)
        for dd in range(-half, half):
            acc = acc + pad_s[pl.ds(POOL_PAD + t0 + GRID_W * dd, tile), :]
        tok = t0 + lax.broadcasted_iota(jnp.int32, (tile, POOL_GROUP), 0)
        gr = tok >> GRID_SHIFT
        gc = tok & (GRID_W - 1)
        cnt_v = jnp.minimum(gr + half, n_rows) - jnp.maximum(gr - half, 0)
        cnt_h = jnp.minimum(gc + half, GRID_W) - jnp.maximum(gc - half, 0)
        mean_v = acc / cnt_v.astype(F32)
        hi, lo = _split2(mean_v)
        mean = (_dot(band, hi) + _dot(band, lo)) / cnt_h.astype(F32)
        x = pad_s[pl.ds(POOL_PAD + t0, tile), :]
        y = _dot((mean - x).astype(BF16), w) * scale
        o_ref[pl.ds(t0, tile), :] = y.astype(BF16)
        return carry

    lax.fori_loop(0, t // tile, body, 0)


def _pool(u_pool, w_pool, scale_row):
    t = u_pool.shape[0]
    return pl.pallas_call(
        functools.partial(_pool_kernel, t=t),
        grid=(len(POOL_WINDOWS),),
        in_specs=[pl.BlockSpec((t, POOL_GROUP), lambda g: (0, g)),
                  pl.BlockSpec((None, POOL_GROUP, POOL_GROUP), lambda g: (g, 0, 0)),
                  pl.BlockSpec((1, POOL_GROUP), lambda g: (0, g))],
        out_specs=pl.BlockSpec((t, POOL_GROUP), lambda g: (0, g)),
        out_shape=jax.ShapeDtypeStruct((t, MIX_HALF), BF16),
        scratch_shapes=[pltpu.VMEM((t + 2 * POOL_PAD, POOL_GROUP), F32)],
        compiler_params=_cparams(("arbitrary",)),
        name="pool_mix",
    )(u_pool, w_pool, scale_row)


def _route(logits):
    lane = lax.broadcasted_iota(jnp.int32, logits.shape, 1).astype(F32)
    neg = -jnp.inf
    big = float(LANES)
    gl = jnp.where(lane < N_GROUPS, logits, neg)
    gmax = jnp.max(gl, axis=1, keepdims=True)
    gsel = jnp.min(jnp.where(gl == gmax, lane, big), axis=1, keepdims=True)
    p_grp = 1.0 / jnp.sum(jnp.exp(gl - gmax), axis=1, keepdims=True)
    lo = ROUTE_LANE0 + EXPERTS_PER_GROUP * gsel
    el = jnp.where((lane >= lo) & (lane < lo + EXPERTS_PER_GROUP), logits, neg)
    m1 = jnp.max(el, axis=1, keepdims=True)
    i1 = jnp.min(jnp.where(el == m1, lane, big), axis=1, keepdims=True)
    el2 = jnp.where(lane == i1, neg, el)
    m2 = jnp.max(el2, axis=1, keepdims=True)
    i2 = jnp.min(jnp.where(el2 == m2, lane, big), axis=1, keepdims=True)
    e2 = jnp.exp(m2 - m1)
    p1 = 1.0 / (1.0 + e2)
    p2 = e2 / (1.0 + e2)
    info = jnp.where(lane == 0.0, i1 - ROUTE_LANE0, 0.0)
    info = jnp.where(lane == 1.0, i2 - ROUTE_LANE0, info)
    info = jnp.where(lane == 2.0, p_grp * p1, info)
    return jnp.where(lane == 3.0, p_grp * p2, info)


def _outproj_kernel(p_ref, hf_ref, hb_ref, uo_ref, x_ref, wout_ref, mod_ref,
                    hg_ref, n2g_ref, wrp_ref, wrh_ref, br_ref, h1_o, fn_o, info_o):
    h = hf_ref[...] + hb_ref[...]
    parts = []
    for hh in range(HEADS):
        hs = h[:, hh * HEAD_DIM:(hh + 1) * HEAD_DIM]
        mu = jnp.mean(hs, axis=-1, keepdims=True)
        ctr = hs - mu
        var = jnp.mean(ctr * ctr, axis=-1, keepdims=True)
        parts.append(ctr * lax.rsqrt(var + EPS))
    hn = jnp.concatenate(parts, axis=1) * hg_ref[...]
    m = (hn * jax.nn.sigmoid(uo_ref[...])).astype(BF16)
    mix = _dot(jnp.concatenate([p_ref[...], m], axis=1), wout_ref[...])
    h1 = x_ref[...] + mod_ref[2:3, :] * mix
    h1_o[...] = h1
    ms = jnp.mean(h1 * h1, axis=-1, keepdims=True)
    fn = h1 * lax.rsqrt(ms + EPS) * n2g_ref[...]
    fn = fn * (1.0 + mod_ref[4:5, :]) + mod_ref[3:4, :]
    fh, fl = _split2(fn)
    fn_o[...] = fh
    logits = _split_dot(fh, fl, wrp_ref[...], wrh_ref[...]) + br_ref[...]
    info_o[...] = _route(logits)


def _outproj(p, hf, hb, uo, x2d, w_out, mod, head_g, norm2_g, wr_pack, wr_hi, b_route, tm):
    t = x2d.shape[0]
    const = lambda i: (0, 0)
    row = lambda i: (i, 0)
    in_specs = ([pl.BlockSpec((tm, MIX_HALF), row)] * 4
                + [pl.BlockSpec((tm, D_MODEL), row),
                   pl.BlockSpec(w_out.shape, const, pipeline_mode=pl.Buffered(1)),
                   pl.BlockSpec(mod.shape, const),
                   pl.BlockSpec((1, MIX_HALF), const),
                   pl.BlockSpec((1, D_MODEL), const),
                   pl.BlockSpec((D_MODEL, LANES), const),
                   pl.BlockSpec((D_MODEL, LANES), const),
                   pl.BlockSpec((1, LANES), const)])
    return pl.pallas_call(
        _outproj_kernel,
        grid=(t // tm,),
        in_specs=in_specs,
        out_specs=[pl.BlockSpec((tm, D_MODEL), row), pl.BlockSpec((tm, D_MODEL), row),
                   pl.BlockSpec((tm, LANES), row)],
        out_shape=[jax.ShapeDtypeStruct((t, D_MODEL), F32),
                   jax.ShapeDtypeStruct((t, D_MODEL), BF16),
                   jax.ShapeDtypeStruct((t, LANES), F32)],
        compiler_params=_cparams(("arbitrary",)),
        name="outproj_route",
    )(p, hf, hb, uo, x2d, w_out, mod, head_g, norm2_g, wr_pack, wr_hi, b_route)


DISPATCH_BLOCK = 256
GRANULE = 16
GRANULE_SHIFT = 4
LOCAL_CAP = 1024
LOCAL_GRANULES = LOCAL_CAP // GRANULE
EXPERT_TILE = 128
TILE_GRANULES = EXPERT_TILE // GRANULE
PLAN_UNROLL = 4
PLAN_SLACK = 8
GATHER_DEPTH = 3


def _dispatch_kernel(fn_ref, info_ref, xs_o, ws_o, pt_o, cnt_o):
    tb = DISPATCH_BLOCK
    info = info_ref[...]
    e1, e2 = info[:, 0:1], info[:, 1:2]
    w1c, w2c = info[:, 2:3], info[:, 3:4]
    lane = lax.broadcasted_iota(jnp.int32, (tb, LANES), 1).astype(F32)
    o1 = jnp.where(lane == e1, 1.0, 0.0)
    o2 = jnp.where(lane == e2, 1.0, 0.0)
    onehot = o1 + o2
    cnt = jnp.sum(onehot, axis=0, keepdims=True)
    gran = jnp.floor((cnt + (GRANULE - 1)) * (1.0 / GRANULE))
    a = lax.broadcasted_iota(jnp.int32, (LANES, LANES), 0)
    b = lax.broadcasted_iota(jnp.int32, (LANES, LANES), 1)
    upper = jnp.where(a < b, 1.0, 0.0).astype(BF16)
    seg_off = _dot(jnp.broadcast_to(gran, (8, LANES)).astype(BF16), upper)[0:1, :] * GRANULE
    r = lax.broadcasted_iota(jnp.int32, (tb, tb), 0)
    c = lax.broadcasted_iota(jnp.int32, (tb, tb), 1)
    strict = jnp.where(r > c, 1.0, 0.0).astype(BF16)
    rank = _dot(strict, onehot.astype(BF16))
    slot = rank + seg_off
    pos1 = jnp.sum(o1 * slot, axis=1, keepdims=True)
    pos2 = jnp.sum(o2 * slot, axis=1, keepdims=True)
    rows = lax.broadcasted_iota(jnp.int32, (tb, LOCAL_CAP), 1).astype(F32)
    pt1 = jnp.where(rows == pos1, 1.0, 0.0)
    pt2 = jnp.where(rows == pos2, 1.0, 0.0)
    pt = pt1 + pt2
    pt_o[...] = pt.astype(BF16)
    perm = pt.T.astype(BF16)
    w_slot = jnp.sum((pt1 * w1c + pt2 * w2c).T, axis=1, keepdims=True)
    xs_o[...] = _dot(perm, fn_ref[...]).astype(BF16)
    ws_o[...] = jnp.broadcast_to(w_slot, (LOCAL_CAP, LANES))
    cnt_o[...] = jnp.broadcast_to(cnt, (8, LANES)).astype(jnp.int32)


def _dispatch(fn, info):
    t = fn.shape[0]
    nb = t // DISPATCH_BLOCK
    row = lambda i: (i, 0)
    return pl.pallas_call(
        _dispatch_kernel,
        grid=(nb,),
        in_specs=[pl.BlockSpec((DISPATCH_BLOCK, D_MODEL), row),
                  pl.BlockSpec((DISPATCH_BLOCK, LANES), row)],
        out_specs=[pl.BlockSpec((LOCAL_CAP, D_MODEL), row),
                   pl.BlockSpec((LOCAL_CAP, LANES), row),
                   pl.BlockSpec((DISPATCH_BLOCK, LOCAL_CAP), row),
                   pl.BlockSpec((8, LANES), row)],
        out_shape=[jax.ShapeDtypeStruct((nb * LOCAL_CAP, D_MODEL), BF16),
                   jax.ShapeDtypeStruct((nb * LOCAL_CAP, LANES), F32),
                   jax.ShapeDtypeStruct((t, LOCAL_CAP), BF16),
                   jax.ShapeDtypeStruct((nb * 8, LANES), jnp.int32)],
        compiler_params=_cparams(("arbitrary",)),
        name="moe_dispatch",
    )(fn, info)


def _plan_kernel(cnt_ref, gmap_o, texp_o, tend_o, ntile_o, lrun, *, nb, max_tiles):
    def init(b, c):
        lrun[b] = 0
        return c

    lax.fori_loop(0, nb, init, 0)

    def per_expert(e, carry):
        g0, last_e = carry

        def per_block(b, g):
            k = (cnt_ref[b, e] + (GRANULE - 1)) >> GRANULE_SHIFT
            lo = lrun[b]
            lrun[b] = lo + k

            base = b * LOCAL_GRANULES + lo

            for j in range(PLAN_UNROLL):
                gmap_o[g + j] = base + j

            @pl.when(k > PLAN_UNROLL)
            def _():
                def put(j, c):
                    gmap_o[g + j] = base + j
                    return c

                lax.fori_loop(PLAN_UNROLL, k, put, 0)

            return g + k

        g1 = lax.fori_loop(0, nb, per_block, g0)
        pad = (-g1) & (TILE_GRANULES - 1)

        def put_pad(j, c):
            gmap_o[g1 + j] = -1
            return c

        lax.fori_loop(0, pad, put_pad, 0)
        g2 = g1 + pad

        def put_tile(tt, c):
            texp_o[tt] = e
            return c

        lax.fori_loop(g0 // TILE_GRANULES, g2 // TILE_GRANULES, put_tile, 0)
        tend_o[e] = g2 // TILE_GRANULES
        return g2, jnp.where(g2 > g0, e, last_e)

    g_end, last_e = lax.fori_loop(0, N_EXPERTS, per_expert, (0, 0))
    n_tiles = g_end // TILE_GRANULES
    ntile_o[0] = n_tiles

    def fill(tt, c):
        texp_o[tt] = last_e
        return c

    lax.fori_loop(n_tiles, max_tiles, fill, 0)

    def fill_map(g, c):
        gmap_o[g] = -1
        return c

    lax.fori_loop(g_end, max_tiles * TILE_GRANULES + PLAN_SLACK, fill_map, 0)


def _max_tiles(t):
    nb = t // DISPATCH_BLOCK
    worst_rows = 2 * t + nb * N_EXPERTS * (GRANULE - 1) + N_EXPERTS * (EXPERT_TILE - GRANULE)
    return -(-worst_rows // EXPERT_TILE)


def _plan(cnt, t):
    nb = cnt.shape[0]
    max_tiles = _max_tiles(t)
    smem = pl.BlockSpec(memory_space=pltpu.SMEM)
    return pl.pallas_call(
        functools.partial(_plan_kernel, nb=nb, max_tiles=max_tiles),
        in_specs=[smem],
        out_specs=[smem, smem, smem, smem],
        out_shape=[jax.ShapeDtypeStruct((max_tiles * TILE_GRANULES + PLAN_SLACK,), jnp.int32),
                   jax.ShapeDtypeStruct((max_tiles,), jnp.int32),
                   jax.ShapeDtypeStruct((N_EXPERTS,), jnp.int32),
                   jax.ShapeDtypeStruct((1,), jnp.int32)],
        scratch_shapes=[pltpu.SMEM((nb,), jnp.int32)],
        name="moe_plan",
    )(cnt)


def _experts_kernel(gmap, texp, tend, ntile, xy_in, ws_in, w1_hbm, w3_hbm, w2_hbm, xy_out,
                    xbuf, wsbuf, ybuf, st1, st3, st2, wb1, wb3, wb2,
                    gsem, ssem, wsem, n_gather, n_scatter):
    nt = ntile[0]

    def weight_copies(e, ws):
        return (pltpu.make_async_copy(w1_hbm.at[e], st1.at[ws], wsem.at[ws]),
                pltpu.make_async_copy(w3_hbm.at[e], st3.at[ws], wsem.at[ws]),
                pltpu.make_async_copy(w2_hbm.at[e], st2.at[ws], wsem.at[ws]))

    def rows(i):
        return pl.ds(pl.multiple_of(i * GRANULE, GRANULE), GRANULE)

    def gather_copies(gm, j, sl):
        return (pltpu.make_async_copy(xy_in.at[rows(gm), :], xbuf.at[sl, rows(j), :], gsem.at[sl]),
                pltpu.make_async_copy(ws_in.at[rows(gm), :], wsbuf.at[sl, rows(j), :], gsem.at[sl]))

    def scatter_copies(gm, j, sl):
        return (pltpu.make_async_copy(ybuf.at[sl, rows(j), :], xy_out.at[rows(gm), :], ssem.at[sl]),)

    def issue(tt, sl, copies, counter):
        def body(j, n):
            gm = gmap[tt * TILE_GRANULES + j]

            @pl.when(gm >= 0)
            def _():
                for cp in copies(gm, j, sl):
                    cp.start()

            return n + (gm >= 0).astype(jnp.int32)

        counter[sl] = lax.fori_loop(0, TILE_GRANULES, body, 0)

    def drain(sl, copies, counter):
        def body(j, c):
            for cp in copies(0, 0, sl):
                cp.wait()
            return c

        lax.fori_loop(0, counter[sl], body, 0)

    xbuf[...] = jnp.zeros_like(xbuf)
    wsbuf[...] = jnp.zeros_like(wsbuf)

    @pl.when(nt > 0)
    def _():
        for cp in weight_copies(texp[0], 0):
            cp.start()

    for ahead in range(GATHER_DEPTH - 1):
        @pl.when(ahead < nt)
        def _():
            issue(ahead, ahead, gather_copies, n_gather)

    def tile(t, wslot):
        slot = t % 2
        gslot = t % GATHER_DEPTH
        e = texp[t]
        first = (t == 0) | (texp[jnp.maximum(t - 1, 0)] != e)
        wslot = jnp.where(first & (t > 0), 1 - wslot, wslot)

        drain(gslot, gather_copies, n_gather)

        @pl.when(t + GATHER_DEPTH - 1 < nt)
        def _():
            issue(t + GATHER_DEPTH - 1, (t + GATHER_DEPTH - 1) % GATHER_DEPTH, gather_copies, n_gather)

        @pl.when(first)
        def _():
            for cp in weight_copies(e, wslot):
                cp.wait()
            wb1[...] = st1[wslot].astype(BF16)
            wb3[...] = st3[wslot].astype(BF16)
            wb2[...] = st2[wslot].astype(BF16)
            nxt = tend[e]

            @pl.when(nxt < nt)
            def _():
                for cp in weight_copies(texp[jnp.minimum(nxt, nt - 1)], 1 - wslot):
                    cp.start()

        x = xbuf[gslot]
        w_row = wsbuf[gslot][:, 0:1]
        a = _dot(x, wb1[...])
        b = _dot(x, wb3[...])
        y = _dot((_silu(a) * b).astype(BF16), wb2[...]) * w_row

        @pl.when(t >= 2)
        def _():
            drain(slot, scatter_copies, n_scatter)

        ybuf[slot] = y.astype(BF16)
        issue(t, slot, scatter_copies, n_scatter)
        return wslot

    lax.fori_loop(0, nt, tile, 0)

    @pl.when(nt >= 1)
    def _():
        drain((nt - 1) % 2, scatter_copies, n_scatter)

    @pl.when(nt >= 2)
    def _():
        drain(nt % 2, scatter_copies, n_scatter)


def _experts(gmap, texp, tend, ntile, xy, ws, w1, w3, w2):
    smem = pl.BlockSpec(memory_space=pltpu.SMEM)
    hbm = pl.BlockSpec(memory_space=pl.ANY)
    return pl.pallas_call(
        _experts_kernel,
        in_specs=[smem, smem, smem, smem, hbm, hbm, hbm, hbm, hbm],
        out_specs=hbm,
        out_shape=jax.ShapeDtypeStruct(xy.shape, xy.dtype),
        scratch_shapes=[pltpu.VMEM((GATHER_DEPTH, EXPERT_TILE, D_MODEL), BF16),
                        pltpu.VMEM((GATHER_DEPTH, EXPERT_TILE, LANES), F32),
                        pltpu.VMEM((2, EXPERT_TILE, D_MODEL), BF16),
                        pltpu.VMEM((2, D_MODEL, D_EXPERT), F32),
                        pltpu.VMEM((2, D_MODEL, D_EXPERT), F32),
                        pltpu.VMEM((2, D_EXPERT, D_MODEL), F32),
                        pltpu.VMEM((D_MODEL, D_EXPERT), BF16),
                        pltpu.VMEM((D_MODEL, D_EXPERT), BF16),
                        pltpu.VMEM((D_EXPERT, D_MODEL), BF16),
                        pltpu.SemaphoreType.DMA((GATHER_DEPTH,)),
                        pltpu.SemaphoreType.DMA((2,)),
                        pltpu.SemaphoreType.DMA((2,)),
                        pltpu.SMEM((GATHER_DEPTH,), jnp.int32),
                        pltpu.SMEM((2,), jnp.int32)],
        input_output_aliases={4: 0},
        compiler_params=pltpu.CompilerParams(vmem_limit_bytes=VMEM_LIMIT),
        name="moe_experts",
    )(gmap, texp, tend, ntile, xy, ws, w1, w3, w2)


def _combine_kernel(y_ref, pt_ref, h1_ref, g2_ref, fg_ref, o_ref):
    moe = _dot(pt_ref[...], y_ref[...])
    h = h1_ref[...] + g2_ref[...] * moe
    ms = jnp.mean(h * h, axis=-1, keepdims=True)
    o_ref[...] = h * lax.rsqrt(ms + EPS) * fg_ref[...]


def _combine(xy, pt, h1, g2, final_g):
    t = h1.shape[0]
    row = lambda i: (i, 0)
    const = lambda i: (0, 0)
    return pl.pallas_call(
        _combine_kernel,
        grid=(t // DISPATCH_BLOCK,),
        in_specs=[pl.BlockSpec((LOCAL_CAP, D_MODEL), row),
                  pl.BlockSpec((DISPATCH_BLOCK, LOCAL_CAP), row),
                  pl.BlockSpec((DISPATCH_BLOCK, D_MODEL), row),
                  pl.BlockSpec((1, D_MODEL), const), pl.BlockSpec((1, D_MODEL), const)],
        out_specs=pl.BlockSpec((DISPATCH_BLOCK, D_MODEL), row),
        out_shape=jax.ShapeDtypeStruct((t, D_MODEL), F32),
        compiler_params=_cparams(("arbitrary",)),
        name="moe_combine_final",
    )(xy, pt, h1, g2, final_g)


def _pad_lanes(a):
    return jnp.pad(a, ((0, 0), (0, LANES - a.shape[1])))


def kernel(x, c, ctx, c_ctx, w_mod, b_mod, norm1_g, w_in, w_conv_q, w_conv_k, gate_bias, head_norm_g, w_pool, pool_scale, w_out, norm2_g, w_group, b_group, w_router, b_router, w1, w3, w2, final_g):
    assert x.shape[0] == 1 and w_mod.shape[0] == 1
    seq = x.shape[1]
    x2d = x[0]
    ctx2d = ctx[0]

    n_main = 5 * MIX_HALF
    w_in_t = jnp.transpose(w_in[0])
    w_main = _cast_transposed_bf16(w_in_t, n_main)
    wg_pack, wg_hi = _split_pack(jnp.transpose(_take_rows(w_in_t, n_main, N_GATES)))
    gate_bias_row = _pad_lanes(gate_bias[0].reshape(1, N_GATES))
    w_out_bf = _cast_bf16(w_out[0])
    wr_pack, wr_hi = _split_pack(jnp.concatenate([w_group[0], w_router[0]], axis=1))
    b_route = _pad_lanes(jnp.concatenate([b_group[0], b_router[0]]).reshape(1, -1))
    norm1 = norm1_g[0].reshape(1, D_MODEL)

    c16 = jnp.zeros((16, D_MODEL), F32).at[0].set(c[0]).at[1].set(c_ctx)
    mods = _adaln(c16, w_mod[0], b_mod[0].reshape(1, -1))
    mod_lat = mods[0].reshape(6, D_MODEL)
    mod_ctx = mods[1].reshape(6, D_MODEL)

    proj = functools.partial(_inproj, g=norm1, w_main=w_main, wg_pack=wg_pack, wg_hi=wg_hi,
                             gate_bias_row=gate_bias_row, wcq=w_conv_q[0], wck=w_conv_k[0])
    _, _, _, k_c, kt_c, v_c, col_c, row_c = proj(ctx2d, mod_ctx[0:2])
    zeros_state = (jnp.zeros((2 * HEADS, HEAD_DIM, HEAD_DIM), F32),
                   jnp.zeros((2 * HEADS, 1, HEAD_DIM), F32),
                   jnp.zeros((2 * HEADS, 1, LANES), F32))
    s0, n0, m0 = _mlstm(None, k_c, kt_c, v_c, col_c, row_c, *zeros_state, need_out=False)

    u_pool, q, uo, k, kt, v, col, rowi = proj(x2d, mod_lat[0:2])
    hf, hb = _mlstm(q, k, kt, v, col, rowi, s0, n0, m0, need_out=True)
    p = _pool(u_pool, w_pool[0], pool_scale[0].reshape(1, -1))
    h1, fn, info = _outproj(p, hf, hb, uo, x2d, w_out_bf, mod_lat, head_norm_g[0].reshape(1, -1),
                            norm2_g[0].reshape(1, -1), wr_pack, wr_hi, b_route, tm=256)

    xs, ws, pt, cnt = _dispatch(fn, info)
    gmap, texp, tend, ntile = _plan(cnt[::8, :N_EXPERTS], seq)
    xy = _experts(gmap, texp, tend, ntile, xs, ws, w1[0], w3[0], w2[0])
    out = _combine(xy, pt, h1, mod_lat[5:6], final_g.reshape(1, -1))
    return out.reshape(1, seq, D_MODEL)
```

```python
import functools

import jax
import jax.numpy as jnp
from jax import lax
from jax.experimental import pallas as pl
from jax.experimental.pallas import tpu as pltpu

F32 = jnp.float32
BF16 = jnp.bfloat16

D_MODEL = 2048
GRID_W = 64
GRID_SHIFT = 6
POOL_WINDOWS = (2, 4, 8, 16)
POOL_GROUP = 256
HEADS = 4
HEAD_DIM = 256
MIX_HALF = 1024
N_GATES = 16
N_GROUPS = 4
EXPERTS_PER_GROUP = 8
N_EXPERTS = 32
D_EXPERT = 512
EPS = 1e-6
LANES = 128
CHUNK = 256
ROUTE_LANE0 = N_GROUPS

VMEM_LIMIT = 56 * 1024 * 1024


def _cparams(sem, vmem=VMEM_LIMIT):
    return pltpu.CompilerParams(dimension_semantics=sem, vmem_limit_bytes=vmem)


def _split2(x):
    hi = x.astype(BF16)
    lo = (x - hi.astype(F32)).astype(BF16)
    return hi, lo


def _split3(x):
    hi = x.astype(BF16)
    r = x - hi.astype(F32)
    mid = r.astype(BF16)
    lo = (r - mid.astype(F32)).astype(BF16)
    return hi, mid, lo


def _dot(a, b):
    return jnp.dot(a, b, preferred_element_type=F32)


SPLIT_LANE = 64


def _split_pack(w):
    hi, lo = _split2(w)
    n = w.shape[1]
    packed = jnp.zeros((w.shape[0], LANES), BF16).at[:, :n].set(hi).at[:, SPLIT_LANE:SPLIT_LANE + n].set(lo)
    return packed, jnp.pad(hi, ((0, 0), (0, LANES - n)))


def _split_dot(xh, xl, w_packed, w_hi):
    r = _dot(xh, w_packed)
    return r + pltpu.roll(r, SPLIT_LANE, 1) + _dot(xl, w_hi)


def _silu(x):
    return x * jax.nn.sigmoid(x)


def _log_sigmoid(x):
    return jnp.minimum(x, 0.0) - jnp.log(1.0 + jnp.exp(-jnp.abs(x)))


def _cast_kernel(w_ref, o_ref):
    o_ref[...] = w_ref[...].astype(BF16)


def _cast_bf16(w, rows=256):
    k, n = w.shape
    return pl.pallas_call(
        _cast_kernel,
        grid=(k // rows,),
        in_specs=[pl.BlockSpec((rows, n), lambda i: (i, 0))],
        out_specs=pl.BlockSpec((rows, n), lambda i: (i, 0)),
        out_shape=jax.ShapeDtypeStruct((k, n), BF16),
        compiler_params=_cparams(("arbitrary",)),
        name="cast_bf16",
    )(w)


def _copy_kernel(w_ref, o_ref):
    o_ref[...] = w_ref[...]


def _take_rows(w, start, n):
    return pl.pallas_call(
        _copy_kernel,
        grid=(1,),
        in_specs=[pl.BlockSpec((n, w.shape[1]), lambda i: (start // n, 0))],
        out_specs=pl.BlockSpec((n, w.shape[1]), lambda i: (0, 0)),
        out_shape=jax.ShapeDtypeStruct((n, w.shape[1]), w.dtype),
        name="take_rows",
    )(w)


def _cast_t_kernel(wt_ref, o_ref):
    o_ref[...] = wt_ref[...].T.astype(BF16)


def _cast_transposed_bf16(wt, n_cols, cols=256):
    k = wt.shape[1]
    return pl.pallas_call(
        _cast_t_kernel,
        grid=(n_cols // cols,),
        in_specs=[pl.BlockSpec((cols, k), lambda i: (i, 0))],
        out_specs=pl.BlockSpec((k, cols), lambda i: (0, i)),
        out_shape=jax.ShapeDtypeStruct((k, n_cols), BF16),
        compiler_params=_cparams(("arbitrary",)),
        name="cast_transposed_bf16",
    )(wt)


def _adaln_kernel(c_ref, w_ref, b_ref, o_ref):
    a = _silu(c_ref[...])
    a3 = jnp.concatenate(_split3(a), axis=0)
    w_hi, w_lo = _split2(w_ref[...])
    acc = _dot(a3, w_hi)
    acc_lo = _dot(a3[:32], w_lo)
    out = acc[0:16] + acc[16:32] + acc[32:48] + acc_lo[0:16] + acc_lo[16:32]
    o_ref[...] = out + b_ref[...]


def _adaln(c16, w_mod, b_mod):
    n = w_mod.shape[1]
    tn = 1536
    return pl.pallas_call(
        _adaln_kernel,
        grid=(n // tn,),
        in_specs=[pl.BlockSpec((16, D_MODEL), lambda j: (0, 0)),
                  pl.BlockSpec((D_MODEL, tn), lambda j: (0, j)),
                  pl.BlockSpec((1, tn), lambda j: (0, j))],
        out_specs=pl.BlockSpec((16, tn), lambda j: (0, j)),
        out_shape=jax.ShapeDtypeStruct((16, n), F32),
        compiler_params=_cparams(("arbitrary",)),
        name="adaln",
    )(c16, w_mod, b_mod)


HALO = 8
NEG_INF = float("-inf")


def _gate_scan_info(gates):
    n = gates.shape[0]
    lane = lax.broadcasted_iota(jnp.int32, gates.shape, 1)
    rows = lax.broadcasted_iota(jnp.int32, gates.shape, 0)
    lf = jnp.where(lane < N_GATES, _log_sigmoid(gates), 0.0)
    hi = lf.astype(BF16).astype(F32)
    rem = lf - hi
    mid = rem.astype(BF16).astype(F32)
    packed = (hi + pltpu.roll(mid, 32, 1) + pltpu.roll(rem - mid, 64, 1)).astype(BF16)
    r = lax.broadcasted_iota(jnp.int32, (n, n), 0)
    c = lax.broadcasted_iota(jnp.int32, (n, n), 1)
    pf = _dot(jnp.where(r >= c, 1.0, 0.0).astype(BF16), packed)
    pb = _dot(jnp.where(r <= c, 1.0, 0.0).astype(BF16), packed)
    bf = pf + pltpu.roll(pf, 96, 1) + pltpu.roll(pf, 64, 1)
    bb = pb + pltpu.roll(pb, 96, 1) + pltpu.roll(pb, 64, 1)
    b = jnp.where(lane < 8, bf, bb)
    cval = gates - pltpu.roll(b, LANES - 4, 1)
    pm = cval
    sm = cval
    step = 1
    while step < n:
        pm = jnp.maximum(pm, jnp.where(rows >= step, pltpu.roll(pm, step, 0), NEG_INF))
        sm = jnp.maximum(sm, jnp.where(rows < n - step, pltpu.roll(sm, n - step, 0), NEG_INF))
        step *= 2
    cm = jnp.where(lane < 8, pm, sm)
    return cval, jnp.where((lane & 4) == 0, cm, b)


def _inproj_kernel(x_ref, xp_ref, xn_ref, mod_ref, g_ref, w_ref, wgp_ref, wgh_ref, gb_ref,
                   wcq_ref, wck_ref, pool_o, q_o, o_o, k_o, kt_o, v_o, col_o, row_o, *, nt):
    i = pl.program_id(0)
    tm = x_ref.shape[0]
    x_all = jnp.concatenate([xp_ref[...], x_ref[...], xn_ref[...]], axis=0)
    ms = jnp.mean(x_all * x_all, axis=-1, keepdims=True)
    y = x_all * lax.rsqrt(ms + EPS) * g_ref[...]
    xn_all = y * (1.0 + mod_ref[1:2, :]) + mod_ref[0:1, :]
    xh_all = xn_all.astype(BF16)
    xh, xl = _split2(xn_all[HALO:HALO + tm])

    def cols(ci):
        return w_ref[:, ci * MIX_HALF:(ci + 1) * MIX_HALF]

    u_q = _dot(xh_all, cols(1))
    u_k = _dot(xh_all, cols(3))
    pool_o[...] = _dot(xh, cols(0))
    o_o[...] = _dot(xh, cols(2))
    v_o[...] = _dot(xh, cols(4)).astype(BF16)

    rowi = lax.broadcasted_iota(jnp.int32, (tm, MIX_HALF), 0)
    at_start = jnp.logical_and(rowi == 0, i == 0)
    at_end = jnp.logical_and(rowi == tm - 1, i == nt - 1)

    def conv_silu(u, wc_ref):
        n = u.shape[0]
        up = jnp.where(at_start, 0.0, pltpu.roll(u, 1, 0)[HALO:HALO + tm])
        un = jnp.where(at_end, 0.0, pltpu.roll(u, n - 1, 0)[HALO:HALO + tm])
        return _silu(wc_ref[0:1, :] * up + wc_ref[1:2, :] * u[HALO:HALO + tm] + wc_ref[2:3, :] * un)

    q_o[...] = (conv_silu(u_q, wcq_ref) * (HEAD_DIM ** -0.5)).astype(BF16)
    k = conv_silu(u_k, wck_ref)
    k_o[...] = k.astype(BF16)
    kt_o[...] = k.T.astype(BF16)
    gates = _split_dot(xh, xl, wgp_ref[...], wgh_ref[...]) + gb_ref[...]
    cval, col = _gate_scan_info(gates)
    col_o[...] = col[:, :N_GATES]
    row_o[...] = cval.T[:N_GATES, :]


def _inproj(x2d, mod, g, w_main, wg_pack, wg_hi, gate_bias_row, wcq, wck):
    t = x2d.shape[0]
    tm = CHUNK
    nt = t // tm
    r8 = tm // HALO
    last8 = t // HALO - 1
    const = lambda i: (0, 0)
    row = lambda i: (i, 0)
    f32_out = jax.ShapeDtypeStruct((t, MIX_HALF), F32)
    bf16_out = jax.ShapeDtypeStruct((t, MIX_HALF), BF16)
    seq = pl.BlockSpec((tm, MIX_HALF), row)
    return pl.pallas_call(
        functools.partial(_inproj_kernel, nt=nt),
        grid=(nt,),
        in_specs=[pl.BlockSpec((tm, D_MODEL), row),
                  pl.BlockSpec((HALO, D_MODEL), lambda i: (jnp.maximum(i * r8 - 1, 0), 0)),
                  pl.BlockSpec((HALO, D_MODEL), lambda i: (jnp.minimum((i + 1) * r8, last8), 0)),
                  pl.BlockSpec((2, D_MODEL), const),
                  pl.BlockSpec((1, D_MODEL), const),
                  pl.BlockSpec(w_main.shape, const, pipeline_mode=pl.Buffered(1)),
                  pl.BlockSpec((D_MODEL, LANES), const),
                  pl.BlockSpec((D_MODEL, LANES), const),
                  pl.BlockSpec((1, LANES), const),
                  pl.BlockSpec((3, MIX_HALF), const),
                  pl.BlockSpec((3, MIX_HALF), const)],
        out_specs=[seq, seq, seq, seq, pl.BlockSpec((MIX_HALF, tm), lambda i: (0, i)), seq,
                   pl.BlockSpec((tm, N_GATES), row), pl.BlockSpec((N_GATES, tm), lambda i: (0, i))],
        out_shape=[f32_out, bf16_out, f32_out, bf16_out,
                   jax.ShapeDtypeStruct((MIX_HALF, t), BF16), bf16_out,
                   jax.ShapeDtypeStruct((t, N_GATES), F32),
                   jax.ShapeDtypeStruct((N_GATES, t), F32)],
        compiler_params=_cparams(("arbitrary",)),
        name="inproj",
    )(x2d, x2d, x2d, mod, g, w_main, wg_pack, wg_hi, gate_bias_row, wcq, wck)


def _mlstm_kernel(*refs, need_out):
    if need_out:
        (qf, kf, ktf, vf, colf, rowf, qb, kb, ktb, vb, colb, rowb, s0, n0, m0,
         hf_o, hb_o, s_s, n_s, m_s) = refs
        q_refs, h_outs = (qf, qb), (hf_o, hb_o)
    else:
        (kf, ktf, vf, colf, rowf, kb, ktb, vb, colb, rowb, s0, n0, m0,
         s_o, n_o, m_o, s_s, n_s, m_s) = refs
    k_refs, kt_refs, v_refs = (kf, kb), (ktf, ktb), (vf, vb)
    col_refs, row_refs = (colf, colb), (rowf, rowb)
    j = pl.program_id(0)

    @pl.when(j == 0)
    def _():
        s_s[...] = s0[...]
        n_s[...] = n0[...]
        m_s[...] = m0[...]

    L = CHUNK
    row = lax.broadcasted_iota(jnp.int32, (L, L), 0)
    col = lax.broadcasted_iota(jnp.int32, (L, L), 1)

    for d in range(2):
        mask = (row >= col) if d == 0 else (row <= col)
        edge = L - 1 if d == 0 else 0
        colv = col_refs[d][...]
        rowv = row_refs[d][...]
        for h in range(HEADS):
            hd = d * HEADS + h
            lc, lb = 8 * d + h, 8 * d + 4 + h
            hs = slice(h * HEAD_DIM, (h + 1) * HEAD_DIM)
            cm_c, b_c = colv[:, lc:lc + 1], colv[:, lb:lb + 1]
            c_r = rowv[lc:lc + 1, :]
            g = b_c[edge:edge + 1, :]
            m_old = m_s[hd][:, 0:1]
            s_old = s_s[hd]
            n_old = n_s[hd]
            m_x = jnp.maximum(m_old, cm_c[edge:edge + 1, :])
            decay = jnp.exp(m_old - m_x)
            wk_r = jnp.exp(c_r - m_x)
            k_h = k_refs[d][:, hs]
            v_h = v_refs[d][:, hs]
            kwt = (kt_refs[d][hs, :].astype(F32) * wk_r).astype(BF16)
            n_new = decay * n_old + _dot(jnp.broadcast_to(wk_r, (8, L)).astype(BF16), k_h)[0:1, :]
            s_new = decay * s_old + _dot(kwt, v_h)
            if need_out:
                q_h = q_refs[d][:, hs]
                m_c = jnp.maximum(m_old, cm_c)
                w_inter = jnp.exp(m_old - m_c)
                qk = lax.dot_general(q_h, k_h, (((1,), (1,)), ((), ())),
                                     preferred_element_type=F32)
                p = jnp.where(mask, jnp.exp(c_r - m_c), 0.0) * qk
                num = w_inter * _dot(q_h, s_old.astype(BF16)) + _dot(p.astype(BF16), v_h)
                den = (w_inter * jnp.sum(q_h.astype(F32) * n_old, axis=1, keepdims=True)
                       + jnp.sum(p, axis=1, keepdims=True))
                h_outs[d][:, hs] = num / jnp.maximum(jnp.abs(den), jnp.exp(-(b_c + m_c)))
            s_s[hd] = s_new
            n_s[hd] = n_new
            m_s[hd] = jnp.broadcast_to(g + m_x, (1, LANES))

    if not need_out:
        s_o[...] = s_s[...]
        n_o[...] = n_s[...]
        m_o[...] = m_s[...]


def _mlstm(q, k, kt, v, col, rowi, s0, n0, m0, need_out):
    t = k.shape[0]
    nc = t // CHUNK
    fwd = lambda j: (j, 0)
    bwd = lambda j: (nc - 1 - j, 0)
    fwd_t = lambda j: (0, j)
    bwd_t = lambda j: (0, nc - 1 - j)
    c3 = lambda j: (0, 0, 0)
    seq = lambda im: pl.BlockSpec((CHUNK, MIX_HALF), im)
    state_specs = [pl.BlockSpec(s0.shape, c3), pl.BlockSpec(n0.shape, c3), pl.BlockSpec(m0.shape, c3)]
    scratch = [pltpu.VMEM(s0.shape, F32), pltpu.VMEM(n0.shape, F32), pltpu.VMEM(m0.shape, F32)]

    def side(im, im_t):
        specs = ([seq(im)] if need_out else []) + [seq(im), pl.BlockSpec((MIX_HALF, CHUNK), im_t), seq(im)]
        return specs + [pl.BlockSpec((CHUNK, N_GATES), im), pl.BlockSpec((N_GATES, CHUNK), im_t)]

    in_specs = side(fwd, fwd_t) + side(bwd, bwd_t) + state_specs
    seq_in = ((q,) if need_out else ()) + (k, kt, v, col, rowi)
    args = seq_in + seq_in + (s0, n0, m0)
    if need_out:
        out_specs = [seq(fwd), seq(bwd)]
        out_shape = [jax.ShapeDtypeStruct((t, MIX_HALF), F32)] * 2
    else:
        out_specs = state_specs
        out_shape = [jax.ShapeDtypeStruct(a.shape, F32) for a in (s0, n0, m0)]
    return pl.pallas_call(
        functools.partial(_mlstm_kernel, need_out=need_out),
        grid=(nc,),
        in_specs=in_specs,
        out_specs=out_specs,
        out_shape=out_shape,
        scratch_shapes=scratch,
        compiler_params=_cparams(("arbitrary",)),
        name="mlstm_out" if need_out else "mlstm_state",
    )(*args)


POOL_PAD = 512


def _pool_kernel(u_ref, w_ref, sc_ref, o_ref, pad_s, *, t):
    for gi, win in enumerate(POOL_WINDOWS):
        @pl.when(pl.program_id(0) == gi)
        def _():
            _pool_group(u_ref, w_ref, sc_ref, o_ref, pad_s, win=win, t=t)


def _pool_group(u_ref, w_ref, sc_ref, o_ref, pad_s, *, win, t):
    half = win // 2
    tile = 256
    zeros = jnp.zeros((POOL_PAD, POOL_GROUP), F32)
    pad_s[0:POOL_PAD, :] = zeros
    pad_s[POOL_PAD + t:POOL_PAD + t + POOL_PAD, :] = zeros

    def copy(r, carry):
        t0 = pl.multiple_of(r * tile, tile)
        pad_s[pl.ds(POOL_PAD + t0, tile), :] = u_ref[pl.ds(t0, tile), :]
        return carry

    lax.fori_loop(0, t // tile, copy, 0)

    row = lax.broadcasted_iota(jnp.int32, (tile, tile), 0)
    col = lax.broadcasted_iota(jnp.int32, (tile, tile), 1)
    same_row = (row >> GRID_SHIFT) == (col >> GRID_SHIFT)
    in_win = (col - row >= -half) & (col - row < half)
    band = jnp.where(same_row & in_win, 1.0, 0.0).astype(BF16)
    w = w_ref[...].astype(BF16)
    scale = sc_ref[...]
    n_rows = t // GRID_W

    def body(r, carry):
        t0 = pl.multiple_of(r * tile, tile)
        acc = jnp.zeros((tile, POOL_GROUP), F32)
        for dd in range(-half, half):
            acc = acc + pad_s[pl.ds(POOL_PAD + t0 + GRID_W * dd, tile), :]
        tok = t0 + lax.broadcasted_iota(jnp.int32, (tile, POOL_GROUP), 0)
        gr = tok >> GRID_SHIFT
        gc = tok & (GRID_W - 1)
        cnt_v = jnp.minimum(gr + half, n_rows) - jnp.maximum(gr - half, 0)
        cnt_h = jnp.minimum(gc + half, GRID_W) - jnp.maximum(gc - half, 0)
        mean_v = acc / cnt_v.astype(F32)
        hi, lo = _split2(mean_v)
        mean = (_dot(band, hi) + _dot(band, lo)) / cnt_h.astype(F32)
        x = pad_s[pl.ds(POOL_PAD + t0, tile), :]
        y = _dot((mean - x).astype(BF16), w) * scale
        o_ref[pl.ds(t0, tile), :] = y.astype(BF16)
        return carry

    lax.fori_loop(0, t // tile, body, 0)


def _pool(u_pool, w_pool, scale_row):
    t = u_pool.shape[0]
    return pl.pallas_call(
        functools.partial(_pool_kernel, t=t),
        grid=(len(POOL_WINDOWS),),
        in_specs=[pl.BlockSpec((t, POOL_GROUP), lambda g: (0, g)),
                  pl.BlockSpec((None, POOL_GROUP, POOL_GROUP), lambda g: (g, 0, 0)),
                  pl.BlockSpec((1, POOL_GROUP), lambda g: (0, g))],
        out_specs=pl.BlockSpec((t, POOL_GROUP), lambda g: (0, g)),
        out_shape=jax.ShapeDtypeStruct((t, MIX_HALF), BF16),
        scratch_shapes=[pltpu.VMEM((t + 2 * POOL_PAD, POOL_GROUP), F32)],
        compiler_params=_cparams(("arbitrary",)),
        name="pool_mix",
    )(u_pool, w_pool, scale_row)


def _route(logits):
    lane = lax.broadcasted_iota(jnp.int32, logits.shape, 1).astype(F32)
    neg = -jnp.inf
    big = float(LANES)
    gl = jnp.where(lane < N_GROUPS, logits, neg)
    gmax = jnp.max(gl, axis=1, keepdims=True)
    gsel = jnp.min(jnp.where(gl == gmax, lane, big), axis=1, keepdims=True)
    p_grp = 1.0 / jnp.sum(jnp.exp(gl - gmax), axis=1, keepdims=True)
    lo = ROUTE_LANE0 + EXPERTS_PER_GROUP * gsel
    el = jnp.where((lane >= lo) & (lane < lo + EXPERTS_PER_GROUP), logits, neg)
    m1 = jnp.max(el, axis=1, keepdims=True)
    i1 = jnp.min(jnp.where(el == m1, lane, big), axis=1, keepdims=True)
    el2 = jnp.where(lane == i1, neg, el)
    m2 = jnp.max(el2, axis=1, keepdims=True)
    i2 = jnp.min(jnp.where(el2 == m2, lane, big), axis=1, keepdims=True)
    e2 = jnp.exp(m2 - m1)
    p1 = 1.0 / (1.0 + e2)
    p2 = e2 / (1.0 + e2)
    info = jnp.where(lane == 0.0, i1 - ROUTE_LANE0, 0.0)
    info = jnp.where(lane == 1.0, i2 - ROUTE_LANE0, info)
    info = jnp.where(lane == 2.0, p_grp * p1, info)
    return jnp.where(lane == 3.0, p_grp * p2, info)


def _outproj_kernel(p_ref, hf_ref, hb_ref, uo_ref, x_ref, wout_ref, mod_ref,
                    hg_ref, n2g_ref, wrp_ref, wrh_ref, br_ref, h1_o, fn_o, info_o):
    h = hf_ref[...] + hb_ref[...]
    parts = []
    for hh in range(HEADS):
        hs = h[:, hh * HEAD_DIM:(hh + 1) * HEAD_DIM]
        mu = jnp.mean(hs, axis=-1, keepdims=True)
        ctr = hs - mu
        var = jnp.mean(ctr * ctr, axis=-1, keepdims=True)
        parts.append(ctr * lax.rsqrt(var + EPS))
    hn = jnp.concatenate(parts, axis=1) * hg_ref[...]
    m = (hn * jax.nn.sigmoid(uo_ref[...])).astype(BF16)
    mix = _dot(jnp.concatenate([p_ref[...], m], axis=1), wout_ref[...])
    h1 = x_ref[...] + mod_ref[2:3, :] * mix
    h1_o[...] = h1
    ms = jnp.mean(h1 * h1, axis=-1, keepdims=True)
    fn = h1 * lax.rsqrt(ms + EPS) * n2g_ref[...]
    fn = fn * (1.0 + mod_ref[4:5, :]) + mod_ref[3:4, :]
    fh, fl = _split2(fn)
    fn_o[...] = fh
    logits = _split_dot(fh, fl, wrp_ref[...], wrh_ref[...]) + br_ref[...]
    info_o[...] = _route(logits)


def _outproj(p, hf, hb, uo, x2d, w_out, mod, head_g, norm2_g, wr_pack, wr_hi, b_route, tm):
    t = x2d.shape[0]
    const = lambda i: (0, 0)
    row = lambda i: (i, 0)
    in_specs = ([pl.BlockSpec((tm, MIX_HALF), row)] * 4
                + [pl.BlockSpec((tm, D_MODEL), row),
                   pl.BlockSpec(w_out.shape, const, pipeline_mode=pl.Buffered(1)),
                   pl.BlockSpec(mod.shape, const),
                   pl.BlockSpec((1, MIX_HALF), const),
                   pl.BlockSpec((1, D_MODEL), const),
                   pl.BlockSpec((D_MODEL, LANES), const),
                   pl.BlockSpec((D_MODEL, LANES), const),
                   pl.BlockSpec((1, LANES), const)])
    return pl.pallas_call(
        _outproj_kernel,
        grid=(t // tm,),
        in_specs=in_specs,
        out_specs=[pl.BlockSpec((tm, D_MODEL), row), pl.BlockSpec((tm, D_MODEL), row),
                   pl.BlockSpec((tm, LANES), row)],
        out_shape=[jax.ShapeDtypeStruct((t, D_MODEL), F32),
                   jax.ShapeDtypeStruct((t, D_MODEL), BF16),
                   jax.ShapeDtypeStruct((t, LANES), F32)],
        compiler_params=_cparams(("arbitrary",)),
        name="outproj_route",
    )(p, hf, hb, uo, x2d, w_out, mod, head_g, norm2_g, wr_pack, wr_hi, b_route)


DISPATCH_BLOCK = 512
GRANULE = 16
GRANULE_SHIFT = 4
LOCAL_CAP = 1536
LOCAL_GRANULES = LOCAL_CAP // GRANULE
FREE_GRANULES = 2
EXPERT_TILE = 256
TILE_GRANULES = EXPERT_TILE // GRANULE
PLAN_UNROLL = 4
PLAN_SLACK = 8
GATHER_DEPTH = 3


def _dispatch_kernel(fn_ref, info_ref, xs_o, ws_o, pt_o, cnt_o):
    tb = DISPATCH_BLOCK
    info = info_ref[...]
    e1, e2 = info[:, 0:1], info[:, 1:2]
    w1c, w2c = info[:, 2:3], info[:, 3:4]
    lane = lax.broadcasted_iota(jnp.int32, (tb, LANES), 1).astype(F32)
    o1 = jnp.where(lane == e1, 1.0, 0.0)
    o2 = jnp.where(lane == e2, 1.0, 0.0)
    onehot = o1 + o2
    cnt = jnp.sum(onehot, axis=0, keepdims=True)
    gran = jnp.floor((cnt + (GRANULE - 1)) * (1.0 / GRANULE))
    a = lax.broadcasted_iota(jnp.int32, (LANES, LANES), 0)
    b = lax.broadcasted_iota(jnp.int32, (LANES, LANES), 1)
    upper = jnp.where(a < b, 1.0, 0.0).astype(BF16)
    seg_off = _dot(jnp.broadcast_to(gran, (8, LANES)).astype(BF16), upper)[0:1, :] * GRANULE
    r = lax.broadcasted_iota(jnp.int32, (tb, tb), 0)
    c = lax.broadcasted_iota(jnp.int32, (tb, tb), 1)
    strict = jnp.where(r > c, 1.0, 0.0).astype(BF16)
    rank = _dot(strict, onehot.astype(BF16))
    slot = rank + seg_off
    pos1 = jnp.sum(o1 * slot, axis=1, keepdims=True)
    pos2 = jnp.sum(o2 * slot, axis=1, keepdims=True)
    rows = lax.broadcasted_iota(jnp.int32, (tb, LOCAL_CAP), 1).astype(F32)
    pt1 = jnp.where(rows == pos1, 1.0, 0.0)
    pt2 = jnp.where(rows == pos2, 1.0, 0.0)
    pt = pt1 + pt2
    pt_o[...] = pt.astype(BF16)
    perm = pt.T.astype(BF16)
    w_slot = jnp.sum((pt1 * w1c + pt2 * w2c).T, axis=1, keepdims=True)
    for c0 in range(0, D_MODEL, 512):
        xs_o[:, c0:c0 + 512] = _dot(perm, fn_ref[:, c0:c0 + 512]).astype(BF16)
    ws_o[...] = jnp.broadcast_to(w_slot, (LOCAL_CAP, LANES))
    cnt_o[...] = jnp.broadcast_to(cnt, (8, LANES)).astype(jnp.int32)


def _dispatch(fn, info):
    t = fn.shape[0]
    nb = t // DISPATCH_BLOCK
    row = lambda i: (i, 0)
    return pl.pallas_call(
        _dispatch_kernel,
        grid=(nb,),
        in_specs=[pl.BlockSpec((DISPATCH_BLOCK, D_MODEL), row),
                  pl.BlockSpec((DISPATCH_BLOCK, LANES), row)],
        out_specs=[pl.BlockSpec((LOCAL_CAP, D_MODEL), row),
                   pl.BlockSpec((LOCAL_CAP, LANES), row),
                   pl.BlockSpec((DISPATCH_BLOCK, LOCAL_CAP), row),
                   pl.BlockSpec((8, LANES), row)],
        out_shape=[jax.ShapeDtypeStruct((nb * LOCAL_CAP, D_MODEL), BF16),
                   jax.ShapeDtypeStruct((nb * LOCAL_CAP, LANES), F32),
                   jax.ShapeDtypeStruct((t, LOCAL_CAP), BF16),
                   jax.ShapeDtypeStruct((nb * 8, LANES), jnp.int32)],
        compiler_params=_cparams(("arbitrary",)),
        name="moe_dispatch",
    )(fn, info)


def _free_granule(q):
    return ((q // FREE_GRANULES) * LOCAL_GRANULES + (LOCAL_GRANULES - FREE_GRANULES)
            + (q % FREE_GRANULES))


def _plan_kernel(cnt_ref, gsrc_o, gdst_o, texp_o, tend_o, ntile_o, lrun, *, nb, max_tiles):
    def init(b, c):
        lrun[b] = 0
        return c

    lax.fori_loop(0, nb, init, 0)

    def per_expert(e, carry):
        g0, last_e = carry

        def per_block(b, g):
            k = (cnt_ref[b, e] + (GRANULE - 1)) >> GRANULE_SHIFT
            lo = lrun[b]
            lrun[b] = lo + k

            base = b * LOCAL_GRANULES + lo

            for j in range(PLAN_UNROLL):
                gsrc_o[g + j] = base + j
                gdst_o[g + j] = base + j

            @pl.when(k > PLAN_UNROLL)
            def _():
                def put(j, c):
                    gsrc_o[g + j] = base + j
                    gdst_o[g + j] = base + j
                    return c

                lax.fori_loop(PLAN_UNROLL, k, put, 0)

            return g + k

        g1 = lax.fori_loop(0, nb, per_block, g0)
        pad = (-g1) & (TILE_GRANULES - 1)

        def put_pad(j, c):
            g = g1 + j
            parity = (g // TILE_GRANULES) & 1
            gsrc_o[g] = _free_granule(0)
            gdst_o[g] = _free_granule(1 + parity * (TILE_GRANULES - 1) + (g & (TILE_GRANULES - 1)))
            return c

        lax.fori_loop(0, pad, put_pad, 0)
        g2 = g1 + pad

        def put_tile(tt, c):
            texp_o[tt] = e
            return c

        lax.fori_loop(g0 // TILE_GRANULES, g2 // TILE_GRANULES, put_tile, 0)
        tend_o[e] = g2 // TILE_GRANULES
        return g2, jnp.where(g2 > g0, e, last_e)

    g_end, last_e = lax.fori_loop(0, N_EXPERTS, per_expert, (0, 0))
    n_tiles = g_end // TILE_GRANULES
    ntile_o[0] = n_tiles

    def fill(tt, c):
        texp_o[tt] = last_e
        return c

    lax.fori_loop(n_tiles, max_tiles, fill, 0)

    def fill_map(g, c):
        gsrc_o[g] = _free_granule(0)
        gdst_o[g] = _free_granule(0)
        return c

    lax.fori_loop(g_end, max_tiles * TILE_GRANULES + PLAN_SLACK, fill_map, 0)


def _max_tiles(t):
    nb = t // DISPATCH_BLOCK
    worst_rows = 2 * t + nb * N_EXPERTS * (GRANULE - 1) + N_EXPERTS * (EXPERT_TILE - GRANULE)
    return -(-worst_rows // EXPERT_TILE)


def _plan(cnt, t):
    nb = cnt.shape[0]
    assert nb * FREE_GRANULES >= 2 + 2 * (TILE_GRANULES - 1)
    max_tiles = _max_tiles(t)
    smem = pl.BlockSpec(memory_space=pltpu.SMEM)
    n_map = max_tiles * TILE_GRANULES + PLAN_SLACK
    return pl.pallas_call(
        functools.partial(_plan_kernel, nb=nb, max_tiles=max_tiles),
        in_specs=[smem],
        out_specs=[smem, smem, smem, smem, smem],
        out_shape=[jax.ShapeDtypeStruct((n_map,), jnp.int32),
                   jax.ShapeDtypeStruct((n_map,), jnp.int32),
                   jax.ShapeDtypeStruct((max_tiles,), jnp.int32),
                   jax.ShapeDtypeStruct((N_EXPERTS,), jnp.int32),
                   jax.ShapeDtypeStruct((1,), jnp.int32)],
        scratch_shapes=[pltpu.SMEM((nb,), jnp.int32)],
        name="moe_plan",
    )(cnt)


def _experts_kernel(gsrc, gdst, texp, tend, ntile, xy_in, ws_in, w1_hbm, w3_hbm, w2_hbm, xy_out,
                    xbuf, wsbuf, ybuf, st1, st3, st2, wb1, wb3, wb2, gsem, ssem, wsem):
    nt = ntile[0]

    def weight_copies(e, ws):
        return (pltpu.make_async_copy(w1_hbm.at[e], st1.at[ws], wsem.at[ws]),
                pltpu.make_async_copy(w3_hbm.at[e], st3.at[ws], wsem.at[ws]),
                pltpu.make_async_copy(w2_hbm.at[e], st2.at[ws], wsem.at[ws]))

    def rows(i):
        return pl.ds(pl.multiple_of(i * GRANULE, GRANULE), GRANULE)

    def gather_copies(g, j, sl):
        return (pltpu.make_async_copy(xy_in.at[rows(g), :], xbuf.at[sl, rows(j), :], gsem.at[sl]),
                pltpu.make_async_copy(ws_in.at[rows(g), :], wsbuf.at[sl, rows(j), :], gsem.at[sl]))

    def scatter_copies(g, j, sl):
        return (pltpu.make_async_copy(ybuf.at[sl, rows(j), :], xy_out.at[rows(g), :], ssem.at[sl]),)

    def issue(tt, sl, gmap, copies):
        for j in range(TILE_GRANULES):
            for cp in copies(gmap[tt * TILE_GRANULES + j], j, sl):
                cp.start()

    def drain(sl, copies):
        full = pl.ds(0, EXPERT_TILE)
        if copies is gather_copies:
            pltpu.make_async_copy(xy_in.at[full, :], xbuf.at[sl], gsem.at[sl]).wait()
            pltpu.make_async_copy(ws_in.at[full, :], wsbuf.at[sl], gsem.at[sl]).wait()
        else:
            pltpu.make_async_copy(ybuf.at[sl], xy_out.at[full, :], ssem.at[sl]).wait()

    @pl.when(nt > 0)
    def _():
        for cp in weight_copies(texp[0], 0):
            cp.start()

    for ahead in range(GATHER_DEPTH - 1):
        @pl.when(ahead < nt)
        def _():
            issue(ahead, ahead, gsrc, gather_copies)

    def tile(t, wslot):
        slot = t % 2
        gslot = t % GATHER_DEPTH
        e = texp[t]
        first = (t == 0) | (texp[jnp.maximum(t - 1, 0)] != e)
        wslot = jnp.where(first & (t > 0), 1 - wslot, wslot)

        drain(gslot, gather_copies)

        @pl.when(t + GATHER_DEPTH - 1 < nt)
        def _():
            issue(t + GATHER_DEPTH - 1, (t + GATHER_DEPTH - 1) % GATHER_DEPTH, gsrc, gather_copies)

        @pl.when(first)
        def _():
            for cp in weight_copies(e, wslot):
                cp.wait()
            wb1[...] = st1[wslot].astype(BF16)
            wb3[...] = st3[wslot].astype(BF16)
            wb2[...] = st2[wslot].astype(BF16)
            nxt = tend[e]

            @pl.when(nxt < nt)
            def _():
                for cp in weight_copies(texp[jnp.minimum(nxt, nt - 1)], 1 - wslot):
                    cp.start()

        x = xbuf[gslot]
        w_row = wsbuf[gslot][:, 0:1]
        a = _dot(x, wb1[...])
        b = _dot(x, wb3[...])
        y = _dot((_silu(a) * b).astype(BF16), wb2[...]) * w_row

        @pl.when(t >= 2)
        def _():
            drain(slot, scatter_copies)

        ybuf[slot] = y.astype(BF16)
        issue(t, slot, gdst, scatter_copies)
        return wslot

    lax.fori_loop(0, nt, tile, 0)

    @pl.when(nt >= 1)
    def _():
        drain((nt - 1) % 2, scatter_copies)

    @pl.when(nt >= 2)
    def _():
        drain(nt % 2, scatter_copies)


def _experts(gsrc, gdst, texp, tend, ntile, xy, ws, w1, w3, w2):
    smem = pl.BlockSpec(memory_space=pltpu.SMEM)
    hbm = pl.BlockSpec(memory_space=pl.ANY)
    return pl.pallas_call(
        _experts_kernel,
        in_specs=[smem, smem, smem, smem, smem, hbm, hbm, hbm, hbm, hbm],
        out_specs=hbm,
        out_shape=jax.ShapeDtypeStruct(xy.shape, xy.dtype),
        scratch_shapes=[pltpu.VMEM((GATHER_DEPTH, EXPERT_TILE, D_MODEL), BF16),
                        pltpu.VMEM((GATHER_DEPTH, EXPERT_TILE, LANES), F32),
                        pltpu.VMEM((2, EXPERT_TILE, D_MODEL), BF16),
                        pltpu.VMEM((2, D_MODEL, D_EXPERT), F32),
                        pltpu.VMEM((2, D_MODEL, D_EXPERT), F32),
                        pltpu.VMEM((2, D_EXPERT, D_MODEL), F32),
                        pltpu.VMEM((D_MODEL, D_EXPERT), BF16),
                        pltpu.VMEM((D_MODEL, D_EXPERT), BF16),
                        pltpu.VMEM((D_EXPERT, D_MODEL), BF16),
                        pltpu.SemaphoreType.DMA((GATHER_DEPTH,)),
                        pltpu.SemaphoreType.DMA((2,)),
                        pltpu.SemaphoreType.DMA((2,))],
        input_output_aliases={5: 0},
        compiler_params=pltpu.CompilerParams(vmem_limit_bytes=VMEM_LIMIT),
        name="moe_experts",
    )(gsrc, gdst, texp, tend, ntile, xy, ws, w1, w3, w2)


def _combine_kernel(y_ref, pt_ref, h1_ref, g2_ref, fg_ref, o_ref):
    moe = _dot(pt_ref[...], y_ref[...])
    h = h1_ref[...] + g2_ref[...] * moe
    ms = jnp.mean(h * h, axis=-1, keepdims=True)
    o_ref[...] = h * lax.rsqrt(ms + EPS) * fg_ref[...]


def _combine(xy, pt, h1, g2, final_g):
    t = h1.shape[0]
    row = lambda i: (i, 0)
    const = lambda i: (0, 0)
    return pl.pallas_call(
        _combine_kernel,
        grid=(t // DISPATCH_BLOCK,),
        in_specs=[pl.BlockSpec((LOCAL_CAP, D_MODEL), row),
                  pl.BlockSpec((DISPATCH_BLOCK, LOCAL_CAP), row),
                  pl.BlockSpec((DISPATCH_BLOCK, D_MODEL), row),
                  pl.BlockSpec((1, D_MODEL), const), pl.BlockSpec((1, D_MODEL), const)],
        out_specs=pl.BlockSpec((DISPATCH_BLOCK, D_MODEL), row),
        out_shape=jax.ShapeDtypeStruct((t, D_MODEL), F32),
        compiler_params=_cparams(("arbitrary",)),
        name="moe_combine_final",
    )(xy, pt, h1, g2, final_g)


def _pad_lanes(a):
    return jnp.pad(a, ((0, 0), (0, LANES - a.shape[1])))


def kernel(x, c, ctx, c_ctx, w_mod, b_mod, norm1_g, w_in, w_conv_q, w_conv_k, gate_bias, head_norm_g, w_pool, pool_scale, w_out, norm2_g, w_group, b_group, w_router, b_router, w1, w3, w2, final_g):
    assert x.shape[0] == 1 and w_mod.shape[0] == 1
    seq = x.shape[1]
    x2d = x[0]
    ctx2d = ctx[0]

    n_main = 5 * MIX_HALF
    w_in_t = jnp.transpose(w_in[0])
    w_main = _cast_transposed_bf16(w_in_t, n_main)
    wg_pack, wg_hi = _split_pack(jnp.transpose(_take_rows(w_in_t, n_main, N_GATES)))
    gate_bias_row = _pad_lanes(gate_bias[0].reshape(1, N_GATES))
    w_out_bf = _cast_bf16(w_out[0])
    wr_pack, wr_hi = _split_pack(jnp.concatenate([w_group[0], w_router[0]], axis=1))
    b_route = _pad_lanes(jnp.concatenate([b_group[0], b_router[0]]).reshape(1, -1))
    norm1 = norm1_g[0].reshape(1, D_MODEL)

    c16 = jnp.zeros((16, D_MODEL), F32).at[0].set(c[0]).at[1].set(c_ctx)
    mods = _adaln(c16, w_mod[0], b_mod[0].reshape(1, -1))
    mod_lat = mods[0].reshape(6, D_MODEL)
    mod_ctx = mods[1].reshape(6, D_MODEL)

    proj = functools.partial(_inproj, g=norm1, w_main=w_main, wg_pack=wg_pack, wg_hi=wg_hi,
                             gate_bias_row=gate_bias_row, wcq=w_conv_q[0], wck=w_conv_k[0])
    _, _, _, k_c, kt_c, v_c, col_c, row_c = proj(ctx2d, mod_ctx[0:2])
    zeros_state = (jnp.zeros((2 * HEADS, HEAD_DIM, HEAD_DIM), F32),
                   jnp.zeros((2 * HEADS, 1, HEAD_DIM), F32),
                   jnp.zeros((2 * HEADS, 1, LANES), F32))
    s0, n0, m0 = _mlstm(None, k_c, kt_c, v_c, col_c, row_c, *zeros_state, need_out=False)

    u_pool, q, uo, k, kt, v, col, rowi = proj(x2d, mod_lat[0:2])
    hf, hb = _mlstm(q, k, kt, v, col, rowi, s0, n0, m0, need_out=True)
    p = _pool(u_pool, w_pool[0], pool_scale[0].reshape(1, -1))
    h1, fn, info = _outproj(p, hf, hb, uo, x2d, w_out_bf, mod_lat, head_norm_g[0].reshape(1, -1),
                            norm2_g[0].reshape(1, -1), wr_pack, wr_hi, b_route, tm=256)

    xs, ws, pt, cnt = _dispatch(fn, info)
    gsrc, gdst, texp, tend, ntile = _plan(cnt[::8, :N_EXPERTS], seq)
    xy = _experts(gsrc, gdst, texp, tend, ntile, xs, ws, w1[0], w3[0], w2[0])
    out = _combine(xy, pt, h1, mod_lat[5:6], final_g.reshape(1, -1))
    return out.reshape(1, seq, D_MODEL)
```

```python
import functools

import jax
import jax.numpy as jnp
from jax import lax
from jax.experimental import pallas as pl
from jax.experimental.pallas import tpu as pltpu

F32 = jnp.float32
BF16 = jnp.bfloat16

D_MODEL = 2048
GRID_W = 64
GRID_SHIFT = 6
POOL_WINDOWS = (2, 4, 8, 16)
POOL_GROUP = 256
HEADS = 4
HEAD_DIM = 256
MIX_HALF = 1024
N_GATES = 16
N_GROUPS = 4
EXPERTS_PER_GROUP = 8
N_EXPERTS = 32
D_EXPERT = 512
EPS = 1e-6
LANES = 128
CHUNK = 256
MLSTM_ROW_BLOCK = 128
ROUTE_LANE0 = N_GROUPS

VMEM_LIMIT = 56 * 1024 * 1024


def _cparams(sem, vmem=VMEM_LIMIT):
    return pltpu.CompilerParams(dimension_semantics=sem, vmem_limit_bytes=vmem)


def _split2(x):
    hi = x.astype(BF16)
    lo = (x - hi.astype(F32)).astype(BF16)
    return hi, lo


def _split3(x):
    hi = x.astype(BF16)
    r = x - hi.astype(F32)
    mid = r.astype(BF16)
    lo = (r - mid.astype(F32)).astype(BF16)
    return hi, mid, lo


def _dot(a, b):
    return jnp.dot(a, b, preferred_element_type=F32)


SPLIT_LANE = 64


def _split_pack(w):
    hi, lo = _split2(w)
    n = w.shape[1]
    packed = jnp.zeros((w.shape[0], LANES), BF16).at[:, :n].set(hi).at[:, SPLIT_LANE:SPLIT_LANE + n].set(lo)
    return packed, jnp.pad(hi, ((0, 0), (0, LANES - n)))


def _split_dot(xh, xl, w_packed, w_hi):
    r = _dot(xh, w_packed)
    return r + pltpu.roll(r, SPLIT_LANE, 1) + _dot(xl, w_hi)


def _silu(x):
    return x * jax.nn.sigmoid(x)


def _log_sigmoid(x):
    return jnp.minimum(x, 0.0) - jnp.log(1.0 + jnp.exp(-jnp.abs(x)))


def _cast_kernel(w_ref, o_ref):
    o_ref[...] = w_ref[...].astype(BF16)


def _cast_bf16(w, rows=256):
    k, n = w.shape
    return pl.pallas_call(
        _cast_kernel,
        grid=(k // rows,),
        in_specs=[pl.BlockSpec((rows, n), lambda i: (i, 0))],
        out_specs=pl.BlockSpec((rows, n), lambda i: (i, 0)),
        out_shape=jax.ShapeDtypeStruct((k, n), BF16),
        compiler_params=_cparams(("arbitrary",)),
        name="cast_bf16",
    )(w)


def _copy_kernel(w_ref, o_ref):
    o_ref[...] = w_ref[...]


def _take_rows(w, start, n):
    return pl.pallas_call(
        _copy_kernel,
        grid=(1,),
        in_specs=[pl.BlockSpec((n, w.shape[1]), lambda i: (start // n, 0))],
        out_specs=pl.BlockSpec((n, w.shape[1]), lambda i: (0, 0)),
        out_shape=jax.ShapeDtypeStruct((n, w.shape[1]), w.dtype),
        name="take_rows",
    )(w)


def _cast_t_kernel(wt_ref, o_ref):
    o_ref[...] = wt_ref[...].T.astype(BF16)


def _cast_transposed_bf16(wt, n_cols, cols=256):
    k = wt.shape[1]
    return pl.pallas_call(
        _cast_t_kernel,
        grid=(n_cols // cols,),
        in_specs=[pl.BlockSpec((cols, k), lambda i: (i, 0))],
        out_specs=pl.BlockSpec((k, cols), lambda i: (0, i)),
        out_shape=jax.ShapeDtypeStruct((k, n_cols), BF16),
        compiler_params=_cparams(("arbitrary",)),
        name="cast_transposed_bf16",
    )(wt)


def _adaln_kernel(c_ref, w_ref, b_ref, o_ref):
    a = _silu(c_ref[...])
    a3 = jnp.concatenate(_split3(a), axis=0)
    w_hi, w_lo = _split2(w_ref[...])
    acc = _dot(a3, w_hi)
    acc_lo = _dot(a3[:32], w_lo)
    out = acc[0:16] + acc[16:32] + acc[32:48] + acc_lo[0:16] + acc_lo[16:32]
    o_ref[...] = out + b_ref[...]


def _adaln(c16, w_mod, b_mod):
    n = w_mod.shape[1]
    tn = 1536
    return pl.pallas_call(
        _adaln_kernel,
        grid=(n // tn,),
        in_specs=[pl.BlockSpec((16, D_MODEL), lambda j: (0, 0)),
                  pl.BlockSpec((D_MODEL, tn), lambda j: (0, j)),
                  pl.BlockSpec((1, tn), lambda j: (0, j))],
        out_specs=pl.BlockSpec((16, tn), lambda j: (0, j)),
        out_shape=jax.ShapeDtypeStruct((16, n), F32),
        compiler_params=_cparams(("arbitrary",)),
        name="adaln",
    )(c16, w_mod, b_mod)


HALO = 8
NEG_INF = float("-inf")


def _gate_scan_info(gates):
    n = gates.shape[0]
    lane = lax.broadcasted_iota(jnp.int32, gates.shape, 1)
    rows = lax.broadcasted_iota(jnp.int32, gates.shape, 0)
    lf = jnp.where(lane < N_GATES, _log_sigmoid(gates), 0.0)
    hi = lf.astype(BF16).astype(F32)
    rem = lf - hi
    mid = rem.astype(BF16).astype(F32)
    packed = (hi + pltpu.roll(mid, 32, 1) + pltpu.roll(rem - mid, 64, 1)).astype(BF16)
    r = lax.broadcasted_iota(jnp.int32, (n, n), 0)
    c = lax.broadcasted_iota(jnp.int32, (n, n), 1)
    pf = _dot(jnp.where(r >= c, 1.0, 0.0).astype(BF16), packed)
    pb = _dot(jnp.where(r <= c, 1.0, 0.0).astype(BF16), packed)
    bf = pf + pltpu.roll(pf, 96, 1) + pltpu.roll(pf, 64, 1)
    bb = pb + pltpu.roll(pb, 96, 1) + pltpu.roll(pb, 64, 1)
    b = jnp.where(lane < 8, bf, bb)
    cval = gates - pltpu.roll(b, LANES - 4, 1)
    pm = cval
    sm = cval
    step = 1
    while step < n:
        pm = jnp.maximum(pm, jnp.where(rows >= step, pltpu.roll(pm, step, 0), NEG_INF))
        sm = jnp.maximum(sm, jnp.where(rows < n - step, pltpu.roll(sm, n - step, 0), NEG_INF))
        step *= 2
    cm = jnp.where(lane < 8, pm, sm)
    return cval, jnp.where((lane & 4) == 0, cm, b)


def _inproj_kernel(x_ref, xp_ref, xn_ref, mod_ref, g_ref, w_ref, wgp_ref, wgh_ref, gb_ref,
                   wcq_ref, wck_ref, pool_o, q_o, o_o, k_o, kt_o, v_o, col_o, row_o, *, nt):
    i = pl.program_id(0)
    tm = x_ref.shape[0]
    x_all = jnp.concatenate([xp_ref[...], x_ref[...], xn_ref[...]], axis=0)
    ms = jnp.mean(x_all * x_all, axis=-1, keepdims=True)
    y = x_all * lax.rsqrt(ms + EPS) * g_ref[...]
    xn_all = y * (1.0 + mod_ref[1:2, :]) + mod_ref[0:1, :]
    xh_all = xn_all.astype(BF16)
    xh, xl = _split2(xn_all[HALO:HALO + tm])

    def cols(ci):
        return w_ref[:, ci * MIX_HALF:(ci + 1) * MIX_HALF]

    u_q = _dot(xh_all, cols(1))
    u_k = _dot(xh_all, cols(3))
    pool_o[...] = _dot(xh, cols(0))
    o_o[...] = _dot(xh, cols(2))
    v_o[...] = _dot(xh, cols(4)).astype(BF16)

    rowi = lax.broadcasted_iota(jnp.int32, (tm, MIX_HALF), 0)
    at_start = jnp.logical_and(rowi == 0, i == 0)
    at_end = jnp.logical_and(rowi == tm - 1, i == nt - 1)

    def conv_silu(u, wc_ref):
        n = u.shape[0]
        up = jnp.where(at_start, 0.0, pltpu.roll(u, 1, 0)[HALO:HALO + tm])
        un = jnp.where(at_end, 0.0, pltpu.roll(u, n - 1, 0)[HALO:HALO + tm])
        return _silu(wc_ref[0:1, :] * up + wc_ref[1:2, :] * u[HALO:HALO + tm] + wc_ref[2:3, :] * un)

    q_o[...] = (conv_silu(u_q, wcq_ref) * (HEAD_DIM ** -0.5)).astype(BF16)
    k = conv_silu(u_k, wck_ref)
    k_o[...] = k.astype(BF16)
    kt_o[...] = k.T.astype(BF16)
    gates = _split_dot(xh, xl, wgp_ref[...], wgh_ref[...]) + gb_ref[...]
    cval, col = _gate_scan_info(gates)
    col_o[...] = col[:, :N_GATES]
    row_o[...] = cval.T[:N_GATES, :]


def _inproj(x2d, mod, g, w_main, wg_pack, wg_hi, gate_bias_row, wcq, wck):
    t = x2d.shape[0]
    tm = CHUNK
    nt = t // tm
    r8 = tm // HALO
    last8 = t // HALO - 1
    const = lambda i: (0, 0)
    row = lambda i: (i, 0)
    f32_out = jax.ShapeDtypeStruct((t, MIX_HALF), F32)
    bf16_out = jax.ShapeDtypeStruct((t, MIX_HALF), BF16)
    seq = pl.BlockSpec((tm, MIX_HALF), row)
    return pl.pallas_call(
        functools.partial(_inproj_kernel, nt=nt),
        grid=(nt,),
        in_specs=[pl.BlockSpec((tm, D_MODEL), row),
                  pl.BlockSpec((HALO, D_MODEL), lambda i: (jnp.maximum(i * r8 - 1, 0), 0)),
                  pl.BlockSpec((HALO, D_MODEL), lambda i: (jnp.minimum((i + 1) * r8, last8), 0)),
                  pl.BlockSpec((2, D_MODEL), const),
                  pl.BlockSpec((1, D_MODEL), const),
                  pl.BlockSpec(w_main.shape, const, pipeline_mode=pl.Buffered(1)),
                  pl.BlockSpec((D_MODEL, LANES), const),
                  pl.BlockSpec((D_MODEL, LANES), const),
                  pl.BlockSpec((1, LANES), const),
                  pl.BlockSpec((3, MIX_HALF), const),
                  pl.BlockSpec((3, MIX_HALF), const)],
        out_specs=[seq, seq, seq, seq, pl.BlockSpec((MIX_HALF, tm), lambda i: (0, i)), seq,
                   pl.BlockSpec((tm, N_GATES), row), pl.BlockSpec((N_GATES, tm), lambda i: (0, i))],
        out_shape=[f32_out, bf16_out, f32_out, bf16_out,
                   jax.ShapeDtypeStruct((MIX_HALF, t), BF16), bf16_out,
                   jax.ShapeDtypeStruct((t, N_GATES), F32),
                   jax.ShapeDtypeStruct((N_GATES, t), F32)],
        compiler_params=_cparams(("arbitrary",)),
        name="inproj",
    )(x2d, x2d, x2d, mod, g, w_main, wg_pack, wg_hi, gate_bias_row, wcq, wck)


def _mlstm_kernel(*refs, need_out):
    if need_out:
        (qf, kf, ktf, vf, colf, rowf, qb, kb, ktb, vb, colb, rowb, s0, n0, m0,
         hf_o, hb_o, s_s, n_s, m_s) = refs
        q_refs, h_outs = (qf, qb), (hf_o, hb_o)
    else:
        (kf, ktf, vf, colf, rowf, kb, ktb, vb, colb, rowb, s0, n0, m0,
         s_o, n_o, m_o, s_s, n_s, m_s) = refs
    k_refs, kt_refs, v_refs = (kf, kb), (ktf, ktb), (vf, vb)
    col_refs, row_refs = (colf, colb), (rowf, rowb)
    j = pl.program_id(0)

    @pl.when(j == 0)
    def _():
        s_s[...] = s0[...]
        n_s[...] = n0[...]
        m_s[...] = m0[...]

    L = CHUNK
    row = lax.broadcasted_iota(jnp.int32, (L, L), 0)
    col = lax.broadcasted_iota(jnp.int32, (L, L), 1)

    heads = [(d, h) for d in range(2) for h in range(HEADS)]

    def head_values(d, h):
        hd = d * HEADS + h
        lc, lb = 8 * d + h, 8 * d + 4 + h
        hs = slice(h * HEAD_DIM, (h + 1) * HEAD_DIM)
        edge = L - 1 if d == 0 else 0
        colv = col_refs[d][...]
        rowv = row_refs[d][...]
        v = dict(hd=hd, hs=hs, d=d)
        v["cm_c"], v["b_c"] = colv[:, lc:lc + 1], colv[:, lb:lb + 1]
        v["c_r"] = rowv[lc:lc + 1, :]
        v["g"] = v["b_c"][edge:edge + 1, :]
        v["m_old"] = m_s[hd][:, 0:1]
        v["s_old"] = s_s[hd]
        v["n_old"] = n_s[hd]
        v["m_x"] = jnp.maximum(v["m_old"], v["cm_c"][edge:edge + 1, :])
        v["decay"] = jnp.exp(v["m_old"] - v["m_x"])
        v["wk_r"] = jnp.exp(v["c_r"] - v["m_x"])
        v["v_h"] = v_refs[d][:, hs]
        v["kt_h"] = kt_refs[d][hs, :]
        return v

    def update_state(v):
        hd = v["hd"]
        n_s[hd] = v["decay"] * v["n_old"] + _dot(
            jnp.broadcast_to(v["wk_r"], (8, L)).astype(BF16), k_refs[v["d"]][:, v["hs"]])[0:1, :]
        kwt = (v["kt_h"].astype(F32) * v["wk_r"]).astype(BF16)
        s_s[hd] = v["decay"] * v["s_old"] + _dot(kwt, v["v_h"])
        m_s[hd] = jnp.broadcast_to(v["g"] + v["m_x"], (1, LANES))

    def weights(v):
        d, hs = v["d"], v["hs"]
        mask = (row >= col) if d == 0 else (row <= col)
        q_h = q_refs[d][:, hs]
        m_c = jnp.maximum(v["m_old"], v["cm_c"])
        w_inter = jnp.exp(v["m_old"] - m_c)
        v["e"] = jnp.where(mask, jnp.exp(v["c_r"] - m_c), 0.0)
        v["qw"] = (q_h.astype(F32) * w_inter).astype(BF16)
        qn = lax.dot_general(q_h, jnp.broadcast_to(v["n_old"], (8, HEAD_DIM)).astype(BF16),
                             (((1,), (1,)), ((), ())), preferred_element_type=F32)
        v["den_inter"] = w_inter * qn[:, 0:1]
        v["floor"] = jnp.exp(-(v["b_c"] + m_c))
        v["s_bf"] = v["s_old"].astype(BF16)

    def outputs(v, qk):
        p = v["e"] * qk
        num = _dot(v["qw"], v["s_bf"]) + _dot(p.astype(BF16), v["v_h"])
        den = v["den_inter"] + jnp.sum(p[:, :LANES] + p[:, LANES:], axis=1, keepdims=True)
        h_outs[v["d"]][:, v["hs"]] = num / jnp.maximum(jnp.abs(den), v["floor"])

    vals = [head_values(d, h) for d, h in heads]
    if need_out:
        qks = [_dot(q_refs[v["d"]][:, v["hs"]], v["kt_h"]) for v in vals]
        for v in vals:
            weights(v)
    for v in vals:
        update_state(v)
    if need_out:
        for v, qk in zip(vals, qks):
            outputs(v, qk)

    if not need_out:
        s_o[...] = s_s[...]
        n_o[...] = n_s[...]
        m_o[...] = m_s[...]


def _mlstm(q, k, kt, v, col, rowi, s0, n0, m0, need_out):
    t = k.shape[0]
    nc = t // CHUNK
    fwd = lambda j: (j, 0)
    bwd = lambda j: (nc - 1 - j, 0)
    fwd_t = lambda j: (0, j)
    bwd_t = lambda j: (0, nc - 1 - j)
    c3 = lambda j: (0, 0, 0)
    seq = lambda im: pl.BlockSpec((CHUNK, MIX_HALF), im)
    state_specs = [pl.BlockSpec(s0.shape, c3), pl.BlockSpec(n0.shape, c3), pl.BlockSpec(m0.shape, c3)]
    scratch = [pltpu.VMEM(s0.shape, F32), pltpu.VMEM(n0.shape, F32), pltpu.VMEM(m0.shape, F32)]

    def side(im, im_t):
        specs = ([seq(im)] if need_out else []) + [seq(im), pl.BlockSpec((MIX_HALF, CHUNK), im_t), seq(im)]
        return specs + [pl.BlockSpec((CHUNK, N_GATES), im), pl.BlockSpec((N_GATES, CHUNK), im_t)]

    in_specs = side(fwd, fwd_t) + side(bwd, bwd_t) + state_specs
    seq_in = ((q,) if need_out else ()) + (k, kt, v, col, rowi)
    args = seq_in + seq_in + (s0, n0, m0)
    if need_out:
        out_specs = [seq(fwd), seq(bwd)]
        out_shape = [jax.ShapeDtypeStruct((t, MIX_HALF), F32)] * 2
    else:
        out_specs = state_specs
        out_shape = [jax.ShapeDtypeStruct(a.shape, F32) for a in (s0, n0, m0)]
    return pl.pallas_call(
        functools.partial(_mlstm_kernel, need_out=need_out),
        grid=(nc,),
        in_specs=in_specs,
        out_specs=out_specs,
        out_shape=out_shape,
        scratch_shapes=scratch,
        compiler_params=_cparams(("arbitrary",)),
        name="mlstm_out" if need_out else "mlstm_state",
    )(*args)


POOL_PAD = 512


def _pool_kernel(u_ref, w_ref, sc_ref, o_ref, pad_s, *, t):
    for gi, win in enumerate(POOL_WINDOWS):
        @pl.when(pl.program_id(0) == gi)
        def _():
            _pool_group(u_ref, w_ref, sc_ref, o_ref, pad_s, win=win, t=t)


def _pool_group(u_ref, w_ref, sc_ref, o_ref, pad_s, *, win, t):
    half = win // 2
    tile = 256
    zeros = jnp.zeros((POOL_PAD, POOL_GROUP), F32)
    pad_s[0:POOL_PAD, :] = zeros
    pad_s[POOL_PAD + t:POOL_PAD + t + POOL_PAD, :] = zeros

    def copy(r, carry):
        t0 = pl.multiple_of(r * tile, tile)
        pad_s[pl.ds(POOL_PAD + t0, tile), :] = u_ref[pl.ds(t0, tile), :]
        return carry

    lax.fori_loop(0, t // tile, copy, 0)

    row = lax.broadcasted_iota(jnp.int32, (tile, tile), 0)
    col = lax.broadcasted_iota(jnp.int32, (tile, tile), 1)
    same_row = (row >> GRID_SHIFT) == (col >> GRID_SHIFT)
    in_win = (col - row >= -half) & (col - row < half)
    band = jnp.where(same_row & in_win, 1.0, 0.0).astype(BF16)
    w = w_ref[...].astype(BF16)
    scale = sc_ref[...]
    n_rows = t // GRID_W

    def body(r, carry):
        t0 = pl.multiple_of(r * tile, tile)
        acc = jnp.zeros((tile, POOL_GROUP), F32)
        for dd in range(-half, half):
            acc = acc + pad_s[pl.ds(POOL_PAD + t0 + GRID_W * dd, tile), :]
        tok = t0 + lax.broadcasted_iota(jnp.int32, (tile, POOL_GROUP), 0)
        gr = tok >> GRID_SHIFT
        gc = tok & (GRID_W - 1)
        cnt_v = jnp.minimum(gr + half, n_rows) - jnp.maximum(gr - half, 0)
        cnt_h = jnp.minimum(gc + half, GRID_W) - jnp.maximum(gc - half, 0)
        mean_v = acc / cnt_v.astype(F32)
        hi, lo = _split2(mean_v)
        mean = (_dot(band, hi) + _dot(band, lo)) / cnt_h.astype(F32)
        x = pad_s[pl.ds(POOL_PAD + t0, tile), :]
        y = _dot((mean - x).astype(BF16), w) * scale
        o_ref[pl.ds(t0, tile), :] = y.astype(BF16)
        return carry

    lax.fori_loop(0, t // tile, body, 0)


def _pool(u_pool, w_pool, scale_row):
    t = u_pool.shape[0]
    return pl.pallas_call(
        functools.partial(_pool_kernel, t=t),
        grid=(len(POOL_WINDOWS),),
        in_specs=[pl.BlockSpec((t, POOL_GROUP), lambda g: (0, g)),
                  pl.BlockSpec((None, POOL_GROUP, POOL_GROUP), lambda g: (g, 0, 0)),
                  pl.BlockSpec((1, POOL_GROUP), lambda g: (0, g))],
        out_specs=pl.BlockSpec((t, POOL_GROUP), lambda g: (0, g)),
        out_shape=jax.ShapeDtypeStruct((t, MIX_HALF), BF16),
        scratch_shapes=[pltpu.VMEM((t + 2 * POOL_PAD, POOL_GROUP), F32)],
        compiler_params=_cparams(("arbitrary",)),
        name="pool_mix",
    )(u_pool, w_pool, scale_row)


def _route(logits):
    lane = lax.broadcasted_iota(jnp.int32, logits.shape, 1).astype(F32)
    neg = -jnp.inf
    big = float(LANES)
    gl = jnp.where(lane < N_GROUPS, logits, neg)
    gmax = jnp.max(gl, axis=1, keepdims=True)
    gsel = jnp.min(jnp.where(gl == gmax, lane, big), axis=1, keepdims=True)
    p_grp = 1.0 / jnp.sum(jnp.exp(gl - gmax), axis=1, keepdims=True)
    lo = ROUTE_LANE0 + EXPERTS_PER_GROUP * gsel
    el = jnp.where((lane >= lo) & (lane < lo + EXPERTS_PER_GROUP), logits, neg)
    m1 = jnp.max(el, axis=1, keepdims=True)
    i1 = jnp.min(jnp.where(el == m1, lane, big), axis=1, keepdims=True)
    el2 = jnp.where(lane == i1, neg, el)
    m2 = jnp.max(el2, axis=1, keepdims=True)
    i2 = jnp.min(jnp.where(el2 == m2, lane, big), axis=1, keepdims=True)
    e2 = jnp.exp(m2 - m1)
    p1 = 1.0 / (1.0 + e2)
    p2 = e2 / (1.0 + e2)
    info = jnp.where(lane == 0.0, i1 - ROUTE_LANE0, 0.0)
    info = jnp.where(lane == 1.0, i2 - ROUTE_LANE0, info)
    info = jnp.where(lane == 2.0, p_grp * p1, info)
    return jnp.where(lane == 3.0, p_grp * p2, info)


def _outproj_kernel(p_ref, hf_ref, hb_ref, uo_ref, x_ref, wout_ref, mod_ref,
                    hg_ref, n2g_ref, wrp_ref, wrh_ref, br_ref, h1_o, fn_o, info_o):
    h = hf_ref[...] + hb_ref[...]
    parts = []
    for hh in range(HEADS):
        hs = h[:, hh * HEAD_DIM:(hh + 1) * HEAD_DIM]
        mu = jnp.mean(hs, axis=-1, keepdims=True)
        ctr = hs - mu
        var = jnp.mean(ctr * ctr, axis=-1, keepdims=True)
        parts.append(ctr * lax.rsqrt(var + EPS))
    hn = jnp.concatenate(parts, axis=1) * hg_ref[...]
    m = (hn * jax.nn.sigmoid(uo_ref[...])).astype(BF16)
    mix = _dot(jnp.concatenate([p_ref[...], m], axis=1), wout_ref[...])
    h1 = x_ref[...] + mod_ref[2:3, :] * mix
    h1_o[...] = h1
    ms = jnp.mean(h1 * h1, axis=-1, keepdims=True)
    fn = h1 * lax.rsqrt(ms + EPS) * n2g_ref[...]
    fn = fn * (1.0 + mod_ref[4:5, :]) + mod_ref[3:4, :]
    fh, fl = _split2(fn)
    fn_o[...] = fh
    logits = _split_dot(fh, fl, wrp_ref[...], wrh_ref[...]) + br_ref[...]
    info_o[...] = _route(logits)


def _outproj(p, hf, hb, uo, x2d, w_out, mod, head_g, norm2_g, wr_pack, wr_hi, b_route, tm):
    t = x2d.shape[0]
    const = lambda i: (0, 0)
    row = lambda i: (i, 0)
    in_specs = ([pl.BlockSpec((tm, MIX_HALF), row)] * 4
                + [pl.BlockSpec((tm, D_MODEL), row),
                   pl.BlockSpec(w_out.shape, const, pipeline_mode=pl.Buffered(1)),
                   pl.BlockSpec(mod.shape, const),
                   pl.BlockSpec((1, MIX_HALF), const),
                   pl.BlockSpec((1, D_MODEL), const),
                   pl.BlockSpec((D_MODEL, LANES), const),
                   pl.BlockSpec((D_MODEL, LANES), const),
                   pl.BlockSpec((1, LANES), const)])
    return pl.pallas_call(
        _outproj_kernel,
        grid=(t // tm,),
        in_specs=in_specs,
        out_specs=[pl.BlockSpec((tm, D_MODEL), row), pl.BlockSpec((tm, D_MODEL), row),
                   pl.BlockSpec((tm, LANES), row)],
        out_shape=[jax.ShapeDtypeStruct((t, D_MODEL), F32),
                   jax.ShapeDtypeStruct((t, D_MODEL), BF16),
                   jax.ShapeDtypeStruct((t, LANES), F32)],
        compiler_params=_cparams(("arbitrary",)),
        name="outproj_route",
    )(p, hf, hb, uo, x2d, w_out, mod, head_g, norm2_g, wr_pack, wr_hi, b_route)


DISPATCH_BLOCK = 512
GRANULE = 16
GRANULE_SHIFT = 4
LOCAL_CAP = 1536
LOCAL_GRANULES = LOCAL_CAP // GRANULE
FREE_GRANULES = 2
EXPERT_TILE = 256
TILE_GRANULES = EXPERT_TILE // GRANULE
PLAN_UNROLL = 4
PLAN_SLACK = 8
GATHER_DEPTH = 3


def _dispatch_kernel(fn_ref, info_ref, xs_o, ws_o, pt_o, cnt_o):
    tb = DISPATCH_BLOCK
    info = info_ref[...]
    e1, e2 = info[:, 0:1], info[:, 1:2]
    w1c, w2c = info[:, 2:3], info[:, 3:4]
    lane = lax.broadcasted_iota(jnp.int32, (tb, LANES), 1).astype(F32)
    o1 = jnp.where(lane == e1, 1.0, 0.0)
    o2 = jnp.where(lane == e2, 1.0, 0.0)
    onehot = o1 + o2
    cnt = jnp.sum(onehot, axis=0, keepdims=True)
    gran = jnp.floor((cnt + (GRANULE - 1)) * (1.0 / GRANULE))
    a = lax.broadcasted_iota(jnp.int32, (LANES, LANES), 0)
    b = lax.broadcasted_iota(jnp.int32, (LANES, LANES), 1)
    upper = jnp.where(a < b, 1.0, 0.0).astype(BF16)
    seg_off = _dot(jnp.broadcast_to(gran, (8, LANES)).astype(BF16), upper)[0:1, :] * GRANULE
    r = lax.broadcasted_iota(jnp.int32, (tb, tb), 0)
    c = lax.broadcasted_iota(jnp.int32, (tb, tb), 1)
    strict = jnp.where(r > c, 1.0, 0.0).astype(BF16)
    rank = _dot(strict, onehot.astype(BF16))
    slot = rank + seg_off
    pos1 = jnp.sum(o1 * slot, axis=1, keepdims=True)
    pos2 = jnp.sum(o2 * slot, axis=1, keepdims=True)
    rows = lax.broadcasted_iota(jnp.int32, (tb, LOCAL_CAP), 1).astype(F32)
    pt1 = jnp.where(rows == pos1, 1.0, 0.0)
    pt2 = jnp.where(rows == pos2, 1.0, 0.0)
    pt = pt1 + pt2
    pt_o[...] = pt.astype(BF16)
    perm = pt.T.astype(BF16)
    w_slot = jnp.sum((pt1 * w1c + pt2 * w2c).T, axis=1, keepdims=True)
    for c0 in range(0, D_MODEL, 512):
        xs_o[:, c0:c0 + 512] = _dot(perm, fn_ref[:, c0:c0 + 512]).astype(BF16)
    ws_o[...] = jnp.broadcast_to(w_slot, (LOCAL_CAP, LANES))
    cnt_o[...] = jnp.broadcast_to(cnt, (8, LANES)).astype(jnp.int32)


def _dispatch(fn, info):
    t = fn.shape[0]
    nb = t // DISPATCH_BLOCK
    row = lambda i: (i, 0)
    return pl.pallas_call(
        _dispatch_kernel,
        grid=(nb,),
        in_specs=[pl.BlockSpec((DISPATCH_BLOCK, D_MODEL), row),
                  pl.BlockSpec((DISPATCH_BLOCK, LANES), row)],
        out_specs=[pl.BlockSpec((LOCAL_CAP, D_MODEL), row),
                   pl.BlockSpec((LOCAL_CAP, LANES), row),
                   pl.BlockSpec((DISPATCH_BLOCK, LOCAL_CAP), row),
                   pl.BlockSpec((8, LANES), row)],
        out_shape=[jax.ShapeDtypeStruct((nb * LOCAL_CAP, D_MODEL), BF16),
                   jax.ShapeDtypeStruct((nb * LOCAL_CAP, LANES), F32),
                   jax.ShapeDtypeStruct((t, LOCAL_CAP), BF16),
                   jax.ShapeDtypeStruct((nb * 8, LANES), jnp.int32)],
        compiler_params=_cparams(("arbitrary",)),
        name="moe_dispatch",
    )(fn, info)


def _free_granule(q):
    return ((q // FREE_GRANULES) * LOCAL_GRANULES + (LOCAL_GRANULES - FREE_GRANULES)
            + (q % FREE_GRANULES))


def _plan_kernel(cnt_ref, gsrc_o, gdst_o, texp_o, tend_o, ntile_o, lrun, *, nb, max_tiles):
    def init(b, c):
        lrun[b] = 0
        return c

    lax.fori_loop(0, nb, init, 0)

    def per_expert(e, carry):
        g0, last_e = carry

        def per_block(b, g):
            k = (cnt_ref[b, e] + (GRANULE - 1)) >> GRANULE_SHIFT
            lo = lrun[b]
            lrun[b] = lo + k

            base = b * LOCAL_GRANULES + lo

            for j in range(PLAN_UNROLL):
                gsrc_o[g + j] = base + j
                gdst_o[g + j] = base + j

            @pl.when(k > PLAN_UNROLL)
            def _():
                def put(j, c):
                    gsrc_o[g + j] = base + j
                    gdst_o[g + j] = base + j
                    return c

                lax.fori_loop(PLAN_UNROLL, k, put, 0)

            return g + k

        g1 = lax.fori_loop(0, nb, per_block, g0)
        pad = (-g1) & (TILE_GRANULES - 1)

        def put_pad(j, c):
            g = g1 + j
            parity = (g // TILE_GRANULES) & 1
            gsrc_o[g] = _free_granule(0)
            gdst_o[g] = _free_granule(1 + parity * (TILE_GRANULES - 1) + (g & (TILE_GRANULES - 1)))
            return c

        lax.fori_loop(0, pad, put_pad, 0)
        g2 = g1 + pad

        def put_tile(tt, c):
            texp_o[tt] = e
            return c

        lax.fori_loop(g0 // TILE_GRANULES, g2 // TILE_GRANULES, put_tile, 0)
        tend_o[e] = g2 // TILE_GRANULES
        return g2, jnp.where(g2 > g0, e, last_e)

    g_end, last_e = lax.fori_loop(0, N_EXPERTS, per_expert, (0, 0))
    n_tiles = g_end // TILE_GRANULES
    ntile_o[0] = n_tiles

    def fill(tt, c):
        texp_o[tt] = last_e
        return c

    lax.fori_loop(n_tiles, max_tiles, fill, 0)

    def fill_map(g, c):
        gsrc_o[g] = _free_granule(0)
        gdst_o[g] = _free_granule(0)
        return c

    lax.fori_loop(g_end, max_tiles * TILE_GRANULES + PLAN_SLACK, fill_map, 0)


def _max_tiles(t):
    nb = t // DISPATCH_BLOCK
    worst_rows = 2 * t + nb * N_EXPERTS * (GRANULE - 1) + N_EXPERTS * (EXPERT_TILE - GRANULE)
    return -(-worst_rows // EXPERT_TILE)


def _plan(cnt, t):
    nb = cnt.shape[0]
    assert nb * FREE_GRANULES >= 2 + 2 * (TILE_GRANULES - 1)
    max_tiles = _max_tiles(t)
    smem = pl.BlockSpec(memory_space=pltpu.SMEM)
    n_map = max_tiles * TILE_GRANULES + PLAN_SLACK
    return pl.pallas_call(
        functools.partial(_plan_kernel, nb=nb, max_tiles=max_tiles),
        in_specs=[smem],
        out_specs=[smem, smem, smem, smem, smem],
        out_shape=[jax.ShapeDtypeStruct((n_map,), jnp.int32),
                   jax.ShapeDtypeStruct((n_map,), jnp.int32),
                   jax.ShapeDtypeStruct((max_tiles,), jnp.int32),
                   jax.ShapeDtypeStruct((N_EXPERTS,), jnp.int32),
                   jax.ShapeDtypeStruct((1,), jnp.int32)],
        scratch_shapes=[pltpu.SMEM((nb,), jnp.int32)],
        name="moe_plan",
    )(cnt)


def _experts_kernel(gsrc, gdst, texp, tend, ntile, xy_in, ws_in, w1_hbm, w3_hbm, w2_hbm, xy_out,
                    xbuf, wsbuf, ybuf, st1, st3, st2, wb1, wb3, wb2, gsem, ssem, wsem):
    nt = ntile[0]

    def weight_copies(e, ws):
        return (pltpu.make_async_copy(w1_hbm.at[e], st1.at[ws], wsem.at[ws]),
                pltpu.make_async_copy(w3_hbm.at[e], st3.at[ws], wsem.at[ws]),
                pltpu.make_async_copy(w2_hbm.at[e], st2.at[ws], wsem.at[ws]))

    def rows(i):
        return pl.ds(pl.multiple_of(i * GRANULE, GRANULE), GRANULE)

    def gather_copies(g, j, sl):
        return (pltpu.make_async_copy(xy_in.at[rows(g), :], xbuf.at[sl, rows(j), :], gsem.at[sl]),
                pltpu.make_async_copy(ws_in.at[rows(g), :], wsbuf.at[sl, rows(j), :], gsem.at[sl]))

    def scatter_copies(g, j, sl):
        return (pltpu.make_async_copy(ybuf.at[sl, rows(j), :], xy_out.at[rows(g), :], ssem.at[sl]),)

    def issue(tt, sl, gmap, copies):
        for j in range(TILE_GRANULES):
            for cp in copies(gmap[tt * TILE_GRANULES + j], j, sl):
                cp.start()

    def drain(sl, copies):
        full = pl.ds(0, EXPERT_TILE)
        if copies is gather_copies:
            pltpu.make_async_copy(xy_in.at[full, :], xbuf.at[sl], gsem.at[sl]).wait()
            pltpu.make_async_copy(ws_in.at[full, :], wsbuf.at[sl], gsem.at[sl]).wait()
        else:
            pltpu.make_async_copy(ybuf.at[sl], xy_out.at[full, :], ssem.at[sl]).wait()

    @pl.when(nt > 0)
    def _():
        for cp in weight_copies(texp[0], 0):
            cp.start()

    for ahead in range(GATHER_DEPTH - 1):
        @pl.when(ahead < nt)
        def _():
            issue(ahead, ahead, gsrc, gather_copies)

    def tile(t, wslot):
        slot = t % 2
        gslot = t % GATHER_DEPTH
        e = texp[t]
        first = (t == 0) | (texp[jnp.maximum(t - 1, 0)] != e)
        wslot = jnp.where(first & (t > 0), 1 - wslot, wslot)

        drain(gslot, gather_copies)

        @pl.when(t + GATHER_DEPTH - 1 < nt)
        def _():
            issue(t + GATHER_DEPTH - 1, (t + GATHER_DEPTH - 1) % GATHER_DEPTH, gsrc, gather_copies)

        @pl.when(first)
        def _():
            for cp in weight_copies(e, wslot):
                cp.wait()
            wb1[...] = st1[wslot].astype(BF16)
            wb3[...] = st3[wslot].astype(BF16)
            wb2[...] = st2[wslot].astype(BF16)
            nxt = tend[e]

            @pl.when(nxt < nt)
            def _():
                for cp in weight_copies(texp[jnp.minimum(nxt, nt - 1)], 1 - wslot):
                    cp.start()

        x = xbuf[gslot]
        w_row = wsbuf[gslot][:, 0:1]
        a = _dot(x, wb1[...])
        b = _dot(x, wb3[...])
        y = _dot((_silu(a) * b).astype(BF16), wb2[...]) * w_row

        @pl.when(t >= 2)
        def _():
            drain(slot, scatter_copies)

        ybuf[slot] = y.astype(BF16)
        issue(t, slot, gdst, scatter_copies)
        return wslot

    lax.fori_loop(0, nt, tile, 0)

    @pl.when(nt >= 1)
    def _():
        drain((nt - 1) % 2, scatter_copies)

    @pl.when(nt >= 2)
    def _():
        drain(nt % 2, scatter_copies)


def _experts(gsrc, gdst, texp, tend, ntile, xy, ws, w1, w3, w2):
    smem = pl.BlockSpec(memory_space=pltpu.SMEM)
    hbm = pl.BlockSpec(memory_space=pl.ANY)
    return pl.pallas_call(
        _experts_kernel,
        in_specs=[smem, smem, smem, smem, smem, hbm, hbm, hbm, hbm, hbm],
        out_specs=hbm,
        out_shape=jax.ShapeDtypeStruct(xy.shape, xy.dtype),
        scratch_shapes=[pltpu.VMEM((GATHER_DEPTH, EXPERT_TILE, D_MODEL), BF16),
                        pltpu.VMEM((GATHER_DEPTH, EXPERT_TILE, LANES), F32),
                        pltpu.VMEM((2, EXPERT_TILE, D_MODEL), BF16),
                        pltpu.VMEM((2, D_MODEL, D_EXPERT), F32),
                        pltpu.VMEM((2, D_MODEL, D_EXPERT), F32),
                        pltpu.VMEM((2, D_EXPERT, D_MODEL), F32),
                        pltpu.VMEM((D_MODEL, D_EXPERT), BF16),
                        pltpu.VMEM((D_MODEL, D_EXPERT), BF16),
                        pltpu.VMEM((D_EXPERT, D_MODEL), BF16),
                        pltpu.SemaphoreType.DMA((GATHER_DEPTH,)),
                        pltpu.SemaphoreType.DMA((2,)),
                        pltpu.SemaphoreType.DMA((2,))],
        input_output_aliases={5: 0},
        compiler_params=pltpu.CompilerParams(vmem_limit_bytes=VMEM_LIMIT),
        name="moe_experts",
    )(gsrc, gdst, texp, tend, ntile, xy, ws, w1, w3, w2)


def _combine_kernel(y_ref, pt_ref, h1_ref, g2_ref, fg_ref, o_ref):
    moe = _dot(pt_ref[...], y_ref[...])
    h = h1_ref[...] + g2_ref[...] * moe
    ms = jnp.mean(h * h, axis=-1, keepdims=True)
    o_ref[...] = h * lax.rsqrt(ms + EPS) * fg_ref[...]


def _combine(xy, pt, h1, g2, final_g):
    t = h1.shape[0]
    row = lambda i: (i, 0)
    const = lambda i: (0, 0)
    return pl.pallas_call(
        _combine_kernel,
        grid=(t // DISPATCH_BLOCK,),
        in_specs=[pl.BlockSpec((LOCAL_CAP, D_MODEL), row),
                  pl.BlockSpec((DISPATCH_BLOCK, LOCAL_CAP), row),
                  pl.BlockSpec((DISPATCH_BLOCK, D_MODEL), row),
                  pl.BlockSpec((1, D_MODEL), const), pl.BlockSpec((1, D_MODEL), const)],
        out_specs=pl.BlockSpec((DISPATCH_BLOCK, D_MODEL), row),
        out_shape=jax.ShapeDtypeStruct((t, D_MODEL), F32),
        compiler_params=_cparams(("arbitrary",)),
        name="moe_combine_final",
    )(xy, pt, h1, g2, final_g)


def _pad_lanes(a):
    return jnp.pad(a, ((0, 0), (0, LANES - a.shape[1])))


def kernel(x, c, ctx, c_ctx, w_mod, b_mod, norm1_g, w_in, w_conv_q, w_conv_k, gate_bias, head_norm_g, w_pool, pool_scale, w_out, norm2_g, w_group, b_group, w_router, b_router, w1, w3, w2, final_g):
    assert x.shape[0] == 1 and w_mod.shape[0] == 1
    seq = x.shape[1]
    x2d = x[0]
    ctx2d = ctx[0]

    n_main = 5 * MIX_HALF
    w_in_t = jnp.transpose(w_in[0])
    w_main = _cast_transposed_bf16(w_in_t, n_main)
    wg_pack, wg_hi = _split_pack(jnp.transpose(_take_rows(w_in_t, n_main, N_GATES)))
    gate_bias_row = _pad_lanes(gate_bias[0].reshape(1, N_GATES))
    w_out_bf = _cast_bf16(w_out[0])
    wr_pack, wr_hi = _split_pack(jnp.concatenate([w_group[0], w_router[0]], axis=1))
    b_route = _pad_lanes(jnp.concatenate([b_group[0], b_router[0]]).reshape(1, -1))
    norm1 = norm1_g[0].reshape(1, D_MODEL)

    c16 = jnp.zeros((16, D_MODEL), F32).at[0].set(c[0]).at[1].set(c_ctx)
    mods = _adaln(c16, w_mod[0], b_mod[0].reshape(1, -1))
    mod_lat = mods[0].reshape(6, D_MODEL)
    mod_ctx = mods[1].reshape(6, D_MODEL)

    proj = functools.partial(_inproj, g=norm1, w_main=w_main, wg_pack=wg_pack, wg_hi=wg_hi,
                             gate_bias_row=gate_bias_row, wcq=w_conv_q[0], wck=w_conv_k[0])
    _, _, _, k_c, kt_c, v_c, col_c, row_c = proj(ctx2d, mod_ctx[0:2])
    zeros_state = (jnp.zeros((2 * HEADS, HEAD_DIM, HEAD_DIM), F32),
                   jnp.zeros((2 * HEADS, 1, HEAD_DIM), F32),
                   jnp.zeros((2 * HEADS, 1, LANES), F32))
    s0, n0, m0 = _mlstm(None, k_c, kt_c, v_c, col_c, row_c, *zeros_state, need_out=False)

    u_pool, q, uo, k, kt, v, col, rowi = proj(x2d, mod_lat[0:2])
    hf, hb = _mlstm(q, k, kt, v, col, rowi, s0, n0, m0, need_out=True)
    p = _pool(u_pool, w_pool[0], pool_scale[0].reshape(1, -1))
    h1, fn, info = _outproj(p, hf, hb, uo, x2d, w_out_bf, mod_lat, head_norm_g[0].reshape(1, -1),
                            norm2_g[0].reshape(1, -1), wr_pack, wr_hi, b_route, tm=256)

    xs, ws, pt, cnt = _dispatch(fn, info)
    gsrc, gdst, texp, tend, ntile = _plan(cnt[::8, :N_EXPERTS], seq)
    xy = _experts(gsrc, gdst, texp, tend, ntile, xs, ws, w1[0], w3[0], w2[0])
    out = _combine(xy, pt, h1, mod_lat[5:6], final_g.reshape(1, -1))
    return out.reshape(1, seq, D_MODEL)
```

```python
import functools

import jax
import jax.numpy as jnp
from jax import lax
from jax.experimental import pallas as pl
from jax.experimental.pallas import tpu as pltpu

F32 = jnp.float32
BF16 = jnp.bfloat16

D_MODEL = 2048
GRID_W = 64
GRID_SHIFT = 6
POOL_WINDOWS = (2, 4, 8, 16)
POOL_GROUP = 256
HEADS = 4
HEAD_DIM = 256
MIX_HALF = 1024
N_GATES = 16
N_GROUPS = 4
EXPERTS_PER_GROUP = 8
N_EXPERTS = 32
D_EXPERT = 512
EPS = 1e-6
LANES = 128
CHUNK = 256
ROUTE_LANE0 = N_GROUPS

VMEM_LIMIT = 56 * 1024 * 1024


def _cparams(sem, vmem=VMEM_LIMIT):
    return pltpu.CompilerParams(dimension_semantics=sem, vmem_limit_bytes=vmem)


def _split2(x):
    hi = x.astype(BF16)
    lo = (x - hi.astype(F32)).astype(BF16)
    return hi, lo


def _split3(x):
    hi = x.astype(BF16)
    r = x - hi.astype(F32)
    mid = r.astype(BF16)
    lo = (r - mid.astype(F32)).astype(BF16)
    return hi, mid, lo


def _dot(a, b):
    return jnp.dot(a, b, preferred_element_type=F32)


SPLIT_LANE = 64


def _split_pack(w):
    hi, lo = _split2(w)
    n = w.shape[1]
    packed = jnp.zeros((w.shape[0], LANES), BF16).at[:, :n].set(hi).at[:, SPLIT_LANE:SPLIT_LANE + n].set(lo)
    return packed, jnp.pad(hi, ((0, 0), (0, LANES - n)))


def _split_dot(xh, xl, w_packed, w_hi):
    r = _dot(xh, w_packed)
    return r + pltpu.roll(r, SPLIT_LANE, 1) + _dot(xl, w_hi)


def _silu(x):
    return x * jax.nn.sigmoid(x)


def _log_sigmoid(x):
    return jnp.minimum(x, 0.0) - jnp.log(1.0 + jnp.exp(-jnp.abs(x)))


def _cast_kernel(w_ref, o_ref):
    o_ref[...] = w_ref[...].astype(BF16)


def _cast_bf16(w, rows=256):
    k, n = w.shape
    return pl.pallas_call(
        _cast_kernel,
        grid=(k // rows,),
        in_specs=[pl.BlockSpec((rows, n), lambda i: (i, 0))],
        out_specs=pl.BlockSpec((rows, n), lambda i: (i, 0)),
        out_shape=jax.ShapeDtypeStruct((k, n), BF16),
        compiler_params=_cparams(("arbitrary",)),
        name="cast_bf16",
    )(w)


def _copy_kernel(w_ref, o_ref):
    o_ref[...] = w_ref[...]


def _take_rows(w, start, n):
    return pl.pallas_call(
        _copy_kernel,
        grid=(1,),
        in_specs=[pl.BlockSpec((n, w.shape[1]), lambda i: (start // n, 0))],
        out_specs=pl.BlockSpec((n, w.shape[1]), lambda i: (0, 0)),
        out_shape=jax.ShapeDtypeStruct((n, w.shape[1]), w.dtype),
        name="take_rows",
    )(w)


def _cast_t_kernel(wt_ref, o_ref):
    o_ref[...] = wt_ref[...].T.astype(BF16)


def _cast_transposed_bf16(wt, n_cols, cols=256):
    k = wt.shape[1]
    return pl.pallas_call(
        _cast_t_kernel,
        grid=(n_cols // cols,),
        in_specs=[pl.BlockSpec((cols, k), lambda i: (i, 0))],
        out_specs=pl.BlockSpec((k, cols), lambda i: (0, i)),
        out_shape=jax.ShapeDtypeStruct((k, n_cols), BF16),
        compiler_params=_cparams(("arbitrary",)),
        name="cast_transposed_bf16",
    )(wt)


def _adaln_kernel(c_ref, w_ref, b_ref, o_ref):
    a = _silu(c_ref[...])
    a3 = jnp.concatenate(_split3(a), axis=0)
    w_hi, w_lo = _split2(w_ref[...])
    acc = _dot(a3, w_hi)
    acc_lo = _dot(a3[:32], w_lo)
    out = acc[0:16] + acc[16:32] + acc[32:48] + acc_lo[0:16] + acc_lo[16:32]
    o_ref[...] = out + b_ref[...]


def _adaln(c16, w_mod, b_mod):
    n = w_mod.shape[1]
    tn = 1536
    return pl.pallas_call(
        _adaln_kernel,
        grid=(n // tn,),
        in_specs=[pl.BlockSpec((16, D_MODEL), lambda j: (0, 0)),
                  pl.BlockSpec((D_MODEL, tn), lambda j: (0, j)),
                  pl.BlockSpec((1, tn), lambda j: (0, j))],
        out_specs=pl.BlockSpec((16, tn), lambda j: (0, j)),
        out_shape=jax.ShapeDtypeStruct((16, n), F32),
        compiler_params=_cparams(("arbitrary",)),
        name="adaln",
    )(c16, w_mod, b_mod)


HALO = 8
NEG_INF = float("-inf")


def _gate_scan_info(gates):
    n = gates.shape[0]
    lane = lax.broadcasted_iota(jnp.int32, gates.shape, 1)
    rows = lax.broadcasted_iota(jnp.int32, gates.shape, 0)
    lf = jnp.where(lane < N_GATES, _log_sigmoid(gates), 0.0)
    hi = lf.astype(BF16).astype(F32)
    rem = lf - hi
    mid = rem.astype(BF16).astype(F32)
    packed = (hi + pltpu.roll(mid, 32, 1) + pltpu.roll(rem - mid, 64, 1)).astype(BF16)
    r = lax.broadcasted_iota(jnp.int32, (n, n), 0)
    c = lax.broadcasted_iota(jnp.int32, (n, n), 1)
    pf = _dot(jnp.where(r >= c, 1.0, 0.0).astype(BF16), packed)
    pb = _dot(jnp.where(r <= c, 1.0, 0.0).astype(BF16), packed)
    bf = pf + pltpu.roll(pf, 96, 1) + pltpu.roll(pf, 64, 1)
    bb = pb + pltpu.roll(pb, 96, 1) + pltpu.roll(pb, 64, 1)
    b = jnp.where(lane < 8, bf, bb)
    cval = gates - pltpu.roll(b, LANES - 4, 1)
    pm = cval
    sm = cval
    step = 1
    while step < n:
        pm = jnp.maximum(pm, jnp.where(rows >= step, pltpu.roll(pm, step, 0), NEG_INF))
        sm = jnp.maximum(sm, jnp.where(rows < n - step, pltpu.roll(sm, n - step, 0), NEG_INF))
        step *= 2
    cm = jnp.where(lane < 8, pm, sm)
    return cval, jnp.where((lane & 4) == 0, cm, b)


def _inproj_kernel(x_ref, xp_ref, xn_ref, mod_ref, g_ref, w_ref, wgp_ref, wgh_ref, gb_ref,
                   wcq_ref, wck_ref, pool_o, q_o, o_o, k_o, kt_o, v_o, col_o, row_o, *, nt):
    i = pl.program_id(0)
    tm = x_ref.shape[0]
    x_all = jnp.concatenate([xp_ref[...], x_ref[...], xn_ref[...]], axis=0)
    ms = jnp.mean(x_all * x_all, axis=-1, keepdims=True)
    y = x_all * lax.rsqrt(ms + EPS) * g_ref[...]
    xn_all = y * (1.0 + mod_ref[1:2, :]) + mod_ref[0:1, :]
    xh_all = xn_all.astype(BF16)
    xh, xl = _split2(xn_all[HALO:HALO + tm])

    def cols(ci):
        return w_ref[:, ci * MIX_HALF:(ci + 1) * MIX_HALF]

    u_q = _dot(xh_all, cols(1))
    u_k = _dot(xh_all, cols(3))
    pool_o[...] = _dot(xh, cols(0))
    o_o[...] = _dot(xh, cols(2))
    v_o[...] = _dot(xh, cols(4)).astype(BF16)

    rowi = lax.broadcasted_iota(jnp.int32, (tm, MIX_HALF), 0)
    at_start = jnp.logical_and(rowi == 0, i == 0)
    at_end = jnp.logical_and(rowi == tm - 1, i == nt - 1)

    def conv_silu(u, wc_ref):
        n = u.shape[0]
        up = jnp.where(at_start, 0.0, pltpu.roll(u, 1, 0)[HALO:HALO + tm])
        un = jnp.where(at_end, 0.0, pltpu.roll(u, n - 1, 0)[HALO:HALO + tm])
        return _silu(wc_ref[0:1, :] * up + wc_ref[1:2, :] * u[HALO:HALO + tm] + wc_ref[2:3, :] * un)

    q_o[...] = (conv_silu(u_q, wcq_ref) * (HEAD_DIM ** -0.5)).astype(BF16)
    k = conv_silu(u_k, wck_ref)
    k_o[...] = k.astype(BF16)
    kt_o[...] = k.astype(BF16).T
    gates = _split_dot(xh, xl, wgp_ref[...], wgh_ref[...]) + gb_ref[...]
    cval, col = _gate_scan_info(gates)
    col_o[...] = col[:, :N_GATES]
    row_o[...] = cval.T[:N_GATES, :]


def _inproj(x2d, mod, g, w_main, wg_pack, wg_hi, gate_bias_row, wcq, wck):
    t = x2d.shape[0]
    tm = CHUNK
    nt = t // tm
    r8 = tm // HALO
    last8 = t // HALO - 1
    const = lambda i: (0, 0)
    row = lambda i: (i, 0)
    f32_out = jax.ShapeDtypeStruct((t, MIX_HALF), F32)
    bf16_out = jax.ShapeDtypeStruct((t, MIX_HALF), BF16)
    seq = pl.BlockSpec((tm, MIX_HALF), row)
    return pl.pallas_call(
        functools.partial(_inproj_kernel, nt=nt),
        grid=(nt,),
        in_specs=[pl.BlockSpec((tm, D_MODEL), row),
                  pl.BlockSpec((HALO, D_MODEL), lambda i: (jnp.maximum(i * r8 - 1, 0), 0)),
                  pl.BlockSpec((HALO, D_MODEL), lambda i: (jnp.minimum((i + 1) * r8, last8), 0)),
                  pl.BlockSpec((2, D_MODEL), const),
                  pl.BlockSpec((1, D_MODEL), const),
                  pl.BlockSpec(w_main.shape, const, pipeline_mode=pl.Buffered(1)),
                  pl.BlockSpec((D_MODEL, LANES), const),
                  pl.BlockSpec((D_MODEL, LANES), const),
                  pl.BlockSpec((1, LANES), const),
                  pl.BlockSpec((3, MIX_HALF), const),
                  pl.BlockSpec((3, MIX_HALF), const)],
        out_specs=[seq, seq, seq, seq, pl.BlockSpec((MIX_HALF, tm), lambda i: (0, i)), seq,
                   pl.BlockSpec((tm, N_GATES), row), pl.BlockSpec((N_GATES, tm), lambda i: (0, i))],
        out_shape=[f32_out, bf16_out, f32_out, bf16_out,
                   jax.ShapeDtypeStruct((MIX_HALF, t), BF16), bf16_out,
                   jax.ShapeDtypeStruct((t, N_GATES), F32),
                   jax.ShapeDtypeStruct((N_GATES, t), F32)],
        compiler_params=_cparams(("arbitrary",)),
        name="inproj",
    )(x2d, x2d, x2d, mod, g, w_main, wg_pack, wg_hi, gate_bias_row, wcq, wck)


def _mlstm_kernel(*refs, need_out):
    if need_out:
        (qf, kf, ktf, vf, colf, rowf, qb, kb, ktb, vb, colb, rowb, s0, n0, m0,
         hf_o, hb_o, s_s, n_s, m_s) = refs
        q_refs, h_outs = (qf, qb), (hf_o, hb_o)
    else:
        (kf, ktf, vf, colf, rowf, kb, ktb, vb, colb, rowb, s0, n0, m0,
         s_o, n_o, m_o, s_s, n_s, m_s) = refs
    k_refs, kt_refs, v_refs = (kf, kb), (ktf, ktb), (vf, vb)
    col_refs, row_refs = (colf, colb), (rowf, rowb)
    j = pl.program_id(0)

    @pl.when(j == 0)
    def _():
        s_s[...] = s0[...]
        n_s[...] = n0[...]
        m_s[...] = m0[...]

    L = CHUNK
    row = lax.broadcasted_iota(jnp.int32, (L, L), 0)
    col = lax.broadcasted_iota(jnp.int32, (L, L), 1)

    heads = [(d, h) for d in range(2) for h in range(HEADS)]

    def head_values(d, h):
        hd = d * HEADS + h
        lc, lb = 8 * d + h, 8 * d + 4 + h
        hs = slice(h * HEAD_DIM, (h + 1) * HEAD_DIM)
        edge = L - 1 if d == 0 else 0
        colv = col_refs[d][...]
        rowv = row_refs[d][...]
        v = dict(hd=hd, hs=hs, d=d)
        v["cm_c"], v["b_c"] = colv[:, lc:lc + 1], colv[:, lb:lb + 1]
        v["c_r"] = rowv[lc:lc + 1, :]
        v["g"] = v["b_c"][edge:edge + 1, :]
        v["m_old"] = m_s[hd][:, 0:1]
        v["s_old"] = s_s[hd]
        v["n_old"] = n_s[hd]
        v["m_x"] = jnp.maximum(v["m_old"], v["cm_c"][edge:edge + 1, :])
        v["decay"] = jnp.exp(v["m_old"] - v["m_x"])
        v["wk_r"] = jnp.exp(v["c_r"] - v["m_x"])
        v["v_h"] = v_refs[d][:, hs]
        v["kt_h"] = kt_refs[d][hs, :]
        return v

    def update_state(v):
        hd = v["hd"]
        n_s[hd] = v["decay"] * v["n_old"] + _dot(
            jnp.broadcast_to(v["wk_r"], (8, L)).astype(BF16), k_refs[v["d"]][:, v["hs"]])[0:1, :]
        kwt = (v["kt_h"].astype(F32) * v["wk_r"]).astype(BF16)
        s_s[hd] = v["decay"] * v["s_old"] + _dot(kwt, v["v_h"])
        m_s[hd] = jnp.broadcast_to(v["g"] + v["m_x"], (1, LANES))

    def weights(v):
        d, hs = v["d"], v["hs"]
        mask = (row >= col) if d == 0 else (row <= col)
        q_h = q_refs[d][:, hs]
        m_c = jnp.maximum(v["m_old"], v["cm_c"])
        w_inter = jnp.exp(v["m_old"] - m_c)
        v["e"] = jnp.where(mask, jnp.exp(v["c_r"] - m_c), 0.0)
        v["qw"] = (q_h.astype(F32) * w_inter).astype(BF16)
        qn = lax.dot_general(q_h, jnp.broadcast_to(v["n_old"], (8, HEAD_DIM)).astype(BF16),
                             (((1,), (1,)), ((), ())), preferred_element_type=F32)
        v["den_inter"] = w_inter * qn[:, 0:1]
        v["floor"] = jnp.exp(-(v["b_c"] + m_c))
        v["s_bf"] = v["s_old"].astype(BF16)

    def outputs(v, qk):
        p = v["e"] * qk
        num = _dot(v["qw"], v["s_bf"]) + _dot(p.astype(BF16), v["v_h"])
        den = v["den_inter"] + jnp.sum(p[:, :LANES] + p[:, LANES:], axis=1, keepdims=True)
        h_outs[v["d"]][:, v["hs"]] = num / jnp.maximum(jnp.abs(den), v["floor"])

    vals = [head_values(d, h) for d, h in heads]
    if need_out:
        qks = [_dot(q_refs[v["d"]][:, v["hs"]], v["kt_h"]) for v in vals]
        for v in vals:
            weights(v)
    for v in vals:
        update_state(v)
    if need_out:
        for v, qk in zip(vals, qks):
            outputs(v, qk)

    if not need_out:
        s_o[...] = s_s[...]
        n_o[...] = n_s[...]
        m_o[...] = m_s[...]


def _mlstm(q, k, kt, v, col, rowi, s0, n0, m0, need_out):
    t = k.shape[0]
    nc = t // CHUNK
    fwd = lambda j: (j, 0)
    bwd = lambda j: (nc - 1 - j, 0)
    fwd_t = lambda j: (0, j)
    bwd_t = lambda j: (0, nc - 1 - j)
    c3 = lambda j: (0, 0, 0)
    seq = lambda im: pl.BlockSpec((CHUNK, MIX_HALF), im)
    state_specs = [pl.BlockSpec(s0.shape, c3), pl.BlockSpec(n0.shape, c3), pl.BlockSpec(m0.shape, c3)]
    scratch = [pltpu.VMEM(s0.shape, F32), pltpu.VMEM(n0.shape, F32), pltpu.VMEM(m0.shape, F32)]

    def side(im, im_t):
        specs = ([seq(im)] if need_out else []) + [seq(im), pl.BlockSpec((MIX_HALF, CHUNK), im_t), seq(im)]
        return specs + [pl.BlockSpec((CHUNK, N_GATES), im), pl.BlockSpec((N_GATES, CHUNK), im_t)]

    in_specs = side(fwd, fwd_t) + side(bwd, bwd_t) + state_specs
    seq_in = ((q,) if need_out else ()) + (k, kt, v, col, rowi)
    args = seq_in + seq_in + (s0, n0, m0)
    if need_out:
        out_specs = [seq(fwd), seq(bwd)]
        out_shape = [jax.ShapeDtypeStruct((t, MIX_HALF), F32)] * 2
    else:
        out_specs = state_specs
        out_shape = [jax.ShapeDtypeStruct(a.shape, F32) for a in (s0, n0, m0)]
    return pl.pallas_call(
        functools.partial(_mlstm_kernel, need_out=need_out),
        grid=(nc,),
        in_specs=in_specs,
        out_specs=out_specs,
        out_shape=out_shape,
        scratch_shapes=scratch,
        compiler_params=_cparams(("arbitrary",)),
        name="mlstm_out" if need_out else "mlstm_state",
    )(*args)


POOL_PAD = 512
POOL_UNROLL = 4


def _pool_kernel(u_ref, w_ref, sc_ref, o_ref, pad_s, *, t):
    for gi, win in enumerate(POOL_WINDOWS):
        @pl.when(pl.program_id(0) == gi)
        def _():
            _pool_group(u_ref, w_ref, sc_ref, o_ref, pad_s, win=win, t=t)


def _pool_group(u_ref, w_ref, sc_ref, o_ref, pad_s, *, win, t):
    half = win // 2
    tile = 256
    zeros = jnp.zeros((POOL_PAD, POOL_GROUP), F32)
    pad_s[0:POOL_PAD, :] = zeros
    pad_s[POOL_PAD + t:POOL_PAD + t + POOL_PAD, :] = zeros

    def copy(r, carry):
        t0 = pl.multiple_of(r * tile, tile)
        pad_s[pl.ds(POOL_PAD + t0, tile), :] = u_ref[pl.ds(t0, tile), :]
        return carry

    lax.fori_loop(0, t // tile, copy, 0)

    row = lax.broadcasted_iota(jnp.int32, (tile, tile), 0)
    col = lax.broadcasted_iota(jnp.int32, (tile, tile), 1)
    same_row = (row >> GRID_SHIFT) == (col >> GRID_SHIFT)
    in_win = (col - row >= -half) & (col - row < half)
    band = jnp.where(same_row & in_win, 1.0, 0.0).astype(BF16)
    w = w_ref[...].astype(BF16)
    scale = sc_ref[...]
    n_rows = t // GRID_W

    tok0 = lax.broadcasted_iota(jnp.int32, (tile, POOL_GROUP), 0)
    gc = tok0 & (GRID_W - 1)
    inv_h = 1.0 / (jnp.minimum(gc + half, GRID_W) - jnp.maximum(gc - half, 0)).astype(F32)

    def body(r, carry):
        t0s = [pl.multiple_of((r * POOL_UNROLL + k) * tile, tile) for k in range(POOL_UNROLL)]
        pieces = []
        for t0 in t0s:
            acc = pad_s[pl.ds(POOL_PAD + t0 - GRID_W * half, tile), :]
            for dd in range(-half + 1, half):
                acc = acc + pad_s[pl.ds(POOL_PAD + t0 + GRID_W * dd, tile), :]
            gr = (t0 + tok0) >> GRID_SHIFT
            cnt_v = jnp.minimum(gr + half, n_rows) - jnp.maximum(gr - half, 0)
            pieces.append(_split2(acc / cnt_v.astype(F32)))
        means = [(_dot(band, hi) + _dot(band, lo)) * inv_h for hi, lo in pieces]
        diffs = [(m - pad_s[pl.ds(POOL_PAD + t0, tile), :]).astype(BF16) for m, t0 in zip(means, t0s)]
        for d, t0 in zip(diffs, t0s):
            o_ref[pl.ds(t0, tile), :] = (_dot(d, w) * scale).astype(BF16)
        return carry

    lax.fori_loop(0, t // (tile * POOL_UNROLL), body, 0)


def _pool(u_pool, w_pool, scale_row):
    t = u_pool.shape[0]
    return pl.pallas_call(
        functools.partial(_pool_kernel, t=t),
        grid=(len(POOL_WINDOWS),),
        in_specs=[pl.BlockSpec((t, POOL_GROUP), lambda g: (0, g)),
                  pl.BlockSpec((None, POOL_GROUP, POOL_GROUP), lambda g: (g, 0, 0)),
                  pl.BlockSpec((1, POOL_GROUP), lambda g: (0, g))],
        out_specs=pl.BlockSpec((t, POOL_GROUP), lambda g: (0, g)),
        out_shape=jax.ShapeDtypeStruct((t, MIX_HALF), BF16),
        scratch_shapes=[pltpu.VMEM((t + 2 * POOL_PAD, POOL_GROUP), F32)],
        compiler_params=_cparams(("arbitrary",)),
        name="pool_mix",
    )(u_pool, w_pool, scale_row)


def _route(logits):
    lane = lax.broadcasted_iota(jnp.int32, logits.shape, 1).astype(F32)
    neg = -jnp.inf
    big = float(LANES)
    gl = jnp.where(lane < N_GROUPS, logits, neg)
    gmax = jnp.max(gl, axis=1, keepdims=True)
    gsel = jnp.min(jnp.where(gl == gmax, lane, big), axis=1, keepdims=True)
    p_grp = 1.0 / jnp.sum(jnp.exp(gl - gmax), axis=1, keepdims=True)
    lo = ROUTE_LANE0 + EXPERTS_PER_GROUP * gsel
    el = jnp.where((lane >= lo) & (lane < lo + EXPERTS_PER_GROUP), logits, neg)
    m1 = jnp.max(el, axis=1, keepdims=True)
    i1 = jnp.min(jnp.where(el == m1, lane, big), axis=1, keepdims=True)
    el2 = jnp.where(lane == i1, neg, el)
    m2 = jnp.max(el2, axis=1, keepdims=True)
    i2 = jnp.min(jnp.where(el2 == m2, lane, big), axis=1, keepdims=True)
    e2 = jnp.exp(m2 - m1)
    p1 = 1.0 / (1.0 + e2)
    p2 = e2 / (1.0 + e2)
    info = jnp.where(lane == 0.0, i1 - ROUTE_LANE0, 0.0)
    info = jnp.where(lane == 1.0, i2 - ROUTE_LANE0, info)
    info = jnp.where(lane == 2.0, p_grp * p1, info)
    return jnp.where(lane == 3.0, p_grp * p2, info)


def _outproj_kernel(p_ref, hf_ref, hb_ref, uo_ref, x_ref, wout_ref, mod_ref,
                    hg_ref, n2g_ref, wrp_ref, wrh_ref, br_ref, h1_o, fn_o, info_o):
    h = hf_ref[...] + hb_ref[...]
    parts = []
    for hh in range(HEADS):
        hs = h[:, hh * HEAD_DIM:(hh + 1) * HEAD_DIM]
        mu = jnp.mean(hs, axis=-1, keepdims=True)
        ctr = hs - mu
        var = jnp.mean(ctr * ctr, axis=-1, keepdims=True)
        parts.append(ctr * lax.rsqrt(var + EPS))
    hn = jnp.concatenate(parts, axis=1) * hg_ref[...]
    m = (hn * jax.nn.sigmoid(uo_ref[...])).astype(BF16)
    mix = _dot(jnp.concatenate([p_ref[...], m], axis=1), wout_ref[...])
    h1 = x_ref[...] + mod_ref[2:3, :] * mix
    h1_o[...] = h1
    ms = jnp.mean(h1 * h1, axis=-1, keepdims=True)
    fn = h1 * lax.rsqrt(ms + EPS) * n2g_ref[...]
    fn = fn * (1.0 + mod_ref[4:5, :]) + mod_ref[3:4, :]
    fh, fl = _split2(fn)
    fn_o[...] = fh
    logits = _split_dot(fh, fl, wrp_ref[...], wrh_ref[...]) + br_ref[...]
    info_o[...] = _route(logits)


def _outproj(p, hf, hb, uo, x2d, w_out, mod, head_g, norm2_g, wr_pack, wr_hi, b_route, tm):
    t = x2d.shape[0]
    const = lambda i: (0, 0)
    row = lambda i: (i, 0)
    in_specs = ([pl.BlockSpec((tm, MIX_HALF), row)] * 4
                + [pl.BlockSpec((tm, D_MODEL), row),
                   pl.BlockSpec(w_out.shape, const, pipeline_mode=pl.Buffered(1)),
                   pl.BlockSpec(mod.shape, const),
                   pl.BlockSpec((1, MIX_HALF), const),
                   pl.BlockSpec((1, D_MODEL), const),
                   pl.BlockSpec((D_MODEL, LANES), const),
                   pl.BlockSpec((D_MODEL, LANES), const),
                   pl.BlockSpec((1, LANES), const)])
    return pl.pallas_call(
        _outproj_kernel,
        grid=(t // tm,),
        in_specs=in_specs,
        out_specs=[pl.BlockSpec((tm, D_MODEL), row), pl.BlockSpec((tm, D_MODEL), row),
                   pl.BlockSpec((tm, LANES), row)],
        out_shape=[jax.ShapeDtypeStruct((t, D_MODEL), F32),
                   jax.ShapeDtypeStruct((t, D_MODEL), BF16),
                   jax.ShapeDtypeStruct((t, LANES), F32)],
        compiler_params=_cparams(("arbitrary",)),
        name="outproj_route",
    )(p, hf, hb, uo, x2d, w_out, mod, head_g, norm2_g, wr_pack, wr_hi, b_route)


DISPATCH_BLOCK = 512
GRANULE = 16
GRANULE_SHIFT = 4
LOCAL_CAP = 1536
LOCAL_GRANULES = LOCAL_CAP // GRANULE
FREE_GRANULES = 2
EXPERT_TILE = 256
TILE_GRANULES = EXPERT_TILE // GRANULE
PLAN_UNROLL = 4
PLAN_SLACK = 8
GATHER_DEPTH = 3


def _dispatch_kernel(fn_ref, info_ref, xs_o, ws_o, pt_o, cnt_o):
    tb = DISPATCH_BLOCK
    info = info_ref[...]
    e1, e2 = info[:, 0:1], info[:, 1:2]
    w1c, w2c = info[:, 2:3], info[:, 3:4]
    lane = lax.broadcasted_iota(jnp.int32, (tb, LANES), 1).astype(F32)
    o1 = jnp.where(lane == e1, 1.0, 0.0)
    o2 = jnp.where(lane == e2, 1.0, 0.0)
    onehot = o1 + o2
    cnt = jnp.sum(onehot, axis=0, keepdims=True)
    gran = jnp.floor((cnt + (GRANULE - 1)) * (1.0 / GRANULE))
    a = lax.broadcasted_iota(jnp.int32, (LANES, LANES), 0)
    b = lax.broadcasted_iota(jnp.int32, (LANES, LANES), 1)
    upper = jnp.where(a < b, 1.0, 0.0).astype(BF16)
    seg_off = _dot(jnp.broadcast_to(gran, (8, LANES)).astype(BF16), upper)[0:1, :] * GRANULE
    r = lax.broadcasted_iota(jnp.int32, (tb, tb), 0)
    c = lax.broadcasted_iota(jnp.int32, (tb, tb), 1)
    strict = jnp.where(r > c, 1.0, 0.0).astype(BF16)
    rank = _dot(strict, onehot.astype(BF16))
    slot = rank + seg_off
    pos1 = jnp.sum(o1 * slot, axis=1, keepdims=True)
    pos2 = jnp.sum(o2 * slot, axis=1, keepdims=True)
    rows = lax.broadcasted_iota(jnp.int32, (tb, LOCAL_CAP), 1).astype(F32)
    pt1 = jnp.where(rows == pos1, 1.0, 0.0)
    pt2 = jnp.where(rows == pos2, 1.0, 0.0)
    pt = pt1 + pt2
    pt_o[...] = pt.astype(BF16)
    perm = pt.T.astype(BF16)
    w_slot = jnp.sum((pt1 * w1c + pt2 * w2c).T, axis=1, keepdims=True)
    for c0 in range(0, D_MODEL, 512):
        xs_o[:, c0:c0 + 512] = _dot(perm, fn_ref[:, c0:c0 + 512]).astype(BF16)
    ws_o[...] = jnp.broadcast_to(w_slot, (LOCAL_CAP, LANES))
    cnt_o[...] = jnp.broadcast_to(cnt, (8, LANES)).astype(jnp.int32)


def _dispatch(fn, info):
    t = fn.shape[0]
    nb = t // DISPATCH_BLOCK
    row = lambda i: (i, 0)
    return pl.pallas_call(
        _dispatch_kernel,
        grid=(nb,),
        in_specs=[pl.BlockSpec((DISPATCH_BLOCK, D_MODEL), row),
                  pl.BlockSpec((DISPATCH_BLOCK, LANES), row)],
        out_specs=[pl.BlockSpec((LOCAL_CAP, D_MODEL), row),
                   pl.BlockSpec((LOCAL_CAP, LANES), row),
                   pl.BlockSpec((DISPATCH_BLOCK, LOCAL_CAP), row),
                   pl.BlockSpec((8, LANES), row)],
        out_shape=[jax.ShapeDtypeStruct((nb * LOCAL_CAP, D_MODEL), BF16),
                   jax.ShapeDtypeStruct((nb * LOCAL_CAP, LANES), F32),
                   jax.ShapeDtypeStruct((t, LOCAL_CAP), BF16),
                   jax.ShapeDtypeStruct((nb * 8, LANES), jnp.int32)],
        compiler_params=_cparams(("arbitrary",)),
        name="moe_dispatch",
    )(fn, info)


def _map_len(max_tiles):
    return (max_tiles + GATHER_DEPTH - 1) * TILE_GRANULES + PLAN_SLACK


def _free_granule(q):
    return ((q // FREE_GRANULES) * LOCAL_GRANULES + (LOCAL_GRANULES - FREE_GRANULES)
            + (q % FREE_GRANULES))


def _plan_kernel(cnt_ref, gsrc_o, gdst_o, texp_o, tend_o, ntile_o, lrun, *, nb, max_tiles):
    def init(b, c):
        lrun[b] = 0
        return c

    lax.fori_loop(0, nb, init, 0)

    def per_expert(e, carry):
        g0, last_e = carry

        def per_block(b, g):
            k = (cnt_ref[b, e] + (GRANULE - 1)) >> GRANULE_SHIFT
            lo = lrun[b]
            lrun[b] = lo + k

            base = b * LOCAL_GRANULES + lo

            for j in range(PLAN_UNROLL):
                gsrc_o[g + j] = base + j
                gdst_o[g + j] = base + j

            @pl.when(k > PLAN_UNROLL)
            def _():
                def put(j, c):
                    gsrc_o[g + j] = base + j
                    gdst_o[g + j] = base + j
                    return c

                lax.fori_loop(PLAN_UNROLL, k, put, 0)

            return g + k

        g1 = lax.fori_loop(0, nb, per_block, g0)
        pad = (-g1) & (TILE_GRANULES - 1)

        def put_pad(j, c):
            g = g1 + j
            parity = (g // TILE_GRANULES) & 1
            gsrc_o[g] = _free_granule(0)
            gdst_o[g] = _free_granule(1 + parity * (TILE_GRANULES - 1) + (g & (TILE_GRANULES - 1)))
            return c

        lax.fori_loop(0, pad, put_pad, 0)
        g2 = g1 + pad

        def put_tile(tt, c):
            texp_o[tt] = e
            return c

        lax.fori_loop(g0 // TILE_GRANULES, g2 // TILE_GRANULES, put_tile, 0)
        tend_o[e] = g2 // TILE_GRANULES
        return g2, jnp.where(g2 > g0, e, last_e)

    g_end, last_e = lax.fori_loop(0, N_EXPERTS, per_expert, (0, 0))
    n_tiles = g_end // TILE_GRANULES
    ntile_o[0] = n_tiles

    def fill(tt, c):
        texp_o[tt] = last_e
        return c

    lax.fori_loop(n_tiles, max_tiles, fill, 0)

    def fill_map(g, c):
        gsrc_o[g] = _free_granule(0)
        gdst_o[g] = _free_granule(0)
        return c

    lax.fori_loop(g_end, _map_len(max_tiles), fill_map, 0)


def _max_tiles(t):
    nb = t // DISPATCH_BLOCK
    worst_rows = 2 * t + nb * N_EXPERTS * (GRANULE - 1) + N_EXPERTS * (EXPERT_TILE - GRANULE)
    return -(-worst_rows // EXPERT_TILE)


def _plan(cnt, t):
    nb = cnt.shape[0]
    assert nb * FREE_GRANULES >= 2 + 2 * (TILE_GRANULES - 1)
    max_tiles = _max_tiles(t)
    smem = pl.BlockSpec(memory_space=pltpu.SMEM)
    n_map = _map_len(max_tiles)
    return pl.pallas_call(
        functools.partial(_plan_kernel, nb=nb, max_tiles=max_tiles),
        in_specs=[smem],
        out_specs=[smem, smem, smem, smem, smem],
        out_shape=[jax.ShapeDtypeStruct((n_map,), jnp.int32),
                   jax.ShapeDtypeStruct((n_map,), jnp.int32),
                   jax.ShapeDtypeStruct((max_tiles,), jnp.int32),
                   jax.ShapeDtypeStruct((N_EXPERTS,), jnp.int32),
                   jax.ShapeDtypeStruct((1,), jnp.int32)],
        scratch_shapes=[pltpu.SMEM((nb,), jnp.int32)],
        name="moe_plan",
    )(cnt)


def _experts_kernel(gsrc, gdst, texp, tend, ntile, xy_in, ws_in, w1_hbm, w3_hbm, w2_hbm, xy_out,
                    xbuf, wsbuf, ybuf, st1, st3, st2, wb1, wb3, wb2, gsem, ssem, wsem):
    nt = ntile[0]

    def weight_copies(e, ws):
        return (pltpu.make_async_copy(w1_hbm.at[e], st1.at[ws], wsem.at[ws]),
                pltpu.make_async_copy(w3_hbm.at[e], st3.at[ws], wsem.at[ws]),
                pltpu.make_async_copy(w2_hbm.at[e], st2.at[ws], wsem.at[ws]))

    def rows(i):
        return pl.ds(pl.multiple_of(i * GRANULE, GRANULE), GRANULE)

    def gather_copies(g, j, sl):
        return (pltpu.make_async_copy(xy_in.at[rows(g), :], xbuf.at[sl, rows(j), :], gsem.at[sl]),
                pltpu.make_async_copy(ws_in.at[rows(g), :], wsbuf.at[sl, rows(j), :], gsem.at[sl]))

    def scatter_copies(g, j, sl):
        return (pltpu.make_async_copy(ybuf.at[sl, rows(j), :], xy_out.at[rows(g), :], ssem.at[sl]),)

    def issue(tt, sl, gmap, copies):
        for j in range(TILE_GRANULES):
            for cp in copies(gmap[tt * TILE_GRANULES + j], j, sl):
                cp.start()

    def drain(sl, copies):
        full = pl.ds(0, EXPERT_TILE)
        if copies is gather_copies:
            pltpu.make_async_copy(xy_in.at[full, :], xbuf.at[sl], gsem.at[sl]).wait()
            pltpu.make_async_copy(ws_in.at[full, :], wsbuf.at[sl], gsem.at[sl]).wait()
        else:
            pltpu.make_async_copy(ybuf.at[sl], xy_out.at[full, :], ssem.at[sl]).wait()

    @pl.when(nt > 0)
    def _():
        last = nt - 1
        for cp in weight_copies(texp[0], 0):
            cp.start()
        for ahead in range(GATHER_DEPTH - 1):
            issue(ahead, ahead, gsrc, gather_copies)

        def tile(t, wslot):
            slot = t % 2
            gslot = t % GATHER_DEPTH
            e = texp[t]
            first = (t == 0) | (texp[jnp.maximum(t - 1, 0)] != e)
            wslot = jnp.where(first & (t > 0), 1 - wslot, wslot)

            drain(gslot, gather_copies)

            @pl.when(t >= 2)
            def _():
                drain(slot, scatter_copies)

            @pl.when(first)
            def _():
                for cp in weight_copies(e, wslot):
                    cp.wait()
                wb1[...] = st1[wslot].astype(BF16)
                wb3[...] = st3[wslot].astype(BF16)
                wb2[...] = st2[wslot].astype(BF16)
                nxt = tend[e]

                @pl.when(nxt < nt)
                def _():
                    for cp in weight_copies(texp[jnp.minimum(nxt, last)], 1 - wslot):
                        cp.start()

            ahead = t + GATHER_DEPTH - 1
            issue(ahead, ahead % GATHER_DEPTH, gsrc, gather_copies)
            x = xbuf[gslot]
            w_row = wsbuf[gslot][:, 0:1]
            a = _dot(x, wb1[...])
            b = _dot(x, wb3[...])
            y = _dot((_silu(a) * b).astype(BF16), wb2[...]) * w_row
            ybuf[slot] = y.astype(BF16)
            issue(t, slot, gdst, scatter_copies)
            return wslot

        lax.fori_loop(0, nt, tile, 0)

        for k in range(GATHER_DEPTH - 1):
            drain((nt + k) % GATHER_DEPTH, gather_copies)
        drain(last % 2, scatter_copies)

        @pl.when(nt >= 2)
        def _():
            drain(nt % 2, scatter_copies)


def _experts(gsrc, gdst, texp, tend, ntile, xy, ws, w1, w3, w2):
    smem = pl.BlockSpec(memory_space=pltpu.SMEM)
    hbm = pl.BlockSpec(memory_space=pl.ANY)
    return pl.pallas_call(
        _experts_kernel,
        in_specs=[smem, smem, smem, smem, smem, hbm, hbm, hbm, hbm, hbm],
        out_specs=hbm,
        out_shape=jax.ShapeDtypeStruct(xy.shape, xy.dtype),
        scratch_shapes=[pltpu.VMEM((GATHER_DEPTH, EXPERT_TILE, D_MODEL), BF16),
                        pltpu.VMEM((GATHER_DEPTH, EXPERT_TILE, LANES), F32),
                        pltpu.VMEM((2, EXPERT_TILE, D_MODEL), BF16),
                        pltpu.VMEM((2, D_MODEL, D_EXPERT), F32),
                        pltpu.VMEM((2, D_MODEL, D_EXPERT), F32),
                        pltpu.VMEM((2, D_EXPERT, D_MODEL), F32),
                        pltpu.VMEM((D_MODEL, D_EXPERT), BF16),
                        pltpu.VMEM((D_MODEL, D_EXPERT), BF16),
                        pltpu.VMEM((D_EXPERT, D_MODEL), BF16),
                        pltpu.SemaphoreType.DMA((GATHER_DEPTH,)),
                        pltpu.SemaphoreType.DMA((2,)),
                        pltpu.SemaphoreType.DMA((2,))],
        input_output_aliases={5: 0},
        compiler_params=pltpu.CompilerParams(vmem_limit_bytes=VMEM_LIMIT),
        name="moe_experts",
    )(gsrc, gdst, texp, tend, ntile, xy, ws, w1, w3, w2)


def _combine_kernel(y_ref, pt_ref, h1_ref, g2_ref, fg_ref, o_ref):
    moe = _dot(pt_ref[...], y_ref[...])
    h = h1_ref[...] + g2_ref[...] * moe
    ms = jnp.mean(h * h, axis=-1, keepdims=True)
    o_ref[...] = h * lax.rsqrt(ms + EPS) * fg_ref[...]


def _combine(xy, pt, h1, g2, final_g):
    t = h1.shape[0]
    row = lambda i: (i, 0)
    const = lambda i: (0, 0)
    return pl.pallas_call(
        _combine_kernel,
        grid=(t // DISPATCH_BLOCK,),
        in_specs=[pl.BlockSpec((LOCAL_CAP, D_MODEL), row),
                  pl.BlockSpec((DISPATCH_BLOCK, LOCAL_CAP), row),
                  pl.BlockSpec((DISPATCH_BLOCK, D_MODEL), row),
                  pl.BlockSpec((1, D_MODEL), const), pl.BlockSpec((1, D_MODEL), const)],
        out_specs=pl.BlockSpec((DISPATCH_BLOCK, D_MODEL), row),
        out_shape=jax.ShapeDtypeStruct((t, D_MODEL), F32),
        compiler_params=_cparams(("arbitrary",)),
        name="moe_combine_final",
    )(xy, pt, h1, g2, final_g)


def _pad_lanes(a):
    return jnp.pad(a, ((0, 0), (0, LANES - a.shape[1])))


def kernel(x, c, ctx, c_ctx, w_mod, b_mod, norm1_g, w_in, w_conv_q, w_conv_k, gate_bias, head_norm_g, w_pool, pool_scale, w_out, norm2_g, w_group, b_group, w_router, b_router, w1, w3, w2, final_g):
    assert x.shape[0] == 1 and w_mod.shape[0] == 1
    seq = x.shape[1]
    x2d = x[0]
    ctx2d = ctx[0]

    n_main = 5 * MIX_HALF
    w_in_t = jnp.transpose(w_in[0])
    w_main = _cast_transposed_bf16(w_in_t, n_main)
    wg_pack, wg_hi = _split_pack(jnp.transpose(_take_rows(w_in_t, n_main, N_GATES)))
    gate_bias_row = _pad_lanes(gate_bias[0].reshape(1, N_GATES))
    w_out_bf = _cast_bf16(w_out[0])
    wr_pack, wr_hi = _split_pack(jnp.concatenate([w_group[0], w_router[0]], axis=1))
    b_route = _pad_lanes(jnp.concatenate([b_group[0], b_router[0]]).reshape(1, -1))
    norm1 = norm1_g[0].reshape(1, D_MODEL)

    c16 = jnp.concatenate([c, c_ctx[None, :], jnp.zeros((14, D_MODEL), F32)], axis=0)
    mods = _adaln(c16, w_mod[0], b_mod[0].reshape(1, -1))
    mod_lat = mods[0].reshape(6, D_MODEL)
    mod_ctx = mods[1].reshape(6, D_MODEL)

    proj = functools.partial(_inproj, g=norm1, w_main=w_main, wg_pack=wg_pack, wg_hi=wg_hi,
                             gate_bias_row=gate_bias_row, wcq=w_conv_q[0], wck=w_conv_k[0])
    _, _, _, k_c, kt_c, v_c, col_c, row_c = proj(ctx2d, mod_ctx[0:2])
    zeros_state = (jnp.zeros((2 * HEADS, HEAD_DIM, HEAD_DIM), F32),
                   jnp.zeros((2 * HEADS, 1, HEAD_DIM), F32),
                   jnp.zeros((2 * HEADS, 1, LANES), F32))
    s0, n0, m0 = _mlstm(None, k_c, kt_c, v_c, col_c, row_c, *zeros_state, need_out=False)

    u_pool, q, uo, k, kt, v, col, rowi = proj(x2d, mod_lat[0:2])
    hf, hb = _mlstm(q, k, kt, v, col, rowi, s0, n0, m0, need_out=True)
    p = _pool(u_pool, w_pool[0], pool_scale[0].reshape(1, -1))
    h1, fn, info = _outproj(p, hf, hb, uo, x2d, w_out_bf, mod_lat, head_norm_g[0].reshape(1, -1),
                            norm2_g[0].reshape(1, -1), wr_pack, wr_hi, b_route, tm=256)

    xs, ws, pt, cnt = _dispatch(fn, info)
    gsrc, gdst, texp, tend, ntile = _plan(cnt[::8, :N_EXPERTS], seq)
    xy = _experts(gsrc, gdst, texp, tend, ntile, xs, ws, w1[0], w3[0], w2[0])
    out = _combine(xy, pt, h1, mod_lat[5:6], final_g.reshape(1, -1))
    return out.reshape(1, seq, D_MODEL)
```

```python
import functools

import jax
import jax.numpy as jnp
from jax import lax
from jax.experimental import pallas as pl
from jax.experimental.pallas import tpu as pltpu

F32 = jnp.float32
BF16 = jnp.bfloat16

D_MODEL = 2048
GRID_W = 64
GRID_SHIFT = 6
POOL_WINDOWS = (2, 4, 8, 16)
POOL_GROUP = 256
HEADS = 4
HEAD_DIM = 256
MIX_HALF = 1024
N_GATES = 16
N_GROUPS = 4
EXPERTS_PER_GROUP = 8
N_EXPERTS = 32
D_EXPERT = 512
EPS = 1e-6
LANES = 128
CHUNK = 256
ROUTE_LANE0 = N_GROUPS

VMEM_LIMIT = 56 * 1024 * 1024


def _cparams(sem, vmem=VMEM_LIMIT):
    return pltpu.CompilerParams(dimension_semantics=sem, vmem_limit_bytes=vmem)


def _split2(x):
    hi = x.astype(BF16)
    lo = (x - hi.astype(F32)).astype(BF16)
    return hi, lo


def _split3(x):
    hi = x.astype(BF16)
    r = x - hi.astype(F32)
    mid = r.astype(BF16)
    lo = (r - mid.astype(F32)).astype(BF16)
    return hi, mid, lo


def _dot(a, b):
    return jnp.dot(a, b, preferred_element_type=F32)


SPLIT_LANE = 64


def _split_pack(w):
    hi, lo = _split2(w)
    n = w.shape[1]
    packed = jnp.zeros((w.shape[0], LANES), BF16).at[:, :n].set(hi).at[:, SPLIT_LANE:SPLIT_LANE + n].set(lo)
    return packed, jnp.pad(hi, ((0, 0), (0, LANES - n)))


def _split_dot(xh, xl, w_packed, w_hi):
    r = _dot(xh, w_packed)
    return r + pltpu.roll(r, SPLIT_LANE, 1) + _dot(xl, w_hi)


def _silu(x):
    return x * jax.nn.sigmoid(x)


def _log_sigmoid(x):
    return jnp.minimum(x, 0.0) - jnp.log(1.0 + jnp.exp(-jnp.abs(x)))


def _cast_kernel(w_ref, o_ref):
    o_ref[...] = w_ref[...].astype(BF16)


def _cast_bf16(w, rows=256):
    k, n = w.shape
    return pl.pallas_call(
        _cast_kernel,
        grid=(k // rows,),
        in_specs=[pl.BlockSpec((rows, n), lambda i: (i, 0))],
        out_specs=pl.BlockSpec((rows, n), lambda i: (i, 0)),
        out_shape=jax.ShapeDtypeStruct((k, n), BF16),
        compiler_params=_cparams(("arbitrary",)),
        name="cast_bf16",
    )(w)


def _copy_kernel(w_ref, o_ref):
    o_ref[...] = w_ref[...]


def _take_rows(w, start, n):
    return pl.pallas_call(
        _copy_kernel,
        grid=(1,),
        in_specs=[pl.BlockSpec((n, w.shape[1]), lambda i: (start // n, 0))],
        out_specs=pl.BlockSpec((n, w.shape[1]), lambda i: (0, 0)),
        out_shape=jax.ShapeDtypeStruct((n, w.shape[1]), w.dtype),
        name="take_rows",
    )(w)


def _adaln_kernel(c_ref, w_ref, b_ref, o_ref):
    a = _silu(c_ref[...])
    a3 = jnp.concatenate(_split3(a), axis=0)
    w_hi, w_lo = _split2(w_ref[...])
    acc = _dot(a3, w_hi)
    acc_lo = _dot(a3[:32], w_lo)
    out = acc[0:16] + acc[16:32] + acc[32:48] + acc_lo[0:16] + acc_lo[16:32]
    o_ref[...] = out + b_ref[...]


def _adaln(c16, w_mod, b_mod):
    n = w_mod.shape[1]
    tn = 1536
    return pl.pallas_call(
        _adaln_kernel,
        grid=(n // tn,),
        in_specs=[pl.BlockSpec((16, D_MODEL), lambda j: (0, 0)),
                  pl.BlockSpec((D_MODEL, tn), lambda j: (0, j)),
                  pl.BlockSpec((1, tn), lambda j: (0, j))],
        out_specs=pl.BlockSpec((16, tn), lambda j: (0, j)),
        out_shape=jax.ShapeDtypeStruct((16, n), F32),
        compiler_params=_cparams(("arbitrary",)),
        name="adaln",
    )(c16, w_mod, b_mod)


HALO = 8
NEG_INF = float("-inf")


def _gate_scan_info(gates):
    n = gates.shape[0]
    lane = lax.broadcasted_iota(jnp.int32, gates.shape, 1)
    rows = lax.broadcasted_iota(jnp.int32, gates.shape, 0)
    lf = jnp.where(lane < N_GATES, _log_sigmoid(gates), 0.0)
    hi = lf.astype(BF16).astype(F32)
    rem = lf - hi
    mid = rem.astype(BF16).astype(F32)
    packed = (hi + pltpu.roll(mid, 32, 1) + pltpu.roll(rem - mid, 64, 1)).astype(BF16)
    r = lax.broadcasted_iota(jnp.int32, (n, n), 0)
    c = lax.broadcasted_iota(jnp.int32, (n, n), 1)
    pf = _dot(jnp.where(r >= c, 1.0, 0.0).astype(BF16), packed)
    pb = _dot(jnp.where(r <= c, 1.0, 0.0).astype(BF16), packed)
    bf = pf + pltpu.roll(pf, 96, 1) + pltpu.roll(pf, 64, 1)
    bb = pb + pltpu.roll(pb, 96, 1) + pltpu.roll(pb, 64, 1)
    b = jnp.where(lane < 8, bf, bb)
    cval = gates - pltpu.roll(b, LANES - 4, 1)
    pm = cval
    sm = cval
    step = 1
    while step < n:
        pm = jnp.maximum(pm, jnp.where(rows >= step, pltpu.roll(pm, step, 0), NEG_INF))
        sm = jnp.maximum(sm, jnp.where(rows < n - step, pltpu.roll(sm, n - step, 0), NEG_INF))
        step *= 2
    cm = jnp.where(lane < 8, pm, sm)
    return cval, jnp.where((lane & 4) == 0, cm, b)


W_CHUNK = 256


def _inproj_kernel(x_ref, xp_ref, xn_ref, ctx_ref, mod_ref, g_ref, wt_hbm, wgp_ref, wgh_ref, gb_ref,
                   wcq_ref, wck_ref, pool_o, q_o, o_o, k_o, kt_o, v_o, col_o, row_o,
                   w_s, stage, sem, *, nt):
    i = pl.program_id(0)
    tm = x_ref.shape[0]
    n_chunks = w_s.shape[1] // W_CHUNK

    def chunk_copy(c, sl):
        return pltpu.make_async_copy(
            wt_hbm.at[pl.ds(pl.multiple_of(c * W_CHUNK, W_CHUNK), W_CHUNK), :], stage.at[sl], sem.at[sl])

    @pl.when(i == 0)
    def _():
        chunk_copy(0, 0).start()

        def convert(c, carry):
            sl = c % 2

            @pl.when(c + 1 < n_chunks)
            def _():
                chunk_copy(c + 1, 1 - sl).start()

            chunk_copy(c, sl).wait()
            w_s[:, pl.ds(pl.multiple_of(c * W_CHUNK, W_CHUNK), W_CHUNK)] = stage[sl].T.astype(BF16)
            return carry

        lax.fori_loop(0, n_chunks, convert, 0)

    is_ctx = i == nt
    x_main = jnp.where(is_ctx, ctx_ref[...], x_ref[...])
    x_all = jnp.concatenate([xp_ref[...], x_main, xn_ref[...]], axis=0)
    shift = jnp.where(is_ctx, mod_ref[0:1, :], mod_ref[2:3, :])
    scale = jnp.where(is_ctx, mod_ref[1:2, :], mod_ref[3:4, :])
    ms = jnp.mean(x_all * x_all, axis=-1, keepdims=True)
    y = x_all * lax.rsqrt(ms + EPS) * g_ref[...]
    xn_all = y * (1.0 + scale) + shift
    xh_all = xn_all.astype(BF16)
    xh, xl = _split2(xn_all[HALO:HALO + tm])

    def cols(ci):
        return w_s[:, ci * MIX_HALF:(ci + 1) * MIX_HALF]

    u_q = _dot(xh_all, cols(1))
    u_k = _dot(xh_all, cols(3))
    pool_o[...] = _dot(xh, cols(0))
    o_o[...] = _dot(xh, cols(2))
    v_o[...] = _dot(xh, cols(4)).astype(BF16)

    rowi = lax.broadcasted_iota(jnp.int32, (tm, MIX_HALF), 0)
    at_start = jnp.logical_and(rowi == 0, jnp.logical_or(i == 0, is_ctx))
    at_end = jnp.logical_and(rowi == tm - 1, jnp.logical_or(i == nt - 1, is_ctx))

    def conv_silu(u, wc_ref):
        n = u.shape[0]
        up = jnp.where(at_start, 0.0, pltpu.roll(u, 1, 0)[HALO:HALO + tm])
        un = jnp.where(at_end, 0.0, pltpu.roll(u, n - 1, 0)[HALO:HALO + tm])
        return _silu(wc_ref[0:1, :] * up + wc_ref[1:2, :] * u[HALO:HALO + tm] + wc_ref[2:3, :] * un)

    q_o[...] = (conv_silu(u_q, wcq_ref) * (HEAD_DIM ** -0.5)).astype(BF16)
    k = conv_silu(u_k, wck_ref)
    k_o[...] = k.astype(BF16)
    kt_o[...] = k.astype(BF16).T
    gates = _split_dot(xh, xl, wgp_ref[...], wgh_ref[...]) + gb_ref[...]
    cval, col = _gate_scan_info(gates)
    col_o[...] = col[:, :N_GATES]
    row_o[...] = cval.T[:N_GATES, :]


def _inproj(x2d, ctx2d, mod, g, w_in_t, wg_pack, wg_hi, gate_bias_row, wcq, wck):
    t = x2d.shape[0]
    tm = CHUNK
    assert ctx2d.shape[0] == tm
    nt = t // tm
    r8 = tm // HALO
    last8 = t // HALO - 1
    n_main = 5 * MIX_HALF
    rows = t + tm
    const = lambda i: (0, 0)
    row = lambda i: (i, 0)
    f32_out = jax.ShapeDtypeStruct((rows, MIX_HALF), F32)
    bf16_out = jax.ShapeDtypeStruct((rows, MIX_HALF), BF16)
    seq = pl.BlockSpec((tm, MIX_HALF), row)
    return pl.pallas_call(
        functools.partial(_inproj_kernel, nt=nt),
        grid=(nt + 1,),
        in_specs=[pl.BlockSpec((tm, D_MODEL), lambda i: (jnp.minimum(i, nt - 1), 0)),
                  pl.BlockSpec((HALO, D_MODEL), lambda i: (jnp.clip(i * r8 - 1, 0, last8), 0)),
                  pl.BlockSpec((HALO, D_MODEL), lambda i: (jnp.minimum((i + 1) * r8, last8), 0)),
                  pl.BlockSpec((tm, D_MODEL), const),
                  pl.BlockSpec((4, D_MODEL), const),
                  pl.BlockSpec((1, D_MODEL), const),
                  pl.BlockSpec(memory_space=pl.ANY),
                  pl.BlockSpec((D_MODEL, LANES), const),
                  pl.BlockSpec((D_MODEL, LANES), const),
                  pl.BlockSpec((1, LANES), const),
                  pl.BlockSpec((3, MIX_HALF), const),
                  pl.BlockSpec((3, MIX_HALF), const)],
        out_specs=[seq, seq, seq, seq, pl.BlockSpec((MIX_HALF, tm), lambda i: (0, i)), seq,
                   pl.BlockSpec((tm, N_GATES), row), pl.BlockSpec((N_GATES, tm), lambda i: (0, i))],
        out_shape=[f32_out, bf16_out, f32_out, bf16_out,
                   jax.ShapeDtypeStruct((MIX_HALF, rows), BF16), bf16_out,
                   jax.ShapeDtypeStruct((rows, N_GATES), F32),
                   jax.ShapeDtypeStruct((N_GATES, rows), F32)],
        scratch_shapes=[pltpu.VMEM((D_MODEL, n_main), BF16),
                        pltpu.VMEM((2, W_CHUNK, D_MODEL), F32),
                        pltpu.SemaphoreType.DMA((2,))],
        compiler_params=_cparams(("arbitrary",)),
        name="inproj",
    )(x2d, x2d, x2d, ctx2d, mod, g, w_in_t, wg_pack, wg_hi, gate_bias_row, wcq, wck)


def _mlstm_kernel(*refs, need_out):
    if need_out:
        (qf, kf, ktf, vf, colf, rowf, qb, kb, ktb, vb, colb, rowb, s0, n0, m0,
         hf_o, hb_o, s_s, n_s, m_s) = refs
        q_refs, h_outs = (qf, qb), (hf_o, hb_o)
    else:
        (kf, ktf, vf, colf, rowf, kb, ktb, vb, colb, rowb, s0, n0, m0,
         s_o, n_o, m_o, s_s, n_s, m_s) = refs
    k_refs, kt_refs, v_refs = (kf, kb), (ktf, ktb), (vf, vb)
    col_refs, row_refs = (colf, colb), (rowf, rowb)
    j = pl.program_id(0)

    @pl.when(j == 0)
    def _():
        s_s[...] = s0[...]
        n_s[...] = n0[...]
        m_s[...] = m0[...]

    L = CHUNK
    row = lax.broadcasted_iota(jnp.int32, (L, L), 0)
    col = lax.broadcasted_iota(jnp.int32, (L, L), 1)

    heads = [(d, h) for d in range(2) for h in range(HEADS)]

    def head_values(d, h):
        hd = d * HEADS + h
        lc, lb = 8 * d + h, 8 * d + 4 + h
        hs = slice(h * HEAD_DIM, (h + 1) * HEAD_DIM)
        edge = L - 1 if d == 0 else 0
        colv = col_refs[d][...]
        rowv = row_refs[d][...]
        v = dict(hd=hd, hs=hs, d=d)
        v["cm_c"], v["b_c"] = colv[:, lc:lc + 1], colv[:, lb:lb + 1]
        v["c_r"] = rowv[lc:lc + 1, :]
        v["g"] = v["b_c"][edge:edge + 1, :]
        v["m_old"] = m_s[hd][:, 0:1]
        v["s_old"] = s_s[hd]
        v["n_old"] = n_s[hd]
        v["m_x"] = jnp.maximum(v["m_old"], v["cm_c"][edge:edge + 1, :])
        v["decay"] = jnp.exp(v["m_old"] - v["m_x"])
        v["wk_r"] = jnp.exp(v["c_r"] - v["m_x"])
        v["v_h"] = v_refs[d][:, hs]
        v["kt_h"] = kt_refs[d][hs, :]
        return v

    def update_state(v):
        hd = v["hd"]
        n_s[hd] = v["decay"] * v["n_old"] + _dot(
            jnp.broadcast_to(v["wk_r"], (8, L)).astype(BF16), k_refs[v["d"]][:, v["hs"]])[0:1, :]
        kwt = (v["kt_h"].astype(F32) * v["wk_r"]).astype(BF16)
        s_s[hd] = v["decay"] * v["s_old"] + _dot(kwt, v["v_h"])
        m_s[hd] = jnp.broadcast_to(v["g"] + v["m_x"], (1, LANES))

    def weights(v):
        d, hs = v["d"], v["hs"]
        mask = (row >= col) if d == 0 else (row <= col)
        q_h = q_refs[d][:, hs]
        m_c = jnp.maximum(v["m_old"], v["cm_c"])
        w_inter = jnp.exp(v["m_old"] - m_c)
        v["e"] = jnp.where(mask, jnp.exp(v["c_r"] - m_c), 0.0)
        v["qw"] = (q_h.astype(F32) * w_inter).astype(BF16)
        qn = lax.dot_general(q_h, jnp.broadcast_to(v["n_old"], (8, HEAD_DIM)).astype(BF16),
                             (((1,), (1,)), ((), ())), preferred_element_type=F32)
        v["den_inter"] = w_inter * qn[:, 0:1]
        v["floor"] = jnp.exp(-(v["b_c"] + m_c))
        v["s_bf"] = v["s_old"].astype(BF16)

    def outputs(v, qk):
        p = v["e"] * qk
        num = _dot(v["qw"], v["s_bf"]) + _dot(p.astype(BF16), v["v_h"])
        den = v["den_inter"] + jnp.sum(p[:, :LANES] + p[:, LANES:], axis=1, keepdims=True)
        h_outs[v["d"]][:, v["hs"]] = num / jnp.maximum(jnp.abs(den), v["floor"])

    vals = [head_values(d, h) for d, h in heads]
    if need_out:
        qks = [_dot(q_refs[v["d"]][:, v["hs"]], v["kt_h"]) for v in vals]
        for v in vals:
            weights(v)
    for v in vals:
        update_state(v)
    if need_out:
        for v, qk in zip(vals, qks):
            outputs(v, qk)

    if not need_out:
        s_o[...] = s_s[...]
        n_o[...] = n_s[...]
        m_o[...] = m_s[...]


def _mlstm(q, k, kt, v, col, rowi, s0, n0, m0, need_out, first, nc):
    t = nc * CHUNK
    fwd = lambda j: (first + j, 0)
    bwd = lambda j: (first + nc - 1 - j, 0)
    fwd_t = lambda j: (0, first + j)
    bwd_t = lambda j: (0, first + nc - 1 - j)
    c3 = lambda j: (0, 0, 0)
    seq = lambda im: pl.BlockSpec((CHUNK, MIX_HALF), im)
    state_specs = [pl.BlockSpec(s0.shape, c3), pl.BlockSpec(n0.shape, c3), pl.BlockSpec(m0.shape, c3)]
    scratch = [pltpu.VMEM(s0.shape, F32), pltpu.VMEM(n0.shape, F32), pltpu.VMEM(m0.shape, F32)]

    def side(im, im_t):
        specs = ([seq(im)] if need_out else []) + [seq(im), pl.BlockSpec((MIX_HALF, CHUNK), im_t), seq(im)]
        return specs + [pl.BlockSpec((CHUNK, N_GATES), im), pl.BlockSpec((N_GATES, CHUNK), im_t)]

    in_specs = side(fwd, fwd_t) + side(bwd, bwd_t) + state_specs
    seq_in = ((q,) if need_out else ()) + (k, kt, v, col, rowi)
    args = seq_in + seq_in + (s0, n0, m0)
    if need_out:
        out_specs = [seq(fwd), seq(bwd)]
        out_shape = [jax.ShapeDtypeStruct((t, MIX_HALF), F32)] * 2
    else:
        out_specs = state_specs
        out_shape = [jax.ShapeDtypeStruct(a.shape, F32) for a in (s0, n0, m0)]
    return pl.pallas_call(
        functools.partial(_mlstm_kernel, need_out=need_out),
        grid=(nc,),
        in_specs=in_specs,
        out_specs=out_specs,
        out_shape=out_shape,
        scratch_shapes=scratch,
        compiler_params=_cparams(("arbitrary",)),
        name="mlstm_out" if need_out else "mlstm_state",
    )(*args)


POOL_PAD = 512
POOL_UNROLL = 4


def _pool_kernel(u_ref, w_ref, sc_ref, o_ref, pad_s, *, t):
    for gi, win in enumerate(POOL_WINDOWS):
        @pl.when(pl.program_id(0) == gi)
        def _():
            _pool_group(u_ref, w_ref, sc_ref, o_ref, pad_s, win=win, t=t)


def _pool_group(u_ref, w_ref, sc_ref, o_ref, pad_s, *, win, t):
    half = win // 2
    tile = 256
    zeros = jnp.zeros((POOL_PAD, POOL_GROUP), F32)
    pad_s[0:POOL_PAD, :] = zeros
    pad_s[POOL_PAD + t:POOL_PAD + t + POOL_PAD, :] = zeros

    def copy(r, carry):
        t0 = pl.multiple_of(r * tile, tile)
        pad_s[pl.ds(POOL_PAD + t0, tile), :] = u_ref[pl.ds(t0, tile), :]
        return carry

    lax.fori_loop(0, t // tile, copy, 0)

    row = lax.broadcasted_iota(jnp.int32, (tile, tile), 0)
    col = lax.broadcasted_iota(jnp.int32, (tile, tile), 1)
    same_row = (row >> GRID_SHIFT) == (col >> GRID_SHIFT)
    in_win = (col - row >= -half) & (col - row < half)
    band = jnp.where(same_row & in_win, 1.0, 0.0).astype(BF16)
    w = w_ref[...].astype(BF16)
    scale = sc_ref[...]
    n_rows = t // GRID_W

    tok0 = lax.broadcasted_iota(jnp.int32, (tile, POOL_GROUP), 0)
    gc = tok0 & (GRID_W - 1)
    inv_h = 1.0 / (jnp.minimum(gc + half, GRID_W) - jnp.maximum(gc - half, 0)).astype(F32)

    def body(r, carry):
        t0s = [pl.multiple_of((r * POOL_UNROLL + k) * tile, tile) for k in range(POOL_UNROLL)]
        pieces = []
        for t0 in t0s:
            acc = pad_s[pl.ds(POOL_PAD + t0 - GRID_W * half, tile), :]
            for dd in range(-half + 1, half):
                acc = acc + pad_s[pl.ds(POOL_PAD + t0 + GRID_W * dd, tile), :]
            gr = (t0 + tok0) >> GRID_SHIFT
            cnt_v = jnp.minimum(gr + half, n_rows) - jnp.maximum(gr - half, 0)
            pieces.append(_split2(acc / cnt_v.astype(F32)))
        means = [(_dot(band, hi) + _dot(band, lo)) * inv_h for hi, lo in pieces]
        diffs = [(m - pad_s[pl.ds(POOL_PAD + t0, tile), :]).astype(BF16) for m, t0 in zip(means, t0s)]
        for d, t0 in zip(diffs, t0s):
            o_ref[pl.ds(t0, tile), :] = (_dot(d, w) * scale).astype(BF16)
        return carry

    lax.fori_loop(0, t // (tile * POOL_UNROLL), body, 0)


def _pool(u_pool, w_pool, scale_row, t):
    return pl.pallas_call(
        functools.partial(_pool_kernel, t=t),
        grid=(len(POOL_WINDOWS),),
        in_specs=[pl.BlockSpec((t, POOL_GROUP), lambda g: (0, g)),
                  pl.BlockSpec((None, POOL_GROUP, POOL_GROUP), lambda g: (g, 0, 0)),
                  pl.BlockSpec((1, POOL_GROUP), lambda g: (0, g))],
        out_specs=pl.BlockSpec((t, POOL_GROUP), lambda g: (0, g)),
        out_shape=jax.ShapeDtypeStruct((t, MIX_HALF), BF16),
        scratch_shapes=[pltpu.VMEM((t + 2 * POOL_PAD, POOL_GROUP), F32)],
        compiler_params=_cparams(("arbitrary",)),
        name="pool_mix",
    )(u_pool, w_pool, scale_row)


def _route(logits):
    lane = lax.broadcasted_iota(jnp.int32, logits.shape, 1).astype(F32)
    neg = -jnp.inf
    big = float(LANES)
    gl = jnp.where(lane < N_GROUPS, logits, neg)
    gmax = jnp.max(gl, axis=1, keepdims=True)
    gsel = jnp.min(jnp.where(gl == gmax, lane, big), axis=1, keepdims=True)
    p_grp = 1.0 / jnp.sum(jnp.exp(gl - gmax), axis=1, keepdims=True)
    lo = ROUTE_LANE0 + EXPERTS_PER_GROUP * gsel
    el = jnp.where((lane >= lo) & (lane < lo + EXPERTS_PER_GROUP), logits, neg)
    m1 = jnp.max(el, axis=1, keepdims=True)
    i1 = jnp.min(jnp.where(el == m1, lane, big), axis=1, keepdims=True)
    el2 = jnp.where(lane == i1, neg, el)
    m2 = jnp.max(el2, axis=1, keepdims=True)
    i2 = jnp.min(jnp.where(el2 == m2, lane, big), axis=1, keepdims=True)
    e2 = jnp.exp(m2 - m1)
    p1 = 1.0 / (1.0 + e2)
    p2 = e2 / (1.0 + e2)
    info = jnp.where(lane == 0.0, i1 - ROUTE_LANE0, 0.0)
    info = jnp.where(lane == 1.0, i2 - ROUTE_LANE0, info)
    info = jnp.where(lane == 2.0, p_grp * p1, info)
    return jnp.where(lane == 3.0, p_grp * p2, info)


def _outproj_kernel(p_ref, hf_ref, hb_ref, uo_ref, x_ref, wout_ref, mod_ref,
                    hg_ref, n2g_ref, wrp_ref, wrh_ref, br_ref, h1_o, fn_o, info_o):
    h = hf_ref[...] + hb_ref[...]
    parts = []
    for hh in range(HEADS):
        hs = h[:, hh * HEAD_DIM:(hh + 1) * HEAD_DIM]
        mu = jnp.mean(hs, axis=-1, keepdims=True)
        ctr = hs - mu
        var = jnp.mean(ctr * ctr, axis=-1, keepdims=True)
        parts.append(ctr * lax.rsqrt(var + EPS))
    hn = jnp.concatenate(parts, axis=1) * hg_ref[...]
    m = (hn * jax.nn.sigmoid(uo_ref[...])).astype(BF16)
    mix = _dot(jnp.concatenate([p_ref[...], m], axis=1), wout_ref[...])
    h1 = x_ref[...] + mod_ref[2:3, :] * mix
    h1_o[...] = h1
    ms = jnp.mean(h1 * h1, axis=-1, keepdims=True)
    fn = h1 * lax.rsqrt(ms + EPS) * n2g_ref[...]
    fn = fn * (1.0 + mod_ref[4:5, :]) + mod_ref[3:4, :]
    fh, fl = _split2(fn)
    fn_o[...] = fh
    logits = _split_dot(fh, fl, wrp_ref[...], wrh_ref[...]) + br_ref[...]
    info_o[...] = _route(logits)


def _outproj(p, hf, hb, uo, x2d, w_out, mod, head_g, norm2_g, wr_pack, wr_hi, b_route, tm):
    t = x2d.shape[0]
    const = lambda i: (0, 0)
    row = lambda i: (i, 0)
    in_specs = ([pl.BlockSpec((tm, MIX_HALF), row)] * 4
                + [pl.BlockSpec((tm, D_MODEL), row),
                   pl.BlockSpec(w_out.shape, const, pipeline_mode=pl.Buffered(1)),
                   pl.BlockSpec(mod.shape, const),
                   pl.BlockSpec((1, MIX_HALF), const),
                   pl.BlockSpec((1, D_MODEL), const),
                   pl.BlockSpec((D_MODEL, LANES), const),
                   pl.BlockSpec((D_MODEL, LANES), const),
                   pl.BlockSpec((1, LANES), const)])
    return pl.pallas_call(
        _outproj_kernel,
        grid=(t // tm,),
        in_specs=in_specs,
        out_specs=[pl.BlockSpec((tm, D_MODEL), row), pl.BlockSpec((tm, D_MODEL), row),
                   pl.BlockSpec((tm, LANES), row)],
        out_shape=[jax.ShapeDtypeStruct((t, D_MODEL), F32),
                   jax.ShapeDtypeStruct((t, D_MODEL), BF16),
                   jax.ShapeDtypeStruct((t, LANES), F32)],
        compiler_params=_cparams(("arbitrary",)),
        name="outproj_route",
    )(p, hf, hb, uo, x2d, w_out, mod, head_g, norm2_g, wr_pack, wr_hi, b_route)


DISPATCH_BLOCK = 512
GRANULE = 16
GRANULE_SHIFT = 4
LOCAL_CAP = 1536
LOCAL_GRANULES = LOCAL_CAP // GRANULE
FREE_GRANULES = 2
EXPERT_TILE = 256
TILE_GRANULES = EXPERT_TILE // GRANULE
PLAN_UNROLL = 4
PLAN_SLACK = 8
GATHER_DEPTH = 3


def _dispatch_kernel(fn_ref, info_ref, xs_o, ws_o, pt_o, cnt_o):
    tb = DISPATCH_BLOCK
    info = info_ref[...]
    e1, e2 = info[:, 0:1], info[:, 1:2]
    w1c, w2c = info[:, 2:3], info[:, 3:4]
    lane = lax.broadcasted_iota(jnp.int32, (tb, LANES), 1).astype(F32)
    o1 = jnp.where(lane == e1, 1.0, 0.0)
    o2 = jnp.where(lane == e2, 1.0, 0.0)
    onehot = o1 + o2
    cnt = jnp.sum(onehot, axis=0, keepdims=True)
    gran = jnp.floor((cnt + (GRANULE - 1)) * (1.0 / GRANULE))
    a = lax.broadcasted_iota(jnp.int32, (LANES, LANES), 0)
    b = lax.broadcasted_iota(jnp.int32, (LANES, LANES), 1)
    upper = jnp.where(a < b, 1.0, 0.0).astype(BF16)
    seg_off = _dot(jnp.broadcast_to(gran, (8, LANES)).astype(BF16), upper)[0:1, :] * GRANULE
    r = lax.broadcasted_iota(jnp.int32, (tb, tb), 0)
    c = lax.broadcasted_iota(jnp.int32, (tb, tb), 1)
    strict = jnp.where(r > c, 1.0, 0.0).astype(BF16)
    rank = _dot(strict, onehot.astype(BF16))
    slot = rank + seg_off
    pos1 = jnp.sum(o1 * slot, axis=1, keepdims=True)
    pos2 = jnp.sum(o2 * slot, axis=1, keepdims=True)
    rows = lax.broadcasted_iota(jnp.int32, (tb, LOCAL_CAP), 1).astype(F32)
    pt1 = jnp.where(rows == pos1, 1.0, 0.0)
    pt2 = jnp.where(rows == pos2, 1.0, 0.0)
    pt = pt1 + pt2
    pt_o[...] = pt.astype(BF16)
    perm = pt.T.astype(BF16)
    w_slot = jnp.sum((pt1 * w1c + pt2 * w2c).T, axis=1, keepdims=True)
    for c0 in range(0, D_MODEL, 512):
        xs_o[:, c0:c0 + 512] = _dot(perm, fn_ref[:, c0:c0 + 512]).astype(BF16)
    ws_o[...] = jnp.broadcast_to(w_slot, (LOCAL_CAP, LANES))
    cnt_o[...] = jnp.broadcast_to(cnt, (8, LANES)).astype(jnp.int32)


def _dispatch(fn, info):
    t = fn.shape[0]
    nb = t // DISPATCH_BLOCK
    row = lambda i: (i, 0)
    return pl.pallas_call(
        _dispatch_kernel,
        grid=(nb,),
        in_specs=[pl.BlockSpec((DISPATCH_BLOCK, D_MODEL), row),
                  pl.BlockSpec((DISPATCH_BLOCK, LANES), row)],
        out_specs=[pl.BlockSpec((LOCAL_CAP, D_MODEL), row),
                   pl.BlockSpec((LOCAL_CAP, LANES), row),
                   pl.BlockSpec((DISPATCH_BLOCK, LOCAL_CAP), row),
                   pl.BlockSpec((8, LANES), row)],
        out_shape=[jax.ShapeDtypeStruct((nb * LOCAL_CAP, D_MODEL), BF16),
                   jax.ShapeDtypeStruct((nb * LOCAL_CAP, LANES), F32),
                   jax.ShapeDtypeStruct((t, LOCAL_CAP), BF16),
                   jax.ShapeDtypeStruct((nb * 8, LANES), jnp.int32)],
        compiler_params=_cparams(("arbitrary",)),
        name="moe_dispatch",
    )(fn, info)


def _map_len(max_tiles):
    return (max_tiles + GATHER_DEPTH - 1) * TILE_GRANULES + PLAN_SLACK


def _free_granule(q):
    return ((q // FREE_GRANULES) * LOCAL_GRANULES + (LOCAL_GRANULES - FREE_GRANULES)
            + (q % FREE_GRANULES))


def _plan_kernel(cnt_ref, gsrc_o, gdst_o, texp_o, tend_o, ntile_o, lrun, *, nb, max_tiles):
    def init(b, c):
        lrun[b] = 0
        return c

    lax.fori_loop(0, nb, init, 0)

    def per_expert(e, carry):
        g0, last_e = carry

        def per_block(b, g):
            k = (cnt_ref[b, e] + (GRANULE - 1)) >> GRANULE_SHIFT
            lo = lrun[b]
            lrun[b] = lo + k

            base = b * LOCAL_GRANULES + lo

            for j in range(PLAN_UNROLL):
                gsrc_o[g + j] = base + j
                gdst_o[g + j] = base + j

            @pl.when(k > PLAN_UNROLL)
            def _():
                def put(j, c):
                    gsrc_o[g + j] = base + j
                    gdst_o[g + j] = base + j
                    return c

                lax.fori_loop(PLAN_UNROLL, k, put, 0)

            return g + k

        g1 = lax.fori_loop(0, nb, per_block, g0)
        pad = (-g1) & (TILE_GRANULES - 1)

        def put_pad(j, c):
            g = g1 + j
            parity = (g // TILE_GRANULES) & 1
            gsrc_o[g] = _free_granule(0)
            gdst_o[g] = _free_granule(1 + parity * (TILE_GRANULES - 1) + (g & (TILE_GRANULES - 1)))
            return c

        lax.fori_loop(0, pad, put_pad, 0)
        g2 = g1 + pad

        def put_tile(tt, c):
            texp_o[tt] = e
            return c

        lax.fori_loop(g0 // TILE_GRANULES, g2 // TILE_GRANULES, put_tile, 0)
        tend_o[e] = g2 // TILE_GRANULES
        return g2, jnp.where(g2 > g0, e, last_e)

    g_end, last_e = lax.fori_loop(0, N_EXPERTS, per_expert, (0, 0))
    n_tiles = g_end // TILE_GRANULES
    ntile_o[0] = n_tiles

    def fill(tt, c):
        texp_o[tt] = last_e
        return c

    lax.fori_loop(n_tiles, max_tiles, fill, 0)

    def fill_map(g, c):
        gsrc_o[g] = _free_granule(0)
        gdst_o[g] = _free_granule(0)
        return c

    lax.fori_loop(g_end, _map_len(max_tiles), fill_map, 0)


def _max_tiles(t):
    nb = t // DISPATCH_BLOCK
    worst_rows = 2 * t + nb * N_EXPERTS * (GRANULE - 1) + N_EXPERTS * (EXPERT_TILE - GRANULE)
    return -(-worst_rows // EXPERT_TILE)


def _plan(cnt, t):
    nb = cnt.shape[0]
    assert nb * FREE_GRANULES >= 2 + 2 * (TILE_GRANULES - 1)
    max_tiles = _max_tiles(t)
    smem = pl.BlockSpec(memory_space=pltpu.SMEM)
    n_map = _map_len(max_tiles)
    return pl.pallas_call(
        functools.partial(_plan_kernel, nb=nb, max_tiles=max_tiles),
        in_specs=[smem],
        out_specs=[smem, smem, smem, smem, smem],
        out_shape=[jax.ShapeDtypeStruct((n_map,), jnp.int32),
                   jax.ShapeDtypeStruct((n_map,), jnp.int32),
                   jax.ShapeDtypeStruct((max_tiles,), jnp.int32),
                   jax.ShapeDtypeStruct((N_EXPERTS,), jnp.int32),
                   jax.ShapeDtypeStruct((1,), jnp.int32)],
        scratch_shapes=[pltpu.SMEM((nb,), jnp.int32)],
        name="moe_plan",
    )(cnt)


def _experts_kernel(gsrc, gdst, texp, tend, ntile, xy_in, ws_in, w1_hbm, w3_hbm, w2_hbm, xy_out,
                    xbuf, wsbuf, ybuf, st1, st3, st2, wb1, wb3, wb2, gsem, ssem, wsem):
    nt = ntile[0]

    def weight_copies(e, ws):
        return (pltpu.make_async_copy(w1_hbm.at[e], st1.at[ws], wsem.at[ws]),
                pltpu.make_async_copy(w3_hbm.at[e], st3.at[ws], wsem.at[ws]),
                pltpu.make_async_copy(w2_hbm.at[e], st2.at[ws], wsem.at[ws]))

    def rows(i):
        return pl.ds(pl.multiple_of(i * GRANULE, GRANULE), GRANULE)

    def gather_copies(g, j, sl):
        return (pltpu.make_async_copy(xy_in.at[rows(g), :], xbuf.at[sl, rows(j), :], gsem.at[sl]),
                pltpu.make_async_copy(ws_in.at[rows(g), :], wsbuf.at[sl, rows(j), :], gsem.at[sl]))

    def scatter_copies(g, j, sl):
        return (pltpu.make_async_copy(ybuf.at[sl, rows(j), :], xy_out.at[rows(g), :], ssem.at[sl]),)

    def issue(tt, sl, gmap, copies):
        for j in range(TILE_GRANULES):
            for cp in copies(gmap[tt * TILE_GRANULES + j], j, sl):
                cp.start()

    def drain(sl, copies):
        full = pl.ds(0, EXPERT_TILE)
        if copies is gather_copies:
            pltpu.make_async_copy(xy_in.at[full, :], xbuf.at[sl], gsem.at[sl]).wait()
            pltpu.make_async_copy(ws_in.at[full, :], wsbuf.at[sl], gsem.at[sl]).wait()
        else:
            pltpu.make_async_copy(ybuf.at[sl], xy_out.at[full, :], ssem.at[sl]).wait()

    @pl.when(nt > 0)
    def _():
        last = nt - 1
        for cp in weight_copies(texp[0], 0):
            cp.start()
        for ahead in range(GATHER_DEPTH - 1):
            issue(ahead, ahead, gsrc, gather_copies)

        def tile(t, wslot):
            slot = t % 2
            gslot = t % GATHER_DEPTH
            e = texp[t]
            first = (t == 0) | (texp[jnp.maximum(t - 1, 0)] != e)
            wslot = jnp.where(first & (t > 0), 1 - wslot, wslot)

            drain(gslot, gather_copies)
            ahead = t + GATHER_DEPTH - 1
            issue(ahead, ahead % GATHER_DEPTH, gsrc, gather_copies)

            @pl.when(first)
            def _():
                for cp in weight_copies(e, wslot):
                    cp.wait()
                wb1[...] = st1[wslot].astype(BF16)
                wb3[...] = st3[wslot].astype(BF16)
                wb2[...] = st2[wslot].astype(BF16)
                nxt = tend[e]

                @pl.when(nxt < nt)
                def _():
                    for cp in weight_copies(texp[jnp.minimum(nxt, last)], 1 - wslot):
                        cp.start()

            x = xbuf[gslot]
            w_row = wsbuf[gslot][:, 0:1]
            a = _dot(x, wb1[...])
            b = _dot(x, wb3[...])
            y = _dot((_silu(a) * b).astype(BF16), wb2[...]) * w_row

            @pl.when(t >= 2)
            def _():
                drain(slot, scatter_copies)

            ybuf[slot] = y.astype(BF16)
            issue(t, slot, gdst, scatter_copies)
            return wslot

        lax.fori_loop(0, nt, tile, 0)

        for k in range(GATHER_DEPTH - 1):
            drain((nt + k) % GATHER_DEPTH, gather_copies)
        drain(last % 2, scatter_copies)

        @pl.when(nt >= 2)
        def _():
            drain(nt % 2, scatter_copies)


def _experts(gsrc, gdst, texp, tend, ntile, xy, ws, w1, w3, w2):
    smem = pl.BlockSpec(memory_space=pltpu.SMEM)
    hbm = pl.BlockSpec(memory_space=pl.ANY)
    return pl.pallas_call(
        _experts_kernel,
        in_specs=[smem, smem, smem, smem, smem, hbm, hbm, hbm, hbm, hbm],
        out_specs=hbm,
        out_shape=jax.ShapeDtypeStruct(xy.shape, xy.dtype),
        scratch_shapes=[pltpu.VMEM((GATHER_DEPTH, EXPERT_TILE, D_MODEL), BF16),
                        pltpu.VMEM((GATHER_DEPTH, EXPERT_TILE, LANES), F32),
                        pltpu.VMEM((2, EXPERT_TILE, D_MODEL), BF16),
                        pltpu.VMEM((2, D_MODEL, D_EXPERT), F32),
                        pltpu.VMEM((2, D_MODEL, D_EXPERT), F32),
                        pltpu.VMEM((2, D_EXPERT, D_MODEL), F32),
                        pltpu.VMEM((D_MODEL, D_EXPERT), BF16),
                        pltpu.VMEM((D_MODEL, D_EXPERT), BF16),
                        pltpu.VMEM((D_EXPERT, D_MODEL), BF16),
                        pltpu.SemaphoreType.DMA((GATHER_DEPTH,)),
                        pltpu.SemaphoreType.DMA((2,)),
                        pltpu.SemaphoreType.DMA((2,))],
        input_output_aliases={5: 0},
        compiler_params=pltpu.CompilerParams(vmem_limit_bytes=VMEM_LIMIT),
        name="moe_experts",
    )(gsrc, gdst, texp, tend, ntile, xy, ws, w1, w3, w2)


def _combine_kernel(y_ref, pt_ref, h1_ref, g2_ref, fg_ref, o_ref):
    moe = _dot(pt_ref[...], y_ref[...])
    h = h1_ref[...] + g2_ref[...] * moe
    ms = jnp.mean(h * h, axis=-1, keepdims=True)
    o_ref[...] = h * lax.rsqrt(ms + EPS) * fg_ref[...]


def _combine(xy, pt, h1, g2, final_g):
    t = h1.shape[0]
    row = lambda i: (i, 0)
    const = lambda i: (0, 0)
    return pl.pallas_call(
        _combine_kernel,
        grid=(t // DISPATCH_BLOCK,),
        in_specs=[pl.BlockSpec((LOCAL_CAP, D_MODEL), row),
                  pl.BlockSpec((DISPATCH_BLOCK, LOCAL_CAP), row),
                  pl.BlockSpec((DISPATCH_BLOCK, D_MODEL), row),
                  pl.BlockSpec((1, D_MODEL), const), pl.BlockSpec((1, D_MODEL), const)],
        out_specs=pl.BlockSpec((DISPATCH_BLOCK, D_MODEL), row),
        out_shape=jax.ShapeDtypeStruct((t, D_MODEL), F32),
        compiler_params=_cparams(("arbitrary",)),
        name="moe_combine_final",
    )(xy, pt, h1, g2, final_g)


def _pad_lanes(a):
    return jnp.pad(a, ((0, 0), (0, LANES - a.shape[1])))


def kernel(x, c, ctx, c_ctx, w_mod, b_mod, norm1_g, w_in, w_conv_q, w_conv_k, gate_bias, head_norm_g, w_pool, pool_scale, w_out, norm2_g, w_group, b_group, w_router, b_router, w1, w3, w2, final_g):
    assert x.shape[0] == 1 and w_mod.shape[0] == 1
    seq = x.shape[1]
    x2d = x[0]
    ctx2d = ctx[0]

    n_main = 5 * MIX_HALF
    w_in_t = jnp.transpose(w_in[0])
    wg_pack, wg_hi = _split_pack(jnp.transpose(_take_rows(w_in_t, n_main, N_GATES)))
    gate_bias_row = _pad_lanes(gate_bias[0].reshape(1, N_GATES))
    w_out_bf = _cast_bf16(w_out[0])
    wr_pack, wr_hi = _split_pack(jnp.concatenate([w_group[0], w_router[0]], axis=1))
    b_route = _pad_lanes(jnp.concatenate([b_group[0], b_router[0]]).reshape(1, -1))
    norm1 = norm1_g[0].reshape(1, D_MODEL)

    c16 = jnp.concatenate([c, c_ctx[None, :], jnp.zeros((14, D_MODEL), F32)], axis=0)
    mods = _adaln(c16, w_mod[0], b_mod[0].reshape(1, -1))
    mod_lat = mods[0].reshape(6, D_MODEL)
    mod_ctx = mods[1].reshape(6, D_MODEL)

    mod_in = jnp.concatenate([mod_ctx[0:2], mod_lat[0:2]], axis=0)
    u_pool, q, uo, k, kt, v, col, rowi = _inproj(x2d, ctx2d, mod_in, norm1, w_in_t, wg_pack, wg_hi,
                                                 gate_bias_row, w_conv_q[0], w_conv_k[0])
    nc = seq // CHUNK
    zeros_state = (jnp.zeros((2 * HEADS, HEAD_DIM, HEAD_DIM), F32),
                   jnp.zeros((2 * HEADS, 1, HEAD_DIM), F32),
                   jnp.zeros((2 * HEADS, 1, LANES), F32))
    s0, n0, m0 = _mlstm(None, k, kt, v, col, rowi, *zeros_state, need_out=False, first=nc, nc=1)

    hf, hb = _mlstm(q, k, kt, v, col, rowi, s0, n0, m0, need_out=True, first=0, nc=nc)
    p = _pool(u_pool, w_pool[0], pool_scale[0].reshape(1, -1), seq)
    h1, fn, info = _outproj(p, hf, hb, uo, x2d, w_out_bf, mod_lat, head_norm_g[0].reshape(1, -1),
                            norm2_g[0].reshape(1, -1), wr_pack, wr_hi, b_route, tm=256)

    xs, ws, pt, cnt = _dispatch(fn, info)
    gsrc, gdst, texp, tend, ntile = _plan(cnt[::8, :N_EXPERTS], seq)
    xy = _experts(gsrc, gdst, texp, tend, ntile, xs, ws, w1[0], w3[0], w2[0])
    out = _combine(xy, pt, h1, mod_lat[5:6], final_g.reshape(1, -1))
    return out.reshape(1, seq, D_MODEL)
```

```python
import functools

import jax
import jax.numpy as jnp
from jax import lax
from jax.experimental import pallas as pl
from jax.experimental.pallas import tpu as pltpu

F32 = jnp.float32
BF16 = jnp.bfloat16

D_MODEL = 2048
GRID_W = 64
GRID_SHIFT = 6
POOL_WINDOWS = (2, 4, 8, 16)
POOL_GROUP = 256
HEADS = 4
HEAD_DIM = 256
MIX_HALF = 1024
N_GATES = 16
N_GROUPS = 4
EXPERTS_PER_GROUP = 8
N_EXPERTS = 32
D_EXPERT = 512
EPS = 1e-6
LANES = 128
CHUNK = 256
ROUTE_LANE0 = N_GROUPS

VMEM_LIMIT = 60 * 1024 * 1024


def _cparams(sem, vmem=VMEM_LIMIT):
    return pltpu.CompilerParams(dimension_semantics=sem, vmem_limit_bytes=vmem)


def _split2(x):
    hi = x.astype(BF16)
    lo = (x - hi.astype(F32)).astype(BF16)
    return hi, lo


def _split3(x):
    hi = x.astype(BF16)
    r = x - hi.astype(F32)
    mid = r.astype(BF16)
    lo = (r - mid.astype(F32)).astype(BF16)
    return hi, mid, lo


def _dot(a, b):
    return jnp.dot(a, b, preferred_element_type=F32)


SPLIT_LANE = 64


def _split_pack(w):
    hi, lo = _split2(w)
    n = w.shape[1]
    gap = jnp.zeros((w.shape[0], SPLIT_LANE - n), BF16)
    rest = jnp.zeros((w.shape[0], LANES - n), BF16)
    return jnp.concatenate([hi, gap, lo, gap], axis=1), jnp.concatenate([hi, rest], axis=1)


def _split_dot(xh, xl, w_packed, w_hi):
    r = _dot(xh, w_packed)
    return r + pltpu.roll(r, SPLIT_LANE, 1) + _dot(xl, w_hi)


def _silu(x):
    return x * jax.nn.sigmoid(x)


def _log_sigmoid(x):
    return jnp.minimum(x, 0.0) - jnp.log(1.0 + jnp.exp(-jnp.abs(x)))


def _copy_kernel(w_ref, o_ref):
    o_ref[...] = w_ref[...]


def _take_rows(w, start, n):
    return pl.pallas_call(
        _copy_kernel,
        grid=(1,),
        in_specs=[pl.BlockSpec((n, w.shape[1]), lambda i: (start // n, 0))],
        out_specs=pl.BlockSpec((n, w.shape[1]), lambda i: (0, 0)),
        out_shape=jax.ShapeDtypeStruct((n, w.shape[1]), w.dtype),
        name="take_rows",
    )(w)


def _adaln_kernel(c_ref, w_ref, b_ref, o_ref):
    a = _silu(c_ref[...])
    a3 = jnp.concatenate(_split3(a), axis=0)
    w_hi, w_lo = _split2(w_ref[...])
    acc = _dot(a3, w_hi)
    acc_lo = _dot(a3[:32], w_lo)
    out = acc[0:16] + acc[16:32] + acc[32:48] + acc_lo[0:16] + acc_lo[16:32]
    o_ref[...] = out + b_ref[...]


def _adaln(c16, w_mod, b_mod):
    n = w_mod.shape[1]
    tn = 1536
    return pl.pallas_call(
        _adaln_kernel,
        grid=(n // tn,),
        in_specs=[pl.BlockSpec((16, D_MODEL), lambda j: (0, 0)),
                  pl.BlockSpec((D_MODEL, tn), lambda j: (0, j)),
                  pl.BlockSpec((1, tn), lambda j: (0, j))],
        out_specs=pl.BlockSpec((16, tn), lambda j: (0, j)),
        out_shape=jax.ShapeDtypeStruct((16, n), F32),
        compiler_params=_cparams(("arbitrary",)),
        name="adaln",
    )(c16, w_mod, b_mod)


HALO = 8
NEG_INF = float("-inf")


def _gate_scan_info(gates):
    n = gates.shape[0]
    lane = lax.broadcasted_iota(jnp.int32, gates.shape, 1)
    rows = lax.broadcasted_iota(jnp.int32, gates.shape, 0)
    lf = jnp.where(lane < N_GATES, _log_sigmoid(gates), 0.0)
    hi = lf.astype(BF16).astype(F32)
    rem = lf - hi
    mid = rem.astype(BF16).astype(F32)
    packed = (hi + pltpu.roll(mid, 32, 1) + pltpu.roll(rem - mid, 64, 1)).astype(BF16)
    r = lax.broadcasted_iota(jnp.int32, (n, n), 0)
    c = lax.broadcasted_iota(jnp.int32, (n, n), 1)
    pf = _dot(jnp.where(r >= c, 1.0, 0.0).astype(BF16), packed)
    pb = _dot(jnp.where(r <= c, 1.0, 0.0).astype(BF16), packed)
    bf = pf + pltpu.roll(pf, 96, 1) + pltpu.roll(pf, 64, 1)
    bb = pb + pltpu.roll(pb, 96, 1) + pltpu.roll(pb, 64, 1)
    b = jnp.where(lane < 8, bf, bb)
    cval = gates - pltpu.roll(b, LANES - 4, 1)
    pm = cval
    sm = cval
    step = 1
    while step < n:
        pm = jnp.maximum(pm, jnp.where(rows >= step, pltpu.roll(pm, step, 0), NEG_INF))
        sm = jnp.maximum(sm, jnp.where(rows < n - step, pltpu.roll(sm, n - step, 0), NEG_INF))
        step *= 2
    cm = jnp.where(lane < 8, pm, sm)
    return cval, jnp.where((lane & 4) == 0, cm, b)


W_CHUNK = 256


def _load_bf16(src_hbm, dst, stage, sem, *, n_chunks, transpose):
    rows = stage.shape[1]
    def chunk_copy(c, sl):
        return pltpu.make_async_copy(
            src_hbm.at[pl.ds(pl.multiple_of(c * rows, rows), rows), :], stage.at[sl], sem.at[sl])

    chunk_copy(0, 0).start()

    def convert(c, carry):
        sl = c % 2

        @pl.when(c + 1 < n_chunks)
        def _():
            chunk_copy(c + 1, 1 - sl).start()

        chunk_copy(c, sl).wait()
        span = pl.ds(pl.multiple_of(c * rows, rows), rows)
        if transpose:
            dst[:, span] = stage[sl].T.astype(BF16)
        else:
            dst[span, :] = stage[sl].astype(BF16)
        return carry

    lax.fori_loop(0, n_chunks, convert, 0)


def _inproj_kernel(x_ref, xp_ref, xn_ref, ctx_ref, mod_ref, g_ref, wt_hbm, wgp_ref, wgh_ref, gb_ref,
                   wcq_ref, wck_ref, pool_o, q_o, o_o, k_o, kt_o, v_o, col_o, row_o,
                   w_s, stage, sem, *, nt):
    i = pl.program_id(0)
    tm = x_ref.shape[0]

    @pl.when(i == 0)
    def _():
        _load_bf16(wt_hbm, w_s, stage, sem, n_chunks=w_s.shape[1] // W_CHUNK, transpose=True)

    is_ctx = i == nt
    x_main = jnp.where(is_ctx, ctx_ref[...], x_ref[...])
    x_all = jnp.concatenate([xp_ref[...], x_main, xn_ref[...]], axis=0)
    shift = jnp.where(is_ctx, mod_ref[0:1, :], mod_ref[2:3, :])
    scale = jnp.where(is_ctx, mod_ref[1:2, :], mod_ref[3:4, :])
    ms = jnp.mean(x_all * x_all, axis=-1, keepdims=True)
    y = x_all * lax.rsqrt(ms + EPS) * g_ref[...]
    xn_all = y * (1.0 + scale) + shift
    xh_all = xn_all.astype(BF16)
    xh, xl = _split2(xn_all[HALO:HALO + tm])

    def cols(ci):
        return w_s[:, ci * MIX_HALF:(ci + 1) * MIX_HALF]

    u_q = _dot(xh_all, cols(1))
    u_k = _dot(xh_all, cols(3))
    pool_o[...] = _dot(xh, cols(0))
    o_o[...] = _dot(xh, cols(2))
    v_o[...] = _dot(xh, cols(4)).astype(BF16)

    rowi = lax.broadcasted_iota(jnp.int32, (tm, MIX_HALF), 0)
    at_start = jnp.logical_and(rowi == 0, jnp.logical_or(i == 0, is_ctx))
    at_end = jnp.logical_and(rowi == tm - 1, jnp.logical_or(i == nt - 1, is_ctx))

    def conv_silu(u, wc_ref):
        n = u.shape[0]
        up = jnp.where(at_start, 0.0, pltpu.roll(u, 1, 0)[HALO:HALO + tm])
        un = jnp.where(at_end, 0.0, pltpu.roll(u, n - 1, 0)[HALO:HALO + tm])
        return _silu(wc_ref[0:1, :] * up + wc_ref[1:2, :] * u[HALO:HALO + tm] + wc_ref[2:3, :] * un)

    q_o[...] = (conv_silu(u_q, wcq_ref) * (HEAD_DIM ** -0.5)).astype(BF16)
    k = conv_silu(u_k, wck_ref)
    k_o[...] = k.astype(BF16)
    kt_o[...] = k.astype(BF16).T
    gates = _split_dot(xh, xl, wgp_ref[...], wgh_ref[...]) + gb_ref[...]
    cval, col = _gate_scan_info(gates)
    col_o[...] = col[:, :N_GATES]
    row_o[...] = cval.T[:N_GATES, :]


def _inproj(x2d, ctx2d, mod, g, w_in_t, wg_pack, wg_hi, gate_bias_row, wcq, wck):
    t = x2d.shape[0]
    tm = CHUNK
    assert ctx2d.shape[0] == tm
    nt = t // tm
    r8 = tm // HALO
    last8 = t // HALO - 1
    n_main = 5 * MIX_HALF
    rows = t + tm
    const = lambda i: (0, 0)
    row = lambda i: (i, 0)
    f32_out = jax.ShapeDtypeStruct((rows, MIX_HALF), F32)
    bf16_out = jax.ShapeDtypeStruct((rows, MIX_HALF), BF16)
    seq = pl.BlockSpec((tm, MIX_HALF), row)
    return pl.pallas_call(
        functools.partial(_inproj_kernel, nt=nt),
        grid=(nt + 1,),
        in_specs=[pl.BlockSpec((tm, D_MODEL), lambda i: (jnp.minimum(i, nt - 1), 0)),
                  pl.BlockSpec((HALO, D_MODEL), lambda i: (jnp.clip(i * r8 - 1, 0, last8), 0)),
                  pl.BlockSpec((HALO, D_MODEL), lambda i: (jnp.minimum((i + 1) * r8, last8), 0)),
                  pl.BlockSpec((tm, D_MODEL), const),
                  pl.BlockSpec((4, D_MODEL), const),
                  pl.BlockSpec((1, D_MODEL), const),
                  pl.BlockSpec(memory_space=pl.ANY),
                  pl.BlockSpec((D_MODEL, LANES), const),
                  pl.BlockSpec((D_MODEL, LANES), const),
                  pl.BlockSpec((1, LANES), const),
                  pl.BlockSpec((3, MIX_HALF), const),
                  pl.BlockSpec((3, MIX_HALF), const)],
        out_specs=[seq, seq, seq, seq, pl.BlockSpec((MIX_HALF, tm), lambda i: (0, i)), seq,
                   pl.BlockSpec((tm, N_GATES), row), pl.BlockSpec((N_GATES, tm), lambda i: (0, i))],
        out_shape=[f32_out, bf16_out, f32_out, bf16_out,
                   jax.ShapeDtypeStruct((MIX_HALF, rows), BF16), bf16_out,
                   jax.ShapeDtypeStruct((rows, N_GATES), F32),
                   jax.ShapeDtypeStruct((N_GATES, rows), F32)],
        scratch_shapes=[pltpu.VMEM((D_MODEL, n_main), BF16),
                        pltpu.VMEM((2, W_CHUNK, D_MODEL), F32),
                        pltpu.SemaphoreType.DMA((2,))],
        compiler_params=_cparams(("arbitrary",)),
        name="inproj",
    )(x2d, x2d, x2d, ctx2d, mod, g, w_in_t, wg_pack, wg_hi, gate_bias_row, wcq, wck)


def _mlstm_kernel(*refs, need_out):
    if need_out:
        (qf, kf, ktf, vf, colf, rowf, qb, kb, ktb, vb, colb, rowb, s0, n0, m0,
         hf_o, hb_o, s_s, n_s, m_s) = refs
        q_refs, h_outs = (qf, qb), (hf_o, hb_o)
    else:
        (kf, ktf, vf, colf, rowf, kb, ktb, vb, colb, rowb, s0, n0, m0,
         s_o, n_o, m_o, s_s, n_s, m_s) = refs
    k_refs, kt_refs, v_refs = (kf, kb), (ktf, ktb), (vf, vb)
    col_refs, row_refs = (colf, colb), (rowf, rowb)
    j = pl.program_id(0)

    @pl.when(j == 0)
    def _():
        s_s[...] = s0[...]
        n_s[...] = n0[...]
        m_s[...] = m0[...]

    L = CHUNK
    row = lax.broadcasted_iota(jnp.int32, (L, L), 0)
    col = lax.broadcasted_iota(jnp.int32, (L, L), 1)

    heads = [(d, h) for d in range(2) for h in range(HEADS)]

    def head_values(d, h):
        hd = d * HEADS + h
        lc, lb = 8 * d + h, 8 * d + 4 + h
        hs = slice(h * HEAD_DIM, (h + 1) * HEAD_DIM)
        edge = L - 1 if d == 0 else 0
        colv = col_refs[d][...]
        rowv = row_refs[d][...]
        v = dict(hd=hd, hs=hs, d=d)
        v["cm_c"], v["b_c"] = colv[:, lc:lc + 1], colv[:, lb:lb + 1]
        v["c_r"] = rowv[lc:lc + 1, :]
        v["g"] = v["b_c"][edge:edge + 1, :]
        v["m_old"] = m_s[hd][:, 0:1]
        v["s_old"] = s_s[hd]
        v["n_old"] = n_s[hd]
        v["m_x"] = jnp.maximum(v["m_old"], v["cm_c"][edge:edge + 1, :])
        v["decay"] = jnp.exp(v["m_old"] - v["m_x"])
        v["wk_r"] = jnp.exp(v["c_r"] - v["m_x"])
        v["v_h"] = v_refs[d][:, hs]
        v["kt_h"] = kt_refs[d][hs, :]
        return v

    def update_state(v):
        hd = v["hd"]
        n_s[hd] = v["decay"] * v["n_old"] + _dot(
            jnp.broadcast_to(v["wk_r"], (8, L)).astype(BF16), k_refs[v["d"]][:, v["hs"]])[0:1, :]
        kwt = (v["kt_h"].astype(F32) * v["wk_r"]).astype(BF16)
        s_s[hd] = v["decay"] * v["s_old"] + _dot(kwt, v["v_h"])
        m_s[hd] = jnp.broadcast_to(v["g"] + v["m_x"], (1, LANES))

    def weights(v):
        d, hs = v["d"], v["hs"]
        mask = (row >= col) if d == 0 else (row <= col)
        q_h = q_refs[d][:, hs]
        m_c = jnp.maximum(v["m_old"], v["cm_c"])
        w_inter = jnp.exp(v["m_old"] - m_c)
        v["e"] = jnp.where(mask, jnp.exp(v["c_r"] - m_c), 0.0)
        v["qw"] = (q_h.astype(F32) * w_inter).astype(BF16)
        qn = lax.dot_general(q_h, jnp.broadcast_to(v["n_old"], (8, HEAD_DIM)).astype(BF16),
                             (((1,), (1,)), ((), ())), preferred_element_type=F32)
        v["den_inter"] = w_inter * qn[:, 0:1]
        v["floor"] = jnp.exp(-(v["b_c"] + m_c))
        v["s_bf"] = v["s_old"].astype(BF16)

    def outputs(v, qk):
        p = v["e"] * qk
        num = _dot(v["qw"], v["s_bf"]) + _dot(p.astype(BF16), v["v_h"])
        den = v["den_inter"] + jnp.sum(p[:, :LANES] + p[:, LANES:], axis=1, keepdims=True)
        h_outs[v["d"]][:, v["hs"]] = num / jnp.maximum(jnp.abs(den), v["floor"])

    vals = [head_values(d, h) for d, h in heads]
    if need_out:
        qks = [_dot(q_refs[v["d"]][:, v["hs"]], v["kt_h"]) for v in vals]
        for v in vals:
            weights(v)
    for v in vals:
        update_state(v)
    if need_out:
        for v, qk in zip(vals, qks):
            outputs(v, qk)

    if not need_out:
        s_o[...] = s_s[...]
        n_o[...] = n_s[...]
        m_o[...] = m_s[...]


def _mlstm(q, k, kt, v, col, rowi, s0, n0, m0, need_out, first, nc):
    t = nc * CHUNK
    fwd = lambda j: (first + j, 0)
    bwd = lambda j: (first + nc - 1 - j, 0)
    fwd_t = lambda j: (0, first + j)
    bwd_t = lambda j: (0, first + nc - 1 - j)
    c3 = lambda j: (0, 0, 0)
    seq = lambda im: pl.BlockSpec((CHUNK, MIX_HALF), im)
    state_specs = [pl.BlockSpec(s0.shape, c3), pl.BlockSpec(n0.shape, c3), pl.BlockSpec(m0.shape, c3)]
    scratch = [pltpu.VMEM(s0.shape, F32), pltpu.VMEM(n0.shape, F32), pltpu.VMEM(m0.shape, F32)]

    def side(im, im_t):
        specs = ([seq(im)] if need_out else []) + [seq(im), pl.BlockSpec((MIX_HALF, CHUNK), im_t), seq(im)]
        return specs + [pl.BlockSpec((CHUNK, N_GATES), im), pl.BlockSpec((N_GATES, CHUNK), im_t)]

    in_specs = side(fwd, fwd_t) + side(bwd, bwd_t) + state_specs
    seq_in = ((q,) if need_out else ()) + (k, kt, v, col, rowi)
    args = seq_in + seq_in + (s0, n0, m0)
    if need_out:
        out_specs = [seq(fwd), seq(bwd)]
        out_shape = [jax.ShapeDtypeStruct((t, MIX_HALF), F32)] * 2
    else:
        out_specs = state_specs
        out_shape = [jax.ShapeDtypeStruct(a.shape, F32) for a in (s0, n0, m0)]
    return pl.pallas_call(
        functools.partial(_mlstm_kernel, need_out=need_out),
        grid=(nc,),
        in_specs=in_specs,
        out_specs=out_specs,
        out_shape=out_shape,
        scratch_shapes=scratch,
        compiler_params=_cparams(("arbitrary",)),
        name="mlstm_out" if need_out else "mlstm_state",
    )(*args)


POOL_PAD = 512
POOL_UNROLL = 4


def _pool_kernel(u_ref, w_ref, sc_ref, o_ref, pad_s, *, t):
    for gi, win in enumerate(POOL_WINDOWS):
        @pl.when(pl.program_id(0) == gi)
        def _():
            _pool_group(u_ref, w_ref, sc_ref, o_ref, pad_s, win=win, t=t)


def _pool_group(u_ref, w_ref, sc_ref, o_ref, pad_s, *, win, t):
    half = win // 2
    tile = 256
    zeros = jnp.zeros((POOL_PAD, POOL_GROUP), F32)
    pad_s[0:POOL_PAD, :] = zeros
    pad_s[POOL_PAD + t:POOL_PAD + t + POOL_PAD, :] = zeros

    def copy(r, carry):
        t0 = pl.multiple_of(r * tile, tile)
        pad_s[pl.ds(POOL_PAD + t0, tile), :] = u_ref[pl.ds(t0, tile), :]
        return carry

    lax.fori_loop(0, t // tile, copy, 0)

    row = lax.broadcasted_iota(jnp.int32, (tile, tile), 0)
    col = lax.broadcasted_iota(jnp.int32, (tile, tile), 1)
    same_row = (row >> GRID_SHIFT) == (col >> GRID_SHIFT)
    in_win = (col - row >= -half) & (col - row < half)
    band = jnp.where(same_row & in_win, 1.0, 0.0).astype(BF16)
    w = w_ref[...].astype(BF16)
    scale = sc_ref[...]
    n_rows = t // GRID_W

    tok0 = lax.broadcasted_iota(jnp.int32, (tile, POOL_GROUP), 0)
    gc = tok0 & (GRID_W - 1)
    inv_h = 1.0 / (jnp.minimum(gc + half, GRID_W) - jnp.maximum(gc - half, 0)).astype(F32)

    def body(r, carry):
        t0s = [pl.multiple_of((r * POOL_UNROLL + k) * tile, tile) for k in range(POOL_UNROLL)]
        pieces = []
        for t0 in t0s:
            acc = pad_s[pl.ds(POOL_PAD + t0 - GRID_W * half, tile), :]
            for dd in range(-half + 1, half):
                acc = acc + pad_s[pl.ds(POOL_PAD + t0 + GRID_W * dd, tile), :]
            gr = (t0 + tok0) >> GRID_SHIFT
            cnt_v = jnp.minimum(gr + half, n_rows) - jnp.maximum(gr - half, 0)
            pieces.append(_split2(acc / cnt_v.astype(F32)))
        means = [(_dot(band, hi) + _dot(band, lo)) * inv_h for hi, lo in pieces]
        diffs = [(m - pad_s[pl.ds(POOL_PAD + t0, tile), :]).astype(BF16) for m, t0 in zip(means, t0s)]
        for d, t0 in zip(diffs, t0s):
            o_ref[pl.ds(t0, tile), :] = (_dot(d, w) * scale).astype(BF16)
        return carry

    lax.fori_loop(0, t // (tile * POOL_UNROLL), body, 0)


def _pool(u_pool, w_pool, scale_row, t):
    return pl.pallas_call(
        functools.partial(_pool_kernel, t=t),
        grid=(len(POOL_WINDOWS),),
        in_specs=[pl.BlockSpec((t, POOL_GROUP), lambda g: (0, g)),
                  pl.BlockSpec((None, POOL_GROUP, POOL_GROUP), lambda g: (g, 0, 0)),
                  pl.BlockSpec((1, POOL_GROUP), lambda g: (0, g))],
        out_specs=pl.BlockSpec((t, POOL_GROUP), lambda g: (0, g)),
        out_shape=jax.ShapeDtypeStruct((t, MIX_HALF), BF16),
        scratch_shapes=[pltpu.VMEM((t + 2 * POOL_PAD, POOL_GROUP), F32)],
        compiler_params=_cparams(("arbitrary",)),
        name="pool_mix",
    )(u_pool, w_pool, scale_row)


OUTPROJ_SUB = 256


def _route(logits):
    lane = lax.broadcasted_iota(jnp.int32, logits.shape, 1).astype(F32)
    neg = -jnp.inf
    big = float(LANES)
    gl = jnp.where(lane < N_GROUPS, logits, neg)
    gmax = jnp.max(gl, axis=1, keepdims=True)
    gsel = jnp.min(jnp.where(gl == gmax, lane, big), axis=1, keepdims=True)
    p_grp = 1.0 / jnp.sum(jnp.exp(gl - gmax), axis=1, keepdims=True)
    lo = ROUTE_LANE0 + EXPERTS_PER_GROUP * gsel
    el = jnp.where((lane >= lo) & (lane < lo + EXPERTS_PER_GROUP), logits, neg)
    m1 = jnp.max(el, axis=1, keepdims=True)
    i1 = jnp.min(jnp.where(el == m1, lane, big), axis=1, keepdims=True)
    el2 = jnp.where(lane == i1, neg, el)
    m2 = jnp.max(el2, axis=1, keepdims=True)
    i2 = jnp.min(jnp.where(el2 == m2, lane, big), axis=1, keepdims=True)
    e2 = jnp.exp(m2 - m1)
    p1 = 1.0 / (1.0 + e2)
    p2 = e2 / (1.0 + e2)
    info = jnp.where(lane == 0.0, i1 - ROUTE_LANE0, 0.0)
    info = jnp.where(lane == 1.0, i2 - ROUTE_LANE0, info)
    info = jnp.where(lane == 2.0, p_grp * p1, info)
    return jnp.where(lane == 3.0, p_grp * p2, info)


def _outproj_kernel(p_ref, hf_ref, hb_ref, uo_ref, x_ref, wout_hbm, mod_ref,
                    hg_ref, n2g_ref, wrp_ref, wrh_ref, br_ref, h1_o, fn_o, info_o, w_s, stage, sem):
    @pl.when(pl.program_id(0) == 0)
    def _():
        _load_bf16(wout_hbm, w_s, stage, sem, n_chunks=w_s.shape[0] // stage.shape[1], transpose=False)

    def mixer_input(rs):
        h = hf_ref[rs, :] + hb_ref[rs, :]
        parts = []
        for hh in range(HEADS):
            hs = h[:, hh * HEAD_DIM:(hh + 1) * HEAD_DIM]
            mu = jnp.mean(hs, axis=-1, keepdims=True)
            ctr = hs - mu
            var = jnp.mean(ctr * ctr, axis=-1, keepdims=True)
            parts.append(ctr * lax.rsqrt(var + EPS))
        hn = jnp.concatenate(parts, axis=1) * hg_ref[...]
        m = (hn * jax.nn.sigmoid(uo_ref[rs, :])).astype(BF16)
        return jnp.concatenate([p_ref[rs, :], m], axis=1)

    def finish(rs, mix):
        h1 = x_ref[rs, :] + mod_ref[2:3, :] * mix
        h1_o[rs, :] = h1
        ms = jnp.mean(h1 * h1, axis=-1, keepdims=True)
        fn = h1 * lax.rsqrt(ms + EPS) * n2g_ref[...]
        fn = fn * (1.0 + mod_ref[4:5, :]) + mod_ref[3:4, :]
        fh, fl = _split2(fn)
        fn_o[rs, :] = fh
        logits = _split_dot(fh, fl, wrp_ref[...], wrh_ref[...]) + br_ref[...]
        info_o[rs, :] = _route(logits)

    tm = x_ref.shape[0]
    subs = [slice(r0, r0 + OUTPROJ_SUB) for r0 in range(0, tm, OUTPROJ_SUB)]
    w = w_s[...]
    mixes = [_dot(mixer_input(rs), w) for rs in subs]
    for rs, mix in zip(subs, mixes):
        finish(rs, mix)


def _outproj(p, hf, hb, uo, x2d, w_out, mod, head_g, norm2_g, wr_pack, wr_hi, b_route, tm):
    t = x2d.shape[0]
    const = lambda i: (0, 0)
    row = lambda i: (i, 0)
    in_specs = ([pl.BlockSpec((tm, MIX_HALF), row)] * 4
                + [pl.BlockSpec((tm, D_MODEL), row),
                   pl.BlockSpec(memory_space=pl.ANY),
                   pl.BlockSpec(mod.shape, const),
                   pl.BlockSpec((1, MIX_HALF), const),
                   pl.BlockSpec((1, D_MODEL), const),
                   pl.BlockSpec((D_MODEL, LANES), const),
                   pl.BlockSpec((D_MODEL, LANES), const),
                   pl.BlockSpec((1, LANES), const)])
    return pl.pallas_call(
        _outproj_kernel,
        grid=(t // tm,),
        in_specs=in_specs,
        out_specs=[pl.BlockSpec((tm, D_MODEL), row), pl.BlockSpec((tm, D_MODEL), row),
                   pl.BlockSpec((tm, LANES), row)],
        out_shape=[jax.ShapeDtypeStruct((t, D_MODEL), F32),
                   jax.ShapeDtypeStruct((t, D_MODEL), BF16),
                   jax.ShapeDtypeStruct((t, LANES), F32)],
        scratch_shapes=[pltpu.VMEM(w_out.shape, BF16),
                        pltpu.VMEM((2, W_CHUNK // 2, w_out.shape[1]), F32),
                        pltpu.SemaphoreType.DMA((2,))],
        compiler_params=_cparams(("arbitrary",)),
        name="outproj_route",
    )(p, hf, hb, uo, x2d, w_out, mod, head_g, norm2_g, wr_pack, wr_hi, b_route)


DISPATCH_BLOCK = 512
GRANULE = 16
GRANULE_SHIFT = 4
LOCAL_CAP = 1536
LOCAL_GRANULES = LOCAL_CAP // GRANULE
FREE_GRANULES = 2
EXPERT_TILE = 256
TILE_GRANULES = EXPERT_TILE // GRANULE
PLAN_UNROLL = 4
PLAN_SLACK = 8
GATHER_DEPTH = 3


def _dispatch_kernel(fn_ref, info_ref, xs_o, ws_o, pt_o, cnt_o):
    tb = DISPATCH_BLOCK
    info = info_ref[...]
    e1, e2 = info[:, 0:1], info[:, 1:2]
    w1c, w2c = info[:, 2:3], info[:, 3:4]
    lane = lax.broadcasted_iota(jnp.int32, (tb, LANES), 1).astype(F32)
    o1 = jnp.where(lane == e1, 1.0, 0.0)
    o2 = jnp.where(lane == e2, 1.0, 0.0)
    onehot = o1 + o2
    cnt = jnp.sum(onehot, axis=0, keepdims=True)
    gran = jnp.floor((cnt + (GRANULE - 1)) * (1.0 / GRANULE))
    a = lax.broadcasted_iota(jnp.int32, (LANES, LANES), 0)
    b = lax.broadcasted_iota(jnp.int32, (LANES, LANES), 1)
    upper = jnp.where(a < b, 1.0, 0.0).astype(BF16)
    seg_off = _dot(jnp.broadcast_to(gran, (8, LANES)).astype(BF16), upper)[0:1, :] * GRANULE
    r = lax.broadcasted_iota(jnp.int32, (tb, tb), 0)
    c = lax.broadcasted_iota(jnp.int32, (tb, tb), 1)
    strict = jnp.where(r > c, 1.0, 0.0).astype(BF16)
    rank = _dot(strict, onehot.astype(BF16))
    slot = rank + seg_off
    pos1 = jnp.sum(o1 * slot, axis=1, keepdims=True)
    pos2 = jnp.sum(o2 * slot, axis=1, keepdims=True)
    rows = lax.broadcasted_iota(jnp.int32, (tb, LOCAL_CAP), 1).astype(F32)
    pt1 = jnp.where(rows == pos1, 1.0, 0.0)
    pt2 = jnp.where(rows == pos2, 1.0, 0.0)
    pt = pt1 + pt2
    pt_o[...] = pt.astype(BF16)
    perm = pt.T.astype(BF16)
    w_slot = jnp.sum((pt1 * w1c + pt2 * w2c).T, axis=1, keepdims=True)
    for c0 in range(0, D_MODEL, 512):
        xs_o[:, c0:c0 + 512] = _dot(perm, fn_ref[:, c0:c0 + 512]).astype(BF16)
    ws_o[...] = jnp.broadcast_to(w_slot, (LOCAL_CAP, LANES))
    cnt_o[...] = jnp.broadcast_to(cnt, (8, LANES)).astype(jnp.int32)


def _dispatch(fn, info):
    t = fn.shape[0]
    nb = t // DISPATCH_BLOCK
    row = lambda i: (i, 0)
    return pl.pallas_call(
        _dispatch_kernel,
        grid=(nb,),
        in_specs=[pl.BlockSpec((DISPATCH_BLOCK, D_MODEL), row),
                  pl.BlockSpec((DISPATCH_BLOCK, LANES), row)],
        out_specs=[pl.BlockSpec((LOCAL_CAP, D_MODEL), row),
                   pl.BlockSpec((LOCAL_CAP, LANES), row),
                   pl.BlockSpec((DISPATCH_BLOCK, LOCAL_CAP), row),
                   pl.BlockSpec((8, LANES), row)],
        out_shape=[jax.ShapeDtypeStruct((nb * LOCAL_CAP, D_MODEL), BF16),
                   jax.ShapeDtypeStruct((nb * LOCAL_CAP, LANES), F32),
                   jax.ShapeDtypeStruct((t, LOCAL_CAP), BF16),
                   jax.ShapeDtypeStruct((nb * 8, LANES), jnp.int32)],
        compiler_params=_cparams(("arbitrary",)),
        name="moe_dispatch",
    )(fn, info)


def _map_len(max_tiles):
    return (max_tiles + GATHER_DEPTH - 1) * TILE_GRANULES + PLAN_SLACK


def _free_granule(q):
    return ((q // FREE_GRANULES) * LOCAL_GRANULES + (LOCAL_GRANULES - FREE_GRANULES)
            + (q % FREE_GRANULES))


def _plan_kernel(cnt_ref, gsrc_o, gdst_o, texp_o, tend_o, ntile_o, lrun, *, nb, max_tiles):
    def init(b, c):
        lrun[b] = 0
        return c

    lax.fori_loop(0, nb, init, 0)

    def per_expert(e, carry):
        g0, last_e = carry

        def per_block(b, g):
            k = (cnt_ref[b, e] + (GRANULE - 1)) >> GRANULE_SHIFT
            lo = lrun[b]
            lrun[b] = lo + k

            base = b * LOCAL_GRANULES + lo

            for j in range(PLAN_UNROLL):
                gsrc_o[g + j] = base + j
                gdst_o[g + j] = base + j

            @pl.when(k > PLAN_UNROLL)
            def _():
                def put(j, c):
                    gsrc_o[g + j] = base + j
                    gdst_o[g + j] = base + j
                    return c

                lax.fori_loop(PLAN_UNROLL, k, put, 0)

            return g + k

        g1 = lax.fori_loop(0, nb, per_block, g0)
        pad = (-g1) & (TILE_GRANULES - 1)

        def put_pad(j, c):
            g = g1 + j
            parity = (g // TILE_GRANULES) & 1
            gsrc_o[g] = _free_granule(0)
            gdst_o[g] = _free_granule(1 + parity * (TILE_GRANULES - 1) + (g & (TILE_GRANULES - 1)))
            return c

        lax.fori_loop(0, pad, put_pad, 0)
        g2 = g1 + pad

        def put_tile(tt, c):
            texp_o[tt] = e
            return c

        lax.fori_loop(g0 // TILE_GRANULES, g2 // TILE_GRANULES, put_tile, 0)
        tend_o[e] = g2 // TILE_GRANULES
        return g2, jnp.where(g2 > g0, e, last_e)

    g_end, last_e = lax.fori_loop(0, N_EXPERTS, per_expert, (0, 0))
    n_tiles = g_end // TILE_GRANULES
    ntile_o[0] = n_tiles

    def fill(tt, c):
        texp_o[tt] = last_e
        return c

    lax.fori_loop(n_tiles, max_tiles, fill, 0)

    def fill_map(g, c):
        gsrc_o[g] = _free_granule(0)
        gdst_o[g] = _free_granule(0)
        return c

    lax.fori_loop(g_end, _map_len(max_tiles), fill_map, 0)


def _max_tiles(t):
    nb = t // DISPATCH_BLOCK
    worst_rows = 2 * t + nb * N_EXPERTS * (GRANULE - 1) + N_EXPERTS * (EXPERT_TILE - GRANULE)
    return -(-worst_rows // EXPERT_TILE)


def _plan(cnt, t):
    nb = cnt.shape[0]
    assert nb * FREE_GRANULES >= 2 + 2 * (TILE_GRANULES - 1)
    max_tiles = _max_tiles(t)
    smem = pl.BlockSpec(memory_space=pltpu.SMEM)
    n_map = _map_len(max_tiles)
    return pl.pallas_call(
        functools.partial(_plan_kernel, nb=nb, max_tiles=max_tiles),
        in_specs=[smem],
        out_specs=[smem, smem, smem, smem, smem],
        out_shape=[jax.ShapeDtypeStruct((n_map,), jnp.int32),
                   jax.ShapeDtypeStruct((n_map,), jnp.int32),
                   jax.ShapeDtypeStruct((max_tiles,), jnp.int32),
                   jax.ShapeDtypeStruct((N_EXPERTS,), jnp.int32),
                   jax.ShapeDtypeStruct((1,), jnp.int32)],
        scratch_shapes=[pltpu.SMEM((nb,), jnp.int32)],
        name="moe_plan",
    )(cnt)


def _experts_kernel(gsrc, gdst, texp, tend, ntile, xy_in, ws_in, w1_hbm, w3_hbm, w2_hbm, xy_out,
                    xbuf, wsbuf, ybuf, st1, st3, st2, wb1, wb3, wb2, gsem, ssem, wsem):
    nt = ntile[0]

    def weight_copies(e, ws):
        return (pltpu.make_async_copy(w1_hbm.at[e], st1.at[ws], wsem.at[ws]),
                pltpu.make_async_copy(w3_hbm.at[e], st3.at[ws], wsem.at[ws]),
                pltpu.make_async_copy(w2_hbm.at[e], st2.at[ws], wsem.at[ws]))

    def rows(i):
        return pl.ds(pl.multiple_of(i * GRANULE, GRANULE), GRANULE)

    def gather_copies(g, j, sl):
        return (pltpu.make_async_copy(xy_in.at[rows(g), :], xbuf.at[sl, rows(j), :], gsem.at[sl]),
                pltpu.make_async_copy(ws_in.at[rows(g), :], wsbuf.at[sl, rows(j), :], gsem.at[sl]))

    def scatter_copies(g, j, sl):
        return (pltpu.make_async_copy(ybuf.at[sl, rows(j), :], xy_out.at[rows(g), :], ssem.at[sl]),)

    def issue(tt, sl, gmap, copies):
        for j in range(TILE_GRANULES):
            for cp in copies(gmap[tt * TILE_GRANULES + j], j, sl):
                cp.start()

    def drain(sl, copies):
        full = pl.ds(0, EXPERT_TILE)
        if copies is gather_copies:
            pltpu.make_async_copy(xy_in.at[full, :], xbuf.at[sl], gsem.at[sl]).wait()
            pltpu.make_async_copy(ws_in.at[full, :], wsbuf.at[sl], gsem.at[sl]).wait()
        else:
            pltpu.make_async_copy(ybuf.at[sl], xy_out.at[full, :], ssem.at[sl]).wait()

    @pl.when(nt > 0)
    def _():
        last = nt - 1
        for cp in weight_copies(texp[0], 0):
            cp.start()
        for ahead in range(GATHER_DEPTH - 1):
            issue(ahead, ahead, gsrc, gather_copies)

        def tile(t, wslot):
            slot = t % 2
            gslot = t % GATHER_DEPTH
            e = texp[t]
            first = (t == 0) | (texp[jnp.maximum(t - 1, 0)] != e)
            wslot = jnp.where(first & (t > 0), 1 - wslot, wslot)

            drain(gslot, gather_copies)
            ahead = t + GATHER_DEPTH - 1
            issue(ahead, ahead % GATHER_DEPTH, gsrc, gather_copies)

            @pl.when(first)
            def _():
                for cp in weight_copies(e, wslot):
                    cp.wait()
                wb1[...] = st1[wslot].astype(BF16)
                wb3[...] = st3[wslot].astype(BF16)
                wb2[...] = st2[wslot].astype(BF16)
                nxt = tend[e]

                @pl.when(nxt < nt)
                def _():
                    for cp in weight_copies(texp[jnp.minimum(nxt, last)], 1 - wslot):
                        cp.start()

            x = xbuf[gslot]
            w_row = wsbuf[gslot][:, 0:1]
            a = _dot(x, wb1[...])
            b = _dot(x, wb3[...])
            y = _dot((_silu(a) * b).astype(BF16), wb2[...]) * w_row

            @pl.when(t >= 2)
            def _():
                drain(slot, scatter_copies)

            ybuf[slot] = y.astype(BF16)
            issue(t, slot, gdst, scatter_copies)
            return wslot

        lax.fori_loop(0, nt, tile, 0)

        for k in range(GATHER_DEPTH - 1):
            drain((nt + k) % GATHER_DEPTH, gather_copies)
        drain(last % 2, scatter_copies)

        @pl.when(nt >= 2)
        def _():
            drain(nt % 2, scatter_copies)


def _experts(gsrc, gdst, texp, tend, ntile, xy, ws, w1, w3, w2):
    smem = pl.BlockSpec(memory_space=pltpu.SMEM)
    hbm = pl.BlockSpec(memory_space=pl.ANY)
    return pl.pallas_call(
        _experts_kernel,
        in_specs=[smem, smem, smem, smem, smem, hbm, hbm, hbm, hbm, hbm],
        out_specs=hbm,
        out_shape=jax.ShapeDtypeStruct(xy.shape, xy.dtype),
        scratch_shapes=[pltpu.VMEM((GATHER_DEPTH, EXPERT_TILE, D_MODEL), BF16),
                        pltpu.VMEM((GATHER_DEPTH, EXPERT_TILE, LANES), F32),
                        pltpu.VMEM((2, EXPERT_TILE, D_MODEL), BF16),
                        pltpu.VMEM((2, D_MODEL, D_EXPERT), F32),
                        pltpu.VMEM((2, D_MODEL, D_EXPERT), F32),
                        pltpu.VMEM((2, D_EXPERT, D_MODEL), F32),
                        pltpu.VMEM((D_MODEL, D_EXPERT), BF16),
                        pltpu.VMEM((D_MODEL, D_EXPERT), BF16),
                        pltpu.VMEM((D_EXPERT, D_MODEL), BF16),
                        pltpu.SemaphoreType.DMA((GATHER_DEPTH,)),
                        pltpu.SemaphoreType.DMA((2,)),
                        pltpu.SemaphoreType.DMA((2,))],
        input_output_aliases={5: 0},
        compiler_params=pltpu.CompilerParams(vmem_limit_bytes=VMEM_LIMIT),
        name="moe_experts",
    )(gsrc, gdst, texp, tend, ntile, xy, ws, w1, w3, w2)


def _combine_kernel(y_ref, pt_ref, h1_ref, g2_ref, fg_ref, o_ref):
    moe = _dot(pt_ref[...], y_ref[...])
    h = h1_ref[...] + g2_ref[...] * moe
    ms = jnp.mean(h * h, axis=-1, keepdims=True)
    o_ref[...] = h * lax.rsqrt(ms + EPS) * fg_ref[...]


def _combine(xy, pt, h1, g2, final_g):
    t = h1.shape[0]
    row = lambda i: (i, 0)
    const = lambda i: (0, 0)
    return pl.pallas_call(
        _combine_kernel,
        grid=(t // DISPATCH_BLOCK,),
        in_specs=[pl.BlockSpec((LOCAL_CAP, D_MODEL), row),
                  pl.BlockSpec((DISPATCH_BLOCK, LOCAL_CAP), row),
                  pl.BlockSpec((DISPATCH_BLOCK, D_MODEL), row),
                  pl.BlockSpec((1, D_MODEL), const), pl.BlockSpec((1, D_MODEL), const)],
        out_specs=pl.BlockSpec((DISPATCH_BLOCK, D_MODEL), row),
        out_shape=jax.ShapeDtypeStruct((t, D_MODEL), F32),
        compiler_params=_cparams(("arbitrary",)),
        name="moe_combine_final",
    )(xy, pt, h1, g2, final_g)


def _pad_lanes(a):
    return jnp.pad(a, ((0, 0), (0, LANES - a.shape[1])))


def kernel(x, c, ctx, c_ctx, w_mod, b_mod, norm1_g, w_in, w_conv_q, w_conv_k, gate_bias, head_norm_g, w_pool, pool_scale, w_out, norm2_g, w_group, b_group, w_router, b_router, w1, w3, w2, final_g):
    assert x.shape[0] == 1 and w_mod.shape[0] == 1
    seq = x.shape[1]
    x2d = x[0]
    ctx2d = ctx[0]

    n_main = 5 * MIX_HALF
    w_in_t = jnp.transpose(w_in[0])
    wg_pack, wg_hi = _split_pack(jnp.transpose(_take_rows(w_in_t, n_main, N_GATES)))
    gate_bias_row = _pad_lanes(gate_bias[0].reshape(1, N_GATES))
    wr_pack, wr_hi = _split_pack(jnp.concatenate([w_group[0], w_router[0]], axis=1))
    b_route = _pad_lanes(jnp.concatenate([b_group[0], b_router[0]]).reshape(1, -1))
    norm1 = norm1_g[0].reshape(1, D_MODEL)

    c16 = jnp.concatenate([c, c_ctx[None, :], jnp.zeros((14, D_MODEL), F32)], axis=0)
    mods = _adaln(c16, w_mod[0], b_mod[0].reshape(1, -1))
    mod_lat = mods[0].reshape(6, D_MODEL)
    mod_ctx = mods[1].reshape(6, D_MODEL)

    mod_in = jnp.concatenate([mod_ctx[0:2], mod_lat[0:2]], axis=0)
    u_pool, q, uo, k, kt, v, col, rowi = _inproj(x2d, ctx2d, mod_in, norm1, w_in_t, wg_pack, wg_hi,
                                                 gate_bias_row, w_conv_q[0], w_conv_k[0])
    nc = seq // CHUNK
    zeros_state = (jnp.zeros((2 * HEADS, HEAD_DIM, HEAD_DIM), F32),
                   jnp.zeros((2 * HEADS, 1, HEAD_DIM), F32),
                   jnp.zeros((2 * HEADS, 1, LANES), F32))
    s0, n0, m0 = _mlstm(None, k, kt, v, col, rowi, *zeros_state, need_out=False, first=nc, nc=1)

    hf, hb = _mlstm(q, k, kt, v, col, rowi, s0, n0, m0, need_out=True, first=0, nc=nc)
    p = _pool(u_pool, w_pool[0], pool_scale[0].reshape(1, -1), seq)
    h1, fn, info = _outproj(p, hf, hb, uo, x2d, w_out[0], mod_lat, head_norm_g[0].reshape(1, -1),
                            norm2_g[0].reshape(1, -1), wr_pack, wr_hi, b_route, tm=2 * OUTPROJ_SUB)

    xs, ws, pt, cnt = _dispatch(fn, info)
    gsrc, gdst, texp, tend, ntile = _plan(cnt[::8, :N_EXPERTS], seq)
    xy = _experts(gsrc, gdst, texp, tend, ntile, xs, ws, w1[0], w3[0], w2[0])
    out = _combine(xy, pt, h1, mod_lat[5:6], final_g.reshape(1, -1))
    return out.reshape(1, seq, D_MODEL)
```

```python
import functools

import jax
import jax.numpy as jnp
from jax import lax
from jax.experimental import pallas as pl
from jax.experimental.pallas import tpu as pltpu

F32 = jnp.float32
BF16 = jnp.bfloat16

D_MODEL = 2048
GRID_W = 64
GRID_SHIFT = 6
POOL_WINDOWS = (2, 4, 8, 16)
POOL_GROUP = 256
HEADS = 4
HEAD_DIM = 256
MIX_HALF = 1024
N_GATES = 16
N_GROUPS = 4
EXPERTS_PER_GROUP = 8
N_EXPERTS = 32
D_EXPERT = 512
EPS = 1e-6
LANES = 128
CHUNK = 256
ROUTE_LANE0 = N_GROUPS

VMEM_LIMIT = 60 * 1024 * 1024


def _cparams(sem, vmem=VMEM_LIMIT):
    return pltpu.CompilerParams(dimension_semantics=sem, vmem_limit_bytes=vmem)


def _split2(x):
    hi = x.astype(BF16)
    lo = (x - hi.astype(F32)).astype(BF16)
    return hi, lo


def _split3(x):
    hi = x.astype(BF16)
    r = x - hi.astype(F32)
    mid = r.astype(BF16)
    lo = (r - mid.astype(F32)).astype(BF16)
    return hi, mid, lo


def _dot(a, b):
    return jnp.dot(a, b, preferred_element_type=F32)


SPLIT_LANE = 64


def _split_pack(w):
    hi, lo = _split2(w)
    n = w.shape[1]
    gap = jnp.zeros((w.shape[0], SPLIT_LANE - n), BF16)
    rest = jnp.zeros((w.shape[0], LANES - n), BF16)
    return jnp.concatenate([hi, gap, lo, gap], axis=1), jnp.concatenate([hi, rest], axis=1)


def _split_dot(xh, xl, w_packed, w_hi):
    r = _dot(xh, w_packed)
    return r + pltpu.roll(r, SPLIT_LANE, 1) + _dot(xl, w_hi)


def _silu(x):
    return x * jax.nn.sigmoid(x)


def _log_sigmoid(x):
    return jnp.minimum(x, 0.0) - jnp.log(1.0 + jnp.exp(-jnp.abs(x)))


def _copy_kernel(w_ref, o_ref):
    o_ref[...] = w_ref[...]


def _take_rows(w, start, n):
    return pl.pallas_call(
        _copy_kernel,
        grid=(1,),
        in_specs=[pl.BlockSpec((n, w.shape[1]), lambda i: (start // n, 0))],
        out_specs=pl.BlockSpec((n, w.shape[1]), lambda i: (0, 0)),
        out_shape=jax.ShapeDtypeStruct((n, w.shape[1]), w.dtype),
        name="take_rows",
    )(w)


def _adaln_kernel(c_ref, w_ref, b_ref, o_ref):
    a = _silu(c_ref[...])
    a3 = jnp.concatenate(_split3(a), axis=0)
    w_hi, w_lo = _split2(w_ref[...])
    acc = _dot(a3, w_hi)
    acc_lo = _dot(a3[:32], w_lo)
    out = acc[0:16] + acc[16:32] + acc[32:48] + acc_lo[0:16] + acc_lo[16:32]
    o_ref[...] = out + b_ref[...]


def _adaln(c16, w_mod, b_mod):
    n = w_mod.shape[1]
    tn = 1536
    return pl.pallas_call(
        _adaln_kernel,
        grid=(n // tn,),
        in_specs=[pl.BlockSpec((16, D_MODEL), lambda j: (0, 0)),
                  pl.BlockSpec((D_MODEL, tn), lambda j: (0, j)),
                  pl.BlockSpec((1, tn), lambda j: (0, j))],
        out_specs=pl.BlockSpec((16, tn), lambda j: (0, j)),
        out_shape=jax.ShapeDtypeStruct((16, n), F32),
        compiler_params=_cparams(("arbitrary",)),
        name="adaln",
    )(c16, w_mod, b_mod)


HALO = 8
NEG_INF = float("-inf")


def _gate_scan_info(gates):
    n = gates.shape[0]
    lane = lax.broadcasted_iota(jnp.int32, gates.shape, 1)
    rows = lax.broadcasted_iota(jnp.int32, gates.shape, 0)
    lf = jnp.where(lane < N_GATES, _log_sigmoid(gates), 0.0)
    hi = lf.astype(BF16).astype(F32)
    rem = lf - hi
    mid = rem.astype(BF16).astype(F32)
    packed = (hi + pltpu.roll(mid, 32, 1) + pltpu.roll(rem - mid, 64, 1)).astype(BF16)
    r = lax.broadcasted_iota(jnp.int32, (n, n), 0)
    c = lax.broadcasted_iota(jnp.int32, (n, n), 1)
    pf = _dot(jnp.where(r >= c, 1.0, 0.0).astype(BF16), packed)
    pb = _dot(jnp.where(r <= c, 1.0, 0.0).astype(BF16), packed)
    bf = pf + pltpu.roll(pf, 96, 1) + pltpu.roll(pf, 64, 1)
    bb = pb + pltpu.roll(pb, 96, 1) + pltpu.roll(pb, 64, 1)
    b = jnp.where(lane < 8, bf, bb)
    cval = gates - pltpu.roll(b, LANES - 4, 1)
    pm = cval
    sm = cval
    step = 1
    while step < n:
        pm = jnp.maximum(pm, jnp.where(rows >= step, pltpu.roll(pm, step, 0), NEG_INF))
        sm = jnp.maximum(sm, jnp.where(rows < n - step, pltpu.roll(sm, n - step, 0), NEG_INF))
        step *= 2
    cm = jnp.where(lane < 8, pm, sm)
    return cval, jnp.where((lane & 4) == 0, cm, b)


W_CHUNK = 256


def _load_bf16(src_hbm, dst, stage, sem, *, n_chunks, transpose):
    rows = stage.shape[1]
    def chunk_copy(c, sl):
        return pltpu.make_async_copy(
            src_hbm.at[pl.ds(pl.multiple_of(c * rows, rows), rows), :], stage.at[sl], sem.at[sl])

    chunk_copy(0, 0).start()

    def convert(c, carry):
        sl = c % 2

        @pl.when(c + 1 < n_chunks)
        def _():
            chunk_copy(c + 1, 1 - sl).start()

        chunk_copy(c, sl).wait()
        span = pl.ds(pl.multiple_of(c * rows, rows), rows)
        if transpose:
            dst[:, span] = stage[sl].T.astype(BF16)
        else:
            dst[span, :] = stage[sl].astype(BF16)
        return carry

    lax.fori_loop(0, n_chunks, convert, 0)


def _inproj_kernel(x_ref, xp_ref, xn_ref, ctx_ref, mod_ref, g_ref, wt_hbm, wgp_ref, wgh_ref, gb_ref,
                   wcq_ref, wck_ref, pool_o, q_o, o_o, k_o, kt_o, v_o, col_o, row_o,
                   w_s, stage, sem, *, nt):
    i = pl.program_id(0)
    tm = x_ref.shape[0]

    @pl.when(i == 0)
    def _():
        _load_bf16(wt_hbm, w_s, stage, sem, n_chunks=w_s.shape[1] // W_CHUNK, transpose=True)

    is_ctx = i == nt
    x_main = jnp.where(is_ctx, ctx_ref[...], x_ref[...])
    x_all = jnp.concatenate([xp_ref[...], x_main, xn_ref[...]], axis=0)
    shift = jnp.where(is_ctx, mod_ref[0:1, :], mod_ref[2:3, :])
    scale = jnp.where(is_ctx, mod_ref[1:2, :], mod_ref[3:4, :])
    ms = jnp.mean(x_all * x_all, axis=-1, keepdims=True)
    y = x_all * lax.rsqrt(ms + EPS) * g_ref[...]
    xn_all = y * (1.0 + scale) + shift
    xh_all = xn_all.astype(BF16)
    xh, xl = _split2(xn_all[HALO:HALO + tm])

    def cols(ci):
        return w_s[:, ci * MIX_HALF:(ci + 1) * MIX_HALF]

    u_q = _dot(xh_all, cols(1))
    u_k = _dot(xh_all, cols(3))
    pool_o[...] = _dot(xh, cols(0))
    o_o[...] = _dot(xh, cols(2))
    v_o[...] = _dot(xh, cols(4)).astype(BF16)

    rowi = lax.broadcasted_iota(jnp.int32, (tm, MIX_HALF), 0)
    at_start = jnp.logical_and(rowi == 0, jnp.logical_or(i == 0, is_ctx))
    at_end = jnp.logical_and(rowi == tm - 1, jnp.logical_or(i == nt - 1, is_ctx))

    def conv_silu(u, wc_ref):
        n = u.shape[0]
        up = jnp.where(at_start, 0.0, pltpu.roll(u, 1, 0)[HALO:HALO + tm])
        un = jnp.where(at_end, 0.0, pltpu.roll(u, n - 1, 0)[HALO:HALO + tm])
        return _silu(wc_ref[0:1, :] * up + wc_ref[1:2, :] * u[HALO:HALO + tm] + wc_ref[2:3, :] * un)

    q_o[...] = (conv_silu(u_q, wcq_ref) * (HEAD_DIM ** -0.5)).astype(BF16)
    k = conv_silu(u_k, wck_ref)
    k_o[...] = k.astype(BF16)
    kt_o[...] = k.astype(BF16).T
    gates = _split_dot(xh, xl, wgp_ref[...], wgh_ref[...]) + gb_ref[...]
    cval, col = _gate_scan_info(gates)
    col_o[...] = col[:, :N_GATES]
    row_o[...] = cval.T[:N_GATES, :]


def _inproj(x2d, ctx2d, mod, g, w_in_t, wg_pack, wg_hi, gate_bias_row, wcq, wck):
    t = x2d.shape[0]
    tm = CHUNK
    assert ctx2d.shape[0] == tm
    nt = t // tm
    r8 = tm // HALO
    last8 = t // HALO - 1
    n_main = 5 * MIX_HALF
    rows = t + tm
    const = lambda i: (0, 0)
    row = lambda i: (i, 0)
    f32_out = jax.ShapeDtypeStruct((rows, MIX_HALF), F32)
    bf16_out = jax.ShapeDtypeStruct((rows, MIX_HALF), BF16)
    seq = pl.BlockSpec((tm, MIX_HALF), row)
    return pl.pallas_call(
        functools.partial(_inproj_kernel, nt=nt),
        grid=(nt + 1,),
        in_specs=[pl.BlockSpec((tm, D_MODEL), lambda i: (jnp.minimum(i, nt - 1), 0)),
                  pl.BlockSpec((HALO, D_MODEL), lambda i: (jnp.clip(i * r8 - 1, 0, last8), 0)),
                  pl.BlockSpec((HALO, D_MODEL), lambda i: (jnp.minimum((i + 1) * r8, last8), 0)),
                  pl.BlockSpec((tm, D_MODEL), const),
                  pl.BlockSpec((4, D_MODEL), const),
                  pl.BlockSpec((1, D_MODEL), const),
                  pl.BlockSpec(memory_space=pl.ANY),
                  pl.BlockSpec((D_MODEL, LANES), const),
                  pl.BlockSpec((D_MODEL, LANES), const),
                  pl.BlockSpec((1, LANES), const),
                  pl.BlockSpec((3, MIX_HALF), const),
                  pl.BlockSpec((3, MIX_HALF), const)],
        out_specs=[seq, seq, seq, seq, pl.BlockSpec((MIX_HALF, tm), lambda i: (0, i)), seq,
                   pl.BlockSpec((tm, N_GATES), row), pl.BlockSpec((N_GATES, tm), lambda i: (0, i))],
        out_shape=[f32_out, bf16_out, f32_out, bf16_out,
                   jax.ShapeDtypeStruct((MIX_HALF, rows), BF16), bf16_out,
                   jax.ShapeDtypeStruct((rows, N_GATES), F32),
                   jax.ShapeDtypeStruct((N_GATES, rows), F32)],
        scratch_shapes=[pltpu.VMEM((D_MODEL, n_main), BF16),
                        pltpu.VMEM((2, W_CHUNK, D_MODEL), F32),
                        pltpu.SemaphoreType.DMA((2,))],
        compiler_params=_cparams(("arbitrary",)),
        name="inproj",
    )(x2d, x2d, x2d, ctx2d, mod, g, w_in_t, wg_pack, wg_hi, gate_bias_row, wcq, wck)


def _mlstm_kernel(*refs, need_out):
    if need_out:
        (qf, kf, ktf, vf, colf, rowf, qb, kb, ktb, vb, colb, rowb, s0, n0, m0,
         hf_o, hb_o, s_s, n_s, m_s) = refs
        q_refs, h_outs = (qf, qb), (hf_o, hb_o)
    else:
        (kf, ktf, vf, colf, rowf, kb, ktb, vb, colb, rowb, s0, n0, m0,
         s_o, n_o, m_o, s_s, n_s, m_s) = refs
    k_refs, kt_refs, v_refs = (kf, kb), (ktf, ktb), (vf, vb)
    col_refs, row_refs = (colf, colb), (rowf, rowb)
    j = pl.program_id(0)

    @pl.when(j == 0)
    def _():
        s_s[...] = s0[...]
        n_s[...] = n0[...]
        m_s[...] = m0[...]

    L = CHUNK
    row = lax.broadcasted_iota(jnp.int32, (L, L), 0)
    col = lax.broadcasted_iota(jnp.int32, (L, L), 1)

    heads = [(d, h) for d in range(2) for h in range(HEADS)]

    def head_values(d, h):
        hd = d * HEADS + h
        lc, lb = 8 * d + h, 8 * d + 4 + h
        hs = slice(h * HEAD_DIM, (h + 1) * HEAD_DIM)
        edge = L - 1 if d == 0 else 0
        colv = col_refs[d][...]
        rowv = row_refs[d][...]
        v = dict(hd=hd, hs=hs, d=d)
        v["cm_c"], v["b_c"] = colv[:, lc:lc + 1], colv[:, lb:lb + 1]
        v["c_r"] = rowv[lc:lc + 1, :]
        v["g"] = v["b_c"][edge:edge + 1, :]
        v["m_old"] = m_s[hd][:, 0:1]
        v["s_old"] = s_s[hd]
        v["n_old"] = n_s[hd]
        v["m_x"] = jnp.maximum(v["m_old"], v["cm_c"][edge:edge + 1, :])
        v["decay"] = jnp.exp(v["m_old"] - v["m_x"])
        v["wk_r"] = jnp.exp(v["c_r"] - v["m_x"])
        v["v_h"] = v_refs[d][:, hs]
        v["kt_h"] = kt_refs[d][hs, :]
        return v

    def update_state(v):
        hd = v["hd"]
        n_s[hd] = v["decay"] * v["n_old"] + _dot(
            jnp.broadcast_to(v["wk_r"], (8, L)).astype(BF16), k_refs[v["d"]][:, v["hs"]])[0:1, :]
        kwt = (v["kt_h"].astype(F32) * v["wk_r"]).astype(BF16)
        s_s[hd] = v["decay"] * v["s_old"] + _dot(kwt, v["v_h"])
        m_s[hd] = jnp.broadcast_to(v["g"] + v["m_x"], (1, LANES))

    def weights(v):
        d, hs = v["d"], v["hs"]
        mask = (row >= col) if d == 0 else (row <= col)
        q_h = q_refs[d][:, hs]
        m_c = jnp.maximum(v["m_old"], v["cm_c"])
        w_inter = jnp.exp(v["m_old"] - m_c)
        v["e"] = jnp.where(mask, jnp.exp(v["c_r"] - m_c), 0.0)
        v["qw"] = (q_h.astype(F32) * w_inter).astype(BF16)
        qn = lax.dot_general(q_h, jnp.broadcast_to(v["n_old"], (8, HEAD_DIM)).astype(BF16),
                             (((1,), (1,)), ((), ())), preferred_element_type=F32)
        v["den_inter"] = w_inter * qn[:, 0:1]
        v["floor"] = jnp.exp(-(v["b_c"] + m_c))
        v["s_bf"] = v["s_old"].astype(BF16)

    def outputs(v, qk):
        p = v["e"] * qk
        num = _dot(v["qw"], v["s_bf"]) + _dot(p.astype(BF16), v["v_h"])
        den = v["den_inter"] + jnp.sum(p[:, :LANES] + p[:, LANES:], axis=1, keepdims=True)
        h_outs[v["d"]][:, v["hs"]] = num / jnp.maximum(jnp.abs(den), v["floor"])

    vals = [head_values(d, h) for d, h in heads]
    if need_out:
        qks = [_dot(q_refs[v["d"]][:, v["hs"]], v["kt_h"]) for v in vals]
        for v in vals:
            weights(v)
    for v in vals:
        update_state(v)
    if need_out:
        for v, qk in zip(vals, qks):
            outputs(v, qk)

    if not need_out:
        s_o[...] = s_s[...]
        n_o[...] = n_s[...]
        m_o[...] = m_s[...]


def _mlstm(q, k, kt, v, col, rowi, s0, n0, m0, need_out, first, nc):
    t = nc * CHUNK
    fwd = lambda j: (first + j, 0)
    bwd = lambda j: (first + nc - 1 - j, 0)
    fwd_t = lambda j: (0, first + j)
    bwd_t = lambda j: (0, first + nc - 1 - j)
    c3 = lambda j: (0, 0, 0)
    seq = lambda im: pl.BlockSpec((CHUNK, MIX_HALF), im)
    state_specs = [pl.BlockSpec(s0.shape, c3), pl.BlockSpec(n0.shape, c3), pl.BlockSpec(m0.shape, c3)]
    scratch = [pltpu.VMEM(s0.shape, F32), pltpu.VMEM(n0.shape, F32), pltpu.VMEM(m0.shape, F32)]

    def side(im, im_t):
        specs = ([seq(im)] if need_out else []) + [seq(im), pl.BlockSpec((MIX_HALF, CHUNK), im_t), seq(im)]
        return specs + [pl.BlockSpec((CHUNK, N_GATES), im), pl.BlockSpec((N_GATES, CHUNK), im_t)]

    in_specs = side(fwd, fwd_t) + side(bwd, bwd_t) + state_specs
    seq_in = ((q,) if need_out else ()) + (k, kt, v, col, rowi)
    args = seq_in + seq_in + (s0, n0, m0)
    if need_out:
        out_specs = [seq(fwd), seq(bwd)]
        out_shape = [jax.ShapeDtypeStruct((t, MIX_HALF), F32)] * 2
    else:
        out_specs = state_specs
        out_shape = [jax.ShapeDtypeStruct(a.shape, F32) for a in (s0, n0, m0)]
    return pl.pallas_call(
        functools.partial(_mlstm_kernel, need_out=need_out),
        grid=(nc,),
        in_specs=in_specs,
        out_specs=out_specs,
        out_shape=out_shape,
        scratch_shapes=scratch,
        compiler_params=_cparams(("arbitrary",)),
        name="mlstm_out" if need_out else "mlstm_state",
    )(*args)


POOL_PAD = 512
POOL_UNROLL = 4


def _pool_kernel(u_ref, w_ref, sc_ref, o_ref, pad_s, *, t):
    for gi, win in enumerate(POOL_WINDOWS):
        @pl.when(pl.program_id(0) == gi)
        def _():
            _pool_group(u_ref, w_ref, sc_ref, o_ref, pad_s, win=win, t=t)


def _pool_group(u_ref, w_ref, sc_ref, o_ref, pad_s, *, win, t):
    half = win // 2
    tile = 256
    zeros = jnp.zeros((POOL_PAD, POOL_GROUP), F32)
    pad_s[0:POOL_PAD, :] = zeros
    pad_s[POOL_PAD + t:POOL_PAD + t + POOL_PAD, :] = zeros

    def copy(r, carry):
        t0 = pl.multiple_of(r * tile, tile)
        pad_s[pl.ds(POOL_PAD + t0, tile), :] = u_ref[pl.ds(t0, tile), :]
        return carry

    lax.fori_loop(0, t // tile, copy, 0)

    row = lax.broadcasted_iota(jnp.int32, (tile, tile), 0)
    col = lax.broadcasted_iota(jnp.int32, (tile, tile), 1)
    same_row = (row >> GRID_SHIFT) == (col >> GRID_SHIFT)
    in_win = (col - row >= -half) & (col - row < half)
    band = jnp.where(same_row & in_win, 1.0, 0.0).astype(BF16)
    w = w_ref[...].astype(BF16)
    scale = sc_ref[...]
    n_rows = t // GRID_W

    tok0 = lax.broadcasted_iota(jnp.int32, (tile, POOL_GROUP), 0)
    gc = tok0 & (GRID_W - 1)
    inv_h = 1.0 / (jnp.minimum(gc + half, GRID_W) - jnp.maximum(gc - half, 0)).astype(F32)

    def body(r, carry):
        t0s = [pl.multiple_of((r * POOL_UNROLL + k) * tile, tile) for k in range(POOL_UNROLL)]
        pieces = []
        for t0 in t0s:
            acc = pad_s[pl.ds(POOL_PAD + t0 - GRID_W * half, tile), :]
            for dd in range(-half + 1, half):
                acc = acc + pad_s[pl.ds(POOL_PAD + t0 + GRID_W * dd, tile), :]
            gr = (t0 + tok0) >> GRID_SHIFT
            cnt_v = jnp.minimum(gr + half, n_rows) - jnp.maximum(gr - half, 0)
            pieces.append(_split2(acc / cnt_v.astype(F32)))
        means = [(_dot(band, hi) + _dot(band, lo)) * inv_h for hi, lo in pieces]
        diffs = [(m - pad_s[pl.ds(POOL_PAD + t0, tile), :]).astype(BF16) for m, t0 in zip(means, t0s)]
        for d, t0 in zip(diffs, t0s):
            o_ref[pl.ds(t0, tile), :] = (_dot(d, w) * scale).astype(BF16)
        return carry

    lax.fori_loop(0, t // (tile * POOL_UNROLL), body, 0)


def _pool(u_pool, w_pool, scale_row, t):
    return pl.pallas_call(
        functools.partial(_pool_kernel, t=t),
        grid=(len(POOL_WINDOWS),),
        in_specs=[pl.BlockSpec((t, POOL_GROUP), lambda g: (0, g)),
                  pl.BlockSpec((None, POOL_GROUP, POOL_GROUP), lambda g: (g, 0, 0)),
                  pl.BlockSpec((1, POOL_GROUP), lambda g: (0, g))],
        out_specs=pl.BlockSpec((t, POOL_GROUP), lambda g: (0, g)),
        out_shape=jax.ShapeDtypeStruct((t, MIX_HALF), BF16),
        scratch_shapes=[pltpu.VMEM((t + 2 * POOL_PAD, POOL_GROUP), F32)],
        compiler_params=_cparams(("arbitrary",)),
        name="pool_mix",
    )(u_pool, w_pool, scale_row)


OUTPROJ_SUB = 256


def _route(logits):
    lane = lax.broadcasted_iota(jnp.int32, logits.shape, 1).astype(F32)
    neg = -jnp.inf
    big = float(LANES)
    gl = jnp.where(lane < N_GROUPS, logits, neg)
    gmax = jnp.max(gl, axis=1, keepdims=True)
    gsel = jnp.min(jnp.where(gl == gmax, lane, big), axis=1, keepdims=True)
    p_grp = 1.0 / jnp.sum(jnp.exp(gl - gmax), axis=1, keepdims=True)
    lo = ROUTE_LANE0 + EXPERTS_PER_GROUP * gsel
    el = jnp.where((lane >= lo) & (lane < lo + EXPERTS_PER_GROUP), logits, neg)
    m1 = jnp.max(el, axis=1, keepdims=True)
    i1 = jnp.min(jnp.where(el == m1, lane, big), axis=1, keepdims=True)
    el2 = jnp.where(lane == i1, neg, el)
    m2 = jnp.max(el2, axis=1, keepdims=True)
    i2 = jnp.min(jnp.where(el2 == m2, lane, big), axis=1, keepdims=True)
    e2 = jnp.exp(m2 - m1)
    p1 = 1.0 / (1.0 + e2)
    p2 = e2 / (1.0 + e2)
    info = jnp.where(lane == 0.0, i1 - ROUTE_LANE0, 0.0)
    info = jnp.where(lane == 1.0, i2 - ROUTE_LANE0, info)
    info = jnp.where(lane == 2.0, p_grp * p1, info)
    return jnp.where(lane == 3.0, p_grp * p2, info)


def _outproj_kernel(p_ref, hf_ref, hb_ref, uo_ref, x_ref, wout_hbm, mod_ref,
                    hg_ref, n2g_ref, wrp_ref, wrh_ref, br_ref, h1_o, fn_o, info_o, w_s, stage, sem):
    @pl.when(pl.program_id(0) == 0)
    def _():
        _load_bf16(wout_hbm, w_s, stage, sem, n_chunks=w_s.shape[0] // stage.shape[1], transpose=False)

    def mixer_input(rs):
        h = hf_ref[rs, :] + hb_ref[rs, :]
        parts = []
        for hh in range(HEADS):
            hs = h[:, hh * HEAD_DIM:(hh + 1) * HEAD_DIM]
            mu = jnp.mean(hs, axis=-1, keepdims=True)
            ctr = hs - mu
            var = jnp.mean(ctr * ctr, axis=-1, keepdims=True)
            parts.append(ctr * lax.rsqrt(var + EPS))
        hn = jnp.concatenate(parts, axis=1) * hg_ref[...]
        m = (hn * jax.nn.sigmoid(uo_ref[rs, :])).astype(BF16)
        return jnp.concatenate([p_ref[rs, :], m], axis=1)

    def finish(rs, mix):
        h1 = x_ref[rs, :] + mod_ref[2:3, :] * mix
        h1_o[rs, :] = h1
        ms = jnp.mean(h1 * h1, axis=-1, keepdims=True)
        fn = h1 * lax.rsqrt(ms + EPS) * n2g_ref[...]
        fn = fn * (1.0 + mod_ref[4:5, :]) + mod_ref[3:4, :]
        fh, fl = _split2(fn)
        fn_o[rs, :] = fh
        logits = _split_dot(fh, fl, wrp_ref[...], wrh_ref[...]) + br_ref[...]
        info_o[rs, :] = _route(logits)

    tm = x_ref.shape[0]
    subs = [slice(r0, r0 + OUTPROJ_SUB) for r0 in range(0, tm, OUTPROJ_SUB)]
    w = w_s[...]
    mixes = [_dot(mixer_input(rs), w) for rs in subs]
    for rs, mix in zip(subs, mixes):
        finish(rs, mix)


def _outproj(p, hf, hb, uo, x2d, w_out, mod, head_g, norm2_g, wr_pack, wr_hi, b_route, tm):
    t = x2d.shape[0]
    const = lambda i: (0, 0)
    row = lambda i: (i, 0)
    in_specs = ([pl.BlockSpec((tm, MIX_HALF), row)] * 4
                + [pl.BlockSpec((tm, D_MODEL), row),
                   pl.BlockSpec(memory_space=pl.ANY),
                   pl.BlockSpec(mod.shape, const),
                   pl.BlockSpec((1, MIX_HALF), const),
                   pl.BlockSpec((1, D_MODEL), const),
                   pl.BlockSpec((D_MODEL, LANES), const),
                   pl.BlockSpec((D_MODEL, LANES), const),
                   pl.BlockSpec((1, LANES), const)])
    return pl.pallas_call(
        _outproj_kernel,
        grid=(t // tm,),
        in_specs=in_specs,
        out_specs=[pl.BlockSpec((tm, D_MODEL), row), pl.BlockSpec((tm, D_MODEL), row),
                   pl.BlockSpec((tm, LANES), row)],
        out_shape=[jax.ShapeDtypeStruct((t, D_MODEL), F32),
                   jax.ShapeDtypeStruct((t, D_MODEL), BF16),
                   jax.ShapeDtypeStruct((t, LANES), F32)],
        scratch_shapes=[pltpu.VMEM(w_out.shape, BF16),
                        pltpu.VMEM((2, W_CHUNK // 2, w_out.shape[1]), F32),
                        pltpu.SemaphoreType.DMA((2,))],
        compiler_params=_cparams(("arbitrary",)),
        name="outproj_route",
    )(p, hf, hb, uo, x2d, w_out, mod, head_g, norm2_g, wr_pack, wr_hi, b_route)


DISPATCH_BLOCK = 512
GRANULE = 16
GRANULE_SHIFT = 4
LOCAL_CAP = 1536
LOCAL_GRANULES = LOCAL_CAP // GRANULE
FREE_GRANULES = 2
EXPERT_TILE = 256
TILE_GRANULES = EXPERT_TILE // GRANULE
PLAN_UNROLL = 4
PLAN_SLACK = 8
GATHER_DEPTH = 3


def _dispatch_kernel(fn_ref, info_ref, xs_o, ws_o, pt_o, cnt_o):
    tb = DISPATCH_BLOCK
    info = info_ref[...]
    e1, e2 = info[:, 0:1], info[:, 1:2]
    w1c, w2c = info[:, 2:3], info[:, 3:4]
    lane = lax.broadcasted_iota(jnp.int32, (tb, LANES), 1).astype(F32)
    o1 = jnp.where(lane == e1, 1.0, 0.0)
    o2 = jnp.where(lane == e2, 1.0, 0.0)
    onehot = o1 + o2
    cnt = jnp.sum(onehot, axis=0, keepdims=True)
    gran = jnp.floor((cnt + (GRANULE - 1)) * (1.0 / GRANULE))
    a = lax.broadcasted_iota(jnp.int32, (LANES, LANES), 0)
    b = lax.broadcasted_iota(jnp.int32, (LANES, LANES), 1)
    upper = jnp.where(a < b, 1.0, 0.0).astype(BF16)
    seg_off = _dot(jnp.broadcast_to(gran, (8, LANES)).astype(BF16), upper)[0:1, :] * GRANULE
    r = lax.broadcasted_iota(jnp.int32, (tb, tb), 0)
    c = lax.broadcasted_iota(jnp.int32, (tb, tb), 1)
    strict = jnp.where(r > c, 1.0, 0.0).astype(BF16)
    rank = _dot(strict, onehot.astype(BF16))
    slot = rank + seg_off
    pos1 = jnp.sum(o1 * slot, axis=1, keepdims=True)
    pos2 = jnp.sum(o2 * slot, axis=1, keepdims=True)
    rows = lax.broadcasted_iota(jnp.int32, (tb, LOCAL_CAP), 1).astype(F32)
    pt1 = jnp.where(rows == pos1, 1.0, 0.0)
    pt2 = jnp.where(rows == pos2, 1.0, 0.0)
    pt = pt1 + pt2
    pt_o[...] = pt.astype(BF16)
    perm = pt.T.astype(BF16)
    w_slot = jnp.sum((pt1 * w1c + pt2 * w2c).T, axis=1, keepdims=True)
    for c0 in range(0, D_MODEL, 512):
        xs_o[:, c0:c0 + 512] = _dot(perm, fn_ref[:, c0:c0 + 512]).astype(BF16)
    ws_o[...] = jnp.broadcast_to(w_slot, (LOCAL_CAP, LANES))
    cnt_o[...] = jnp.broadcast_to(cnt, (8, LANES)).astype(jnp.int32)


def _dispatch(fn, info):
    t = fn.shape[0]
    nb = t // DISPATCH_BLOCK
    row = lambda i: (i, 0)
    return pl.pallas_call(
        _dispatch_kernel,
        grid=(nb,),
        in_specs=[pl.BlockSpec((DISPATCH_BLOCK, D_MODEL), row),
                  pl.BlockSpec((DISPATCH_BLOCK, LANES), row)],
        out_specs=[pl.BlockSpec((LOCAL_CAP, D_MODEL), row),
                   pl.BlockSpec((LOCAL_CAP, LANES), row),
                   pl.BlockSpec((DISPATCH_BLOCK, LOCAL_CAP), row),
                   pl.BlockSpec((8, LANES), row)],
        out_shape=[jax.ShapeDtypeStruct((nb * LOCAL_CAP, D_MODEL), BF16),
                   jax.ShapeDtypeStruct((nb * LOCAL_CAP, LANES), F32),
                   jax.ShapeDtypeStruct((t, LOCAL_CAP), BF16),
                   jax.ShapeDtypeStruct((nb * 8, LANES), jnp.int32)],
        compiler_params=_cparams(("arbitrary",)),
        name="moe_dispatch",
    )(fn, info)


def _map_len(max_tiles):
    return (max_tiles + GATHER_DEPTH - 1) * TILE_GRANULES + PLAN_SLACK


def _free_granule(q):
    return ((q // FREE_GRANULES) * LOCAL_GRANULES + (LOCAL_GRANULES - FREE_GRANULES)
            + (q % FREE_GRANULES))


def _plan_kernel(cnt_ref, gsrc_o, gdst_o, texp_o, tend_o, ntile_o, lrun, *, nb, max_tiles):
    def init(b, c):
        lrun[b] = 0
        return c

    lax.fori_loop(0, nb, init, 0)

    def per_expert(e, carry):
        g0, last_e = carry

        def per_block(b, g):
            k = (cnt_ref[b, e] + (GRANULE - 1)) >> GRANULE_SHIFT
            lo = lrun[b]
            lrun[b] = lo + k

            base = b * LOCAL_GRANULES + lo

            for j in range(PLAN_UNROLL):
                gsrc_o[g + j] = base + j
                gdst_o[g + j] = base + j

            @pl.when(k > PLAN_UNROLL)
            def _():
                def put(j, c):
                    gsrc_o[g + j] = base + j
                    gdst_o[g + j] = base + j
                    return c

                lax.fori_loop(PLAN_UNROLL, k, put, 0)

            return g + k

        g1 = lax.fori_loop(0, nb, per_block, g0)
        pad = (-g1) & (TILE_GRANULES - 1)

        def put_pad(j, c):
            g = g1 + j
            parity = (g // TILE_GRANULES) & 1
            gsrc_o[g] = _free_granule(0)
            gdst_o[g] = _free_granule(1 + parity * (TILE_GRANULES - 1) + (g & (TILE_GRANULES - 1)))
            return c

        lax.fori_loop(0, pad, put_pad, 0)
        g2 = g1 + pad

        def put_tile(tt, c):
            texp_o[tt] = e
            return c

        lax.fori_loop(g0 // TILE_GRANULES, g2 // TILE_GRANULES, put_tile, 0)
        tend_o[e] = g2 // TILE_GRANULES
        return g2, jnp.where(g2 > g0, e, last_e)

    g_end, last_e = lax.fori_loop(0, N_EXPERTS, per_expert, (0, 0))
    n_tiles = g_end // TILE_GRANULES
    ntile_o[0] = n_tiles

    def fill(tt, c):
        texp_o[tt] = last_e
        return c

    lax.fori_loop(n_tiles, max_tiles, fill, 0)

    def fill_map(g, c):
        gsrc_o[g] = _free_granule(0)
        gdst_o[g] = _free_granule(0)
        return c

    lax.fori_loop(g_end, _map_len(max_tiles), fill_map, 0)


def _max_tiles(t):
    nb = t // DISPATCH_BLOCK
    worst_rows = 2 * t + nb * N_EXPERTS * (GRANULE - 1) + N_EXPERTS * (EXPERT_TILE - GRANULE)
    return -(-worst_rows // EXPERT_TILE)


def _plan(cnt, t):
    nb = cnt.shape[0]
    assert nb * FREE_GRANULES >= 2 + 2 * (TILE_GRANULES - 1)
    max_tiles = _max_tiles(t)
    smem = pl.BlockSpec(memory_space=pltpu.SMEM)
    n_map = _map_len(max_tiles)
    return pl.pallas_call(
        functools.partial(_plan_kernel, nb=nb, max_tiles=max_tiles),
        in_specs=[smem],
        out_specs=[smem, smem, smem, smem, smem],
        out_shape=[jax.ShapeDtypeStruct((n_map,), jnp.int32),
                   jax.ShapeDtypeStruct((n_map,), jnp.int32),
                   jax.ShapeDtypeStruct((max_tiles,), jnp.int32),
                   jax.ShapeDtypeStruct((N_EXPERTS,), jnp.int32),
                   jax.ShapeDtypeStruct((1,), jnp.int32)],
        scratch_shapes=[pltpu.SMEM((nb,), jnp.int32)],
        name="moe_plan",
    )(cnt)


def _experts_kernel(gsrc, gdst, texp, tend, ntile, xy_in, ws_in, w1_hbm, w3_hbm, w2_hbm, xy_out,
                    xbuf, wsbuf, ybuf, st1, st3, st2, gsem, ssem, wsem):
    nt = ntile[0]

    def weight_copies(e, ws):
        return (pltpu.make_async_copy(w1_hbm.at[e], st1.at[ws], wsem.at[ws]),
                pltpu.make_async_copy(w3_hbm.at[e], st3.at[ws], wsem.at[ws]),
                pltpu.make_async_copy(w2_hbm.at[e], st2.at[ws], wsem.at[ws]))

    def rows(i):
        return pl.ds(pl.multiple_of(i * GRANULE, GRANULE), GRANULE)

    def gather_copies(g, j, sl):
        return (pltpu.make_async_copy(xy_in.at[rows(g), :], xbuf.at[sl, rows(j), :], gsem.at[sl]),
                pltpu.make_async_copy(ws_in.at[rows(g), :], wsbuf.at[sl, rows(j), :], gsem.at[sl]))

    def scatter_copies(g, j, sl):
        return (pltpu.make_async_copy(ybuf.at[sl, rows(j), :], xy_out.at[rows(g), :], ssem.at[sl]),)

    def issue(tt, sl, gmap, copies):
        for j in range(TILE_GRANULES):
            for cp in copies(gmap[tt * TILE_GRANULES + j], j, sl):
                cp.start()

    def drain(sl, copies):
        full = pl.ds(0, EXPERT_TILE)
        if copies is gather_copies:
            pltpu.make_async_copy(xy_in.at[full, :], xbuf.at[sl], gsem.at[sl]).wait()
            pltpu.make_async_copy(ws_in.at[full, :], wsbuf.at[sl], gsem.at[sl]).wait()
        else:
            pltpu.make_async_copy(ybuf.at[sl], xy_out.at[full, :], ssem.at[sl]).wait()

    @pl.when(nt > 0)
    def _():
        last = nt - 1
        for cp in weight_copies(texp[0], 0):
            cp.start()
        for ahead in range(GATHER_DEPTH - 1):
            issue(ahead, ahead, gsrc, gather_copies)

        def tile(t, wslot):
            slot = t % 2
            gslot = t % GATHER_DEPTH
            e = texp[t]
            first = (t == 0) | (texp[jnp.maximum(t - 1, 0)] != e)
            wslot = jnp.where(first & (t > 0), 1 - wslot, wslot)

            drain(gslot, gather_copies)
            ahead = t + GATHER_DEPTH - 1
            issue(ahead, ahead % GATHER_DEPTH, gsrc, gather_copies)

            @pl.when(first)
            def _():
                for cp in weight_copies(e, wslot):
                    cp.wait()
                nxt = tend[e]

                @pl.when(nxt < nt)
                def _():
                    for cp in weight_copies(texp[jnp.minimum(nxt, last)], 1 - wslot):
                        cp.start()

            x = xbuf[gslot]
            w_row = wsbuf[gslot][:, 0:1]
            a = _dot(x, st1[wslot].astype(BF16))
            b = _dot(x, st3[wslot].astype(BF16))
            y = _dot((_silu(a) * b).astype(BF16), st2[wslot].astype(BF16)) * w_row

            @pl.when(t >= 2)
            def _():
                drain(slot, scatter_copies)

            ybuf[slot] = y.astype(BF16)
            issue(t, slot, gdst, scatter_copies)
            return wslot

        lax.fori_loop(0, nt, tile, 0)

        for k in range(GATHER_DEPTH - 1):
            drain((nt + k) % GATHER_DEPTH, gather_copies)
        drain(last % 2, scatter_copies)

        @pl.when(nt >= 2)
        def _():
            drain(nt % 2, scatter_copies)


def _experts(gsrc, gdst, texp, tend, ntile, xy, ws, w1, w3, w2):
    smem = pl.BlockSpec(memory_space=pltpu.SMEM)
    hbm = pl.BlockSpec(memory_space=pl.ANY)
    return pl.pallas_call(
        _experts_kernel,
        in_specs=[smem, smem, smem, smem, smem, hbm, hbm, hbm, hbm, hbm],
        out_specs=hbm,
        out_shape=jax.ShapeDtypeStruct(xy.shape, xy.dtype),
        scratch_shapes=[pltpu.VMEM((GATHER_DEPTH, EXPERT_TILE, D_MODEL), BF16),
                        pltpu.VMEM((GATHER_DEPTH, EXPERT_TILE, LANES), F32),
                        pltpu.VMEM((2, EXPERT_TILE, D_MODEL), BF16),
                        pltpu.VMEM((2, D_MODEL, D_EXPERT), F32),
                        pltpu.VMEM((2, D_MODEL, D_EXPERT), F32),
                        pltpu.VMEM((2, D_EXPERT, D_MODEL), F32),
                        pltpu.SemaphoreType.DMA((GATHER_DEPTH,)),
                        pltpu.SemaphoreType.DMA((2,)),
                        pltpu.SemaphoreType.DMA((2,))],
        input_output_aliases={5: 0},
        compiler_params=pltpu.CompilerParams(vmem_limit_bytes=VMEM_LIMIT),
        name="moe_experts",
    )(gsrc, gdst, texp, tend, ntile, xy, ws, w1, w3, w2)


def _combine_kernel(y_ref, pt_ref, h1_ref, g2_ref, fg_ref, o_ref):
    moe = _dot(pt_ref[...], y_ref[...])
    h = h1_ref[...] + g2_ref[...] * moe
    ms = jnp.mean(h * h, axis=-1, keepdims=True)
    o_ref[...] = h * lax.rsqrt(ms + EPS) * fg_ref[...]


def _combine(xy, pt, h1, g2, final_g):
    t = h1.shape[0]
    row = lambda i: (i, 0)
    const = lambda i: (0, 0)
    return pl.pallas_call(
        _combine_kernel,
        grid=(t // DISPATCH_BLOCK,),
        in_specs=[pl.BlockSpec((LOCAL_CAP, D_MODEL), row),
                  pl.BlockSpec((DISPATCH_BLOCK, LOCAL_CAP), row),
                  pl.BlockSpec((DISPATCH_BLOCK, D_MODEL), row),
                  pl.BlockSpec((1, D_MODEL), const), pl.BlockSpec((1, D_MODEL), const)],
        out_specs=pl.BlockSpec((DISPATCH_BLOCK, D_MODEL), row),
        out_shape=jax.ShapeDtypeStruct((t, D_MODEL), F32),
        compiler_params=_cparams(("arbitrary",)),
        name="moe_combine_final",
    )(xy, pt, h1, g2, final_g)


def _pad_lanes(a):
    return jnp.pad(a, ((0, 0), (0, LANES - a.shape[1])))


def kernel(x, c, ctx, c_ctx, w_mod, b_mod, norm1_g, w_in, w_conv_q, w_conv_k, gate_bias, head_norm_g, w_pool, pool_scale, w_out, norm2_g, w_group, b_group, w_router, b_router, w1, w3, w2, final_g):
    assert x.shape[0] == 1 and w_mod.shape[0] == 1
    seq = x.shape[1]
    x2d = x[0]
    ctx2d = ctx[0]

    n_main = 5 * MIX_HALF
    w_in_t = jnp.transpose(w_in[0])
    wg_pack, wg_hi = _split_pack(jnp.transpose(_take_rows(w_in_t, n_main, N_GATES)))
    gate_bias_row = _pad_lanes(gate_bias[0].reshape(1, N_GATES))
    wr_pack, wr_hi = _split_pack(jnp.concatenate([w_group[0], w_router[0]], axis=1))
    b_route = _pad_lanes(jnp.concatenate([b_group[0], b_router[0]]).reshape(1, -1))
    norm1 = norm1_g[0].reshape(1, D_MODEL)

    c16 = jnp.concatenate([c, c_ctx[None, :], jnp.zeros((14, D_MODEL), F32)], axis=0)
    mods = _adaln(c16, w_mod[0], b_mod[0].reshape(1, -1))
    mod_lat = mods[0].reshape(6, D_MODEL)
    mod_ctx = mods[1].reshape(6, D_MODEL)

    mod_in = jnp.concatenate([mod_ctx[0:2], mod_lat[0:2]], axis=0)
    u_pool, q, uo, k, kt, v, col, rowi = _inproj(x2d, ctx2d, mod_in, norm1, w_in_t, wg_pack, wg_hi,
                                                 gate_bias_row, w_conv_q[0], w_conv_k[0])
    nc = seq // CHUNK
    zeros_state = (jnp.zeros((2 * HEADS, HEAD_DIM, HEAD_DIM), F32),
                   jnp.zeros((2 * HEADS, 1, HEAD_DIM), F32),
                   jnp.zeros((2 * HEADS, 1, LANES), F32))
    s0, n0, m0 = _mlstm(None, k, kt, v, col, rowi, *zeros_state, need_out=False, first=nc, nc=1)

    hf, hb = _mlstm(q, k, kt, v, col, rowi, s0, n0, m0, need_out=True, first=0, nc=nc)
    p = _pool(u_pool, w_pool[0], pool_scale[0].reshape(1, -1), seq)
    h1, fn, info = _outproj(p, hf, hb, uo, x2d, w_out[0], mod_lat, head_norm_g[0].reshape(1, -1),
                            norm2_g[0].reshape(1, -1), wr_pack, wr_hi, b_route, tm=2 * OUTPROJ_SUB)

    xs, ws, pt, cnt = _dispatch(fn, info)
    gsrc, gdst, texp, tend, ntile = _plan(cnt[::8, :N_EXPERTS], seq)
    xy = _experts(gsrc, gdst, texp, tend, ntile, xs, ws, w1[0], w3[0], w2[0])
    out = _combine(xy, pt, h1, mod_lat[5:6], final_g.reshape(1, -1))
    return out.reshape(1, seq, D_MODEL)
```

```python
import functools

import jax
import jax.numpy as jnp
from jax import lax
from jax.experimental import pallas as pl
from jax.experimental.pallas import tpu as pltpu

F32 = jnp.float32
BF16 = jnp.bfloat16

D_MODEL = 2048
GRID_W = 64
GRID_SHIFT = 6
POOL_WINDOWS = (2, 4, 8, 16)
POOL_GROUP = 256
HEADS = 4
HEAD_DIM = 256
MIX_HALF = 1024
N_GATES = 16
N_GROUPS = 4
EXPERTS_PER_GROUP = 8
N_EXPERTS = 32
D_EXPERT = 512
EPS = 1e-6
LANES = 128
CHUNK = 256
ROUTE_LANE0 = N_GROUPS

VMEM_LIMIT = 60 * 1024 * 1024


def _cparams(sem, vmem=VMEM_LIMIT):
    return pltpu.CompilerParams(dimension_semantics=sem, vmem_limit_bytes=vmem)


def _split2(x):
    hi = x.astype(BF16)
    lo = (x - hi.astype(F32)).astype(BF16)
    return hi, lo


def _split3(x):
    hi = x.astype(BF16)
    r = x - hi.astype(F32)
    mid = r.astype(BF16)
    lo = (r - mid.astype(F32)).astype(BF16)
    return hi, mid, lo


def _dot(a, b):
    return jnp.dot(a, b, preferred_element_type=F32)


SPLIT_LANE = 64


def _split_pack(w):
    hi, lo = _split2(w)
    n = w.shape[1]
    gap = jnp.zeros((w.shape[0], SPLIT_LANE - n), BF16)
    rest = jnp.zeros((w.shape[0], LANES - n), BF16)
    return jnp.concatenate([hi, gap, lo, gap], axis=1), jnp.concatenate([hi, rest], axis=1)


def _split_dot(xh, xl, w_packed, w_hi):
    r = _dot(xh, w_packed)
    return r + pltpu.roll(r, SPLIT_LANE, 1) + _dot(xl, w_hi)


def _silu(x):
    return x * jax.nn.sigmoid(x)


def _log_sigmoid(x):
    return jnp.minimum(x, 0.0) - jnp.log(1.0 + jnp.exp(-jnp.abs(x)))


def _copy_kernel(w_ref, o_ref):
    o_ref[...] = w_ref[...]


def _take_rows(w, start, n):
    return pl.pallas_call(
        _copy_kernel,
        grid=(1,),
        in_specs=[pl.BlockSpec((n, w.shape[1]), lambda i: (start // n, 0))],
        out_specs=pl.BlockSpec((n, w.shape[1]), lambda i: (0, 0)),
        out_shape=jax.ShapeDtypeStruct((n, w.shape[1]), w.dtype),
        name="take_rows",
    )(w)


def _adaln_kernel(c_ref, w_ref, b_ref, o_ref):
    a = _silu(c_ref[...])
    a3 = jnp.concatenate(_split3(a), axis=0)
    w_hi, w_lo = _split2(w_ref[...])
    acc = _dot(a3, w_hi)
    acc_lo = _dot(a3[:32], w_lo)
    out = acc[0:16] + acc[16:32] + acc[32:48] + acc_lo[0:16] + acc_lo[16:32]
    o_ref[...] = out + b_ref[...]


def _adaln(c16, w_mod, b_mod):
    n = w_mod.shape[1]
    tn = 1536
    return pl.pallas_call(
        _adaln_kernel,
        grid=(n // tn,),
        in_specs=[pl.BlockSpec((16, D_MODEL), lambda j: (0, 0)),
                  pl.BlockSpec((D_MODEL, tn), lambda j: (0, j)),
                  pl.BlockSpec((1, tn), lambda j: (0, j))],
        out_specs=pl.BlockSpec((16, tn), lambda j: (0, j)),
        out_shape=jax.ShapeDtypeStruct((16, n), F32),
        compiler_params=_cparams(("arbitrary",)),
        name="adaln",
    )(c16, w_mod, b_mod)


HALO = 8
NEG_INF = float("-inf")


def _gate_scan_info(gates):
    n = gates.shape[0]
    lane = lax.broadcasted_iota(jnp.int32, gates.shape, 1)
    rows = lax.broadcasted_iota(jnp.int32, gates.shape, 0)
    lf = jnp.where(lane < N_GATES, _log_sigmoid(gates), 0.0)
    hi = lf.astype(BF16).astype(F32)
    rem = lf - hi
    mid = rem.astype(BF16).astype(F32)
    packed = (hi + pltpu.roll(mid, 32, 1) + pltpu.roll(rem - mid, 64, 1)).astype(BF16)
    r = lax.broadcasted_iota(jnp.int32, (n, n), 0)
    c = lax.broadcasted_iota(jnp.int32, (n, n), 1)
    pf = _dot(jnp.where(r >= c, 1.0, 0.0).astype(BF16), packed)
    pb = _dot(jnp.where(r <= c, 1.0, 0.0).astype(BF16), packed)
    bf = pf + pltpu.roll(pf, 96, 1) + pltpu.roll(pf, 64, 1)
    bb = pb + pltpu.roll(pb, 96, 1) + pltpu.roll(pb, 64, 1)
    b = jnp.where(lane < 8, bf, bb)
    cval = gates - pltpu.roll(b, LANES - 4, 1)
    pm = cval
    sm = cval
    step = 1
    while step < n:
        pm = jnp.maximum(pm, jnp.where(rows >= step, pltpu.roll(pm, step, 0), NEG_INF))
        sm = jnp.maximum(sm, jnp.where(rows < n - step, pltpu.roll(sm, n - step, 0), NEG_INF))
        step *= 2
    cm = jnp.where(lane < 8, pm, sm)
    return cval, jnp.where((lane & 4) == 0, cm, b)


W_CHUNK = 256


def _load_bf16(src_hbm, dst, stage, sem, *, n_chunks, transpose):
    rows = stage.shape[1]
    def chunk_copy(c, sl):
        return pltpu.make_async_copy(
            src_hbm.at[pl.ds(pl.multiple_of(c * rows, rows), rows), :], stage.at[sl], sem.at[sl])

    chunk_copy(0, 0).start()

    def convert(c, carry):
        sl = c % 2

        @pl.when(c + 1 < n_chunks)
        def _():
            chunk_copy(c + 1, 1 - sl).start()

        chunk_copy(c, sl).wait()
        span = pl.ds(pl.multiple_of(c * rows, rows), rows)
        if transpose:
            dst[:, span] = stage[sl].T.astype(BF16)
        else:
            dst[span, :] = stage[sl].astype(BF16)
        return carry

    lax.fori_loop(0, n_chunks, convert, 0)


def _inproj_kernel(x_ref, xp_ref, xn_ref, ctx_ref, mod_ref, g_ref, wt_hbm, wgp_ref, wgh_ref, gb_ref,
                   wcq_ref, wck_ref, pool_o, q_o, o_o, k_o, kt_o, v_o, col_o, row_o,
                   w_s, stage, sem, *, nt):
    i = pl.program_id(0)
    tm = x_ref.shape[0]

    @pl.when(i == 0)
    def _():
        _load_bf16(wt_hbm, w_s, stage, sem, n_chunks=w_s.shape[1] // W_CHUNK, transpose=True)

    is_ctx = i == nt
    x_main = jnp.where(is_ctx, ctx_ref[...], x_ref[...])
    x_all = jnp.concatenate([xp_ref[...], x_main, xn_ref[...]], axis=0)
    shift = jnp.where(is_ctx, mod_ref[0:1, :], mod_ref[2:3, :])
    scale = jnp.where(is_ctx, mod_ref[1:2, :], mod_ref[3:4, :])
    ms = jnp.mean(x_all * x_all, axis=-1, keepdims=True)
    y = x_all * lax.rsqrt(ms + EPS) * g_ref[...]
    xn_all = y * (1.0 + scale) + shift
    xh_all = xn_all.astype(BF16)
    xh, xl = _split2(xn_all[HALO:HALO + tm])

    def cols(ci):
        return w_s[:, ci * MIX_HALF:(ci + 1) * MIX_HALF]

    u_q = _dot(xh_all, cols(1))
    u_k = _dot(xh_all, cols(3))
    pool_o[...] = _dot(xh, cols(0))
    o_o[...] = _dot(xh, cols(2))
    v_o[...] = _dot(xh, cols(4)).astype(BF16)

    rowi = lax.broadcasted_iota(jnp.int32, (tm, MIX_HALF), 0)
    at_start = jnp.logical_and(rowi == 0, jnp.logical_or(i == 0, is_ctx))
    at_end = jnp.logical_and(rowi == tm - 1, jnp.logical_or(i == nt - 1, is_ctx))

    def conv_silu(u, wc_ref):
        n = u.shape[0]
        up = jnp.where(at_start, 0.0, pltpu.roll(u, 1, 0)[HALO:HALO + tm])
        un = jnp.where(at_end, 0.0, pltpu.roll(u, n - 1, 0)[HALO:HALO + tm])
        return _silu(wc_ref[0:1, :] * up + wc_ref[1:2, :] * u[HALO:HALO + tm] + wc_ref[2:3, :] * un)

    q_o[...] = (conv_silu(u_q, wcq_ref) * (HEAD_DIM ** -0.5)).astype(BF16)
    k = conv_silu(u_k, wck_ref)
    k_o[...] = k.astype(BF16)
    kt_o[...] = k.astype(BF16).T
    gates = _split_dot(xh, xl, wgp_ref[...], wgh_ref[...]) + gb_ref[...]
    cval, col = _gate_scan_info(gates)
    col_o[...] = col[:, :N_GATES]
    row_o[...] = cval.T[:N_GATES, :]


def _inproj(x2d, ctx2d, mod, g, w_in_t, wg_pack, wg_hi, gate_bias_row, wcq, wck):
    t = x2d.shape[0]
    tm = CHUNK
    assert ctx2d.shape[0] == tm
    nt = t // tm
    r8 = tm // HALO
    last8 = t // HALO - 1
    n_main = 5 * MIX_HALF
    rows = t + tm
    const = lambda i: (0, 0)
    row = lambda i: (i, 0)
    f32_out = jax.ShapeDtypeStruct((rows, MIX_HALF), F32)
    bf16_out = jax.ShapeDtypeStruct((rows, MIX_HALF), BF16)
    seq = pl.BlockSpec((tm, MIX_HALF), row)
    return pl.pallas_call(
        functools.partial(_inproj_kernel, nt=nt),
        grid=(nt + 1,),
        in_specs=[pl.BlockSpec((tm, D_MODEL), lambda i: (jnp.minimum(i, nt - 1), 0)),
                  pl.BlockSpec((HALO, D_MODEL), lambda i: (jnp.clip(i * r8 - 1, 0, last8), 0)),
                  pl.BlockSpec((HALO, D_MODEL), lambda i: (jnp.minimum((i + 1) * r8, last8), 0)),
                  pl.BlockSpec((tm, D_MODEL), const),
                  pl.BlockSpec((4, D_MODEL), const),
                  pl.BlockSpec((1, D_MODEL), const),
                  pl.BlockSpec(memory_space=pl.ANY),
                  pl.BlockSpec((D_MODEL, LANES), const),
                  pl.BlockSpec((D_MODEL, LANES), const),
                  pl.BlockSpec((1, LANES), const),
                  pl.BlockSpec((3, MIX_HALF), const),
                  pl.BlockSpec((3, MIX_HALF), const)],
        out_specs=[seq, seq, seq, seq, pl.BlockSpec((MIX_HALF, tm), lambda i: (0, i)), seq,
                   pl.BlockSpec((tm, N_GATES), row), pl.BlockSpec((N_GATES, tm), lambda i: (0, i))],
        out_shape=[f32_out, bf16_out, f32_out, bf16_out,
                   jax.ShapeDtypeStruct((MIX_HALF, rows), BF16), bf16_out,
                   jax.ShapeDtypeStruct((rows, N_GATES), F32),
                   jax.ShapeDtypeStruct((N_GATES, rows), F32)],
        scratch_shapes=[pltpu.VMEM((D_MODEL, n_main), BF16),
                        pltpu.VMEM((2, W_CHUNK, D_MODEL), F32),
                        pltpu.SemaphoreType.DMA((2,))],
        compiler_params=_cparams(("arbitrary",)),
        name="inproj",
    )(x2d, x2d, x2d, ctx2d, mod, g, w_in_t, wg_pack, wg_hi, gate_bias_row, wcq, wck)


def _mlstm_kernel(*refs, need_out):
    if need_out:
        (qf, kf, ktf, vf, colf, rowf, qb, kb, ktb, vb, colb, rowb, s0, n0, m0,
         hf_o, hb_o, s_s, n_s, m_s) = refs
        q_refs, h_outs = (qf, qb), (hf_o, hb_o)
    else:
        (kf, ktf, vf, colf, rowf, kb, ktb, vb, colb, rowb, s0, n0, m0,
         s_o, n_o, m_o, s_s, n_s, m_s) = refs
    k_refs, kt_refs, v_refs = (kf, kb), (ktf, ktb), (vf, vb)
    col_refs, row_refs = (colf, colb), (rowf, rowb)
    j = pl.program_id(0)

    @pl.when(j == 0)
    def _():
        s_s[...] = s0[...]
        n_s[...] = n0[...]
        m_s[...] = m0[...]

    L = CHUNK
    row = lax.broadcasted_iota(jnp.int32, (L, L), 0)
    col = lax.broadcasted_iota(jnp.int32, (L, L), 1)

    heads = [(d, h) for d in range(2) for h in range(HEADS)]

    def head_values(d, h):
        hd = d * HEADS + h
        lc, lb = 8 * d + h, 8 * d + 4 + h
        hs = slice(h * HEAD_DIM, (h + 1) * HEAD_DIM)
        edge = L - 1 if d == 0 else 0
        colv = col_refs[d][...]
        rowv = row_refs[d][...]
        v = dict(hd=hd, hs=hs, d=d)
        v["cm_c"], v["b_c"] = colv[:, lc:lc + 1], colv[:, lb:lb + 1]
        v["c_r"] = rowv[lc:lc + 1, :]
        v["g"] = v["b_c"][edge:edge + 1, :]
        v["m_old"] = m_s[hd][:, 0:1]
        v["s_old"] = s_s[hd]
        v["n_old"] = n_s[hd]
        v["m_x"] = jnp.maximum(v["m_old"], v["cm_c"][edge:edge + 1, :])
        v["decay"] = jnp.exp(v["m_old"] - v["m_x"])
        v["wk_r"] = jnp.exp(v["c_r"] - v["m_x"])
        v["v_h"] = v_refs[d][:, hs]
        v["kt_h"] = kt_refs[d][hs, :]
        return v

    def update_state(v):
        hd = v["hd"]
        n_s[hd] = v["decay"] * v["n_old"] + _dot(
            jnp.broadcast_to(v["wk_r"], (8, L)).astype(BF16), k_refs[v["d"]][:, v["hs"]])[0:1, :]
        kwt = (v["kt_h"].astype(F32) * v["wk_r"]).astype(BF16)
        s_s[hd] = v["decay"] * v["s_old"] + _dot(kwt, v["v_h"])
        m_s[hd] = jnp.broadcast_to(v["g"] + v["m_x"], (1, LANES))

    def weights(v):
        d, hs = v["d"], v["hs"]
        mask = (row >= col) if d == 0 else (row <= col)
        q_h = q_refs[d][:, hs]
        m_c = jnp.maximum(v["m_old"], v["cm_c"])
        w_inter = jnp.exp(v["m_old"] - m_c)
        v["e"] = jnp.where(mask, jnp.exp(v["c_r"] - m_c), 0.0)
        v["qw"] = (q_h.astype(F32) * w_inter).astype(BF16)
        qn = lax.dot_general(q_h, jnp.broadcast_to(v["n_old"], (8, HEAD_DIM)).astype(BF16),
                             (((1,), (1,)), ((), ())), preferred_element_type=F32)
        v["den_inter"] = w_inter * qn[:, 0:1]
        v["floor"] = jnp.exp(-(v["b_c"] + m_c))
        v["s_bf"] = v["s_old"].astype(BF16)

    def outputs(v, qk):
        p = (v["e"] * qk).astype(BF16)
        num = _dot(v["qw"], v["s_bf"]) + _dot(p, v["v_h"])
        p_sum = lax.dot_general(p, jnp.ones((8, L), BF16), (((1,), (1,)), ((), ())),
                                preferred_element_type=F32)
        den = v["den_inter"] + p_sum[:, 0:1]
        h_outs[v["d"]][:, v["hs"]] = num / jnp.maximum(jnp.abs(den), v["floor"])

    vals = [head_values(d, h) for d, h in heads]
    if need_out:
        qks = [_dot(q_refs[v["d"]][:, v["hs"]], v["kt_h"]) for v in vals]
        for v in vals:
            weights(v)
    for v in vals:
        update_state(v)
    if need_out:
        for v, qk in zip(vals, qks):
            outputs(v, qk)

    if not need_out:
        s_o[...] = s_s[...]
        n_o[...] = n_s[...]
        m_o[...] = m_s[...]


def _mlstm(q, k, kt, v, col, rowi, s0, n0, m0, need_out, first, nc):
    t = nc * CHUNK
    fwd = lambda j: (first + j, 0)
    bwd = lambda j: (first + nc - 1 - j, 0)
    fwd_t = lambda j: (0, first + j)
    bwd_t = lambda j: (0, first + nc - 1 - j)
    c3 = lambda j: (0, 0, 0)
    seq = lambda im: pl.BlockSpec((CHUNK, MIX_HALF), im)
    state_specs = [pl.BlockSpec(s0.shape, c3), pl.BlockSpec(n0.shape, c3), pl.BlockSpec(m0.shape, c3)]
    scratch = [pltpu.VMEM(s0.shape, F32), pltpu.VMEM(n0.shape, F32), pltpu.VMEM(m0.shape, F32)]

    def side(im, im_t):
        specs = ([seq(im)] if need_out else []) + [seq(im), pl.BlockSpec((MIX_HALF, CHUNK), im_t), seq(im)]
        return specs + [pl.BlockSpec((CHUNK, N_GATES), im), pl.BlockSpec((N_GATES, CHUNK), im_t)]

    in_specs = side(fwd, fwd_t) + side(bwd, bwd_t) + state_specs
    seq_in = ((q,) if need_out else ()) + (k, kt, v, col, rowi)
    args = seq_in + seq_in + (s0, n0, m0)
    if need_out:
        out_specs = [seq(fwd), seq(bwd)]
        out_shape = [jax.ShapeDtypeStruct((t, MIX_HALF), F32)] * 2
    else:
        out_specs = state_specs
        out_shape = [jax.ShapeDtypeStruct(a.shape, F32) for a in (s0, n0, m0)]
    return pl.pallas_call(
        functools.partial(_mlstm_kernel, need_out=need_out),
        grid=(nc,),
        in_specs=in_specs,
        out_specs=out_specs,
        out_shape=out_shape,
        scratch_shapes=scratch,
        compiler_params=_cparams(("arbitrary",)),
        name="mlstm_out" if need_out else "mlstm_state",
    )(*args)


POOL_PAD = 512
POOL_UNROLL = 4


def _pool_kernel(u_ref, w_ref, sc_ref, o_ref, pad_s, *, t):
    for gi, win in enumerate(POOL_WINDOWS):
        @pl.when(pl.program_id(0) == gi)
        def _():
            _pool_group(u_ref, w_ref, sc_ref, o_ref, pad_s, win=win, t=t)


def _pool_group(u_ref, w_ref, sc_ref, o_ref, pad_s, *, win, t):
    half = win // 2
    tile = 256
    zeros = jnp.zeros((POOL_PAD, POOL_GROUP), F32)
    pad_s[0:POOL_PAD, :] = zeros
    pad_s[POOL_PAD + t:POOL_PAD + t + POOL_PAD, :] = zeros

    def copy(r, carry):
        t0 = pl.multiple_of(r * tile, tile)
        pad_s[pl.ds(POOL_PAD + t0, tile), :] = u_ref[pl.ds(t0, tile), :]
        return carry

    lax.fori_loop(0, t // tile, copy, 0)

    row = lax.broadcasted_iota(jnp.int32, (tile, tile), 0)
    col = lax.broadcasted_iota(jnp.int32, (tile, tile), 1)
    same_row = (row >> GRID_SHIFT) == (col >> GRID_SHIFT)
    in_win = (col - row >= -half) & (col - row < half)
    band = jnp.where(same_row & in_win, 1.0, 0.0).astype(BF16)
    w = w_ref[...].astype(BF16)
    scale = sc_ref[...]
    n_rows = t // GRID_W

    tok0 = lax.broadcasted_iota(jnp.int32, (tile, POOL_GROUP), 0)
    gc = tok0 & (GRID_W - 1)
    inv_h = 1.0 / (jnp.minimum(gc + half, GRID_W) - jnp.maximum(gc - half, 0)).astype(F32)

    def body(r, carry):
        t0s = [pl.multiple_of((r * POOL_UNROLL + k) * tile, tile) for k in range(POOL_UNROLL)]
        pieces = []
        for t0 in t0s:
            acc = pad_s[pl.ds(POOL_PAD + t0 - GRID_W * half, tile), :]
            for dd in range(-half + 1, half):
                acc = acc + pad_s[pl.ds(POOL_PAD + t0 + GRID_W * dd, tile), :]
            gr = (t0 + tok0) >> GRID_SHIFT
            cnt_v = jnp.minimum(gr + half, n_rows) - jnp.maximum(gr - half, 0)
            pieces.append(_split2(acc / cnt_v.astype(F32)))
        means = [(_dot(band, hi) + _dot(band, lo)) * inv_h for hi, lo in pieces]
        diffs = [(m - pad_s[pl.ds(POOL_PAD + t0, tile), :]).astype(BF16) for m, t0 in zip(means, t0s)]
        for d, t0 in zip(diffs, t0s):
            o_ref[pl.ds(t0, tile), :] = (_dot(d, w) * scale).astype(BF16)
        return carry

    lax.fori_loop(0, t // (tile * POOL_UNROLL), body, 0)


def _pool(u_pool, w_pool, scale_row, t):
    return pl.pallas_call(
        functools.partial(_pool_kernel, t=t),
        grid=(len(POOL_WINDOWS),),
        in_specs=[pl.BlockSpec((t, POOL_GROUP), lambda g: (0, g)),
                  pl.BlockSpec((None, POOL_GROUP, POOL_GROUP), lambda g: (g, 0, 0)),
                  pl.BlockSpec((1, POOL_GROUP), lambda g: (0, g))],
        out_specs=pl.BlockSpec((t, POOL_GROUP), lambda g: (0, g)),
        out_shape=jax.ShapeDtypeStruct((t, MIX_HALF), BF16),
        scratch_shapes=[pltpu.VMEM((t + 2 * POOL_PAD, POOL_GROUP), F32)],
        compiler_params=_cparams(("arbitrary",)),
        name="pool_mix",
    )(u_pool, w_pool, scale_row)


OUTPROJ_SUB = 256


def _route(logits):
    lane = lax.broadcasted_iota(jnp.int32, logits.shape, 1).astype(F32)
    neg = -jnp.inf
    big = float(LANES)
    gl = jnp.where(lane < N_GROUPS, logits, neg)
    gmax = jnp.max(gl, axis=1, keepdims=True)
    gsel = jnp.min(jnp.where(gl == gmax, lane, big), axis=1, keepdims=True)
    p_grp = 1.0 / jnp.sum(jnp.exp(gl - gmax), axis=1, keepdims=True)
    lo = ROUTE_LANE0 + EXPERTS_PER_GROUP * gsel
    el = jnp.where((lane >= lo) & (lane < lo + EXPERTS_PER_GROUP), logits, neg)
    m1 = jnp.max(el, axis=1, keepdims=True)
    i1 = jnp.min(jnp.where(el == m1, lane, big), axis=1, keepdims=True)
    el2 = jnp.where(lane == i1, neg, el)
    m2 = jnp.max(el2, axis=1, keepdims=True)
    i2 = jnp.min(jnp.where(el2 == m2, lane, big), axis=1, keepdims=True)
    e2 = jnp.exp(m2 - m1)
    p1 = 1.0 / (1.0 + e2)
    p2 = e2 / (1.0 + e2)
    info = jnp.where(lane == 0.0, i1 - ROUTE_LANE0, 0.0)
    info = jnp.where(lane == 1.0, i2 - ROUTE_LANE0, info)
    info = jnp.where(lane == 2.0, p_grp * p1, info)
    return jnp.where(lane == 3.0, p_grp * p2, info)


def _outproj_kernel(p_ref, hf_ref, hb_ref, uo_ref, x_ref, wout_hbm, mod_ref,
                    hg_ref, n2g_ref, wrp_ref, wrh_ref, br_ref, h1_o, fn_o, info_o, w_s, stage, sem):
    @pl.when(pl.program_id(0) == 0)
    def _():
        _load_bf16(wout_hbm, w_s, stage, sem, n_chunks=w_s.shape[0] // stage.shape[1], transpose=False)

    def mixer_input(rs):
        h = hf_ref[rs, :] + hb_ref[rs, :]
        parts = []
        for hh in range(HEADS):
            hs = h[:, hh * HEAD_DIM:(hh + 1) * HEAD_DIM]
            mu = jnp.mean(hs, axis=-1, keepdims=True)
            ctr = hs - mu
            var = jnp.mean(ctr * ctr, axis=-1, keepdims=True)
            parts.append(ctr * lax.rsqrt(var + EPS))
        hn = jnp.concatenate(parts, axis=1) * hg_ref[...]
        m = (hn * jax.nn.sigmoid(uo_ref[rs, :])).astype(BF16)
        return jnp.concatenate([p_ref[rs, :], m], axis=1)

    def finish(rs, mix):
        h1 = x_ref[rs, :] + mod_ref[2:3, :] * mix
        h1_o[rs, :] = h1
        ms = jnp.mean(h1 * h1, axis=-1, keepdims=True)
        fn = h1 * lax.rsqrt(ms + EPS) * n2g_ref[...]
        fn = fn * (1.0 + mod_ref[4:5, :]) + mod_ref[3:4, :]
        fh, fl = _split2(fn)
        fn_o[rs, :] = fh
        logits = _split_dot(fh, fl, wrp_ref[...], wrh_ref[...]) + br_ref[...]
        info_o[rs, :] = _route(logits)

    tm = x_ref.shape[0]
    subs = [slice(r0, r0 + OUTPROJ_SUB) for r0 in range(0, tm, OUTPROJ_SUB)]
    w = w_s[...]
    mixes = [_dot(mixer_input(rs), w) for rs in subs]
    for rs, mix in zip(subs, mixes):
        finish(rs, mix)


def _outproj(p, hf, hb, uo, x2d, w_out, mod, head_g, norm2_g, wr_pack, wr_hi, b_route, tm):
    t = x2d.shape[0]
    const = lambda i: (0, 0)
    row = lambda i: (i, 0)
    in_specs = ([pl.BlockSpec((tm, MIX_HALF), row)] * 4
                + [pl.BlockSpec((tm, D_MODEL), row),
                   pl.BlockSpec(memory_space=pl.ANY),
                   pl.BlockSpec(mod.shape, const),
                   pl.BlockSpec((1, MIX_HALF), const),
                   pl.BlockSpec((1, D_MODEL), const),
                   pl.BlockSpec((D_MODEL, LANES), const),
                   pl.BlockSpec((D_MODEL, LANES), const),
                   pl.BlockSpec((1, LANES), const)])
    return pl.pallas_call(
        _outproj_kernel,
        grid=(t // tm,),
        in_specs=in_specs,
        out_specs=[pl.BlockSpec((tm, D_MODEL), row), pl.BlockSpec((tm, D_MODEL), row),
                   pl.BlockSpec((tm, LANES), row)],
        out_shape=[jax.ShapeDtypeStruct((t, D_MODEL), F32),
                   jax.ShapeDtypeStruct((t, D_MODEL), BF16),
                   jax.ShapeDtypeStruct((t, LANES), F32)],
        scratch_shapes=[pltpu.VMEM(w_out.shape, BF16),
                        pltpu.VMEM((2, W_CHUNK // 2, w_out.shape[1]), F32),
                        pltpu.SemaphoreType.DMA((2,))],
        compiler_params=_cparams(("arbitrary",)),
        name="outproj_route",
    )(p, hf, hb, uo, x2d, w_out, mod, head_g, norm2_g, wr_pack, wr_hi, b_route)


DISPATCH_BLOCK = 512
GRANULE = 16
GRANULE_SHIFT = 4
LOCAL_CAP = 1536
LOCAL_GRANULES = LOCAL_CAP // GRANULE
FREE_GRANULES = 2
EXPERT_TILE = 256
TILE_GRANULES = EXPERT_TILE // GRANULE
PLAN_UNROLL = 4
PLAN_SLACK = 8
GATHER_DEPTH = 3


def _dispatch_kernel(fn_ref, info_ref, xs_o, ws_o, pt_o, cnt_o):
    tb = DISPATCH_BLOCK
    info = info_ref[...]
    e1, e2 = info[:, 0:1], info[:, 1:2]
    w1c, w2c = info[:, 2:3], info[:, 3:4]
    lane = lax.broadcasted_iota(jnp.int32, (tb, LANES), 1).astype(F32)
    o1 = jnp.where(lane == e1, 1.0, 0.0)
    o2 = jnp.where(lane == e2, 1.0, 0.0)
    onehot = o1 + o2
    cnt = jnp.sum(onehot, axis=0, keepdims=True)
    gran = jnp.floor((cnt + (GRANULE - 1)) * (1.0 / GRANULE))
    a = lax.broadcasted_iota(jnp.int32, (LANES, LANES), 0)
    b = lax.broadcasted_iota(jnp.int32, (LANES, LANES), 1)
    upper = jnp.where(a < b, 1.0, 0.0).astype(BF16)
    seg_off = _dot(jnp.broadcast_to(gran, (8, LANES)).astype(BF16), upper)[0:1, :] * GRANULE
    r = lax.broadcasted_iota(jnp.int32, (tb, tb), 0)
    c = lax.broadcasted_iota(jnp.int32, (tb, tb), 1)
    strict = jnp.where(r > c, 1.0, 0.0).astype(BF16)
    rank = _dot(strict, onehot.astype(BF16))
    slot = rank + seg_off
    pos1 = jnp.sum(o1 * slot, axis=1, keepdims=True)
    pos2 = jnp.sum(o2 * slot, axis=1, keepdims=True)
    rows = lax.broadcasted_iota(jnp.int32, (tb, LOCAL_CAP), 1).astype(F32)
    pt1 = jnp.where(rows == pos1, 1.0, 0.0)
    pt2 = jnp.where(rows == pos2, 1.0, 0.0)
    pt = pt1 + pt2
    pt_o[...] = pt.astype(BF16)
    perm = pt.T.astype(BF16)
    w_slot = jnp.sum((pt1 * w1c + pt2 * w2c).T, axis=1, keepdims=True)
    for c0 in range(0, D_MODEL, 512):
        xs_o[:, c0:c0 + 512] = _dot(perm, fn_ref[:, c0:c0 + 512]).astype(BF16)
    ws_o[...] = jnp.broadcast_to(w_slot, (LOCAL_CAP, LANES))
    cnt_o[...] = jnp.broadcast_to(cnt, (8, LANES)).astype(jnp.int32)


def _dispatch(fn, info):
    t = fn.shape[0]
    nb = t // DISPATCH_BLOCK
    row = lambda i: (i, 0)
    return pl.pallas_call(
        _dispatch_kernel,
        grid=(nb,),
        in_specs=[pl.BlockSpec((DISPATCH_BLOCK, D_MODEL), row),
                  pl.BlockSpec((DISPATCH_BLOCK, LANES), row)],
        out_specs=[pl.BlockSpec((LOCAL_CAP, D_MODEL), row),
                   pl.BlockSpec((LOCAL_CAP, LANES), row),
                   pl.BlockSpec((DISPATCH_BLOCK, LOCAL_CAP), row),
                   pl.BlockSpec((8, LANES), row)],
        out_shape=[jax.ShapeDtypeStruct((nb * LOCAL_CAP, D_MODEL), BF16),
                   jax.ShapeDtypeStruct((nb * LOCAL_CAP, LANES), F32),
                   jax.ShapeDtypeStruct((t, LOCAL_CAP), BF16),
                   jax.ShapeDtypeStruct((nb * 8, LANES), jnp.int32)],
        compiler_params=_cparams(("arbitrary",)),
        name="moe_dispatch",
    )(fn, info)


def _map_len(max_tiles):
    return (max_tiles + GATHER_DEPTH - 1) * TILE_GRANULES + PLAN_SLACK


def _free_granule(q):
    return ((q // FREE_GRANULES) * LOCAL_GRANULES + (LOCAL_GRANULES - FREE_GRANULES)
            + (q % FREE_GRANULES))


def _plan_kernel(cnt_ref, gsrc_o, gdst_o, texp_o, tend_o, ntile_o, lrun, *, nb, max_tiles):
    def init(b, c):
        lrun[b] = 0
        return c

    lax.fori_loop(0, nb, init, 0)

    def per_expert(e, carry):
        g0, last_e = carry

        def per_block(b, g):
            k = (cnt_ref[b, e] + (GRANULE - 1)) >> GRANULE_SHIFT
            lo = lrun[b]
            lrun[b] = lo + k

            base = b * LOCAL_GRANULES + lo

            for j in range(PLAN_UNROLL):
                gsrc_o[g + j] = base + j
                gdst_o[g + j] = base + j

            @pl.when(k > PLAN_UNROLL)
            def _():
                def put(j, c):
                    gsrc_o[g + j] = base + j
                    gdst_o[g + j] = base + j
                    return c

                lax.fori_loop(PLAN_UNROLL, k, put, 0)

            return g + k

        g1 = lax.fori_loop(0, nb, per_block, g0)
        pad = (-g1) & (TILE_GRANULES - 1)

        def put_pad(j, c):
            g = g1 + j
            parity = (g // TILE_GRANULES) & 1
            gsrc_o[g] = _free_granule(0)
            gdst_o[g] = _free_granule(1 + parity * (TILE_GRANULES - 1) + (g & (TILE_GRANULES - 1)))
            return c

        lax.fori_loop(0, pad, put_pad, 0)
        g2 = g1 + pad

        def put_tile(tt, c):
            texp_o[tt] = e
            return c

        lax.fori_loop(g0 // TILE_GRANULES, g2 // TILE_GRANULES, put_tile, 0)
        tend_o[e] = g2 // TILE_GRANULES
        return g2, jnp.where(g2 > g0, e, last_e)

    g_end, last_e = lax.fori_loop(0, N_EXPERTS, per_expert, (0, 0))
    n_tiles = g_end // TILE_GRANULES
    ntile_o[0] = n_tiles

    def fill(tt, c):
        texp_o[tt] = last_e
        return c

    lax.fori_loop(n_tiles, max_tiles, fill, 0)

    def fill_map(g, c):
        gsrc_o[g] = _free_granule(0)
        gdst_o[g] = _free_granule(0)
        return c

    lax.fori_loop(g_end, _map_len(max_tiles), fill_map, 0)


def _max_tiles(t):
    nb = t // DISPATCH_BLOCK
    worst_rows = 2 * t + nb * N_EXPERTS * (GRANULE - 1) + N_EXPERTS * (EXPERT_TILE - GRANULE)
    return -(-worst_rows // EXPERT_TILE)


def _plan(cnt, t):
    nb = cnt.shape[0]
    assert nb * FREE_GRANULES >= 2 + 2 * (TILE_GRANULES - 1)
    max_tiles = _max_tiles(t)
    smem = pl.BlockSpec(memory_space=pltpu.SMEM)
    n_map = _map_len(max_tiles)
    return pl.pallas_call(
        functools.partial(_plan_kernel, nb=nb, max_tiles=max_tiles),
        in_specs=[smem],
        out_specs=[smem, smem, smem, smem, smem],
        out_shape=[jax.ShapeDtypeStruct((n_map,), jnp.int32),
                   jax.ShapeDtypeStruct((n_map,), jnp.int32),
                   jax.ShapeDtypeStruct((max_tiles,), jnp.int32),
                   jax.ShapeDtypeStruct((N_EXPERTS,), jnp.int32),
                   jax.ShapeDtypeStruct((1,), jnp.int32)],
        scratch_shapes=[pltpu.SMEM((nb,), jnp.int32)],
        name="moe_plan",
    )(cnt)


def _experts_kernel(gsrc, gdst, texp, tend, ntile, xy_in, ws_in, w1_hbm, w3_hbm, w2_hbm, xy_out,
                    xbuf, wsbuf, ybuf, st1, st3, st2, gsem, ssem, wsem):
    nt = ntile[0]

    def weight_copies(e, ws):
        return (pltpu.make_async_copy(w1_hbm.at[e], st1.at[ws], wsem.at[ws]),
                pltpu.make_async_copy(w3_hbm.at[e], st3.at[ws], wsem.at[ws]),
                pltpu.make_async_copy(w2_hbm.at[e], st2.at[ws], wsem.at[ws]))

    def rows(i):
        return pl.ds(pl.multiple_of(i * GRANULE, GRANULE), GRANULE)

    def gather_copies(g, j, sl):
        return (pltpu.make_async_copy(xy_in.at[rows(g), :], xbuf.at[sl, rows(j), :], gsem.at[sl]),
                pltpu.make_async_copy(ws_in.at[rows(g), :], wsbuf.at[sl, rows(j), :], gsem.at[sl]))

    def scatter_copies(g, j, sl):
        return (pltpu.make_async_copy(ybuf.at[sl, rows(j), :], xy_out.at[rows(g), :], ssem.at[sl]),)

    def issue(tt, sl, gmap, copies):
        for j in range(TILE_GRANULES):
            for cp in copies(gmap[tt * TILE_GRANULES + j], j, sl):
                cp.start()

    def drain(sl, copies):
        full = pl.ds(0, EXPERT_TILE)
        if copies is gather_copies:
            pltpu.make_async_copy(xy_in.at[full, :], xbuf.at[sl], gsem.at[sl]).wait()
            pltpu.make_async_copy(ws_in.at[full, :], wsbuf.at[sl], gsem.at[sl]).wait()
        else:
            pltpu.make_async_copy(ybuf.at[sl], xy_out.at[full, :], ssem.at[sl]).wait()

    @pl.when(nt > 0)
    def _():
        last = nt - 1
        for cp in weight_copies(texp[0], 0):
            cp.start()
        for ahead in range(GATHER_DEPTH - 1):
            issue(ahead, ahead, gsrc, gather_copies)

        def tile(t, wslot):
            slot = t % 2
            gslot = t % GATHER_DEPTH
            e = texp[t]
            first = (t == 0) | (texp[jnp.maximum(t - 1, 0)] != e)
            wslot = jnp.where(first & (t > 0), 1 - wslot, wslot)

            drain(gslot, gather_copies)
            ahead = t + GATHER_DEPTH - 1
            issue(ahead, ahead % GATHER_DEPTH, gsrc, gather_copies)

            @pl.when(first)
            def _():
                for cp in weight_copies(e, wslot):
                    cp.wait()
                nxt = tend[e]

                @pl.when(nxt < nt)
                def _():
                    for cp in weight_copies(texp[jnp.minimum(nxt, last)], 1 - wslot):
                        cp.start()

            x = xbuf[gslot]
            w_row = wsbuf[gslot][:, 0:1]
            a = _dot(x, st1[wslot].astype(BF16))
            b = _dot(x, st3[wslot].astype(BF16))
            y = _dot((_silu(a) * b).astype(BF16), st2[wslot].astype(BF16)) * w_row

            @pl.when(t >= 2)
            def _():
                drain(slot, scatter_copies)

            ybuf[slot] = y.astype(BF16)
            issue(t, slot, gdst, scatter_copies)
            return wslot

        lax.fori_loop(0, nt, tile, 0)

        for k in range(GATHER_DEPTH - 1):
            drain((nt + k) % GATHER_DEPTH, gather_copies)
        drain(last % 2, scatter_copies)

        @pl.when(nt >= 2)
        def _():
            drain(nt % 2, scatter_copies)


def _experts(gsrc, gdst, texp, tend, ntile, xy, ws, w1, w3, w2):
    smem = pl.BlockSpec(memory_space=pltpu.SMEM)
    hbm = pl.BlockSpec(memory_space=pl.ANY)
    return pl.pallas_call(
        _experts_kernel,
        in_specs=[smem, smem, smem, smem, smem, hbm, hbm, hbm, hbm, hbm],
        out_specs=hbm,
        out_shape=jax.ShapeDtypeStruct(xy.shape, xy.dtype),
        scratch_shapes=[pltpu.VMEM((GATHER_DEPTH, EXPERT_TILE, D_MODEL), BF16),
                        pltpu.VMEM((GATHER_DEPTH, EXPERT_TILE, LANES), F32),
                        pltpu.VMEM((2, EXPERT_TILE, D_MODEL), BF16),
                        pltpu.VMEM((2, D_MODEL, D_EXPERT), F32),
                        pltpu.VMEM((2, D_MODEL, D_EXPERT), F32),
                        pltpu.VMEM((2, D_EXPERT, D_MODEL), F32),
                        pltpu.SemaphoreType.DMA((GATHER_DEPTH,)),
                        pltpu.SemaphoreType.DMA((2,)),
                        pltpu.SemaphoreType.DMA((2,))],
        input_output_aliases={5: 0},
        compiler_params=pltpu.CompilerParams(vmem_limit_bytes=VMEM_LIMIT),
        name="moe_experts",
    )(gsrc, gdst, texp, tend, ntile, xy, ws, w1, w3, w2)


def _combine_kernel(y_ref, pt_ref, h1_ref, g2_ref, fg_ref, o_ref):
    moe = _dot(pt_ref[...], y_ref[...])
    h = h1_ref[...] + g2_ref[...] * moe
    ms = jnp.mean(h * h, axis=-1, keepdims=True)
    o_ref[...] = h * lax.rsqrt(ms + EPS) * fg_ref[...]


def _combine(xy, pt, h1, g2, final_g):
    t = h1.shape[0]
    row = lambda i: (i, 0)
    const = lambda i: (0, 0)
    return pl.pallas_call(
        _combine_kernel,
        grid=(t // DISPATCH_BLOCK,),
        in_specs=[pl.BlockSpec((LOCAL_CAP, D_MODEL), row),
                  pl.BlockSpec((DISPATCH_BLOCK, LOCAL_CAP), row),
                  pl.BlockSpec((DISPATCH_BLOCK, D_MODEL), row),
                  pl.BlockSpec((1, D_MODEL), const), pl.BlockSpec((1, D_MODEL), const)],
        out_specs=pl.BlockSpec((DISPATCH_BLOCK, D_MODEL), row),
        out_shape=jax.ShapeDtypeStruct((t, D_MODEL), F32),
        compiler_params=_cparams(("arbitrary",)),
        name="moe_combine_final",
    )(xy, pt, h1, g2, final_g)


def _pad_lanes(a):
    return jnp.pad(a, ((0, 0), (0, LANES - a.shape[1])))


def kernel(x, c, ctx, c_ctx, w_mod, b_mod, norm1_g, w_in, w_conv_q, w_conv_k, gate_bias, head_norm_g, w_pool, pool_scale, w_out, norm2_g, w_group, b_group, w_router, b_router, w1, w3, w2, final_g):
    assert x.shape[0] == 1 and w_mod.shape[0] == 1
    seq = x.shape[1]
    x2d = x[0]
    ctx2d = ctx[0]

    n_main = 5 * MIX_HALF
    w_in_t = jnp.transpose(w_in[0])
    wg_pack, wg_hi = _split_pack(jnp.transpose(_take_rows(w_in_t, n_main, N_GATES)))
    gate_bias_row = _pad_lanes(gate_bias[0].reshape(1, N_GATES))
    wr_pack, wr_hi = _split_pack(jnp.concatenate([w_group[0], w_router[0]], axis=1))
    b_route = _pad_lanes(jnp.concatenate([b_group[0], b_router[0]]).reshape(1, -1))
    norm1 = norm1_g[0].reshape(1, D_MODEL)

    c16 = jnp.concatenate([c, c_ctx[None, :], jnp.zeros((14, D_MODEL), F32)], axis=0)
    mods = _adaln(c16, w_mod[0], b_mod[0].reshape(1, -1))
    mod_lat = mods[0].reshape(6, D_MODEL)
    mod_ctx = mods[1].reshape(6, D_MODEL)

    mod_in = jnp.concatenate([mod_ctx[0:2], mod_lat[0:2]], axis=0)
    u_pool, q, uo, k, kt, v, col, rowi = _inproj(x2d, ctx2d, mod_in, norm1, w_in_t, wg_pack, wg_hi,
                                                 gate_bias_row, w_conv_q[0], w_conv_k[0])
    nc = seq // CHUNK
    zeros_state = (jnp.zeros((2 * HEADS, HEAD_DIM, HEAD_DIM), F32),
                   jnp.zeros((2 * HEADS, 1, HEAD_DIM), F32),
                   jnp.zeros((2 * HEADS, 1, LANES), F32))
    s0, n0, m0 = _mlstm(None, k, kt, v, col, rowi, *zeros_state, need_out=False, first=nc, nc=1)

    hf, hb = _mlstm(q, k, kt, v, col, rowi, s0, n0, m0, need_out=True, first=0, nc=nc)
    p = _pool(u_pool, w_pool[0], pool_scale[0].reshape(1, -1), seq)
    h1, fn, info = _outproj(p, hf, hb, uo, x2d, w_out[0], mod_lat, head_norm_g[0].reshape(1, -1),
                            norm2_g[0].reshape(1, -1), wr_pack, wr_hi, b_route, tm=2 * OUTPROJ_SUB)

    xs, ws, pt, cnt = _dispatch(fn, info)
    gsrc, gdst, texp, tend, ntile = _plan(cnt[::8, :N_EXPERTS], seq)
    xy = _experts(gsrc, gdst, texp, tend, ntile, xs, ws, w1[0], w3[0], w2[0])
    out = _combine(xy, pt, h1, mod_lat[5:6], final_g.reshape(1, -1))
    return out.reshape(1, seq, D_MODEL)
```

```python
import functools

import jax
import jax.numpy as jnp
from jax import lax
from jax.experimental import pallas as pl
from jax.experimental.pallas import tpu as pltpu

F32 = jnp.float32
BF16 = jnp.bfloat16

D_MODEL = 2048
GRID_W = 64
GRID_SHIFT = 6
POOL_WINDOWS = (2, 4, 8, 16)
POOL_GROUP = 256
HEADS = 4
HEAD_DIM = 256
MIX_HALF = 1024
N_GATES = 16
N_GROUPS = 4
EXPERTS_PER_GROUP = 8
N_EXPERTS = 32
D_EXPERT = 512
EPS = 1e-6
LANES = 128
CHUNK = 256
ROUTE_LANE0 = N_GROUPS

VMEM_LIMIT = 60 * 1024 * 1024


def _cparams(sem, vmem=VMEM_LIMIT):
    return pltpu.CompilerParams(dimension_semantics=sem, vmem_limit_bytes=vmem)


def _split2(x):
    hi = x.astype(BF16)
    lo = (x - hi.astype(F32)).astype(BF16)
    return hi, lo


def _split3(x):
    hi = x.astype(BF16)
    r = x - hi.astype(F32)
    mid = r.astype(BF16)
    lo = (r - mid.astype(F32)).astype(BF16)
    return hi, mid, lo


def _dot(a, b):
    return jnp.dot(a, b, preferred_element_type=F32)


SPLIT_LANE = 64


def _split_pack(w):
    hi, lo = _split2(w)
    n = w.shape[1]
    gap = jnp.zeros((w.shape[0], SPLIT_LANE - n), BF16)
    rest = jnp.zeros((w.shape[0], LANES - n), BF16)
    return jnp.concatenate([hi, gap, lo, gap], axis=1), jnp.concatenate([hi, rest], axis=1)


def _split_dot(xh, xl, w_packed, w_hi):
    r = _dot(xh, w_packed)
    return r + pltpu.roll(r, SPLIT_LANE, 1) + _dot(xl, w_hi)


def _silu(x):
    return x * jax.nn.sigmoid(x)


def _log_sigmoid(x):
    return jnp.minimum(x, 0.0) - jnp.log(1.0 + jnp.exp(-jnp.abs(x)))


def _copy_kernel(w_ref, o_ref):
    o_ref[...] = w_ref[...]


def _take_rows(w, start, n):
    return pl.pallas_call(
        _copy_kernel,
        grid=(1,),
        in_specs=[pl.BlockSpec((n, w.shape[1]), lambda i: (start // n, 0))],
        out_specs=pl.BlockSpec((n, w.shape[1]), lambda i: (0, 0)),
        out_shape=jax.ShapeDtypeStruct((n, w.shape[1]), w.dtype),
        name="take_rows",
    )(w)


def _adaln_kernel(c_ref, w_ref, b_ref, o_ref):
    a = _silu(c_ref[...])
    a3 = jnp.concatenate(_split3(a), axis=0)
    w_hi, w_lo = _split2(w_ref[...])
    acc = _dot(a3, w_hi)
    acc_lo = _dot(a3[:32], w_lo)
    out = acc[0:16] + acc[16:32] + acc[32:48] + acc_lo[0:16] + acc_lo[16:32]
    o_ref[...] = out + b_ref[...]


def _adaln(c16, w_mod, b_mod):
    n = w_mod.shape[1]
    tn = 1536
    return pl.pallas_call(
        _adaln_kernel,
        grid=(n // tn,),
        in_specs=[pl.BlockSpec((16, D_MODEL), lambda j: (0, 0)),
                  pl.BlockSpec((D_MODEL, tn), lambda j: (0, j)),
                  pl.BlockSpec((1, tn), lambda j: (0, j))],
        out_specs=pl.BlockSpec((16, tn), lambda j: (0, j)),
        out_shape=jax.ShapeDtypeStruct((16, n), F32),
        compiler_params=_cparams(("arbitrary",)),
        name="adaln",
    )(c16, w_mod, b_mod)


HALO = 8
NEG_INF = float("-inf")


def _gate_scan_info(gates):
    n = gates.shape[0]
    lane = lax.broadcasted_iota(jnp.int32, gates.shape, 1)
    rows = lax.broadcasted_iota(jnp.int32, gates.shape, 0)
    lf = jnp.where(lane < N_GATES, _log_sigmoid(gates), 0.0)
    hi = lf.astype(BF16).astype(F32)
    rem = lf - hi
    mid = rem.astype(BF16).astype(F32)
    packed = (hi + pltpu.roll(mid, 32, 1) + pltpu.roll(rem - mid, 64, 1)).astype(BF16)
    r = lax.broadcasted_iota(jnp.int32, (n, n), 0)
    c = lax.broadcasted_iota(jnp.int32, (n, n), 1)
    pf = _dot(jnp.where(r >= c, 1.0, 0.0).astype(BF16), packed)
    pb = _dot(jnp.where(r <= c, 1.0, 0.0).astype(BF16), packed)
    bf = pf + pltpu.roll(pf, 96, 1) + pltpu.roll(pf, 64, 1)
    bb = pb + pltpu.roll(pb, 96, 1) + pltpu.roll(pb, 64, 1)
    b = jnp.where(lane < 8, bf, bb)
    cval = gates - pltpu.roll(b, LANES - 4, 1)
    pm = cval
    sm = cval
    step = 1
    while step < n:
        pm = jnp.maximum(pm, jnp.where(rows >= step, pltpu.roll(pm, step, 0), NEG_INF))
        sm = jnp.maximum(sm, jnp.where(rows < n - step, pltpu.roll(sm, n - step, 0), NEG_INF))
        step *= 2
    cm = jnp.where(lane < 8, pm, sm)
    return cval, jnp.where((lane & 4) == 0, cm, b)


W_CHUNK = 256


def _load_bf16(src_hbm, dst, stage, sem, *, n_chunks, transpose):
    rows = stage.shape[1]
    def chunk_copy(c, sl):
        return pltpu.make_async_copy(
            src_hbm.at[pl.ds(pl.multiple_of(c * rows, rows), rows), :], stage.at[sl], sem.at[sl])

    chunk_copy(0, 0).start()

    def convert(c, carry):
        sl = c % 2

        @pl.when(c + 1 < n_chunks)
        def _():
            chunk_copy(c + 1, 1 - sl).start()

        chunk_copy(c, sl).wait()
        span = pl.ds(pl.multiple_of(c * rows, rows), rows)
        if transpose:
            dst[:, span] = stage[sl].T.astype(BF16)
        else:
            dst[span, :] = stage[sl].astype(BF16)
        return carry

    lax.fori_loop(0, n_chunks, convert, 0)


def _inproj_kernel(x_ref, xp_ref, xn_ref, ctx_ref, mod_ref, g_ref, wt_hbm, wg_ref, gb_ref,
                   wcq_ref, wck_ref, pool_o, q_o, o_o, k_o, kt_o, v_o, col_o, row_o,
                   w_s, stage, sem, *, nt):
    i = pl.program_id(0)
    tm = x_ref.shape[0]

    @pl.when(i == 0)
    def _():
        _load_bf16(wt_hbm, w_s, stage, sem, n_chunks=w_s.shape[1] // W_CHUNK, transpose=True)

    is_ctx = i == nt
    x_main = jnp.where(is_ctx, ctx_ref[...], x_ref[...])
    x_all = jnp.concatenate([xp_ref[...], x_main, xn_ref[...]], axis=0)
    shift = jnp.where(is_ctx, mod_ref[0:1, :], mod_ref[2:3, :])
    scale = jnp.where(is_ctx, mod_ref[1:2, :], mod_ref[3:4, :])
    ms = jnp.mean(x_all * x_all, axis=-1, keepdims=True)
    y = x_all * lax.rsqrt(ms + EPS) * g_ref[...]
    xn_all = y * (1.0 + scale) + shift
    xh_all = xn_all.astype(BF16)
    xh = xn_all[HALO:HALO + tm].astype(BF16)

    def cols(ci):
        return w_s[:, ci * MIX_HALF:(ci + 1) * MIX_HALF]

    u_q = _dot(xh_all, cols(1))
    u_k = _dot(xh_all, cols(3))
    pool_o[...] = _dot(xh, cols(0))
    o_o[...] = _dot(xh, cols(2))
    v_o[...] = _dot(xh, cols(4)).astype(BF16)

    rowi = lax.broadcasted_iota(jnp.int32, (tm, MIX_HALF), 0)
    at_start = jnp.logical_and(rowi == 0, jnp.logical_or(i == 0, is_ctx))
    at_end = jnp.logical_and(rowi == tm - 1, jnp.logical_or(i == nt - 1, is_ctx))

    def conv_silu(u, wc_ref):
        n = u.shape[0]
        up = jnp.where(at_start, 0.0, pltpu.roll(u, 1, 0)[HALO:HALO + tm])
        un = jnp.where(at_end, 0.0, pltpu.roll(u, n - 1, 0)[HALO:HALO + tm])
        return _silu(wc_ref[0:1, :] * up + wc_ref[1:2, :] * u[HALO:HALO + tm] + wc_ref[2:3, :] * un)

    q_o[...] = (conv_silu(u_q, wcq_ref) * (HEAD_DIM ** -0.5)).astype(BF16)
    k = conv_silu(u_k, wck_ref)
    k_o[...] = k.astype(BF16)
    kt_o[...] = k.astype(BF16).T
    gates = _dot(xh, wg_ref[...]) + gb_ref[...]
    cval, col = _gate_scan_info(gates)
    col_o[...] = col[:, :N_GATES]
    row_o[...] = cval.T[:N_GATES, :]


def _inproj(x2d, ctx2d, mod, g, w_in_t, w_gate, gate_bias_row, wcq, wck):
    t = x2d.shape[0]
    tm = CHUNK
    assert ctx2d.shape[0] == tm
    nt = t // tm
    r8 = tm // HALO
    last8 = t // HALO - 1
    n_main = 5 * MIX_HALF
    rows = t + tm
    const = lambda i: (0, 0)
    row = lambda i: (i, 0)
    f32_out = jax.ShapeDtypeStruct((rows, MIX_HALF), F32)
    bf16_out = jax.ShapeDtypeStruct((rows, MIX_HALF), BF16)
    seq = pl.BlockSpec((tm, MIX_HALF), row)
    return pl.pallas_call(
        functools.partial(_inproj_kernel, nt=nt),
        grid=(nt + 1,),
        in_specs=[pl.BlockSpec((tm, D_MODEL), lambda i: (jnp.minimum(i, nt - 1), 0)),
                  pl.BlockSpec((HALO, D_MODEL), lambda i: (jnp.clip(i * r8 - 1, 0, last8), 0)),
                  pl.BlockSpec((HALO, D_MODEL), lambda i: (jnp.minimum((i + 1) * r8, last8), 0)),
                  pl.BlockSpec((tm, D_MODEL), const),
                  pl.BlockSpec((4, D_MODEL), const),
                  pl.BlockSpec((1, D_MODEL), const),
                  pl.BlockSpec(memory_space=pl.ANY),
                  pl.BlockSpec((D_MODEL, LANES), const),
                  pl.BlockSpec((1, LANES), const),
                  pl.BlockSpec((3, MIX_HALF), const),
                  pl.BlockSpec((3, MIX_HALF), const)],
        out_specs=[seq, seq, seq, seq, pl.BlockSpec((MIX_HALF, tm), lambda i: (0, i)), seq,
                   pl.BlockSpec((tm, N_GATES), row), pl.BlockSpec((N_GATES, tm), lambda i: (0, i))],
        out_shape=[f32_out, bf16_out, f32_out, bf16_out,
                   jax.ShapeDtypeStruct((MIX_HALF, rows), BF16), bf16_out,
                   jax.ShapeDtypeStruct((rows, N_GATES), F32),
                   jax.ShapeDtypeStruct((N_GATES, rows), F32)],
        scratch_shapes=[pltpu.VMEM((D_MODEL, n_main), BF16),
                        pltpu.VMEM((2, W_CHUNK, D_MODEL), F32),
                        pltpu.SemaphoreType.DMA((2,))],
        compiler_params=_cparams(("arbitrary",)),
        name="inproj",
    )(x2d, x2d, x2d, ctx2d, mod, g, w_in_t, w_gate, gate_bias_row, wcq, wck)


def _mlstm_kernel(*refs, need_out):
    if need_out:
        (qf, kf, ktf, vf, colf, rowf, qb, kb, ktb, vb, colb, rowb, s0, n0, m0,
         hf_o, hb_o, s_s, n_s, m_s) = refs
        q_refs, h_outs = (qf, qb), (hf_o, hb_o)
    else:
        (kf, ktf, vf, colf, rowf, kb, ktb, vb, colb, rowb, s0, n0, m0,
         s_o, n_o, m_o, s_s, n_s, m_s) = refs
    k_refs, kt_refs, v_refs = (kf, kb), (ktf, ktb), (vf, vb)
    col_refs, row_refs = (colf, colb), (rowf, rowb)
    j = pl.program_id(0)

    @pl.when(j == 0)
    def _():
        s_s[...] = s0[...]
        n_s[...] = n0[...]
        m_s[...] = m0[...]

    L = CHUNK
    row = lax.broadcasted_iota(jnp.int32, (L, L), 0)
    col = lax.broadcasted_iota(jnp.int32, (L, L), 1)

    heads = [(d, h) for d in range(2) for h in range(HEADS)]

    def head_values(d, h):
        hd = d * HEADS + h
        lc, lb = 8 * d + h, 8 * d + 4 + h
        hs = slice(h * HEAD_DIM, (h + 1) * HEAD_DIM)
        edge = L - 1 if d == 0 else 0
        colv = col_refs[d][...]
        rowv = row_refs[d][...]
        v = dict(hd=hd, hs=hs, d=d)
        v["cm_c"], v["b_c"] = colv[:, lc:lc + 1], colv[:, lb:lb + 1]
        v["c_r"] = rowv[lc:lc + 1, :]
        v["g"] = v["b_c"][edge:edge + 1, :]
        v["m_old"] = m_s[hd][:, 0:1]
        v["s_old"] = s_s[hd]
        v["n_old"] = n_s[hd]
        v["m_x"] = jnp.maximum(v["m_old"], v["cm_c"][edge:edge + 1, :])
        v["decay"] = jnp.exp(v["m_old"] - v["m_x"])
        v["wk_r"] = jnp.exp(v["c_r"] - v["m_x"])
        v["v_h"] = v_refs[d][:, hs]
        v["kt_h"] = kt_refs[d][hs, :]
        return v

    def update_state(v):
        hd = v["hd"]
        n_s[hd] = v["decay"] * v["n_old"] + _dot(
            jnp.broadcast_to(v["wk_r"], (8, L)).astype(BF16), k_refs[v["d"]][:, v["hs"]])[0:1, :]
        kwt = (v["kt_h"].astype(F32) * v["wk_r"]).astype(BF16)
        s_s[hd] = v["decay"] * v["s_old"] + _dot(kwt, v["v_h"])
        m_s[hd] = jnp.broadcast_to(v["g"] + v["m_x"], (1, LANES))

    def weights(v):
        d, hs = v["d"], v["hs"]
        mask = (row >= col) if d == 0 else (row <= col)
        q_h = q_refs[d][:, hs]
        m_c = jnp.maximum(v["m_old"], v["cm_c"])
        w_inter = jnp.exp(v["m_old"] - m_c)
        v["e"] = jnp.where(mask, jnp.exp(v["c_r"] - m_c), 0.0)
        v["qw"] = (q_h.astype(F32) * w_inter).astype(BF16)
        qn = lax.dot_general(q_h, jnp.broadcast_to(v["n_old"], (8, HEAD_DIM)).astype(BF16),
                             (((1,), (1,)), ((), ())), preferred_element_type=F32)
        v["den_inter"] = w_inter * qn[:, 0:1]
        v["floor"] = jnp.exp(-(v["b_c"] + m_c))
        v["s_bf"] = v["s_old"].astype(BF16)

    def outputs(v, qk):
        p = (v["e"] * qk).astype(BF16)
        num = _dot(v["qw"], v["s_bf"]) + _dot(p, v["v_h"])
        p_sum = lax.dot_general(p, jnp.ones((8, L), BF16), (((1,), (1,)), ((), ())),
                                preferred_element_type=F32)
        den = v["den_inter"] + p_sum[:, 0:1]
        h_outs[v["d"]][:, v["hs"]] = num / jnp.maximum(jnp.abs(den), v["floor"])

    vals = [head_values(d, h) for d, h in heads]
    if need_out:
        qks = [_dot(q_refs[v["d"]][:, v["hs"]], v["kt_h"]) for v in vals]
        for v in vals:
            weights(v)
    for v in vals:
        update_state(v)
    if need_out:
        for v, qk in zip(vals, qks):
            outputs(v, qk)

    if not need_out:
        s_o[...] = s_s[...]
        n_o[...] = n_s[...]
        m_o[...] = m_s[...]


def _mlstm(q, k, kt, v, col, rowi, s0, n0, m0, need_out, first, nc):
    t = nc * CHUNK
    fwd = lambda j: (first + j, 0)
    bwd = lambda j: (first + nc - 1 - j, 0)
    fwd_t = lambda j: (0, first + j)
    bwd_t = lambda j: (0, first + nc - 1 - j)
    c3 = lambda j: (0, 0, 0)
    seq = lambda im: pl.BlockSpec((CHUNK, MIX_HALF), im)
    state_specs = [pl.BlockSpec(s0.shape, c3), pl.BlockSpec(n0.shape, c3), pl.BlockSpec(m0.shape, c3)]
    scratch = [pltpu.VMEM(s0.shape, F32), pltpu.VMEM(n0.shape, F32), pltpu.VMEM(m0.shape, F32)]

    def side(im, im_t):
        specs = ([seq(im)] if need_out else []) + [seq(im), pl.BlockSpec((MIX_HALF, CHUNK), im_t), seq(im)]
        return specs + [pl.BlockSpec((CHUNK, N_GATES), im), pl.BlockSpec((N_GATES, CHUNK), im_t)]

    in_specs = side(fwd, fwd_t) + side(bwd, bwd_t) + state_specs
    seq_in = ((q,) if need_out else ()) + (k, kt, v, col, rowi)
    args = seq_in + seq_in + (s0, n0, m0)
    if need_out:
        out_specs = [seq(fwd), seq(bwd)]
        out_shape = [jax.ShapeDtypeStruct((t, MIX_HALF), F32)] * 2
    else:
        out_specs = state_specs
        out_shape = [jax.ShapeDtypeStruct(a.shape, F32) for a in (s0, n0, m0)]
    return pl.pallas_call(
        functools.partial(_mlstm_kernel, need_out=need_out),
        grid=(nc,),
        in_specs=in_specs,
        out_specs=out_specs,
        out_shape=out_shape,
        scratch_shapes=scratch,
        compiler_params=_cparams(("arbitrary",)),
        name="mlstm_out" if need_out else "mlstm_state",
    )(*args)


POOL_PAD = 512
POOL_UNROLL = 4


def _pool_kernel(u_ref, w_ref, sc_ref, o_ref, pad_s, *, t):
    for gi, win in enumerate(POOL_WINDOWS):
        @pl.when(pl.program_id(0) == gi)
        def _():
            _pool_group(u_ref, w_ref, sc_ref, o_ref, pad_s, win=win, t=t)


def _pool_group(u_ref, w_ref, sc_ref, o_ref, pad_s, *, win, t):
    half = win // 2
    tile = 256
    zeros = jnp.zeros((POOL_PAD, POOL_GROUP), F32)
    pad_s[0:POOL_PAD, :] = zeros
    pad_s[POOL_PAD + t:POOL_PAD + t + POOL_PAD, :] = zeros

    def copy(r, carry):
        t0 = pl.multiple_of(r * tile, tile)
        pad_s[pl.ds(POOL_PAD + t0, tile), :] = u_ref[pl.ds(t0, tile), :]
        return carry

    lax.fori_loop(0, t // tile, copy, 0)

    row = lax.broadcasted_iota(jnp.int32, (tile, tile), 0)
    col = lax.broadcasted_iota(jnp.int32, (tile, tile), 1)
    same_row = (row >> GRID_SHIFT) == (col >> GRID_SHIFT)
    in_win = (col - row >= -half) & (col - row < half)
    band = jnp.where(same_row & in_win, 1.0, 0.0).astype(BF16)
    w = w_ref[...].astype(BF16)
    scale = sc_ref[...]
    n_rows = t // GRID_W

    tok0 = lax.broadcasted_iota(jnp.int32, (tile, POOL_GROUP), 0)
    gc = tok0 & (GRID_W - 1)
    inv_h = 1.0 / (jnp.minimum(gc + half, GRID_W) - jnp.maximum(gc - half, 0)).astype(F32)

    def body(r, carry):
        t0s = [pl.multiple_of((r * POOL_UNROLL + k) * tile, tile) for k in range(POOL_UNROLL)]
        pieces = []
        for t0 in t0s:
            acc = pad_s[pl.ds(POOL_PAD + t0 - GRID_W * half, tile), :]
            for dd in range(-half + 1, half):
                acc = acc + pad_s[pl.ds(POOL_PAD + t0 + GRID_W * dd, tile), :]
            gr = (t0 + tok0) >> GRID_SHIFT
            cnt_v = jnp.minimum(gr + half, n_rows) - jnp.maximum(gr - half, 0)
            pieces.append(_split2(acc / cnt_v.astype(F32)))
        means = [(_dot(band, hi) + _dot(band, lo)) * inv_h for hi, lo in pieces]
        diffs = [(m - pad_s[pl.ds(POOL_PAD + t0, tile), :]).astype(BF16) for m, t0 in zip(means, t0s)]
        for d, t0 in zip(diffs, t0s):
            o_ref[pl.ds(t0, tile), :] = (_dot(d, w) * scale).astype(BF16)
        return carry

    lax.fori_loop(0, t // (tile * POOL_UNROLL), body, 0)


def _pool(u_pool, w_pool, scale_row, t):
    return pl.pallas_call(
        functools.partial(_pool_kernel, t=t),
        grid=(len(POOL_WINDOWS),),
        in_specs=[pl.BlockSpec((t, POOL_GROUP), lambda g: (0, g)),
                  pl.BlockSpec((None, POOL_GROUP, POOL_GROUP), lambda g: (g, 0, 0)),
                  pl.BlockSpec((1, POOL_GROUP), lambda g: (0, g))],
        out_specs=pl.BlockSpec((t, POOL_GROUP), lambda g: (0, g)),
        out_shape=jax.ShapeDtypeStruct((t, MIX_HALF), BF16),
        scratch_shapes=[pltpu.VMEM((t + 2 * POOL_PAD, POOL_GROUP), F32)],
        compiler_params=_cparams(("arbitrary",)),
        name="pool_mix",
    )(u_pool, w_pool, scale_row)


OUTPROJ_SUB = 256


def _route(logits):
    lane = lax.broadcasted_iota(jnp.int32, logits.shape, 1).astype(F32)
    neg = -jnp.inf
    big = float(LANES)
    gl = jnp.where(lane < N_GROUPS, logits, neg)
    gmax = jnp.max(gl, axis=1, keepdims=True)
    gsel = jnp.min(jnp.where(gl == gmax, lane, big), axis=1, keepdims=True)
    p_grp = 1.0 / jnp.sum(jnp.exp(gl - gmax), axis=1, keepdims=True)
    lo = ROUTE_LANE0 + EXPERTS_PER_GROUP * gsel
    el = jnp.where((lane >= lo) & (lane < lo + EXPERTS_PER_GROUP), logits, neg)
    m1 = jnp.max(el, axis=1, keepdims=True)
    i1 = jnp.min(jnp.where(el == m1, lane, big), axis=1, keepdims=True)
    el2 = jnp.where(lane == i1, neg, el)
    m2 = jnp.max(el2, axis=1, keepdims=True)
    i2 = jnp.min(jnp.where(el2 == m2, lane, big), axis=1, keepdims=True)
    e2 = jnp.exp(m2 - m1)
    p1 = 1.0 / (1.0 + e2)
    p2 = e2 / (1.0 + e2)
    info = jnp.where(lane == 0.0, i1 - ROUTE_LANE0, 0.0)
    info = jnp.where(lane == 1.0, i2 - ROUTE_LANE0, info)
    info = jnp.where(lane == 2.0, p_grp * p1, info)
    return jnp.where(lane == 3.0, p_grp * p2, info)


def _outproj_kernel(p_ref, hf_ref, hb_ref, uo_ref, x_ref, wout_hbm, mod_ref,
                    hg_ref, n2g_ref, wrp_ref, wrh_ref, br_ref, h1_o, fn_o, info_o, w_s, stage, sem):
    @pl.when(pl.program_id(0) == 0)
    def _():
        _load_bf16(wout_hbm, w_s, stage, sem, n_chunks=w_s.shape[0] // stage.shape[1], transpose=False)

    def mixer_input(rs):
        h = hf_ref[rs, :] + hb_ref[rs, :]
        parts = []
        for hh in range(HEADS):
            hs = h[:, hh * HEAD_DIM:(hh + 1) * HEAD_DIM]
            mu = jnp.mean(hs, axis=-1, keepdims=True)
            ctr = hs - mu
            var = jnp.mean(ctr * ctr, axis=-1, keepdims=True)
            parts.append(ctr * lax.rsqrt(var + EPS))
        hn = jnp.concatenate(parts, axis=1) * hg_ref[...]
        m = (hn * jax.nn.sigmoid(uo_ref[rs, :])).astype(BF16)
        return jnp.concatenate([p_ref[rs, :], m], axis=1)

    def finish(rs, mix):
        h1 = x_ref[rs, :] + mod_ref[2:3, :] * mix
        h1_o[rs, :] = h1
        ms = jnp.mean(h1 * h1, axis=-1, keepdims=True)
        fn = h1 * lax.rsqrt(ms + EPS) * n2g_ref[...]
        fn = fn * (1.0 + mod_ref[4:5, :]) + mod_ref[3:4, :]
        fh, fl = _split2(fn)
        fn_o[rs, :] = fh
        logits = _split_dot(fh, fl, wrp_ref[...], wrh_ref[...]) + br_ref[...]
        info_o[rs, :] = _route(logits)

    tm = x_ref.shape[0]
    subs = [slice(r0, r0 + OUTPROJ_SUB) for r0 in range(0, tm, OUTPROJ_SUB)]
    w = w_s[...]
    mixes = [_dot(mixer_input(rs), w) for rs in subs]
    for rs, mix in zip(subs, mixes):
        finish(rs, mix)


def _outproj(p, hf, hb, uo, x2d, w_out, mod, head_g, norm2_g, wr_pack, wr_hi, b_route, tm):
    t = x2d.shape[0]
    const = lambda i: (0, 0)
    row = lambda i: (i, 0)
    in_specs = ([pl.BlockSpec((tm, MIX_HALF), row)] * 4
                + [pl.BlockSpec((tm, D_MODEL), row),
                   pl.BlockSpec(memory_space=pl.ANY),
                   pl.BlockSpec(mod.shape, const),
                   pl.BlockSpec((1, MIX_HALF), const),
                   pl.BlockSpec((1, D_MODEL), const),
                   pl.BlockSpec((D_MODEL, LANES), const),
                   pl.BlockSpec((D_MODEL, LANES), const),
                   pl.BlockSpec((1, LANES), const)])
    return pl.pallas_call(
        _outproj_kernel,
        grid=(t // tm,),
        in_specs=in_specs,
        out_specs=[pl.BlockSpec((tm, D_MODEL), row), pl.BlockSpec((tm, D_MODEL), row),
                   pl.BlockSpec((tm, LANES), row)],
        out_shape=[jax.ShapeDtypeStruct((t, D_MODEL), F32),
                   jax.ShapeDtypeStruct((t, D_MODEL), BF16),
                   jax.ShapeDtypeStruct((t, LANES), F32)],
        scratch_shapes=[pltpu.VMEM(w_out.shape, BF16),
                        pltpu.VMEM((2, W_CHUNK // 2, w_out.shape[1]), F32),
                        pltpu.SemaphoreType.DMA((2,))],
        compiler_params=_cparams(("arbitrary",)),
        name="outproj_route",
    )(p, hf, hb, uo, x2d, w_out, mod, head_g, norm2_g, wr_pack, wr_hi, b_route)


DISPATCH_BLOCK = 512
GRANULE = 16
GRANULE_SHIFT = 4
LOCAL_CAP = 1536
LOCAL_GRANULES = LOCAL_CAP // GRANULE
FREE_GRANULES = 2
EXPERT_TILE = 256
TILE_GRANULES = EXPERT_TILE // GRANULE
PLAN_UNROLL = 4
PLAN_SLACK = 8
GATHER_DEPTH = 3


def _dispatch_kernel(fn_ref, info_ref, xs_o, ws_o, pt_o, cnt_o):
    tb = DISPATCH_BLOCK
    info = info_ref[...]
    e1, e2 = info[:, 0:1], info[:, 1:2]
    w1c, w2c = info[:, 2:3], info[:, 3:4]
    lane = lax.broadcasted_iota(jnp.int32, (tb, LANES), 1).astype(F32)
    o1 = jnp.where(lane == e1, 1.0, 0.0)
    o2 = jnp.where(lane == e2, 1.0, 0.0)
    onehot = o1 + o2
    cnt = jnp.sum(onehot, axis=0, keepdims=True)
    gran = jnp.floor((cnt + (GRANULE - 1)) * (1.0 / GRANULE))
    a = lax.broadcasted_iota(jnp.int32, (LANES, LANES), 0)
    b = lax.broadcasted_iota(jnp.int32, (LANES, LANES), 1)
    upper = jnp.where(a < b, 1.0, 0.0).astype(BF16)
    seg_off = _dot(jnp.broadcast_to(gran, (8, LANES)).astype(BF16), upper)[0:1, :] * GRANULE
    r = lax.broadcasted_iota(jnp.int32, (tb, tb), 0)
    c = lax.broadcasted_iota(jnp.int32, (tb, tb), 1)
    strict = jnp.where(r > c, 1.0, 0.0).astype(BF16)
    rank = _dot(strict, onehot.astype(BF16))
    slot = rank + seg_off
    pos1 = jnp.sum(o1 * slot, axis=1, keepdims=True)
    pos2 = jnp.sum(o2 * slot, axis=1, keepdims=True)
    rows = lax.broadcasted_iota(jnp.int32, (tb, LOCAL_CAP), 1).astype(F32)
    pt1 = jnp.where(rows == pos1, 1.0, 0.0)
    pt2 = jnp.where(rows == pos2, 1.0, 0.0)
    pt = pt1 + pt2
    pt_o[...] = pt.astype(BF16)
    perm = pt.T.astype(BF16)
    w_slot = jnp.sum((pt1 * w1c + pt2 * w2c).T, axis=1, keepdims=True)
    used = jnp.sum(gran).astype(jnp.int32) * GRANULE
    fn = fn_ref[...]
    for r0 in range(0, LOCAL_CAP, 256):
        rows_r = slice(r0, r0 + 256)
        if r0 < 2 * tb:
            xs_o[rows_r, :] = _dot(perm[rows_r], fn).astype(BF16)
        else:
            @pl.when(used > r0)
            def _():
                xs_o[rows_r, :] = _dot(perm[rows_r], fn).astype(BF16)

            @pl.when(used <= r0)
            def _():
                xs_o[rows_r, :] = jnp.zeros((256, D_MODEL), BF16)
    ws_o[...] = jnp.broadcast_to(w_slot, (LOCAL_CAP, LANES))
    cnt_o[...] = jnp.broadcast_to(cnt, (8, LANES)).astype(jnp.int32)


def _dispatch(fn, info):
    t = fn.shape[0]
    nb = t // DISPATCH_BLOCK
    row = lambda i: (i, 0)
    return pl.pallas_call(
        _dispatch_kernel,
        grid=(nb,),
        in_specs=[pl.BlockSpec((DISPATCH_BLOCK, D_MODEL), row),
                  pl.BlockSpec((DISPATCH_BLOCK, LANES), row)],
        out_specs=[pl.BlockSpec((LOCAL_CAP, D_MODEL), row),
                   pl.BlockSpec((LOCAL_CAP, LANES), row),
                   pl.BlockSpec((DISPATCH_BLOCK, LOCAL_CAP), row),
                   pl.BlockSpec((8, LANES), row)],
        out_shape=[jax.ShapeDtypeStruct((nb * LOCAL_CAP, D_MODEL), BF16),
                   jax.ShapeDtypeStruct((nb * LOCAL_CAP, LANES), F32),
                   jax.ShapeDtypeStruct((t, LOCAL_CAP), BF16),
                   jax.ShapeDtypeStruct((nb * 8, LANES), jnp.int32)],
        compiler_params=_cparams(("arbitrary",)),
        name="moe_dispatch",
    )(fn, info)


def _map_len(max_tiles):
    return (max_tiles + GATHER_DEPTH - 1) * TILE_GRANULES + PLAN_SLACK


def _free_granule(q):
    return ((q // FREE_GRANULES) * LOCAL_GRANULES + (LOCAL_GRANULES - FREE_GRANULES)
            + (q % FREE_GRANULES))


def _plan_kernel(cnt_ref, gsrc_o, gdst_o, texp_o, tend_o, ntile_o, lrun, *, nb, max_tiles):
    def init(b, c):
        lrun[b] = 0
        return c

    lax.fori_loop(0, nb, init, 0)

    def per_expert(e, carry):
        g0, last_e = carry

        def per_block(b, g):
            k = (cnt_ref[b, e] + (GRANULE - 1)) >> GRANULE_SHIFT
            lo = lrun[b]
            lrun[b] = lo + k

            base = b * LOCAL_GRANULES + lo

            for j in range(PLAN_UNROLL):
                gsrc_o[g + j] = base + j
                gdst_o[g + j] = base + j

            @pl.when(k > PLAN_UNROLL)
            def _():
                def put(j, c):
                    gsrc_o[g + j] = base + j
                    gdst_o[g + j] = base + j
                    return c

                lax.fori_loop(PLAN_UNROLL, k, put, 0)

            return g + k

        g1 = lax.fori_loop(0, nb, per_block, g0)
        pad = (-g1) & (TILE_GRANULES - 1)

        def put_pad(j, c):
            g = g1 + j
            parity = (g // TILE_GRANULES) & 1
            gsrc_o[g] = _free_granule(0)
            gdst_o[g] = _free_granule(1 + parity * (TILE_GRANULES - 1) + (g & (TILE_GRANULES - 1)))
            return c

        lax.fori_loop(0, pad, put_pad, 0)
        g2 = g1 + pad

        def put_tile(tt, c):
            texp_o[tt] = e
            return c

        lax.fori_loop(g0 // TILE_GRANULES, g2 // TILE_GRANULES, put_tile, 0)
        tend_o[e] = g2 // TILE_GRANULES
        return g2, jnp.where(g2 > g0, e, last_e)

    g_end, last_e = lax.fori_loop(0, N_EXPERTS, per_expert, (0, 0))
    n_tiles = g_end // TILE_GRANULES
    ntile_o[0] = n_tiles

    def fill(tt, c):
        texp_o[tt] = last_e
        return c

    lax.fori_loop(n_tiles, max_tiles, fill, 0)

    def fill_map(g, c):
        gsrc_o[g] = _free_granule(0)
        gdst_o[g] = _free_granule(0)
        return c

    lax.fori_loop(g_end, _map_len(max_tiles), fill_map, 0)


def _max_tiles(t):
    nb = t // DISPATCH_BLOCK
    worst_rows = 2 * t + nb * N_EXPERTS * (GRANULE - 1) + N_EXPERTS * (EXPERT_TILE - GRANULE)
    return -(-worst_rows // EXPERT_TILE)


def _plan(cnt, t):
    nb = cnt.shape[0]
    assert nb * FREE_GRANULES >= 2 + 2 * (TILE_GRANULES - 1)
    max_tiles = _max_tiles(t)
    smem = pl.BlockSpec(memory_space=pltpu.SMEM)
    n_map = _map_len(max_tiles)
    return pl.pallas_call(
        functools.partial(_plan_kernel, nb=nb, max_tiles=max_tiles),
        in_specs=[smem],
        out_specs=[smem, smem, smem, smem, smem],
        out_shape=[jax.ShapeDtypeStruct((n_map,), jnp.int32),
                   jax.ShapeDtypeStruct((n_map,), jnp.int32),
                   jax.ShapeDtypeStruct((max_tiles,), jnp.int32),
                   jax.ShapeDtypeStruct((N_EXPERTS,), jnp.int32),
                   jax.ShapeDtypeStruct((1,), jnp.int32)],
        scratch_shapes=[pltpu.SMEM((nb,), jnp.int32)],
        name="moe_plan",
    )(cnt)


def _experts_kernel(gsrc, gdst, texp, tend, ntile, xy_in, ws_in, w1_hbm, w3_hbm, w2_hbm, xy_out,
                    xbuf, wsbuf, ybuf, st1, st3, st2, gsem, ssem, wsem):
    nt = ntile[0]

    def weight_copies(e, ws):
        return (pltpu.make_async_copy(w1_hbm.at[e], st1.at[ws], wsem.at[ws]),
                pltpu.make_async_copy(w3_hbm.at[e], st3.at[ws], wsem.at[ws]),
                pltpu.make_async_copy(w2_hbm.at[e], st2.at[ws], wsem.at[ws]))

    def rows(i):
        return pl.ds(pl.multiple_of(i * GRANULE, GRANULE), GRANULE)

    def gather_copies(g, j, sl):
        return (pltpu.make_async_copy(xy_in.at[rows(g), :], xbuf.at[sl, rows(j), :], gsem.at[sl]),
                pltpu.make_async_copy(ws_in.at[rows(g), :], wsbuf.at[sl, rows(j), :], gsem.at[sl]))

    def scatter_copies(g, j, sl):
        return (pltpu.make_async_copy(ybuf.at[sl, rows(j), :], xy_out.at[rows(g), :], ssem.at[sl]),)

    def issue(tt, sl, gmap, copies):
        for j in range(TILE_GRANULES):
            for cp in copies(gmap[tt * TILE_GRANULES + j], j, sl):
                cp.start()

    def drain(sl, copies):
        full = pl.ds(0, EXPERT_TILE)
        if copies is gather_copies:
            pltpu.make_async_copy(xy_in.at[full, :], xbuf.at[sl], gsem.at[sl]).wait()
            pltpu.make_async_copy(ws_in.at[full, :], wsbuf.at[sl], gsem.at[sl]).wait()
        else:
            pltpu.make_async_copy(ybuf.at[sl], xy_out.at[full, :], ssem.at[sl]).wait()

    @pl.when(nt > 0)
    def _():
        last = nt - 1
        for cp in weight_copies(texp[0], 0):
            cp.start()
        for ahead in range(GATHER_DEPTH - 1):
            issue(ahead, ahead, gsrc, gather_copies)

        def tile(t, wslot):
            slot = t % 2
            gslot = t % GATHER_DEPTH
            e = texp[t]
            first = (t == 0) | (texp[jnp.maximum(t - 1, 0)] != e)
            wslot = jnp.where(first & (t > 0), 1 - wslot, wslot)

            drain(gslot, gather_copies)
            ahead = t + GATHER_DEPTH - 1
            issue(ahead, ahead % GATHER_DEPTH, gsrc, gather_copies)

            @pl.when(first)
            def _():
                for cp in weight_copies(e, wslot):
                    cp.wait()
                nxt = tend[e]

                @pl.when(nxt < nt)
                def _():
                    for cp in weight_copies(texp[jnp.minimum(nxt, last)], 1 - wslot):
                        cp.start()

            x = xbuf[gslot]
            w_row = wsbuf[gslot][:, 0:1]
            a = _dot(x, st1[wslot].astype(BF16))
            b = _dot(x, st3[wslot].astype(BF16))
            y = _dot((_silu(a) * b).astype(BF16), st2[wslot].astype(BF16)) * w_row

            @pl.when(t >= 2)
            def _():
                drain(slot, scatter_copies)

            ybuf[slot] = y.astype(BF16)
            issue(t, slot, gdst, scatter_copies)
            return wslot

        lax.fori_loop(0, nt, tile, 0)

        for k in range(GATHER_DEPTH - 1):
            drain((nt + k) % GATHER_DEPTH, gather_copies)
        drain(last % 2, scatter_copies)

        @pl.when(nt >= 2)
        def _():
            drain(nt % 2, scatter_copies)


def _experts(gsrc, gdst, texp, tend, ntile, xy, ws, w1, w3, w2):
    smem = pl.BlockSpec(memory_space=pltpu.SMEM)
    hbm = pl.BlockSpec(memory_space=pl.ANY)
    return pl.pallas_call(
        _experts_kernel,
        in_specs=[smem, smem, smem, smem, smem, hbm, hbm, hbm, hbm, hbm],
        out_specs=hbm,
        out_shape=jax.ShapeDtypeStruct(xy.shape, xy.dtype),
        scratch_shapes=[pltpu.VMEM((GATHER_DEPTH, EXPERT_TILE, D_MODEL), BF16),
                        pltpu.VMEM((GATHER_DEPTH, EXPERT_TILE, LANES), F32),
                        pltpu.VMEM((2, EXPERT_TILE, D_MODEL), BF16),
                        pltpu.VMEM((2, D_MODEL, D_EXPERT), F32),
                        pltpu.VMEM((2, D_MODEL, D_EXPERT), F32),
                        pltpu.VMEM((2, D_EXPERT, D_MODEL), F32),
                        pltpu.SemaphoreType.DMA((GATHER_DEPTH,)),
                        pltpu.SemaphoreType.DMA((2,)),
                        pltpu.SemaphoreType.DMA((2,))],
        input_output_aliases={5: 0},
        compiler_params=pltpu.CompilerParams(vmem_limit_bytes=VMEM_LIMIT),
        name="moe_experts",
    )(gsrc, gdst, texp, tend, ntile, xy, ws, w1, w3, w2)


def _combine_kernel(y_ref, pt_ref, h1_ref, g2_ref, fg_ref, o_ref):
    moe = _dot(pt_ref[...], y_ref[...])
    h = h1_ref[...] + g2_ref[...] * moe
    ms = jnp.mean(h * h, axis=-1, keepdims=True)
    o_ref[...] = h * lax.rsqrt(ms + EPS) * fg_ref[...]


def _combine(xy, pt, h1, g2, final_g):
    t = h1.shape[0]
    row = lambda i: (i, 0)
    const = lambda i: (0, 0)
    return pl.pallas_call(
        _combine_kernel,
        grid=(t // DISPATCH_BLOCK,),
        in_specs=[pl.BlockSpec((LOCAL_CAP, D_MODEL), row),
                  pl.BlockSpec((DISPATCH_BLOCK, LOCAL_CAP), row),
                  pl.BlockSpec((DISPATCH_BLOCK, D_MODEL), row),
                  pl.BlockSpec((1, D_MODEL), const), pl.BlockSpec((1, D_MODEL), const)],
        out_specs=pl.BlockSpec((DISPATCH_BLOCK, D_MODEL), row),
        out_shape=jax.ShapeDtypeStruct((t, D_MODEL), F32),
        compiler_params=_cparams(("arbitrary",)),
        name="moe_combine_final",
    )(xy, pt, h1, g2, final_g)


def _pad_lanes(a):
    return jnp.pad(a, ((0, 0), (0, LANES - a.shape[1])))


def kernel(x, c, ctx, c_ctx, w_mod, b_mod, norm1_g, w_in, w_conv_q, w_conv_k, gate_bias, head_norm_g, w_pool, pool_scale, w_out, norm2_g, w_group, b_group, w_router, b_router, w1, w3, w2, final_g):
    assert x.shape[0] == 1 and w_mod.shape[0] == 1
    seq = x.shape[1]
    x2d = x[0]
    ctx2d = ctx[0]

    n_main = 5 * MIX_HALF
    w_in_t = jnp.transpose(w_in[0])
    w_gate = _pad_lanes(jnp.transpose(_take_rows(w_in_t, n_main, N_GATES))).astype(BF16)
    gate_bias_row = _pad_lanes(gate_bias[0].reshape(1, N_GATES))
    wr_pack, wr_hi = _split_pack(jnp.concatenate([w_group[0], w_router[0]], axis=1))
    b_route = _pad_lanes(jnp.concatenate([b_group[0], b_router[0]]).reshape(1, -1))
    norm1 = norm1_g[0].reshape(1, D_MODEL)

    c16 = jnp.concatenate([c, c_ctx[None, :], jnp.zeros((14, D_MODEL), F32)], axis=0)
    mods = _adaln(c16, w_mod[0], b_mod[0].reshape(1, -1))
    mod_lat = mods[0].reshape(6, D_MODEL)
    mod_ctx = mods[1].reshape(6, D_MODEL)

    mod_in = jnp.concatenate([mod_ctx[0:2], mod_lat[0:2]], axis=0)
    u_pool, q, uo, k, kt, v, col, rowi = _inproj(x2d, ctx2d, mod_in, norm1, w_in_t, w_gate,
                                                 gate_bias_row, w_conv_q[0], w_conv_k[0])
    nc = seq // CHUNK
    zeros_state = (jnp.zeros((2 * HEADS, HEAD_DIM, HEAD_DIM), F32),
                   jnp.zeros((2 * HEADS, 1, HEAD_DIM), F32),
                   jnp.zeros((2 * HEADS, 1, LANES), F32))
    s0, n0, m0 = _mlstm(None, k, kt, v, col, rowi, *zeros_state, need_out=False, first=nc, nc=1)

    hf, hb = _mlstm(q, k, kt, v, col, rowi, s0, n0, m0, need_out=True, first=0, nc=nc)
    p = _pool(u_pool, w_pool[0], pool_scale[0].reshape(1, -1), seq)
    h1, fn, info = _outproj(p, hf, hb, uo, x2d, w_out[0], mod_lat, head_norm_g[0].reshape(1, -1),
                            norm2_g[0].reshape(1, -1), wr_pack, wr_hi, b_route, tm=2 * OUTPROJ_SUB)

    xs, ws, pt, cnt = _dispatch(fn, info)
    gsrc, gdst, texp, tend, ntile = _plan(cnt[::8, :N_EXPERTS], seq)
    xy = _experts(gsrc, gdst, texp, tend, ntile, xs, ws, w1[0], w3[0], w2[0])
    out = _combine(xy, pt, h1, mod_lat[5:6], final_g.reshape(1, -1))
    return out.reshape(1, seq, D_MODEL)
```

```python
import functools

import jax
import jax.numpy as jnp
from jax import lax
from jax.experimental import pallas as pl
from jax.experimental.pallas import tpu as pltpu

F32 = jnp.float32
BF16 = jnp.bfloat16

D_MODEL = 2048
GRID_W = 64
GRID_SHIFT = 6
POOL_WINDOWS = (2, 4, 8, 16)
POOL_GROUP = 256
HEADS = 4
HEAD_DIM = 256
MIX_HALF = 1024
N_GATES = 16
N_GROUPS = 4
EXPERTS_PER_GROUP = 8
N_EXPERTS = 32
D_EXPERT = 512
EPS = 1e-6
LANES = 128
CHUNK = 256
ROUTE_LANE0 = N_GROUPS

VMEM_LIMIT = 60 * 1024 * 1024


def _cparams(sem, vmem=VMEM_LIMIT):
    return pltpu.CompilerParams(dimension_semantics=sem, vmem_limit_bytes=vmem)


def _split2(x):
    hi = x.astype(BF16)
    lo = (x - hi.astype(F32)).astype(BF16)
    return hi, lo


def _split3(x):
    hi = x.astype(BF16)
    r = x - hi.astype(F32)
    mid = r.astype(BF16)
    lo = (r - mid.astype(F32)).astype(BF16)
    return hi, mid, lo


def _dot(a, b):
    return jnp.dot(a, b, preferred_element_type=F32)


SPLIT_LANE = 64


def _split_pack(w):
    hi, lo = _split2(w)
    n = w.shape[1]
    gap = jnp.zeros((w.shape[0], SPLIT_LANE - n), BF16)
    rest = jnp.zeros((w.shape[0], LANES - n), BF16)
    return jnp.concatenate([hi, gap, lo, gap], axis=1), jnp.concatenate([hi, rest], axis=1)


def _split_dot(xh, xl, w_packed, w_hi):
    r = _dot(xh, w_packed)
    return r + pltpu.roll(r, SPLIT_LANE, 1) + _dot(xl, w_hi)


def _silu(x):
    return x * jax.nn.sigmoid(x)


def _log_sigmoid(x):
    return jnp.minimum(x, 0.0) - jnp.log(1.0 + jnp.exp(-jnp.abs(x)))


def _copy_kernel(w_ref, o_ref):
    o_ref[...] = w_ref[...]


def _take_rows(w, start, n):
    return pl.pallas_call(
        _copy_kernel,
        grid=(1,),
        in_specs=[pl.BlockSpec((n, w.shape[1]), lambda i: (start // n, 0))],
        out_specs=pl.BlockSpec((n, w.shape[1]), lambda i: (0, 0)),
        out_shape=jax.ShapeDtypeStruct((n, w.shape[1]), w.dtype),
        name="take_rows",
    )(w)


def _adaln_kernel(c_ref, w_ref, b_ref, o_ref):
    a = _silu(c_ref[...])
    a3 = jnp.concatenate(_split3(a), axis=0)
    w_hi, w_lo = _split2(w_ref[...])
    acc = _dot(a3, w_hi)
    acc_lo = _dot(a3[:32], w_lo)
    out = acc[0:16] + acc[16:32] + acc[32:48] + acc_lo[0:16] + acc_lo[16:32]
    o_ref[...] = out + b_ref[...]


def _adaln(c16, w_mod, b_mod):
    n = w_mod.shape[1]
    tn = 1536
    return pl.pallas_call(
        _adaln_kernel,
        grid=(n // tn,),
        in_specs=[pl.BlockSpec((16, D_MODEL), lambda j: (0, 0)),
                  pl.BlockSpec((D_MODEL, tn), lambda j: (0, j)),
                  pl.BlockSpec((1, tn), lambda j: (0, j))],
        out_specs=pl.BlockSpec((16, tn), lambda j: (0, j)),
        out_shape=jax.ShapeDtypeStruct((16, n), F32),
        compiler_params=_cparams(("arbitrary",)),
        name="adaln",
    )(c16, w_mod, b_mod)


HALO = 8
NEG_INF = float("-inf")


def _gate_scan_info(gates):
    n = gates.shape[0]
    lane = lax.broadcasted_iota(jnp.int32, gates.shape, 1)
    rows = lax.broadcasted_iota(jnp.int32, gates.shape, 0)
    lf = jnp.where(lane < N_GATES, _log_sigmoid(gates), 0.0)
    hi = lf.astype(BF16).astype(F32)
    rem = lf - hi
    mid = rem.astype(BF16).astype(F32)
    packed = (hi + pltpu.roll(mid, 32, 1) + pltpu.roll(rem - mid, 64, 1)).astype(BF16)
    r = lax.broadcasted_iota(jnp.int32, (n, n), 0)
    c = lax.broadcasted_iota(jnp.int32, (n, n), 1)
    pf = _dot(jnp.where(r >= c, 1.0, 0.0).astype(BF16), packed)
    pb = _dot(jnp.where(r <= c, 1.0, 0.0).astype(BF16), packed)
    bf = pf + pltpu.roll(pf, 96, 1) + pltpu.roll(pf, 64, 1)
    bb = pb + pltpu.roll(pb, 96, 1) + pltpu.roll(pb, 64, 1)
    b = jnp.where(lane < 8, bf, bb)
    cval = gates - pltpu.roll(b, LANES - 4, 1)
    pm = cval
    sm = cval
    step = 1
    while step < n:
        pm = jnp.maximum(pm, jnp.where(rows >= step, pltpu.roll(pm, step, 0), NEG_INF))
        sm = jnp.maximum(sm, jnp.where(rows < n - step, pltpu.roll(sm, n - step, 0), NEG_INF))
        step *= 2
    cm = jnp.where(lane < 8, pm, sm)
    return cval, jnp.where((lane & 4) == 0, cm, b)


W_CHUNK = 256


def _load_bf16(src_hbm, dst, stage, sem, *, n_chunks, transpose):
    rows = stage.shape[1]
    def chunk_copy(c, sl):
        return pltpu.make_async_copy(
            src_hbm.at[pl.ds(pl.multiple_of(c * rows, rows), rows), :], stage.at[sl], sem.at[sl])

    chunk_copy(0, 0).start()

    def convert(c, carry):
        sl = c % 2

        @pl.when(c + 1 < n_chunks)
        def _():
            chunk_copy(c + 1, 1 - sl).start()

        chunk_copy(c, sl).wait()
        span = pl.ds(pl.multiple_of(c * rows, rows), rows)
        if transpose:
            dst[:, span] = stage[sl].T.astype(BF16)
        else:
            dst[span, :] = stage[sl].astype(BF16)
        return carry

    lax.fori_loop(0, n_chunks, convert, 0)


def _inproj_kernel(x_ref, xp_ref, xn_ref, ctx_ref, mod_ref, g_ref, wt_hbm, wg_ref, gb_ref,
                   wcq_ref, wck_ref, pool_o, q_o, o_o, k_o, kt_o, v_o, col_o, row_o,
                   w_s, stage, sem, *, nt):
    i = pl.program_id(0)
    tm = x_ref.shape[0]

    @pl.when(i == 0)
    def _():
        _load_bf16(wt_hbm, w_s, stage, sem, n_chunks=w_s.shape[1] // W_CHUNK, transpose=True)

    is_ctx = i == nt
    x_main = jnp.where(is_ctx, ctx_ref[...], x_ref[...])
    x_all = jnp.concatenate([xp_ref[...], x_main, xn_ref[...]], axis=0)
    shift = jnp.where(is_ctx, mod_ref[0:1, :], mod_ref[2:3, :])
    scale = jnp.where(is_ctx, mod_ref[1:2, :], mod_ref[3:4, :])
    ms = jnp.mean(x_all * x_all, axis=-1, keepdims=True)
    y = x_all * lax.rsqrt(ms + EPS) * g_ref[...]
    xn_all = y * (1.0 + scale) + shift
    xh_all = xn_all.astype(BF16)
    xh = xn_all[HALO:HALO + tm].astype(BF16)

    def cols(ci):
        return w_s[:, ci * MIX_HALF:(ci + 1) * MIX_HALF]

    u_q = _dot(xh_all, cols(1))
    u_k = _dot(xh_all, cols(3))
    pool_o[...] = _dot(xh, cols(0))
    o_o[...] = _dot(xh, cols(2))
    v_o[...] = _dot(xh, cols(4)).astype(BF16)

    keep_prev = jnp.where(jnp.logical_or(i == 0, is_ctx), 0.0, 1.0)
    keep_next = jnp.where(jnp.logical_or(i == nt - 1, is_ctx), 0.0, 1.0)

    def conv_silu(u, wc_ref):
        n = u.shape[0]
        u = jnp.concatenate([u[:HALO] * keep_prev, u[HALO:HALO + tm], u[HALO + tm:] * keep_next], axis=0)
        up = pltpu.roll(u, 1, 0)[HALO:HALO + tm]
        un = pltpu.roll(u, n - 1, 0)[HALO:HALO + tm]
        return _silu(wc_ref[0:1, :] * up + wc_ref[1:2, :] * u[HALO:HALO + tm] + wc_ref[2:3, :] * un)

    q_o[...] = (conv_silu(u_q, wcq_ref) * (HEAD_DIM ** -0.5)).astype(BF16)
    k = conv_silu(u_k, wck_ref)
    k_o[...] = k.astype(BF16)
    kt_o[...] = k.astype(BF16).T
    gates = _dot(xh, wg_ref[...]) + gb_ref[...]
    cval, col = _gate_scan_info(gates)
    col_o[...] = col[:, :N_GATES]
    row_o[...] = cval.T[:N_GATES, :]


def _inproj(x2d, ctx2d, mod, g, w_in_t, w_gate, gate_bias_row, wcq, wck):
    t = x2d.shape[0]
    tm = CHUNK
    assert ctx2d.shape[0] == tm
    nt = t // tm
    r8 = tm // HALO
    last8 = t // HALO - 1
    n_main = 5 * MIX_HALF
    rows = t + tm
    const = lambda i: (0, 0)
    row = lambda i: (i, 0)
    f32_out = jax.ShapeDtypeStruct((rows, MIX_HALF), F32)
    bf16_out = jax.ShapeDtypeStruct((rows, MIX_HALF), BF16)
    seq = pl.BlockSpec((tm, MIX_HALF), row)
    return pl.pallas_call(
        functools.partial(_inproj_kernel, nt=nt),
        grid=(nt + 1,),
        in_specs=[pl.BlockSpec((tm, D_MODEL), lambda i: (jnp.minimum(i, nt - 1), 0)),
                  pl.BlockSpec((HALO, D_MODEL), lambda i: (jnp.clip(i * r8 - 1, 0, last8), 0)),
                  pl.BlockSpec((HALO, D_MODEL), lambda i: (jnp.minimum((i + 1) * r8, last8), 0)),
                  pl.BlockSpec((tm, D_MODEL), const),
                  pl.BlockSpec((4, D_MODEL), const),
                  pl.BlockSpec((1, D_MODEL), const),
                  pl.BlockSpec(memory_space=pl.ANY),
                  pl.BlockSpec((D_MODEL, LANES), const),
                  pl.BlockSpec((1, LANES), const),
                  pl.BlockSpec((3, MIX_HALF), const),
                  pl.BlockSpec((3, MIX_HALF), const)],
        out_specs=[seq, seq, seq, seq, pl.BlockSpec((MIX_HALF, tm), lambda i: (0, i)), seq,
                   pl.BlockSpec((tm, N_GATES), row), pl.BlockSpec((N_GATES, tm), lambda i: (0, i))],
        out_shape=[f32_out, bf16_out, f32_out, bf16_out,
                   jax.ShapeDtypeStruct((MIX_HALF, rows), BF16), bf16_out,
                   jax.ShapeDtypeStruct((rows, N_GATES), F32),
                   jax.ShapeDtypeStruct((N_GATES, rows), F32)],
        scratch_shapes=[pltpu.VMEM((D_MODEL, n_main), BF16),
                        pltpu.VMEM((2, W_CHUNK, D_MODEL), F32),
                        pltpu.SemaphoreType.DMA((2,))],
        compiler_params=_cparams(("arbitrary",)),
        name="inproj",
    )(x2d, x2d, x2d, ctx2d, mod, g, w_in_t, w_gate, gate_bias_row, wcq, wck)


def _mlstm_kernel(*refs, need_out):
    if need_out:
        (qf, kf, ktf, vf, colf, rowf, qb, kb, ktb, vb, colb, rowb, s0, n0, m0,
         hf_o, hb_o, s_s, n_s, m_s) = refs
        q_refs, h_outs = (qf, qb), (hf_o, hb_o)
    else:
        (kf, ktf, vf, colf, rowf, kb, ktb, vb, colb, rowb, s0, n0, m0,
         s_o, n_o, m_o, s_s, n_s, m_s) = refs
    k_refs, kt_refs, v_refs = (kf, kb), (ktf, ktb), (vf, vb)
    col_refs, row_refs = (colf, colb), (rowf, rowb)
    j = pl.program_id(0)

    @pl.when(j == 0)
    def _():
        s_s[...] = s0[...]
        n_s[...] = n0[...]
        m_s[...] = m0[...]

    L = CHUNK
    row = lax.broadcasted_iota(jnp.int32, (L, L), 0)
    col = lax.broadcasted_iota(jnp.int32, (L, L), 1)

    heads = [(d, h) for d in range(2) for h in range(HEADS)]

    def head_values(d, h):
        hd = d * HEADS + h
        lc, lb = 8 * d + h, 8 * d + 4 + h
        hs = slice(h * HEAD_DIM, (h + 1) * HEAD_DIM)
        edge = L - 1 if d == 0 else 0
        colv = col_refs[d][...]
        rowv = row_refs[d][...]
        v = dict(hd=hd, hs=hs, d=d)
        v["cm_c"], v["b_c"] = colv[:, lc:lc + 1], colv[:, lb:lb + 1]
        v["c_r"] = rowv[lc:lc + 1, :]
        v["g"] = v["b_c"][edge:edge + 1, :]
        v["m_old"] = m_s[hd][:, 0:1]
        v["s_old"] = s_s[hd]
        v["n_old"] = n_s[hd]
        v["m_x"] = jnp.maximum(v["m_old"], v["cm_c"][edge:edge + 1, :])
        v["decay"] = jnp.exp(v["m_old"] - v["m_x"])
        v["wk_r"] = jnp.exp(v["c_r"] - v["m_x"])
        v["v_h"] = v_refs[d][:, hs]
        v["kt_h"] = kt_refs[d][hs, :]
        return v

    def update_state(v):
        hd = v["hd"]
        n_s[hd] = v["decay"] * v["n_old"] + _dot(
            jnp.broadcast_to(v["wk_r"], (8, L)).astype(BF16), k_refs[v["d"]][:, v["hs"]])[0:1, :]
        kwt = (v["kt_h"].astype(F32) * v["wk_r"]).astype(BF16)
        s_s[hd] = v["decay"] * v["s_old"] + _dot(kwt, v["v_h"])
        m_s[hd] = jnp.broadcast_to(v["g"] + v["m_x"], (1, LANES))

    def weights(v):
        d, hs = v["d"], v["hs"]
        mask = (row >= col) if d == 0 else (row <= col)
        q_h = q_refs[d][:, hs]
        m_c = jnp.maximum(v["m_old"], v["cm_c"])
        w_inter = jnp.exp(v["m_old"] - m_c)
        v["e"] = jnp.where(mask, jnp.exp(v["c_r"] - m_c), 0.0)
        v["qw"] = (q_h.astype(F32) * w_inter).astype(BF16)
        qn = lax.dot_general(q_h, jnp.broadcast_to(v["n_old"], (8, HEAD_DIM)).astype(BF16),
                             (((1,), (1,)), ((), ())), preferred_element_type=F32)
        v["den_inter"] = w_inter * qn[:, 0:1]
        v["floor"] = jnp.exp(-(v["b_c"] + m_c))
        v["s_bf"] = v["s_old"].astype(BF16)

    def outputs(v, qk):
        p = (v["e"] * qk).astype(BF16)
        num = _dot(v["qw"], v["s_bf"]) + _dot(p, v["v_h"])
        p_sum = lax.dot_general(p, jnp.ones((8, L), BF16), (((1,), (1,)), ((), ())),
                                preferred_element_type=F32)
        den = v["den_inter"] + p_sum[:, 0:1]
        h_outs[v["d"]][:, v["hs"]] = num / jnp.maximum(jnp.abs(den), v["floor"])

    vals = [head_values(d, h) for d, h in heads]
    if need_out:
        qks = [_dot(q_refs[v["d"]][:, v["hs"]], v["kt_h"]) for v in vals]
        for v in vals:
            weights(v)
    for v in vals:
        update_state(v)
    if need_out:
        for v, qk in zip(vals, qks):
            outputs(v, qk)

    if not need_out:
        s_o[...] = s_s[...]
        n_o[...] = n_s[...]
        m_o[...] = m_s[...]


def _mlstm(q, k, kt, v, col, rowi, s0, n0, m0, need_out, first, nc):
    t = nc * CHUNK
    fwd = lambda j: (first + j, 0)
    bwd = lambda j: (first + nc - 1 - j, 0)
    fwd_t = lambda j: (0, first + j)
    bwd_t = lambda j: (0, first + nc - 1 - j)
    c3 = lambda j: (0, 0, 0)
    seq = lambda im: pl.BlockSpec((CHUNK, MIX_HALF), im)
    state_specs = [pl.BlockSpec(s0.shape, c3), pl.BlockSpec(n0.shape, c3), pl.BlockSpec(m0.shape, c3)]
    scratch = [pltpu.VMEM(s0.shape, F32), pltpu.VMEM(n0.shape, F32), pltpu.VMEM(m0.shape, F32)]

    def side(im, im_t):
        specs = ([seq(im)] if need_out else []) + [seq(im), pl.BlockSpec((MIX_HALF, CHUNK), im_t), seq(im)]
        return specs + [pl.BlockSpec((CHUNK, N_GATES), im), pl.BlockSpec((N_GATES, CHUNK), im_t)]

    in_specs = side(fwd, fwd_t) + side(bwd, bwd_t) + state_specs
    seq_in = ((q,) if need_out else ()) + (k, kt, v, col, rowi)
    args = seq_in + seq_in + (s0, n0, m0)
    if need_out:
        out_specs = [seq(fwd), seq(bwd)]
        out_shape = [jax.ShapeDtypeStruct((t, MIX_HALF), F32)] * 2
    else:
        out_specs = state_specs
        out_shape = [jax.ShapeDtypeStruct(a.shape, F32) for a in (s0, n0, m0)]
    return pl.pallas_call(
        functools.partial(_mlstm_kernel, need_out=need_out),
        grid=(nc,),
        in_specs=in_specs,
        out_specs=out_specs,
        out_shape=out_shape,
        scratch_shapes=scratch,
        compiler_params=_cparams(("arbitrary",)),
        name="mlstm_out" if need_out else "mlstm_state",
    )(*args)


POOL_PAD = 512
POOL_UNROLL = 4


def _pool_kernel(u_ref, w_ref, sc_ref, o_ref, pad_s, *, t):
    for gi, win in enumerate(POOL_WINDOWS):
        @pl.when(pl.program_id(0) == gi)
        def _():
            _pool_group(u_ref, w_ref, sc_ref, o_ref, pad_s, win=win, t=t)


def _pool_group(u_ref, w_ref, sc_ref, o_ref, pad_s, *, win, t):
    half = win // 2
    tile = 256
    zeros = jnp.zeros((POOL_PAD, POOL_GROUP), F32)
    pad_s[0:POOL_PAD, :] = zeros
    pad_s[POOL_PAD + t:POOL_PAD + t + POOL_PAD, :] = zeros

    def copy(r, carry):
        t0 = pl.multiple_of(r * tile, tile)
        pad_s[pl.ds(POOL_PAD + t0, tile), :] = u_ref[pl.ds(t0, tile), :]
        return carry

    lax.fori_loop(0, t // tile, copy, 0)

    row = lax.broadcasted_iota(jnp.int32, (tile, tile), 0)
    col = lax.broadcasted_iota(jnp.int32, (tile, tile), 1)
    same_row = (row >> GRID_SHIFT) == (col >> GRID_SHIFT)
    in_win = (col - row >= -half) & (col - row < half)
    band = jnp.where(same_row & in_win, 1.0, 0.0).astype(BF16)
    w = w_ref[...].astype(BF16)
    scale = sc_ref[...]
    n_rows = t // GRID_W

    tok0 = lax.broadcasted_iota(jnp.int32, (tile, POOL_GROUP), 0)
    gc = tok0 & (GRID_W - 1)
    inv_h = 1.0 / (jnp.minimum(gc + half, GRID_W) - jnp.maximum(gc - half, 0)).astype(F32)

    def body(r, carry):
        t0s = [pl.multiple_of((r * POOL_UNROLL + k) * tile, tile) for k in range(POOL_UNROLL)]
        pieces = []
        for t0 in t0s:
            acc = pad_s[pl.ds(POOL_PAD + t0 - GRID_W * half, tile), :]
            for dd in range(-half + 1, half):
                acc = acc + pad_s[pl.ds(POOL_PAD + t0 + GRID_W * dd, tile), :]
            gr = (t0 + tok0) >> GRID_SHIFT
            cnt_v = jnp.minimum(gr + half, n_rows) - jnp.maximum(gr - half, 0)
            pieces.append(_split2(acc / cnt_v.astype(F32)))
        means = [(_dot(band, hi) + _dot(band, lo)) * inv_h for hi, lo in pieces]
        diffs = [(m - pad_s[pl.ds(POOL_PAD + t0, tile), :]).astype(BF16) for m, t0 in zip(means, t0s)]
        for d, t0 in zip(diffs, t0s):
            o_ref[pl.ds(t0, tile), :] = (_dot(d, w) * scale).astype(BF16)
        return carry

    lax.fori_loop(0, t // (tile * POOL_UNROLL), body, 0)


def _pool(u_pool, w_pool, scale_row, t):
    return pl.pallas_call(
        functools.partial(_pool_kernel, t=t),
        grid=(len(POOL_WINDOWS),),
        in_specs=[pl.BlockSpec((t, POOL_GROUP), lambda g: (0, g)),
                  pl.BlockSpec((None, POOL_GROUP, POOL_GROUP), lambda g: (g, 0, 0)),
                  pl.BlockSpec((1, POOL_GROUP), lambda g: (0, g))],
        out_specs=pl.BlockSpec((t, POOL_GROUP), lambda g: (0, g)),
        out_shape=jax.ShapeDtypeStruct((t, MIX_HALF), BF16),
        scratch_shapes=[pltpu.VMEM((t + 2 * POOL_PAD, POOL_GROUP), F32)],
        compiler_params=_cparams(("arbitrary",)),
        name="pool_mix",
    )(u_pool, w_pool, scale_row)


OUTPROJ_SUB = 256


def _route(logits):
    lane = lax.broadcasted_iota(jnp.int32, logits.shape, 1).astype(F32)
    neg = -jnp.inf
    big = float(LANES)
    gl = jnp.where(lane < N_GROUPS, logits, neg)
    gmax = jnp.max(gl, axis=1, keepdims=True)
    gsel = jnp.min(jnp.where(gl == gmax, lane, big), axis=1, keepdims=True)
    p_grp = 1.0 / jnp.sum(jnp.exp(gl - gmax), axis=1, keepdims=True)
    lo = ROUTE_LANE0 + EXPERTS_PER_GROUP * gsel
    el = jnp.where((lane >= lo) & (lane < lo + EXPERTS_PER_GROUP), logits, neg)
    m1 = jnp.max(el, axis=1, keepdims=True)
    i1 = jnp.min(jnp.where(el == m1, lane, big), axis=1, keepdims=True)
    el2 = jnp.where(lane == i1, neg, el)
    m2 = jnp.max(el2, axis=1, keepdims=True)
    i2 = jnp.min(jnp.where(el2 == m2, lane, big), axis=1, keepdims=True)
    e2 = jnp.exp(m2 - m1)
    p1 = 1.0 / (1.0 + e2)
    p2 = e2 / (1.0 + e2)
    info = jnp.where(lane == 0.0, i1 - ROUTE_LANE0, 0.0)
    info = jnp.where(lane == 1.0, i2 - ROUTE_LANE0, info)
    info = jnp.where(lane == 2.0, p_grp * p1, info)
    return jnp.where(lane == 3.0, p_grp * p2, info)


def _outproj_kernel(p_ref, hf_ref, hb_ref, uo_ref, x_ref, wout_hbm, mod_ref,
                    hg_ref, n2g_ref, wrp_ref, wrh_ref, br_ref, h1_o, fn_o, info_o, w_s, stage, sem):
    @pl.when(pl.program_id(0) == 0)
    def _():
        _load_bf16(wout_hbm, w_s, stage, sem, n_chunks=w_s.shape[0] // stage.shape[1], transpose=False)

    def mixer_input(rs):
        h = hf_ref[rs, :] + hb_ref[rs, :]
        parts = []
        for hh in range(HEADS):
            hs = h[:, hh * HEAD_DIM:(hh + 1) * HEAD_DIM]
            mu = jnp.mean(hs, axis=-1, keepdims=True)
            ctr = hs - mu
            var = jnp.mean(ctr * ctr, axis=-1, keepdims=True)
            parts.append(ctr * lax.rsqrt(var + EPS))
        hn = jnp.concatenate(parts, axis=1) * hg_ref[...]
        m = (hn * jax.nn.sigmoid(uo_ref[rs, :])).astype(BF16)
        return jnp.concatenate([p_ref[rs, :], m], axis=1)

    def finish(rs, mix):
        h1 = x_ref[rs, :] + mod_ref[2:3, :] * mix
        h1_o[rs, :] = h1
        ms = jnp.mean(h1 * h1, axis=-1, keepdims=True)
        fn = h1 * lax.rsqrt(ms + EPS) * n2g_ref[...]
        fn = fn * (1.0 + mod_ref[4:5, :]) + mod_ref[3:4, :]
        fh, fl = _split2(fn)
        fn_o[rs, :] = fh
        logits = _split_dot(fh, fl, wrp_ref[...], wrh_ref[...]) + br_ref[...]
        info_o[rs, :] = _route(logits)

    tm = x_ref.shape[0]
    subs = [slice(r0, r0 + OUTPROJ_SUB) for r0 in range(0, tm, OUTPROJ_SUB)]
    w = w_s[...]
    mixes = [_dot(mixer_input(rs), w) for rs in subs]
    for rs, mix in zip(subs, mixes):
        finish(rs, mix)


def _outproj(p, hf, hb, uo, x2d, w_out, mod, head_g, norm2_g, wr_pack, wr_hi, b_route, tm):
    t = x2d.shape[0]
    const = lambda i: (0, 0)
    row = lambda i: (i, 0)
    in_specs = ([pl.BlockSpec((tm, MIX_HALF), row)] * 4
                + [pl.BlockSpec((tm, D_MODEL), row),
                   pl.BlockSpec(memory_space=pl.ANY),
                   pl.BlockSpec(mod.shape, const),
                   pl.BlockSpec((1, MIX_HALF), const),
                   pl.BlockSpec((1, D_MODEL), const),
                   pl.BlockSpec((D_MODEL, LANES), const),
                   pl.BlockSpec((D_MODEL, LANES), const),
                   pl.BlockSpec((1, LANES), const)])
    return pl.pallas_call(
        _outproj_kernel,
        grid=(t // tm,),
        in_specs=in_specs,
        out_specs=[pl.BlockSpec((tm, D_MODEL), row), pl.BlockSpec((tm, D_MODEL), row),
                   pl.BlockSpec((tm, LANES), row)],
        out_shape=[jax.ShapeDtypeStruct((t, D_MODEL), F32),
                   jax.ShapeDtypeStruct((t, D_MODEL), BF16),
                   jax.ShapeDtypeStruct((t, LANES), F32)],
        scratch_shapes=[pltpu.VMEM(w_out.shape, BF16),
                        pltpu.VMEM((2, W_CHUNK // 2, w_out.shape[1]), F32),
                        pltpu.SemaphoreType.DMA((2,))],
        compiler_params=_cparams(("arbitrary",)),
        name="outproj_route",
    )(p, hf, hb, uo, x2d, w_out, mod, head_g, norm2_g, wr_pack, wr_hi, b_route)


DISPATCH_BLOCK = 512
GRANULE = 16
GRANULE_SHIFT = 4
LOCAL_CAP = 1536
LOCAL_GRANULES = LOCAL_CAP // GRANULE
FREE_GRANULES = 2
EXPERT_TILE = 256
TILE_GRANULES = EXPERT_TILE // GRANULE
PLAN_UNROLL = 4
PLAN_SLACK = 8
GATHER_DEPTH = 4


def _dispatch_kernel(fn_ref, info_ref, xs_o, ws_o, pt_o, cnt_o):
    tb = DISPATCH_BLOCK
    info = info_ref[...]
    e1, e2 = info[:, 0:1], info[:, 1:2]
    w1c, w2c = info[:, 2:3], info[:, 3:4]
    lane = lax.broadcasted_iota(jnp.int32, (tb, LANES), 1).astype(F32)
    o1 = jnp.where(lane == e1, 1.0, 0.0)
    o2 = jnp.where(lane == e2, 1.0, 0.0)
    onehot = o1 + o2
    cnt = jnp.sum(onehot, axis=0, keepdims=True)
    gran = jnp.floor((cnt + (GRANULE - 1)) * (1.0 / GRANULE))
    a = lax.broadcasted_iota(jnp.int32, (LANES, LANES), 0)
    b = lax.broadcasted_iota(jnp.int32, (LANES, LANES), 1)
    upper = jnp.where(a < b, 1.0, 0.0).astype(BF16)
    seg_off = _dot(jnp.broadcast_to(gran, (8, LANES)).astype(BF16), upper)[0:1, :] * GRANULE
    r = lax.broadcasted_iota(jnp.int32, (tb, tb), 0)
    c = lax.broadcasted_iota(jnp.int32, (tb, tb), 1)
    strict = jnp.where(r > c, 1.0, 0.0).astype(BF16)
    rank = _dot(strict, onehot.astype(BF16))
    slot = rank + seg_off
    pos1 = jnp.sum(o1 * slot, axis=1, keepdims=True)
    pos2 = jnp.sum(o2 * slot, axis=1, keepdims=True)
    rows = lax.broadcasted_iota(jnp.int32, (tb, LOCAL_CAP), 1).astype(F32)
    pt1 = jnp.where(rows == pos1, 1.0, 0.0)
    pt2 = jnp.where(rows == pos2, 1.0, 0.0)
    pt = pt1 + pt2
    pt_o[...] = pt.astype(BF16)
    perm = pt.T.astype(BF16)
    w_slot = jnp.sum((pt1 * w1c + pt2 * w2c).T, axis=1, keepdims=True)
    for c0 in range(0, D_MODEL, 512):
        xs_o[:, c0:c0 + 512] = _dot(perm, fn_ref[:, c0:c0 + 512]).astype(BF16)
    ws_o[...] = jnp.broadcast_to(w_slot, (LOCAL_CAP, LANES))
    cnt_o[...] = jnp.broadcast_to(cnt, (8, LANES)).astype(jnp.int32)


def _dispatch(fn, info):
    t = fn.shape[0]
    nb = t // DISPATCH_BLOCK
    row = lambda i: (i, 0)
    return pl.pallas_call(
        _dispatch_kernel,
        grid=(nb,),
        in_specs=[pl.BlockSpec((DISPATCH_BLOCK, D_MODEL), row),
                  pl.BlockSpec((DISPATCH_BLOCK, LANES), row)],
        out_specs=[pl.BlockSpec((LOCAL_CAP, D_MODEL), row),
                   pl.BlockSpec((LOCAL_CAP, LANES), row),
                   pl.BlockSpec((DISPATCH_BLOCK, LOCAL_CAP), row),
                   pl.BlockSpec((8, LANES), row)],
        out_shape=[jax.ShapeDtypeStruct((nb * LOCAL_CAP, D_MODEL), BF16),
                   jax.ShapeDtypeStruct((nb * LOCAL_CAP, LANES), F32),
                   jax.ShapeDtypeStruct((t, LOCAL_CAP), BF16),
                   jax.ShapeDtypeStruct((nb * 8, LANES), jnp.int32)],
        compiler_params=_cparams(("arbitrary",)),
        name="moe_dispatch",
    )(fn, info)


def _map_len(max_tiles):
    return (max_tiles + GATHER_DEPTH - 1) * TILE_GRANULES + PLAN_SLACK


def _free_granule(q):
    return ((q // FREE_GRANULES) * LOCAL_GRANULES + (LOCAL_GRANULES - FREE_GRANULES)
            + (q % FREE_GRANULES))


def _plan_kernel(cnt_ref, gsrc_o, gdst_o, texp_o, tend_o, ntile_o, lrun, *, nb, max_tiles):
    def init(b, c):
        lrun[b] = 0
        return c

    lax.fori_loop(0, nb, init, 0)

    def per_expert(e, carry):
        g0, last_e = carry

        def per_block(b, g):
            k = (cnt_ref[b, e] + (GRANULE - 1)) >> GRANULE_SHIFT
            lo = lrun[b]
            lrun[b] = lo + k

            base = b * LOCAL_GRANULES + lo

            for j in range(PLAN_UNROLL):
                gsrc_o[g + j] = base + j
                gdst_o[g + j] = base + j

            @pl.when(k > PLAN_UNROLL)
            def _():
                def put(j, c):
                    gsrc_o[g + j] = base + j
                    gdst_o[g + j] = base + j
                    return c

                lax.fori_loop(PLAN_UNROLL, k, put, 0)

            return g + k

        g1 = lax.fori_loop(0, nb, per_block, g0)
        pad = (-g1) & (TILE_GRANULES - 1)

        def put_pad(j, c):
            g = g1 + j
            parity = (g // TILE_GRANULES) & 1
            gsrc_o[g] = _free_granule(0)
            gdst_o[g] = _free_granule(1 + parity * (TILE_GRANULES - 1) + (g & (TILE_GRANULES - 1)))
            return c

        lax.fori_loop(0, pad, put_pad, 0)
        g2 = g1 + pad

        def put_tile(tt, c):
            texp_o[tt] = e
            return c

        lax.fori_loop(g0 // TILE_GRANULES, g2 // TILE_GRANULES, put_tile, 0)
        tend_o[e] = g2 // TILE_GRANULES
        return g2, jnp.where(g2 > g0, e, last_e)

    g_end, last_e = lax.fori_loop(0, N_EXPERTS, per_expert, (0, 0))
    n_tiles = g_end // TILE_GRANULES
    ntile_o[0] = n_tiles

    def fill(tt, c):
        texp_o[tt] = last_e
        return c

    lax.fori_loop(n_tiles, max_tiles, fill, 0)

    def fill_map(g, c):
        gsrc_o[g] = _free_granule(0)
        gdst_o[g] = _free_granule(0)
        return c

    lax.fori_loop(g_end, _map_len(max_tiles), fill_map, 0)


def _max_tiles(t):
    nb = t // DISPATCH_BLOCK
    worst_rows = 2 * t + nb * N_EXPERTS * (GRANULE - 1) + N_EXPERTS * (EXPERT_TILE - GRANULE)
    return -(-worst_rows // EXPERT_TILE)


def _plan(cnt, t):
    nb = cnt.shape[0]
    assert nb * FREE_GRANULES >= 2 + 2 * (TILE_GRANULES - 1)
    max_tiles = _max_tiles(t)
    smem = pl.BlockSpec(memory_space=pltpu.SMEM)
    n_map = _map_len(max_tiles)
    return pl.pallas_call(
        functools.partial(_plan_kernel, nb=nb, max_tiles=max_tiles),
        in_specs=[smem],
        out_specs=[smem, smem, smem, smem, smem],
        out_shape=[jax.ShapeDtypeStruct((n_map,), jnp.int32),
                   jax.ShapeDtypeStruct((n_map,), jnp.int32),
                   jax.ShapeDtypeStruct((max_tiles,), jnp.int32),
                   jax.ShapeDtypeStruct((N_EXPERTS,), jnp.int32),
                   jax.ShapeDtypeStruct((1,), jnp.int32)],
        scratch_shapes=[pltpu.SMEM((nb,), jnp.int32)],
        name="moe_plan",
    )(cnt)


def _experts_kernel(gsrc, gdst, texp, tend, ntile, xy_in, ws_in, w1_hbm, w3_hbm, w2_hbm, xy_out,
                    xbuf, wsbuf, ybuf, st1, st3, st2, gsem, ssem, wsem):
    nt = ntile[0]

    def weight_copies(e, ws):
        return (pltpu.make_async_copy(w1_hbm.at[e], st1.at[ws], wsem.at[ws]),
                pltpu.make_async_copy(w3_hbm.at[e], st3.at[ws], wsem.at[ws]),
                pltpu.make_async_copy(w2_hbm.at[e], st2.at[ws], wsem.at[ws]))

    def rows(i):
        return pl.ds(pl.multiple_of(i * GRANULE, GRANULE), GRANULE)

    def gather_copies(g, j, sl):
        return (pltpu.make_async_copy(xy_in.at[rows(g), :], xbuf.at[sl, rows(j), :], gsem.at[sl]),
                pltpu.make_async_copy(ws_in.at[rows(g), :], wsbuf.at[sl, rows(j), :], gsem.at[sl]))

    def scatter_copies(g, j, sl):
        return (pltpu.make_async_copy(ybuf.at[sl, rows(j), :], xy_out.at[rows(g), :], ssem.at[sl]),)

    def issue(tt, sl, gmap, copies):
        for j in range(TILE_GRANULES):
            for cp in copies(gmap[tt * TILE_GRANULES + j], j, sl):
                cp.start()

    def drain(sl, copies):
        full = pl.ds(0, EXPERT_TILE)
        if copies is gather_copies:
            pltpu.make_async_copy(xy_in.at[full, :], xbuf.at[sl], gsem.at[sl]).wait()
            pltpu.make_async_copy(ws_in.at[full, :], wsbuf.at[sl], gsem.at[sl]).wait()
        else:
            pltpu.make_async_copy(ybuf.at[sl], xy_out.at[full, :], ssem.at[sl]).wait()

    @pl.when(nt > 0)
    def _():
        last = nt - 1
        for cp in weight_copies(texp[0], 0):
            cp.start()
        for ahead in range(GATHER_DEPTH - 1):
            issue(ahead, ahead, gsrc, gather_copies)

        def tile(t, wslot):
            slot = t % 2
            gslot = t % GATHER_DEPTH
            e = texp[t]
            first = (t == 0) | (texp[jnp.maximum(t - 1, 0)] != e)
            wslot = jnp.where(first & (t > 0), 1 - wslot, wslot)

            drain(gslot, gather_copies)

            @pl.when(t >= 2)
            def _():
                drain(slot, scatter_copies)

            @pl.when(first)
            def _():
                for cp in weight_copies(e, wslot):
                    cp.wait()
                nxt = tend[e]

                @pl.when(nxt < nt)
                def _():
                    for cp in weight_copies(texp[jnp.minimum(nxt, last)], 1 - wslot):
                        cp.start()

            x = xbuf[gslot]
            w_row = wsbuf[gslot][:, 0:1]
            a = _dot(x, st1[wslot].astype(BF16))
            b = _dot(x, st3[wslot].astype(BF16))
            y = _dot((_silu(a) * b).astype(BF16), st2[wslot].astype(BF16)) * w_row
            ybuf[slot] = y.astype(BF16)
            issue(t, slot, gdst, scatter_copies)
            ahead = t + GATHER_DEPTH - 1
            issue(ahead, ahead % GATHER_DEPTH, gsrc, gather_copies)
            return wslot

        lax.fori_loop(0, nt, tile, 0)

        for k in range(GATHER_DEPTH - 1):
            drain((nt + k) % GATHER_DEPTH, gather_copies)
        drain(last % 2, scatter_copies)

        @pl.when(nt >= 2)
        def _():
            drain(nt % 2, scatter_copies)


def _experts(gsrc, gdst, texp, tend, ntile, xy, ws, w1, w3, w2):
    smem = pl.BlockSpec(memory_space=pltpu.SMEM)
    hbm = pl.BlockSpec(memory_space=pl.ANY)
    return pl.pallas_call(
        _experts_kernel,
        in_specs=[smem, smem, smem, smem, smem, hbm, hbm, hbm, hbm, hbm],
        out_specs=hbm,
        out_shape=jax.ShapeDtypeStruct(xy.shape, xy.dtype),
        scratch_shapes=[pltpu.VMEM((GATHER_DEPTH, EXPERT_TILE, D_MODEL), BF16),
                        pltpu.VMEM((GATHER_DEPTH, EXPERT_TILE, LANES), F32),
                        pltpu.VMEM((2, EXPERT_TILE, D_MODEL), BF16),
                        pltpu.VMEM((2, D_MODEL, D_EXPERT), F32),
                        pltpu.VMEM((2, D_MODEL, D_EXPERT), F32),
                        pltpu.VMEM((2, D_EXPERT, D_MODEL), F32),
                        pltpu.SemaphoreType.DMA((GATHER_DEPTH,)),
                        pltpu.SemaphoreType.DMA((2,)),
                        pltpu.SemaphoreType.DMA((2,))],
        input_output_aliases={5: 0},
        compiler_params=pltpu.CompilerParams(vmem_limit_bytes=VMEM_LIMIT),
        name="moe_experts",
    )(gsrc, gdst, texp, tend, ntile, xy, ws, w1, w3, w2)


def _combine_kernel(y_ref, pt_ref, h1_ref, g2_ref, fg_ref, o_ref):
    moe = _dot(pt_ref[...], y_ref[...])
    h = h1_ref[...] + g2_ref[...] * moe
    ms = jnp.mean(h * h, axis=-1, keepdims=True)
    o_ref[...] = h * lax.rsqrt(ms + EPS) * fg_ref[...]


def _combine(xy, pt, h1, g2, final_g):
    t = h1.shape[0]
    row = lambda i: (i, 0)
    const = lambda i: (0, 0)
    return pl.pallas_call(
        _combine_kernel,
        grid=(t // DISPATCH_BLOCK,),
        in_specs=[pl.BlockSpec((LOCAL_CAP, D_MODEL), row),
                  pl.BlockSpec((DISPATCH_BLOCK, LOCAL_CAP), row),
                  pl.BlockSpec((DISPATCH_BLOCK, D_MODEL), row),
                  pl.BlockSpec((1, D_MODEL), const), pl.BlockSpec((1, D_MODEL), const)],
        out_specs=pl.BlockSpec((DISPATCH_BLOCK, D_MODEL), row),
        out_shape=jax.ShapeDtypeStruct((t, D_MODEL), F32),
        compiler_params=_cparams(("arbitrary",)),
        name="moe_combine_final",
    )(xy, pt, h1, g2, final_g)


def _pad_lanes(a):
    return jnp.pad(a, ((0, 0), (0, LANES - a.shape[1])))


def kernel(x, c, ctx, c_ctx, w_mod, b_mod, norm1_g, w_in, w_conv_q, w_conv_k, gate_bias, head_norm_g, w_pool, pool_scale, w_out, norm2_g, w_group, b_group, w_router, b_router, w1, w3, w2, final_g):
    assert x.shape[0] == 1 and w_mod.shape[0] == 1
    seq = x.shape[1]
    x2d = x[0]
    ctx2d = ctx[0]

    n_main = 5 * MIX_HALF
    w_in_t = jnp.transpose(w_in[0])
    w_gate = _pad_lanes(jnp.transpose(_take_rows(w_in_t, n_main, N_GATES))).astype(BF16)
    gate_bias_row = _pad_lanes(gate_bias[0].reshape(1, N_GATES))
    wr_pack, wr_hi = _split_pack(jnp.concatenate([w_group[0], w_router[0]], axis=1))
    b_route = _pad_lanes(jnp.concatenate([b_group[0], b_router[0]]).reshape(1, -1))
    norm1 = norm1_g[0].reshape(1, D_MODEL)

    c16 = jnp.concatenate([c, c_ctx[None, :], jnp.zeros((14, D_MODEL), F32)], axis=0)
    mods = _adaln(c16, w_mod[0], b_mod[0].reshape(1, -1))
    mod_lat = mods[0].reshape(6, D_MODEL)
    mod_ctx = mods[1].reshape(6, D_MODEL)

    mod_in = jnp.concatenate([mod_ctx[0:2], mod_lat[0:2]], axis=0)
    u_pool, q, uo, k, kt, v, col, rowi = _inproj(x2d, ctx2d, mod_in, norm1, w_in_t, w_gate,
                                                 gate_bias_row, w_conv_q[0], w_conv_k[0])
    nc = seq // CHUNK
    zeros_state = (jnp.zeros((2 * HEADS, HEAD_DIM, HEAD_DIM), F32),
                   jnp.zeros((2 * HEADS, 1, HEAD_DIM), F32),
                   jnp.zeros((2 * HEADS, 1, LANES), F32))
    s0, n0, m0 = _mlstm(None, k, kt, v, col, rowi, *zeros_state, need_out=False, first=nc, nc=1)

    hf, hb = _mlstm(q, k, kt, v, col, rowi, s0, n0, m0, need_out=True, first=0, nc=nc)
    p = _pool(u_pool, w_pool[0], pool_scale[0].reshape(1, -1), seq)
    h1, fn, info = _outproj(p, hf, hb, uo, x2d, w_out[0], mod_lat, head_norm_g[0].reshape(1, -1),
                            norm2_g[0].reshape(1, -1), wr_pack, wr_hi, b_route, tm=2 * OUTPROJ_SUB)

    xs, ws, pt, cnt = _dispatch(fn, info)
    gsrc, gdst, texp, tend, ntile = _plan(cnt[::8, :N_EXPERTS], seq)
    xy = _experts(gsrc, gdst, texp, tend, ntile, xs, ws, w1[0], w3[0], w2[0])
    out = _combine(xy, pt, h1, mod_lat[5:6], final_g.reshape(1, -1))
    return out.reshape(1, seq, D_MODEL)
```

```python
import functools

import jax
import jax.numpy as jnp
from jax import lax
from jax.experimental import pallas as pl
from jax.experimental.pallas import tpu as pltpu

F32 = jnp.float32
BF16 = jnp.bfloat16

D_MODEL = 2048
GRID_W = 64
GRID_SHIFT = 6
POOL_WINDOWS = (2, 4, 8, 16)
POOL_GROUP = 256
HEADS = 4
HEAD_DIM = 256
MIX_HALF = 1024
N_GATES = 16
N_GROUPS = 4
EXPERTS_PER_GROUP = 8
N_EXPERTS = 32
D_EXPERT = 512
EPS = 1e-6
LANES = 128
CHUNK = 256
ROUTE_LANE0 = N_GROUPS

VMEM_LIMIT = 60 * 1024 * 1024


def _cparams(sem, vmem=VMEM_LIMIT):
    return pltpu.CompilerParams(dimension_semantics=sem, vmem_limit_bytes=vmem)


def _split2(x):
    hi = x.astype(BF16)
    lo = (x - hi.astype(F32)).astype(BF16)
    return hi, lo


def _split3(x):
    hi = x.astype(BF16)
    r = x - hi.astype(F32)
    mid = r.astype(BF16)
    lo = (r - mid.astype(F32)).astype(BF16)
    return hi, mid, lo


def _dot(a, b):
    return jnp.dot(a, b, preferred_element_type=F32)


SPLIT_LANE = 64


def _split_pack(w):
    hi, lo = _split2(w)
    n = w.shape[1]
    gap = jnp.zeros((w.shape[0], SPLIT_LANE - n), BF16)
    rest = jnp.zeros((w.shape[0], LANES - n), BF16)
    return jnp.concatenate([hi, gap, lo, gap], axis=1), jnp.concatenate([hi, rest], axis=1)


def _split_dot(xh, xl, w_packed, w_hi):
    r = _dot(xh, w_packed)
    return r + pltpu.roll(r, SPLIT_LANE, 1) + _dot(xl, w_hi)


def _silu(x):
    return x * jax.nn.sigmoid(x)


def _log_sigmoid(x):
    return jnp.minimum(x, 0.0) - jnp.log(1.0 + jnp.exp(-jnp.abs(x)))


def _copy_kernel(w_ref, o_ref):
    o_ref[...] = w_ref[...]


def _take_rows(w, start, n):
    return pl.pallas_call(
        _copy_kernel,
        grid=(1,),
        in_specs=[pl.BlockSpec((n, w.shape[1]), lambda i: (start // n, 0))],
        out_specs=pl.BlockSpec((n, w.shape[1]), lambda i: (0, 0)),
        out_shape=jax.ShapeDtypeStruct((n, w.shape[1]), w.dtype),
        name="take_rows",
    )(w)


def _adaln_kernel(c_ref, w_ref, b_ref, o_ref):
    a = _silu(c_ref[...])
    a3 = jnp.concatenate(_split3(a), axis=0)
    w_hi, w_lo = _split2(w_ref[...])
    acc = _dot(a3, w_hi)
    acc_lo = _dot(a3[:32], w_lo)
    out = acc[0:16] + acc[16:32] + acc[32:48] + acc_lo[0:16] + acc_lo[16:32]
    o_ref[...] = out + b_ref[...]


def _adaln(c16, w_mod, b_mod):
    n = w_mod.shape[1]
    tn = 1536
    return pl.pallas_call(
        _adaln_kernel,
        grid=(n // tn,),
        in_specs=[pl.BlockSpec((16, D_MODEL), lambda j: (0, 0)),
                  pl.BlockSpec((D_MODEL, tn), lambda j: (0, j)),
                  pl.BlockSpec((1, tn), lambda j: (0, j))],
        out_specs=pl.BlockSpec((16, tn), lambda j: (0, j)),
        out_shape=jax.ShapeDtypeStruct((16, n), F32),
        compiler_params=_cparams(("arbitrary",)),
        name="adaln",
    )(c16, w_mod, b_mod)


HALO = 8
NEG_INF = float("-inf")


def _gate_scan_info(gates):
    n = gates.shape[0]
    lane = lax.broadcasted_iota(jnp.int32, gates.shape, 1)
    rows = lax.broadcasted_iota(jnp.int32, gates.shape, 0)
    lf = jnp.where(lane < N_GATES, _log_sigmoid(gates), 0.0)
    hi = lf.astype(BF16).astype(F32)
    rem = lf - hi
    mid = rem.astype(BF16).astype(F32)
    packed = (hi + pltpu.roll(mid, 32, 1) + pltpu.roll(rem - mid, 64, 1)).astype(BF16)
    r = lax.broadcasted_iota(jnp.int32, (n, n), 0)
    c = lax.broadcasted_iota(jnp.int32, (n, n), 1)
    pf = _dot(jnp.where(r >= c, 1.0, 0.0).astype(BF16), packed)
    pb = _dot(jnp.where(r <= c, 1.0, 0.0).astype(BF16), packed)
    bf = pf + pltpu.roll(pf, 96, 1) + pltpu.roll(pf, 64, 1)
    bb = pb + pltpu.roll(pb, 96, 1) + pltpu.roll(pb, 64, 1)
    b = jnp.where(lane < 8, bf, bb)
    cval = gates - pltpu.roll(b, LANES - 4, 1)
    pm = cval
    sm = cval
    step = 1
    while step < n:
        pm = jnp.maximum(pm, jnp.where(rows >= step, pltpu.roll(pm, step, 0), NEG_INF))
        sm = jnp.maximum(sm, jnp.where(rows < n - step, pltpu.roll(sm, n - step, 0), NEG_INF))
        step *= 2
    cm = jnp.where(lane < 8, pm, sm)
    return cval, jnp.where((lane & 4) == 0, cm, b)


W_CHUNK = 256


def _load_bf16(src_hbm, dst, stage, sem, *, n_chunks, transpose):
    rows = stage.shape[1]
    def chunk_copy(c, sl):
        return pltpu.make_async_copy(
            src_hbm.at[pl.ds(pl.multiple_of(c * rows, rows), rows), :], stage.at[sl], sem.at[sl])

    chunk_copy(0, 0).start()

    def convert(c, carry):
        sl = c % 2

        @pl.when(c + 1 < n_chunks)
        def _():
            chunk_copy(c + 1, 1 - sl).start()

        chunk_copy(c, sl).wait()
        span = pl.ds(pl.multiple_of(c * rows, rows), rows)
        if transpose:
            dst[:, span] = stage[sl].T.astype(BF16)
        else:
            dst[span, :] = stage[sl].astype(BF16)
        return carry

    lax.fori_loop(0, n_chunks, convert, 0)


def _inproj_kernel(x_ref, xp_ref, xn_ref, ctx_ref, mod_ref, g_ref, wt_hbm, wg_ref, gb_ref,
                   wcq_ref, wck_ref, pool_o, q_o, o_o, k_o, kt_o, v_o, col_o, row_o,
                   w_s, stage, sem, *, nt):
    i = pl.program_id(0)
    tm = x_ref.shape[0]

    @pl.when(i == 0)
    def _():
        _load_bf16(wt_hbm, w_s, stage, sem, n_chunks=w_s.shape[1] // W_CHUNK, transpose=True)

    is_ctx = i == nt
    x_main = jnp.where(is_ctx, ctx_ref[...], x_ref[...])
    x_all = jnp.concatenate([xp_ref[...], x_main, xn_ref[...]], axis=0)
    shift = jnp.where(is_ctx, mod_ref[0:1, :], mod_ref[2:3, :])
    scale = jnp.where(is_ctx, mod_ref[1:2, :], mod_ref[3:4, :])
    ms = jnp.mean(x_all * x_all, axis=-1, keepdims=True)
    y = x_all * lax.rsqrt(ms + EPS) * g_ref[...]
    xn_all = y * (1.0 + scale) + shift
    xh_all = xn_all.astype(BF16)
    xh = xn_all[HALO:HALO + tm].astype(BF16)

    def cols(ci):
        return w_s[:, ci * MIX_HALF:(ci + 1) * MIX_HALF]

    u_q = _dot(xh_all, cols(1))
    u_k = _dot(xh_all, cols(3))
    pool_o[...] = _dot(xh, cols(0))
    o_o[...] = _dot(xh, cols(2))
    v_o[...] = _dot(xh, cols(4)).astype(BF16)

    keep_prev = jnp.where(jnp.logical_or(i == 0, is_ctx), 0.0, 1.0)
    keep_next = jnp.where(jnp.logical_or(i == nt - 1, is_ctx), 0.0, 1.0)

    def conv_silu(u, wc_ref):
        n = u.shape[0]
        u = jnp.concatenate([u[:HALO] * keep_prev, u[HALO:HALO + tm], u[HALO + tm:] * keep_next], axis=0)
        up = pltpu.roll(u, 1, 0)[HALO:HALO + tm]
        un = pltpu.roll(u, n - 1, 0)[HALO:HALO + tm]
        return _silu(wc_ref[0:1, :] * up + wc_ref[1:2, :] * u[HALO:HALO + tm] + wc_ref[2:3, :] * un)

    q_o[...] = (conv_silu(u_q, wcq_ref) * (HEAD_DIM ** -0.5)).astype(BF16)
    k = conv_silu(u_k, wck_ref)
    k_o[...] = k.astype(BF16)
    kt_o[...] = k.astype(BF16).T
    gates = _dot(xh, wg_ref[...]) + gb_ref[...]
    cval, col = _gate_scan_info(gates)
    col_o[...] = col[:, :N_GATES]
    row_o[...] = cval.T[:N_GATES, :]


def _inproj(x2d, ctx2d, mod, g, w_in_t, w_gate, gate_bias_row, wcq, wck):
    t = x2d.shape[0]
    tm = CHUNK
    assert ctx2d.shape[0] == tm
    nt = t // tm
    r8 = tm // HALO
    last8 = t // HALO - 1
    n_main = 5 * MIX_HALF
    rows = t + tm
    const = lambda i: (0, 0)
    row = lambda i: (i, 0)
    f32_out = jax.ShapeDtypeStruct((rows, MIX_HALF), F32)
    bf16_out = jax.ShapeDtypeStruct((rows, MIX_HALF), BF16)
    seq = pl.BlockSpec((tm, MIX_HALF), row)
    return pl.pallas_call(
        functools.partial(_inproj_kernel, nt=nt),
        grid=(nt + 1,),
        in_specs=[pl.BlockSpec((tm, D_MODEL), lambda i: (jnp.minimum(i, nt - 1), 0)),
                  pl.BlockSpec((HALO, D_MODEL), lambda i: (jnp.clip(i * r8 - 1, 0, last8), 0)),
                  pl.BlockSpec((HALO, D_MODEL), lambda i: (jnp.minimum((i + 1) * r8, last8), 0)),
                  pl.BlockSpec((tm, D_MODEL), const),
                  pl.BlockSpec((4, D_MODEL), const),
                  pl.BlockSpec((1, D_MODEL), const),
                  pl.BlockSpec(memory_space=pl.ANY),
                  pl.BlockSpec((D_MODEL, LANES), const),
                  pl.BlockSpec((1, LANES), const),
                  pl.BlockSpec((3, MIX_HALF), const),
                  pl.BlockSpec((3, MIX_HALF), const)],
        out_specs=[seq, seq, seq, seq, pl.BlockSpec((MIX_HALF, tm), lambda i: (0, i)), seq,
                   pl.BlockSpec((tm, N_GATES), row), pl.BlockSpec((N_GATES, tm), lambda i: (0, i))],
        out_shape=[f32_out, bf16_out, f32_out, bf16_out,
                   jax.ShapeDtypeStruct((MIX_HALF, rows), BF16), bf16_out,
                   jax.ShapeDtypeStruct((rows, N_GATES), F32),
                   jax.ShapeDtypeStruct((N_GATES, rows), F32)],
        scratch_shapes=[pltpu.VMEM((D_MODEL, n_main), BF16),
                        pltpu.VMEM((2, W_CHUNK, D_MODEL), F32),
                        pltpu.SemaphoreType.DMA((2,))],
        compiler_params=_cparams(("arbitrary",)),
        name="inproj",
    )(x2d, x2d, x2d, ctx2d, mod, g, w_in_t, w_gate, gate_bias_row, wcq, wck)


def _mlstm_kernel(*refs, need_out):
    if need_out:
        (qf, kf, ktf, vf, colf, rowf, qb, kb, ktb, vb, colb, rowb, s0, n0, m0,
         hf_o, hb_o, s_s, n_s, m_s) = refs
        q_refs, h_outs = (qf, qb), (hf_o, hb_o)
    else:
        (kf, ktf, vf, colf, rowf, kb, ktb, vb, colb, rowb, s0, n0, m0,
         s_o, n_o, m_o, s_s, n_s, m_s) = refs
    k_refs, kt_refs, v_refs = (kf, kb), (ktf, ktb), (vf, vb)
    col_refs, row_refs = (colf, colb), (rowf, rowb)
    j = pl.program_id(0)

    @pl.when(j == 0)
    def _():
        s_s[...] = s0[...]
        n_s[...] = n0[...]
        m_s[...] = m0[...]

    L = CHUNK
    row = lax.broadcasted_iota(jnp.int32, (L, L), 0)
    col = lax.broadcasted_iota(jnp.int32, (L, L), 1)

    heads = [(d, h) for d in range(2) for h in range(HEADS)]

    def head_values(d, h):
        hd = d * HEADS + h
        lc, lb = 8 * d + h, 8 * d + 4 + h
        hs = slice(h * HEAD_DIM, (h + 1) * HEAD_DIM)
        edge = L - 1 if d == 0 else 0
        colv = col_refs[d][...]
        rowv = row_refs[d][...]
        v = dict(hd=hd, hs=hs, d=d)
        v["cm_c"], v["b_c"] = colv[:, lc:lc + 1], colv[:, lb:lb + 1]
        v["c_r"] = rowv[lc:lc + 1, :]
        v["g"] = v["b_c"][edge:edge + 1, :]
        v["m_old"] = m_s[hd][:, 0:1]
        v["s_old"] = s_s[hd]
        v["n_old"] = n_s[hd]
        v["m_x"] = jnp.maximum(v["m_old"], v["cm_c"][edge:edge + 1, :])
        v["decay"] = jnp.exp(v["m_old"] - v["m_x"])
        v["wk_r"] = jnp.exp(v["c_r"] - v["m_x"])
        v["v_h"] = v_refs[d][:, hs]
        v["kt_h"] = kt_refs[d][hs, :]
        return v

    def update_state(v):
        hd = v["hd"]
        n_s[hd] = v["decay"] * v["n_old"] + _dot(
            jnp.broadcast_to(v["wk_r"], (8, L)).astype(BF16), k_refs[v["d"]][:, v["hs"]])[0:1, :]
        kwt = (v["kt_h"].astype(F32) * v["wk_r"]).astype(BF16)
        s_s[hd] = v["decay"] * v["s_old"] + _dot(kwt, v["v_h"])
        m_s[hd] = jnp.broadcast_to(v["g"] + v["m_x"], (1, LANES))

    def weights(v):
        d, hs = v["d"], v["hs"]
        mask = (row >= col) if d == 0 else (row <= col)
        q_h = q_refs[d][:, hs]
        m_c = jnp.maximum(v["m_old"], v["cm_c"])
        w_inter = jnp.exp(v["m_old"] - m_c)
        v["e"] = jnp.where(mask, jnp.exp(v["c_r"] - m_c), 0.0)
        v["qw"] = (q_h.astype(F32) * w_inter).astype(BF16)
        qn = lax.dot_general(q_h, jnp.broadcast_to(v["n_old"], (8, HEAD_DIM)).astype(BF16),
                             (((1,), (1,)), ((), ())), preferred_element_type=F32)
        v["den_inter"] = w_inter * qn[:, 0:1]
        v["floor"] = jnp.exp(-(v["b_c"] + m_c))
        v["s_bf"] = v["s_old"].astype(BF16)

    def outputs(v, qk):
        p = (v["e"] * qk).astype(BF16)
        num = _dot(v["qw"], v["s_bf"]) + _dot(p, v["v_h"])
        p_sum = lax.dot_general(p, jnp.ones((8, L), BF16), (((1,), (1,)), ((), ())),
                                preferred_element_type=F32)
        den = v["den_inter"] + p_sum[:, 0:1]
        h_outs[v["d"]][:, v["hs"]] = num / jnp.maximum(jnp.abs(den), v["floor"])

    vals = [head_values(d, h) for d, h in heads]
    if need_out:
        qks = [_dot(q_refs[v["d"]][:, v["hs"]], v["kt_h"]) for v in vals]
        for v in vals:
            weights(v)
    for v in vals:
        update_state(v)
    if need_out:
        for v, qk in zip(vals, qks):
            outputs(v, qk)

    if not need_out:
        s_o[...] = s_s[...]
        n_o[...] = n_s[...]
        m_o[...] = m_s[...]


def _mlstm(q, k, kt, v, col, rowi, s0, n0, m0, need_out, first, nc):
    t = nc * CHUNK
    fwd = lambda j: (first + j, 0)
    bwd = lambda j: (first + nc - 1 - j, 0)
    fwd_t = lambda j: (0, first + j)
    bwd_t = lambda j: (0, first + nc - 1 - j)
    c3 = lambda j: (0, 0, 0)
    seq = lambda im: pl.BlockSpec((CHUNK, MIX_HALF), im)
    state_specs = [pl.BlockSpec(s0.shape, c3), pl.BlockSpec(n0.shape, c3), pl.BlockSpec(m0.shape, c3)]
    scratch = [pltpu.VMEM(s0.shape, F32), pltpu.VMEM(n0.shape, F32), pltpu.VMEM(m0.shape, F32)]

    def side(im, im_t):
        specs = ([seq(im)] if need_out else []) + [seq(im), pl.BlockSpec((MIX_HALF, CHUNK), im_t), seq(im)]
        return specs + [pl.BlockSpec((CHUNK, N_GATES), im), pl.BlockSpec((N_GATES, CHUNK), im_t)]

    in_specs = side(fwd, fwd_t) + side(bwd, bwd_t) + state_specs
    seq_in = ((q,) if need_out else ()) + (k, kt, v, col, rowi)
    args = seq_in + seq_in + (s0, n0, m0)
    if need_out:
        out_specs = [seq(fwd), seq(bwd)]
        out_shape = [jax.ShapeDtypeStruct((t, MIX_HALF), F32)] * 2
    else:
        out_specs = state_specs
        out_shape = [jax.ShapeDtypeStruct(a.shape, F32) for a in (s0, n0, m0)]
    return pl.pallas_call(
        functools.partial(_mlstm_kernel, need_out=need_out),
        grid=(nc,),
        in_specs=in_specs,
        out_specs=out_specs,
        out_shape=out_shape,
        scratch_shapes=scratch,
        compiler_params=_cparams(("arbitrary",)),
        name="mlstm_out" if need_out else "mlstm_state",
    )(*args)


POOL_PAD = 512
POOL_UNROLL = 4


def _pool_kernel(u_ref, w_ref, sc_ref, o_ref, pad_s, *, t):
    for gi, win in enumerate(POOL_WINDOWS):
        @pl.when(pl.program_id(0) == gi)
        def _():
            _pool_group(u_ref, w_ref, sc_ref, o_ref, pad_s, win=win, t=t)


def _pool_group(u_ref, w_ref, sc_ref, o_ref, pad_s, *, win, t):
    half = win // 2
    tile = 256
    zeros = jnp.zeros((POOL_PAD, POOL_GROUP), F32)
    pad_s[0:POOL_PAD, :] = zeros
    pad_s[POOL_PAD + t:POOL_PAD + t + POOL_PAD, :] = zeros

    def copy(r, carry):
        t0 = pl.multiple_of(r * tile, tile)
        pad_s[pl.ds(POOL_PAD + t0, tile), :] = u_ref[pl.ds(t0, tile), :]
        return carry

    lax.fori_loop(0, t // tile, copy, 0)

    row = lax.broadcasted_iota(jnp.int32, (tile, tile), 0)
    col = lax.broadcasted_iota(jnp.int32, (tile, tile), 1)
    same_row = (row >> GRID_SHIFT) == (col >> GRID_SHIFT)
    in_win = (col - row >= -half) & (col - row < half)
    band = jnp.where(same_row & in_win, 1.0, 0.0).astype(BF16)
    w = w_ref[...].astype(BF16)
    scale = sc_ref[...]
    n_rows = t // GRID_W

    tok0 = lax.broadcasted_iota(jnp.int32, (tile, POOL_GROUP), 0)
    gc = tok0 & (GRID_W - 1)
    inv_h = 1.0 / (jnp.minimum(gc + half, GRID_W) - jnp.maximum(gc - half, 0)).astype(F32)

    def body(r, carry):
        t0s = [pl.multiple_of((r * POOL_UNROLL + k) * tile, tile) for k in range(POOL_UNROLL)]
        pieces = []
        for t0 in t0s:
            acc = pad_s[pl.ds(POOL_PAD + t0 - GRID_W * half, tile), :]
            for dd in range(-half + 1, half):
                acc = acc + pad_s[pl.ds(POOL_PAD + t0 + GRID_W * dd, tile), :]
            gr = (t0 + tok0) >> GRID_SHIFT
            cnt_v = jnp.minimum(gr + half, n_rows) - jnp.maximum(gr - half, 0)
            pieces.append(_split2(acc / cnt_v.astype(F32)))
        means = [(_dot(band, hi) + _dot(band, lo)) * inv_h for hi, lo in pieces]
        diffs = [(m - pad_s[pl.ds(POOL_PAD + t0, tile), :]).astype(BF16) for m, t0 in zip(means, t0s)]
        for d, t0 in zip(diffs, t0s):
            o_ref[pl.ds(t0, tile), :] = (_dot(d, w) * scale).astype(BF16)
        return carry

    lax.fori_loop(0, t // (tile * POOL_UNROLL), body, 0)


def _pool(u_pool, w_pool, scale_row, t):
    return pl.pallas_call(
        functools.partial(_pool_kernel, t=t),
        grid=(len(POOL_WINDOWS),),
        in_specs=[pl.BlockSpec((t, POOL_GROUP), lambda g: (0, g)),
                  pl.BlockSpec((None, POOL_GROUP, POOL_GROUP), lambda g: (g, 0, 0)),
                  pl.BlockSpec((1, POOL_GROUP), lambda g: (0, g))],
        out_specs=pl.BlockSpec((t, POOL_GROUP), lambda g: (0, g)),
        out_shape=jax.ShapeDtypeStruct((t, MIX_HALF), BF16),
        scratch_shapes=[pltpu.VMEM((t + 2 * POOL_PAD, POOL_GROUP), F32)],
        compiler_params=_cparams(("arbitrary",)),
        name="pool_mix",
    )(u_pool, w_pool, scale_row)


OUTPROJ_SUB = 256


def _route(logits):
    lane = lax.broadcasted_iota(jnp.int32, logits.shape, 1).astype(F32)
    neg = -jnp.inf
    big = float(LANES)
    gl = jnp.where(lane < N_GROUPS, logits, neg)
    gmax = jnp.max(gl, axis=1, keepdims=True)
    gsel = jnp.min(jnp.where(gl == gmax, lane, big), axis=1, keepdims=True)
    p_grp = 1.0 / jnp.sum(jnp.exp(gl - gmax), axis=1, keepdims=True)
    lo = ROUTE_LANE0 + EXPERTS_PER_GROUP * gsel
    el = jnp.where((lane >= lo) & (lane < lo + EXPERTS_PER_GROUP), logits, neg)
    m1 = jnp.max(el, axis=1, keepdims=True)
    i1 = jnp.min(jnp.where(el == m1, lane, big), axis=1, keepdims=True)
    el2 = jnp.where(lane == i1, neg, el)
    m2 = jnp.max(el2, axis=1, keepdims=True)
    i2 = jnp.min(jnp.where(el2 == m2, lane, big), axis=1, keepdims=True)
    e2 = jnp.exp(m2 - m1)
    p1 = 1.0 / (1.0 + e2)
    p2 = e2 / (1.0 + e2)
    info = jnp.where(lane == 0.0, i1 - ROUTE_LANE0, 0.0)
    info = jnp.where(lane == 1.0, i2 - ROUTE_LANE0, info)
    info = jnp.where(lane == 2.0, p_grp * p1, info)
    return jnp.where(lane == 3.0, p_grp * p2, info)


def _outproj_kernel(p_ref, hf_ref, hb_ref, uo_ref, x_ref, wout_hbm, mod_ref,
                    hg_ref, n2g_ref, wrp_ref, wrh_ref, br_ref, h1_o, fn_o, info_o, w_s, stage, sem):
    @pl.when(pl.program_id(0) == 0)
    def _():
        _load_bf16(wout_hbm, w_s, stage, sem, n_chunks=w_s.shape[0] // stage.shape[1], transpose=False)

    def mixer_input(rs):
        h = hf_ref[rs, :] + hb_ref[rs, :]
        parts = []
        for hh in range(HEADS):
            hs = h[:, hh * HEAD_DIM:(hh + 1) * HEAD_DIM]
            mu = jnp.mean(hs, axis=-1, keepdims=True)
            ctr = hs - mu
            var = jnp.mean(ctr * ctr, axis=-1, keepdims=True)
            parts.append(ctr * lax.rsqrt(var + EPS))
        hn = jnp.concatenate(parts, axis=1) * hg_ref[...]
        m = (hn * jax.nn.sigmoid(uo_ref[rs, :])).astype(BF16)
        return jnp.concatenate([p_ref[rs, :], m], axis=1)

    def finish(rs, mix):
        h1 = x_ref[rs, :] + mod_ref[2:3, :] * mix
        h1_o[rs, :] = h1
        ms = jnp.mean(h1 * h1, axis=-1, keepdims=True)
        fn = h1 * lax.rsqrt(ms + EPS) * n2g_ref[...]
        fn = fn * (1.0 + mod_ref[4:5, :]) + mod_ref[3:4, :]
        fh, fl = _split2(fn)
        fn_o[rs, :] = fh
        logits = _split_dot(fh, fl, wrp_ref[...], wrh_ref[...]) + br_ref[...]
        info_o[rs, :] = _route(logits)

    tm = x_ref.shape[0]
    subs = [slice(r0, r0 + OUTPROJ_SUB) for r0 in range(0, tm, OUTPROJ_SUB)]
    w = w_s[...]
    mixes = [_dot(mixer_input(rs), w) for rs in subs]
    for rs, mix in zip(subs, mixes):
        finish(rs, mix)


def _outproj(p, hf, hb, uo, x2d, w_out, mod, head_g, norm2_g, wr_pack, wr_hi, b_route, tm):
    t = x2d.shape[0]
    const = lambda i: (0, 0)
    row = lambda i: (i, 0)
    in_specs = ([pl.BlockSpec((tm, MIX_HALF), row)] * 4
                + [pl.BlockSpec((tm, D_MODEL), row),
                   pl.BlockSpec(memory_space=pl.ANY),
                   pl.BlockSpec(mod.shape, const),
                   pl.BlockSpec((1, MIX_HALF), const),
                   pl.BlockSpec((1, D_MODEL), const),
                   pl.BlockSpec((D_MODEL, LANES), const),
                   pl.BlockSpec((D_MODEL, LANES), const),
                   pl.BlockSpec((1, LANES), const)])
    return pl.pallas_call(
        _outproj_kernel,
        grid=(t // tm,),
        in_specs=in_specs,
        out_specs=[pl.BlockSpec((tm, D_MODEL), row), pl.BlockSpec((tm, D_MODEL), row),
                   pl.BlockSpec((tm, LANES), row)],
        out_shape=[jax.ShapeDtypeStruct((t, D_MODEL), F32),
                   jax.ShapeDtypeStruct((t, D_MODEL), BF16),
                   jax.ShapeDtypeStruct((t, LANES), F32)],
        scratch_shapes=[pltpu.VMEM(w_out.shape, BF16),
                        pltpu.VMEM((2, W_CHUNK // 2, w_out.shape[1]), F32),
                        pltpu.SemaphoreType.DMA((2,))],
        compiler_params=_cparams(("arbitrary",)),
        name="outproj_route",
    )(p, hf, hb, uo, x2d, w_out, mod, head_g, norm2_g, wr_pack, wr_hi, b_route)


DISPATCH_BLOCK = 512
GRANULE = 16
GRANULE_SHIFT = 4
LOCAL_CAP = 1536
LOCAL_GRANULES = LOCAL_CAP // GRANULE
FREE_GRANULES = 2
EXPERT_TILE = 256
TILE_GRANULES = EXPERT_TILE // GRANULE
PLAN_UNROLL = 4
PLAN_SLACK = 8
GATHER_DEPTH = 4


def _pair_rows(pos1, pos2):
    rows = lax.broadcasted_iota(jnp.int32, (pos1.shape[0], LOCAL_CAP), 1).astype(F32)
    return jnp.where(rows == pos1, 1.0, 0.0), jnp.where(rows == pos2, 1.0, 0.0)


def _dispatch_kernel(fn_ref, info_ref, xs_o, ws_o, pos_o, cnt_o):
    tb = DISPATCH_BLOCK
    info = info_ref[...]
    e1, e2 = info[:, 0:1], info[:, 1:2]
    w1c, w2c = info[:, 2:3], info[:, 3:4]
    lane = lax.broadcasted_iota(jnp.int32, (tb, LANES), 1).astype(F32)
    o1 = jnp.where(lane == e1, 1.0, 0.0)
    o2 = jnp.where(lane == e2, 1.0, 0.0)
    onehot = o1 + o2
    cnt = jnp.sum(onehot, axis=0, keepdims=True)
    gran = jnp.floor((cnt + (GRANULE - 1)) * (1.0 / GRANULE))
    a = lax.broadcasted_iota(jnp.int32, (LANES, LANES), 0)
    b = lax.broadcasted_iota(jnp.int32, (LANES, LANES), 1)
    upper = jnp.where(a < b, 1.0, 0.0).astype(BF16)
    seg_off = _dot(jnp.broadcast_to(gran, (8, LANES)).astype(BF16), upper)[0:1, :] * GRANULE
    r = lax.broadcasted_iota(jnp.int32, (tb, tb), 0)
    c = lax.broadcasted_iota(jnp.int32, (tb, tb), 1)
    strict = jnp.where(r > c, 1.0, 0.0).astype(BF16)
    rank = _dot(strict, onehot.astype(BF16))
    slot = rank + seg_off
    pos1 = jnp.sum(o1 * slot, axis=1, keepdims=True)
    pos2 = jnp.sum(o2 * slot, axis=1, keepdims=True)
    pt1, pt2 = _pair_rows(pos1, pos2)
    pos_o[...] = jnp.where(lane == 0.0, pos1, jnp.where(lane == 1.0, pos2, 0.0))
    perm = (pt1 + pt2).T.astype(BF16)
    w_slot = jnp.sum((pt1 * w1c + pt2 * w2c).T, axis=1, keepdims=True)
    for c0 in range(0, D_MODEL, 512):
        xs_o[:, c0:c0 + 512] = _dot(perm, fn_ref[:, c0:c0 + 512]).astype(BF16)
    ws_o[...] = jnp.broadcast_to(w_slot, (LOCAL_CAP, LANES))
    cnt_o[...] = jnp.broadcast_to(cnt, (8, LANES)).astype(jnp.int32)


def _dispatch(fn, info):
    t = fn.shape[0]
    nb = t // DISPATCH_BLOCK
    row = lambda i: (i, 0)
    return pl.pallas_call(
        _dispatch_kernel,
        grid=(nb,),
        in_specs=[pl.BlockSpec((DISPATCH_BLOCK, D_MODEL), row),
                  pl.BlockSpec((DISPATCH_BLOCK, LANES), row)],
        out_specs=[pl.BlockSpec((LOCAL_CAP, D_MODEL), row),
                   pl.BlockSpec((LOCAL_CAP, LANES), row),
                   pl.BlockSpec((DISPATCH_BLOCK, LANES), row),
                   pl.BlockSpec((8, LANES), row)],
        out_shape=[jax.ShapeDtypeStruct((nb * LOCAL_CAP, D_MODEL), BF16),
                   jax.ShapeDtypeStruct((nb * LOCAL_CAP, LANES), F32),
                   jax.ShapeDtypeStruct((t, LANES), F32),
                   jax.ShapeDtypeStruct((nb * 8, LANES), jnp.int32)],
        compiler_params=_cparams(("arbitrary",)),
        name="moe_dispatch",
    )(fn, info)


def _map_len(max_tiles):
    return (max_tiles + GATHER_DEPTH - 1) * TILE_GRANULES + PLAN_SLACK


def _free_granule(q):
    return ((q // FREE_GRANULES) * LOCAL_GRANULES + (LOCAL_GRANULES - FREE_GRANULES)
            + (q % FREE_GRANULES))


def _plan_kernel(cnt_ref, gsrc_o, gdst_o, texp_o, tend_o, ntile_o, lrun, *, nb, max_tiles):
    def init(b, c):
        lrun[b] = 0
        return c

    lax.fori_loop(0, nb, init, 0)

    def per_expert(e, carry):
        g0, last_e = carry

        def per_block(b, g):
            k = (cnt_ref[b, e] + (GRANULE - 1)) >> GRANULE_SHIFT
            lo = lrun[b]
            lrun[b] = lo + k

            base = b * LOCAL_GRANULES + lo

            for j in range(PLAN_UNROLL):
                gsrc_o[g + j] = base + j
                gdst_o[g + j] = base + j

            @pl.when(k > PLAN_UNROLL)
            def _():
                def put(j, c):
                    gsrc_o[g + j] = base + j
                    gdst_o[g + j] = base + j
                    return c

                lax.fori_loop(PLAN_UNROLL, k, put, 0)

            return g + k

        g1 = lax.fori_loop(0, nb, per_block, g0)
        pad = (-g1) & (TILE_GRANULES - 1)

        def put_pad(j, c):
            g = g1 + j
            parity = (g // TILE_GRANULES) & 1
            gsrc_o[g] = _free_granule(0)
            gdst_o[g] = _free_granule(1 + parity * (TILE_GRANULES - 1) + (g & (TILE_GRANULES - 1)))
            return c

        lax.fori_loop(0, pad, put_pad, 0)
        g2 = g1 + pad

        def put_tile(tt, c):
            texp_o[tt] = e
            return c

        lax.fori_loop(g0 // TILE_GRANULES, g2 // TILE_GRANULES, put_tile, 0)
        tend_o[e] = g2 // TILE_GRANULES
        return g2, jnp.where(g2 > g0, e, last_e)

    g_end, last_e = lax.fori_loop(0, N_EXPERTS, per_expert, (0, 0))
    n_tiles = g_end // TILE_GRANULES
    ntile_o[0] = n_tiles

    def fill(tt, c):
        texp_o[tt] = last_e
        return c

    lax.fori_loop(n_tiles, max_tiles, fill, 0)

    def fill_map(g, c):
        gsrc_o[g] = _free_granule(0)
        gdst_o[g] = _free_granule(0)
        return c

    lax.fori_loop(g_end, _map_len(max_tiles), fill_map, 0)


def _max_tiles(t):
    nb = t // DISPATCH_BLOCK
    worst_rows = 2 * t + nb * N_EXPERTS * (GRANULE - 1) + N_EXPERTS * (EXPERT_TILE - GRANULE)
    return -(-worst_rows // EXPERT_TILE)


def _plan(cnt, t):
    nb = cnt.shape[0]
    assert nb * FREE_GRANULES >= 2 + 2 * (TILE_GRANULES - 1)
    max_tiles = _max_tiles(t)
    smem = pl.BlockSpec(memory_space=pltpu.SMEM)
    n_map = _map_len(max_tiles)
    return pl.pallas_call(
        functools.partial(_plan_kernel, nb=nb, max_tiles=max_tiles),
        in_specs=[smem],
        out_specs=[smem, smem, smem, smem, smem],
        out_shape=[jax.ShapeDtypeStruct((n_map,), jnp.int32),
                   jax.ShapeDtypeStruct((n_map,), jnp.int32),
                   jax.ShapeDtypeStruct((max_tiles,), jnp.int32),
                   jax.ShapeDtypeStruct((N_EXPERTS,), jnp.int32),
                   jax.ShapeDtypeStruct((1,), jnp.int32)],
        scratch_shapes=[pltpu.SMEM((nb,), jnp.int32)],
        name="moe_plan",
    )(cnt)


def _experts_kernel(gsrc, gdst, texp, tend, ntile, xy_in, ws_in, w1_hbm, w3_hbm, w2_hbm, xy_out,
                    xbuf, wsbuf, ybuf, st1, st3, st2, gsem, ssem, wsem):
    nt = ntile[0]

    def weight_copies(e, ws):
        return (pltpu.make_async_copy(w1_hbm.at[e], st1.at[ws], wsem.at[ws]),
                pltpu.make_async_copy(w3_hbm.at[e], st3.at[ws], wsem.at[ws]),
                pltpu.make_async_copy(w2_hbm.at[e], st2.at[ws], wsem.at[ws]))

    def rows(i):
        return pl.ds(pl.multiple_of(i * GRANULE, GRANULE), GRANULE)

    def gather_copies(g, j, sl):
        return (pltpu.make_async_copy(xy_in.at[rows(g), :], xbuf.at[sl, rows(j), :], gsem.at[sl]),
                pltpu.make_async_copy(ws_in.at[rows(g), :], wsbuf.at[sl, rows(j), :], gsem.at[sl]))

    def scatter_copies(g, j, sl):
        return (pltpu.make_async_copy(ybuf.at[sl, rows(j), :], xy_out.at[rows(g), :], ssem.at[sl]),)

    def issue(tt, sl, gmap, copies):
        for j in range(TILE_GRANULES):
            for cp in copies(gmap[tt * TILE_GRANULES + j], j, sl):
                cp.start()

    def drain(sl, copies):
        full = pl.ds(0, EXPERT_TILE)
        if copies is gather_copies:
            pltpu.make_async_copy(xy_in.at[full, :], xbuf.at[sl], gsem.at[sl]).wait()
            pltpu.make_async_copy(ws_in.at[full, :], wsbuf.at[sl], gsem.at[sl]).wait()
        else:
            pltpu.make_async_copy(ybuf.at[sl], xy_out.at[full, :], ssem.at[sl]).wait()

    @pl.when(nt > 0)
    def _():
        last = nt - 1
        for cp in weight_copies(texp[0], 0):
            cp.start()
        for ahead in range(GATHER_DEPTH - 1):
            issue(ahead, ahead, gsrc, gather_copies)

        def tile(t, wslot):
            slot = t % 2
            gslot = t % GATHER_DEPTH
            e = texp[t]
            first = (t == 0) | (texp[jnp.maximum(t - 1, 0)] != e)
            wslot = jnp.where(first & (t > 0), 1 - wslot, wslot)

            drain(gslot, gather_copies)

            @pl.when(t >= 2)
            def _():
                drain(slot, scatter_copies)

            @pl.when(first)
            def _():
                for cp in weight_copies(e, wslot):
                    cp.wait()
                nxt = tend[e]

                @pl.when(nxt < nt)
                def _():
                    for cp in weight_copies(texp[jnp.minimum(nxt, last)], 1 - wslot):
                        cp.start()

            x = xbuf[gslot]
            w_row = wsbuf[gslot][:, 0:1]
            a = _dot(x, st1[wslot].astype(BF16))
            b = _dot(x, st3[wslot].astype(BF16))
            y = _dot((_silu(a) * b).astype(BF16), st2[wslot].astype(BF16)) * w_row
            ybuf[slot] = y.astype(BF16)
            issue(t, slot, gdst, scatter_copies)
            ahead = t + GATHER_DEPTH - 1
            issue(ahead, ahead % GATHER_DEPTH, gsrc, gather_copies)
            return wslot

        lax.fori_loop(0, nt, tile, 0)

        for k in range(GATHER_DEPTH - 1):
            drain((nt + k) % GATHER_DEPTH, gather_copies)
        drain(last % 2, scatter_copies)

        @pl.when(nt >= 2)
        def _():
            drain(nt % 2, scatter_copies)


def _experts(gsrc, gdst, texp, tend, ntile, xy, ws, w1, w3, w2):
    smem = pl.BlockSpec(memory_space=pltpu.SMEM)
    hbm = pl.BlockSpec(memory_space=pl.ANY)
    return pl.pallas_call(
        _experts_kernel,
        in_specs=[smem, smem, smem, smem, smem, hbm, hbm, hbm, hbm, hbm],
        out_specs=hbm,
        out_shape=jax.ShapeDtypeStruct(xy.shape, xy.dtype),
        scratch_shapes=[pltpu.VMEM((GATHER_DEPTH, EXPERT_TILE, D_MODEL), BF16),
                        pltpu.VMEM((GATHER_DEPTH, EXPERT_TILE, LANES), F32),
                        pltpu.VMEM((2, EXPERT_TILE, D_MODEL), BF16),
                        pltpu.VMEM((2, D_MODEL, D_EXPERT), F32),
                        pltpu.VMEM((2, D_MODEL, D_EXPERT), F32),
                        pltpu.VMEM((2, D_EXPERT, D_MODEL), F32),
                        pltpu.SemaphoreType.DMA((GATHER_DEPTH,)),
                        pltpu.SemaphoreType.DMA((2,)),
                        pltpu.SemaphoreType.DMA((2,))],
        input_output_aliases={5: 0},
        compiler_params=pltpu.CompilerParams(vmem_limit_bytes=VMEM_LIMIT),
        name="moe_experts",
    )(gsrc, gdst, texp, tend, ntile, xy, ws, w1, w3, w2)


def _combine_kernel(y_ref, pos_ref, h1_ref, g2_ref, fg_ref, o_ref):
    pos = pos_ref[...]
    pt1, pt2 = _pair_rows(pos[:, 0:1], pos[:, 1:2])
    moe = _dot((pt1 + pt2).astype(BF16), y_ref[...])
    h = h1_ref[...] + g2_ref[...] * moe
    ms = jnp.mean(h * h, axis=-1, keepdims=True)
    o_ref[...] = h * lax.rsqrt(ms + EPS) * fg_ref[...]


def _combine(xy, pos, h1, g2, final_g):
    t = h1.shape[0]
    row = lambda i: (i, 0)
    const = lambda i: (0, 0)
    return pl.pallas_call(
        _combine_kernel,
        grid=(t // DISPATCH_BLOCK,),
        in_specs=[pl.BlockSpec((LOCAL_CAP, D_MODEL), row),
                  pl.BlockSpec((DISPATCH_BLOCK, LANES), row),
                  pl.BlockSpec((DISPATCH_BLOCK, D_MODEL), row),
                  pl.BlockSpec((1, D_MODEL), const), pl.BlockSpec((1, D_MODEL), const)],
        out_specs=pl.BlockSpec((DISPATCH_BLOCK, D_MODEL), row),
        out_shape=jax.ShapeDtypeStruct((t, D_MODEL), F32),
        compiler_params=_cparams(("arbitrary",)),
        name="moe_combine_final",
    )(xy, pos, h1, g2, final_g)


def _pad_lanes(a):
    return jnp.pad(a, ((0, 0), (0, LANES - a.shape[1])))


def kernel(x, c, ctx, c_ctx, w_mod, b_mod, norm1_g, w_in, w_conv_q, w_conv_k, gate_bias, head_norm_g, w_pool, pool_scale, w_out, norm2_g, w_group, b_group, w_router, b_router, w1, w3, w2, final_g):
    assert x.shape[0] == 1 and w_mod.shape[0] == 1
    seq = x.shape[1]
    x2d = x[0]
    ctx2d = ctx[0]

    n_main = 5 * MIX_HALF
    w_in_t = jnp.transpose(w_in[0])
    w_gate = _pad_lanes(jnp.transpose(_take_rows(w_in_t, n_main, N_GATES))).astype(BF16)
    gate_bias_row = _pad_lanes(gate_bias[0].reshape(1, N_GATES))
    wr_pack, wr_hi = _split_pack(jnp.concatenate([w_group[0], w_router[0]], axis=1))
    b_route = _pad_lanes(jnp.concatenate([b_group[0], b_router[0]]).reshape(1, -1))
    norm1 = norm1_g[0].reshape(1, D_MODEL)

    c16 = jnp.concatenate([c, c_ctx[None, :], jnp.zeros((14, D_MODEL), F32)], axis=0)
    mods = _adaln(c16, w_mod[0], b_mod[0].reshape(1, -1))
    mod_lat = mods[0].reshape(6, D_MODEL)
    mod_ctx = mods[1].reshape(6, D_MODEL)

    mod_in = jnp.concatenate([mod_ctx[0:2], mod_lat[0:2]], axis=0)
    u_pool, q, uo, k, kt, v, col, rowi = _inproj(x2d, ctx2d, mod_in, norm1, w_in_t, w_gate,
                                                 gate_bias_row, w_conv_q[0], w_conv_k[0])
    nc = seq // CHUNK
    zeros_state = (jnp.zeros((2 * HEADS, HEAD_DIM, HEAD_DIM), F32),
                   jnp.zeros((2 * HEADS, 1, HEAD_DIM), F32),
                   jnp.zeros((2 * HEADS, 1, LANES), F32))
    s0, n0, m0 = _mlstm(None, k, kt, v, col, rowi, *zeros_state, need_out=False, first=nc, nc=1)

    hf, hb = _mlstm(q, k, kt, v, col, rowi, s0, n0, m0, need_out=True, first=0, nc=nc)
    p = _pool(u_pool, w_pool[0], pool_scale[0].reshape(1, -1), seq)
    h1, fn, info = _outproj(p, hf, hb, uo, x2d, w_out[0], mod_lat, head_norm_g[0].reshape(1, -1),
                            norm2_g[0].reshape(1, -1), wr_pack, wr_hi, b_route, tm=2 * OUTPROJ_SUB)

    xs, ws, pos, cnt = _dispatch(fn, info)
    gsrc, gdst, texp, tend, ntile = _plan(cnt[::8, :N_EXPERTS], seq)
    xy = _experts(gsrc, gdst, texp, tend, ntile, xs, ws, w1[0], w3[0], w2[0])
    out = _combine(xy, pos, h1, mod_lat[5:6], final_g.reshape(1, -1))
    return out.reshape(1, seq, D_MODEL)
```

```python
import functools

import jax
import jax.numpy as jnp
from jax import lax
from jax.experimental import pallas as pl
from jax.experimental.pallas import tpu as pltpu

F32 = jnp.float32
BF16 = jnp.bfloat16

D_MODEL = 2048
GRID_W = 64
GRID_SHIFT = 6
POOL_WINDOWS = (2, 4, 8, 16)
POOL_GROUP = 256
HEADS = 4
HEAD_DIM = 256
MIX_HALF = 1024
N_GATES = 16
N_GROUPS = 4
EXPERTS_PER_GROUP = 8
N_EXPERTS = 32
D_EXPERT = 512
EPS = 1e-6
LANES = 128
CHUNK = 256
ROUTE_LANE0 = N_GROUPS

VMEM_LIMIT = 60 * 1024 * 1024


def _cparams(sem, vmem=VMEM_LIMIT):
    return pltpu.CompilerParams(dimension_semantics=sem, vmem_limit_bytes=vmem)


def _split2(x):
    hi = x.astype(BF16)
    lo = (x - hi.astype(F32)).astype(BF16)
    return hi, lo


def _split3(x):
    hi = x.astype(BF16)
    r = x - hi.astype(F32)
    mid = r.astype(BF16)
    lo = (r - mid.astype(F32)).astype(BF16)
    return hi, mid, lo


def _dot(a, b):
    return jnp.dot(a, b, preferred_element_type=F32)


SPLIT_LANE = 64


def _split_pack(w):
    hi, lo = _split2(w)
    n = w.shape[1]
    gap = jnp.zeros((w.shape[0], SPLIT_LANE - n), BF16)
    rest = jnp.zeros((w.shape[0], LANES - n), BF16)
    return jnp.concatenate([hi, gap, lo, gap], axis=1), jnp.concatenate([hi, rest], axis=1)


def _split_dot(xh, xl, w_packed, w_hi):
    r = _dot(xh, w_packed)
    return r + pltpu.roll(r, SPLIT_LANE, 1) + _dot(xl, w_hi)


def _silu(x):
    return x * jax.nn.sigmoid(x)


def _log_sigmoid(x):
    return jnp.minimum(x, 0.0) - jnp.log(1.0 + jnp.exp(-jnp.abs(x)))


def _copy_kernel(w_ref, o_ref):
    o_ref[...] = w_ref[...]


def _take_rows(w, start, n):
    return pl.pallas_call(
        _copy_kernel,
        grid=(1,),
        in_specs=[pl.BlockSpec((n, w.shape[1]), lambda i: (start // n, 0))],
        out_specs=pl.BlockSpec((n, w.shape[1]), lambda i: (0, 0)),
        out_shape=jax.ShapeDtypeStruct((n, w.shape[1]), w.dtype),
        name="take_rows",
    )(w)


def _adaln_kernel(c_ref, w_ref, b_ref, o_ref):
    a = _silu(c_ref[...])
    a3 = jnp.concatenate(_split3(a), axis=0)
    w_hi, w_lo = _split2(w_ref[...])
    acc = _dot(a3, w_hi)
    acc_lo = _dot(a3[:32], w_lo)
    out = acc[0:16] + acc[16:32] + acc[32:48] + acc_lo[0:16] + acc_lo[16:32]
    o_ref[...] = out + b_ref[...]


def _adaln(c16, w_mod, b_mod):
    n = w_mod.shape[1]
    tn = 768
    return pl.pallas_call(
        _adaln_kernel,
        grid=(n // tn,),
        in_specs=[pl.BlockSpec((16, D_MODEL), lambda j: (0, 0)),
                  pl.BlockSpec((D_MODEL, tn), lambda j: (0, j)),
                  pl.BlockSpec((1, tn), lambda j: (0, j))],
        out_specs=pl.BlockSpec((16, tn), lambda j: (0, j)),
        out_shape=jax.ShapeDtypeStruct((16, n), F32),
        compiler_params=_cparams(("arbitrary",)),
        name="adaln",
    )(c16, w_mod, b_mod)


HALO = 8
NEG_INF = float("-inf")


def _gate_scan_info(gates):
    n = gates.shape[0]
    lane = lax.broadcasted_iota(jnp.int32, gates.shape, 1)
    rows = lax.broadcasted_iota(jnp.int32, gates.shape, 0)
    lf = jnp.where(lane < N_GATES, _log_sigmoid(gates), 0.0)
    hi = lf.astype(BF16).astype(F32)
    rem = lf - hi
    mid = rem.astype(BF16).astype(F32)
    packed = (hi + pltpu.roll(mid, 32, 1) + pltpu.roll(rem - mid, 64, 1)).astype(BF16)
    r = lax.broadcasted_iota(jnp.int32, (n, n), 0)
    c = lax.broadcasted_iota(jnp.int32, (n, n), 1)
    pf = _dot(jnp.where(r >= c, 1.0, 0.0).astype(BF16), packed)
    pb = _dot(jnp.where(r <= c, 1.0, 0.0).astype(BF16), packed)
    bf = pf + pltpu.roll(pf, 96, 1) + pltpu.roll(pf, 64, 1)
    bb = pb + pltpu.roll(pb, 96, 1) + pltpu.roll(pb, 64, 1)
    b = jnp.where(lane < 8, bf, bb)
    cval = gates - pltpu.roll(b, LANES - 4, 1)
    pm = cval
    sm = cval
    step = 1
    while step < n:
        pm = jnp.maximum(pm, jnp.where(rows >= step, pltpu.roll(pm, step, 0), NEG_INF))
        sm = jnp.maximum(sm, jnp.where(rows < n - step, pltpu.roll(sm, n - step, 0), NEG_INF))
        step *= 2
    cm = jnp.where(lane < 8, pm, sm)
    return cval, jnp.where((lane & 4) == 0, cm, b)


W_CHUNK = 256


def _load_bf16(src_hbm, dst, stage, sem, *, n_chunks, transpose):
    rows = stage.shape[1]
    def chunk_copy(c, sl):
        return pltpu.make_async_copy(
            src_hbm.at[pl.ds(pl.multiple_of(c * rows, rows), rows), :], stage.at[sl], sem.at[sl])

    chunk_copy(0, 0).start()

    def convert(c, carry):
        sl = c % 2

        @pl.when(c + 1 < n_chunks)
        def _():
            chunk_copy(c + 1, 1 - sl).start()

        chunk_copy(c, sl).wait()
        span = pl.ds(pl.multiple_of(c * rows, rows), rows)
        if transpose:
            dst[:, span] = stage[sl].T.astype(BF16)
        else:
            dst[span, :] = stage[sl].astype(BF16)
        return carry

    lax.fori_loop(0, n_chunks, convert, 0)


def _inproj_kernel(x_ref, xp_ref, xn_ref, ctx_ref, mod_ref, g_ref, wt_hbm, wg_ref, gb_ref,
                   wcq_ref, wck_ref, pool_o, q_o, o_o, k_o, kt_o, v_o, col_o, row_o,
                   w_s, stage, sem, *, nt):
    i = pl.program_id(0)
    tm = x_ref.shape[0]

    @pl.when(i == 0)
    def _():
        _load_bf16(wt_hbm, w_s, stage, sem, n_chunks=w_s.shape[1] // W_CHUNK, transpose=True)

    is_ctx = i == nt
    x_main = jnp.where(is_ctx, ctx_ref[...], x_ref[...])
    x_all = jnp.concatenate([xp_ref[...], x_main, xn_ref[...]], axis=0)
    shift = jnp.where(is_ctx, mod_ref[0:1, :], mod_ref[2:3, :])
    scale = jnp.where(is_ctx, mod_ref[1:2, :], mod_ref[3:4, :])
    ms = jnp.mean(x_all * x_all, axis=-1, keepdims=True)
    y = x_all * lax.rsqrt(ms + EPS) * g_ref[...]
    xn_all = y * (1.0 + scale) + shift
    xh_all = xn_all.astype(BF16)
    xh = xn_all[HALO:HALO + tm].astype(BF16)

    def cols(ci):
        return w_s[:, ci * MIX_HALF:(ci + 1) * MIX_HALF]

    u_q = _dot(xh_all, cols(1))
    u_k = _dot(xh_all, cols(3))
    pool_o[...] = _dot(xh, cols(0))
    o_o[...] = _dot(xh, cols(2))
    v_o[...] = _dot(xh, cols(4)).astype(BF16)

    keep_prev = jnp.where(jnp.logical_or(i == 0, is_ctx), 0.0, 1.0)
    keep_next = jnp.where(jnp.logical_or(i == nt - 1, is_ctx), 0.0, 1.0)

    def conv_silu(u, wc_ref):
        n = u.shape[0]
        u = jnp.concatenate([u[:HALO] * keep_prev, u[HALO:HALO + tm], u[HALO + tm:] * keep_next], axis=0)
        up = pltpu.roll(u, 1, 0)[HALO:HALO + tm]
        un = pltpu.roll(u, n - 1, 0)[HALO:HALO + tm]
        return _silu(wc_ref[0:1, :] * up + wc_ref[1:2, :] * u[HALO:HALO + tm] + wc_ref[2:3, :] * un)

    q_o[...] = (conv_silu(u_q, wcq_ref) * (HEAD_DIM ** -0.5)).astype(BF16)
    k = conv_silu(u_k, wck_ref)
    k_o[...] = k.astype(BF16)
    kt_o[...] = k.astype(BF16).T
    gates = _dot(xh, wg_ref[...]) + gb_ref[...]
    cval, col = _gate_scan_info(gates)
    col_o[...] = col[:, :N_GATES]
    row_o[...] = cval.T[:N_GATES, :]


def _inproj(x2d, ctx2d, mod, g, w_in_t, w_gate, gate_bias_row, wcq, wck):
    t = x2d.shape[0]
    tm = CHUNK
    assert ctx2d.shape[0] == tm
    nt = t // tm
    r8 = tm // HALO
    last8 = t // HALO - 1
    n_main = 5 * MIX_HALF
    rows = t + tm
    const = lambda i: (0, 0)
    row = lambda i: (i, 0)
    f32_out = jax.ShapeDtypeStruct((rows, MIX_HALF), F32)
    bf16_out = jax.ShapeDtypeStruct((rows, MIX_HALF), BF16)
    seq = pl.BlockSpec((tm, MIX_HALF), row)
    return pl.pallas_call(
        functools.partial(_inproj_kernel, nt=nt),
        grid=(nt + 1,),
        in_specs=[pl.BlockSpec((tm, D_MODEL), lambda i: (jnp.minimum(i, nt - 1), 0)),
                  pl.BlockSpec((HALO, D_MODEL), lambda i: (jnp.clip(i * r8 - 1, 0, last8), 0)),
                  pl.BlockSpec((HALO, D_MODEL), lambda i: (jnp.minimum((i + 1) * r8, last8), 0)),
                  pl.BlockSpec((tm, D_MODEL), const),
                  pl.BlockSpec((4, D_MODEL), const),
                  pl.BlockSpec((1, D_MODEL), const),
                  pl.BlockSpec(memory_space=pl.ANY),
                  pl.BlockSpec((D_MODEL, LANES), const),
                  pl.BlockSpec((1, LANES), const),
                  pl.BlockSpec((3, MIX_HALF), const),
                  pl.BlockSpec((3, MIX_HALF), const)],
        out_specs=[seq, seq, seq, seq, pl.BlockSpec((MIX_HALF, tm), lambda i: (0, i)), seq,
                   pl.BlockSpec((tm, N_GATES), row), pl.BlockSpec((N_GATES, tm), lambda i: (0, i))],
        out_shape=[f32_out, bf16_out, f32_out, bf16_out,
                   jax.ShapeDtypeStruct((MIX_HALF, rows), BF16), bf16_out,
                   jax.ShapeDtypeStruct((rows, N_GATES), F32),
                   jax.ShapeDtypeStruct((N_GATES, rows), F32)],
        scratch_shapes=[pltpu.VMEM((D_MODEL, n_main), BF16),
                        pltpu.VMEM((2, W_CHUNK, D_MODEL), F32),
                        pltpu.SemaphoreType.DMA((2,))],
        compiler_params=_cparams(("arbitrary",)),
        name="inproj",
    )(x2d, x2d, x2d, ctx2d, mod, g, w_in_t, w_gate, gate_bias_row, wcq, wck)


def _mlstm_kernel(*refs, need_out):
    if need_out:
        (qf, kf, ktf, vf, colf, rowf, qb, kb, ktb, vb, colb, rowb, s0, n0, m0,
         hf_o, hb_o, s_s, n_s, m_s) = refs
        q_refs, h_outs = (qf, qb), (hf_o, hb_o)
    else:
        (kf, ktf, vf, colf, rowf, kb, ktb, vb, colb, rowb, s0, n0, m0,
         s_o, n_o, m_o, s_s, n_s, m_s) = refs
    k_refs, kt_refs, v_refs = (kf, kb), (ktf, ktb), (vf, vb)
    col_refs, row_refs = (colf, colb), (rowf, rowb)
    j = pl.program_id(0)

    @pl.when(j == 0)
    def _():
        s_s[...] = s0[...]
        n_s[...] = n0[...]
        m_s[...] = m0[...]

    L = CHUNK
    row = lax.broadcasted_iota(jnp.int32, (L, L), 0)
    col = lax.broadcasted_iota(jnp.int32, (L, L), 1)

    heads = [(d, h) for d in range(2) for h in range(HEADS)]

    def head_values(d, h):
        hd = d * HEADS + h
        lc, lb = 8 * d + h, 8 * d + 4 + h
        hs = slice(h * HEAD_DIM, (h + 1) * HEAD_DIM)
        edge = L - 1 if d == 0 else 0
        colv = col_refs[d][...]
        rowv = row_refs[d][...]
        v = dict(hd=hd, hs=hs, d=d)
        v["cm_c"], v["b_c"] = colv[:, lc:lc + 1], colv[:, lb:lb + 1]
        v["c_r"] = rowv[lc:lc + 1, :]
        v["g"] = v["b_c"][edge:edge + 1, :]
        v["m_old"] = m_s[hd][:, 0:1]
        v["s_old"] = s_s[hd]
        v["n_old"] = n_s[hd]
        v["m_x"] = jnp.maximum(v["m_old"], v["cm_c"][edge:edge + 1, :])
        v["decay"] = jnp.exp(v["m_old"] - v["m_x"])
        v["wk_r"] = jnp.exp(v["c_r"] - v["m_x"])
        v["v_h"] = v_refs[d][:, hs]
        v["kt_h"] = kt_refs[d][hs, :]
        return v

    def update_state(v):
        hd = v["hd"]
        n_s[hd] = v["decay"] * v["n_old"] + _dot(
            jnp.broadcast_to(v["wk_r"], (8, L)).astype(BF16), k_refs[v["d"]][:, v["hs"]])[0:1, :]
        kwt = (v["kt_h"].astype(F32) * v["wk_r"]).astype(BF16)
        s_s[hd] = v["decay"] * v["s_old"] + _dot(kwt, v["v_h"])
        m_s[hd] = jnp.broadcast_to(v["g"] + v["m_x"], (1, LANES))

    def weights(v):
        d, hs = v["d"], v["hs"]
        mask = (row >= col) if d == 0 else (row <= col)
        q_h = q_refs[d][:, hs]
        m_c = jnp.maximum(v["m_old"], v["cm_c"])
        w_inter = jnp.exp(v["m_old"] - m_c)
        v["e"] = jnp.where(mask, jnp.exp(v["c_r"] - m_c), 0.0)
        v["qw"] = (q_h.astype(F32) * w_inter).astype(BF16)
        qn = lax.dot_general(q_h, jnp.broadcast_to(v["n_old"], (8, HEAD_DIM)).astype(BF16),
                             (((1,), (1,)), ((), ())), preferred_element_type=F32)
        v["den_inter"] = w_inter * qn[:, 0:1]
        v["floor"] = jnp.exp(-(v["b_c"] + m_c))
        v["s_bf"] = v["s_old"].astype(BF16)

    def outputs(v, qk):
        p = (v["e"] * qk).astype(BF16)
        num = _dot(v["qw"], v["s_bf"]) + _dot(p, v["v_h"])
        p_sum = lax.dot_general(p, jnp.ones((8, L), BF16), (((1,), (1,)), ((), ())),
                                preferred_element_type=F32)
        den = v["den_inter"] + p_sum[:, 0:1]
        h_outs[v["d"]][:, v["hs"]] = num / jnp.maximum(jnp.abs(den), v["floor"])

    vals = [head_values(d, h) for d, h in heads]
    if need_out:
        qks = [_dot(q_refs[v["d"]][:, v["hs"]], v["kt_h"]) for v in vals]
        for v in vals:
            weights(v)
    for v in vals:
        update_state(v)
    if need_out:
        for v, qk in zip(vals, qks):
            outputs(v, qk)

    if not need_out:
        s_o[...] = s_s[...]
        n_o[...] = n_s[...]
        m_o[...] = m_s[...]


def _mlstm(q, k, kt, v, col, rowi, s0, n0, m0, need_out, first, nc):
    t = nc * CHUNK
    fwd = lambda j: (first + j, 0)
    bwd = lambda j: (first + nc - 1 - j, 0)
    fwd_t = lambda j: (0, first + j)
    bwd_t = lambda j: (0, first + nc - 1 - j)
    c3 = lambda j: (0, 0, 0)
    seq = lambda im: pl.BlockSpec((CHUNK, MIX_HALF), im)
    state_specs = [pl.BlockSpec(s0.shape, c3), pl.BlockSpec(n0.shape, c3), pl.BlockSpec(m0.shape, c3)]
    scratch = [pltpu.VMEM(s0.shape, F32), pltpu.VMEM(n0.shape, F32), pltpu.VMEM(m0.shape, F32)]

    def side(im, im_t):
        specs = ([seq(im)] if need_out else []) + [seq(im), pl.BlockSpec((MIX_HALF, CHUNK), im_t), seq(im)]
        return specs + [pl.BlockSpec((CHUNK, N_GATES), im), pl.BlockSpec((N_GATES, CHUNK), im_t)]

    in_specs = side(fwd, fwd_t) + side(bwd, bwd_t) + state_specs
    seq_in = ((q,) if need_out else ()) + (k, kt, v, col, rowi)
    args = seq_in + seq_in + (s0, n0, m0)
    if need_out:
        out_specs = [seq(fwd), seq(bwd)]
        out_shape = [jax.ShapeDtypeStruct((t, MIX_HALF), F32)] * 2
    else:
        out_specs = state_specs
        out_shape = [jax.ShapeDtypeStruct(a.shape, F32) for a in (s0, n0, m0)]
    return pl.pallas_call(
        functools.partial(_mlstm_kernel, need_out=need_out),
        grid=(nc,),
        in_specs=in_specs,
        out_specs=out_specs,
        out_shape=out_shape,
        scratch_shapes=scratch,
        compiler_params=_cparams(("arbitrary",)),
        name="mlstm_out" if need_out else "mlstm_state",
    )(*args)


POOL_PAD = 512
POOL_UNROLL = 4


def _pool_kernel(u_ref, w_ref, sc_ref, o_ref, pad_s, *, t):
    for gi, win in enumerate(POOL_WINDOWS):
        @pl.when(pl.program_id(0) == gi)
        def _():
            _pool_group(u_ref, w_ref, sc_ref, o_ref, pad_s, win=win, t=t)


def _pool_group(u_ref, w_ref, sc_ref, o_ref, pad_s, *, win, t):
    half = win // 2
    tile = 256
    zeros = jnp.zeros((POOL_PAD, POOL_GROUP), F32)
    pad_s[0:POOL_PAD, :] = zeros
    pad_s[POOL_PAD + t:POOL_PAD + t + POOL_PAD, :] = zeros

    def copy(r, carry):
        t0 = pl.multiple_of(r * tile, tile)
        pad_s[pl.ds(POOL_PAD + t0, tile), :] = u_ref[pl.ds(t0, tile), :]
        return carry

    lax.fori_loop(0, t // tile, copy, 0)

    row = lax.broadcasted_iota(jnp.int32, (tile, tile), 0)
    col = lax.broadcasted_iota(jnp.int32, (tile, tile), 1)
    same_row = (row >> GRID_SHIFT) == (col >> GRID_SHIFT)
    in_win = (col - row >= -half) & (col - row < half)
    band = jnp.where(same_row & in_win, 1.0, 0.0).astype(BF16)
    w = w_ref[...].astype(BF16)
    scale = sc_ref[...]
    n_rows = t // GRID_W

    tok0 = lax.broadcasted_iota(jnp.int32, (tile, POOL_GROUP), 0)
    gc = tok0 & (GRID_W - 1)
    inv_h = 1.0 / (jnp.minimum(gc + half, GRID_W) - jnp.maximum(gc - half, 0)).astype(F32)

    def body(r, carry):
        t0s = [pl.multiple_of((r * POOL_UNROLL + k) * tile, tile) for k in range(POOL_UNROLL)]
        pieces = []
        for t0 in t0s:
            acc = pad_s[pl.ds(POOL_PAD + t0 - GRID_W * half, tile), :]
            for dd in range(-half + 1, half):
                acc = acc + pad_s[pl.ds(POOL_PAD + t0 + GRID_W * dd, tile), :]
            gr = (t0 + tok0) >> GRID_SHIFT
            cnt_v = jnp.minimum(gr + half, n_rows) - jnp.maximum(gr - half, 0)
            pieces.append(_split2(acc / cnt_v.astype(F32)))
        means = [(_dot(band, hi) + _dot(band, lo)) * inv_h for hi, lo in pieces]
        diffs = [(m - pad_s[pl.ds(POOL_PAD + t0, tile), :]).astype(BF16) for m, t0 in zip(means, t0s)]
        for d, t0 in zip(diffs, t0s):
            o_ref[pl.ds(t0, tile), :] = (_dot(d, w) * scale).astype(BF16)
        return carry

    lax.fori_loop(0, t // (tile * POOL_UNROLL), body, 0)


def _pool(u_pool, w_pool, scale_row, t):
    return pl.pallas_call(
        functools.partial(_pool_kernel, t=t),
        grid=(len(POOL_WINDOWS),),
        in_specs=[pl.BlockSpec((t, POOL_GROUP), lambda g: (0, g)),
                  pl.BlockSpec((None, POOL_GROUP, POOL_GROUP), lambda g: (g, 0, 0)),
                  pl.BlockSpec((1, POOL_GROUP), lambda g: (0, g))],
        out_specs=pl.BlockSpec((t, POOL_GROUP), lambda g: (0, g)),
        out_shape=jax.ShapeDtypeStruct((t, MIX_HALF), BF16),
        scratch_shapes=[pltpu.VMEM((t + 2 * POOL_PAD, POOL_GROUP), F32)],
        compiler_params=_cparams(("arbitrary",)),
        name="pool_mix",
    )(u_pool, w_pool, scale_row)


OUTPROJ_SUB = 256


def _route(logits):
    lane = lax.broadcasted_iota(jnp.int32, logits.shape, 1).astype(F32)
    neg = -jnp.inf
    big = float(LANES)
    gl = jnp.where(lane < N_GROUPS, logits, neg)
    gmax = jnp.max(gl, axis=1, keepdims=True)
    gsel = jnp.min(jnp.where(gl == gmax, lane, big), axis=1, keepdims=True)
    p_grp = 1.0 / jnp.sum(jnp.exp(gl - gmax), axis=1, keepdims=True)
    lo = ROUTE_LANE0 + EXPERTS_PER_GROUP * gsel
    el = jnp.where((lane >= lo) & (lane < lo + EXPERTS_PER_GROUP), logits, neg)
    m1 = jnp.max(el, axis=1, keepdims=True)
    i1 = jnp.min(jnp.where(el == m1, lane, big), axis=1, keepdims=True)
    el2 = jnp.where(lane == i1, neg, el)
    m2 = jnp.max(el2, axis=1, keepdims=True)
    i2 = jnp.min(jnp.where(el2 == m2, lane, big), axis=1, keepdims=True)
    e2 = jnp.exp(m2 - m1)
    p1 = 1.0 / (1.0 + e2)
    p2 = e2 / (1.0 + e2)
    info = jnp.where(lane == 0.0, i1 - ROUTE_LANE0, 0.0)
    info = jnp.where(lane == 1.0, i2 - ROUTE_LANE0, info)
    info = jnp.where(lane == 2.0, p_grp * p1, info)
    return jnp.where(lane == 3.0, p_grp * p2, info)


def _outproj_kernel(p_ref, hf_ref, hb_ref, uo_ref, x_ref, wout_hbm, mod_ref,
                    hg_ref, n2g_ref, wrp_ref, wrh_ref, br_ref, h1_o, fn_o, info_o, w_s, stage, sem):
    @pl.when(pl.program_id(0) == 0)
    def _():
        _load_bf16(wout_hbm, w_s, stage, sem, n_chunks=w_s.shape[0] // stage.shape[1], transpose=False)

    def mixer_input(rs):
        h = hf_ref[rs, :] + hb_ref[rs, :]
        parts = []
        for hh in range(HEADS):
            hs = h[:, hh * HEAD_DIM:(hh + 1) * HEAD_DIM]
            mu = jnp.mean(hs, axis=-1, keepdims=True)
            ctr = hs - mu
            var = jnp.mean(ctr * ctr, axis=-1, keepdims=True)
            parts.append(ctr * lax.rsqrt(var + EPS))
        hn = jnp.concatenate(parts, axis=1) * hg_ref[...]
        m = (hn * jax.nn.sigmoid(uo_ref[rs, :])).astype(BF16)
        return jnp.concatenate([p_ref[rs, :], m], axis=1)

    def finish(rs, mix):
        h1 = x_ref[rs, :] + mod_ref[2:3, :] * mix
        h1_o[rs, :] = h1
        ms = jnp.mean(h1 * h1, axis=-1, keepdims=True)
        fn = h1 * lax.rsqrt(ms + EPS) * n2g_ref[...]
        fn = fn * (1.0 + mod_ref[4:5, :]) + mod_ref[3:4, :]
        fh, fl = _split2(fn)
        fn_o[rs, :] = fh
        logits = _split_dot(fh, fl, wrp_ref[...], wrh_ref[...]) + br_ref[...]
        info_o[rs, :] = _route(logits)

    tm = x_ref.shape[0]
    subs = [slice(r0, r0 + OUTPROJ_SUB) for r0 in range(0, tm, OUTPROJ_SUB)]
    w = w_s[...]
    mixes = [_dot(mixer_input(rs), w) for rs in subs]
    for rs, mix in zip(subs, mixes):
        finish(rs, mix)


def _outproj(p, hf, hb, uo, x2d, w_out, mod, head_g, norm2_g, wr_pack, wr_hi, b_route, tm):
    t = x2d.shape[0]
    const = lambda i: (0, 0)
    row = lambda i: (i, 0)
    in_specs = ([pl.BlockSpec((tm, MIX_HALF), row)] * 4
                + [pl.BlockSpec((tm, D_MODEL), row),
                   pl.BlockSpec(memory_space=pl.ANY),
                   pl.BlockSpec(mod.shape, const),
                   pl.BlockSpec((1, MIX_HALF), const),
                   pl.BlockSpec((1, D_MODEL), const),
                   pl.BlockSpec((D_MODEL, LANES), const),
                   pl.BlockSpec((D_MODEL, LANES), const),
                   pl.BlockSpec((1, LANES), const)])
    return pl.pallas_call(
        _outproj_kernel,
        grid=(t // tm,),
        in_specs=in_specs,
        out_specs=[pl.BlockSpec((tm, D_MODEL), row), pl.BlockSpec((tm, D_MODEL), row),
                   pl.BlockSpec((tm, LANES), row)],
        out_shape=[jax.ShapeDtypeStruct((t, D_MODEL), F32),
                   jax.ShapeDtypeStruct((t, D_MODEL), BF16),
                   jax.ShapeDtypeStruct((t, LANES), F32)],
        scratch_shapes=[pltpu.VMEM(w_out.shape, BF16),
                        pltpu.VMEM((2, W_CHUNK // 2, w_out.shape[1]), F32),
                        pltpu.SemaphoreType.DMA((2,))],
        compiler_params=_cparams(("arbitrary",)),
        name="outproj_route",
    )(p, hf, hb, uo, x2d, w_out, mod, head_g, norm2_g, wr_pack, wr_hi, b_route)


DISPATCH_BLOCK = 512
GRANULE = 16
GRANULE_SHIFT = 4
LOCAL_CAP = 1536
LOCAL_GRANULES = LOCAL_CAP // GRANULE
FREE_GRANULES = 2
EXPERT_TILE = 256
TILE_GRANULES = EXPERT_TILE // GRANULE
PLAN_UNROLL = 4
PLAN_SLACK = 8
GATHER_DEPTH = 4


def _pair_rows(pos1, pos2):
    rows = lax.broadcasted_iota(jnp.int32, (pos1.shape[0], LOCAL_CAP), 1).astype(F32)
    return jnp.where(rows == pos1, 1.0, 0.0), jnp.where(rows == pos2, 1.0, 0.0)


def _dispatch_kernel(fn_ref, info_ref, xs_o, ws_o, pos_o, cnt_o):
    tb = DISPATCH_BLOCK
    info = info_ref[...]
    e1, e2 = info[:, 0:1], info[:, 1:2]
    w1c, w2c = info[:, 2:3], info[:, 3:4]
    lane = lax.broadcasted_iota(jnp.int32, (tb, LANES), 1).astype(F32)
    o1 = jnp.where(lane == e1, 1.0, 0.0)
    o2 = jnp.where(lane == e2, 1.0, 0.0)
    onehot = o1 + o2
    cnt = jnp.sum(onehot, axis=0, keepdims=True)
    gran = jnp.floor((cnt + (GRANULE - 1)) * (1.0 / GRANULE))
    a = lax.broadcasted_iota(jnp.int32, (LANES, LANES), 0)
    b = lax.broadcasted_iota(jnp.int32, (LANES, LANES), 1)
    upper = jnp.where(a < b, 1.0, 0.0).astype(BF16)
    seg_off = _dot(jnp.broadcast_to(gran, (8, LANES)).astype(BF16), upper)[0:1, :] * GRANULE
    r = lax.broadcasted_iota(jnp.int32, (tb, tb), 0)
    c = lax.broadcasted_iota(jnp.int32, (tb, tb), 1)
    strict = jnp.where(r > c, 1.0, 0.0).astype(BF16)
    rank = _dot(strict, onehot.astype(BF16))
    slot = rank + seg_off
    pos1 = jnp.sum(o1 * slot, axis=1, keepdims=True)
    pos2 = jnp.sum(o2 * slot, axis=1, keepdims=True)
    pt1, pt2 = _pair_rows(pos1, pos2)
    pos_o[...] = jnp.where(lane == 0.0, pos1, jnp.where(lane == 1.0, pos2, 0.0))
    perm = (pt1 + pt2).T.astype(BF16)
    w_slot = jnp.sum((pt1 * w1c + pt2 * w2c).T, axis=1, keepdims=True)
    for c0 in range(0, D_MODEL, 512):
        xs_o[:, c0:c0 + 512] = _dot(perm, fn_ref[:, c0:c0 + 512]).astype(BF16)
    ws_o[...] = jnp.broadcast_to(w_slot, (LOCAL_CAP, LANES))
    cnt_o[...] = jnp.broadcast_to(cnt, (8, LANES)).astype(jnp.int32)


def _dispatch(fn, info):
    t = fn.shape[0]
    nb = t // DISPATCH_BLOCK
    row = lambda i: (i, 0)
    return pl.pallas_call(
        _dispatch_kernel,
        grid=(nb,),
        in_specs=[pl.BlockSpec((DISPATCH_BLOCK, D_MODEL), row),
                  pl.BlockSpec((DISPATCH_BLOCK, LANES), row)],
        out_specs=[pl.BlockSpec((LOCAL_CAP, D_MODEL), row),
                   pl.BlockSpec((LOCAL_CAP, LANES), row),
                   pl.BlockSpec((DISPATCH_BLOCK, LANES), row),
                   pl.BlockSpec((8, LANES), row)],
        out_shape=[jax.ShapeDtypeStruct((nb * LOCAL_CAP, D_MODEL), BF16),
                   jax.ShapeDtypeStruct((nb * LOCAL_CAP, LANES), F32),
                   jax.ShapeDtypeStruct((t, LANES), F32),
                   jax.ShapeDtypeStruct((nb * 8, LANES), jnp.int32)],
        compiler_params=_cparams(("arbitrary",)),
        name="moe_dispatch",
    )(fn, info)


def _map_len(max_tiles):
    return (max_tiles + GATHER_DEPTH - 1) * TILE_GRANULES + PLAN_SLACK


def _free_granule(q):
    return ((q // FREE_GRANULES) * LOCAL_GRANULES + (LOCAL_GRANULES - FREE_GRANULES)
            + (q % FREE_GRANULES))


def _plan_kernel(cnt_ref, gsrc_o, gdst_o, texp_o, tend_o, ntile_o, lrun, *, nb, max_tiles):
    def init(b, c):
        lrun[b] = 0
        return c

    lax.fori_loop(0, nb, init, 0)

    def per_expert(e, carry):
        g0, last_e = carry

        def per_block(b, g):
            k = (cnt_ref[b, e] + (GRANULE - 1)) >> GRANULE_SHIFT
            lo = lrun[b]
            lrun[b] = lo + k

            base = b * LOCAL_GRANULES + lo

            for j in range(PLAN_UNROLL):
                gsrc_o[g + j] = base + j
                gdst_o[g + j] = base + j

            @pl.when(k > PLAN_UNROLL)
            def _():
                def put(j, c):
                    gsrc_o[g + j] = base + j
                    gdst_o[g + j] = base + j
                    return c

                lax.fori_loop(PLAN_UNROLL, k, put, 0)

            return g + k

        g1 = lax.fori_loop(0, nb, per_block, g0, unroll=True)
        pad = (-g1) & (TILE_GRANULES - 1)

        def put_pad(j, c):
            g = g1 + j
            parity = (g // TILE_GRANULES) & 1
            gsrc_o[g] = _free_granule(0)
            gdst_o[g] = _free_granule(1 + parity * (TILE_GRANULES - 1) + (g & (TILE_GRANULES - 1)))
            return c

        lax.fori_loop(0, pad, put_pad, 0)
        g2 = g1 + pad

        def put_tile(tt, c):
            texp_o[tt] = e
            return c

        lax.fori_loop(g0 // TILE_GRANULES, g2 // TILE_GRANULES, put_tile, 0)
        tend_o[e] = g2 // TILE_GRANULES
        return g2, jnp.where(g2 > g0, e, last_e)

    g_end, last_e = lax.fori_loop(0, N_EXPERTS, per_expert, (0, 0))
    n_tiles = g_end // TILE_GRANULES
    ntile_o[0] = n_tiles

    def fill(tt, c):
        texp_o[tt] = last_e
        return c

    lax.fori_loop(n_tiles, max_tiles, fill, 0)

    def fill_map(g, c):
        gsrc_o[g] = _free_granule(0)
        gdst_o[g] = _free_granule(0)
        return c

    lax.fori_loop(g_end, _map_len(max_tiles), fill_map, 0)


def _max_tiles(t):
    nb = t // DISPATCH_BLOCK
    worst_rows = 2 * t + nb * N_EXPERTS * (GRANULE - 1) + N_EXPERTS * (EXPERT_TILE - GRANULE)
    return -(-worst_rows // EXPERT_TILE)


def _plan(cnt, t):
    nb = cnt.shape[0]
    assert nb * FREE_GRANULES >= 2 + 2 * (TILE_GRANULES - 1)
    max_tiles = _max_tiles(t)
    smem = pl.BlockSpec(memory_space=pltpu.SMEM)
    n_map = _map_len(max_tiles)
    return pl.pallas_call(
        functools.partial(_plan_kernel, nb=nb, max_tiles=max_tiles),
        in_specs=[smem],
        out_specs=[smem, smem, smem, smem, smem],
        out_shape=[jax.ShapeDtypeStruct((n_map,), jnp.int32),
                   jax.ShapeDtypeStruct((n_map,), jnp.int32),
                   jax.ShapeDtypeStruct((max_tiles,), jnp.int32),
                   jax.ShapeDtypeStruct((N_EXPERTS,), jnp.int32),
                   jax.ShapeDtypeStruct((1,), jnp.int32)],
        scratch_shapes=[pltpu.SMEM((nb,), jnp.int32)],
        name="moe_plan",
    )(cnt)


def _experts_kernel(gsrc, gdst, texp, tend, ntile, xy_in, ws_in, w1_hbm, w3_hbm, w2_hbm, xy_out,
                    xbuf, wsbuf, ybuf, st1, st3, st2, gsem, ssem, wsem):
    nt = ntile[0]

    def weight_copies(e, ws):
        return (pltpu.make_async_copy(w1_hbm.at[e], st1.at[ws], wsem.at[ws]),
                pltpu.make_async_copy(w3_hbm.at[e], st3.at[ws], wsem.at[ws]),
                pltpu.make_async_copy(w2_hbm.at[e], st2.at[ws], wsem.at[ws]))

    def rows(i):
        return pl.ds(pl.multiple_of(i * GRANULE, GRANULE), GRANULE)

    def gather_copies(g, j, sl):
        return (pltpu.make_async_copy(xy_in.at[rows(g), :], xbuf.at[sl, rows(j), :], gsem.at[sl]),
                pltpu.make_async_copy(ws_in.at[rows(g), :], wsbuf.at[sl, rows(j), :], gsem.at[sl]))

    def scatter_copies(g, j, sl):
        return (pltpu.make_async_copy(ybuf.at[sl, rows(j), :], xy_out.at[rows(g), :], ssem.at[sl]),)

    def issue(tt, sl, gmap, copies):
        for j in range(TILE_GRANULES):
            for cp in copies(gmap[tt * TILE_GRANULES + j], j, sl):
                cp.start()

    def drain(sl, copies):
        full = pl.ds(0, EXPERT_TILE)
        if copies is gather_copies:
            pltpu.make_async_copy(xy_in.at[full, :], xbuf.at[sl], gsem.at[sl]).wait()
            pltpu.make_async_copy(ws_in.at[full, :], wsbuf.at[sl], gsem.at[sl]).wait()
        else:
            pltpu.make_async_copy(ybuf.at[sl], xy_out.at[full, :], ssem.at[sl]).wait()

    @pl.when(nt > 0)
    def _():
        last = nt - 1
        for cp in weight_copies(texp[0], 0):
            cp.start()
        for ahead in range(GATHER_DEPTH - 1):
            issue(ahead, ahead, gsrc, gather_copies)

        def tile(t, wslot):
            slot = t % 2
            gslot = t % GATHER_DEPTH
            e = texp[t]
            first = (t == 0) | (texp[jnp.maximum(t - 1, 0)] != e)
            wslot = jnp.where(first & (t > 0), 1 - wslot, wslot)

            drain(gslot, gather_copies)

            @pl.when(t >= 2)
            def _():
                drain(slot, scatter_copies)

            @pl.when(first)
            def _():
                for cp in weight_copies(e, wslot):
                    cp.wait()
                nxt = tend[e]

                @pl.when(nxt < nt)
                def _():
                    for cp in weight_copies(texp[jnp.minimum(nxt, last)], 1 - wslot):
                        cp.start()

            x = xbuf[gslot]
            w_row = wsbuf[gslot][:, 0:1]
            a = _dot(x, st1[wslot].astype(BF16))
            b = _dot(x, st3[wslot].astype(BF16))
            y = _dot((_silu(a) * b).astype(BF16), st2[wslot].astype(BF16)) * w_row
            ybuf[slot] = y.astype(BF16)
            issue(t, slot, gdst, scatter_copies)
            ahead = t + GATHER_DEPTH - 1
            issue(ahead, ahead % GATHER_DEPTH, gsrc, gather_copies)
            return wslot

        lax.fori_loop(0, nt, tile, 0)

        for k in range(GATHER_DEPTH - 1):
            drain((nt + k) % GATHER_DEPTH, gather_copies)
        drain(last % 2, scatter_copies)

        @pl.when(nt >= 2)
        def _():
            drain(nt % 2, scatter_copies)


def _experts(gsrc, gdst, texp, tend, ntile, xy, ws, w1, w3, w2):
    smem = pl.BlockSpec(memory_space=pltpu.SMEM)
    hbm = pl.BlockSpec(memory_space=pl.ANY)
    return pl.pallas_call(
        _experts_kernel,
        in_specs=[smem, smem, smem, smem, smem, hbm, hbm, hbm, hbm, hbm],
        out_specs=hbm,
        out_shape=jax.ShapeDtypeStruct(xy.shape, xy.dtype),
        scratch_shapes=[pltpu.VMEM((GATHER_DEPTH, EXPERT_TILE, D_MODEL), BF16),
                        pltpu.VMEM((GATHER_DEPTH, EXPERT_TILE, LANES), F32),
                        pltpu.VMEM((2, EXPERT_TILE, D_MODEL), BF16),
                        pltpu.VMEM((2, D_MODEL, D_EXPERT), F32),
                        pltpu.VMEM((2, D_MODEL, D_EXPERT), F32),
                        pltpu.VMEM((2, D_EXPERT, D_MODEL), F32),
                        pltpu.SemaphoreType.DMA((GATHER_DEPTH,)),
                        pltpu.SemaphoreType.DMA((2,)),
                        pltpu.SemaphoreType.DMA((2,))],
        input_output_aliases={5: 0},
        compiler_params=pltpu.CompilerParams(vmem_limit_bytes=VMEM_LIMIT),
        name="moe_experts",
    )(gsrc, gdst, texp, tend, ntile, xy, ws, w1, w3, w2)


def _combine_kernel(y_ref, pos_ref, h1_ref, g2_ref, fg_ref, o_ref):
    pos = pos_ref[...]
    pt1, pt2 = _pair_rows(pos[:, 0:1], pos[:, 1:2])
    moe = _dot((pt1 + pt2).astype(BF16), y_ref[...])
    h = h1_ref[...] + g2_ref[...] * moe
    ms = jnp.mean(h * h, axis=-1, keepdims=True)
    o_ref[...] = h * lax.rsqrt(ms + EPS) * fg_ref[...]


def _combine(xy, pos, h1, g2, final_g):
    t = h1.shape[0]
    row = lambda i: (i, 0)
    const = lambda i: (0, 0)
    return pl.pallas_call(
        _combine_kernel,
        grid=(t // DISPATCH_BLOCK,),
        in_specs=[pl.BlockSpec((LOCAL_CAP, D_MODEL), row),
                  pl.BlockSpec((DISPATCH_BLOCK, LANES), row),
                  pl.BlockSpec((DISPATCH_BLOCK, D_MODEL), row),
                  pl.BlockSpec((1, D_MODEL), const), pl.BlockSpec((1, D_MODEL), const)],
        out_specs=pl.BlockSpec((DISPATCH_BLOCK, D_MODEL), row),
        out_shape=jax.ShapeDtypeStruct((t, D_MODEL), F32),
        compiler_params=_cparams(("arbitrary",)),
        name="moe_combine_final",
    )(xy, pos, h1, g2, final_g)


def _pad_lanes(a):
    return jnp.pad(a, ((0, 0), (0, LANES - a.shape[1])))


def kernel(x, c, ctx, c_ctx, w_mod, b_mod, norm1_g, w_in, w_conv_q, w_conv_k, gate_bias, head_norm_g, w_pool, pool_scale, w_out, norm2_g, w_group, b_group, w_router, b_router, w1, w3, w2, final_g):
    assert x.shape[0] == 1 and w_mod.shape[0] == 1
    seq = x.shape[1]
    x2d = x[0]
    ctx2d = ctx[0]

    n_main = 5 * MIX_HALF
    w_in_t = jnp.transpose(w_in[0])
    w_gate = _pad_lanes(jnp.transpose(_take_rows(w_in_t, n_main, N_GATES))).astype(BF16)
    gate_bias_row = _pad_lanes(gate_bias[0].reshape(1, N_GATES))
    wr_pack, wr_hi = _split_pack(jnp.concatenate([w_group[0], w_router[0]], axis=1))
    b_route = _pad_lanes(jnp.concatenate([b_group[0], b_router[0]]).reshape(1, -1))
    norm1 = norm1_g[0].reshape(1, D_MODEL)

    c16 = jnp.concatenate([c, c_ctx[None, :], jnp.zeros((14, D_MODEL), F32)], axis=0)
    mods = _adaln(c16, w_mod[0], b_mod[0].reshape(1, -1))
    mod_lat = mods[0].reshape(6, D_MODEL)
    mod_ctx = mods[1].reshape(6, D_MODEL)

    mod_in = jnp.concatenate([mod_ctx[0:2], mod_lat[0:2]], axis=0)
    u_pool, q, uo, k, kt, v, col, rowi = _inproj(x2d, ctx2d, mod_in, norm1, w_in_t, w_gate,
                                                 gate_bias_row, w_conv_q[0], w_conv_k[0])
    nc = seq // CHUNK
    zeros_state = (jnp.zeros((2 * HEADS, HEAD_DIM, HEAD_DIM), F32),
                   jnp.zeros((2 * HEADS, 1, HEAD_DIM), F32),
                   jnp.zeros((2 * HEADS, 1, LANES), F32))
    s0, n0, m0 = _mlstm(None, k, kt, v, col, rowi, *zeros_state, need_out=False, first=nc, nc=1)

    hf, hb = _mlstm(q, k, kt, v, col, rowi, s0, n0, m0, need_out=True, first=0, nc=nc)
    p = _pool(u_pool, w_pool[0], pool_scale[0].reshape(1, -1), seq)
    h1, fn, info = _outproj(p, hf, hb, uo, x2d, w_out[0], mod_lat, head_norm_g[0].reshape(1, -1),
                            norm2_g[0].reshape(1, -1), wr_pack, wr_hi, b_route, tm=2 * OUTPROJ_SUB)

    xs, ws, pos, cnt = _dispatch(fn, info)
    gsrc, gdst, texp, tend, ntile = _plan(cnt[::8, :N_EXPERTS], seq)
    xy = _experts(gsrc, gdst, texp, tend, ntile, xs, ws, w1[0], w3[0], w2[0])
    out = _combine(xy, pos, h1, mod_lat[5:6], final_g.reshape(1, -1))
    return out.reshape(1, seq, D_MODEL)
```

```python
import functools

import jax
import jax.numpy as jnp
from jax import lax
from jax.experimental import pallas as pl
from jax.experimental.pallas import tpu as pltpu

F32 = jnp.float32
BF16 = jnp.bfloat16

D_MODEL = 2048
GRID_W = 64
GRID_SHIFT = 6
POOL_WINDOWS = (2, 4, 8, 16)
POOL_GROUP = 256
HEADS = 4
HEAD_DIM = 256
MIX_HALF = 1024
N_GATES = 16
N_GROUPS = 4
EXPERTS_PER_GROUP = 8
N_EXPERTS = 32
D_EXPERT = 512
EPS = 1e-6
LANES = 128
CHUNK = 256
ROUTE_LANE0 = N_GROUPS

VMEM_LIMIT = 60 * 1024 * 1024


def _cparams(sem, vmem=VMEM_LIMIT):
    return pltpu.CompilerParams(dimension_semantics=sem, vmem_limit_bytes=vmem)


def _split2(x):
    hi = x.astype(BF16)
    lo = (x - hi.astype(F32)).astype(BF16)
    return hi, lo


def _split3(x):
    hi = x.astype(BF16)
    r = x - hi.astype(F32)
    mid = r.astype(BF16)
    lo = (r - mid.astype(F32)).astype(BF16)
    return hi, mid, lo


def _dot(a, b):
    return jnp.dot(a, b, preferred_element_type=F32)


SPLIT_LANE = 64


def _split_pack(w):
    hi, lo = _split2(w)
    n = w.shape[1]
    gap = jnp.zeros((w.shape[0], SPLIT_LANE - n), BF16)
    rest = jnp.zeros((w.shape[0], LANES - n), BF16)
    return jnp.concatenate([hi, gap, lo, gap], axis=1), jnp.concatenate([hi, rest], axis=1)


def _split_dot(xh, xl, w_packed, w_hi):
    r = _dot(xh, w_packed)
    return r + pltpu.roll(r, SPLIT_LANE, 1) + _dot(xl, w_hi)


def _silu(x):
    return x * jax.nn.sigmoid(x)


def _log_sigmoid(x):
    return jnp.minimum(x, 0.0) - jnp.log(1.0 + jnp.exp(-jnp.abs(x)))


def _copy_kernel(w_ref, o_ref):
    o_ref[...] = w_ref[...]


def _take_rows(w, start, n):
    return pl.pallas_call(
        _copy_kernel,
        grid=(1,),
        in_specs=[pl.BlockSpec((n, w.shape[1]), lambda i: (start // n, 0))],
        out_specs=pl.BlockSpec((n, w.shape[1]), lambda i: (0, 0)),
        out_shape=jax.ShapeDtypeStruct((n, w.shape[1]), w.dtype),
        name="take_rows",
    )(w)


def _adaln_kernel(c_ref, w_ref, b_ref, o_ref):
    a = _silu(c_ref[...])
    a3 = jnp.concatenate(_split3(a), axis=0)
    w_hi, w_lo = _split2(w_ref[...])
    acc = _dot(a3, w_hi)
    acc_lo = _dot(a3[:32], w_lo)
    out = acc[0:16] + acc[16:32] + acc[32:48] + acc_lo[0:16] + acc_lo[16:32]
    o_ref[...] = out + b_ref[...]


ADALN_COLS = 2048


def _adaln(c16, w_mod, b_mod):
    n = w_mod.shape[1]
    tn = ADALN_COLS
    return pl.pallas_call(
        _adaln_kernel,
        grid=(n // tn,),
        in_specs=[pl.BlockSpec((16, D_MODEL), lambda j: (0, 0)),
                  pl.BlockSpec((D_MODEL, tn), lambda j: (0, j)),
                  pl.BlockSpec((1, tn), lambda j: (0, j))],
        out_specs=pl.BlockSpec((16, tn), lambda j: (0, j)),
        out_shape=jax.ShapeDtypeStruct((16, n), F32),
        compiler_params=_cparams(("arbitrary",)),
        name="adaln",
    )(c16, w_mod, b_mod)


HALO = 8
NEG_INF = float("-inf")


def _gate_scan_info(gates):
    n = gates.shape[0]
    lane = lax.broadcasted_iota(jnp.int32, gates.shape, 1)
    rows = lax.broadcasted_iota(jnp.int32, gates.shape, 0)
    lf = jnp.where(lane < N_GATES, _log_sigmoid(gates), 0.0)
    hi = lf.astype(BF16).astype(F32)
    rem = lf - hi
    mid = rem.astype(BF16).astype(F32)
    packed = (hi + pltpu.roll(mid, 32, 1) + pltpu.roll(rem - mid, 64, 1)).astype(BF16)
    r = lax.broadcasted_iota(jnp.int32, (n, n), 0)
    c = lax.broadcasted_iota(jnp.int32, (n, n), 1)
    pf = _dot(jnp.where(r >= c, 1.0, 0.0).astype(BF16), packed)
    pb = _dot(jnp.where(r <= c, 1.0, 0.0).astype(BF16), packed)
    bf = pf + pltpu.roll(pf, 96, 1) + pltpu.roll(pf, 64, 1)
    bb = pb + pltpu.roll(pb, 96, 1) + pltpu.roll(pb, 64, 1)
    b = jnp.where(lane < 8, bf, bb)
    cval = gates - pltpu.roll(b, LANES - 4, 1)
    pm = cval
    sm = cval
    step = 1
    while step < n:
        pm = jnp.maximum(pm, jnp.where(rows >= step, pltpu.roll(pm, step, 0), NEG_INF))
        sm = jnp.maximum(sm, jnp.where(rows < n - step, pltpu.roll(sm, n - step, 0), NEG_INF))
        step *= 2
    cm = jnp.where(lane < 8, pm, sm)
    return cval, jnp.where((lane & 4) == 0, cm, b)


W_CHUNK = 256


def _load_bf16(src_hbm, dst, stage, sem, *, n_chunks, transpose):
    rows = stage.shape[1]
    def chunk_copy(c, sl):
        return pltpu.make_async_copy(
            src_hbm.at[pl.ds(pl.multiple_of(c * rows, rows), rows), :], stage.at[sl], sem.at[sl])

    chunk_copy(0, 0).start()

    def convert(c, carry):
        sl = c % 2

        @pl.when(c + 1 < n_chunks)
        def _():
            chunk_copy(c + 1, 1 - sl).start()

        chunk_copy(c, sl).wait()
        span = pl.ds(pl.multiple_of(c * rows, rows), rows)
        if transpose:
            dst[:, span] = stage[sl].T.astype(BF16)
        else:
            dst[span, :] = stage[sl].astype(BF16)
        return carry

    lax.fori_loop(0, n_chunks, convert, 0)


def _inproj_kernel(x_ref, xp_ref, xn_ref, ctx_ref, mod_ref, g_ref, wt_hbm, wg_ref, gb_ref,
                   wcq_ref, wck_ref, pool_o, q_o, o_o, k_o, kt_o, v_o, col_o, row_o,
                   w_s, stage, sem, *, nt):
    i = pl.program_id(0)
    tm = x_ref.shape[0]

    @pl.when(i == 0)
    def _():
        _load_bf16(wt_hbm, w_s, stage, sem, n_chunks=w_s.shape[1] // W_CHUNK, transpose=True)

    is_ctx = i == nt
    x_main = jnp.where(is_ctx, ctx_ref[...], x_ref[...])
    x_all = jnp.concatenate([xp_ref[...], x_main, xn_ref[...]], axis=0)
    shift = jnp.where(is_ctx, mod_ref[0:1, :], mod_ref[2:3, :])
    scale = jnp.where(is_ctx, mod_ref[1:2, :], mod_ref[3:4, :])
    ms = jnp.mean(x_all * x_all, axis=-1, keepdims=True)
    y = x_all * lax.rsqrt(ms + EPS) * g_ref[...]
    xn_all = y * (1.0 + scale) + shift
    xh_all = xn_all.astype(BF16)
    xh = xn_all[HALO:HALO + tm].astype(BF16)

    def cols(ci):
        return w_s[:, ci * MIX_HALF:(ci + 1) * MIX_HALF]

    u_q = _dot(xh_all, cols(1))
    u_k = _dot(xh_all, cols(3))
    pool_o[...] = _dot(xh, cols(0))
    o_o[...] = _dot(xh, cols(2))
    v_o[...] = _dot(xh, cols(4)).astype(BF16)

    keep_prev = jnp.where(jnp.logical_or(i == 0, is_ctx), 0.0, 1.0)
    keep_next = jnp.where(jnp.logical_or(i == nt - 1, is_ctx), 0.0, 1.0)

    def conv_silu(u, wc_ref):
        n = u.shape[0]
        u = jnp.concatenate([u[:HALO] * keep_prev, u[HALO:HALO + tm], u[HALO + tm:] * keep_next], axis=0)
        up = pltpu.roll(u, 1, 0)[HALO:HALO + tm]
        un = pltpu.roll(u, n - 1, 0)[HALO:HALO + tm]
        return _silu(wc_ref[0:1, :] * up + wc_ref[1:2, :] * u[HALO:HALO + tm] + wc_ref[2:3, :] * un)

    q_o[...] = (conv_silu(u_q, wcq_ref) * (HEAD_DIM ** -0.5)).astype(BF16)
    k = conv_silu(u_k, wck_ref)
    k_o[...] = k.astype(BF16)
    kt_o[...] = k.astype(BF16).T
    gates = _dot(xh, wg_ref[...]) + gb_ref[...]
    cval, col = _gate_scan_info(gates)
    col_o[...] = col[:, :N_GATES]
    row_o[...] = cval.T[:N_GATES, :]


def _inproj(x2d, ctx2d, mod, g, w_in_t, w_gate, gate_bias_row, wcq, wck):
    t = x2d.shape[0]
    tm = CHUNK
    assert ctx2d.shape[0] == tm
    nt = t // tm
    r8 = tm // HALO
    last8 = t // HALO - 1
    n_main = 5 * MIX_HALF
    rows = t + tm
    const = lambda i: (0, 0)
    row = lambda i: (i, 0)
    f32_out = jax.ShapeDtypeStruct((rows, MIX_HALF), F32)
    bf16_out = jax.ShapeDtypeStruct((rows, MIX_HALF), BF16)
    seq = pl.BlockSpec((tm, MIX_HALF), row)
    return pl.pallas_call(
        functools.partial(_inproj_kernel, nt=nt),
        grid=(nt + 1,),
        in_specs=[pl.BlockSpec((tm, D_MODEL), lambda i: (jnp.minimum(i, nt - 1), 0)),
                  pl.BlockSpec((HALO, D_MODEL), lambda i: (jnp.clip(i * r8 - 1, 0, last8), 0)),
                  pl.BlockSpec((HALO, D_MODEL), lambda i: (jnp.minimum((i + 1) * r8, last8), 0)),
                  pl.BlockSpec((tm, D_MODEL), const),
                  pl.BlockSpec((4, D_MODEL), const),
                  pl.BlockSpec((1, D_MODEL), const),
                  pl.BlockSpec(memory_space=pl.ANY),
                  pl.BlockSpec((D_MODEL, LANES), const),
                  pl.BlockSpec((1, LANES), const),
                  pl.BlockSpec((3, MIX_HALF), const),
                  pl.BlockSpec((3, MIX_HALF), const)],
        out_specs=[seq, seq, seq, seq, pl.BlockSpec((MIX_HALF, tm), lambda i: (0, i)), seq,
                   pl.BlockSpec((tm, N_GATES), row), pl.BlockSpec((N_GATES, tm), lambda i: (0, i))],
        out_shape=[f32_out, bf16_out, f32_out, bf16_out,
                   jax.ShapeDtypeStruct((MIX_HALF, rows), BF16), bf16_out,
                   jax.ShapeDtypeStruct((rows, N_GATES), F32),
                   jax.ShapeDtypeStruct((N_GATES, rows), F32)],
        scratch_shapes=[pltpu.VMEM((D_MODEL, n_main), BF16),
                        pltpu.VMEM((2, W_CHUNK, D_MODEL), F32),
                        pltpu.SemaphoreType.DMA((2,))],
        compiler_params=_cparams(("arbitrary",)),
        name="inproj",
    )(x2d, x2d, x2d, ctx2d, mod, g, w_in_t, w_gate, gate_bias_row, wcq, wck)


def _mlstm_kernel(*refs, need_out):
    if need_out:
        (qf, kf, ktf, vf, colf, rowf, qb, kb, ktb, vb, colb, rowb, s0, n0, m0,
         hf_o, hb_o, s_s, n_s, m_s) = refs
        q_refs, h_outs = (qf, qb), (hf_o, hb_o)
    else:
        (kf, ktf, vf, colf, rowf, kb, ktb, vb, colb, rowb, s0, n0, m0,
         s_o, n_o, m_o, s_s, n_s, m_s) = refs
    k_refs, kt_refs, v_refs = (kf, kb), (ktf, ktb), (vf, vb)
    col_refs, row_refs = (colf, colb), (rowf, rowb)
    j = pl.program_id(0)

    @pl.when(j == 0)
    def _():
        s_s[...] = s0[...]
        n_s[...] = n0[...]
        m_s[...] = m0[...]

    L = CHUNK
    row = lax.broadcasted_iota(jnp.int32, (L, L), 0)
    col = lax.broadcasted_iota(jnp.int32, (L, L), 1)

    heads = [(d, h) for d in range(2) for h in range(HEADS)]

    def head_values(d, h):
        hd = d * HEADS + h
        lc, lb = 8 * d + h, 8 * d + 4 + h
        hs = slice(h * HEAD_DIM, (h + 1) * HEAD_DIM)
        edge = L - 1 if d == 0 else 0
        colv = col_refs[d][...]
        rowv = row_refs[d][...]
        v = dict(hd=hd, hs=hs, d=d)
        v["cm_c"], v["b_c"] = colv[:, lc:lc + 1], colv[:, lb:lb + 1]
        v["c_r"] = rowv[lc:lc + 1, :]
        v["g"] = v["b_c"][edge:edge + 1, :]
        v["m_old"] = m_s[hd][:, 0:1]
        v["s_old"] = s_s[hd]
        v["n_old"] = n_s[hd]
        v["m_x"] = jnp.maximum(v["m_old"], v["cm_c"][edge:edge + 1, :])
        v["decay"] = jnp.exp(v["m_old"] - v["m_x"])
        v["wk_r"] = jnp.exp(v["c_r"] - v["m_x"])
        v["v_h"] = v_refs[d][:, hs]
        v["kt_h"] = kt_refs[d][hs, :]
        return v

    def update_state(v):
        hd = v["hd"]
        n_s[hd] = v["decay"] * v["n_old"] + _dot(
            jnp.broadcast_to(v["wk_r"], (8, L)).astype(BF16), k_refs[v["d"]][:, v["hs"]])[0:1, :]
        kwt = (v["kt_h"].astype(F32) * v["wk_r"]).astype(BF16)
        s_s[hd] = v["decay"] * v["s_old"] + _dot(kwt, v["v_h"])
        m_s[hd] = jnp.broadcast_to(v["g"] + v["m_x"], (1, LANES))

    def weights(v):
        d, hs = v["d"], v["hs"]
        mask = (row >= col) if d == 0 else (row <= col)
        q_h = q_refs[d][:, hs]
        m_c = jnp.maximum(v["m_old"], v["cm_c"])
        w_inter = jnp.exp(v["m_old"] - m_c)
        v["e"] = jnp.where(mask, jnp.exp(v["c_r"] - m_c), 0.0)
        v["qw"] = (q_h.astype(F32) * w_inter).astype(BF16)
        qn = lax.dot_general(q_h, jnp.broadcast_to(v["n_old"], (8, HEAD_DIM)).astype(BF16),
                             (((1,), (1,)), ((), ())), preferred_element_type=F32)
        v["den_inter"] = w_inter * qn[:, 0:1]
        v["floor"] = jnp.exp(-(v["b_c"] + m_c))
        v["s_bf"] = v["s_old"].astype(BF16)

    def outputs(v, qk):
        p = (v["e"] * qk).astype(BF16)
        num = _dot(v["qw"], v["s_bf"]) + _dot(p, v["v_h"])
        p_sum = lax.dot_general(p, jnp.ones((8, L), BF16), (((1,), (1,)), ((), ())),
                                preferred_element_type=F32)
        den = v["den_inter"] + p_sum[:, 0:1]
        h_outs[v["d"]][:, v["hs"]] = num / jnp.maximum(jnp.abs(den), v["floor"])

    vals = [head_values(d, h) for d, h in heads]
    if need_out:
        qks = [_dot(q_refs[v["d"]][:, v["hs"]], v["kt_h"]) for v in vals]
        for v in vals:
            weights(v)
    for v in vals:
        update_state(v)
    if need_out:
        for v, qk in zip(vals, qks):
            outputs(v, qk)

    if not need_out:
        s_o[...] = s_s[...]
        n_o[...] = n_s[...]
        m_o[...] = m_s[...]


def _mlstm(q, k, kt, v, col, rowi, s0, n0, m0, need_out, first, nc):
    t = nc * CHUNK
    fwd = lambda j: (first + j, 0)
    bwd = lambda j: (first + nc - 1 - j, 0)
    fwd_t = lambda j: (0, first + j)
    bwd_t = lambda j: (0, first + nc - 1 - j)
    c3 = lambda j: (0, 0, 0)
    seq = lambda im: pl.BlockSpec((CHUNK, MIX_HALF), im)
    state_specs = [pl.BlockSpec(s0.shape, c3), pl.BlockSpec(n0.shape, c3), pl.BlockSpec(m0.shape, c3)]
    scratch = [pltpu.VMEM(s0.shape, F32), pltpu.VMEM(n0.shape, F32), pltpu.VMEM(m0.shape, F32)]

    def side(im, im_t):
        specs = ([seq(im)] if need_out else []) + [seq(im), pl.BlockSpec((MIX_HALF, CHUNK), im_t), seq(im)]
        return specs + [pl.BlockSpec((CHUNK, N_GATES), im), pl.BlockSpec((N_GATES, CHUNK), im_t)]

    in_specs = side(fwd, fwd_t) + side(bwd, bwd_t) + state_specs
    seq_in = ((q,) if need_out else ()) + (k, kt, v, col, rowi)
    args = seq_in + seq_in + (s0, n0, m0)
    if need_out:
        out_specs = [seq(fwd), seq(bwd)]
        out_shape = [jax.ShapeDtypeStruct((t, MIX_HALF), F32)] * 2
    else:
        out_specs = state_specs
        out_shape = [jax.ShapeDtypeStruct(a.shape, F32) for a in (s0, n0, m0)]
    return pl.pallas_call(
        functools.partial(_mlstm_kernel, need_out=need_out),
        grid=(nc,),
        in_specs=in_specs,
        out_specs=out_specs,
        out_shape=out_shape,
        scratch_shapes=scratch,
        compiler_params=_cparams(("arbitrary",)),
        name="mlstm_out" if need_out else "mlstm_state",
    )(*args)


POOL_PAD = 512
POOL_UNROLL = 4


def _pool_kernel(u_ref, w_ref, sc_ref, o_ref, pad_s, *, t):
    for gi, win in enumerate(POOL_WINDOWS):
        @pl.when(pl.program_id(0) == gi)
        def _():
            _pool_group(u_ref, w_ref, sc_ref, o_ref, pad_s, win=win, t=t)


def _pool_group(u_ref, w_ref, sc_ref, o_ref, pad_s, *, win, t):
    half = win // 2
    tile = 256
    zeros = jnp.zeros((POOL_PAD, POOL_GROUP), F32)
    pad_s[0:POOL_PAD, :] = zeros
    pad_s[POOL_PAD + t:POOL_PAD + t + POOL_PAD, :] = zeros

    def copy(r, carry):
        t0 = pl.multiple_of(r * tile, tile)
        pad_s[pl.ds(POOL_PAD + t0, tile), :] = u_ref[pl.ds(t0, tile), :]
        return carry

    lax.fori_loop(0, t // tile, copy, 0)

    row = lax.broadcasted_iota(jnp.int32, (tile, tile), 0)
    col = lax.broadcasted_iota(jnp.int32, (tile, tile), 1)
    same_row = (row >> GRID_SHIFT) == (col >> GRID_SHIFT)
    in_win = (col - row >= -half) & (col - row < half)
    band = jnp.where(same_row & in_win, 1.0, 0.0).astype(BF16)
    w = w_ref[...].astype(BF16)
    scale = sc_ref[...]
    n_rows = t // GRID_W

    tok0 = lax.broadcasted_iota(jnp.int32, (tile, POOL_GROUP), 0)
    gc = tok0 & (GRID_W - 1)
    inv_h = 1.0 / (jnp.minimum(gc + half, GRID_W) - jnp.maximum(gc - half, 0)).astype(F32)

    def body(r, carry):
        t0s = [pl.multiple_of((r * POOL_UNROLL + k) * tile, tile) for k in range(POOL_UNROLL)]
        pieces = []
        for t0 in t0s:
            acc = pad_s[pl.ds(POOL_PAD + t0 - GRID_W * half, tile), :]
            for dd in range(-half + 1, half):
                acc = acc + pad_s[pl.ds(POOL_PAD + t0 + GRID_W * dd, tile), :]
            gr = (t0 + tok0) >> GRID_SHIFT
            cnt_v = jnp.minimum(gr + half, n_rows) - jnp.maximum(gr - half, 0)
            pieces.append(_split2(acc / cnt_v.astype(F32)))
        means = [(_dot(band, hi) + _dot(band, lo)) * inv_h for hi, lo in pieces]
        diffs = [(m - pad_s[pl.ds(POOL_PAD + t0, tile), :]).astype(BF16) for m, t0 in zip(means, t0s)]
        for d, t0 in zip(diffs, t0s):
            o_ref[pl.ds(t0, tile), :] = (_dot(d, w) * scale).astype(BF16)
        return carry

    lax.fori_loop(0, t // (tile * POOL_UNROLL), body, 0)


def _pool(u_pool, w_pool, scale_row, t):
    return pl.pallas_call(
        functools.partial(_pool_kernel, t=t),
        grid=(len(POOL_WINDOWS),),
        in_specs=[pl.BlockSpec((t, POOL_GROUP), lambda g: (0, g)),
                  pl.BlockSpec((None, POOL_GROUP, POOL_GROUP), lambda g: (g, 0, 0)),
                  pl.BlockSpec((1, POOL_GROUP), lambda g: (0, g))],
        out_specs=pl.BlockSpec((t, POOL_GROUP), lambda g: (0, g)),
        out_shape=jax.ShapeDtypeStruct((t, MIX_HALF), BF16),
        scratch_shapes=[pltpu.VMEM((t + 2 * POOL_PAD, POOL_GROUP), F32)],
        compiler_params=_cparams(("arbitrary",)),
        name="pool_mix",
    )(u_pool, w_pool, scale_row)


OUTPROJ_SUB = 256


def _route(logits):
    lane = lax.broadcasted_iota(jnp.int32, logits.shape, 1).astype(F32)
    neg = -jnp.inf
    big = float(LANES)
    gl = jnp.where(lane < N_GROUPS, logits, neg)
    gmax = jnp.max(gl, axis=1, keepdims=True)
    gsel = jnp.min(jnp.where(gl == gmax, lane, big), axis=1, keepdims=True)
    p_grp = 1.0 / jnp.sum(jnp.exp(gl - gmax), axis=1, keepdims=True)
    lo = ROUTE_LANE0 + EXPERTS_PER_GROUP * gsel
    el = jnp.where((lane >= lo) & (lane < lo + EXPERTS_PER_GROUP), logits, neg)
    m1 = jnp.max(el, axis=1, keepdims=True)
    i1 = jnp.min(jnp.where(el == m1, lane, big), axis=1, keepdims=True)
    el2 = jnp.where(lane == i1, neg, el)
    m2 = jnp.max(el2, axis=1, keepdims=True)
    i2 = jnp.min(jnp.where(el2 == m2, lane, big), axis=1, keepdims=True)
    e2 = jnp.exp(m2 - m1)
    p1 = 1.0 / (1.0 + e2)
    p2 = e2 / (1.0 + e2)
    info = jnp.where(lane == 0.0, i1 - ROUTE_LANE0, 0.0)
    info = jnp.where(lane == 1.0, i2 - ROUTE_LANE0, info)
    info = jnp.where(lane == 2.0, p_grp * p1, info)
    return jnp.where(lane == 3.0, p_grp * p2, info)


def _outproj_kernel(p_ref, hf_ref, hb_ref, uo_ref, x_ref, wout_hbm, mod_ref,
                    hg_ref, n2g_ref, wrp_ref, wrh_ref, br_ref, h1_o, fn_o, info_o, w_s, stage, sem):
    @pl.when(pl.program_id(0) == 0)
    def _():
        _load_bf16(wout_hbm, w_s, stage, sem, n_chunks=w_s.shape[0] // stage.shape[1], transpose=False)

    def mixer_input(rs):
        h = hf_ref[rs, :] + hb_ref[rs, :]
        parts = []
        for hh in range(HEADS):
            hs = h[:, hh * HEAD_DIM:(hh + 1) * HEAD_DIM]
            mu = jnp.mean(hs, axis=-1, keepdims=True)
            ctr = hs - mu
            var = jnp.mean(ctr * ctr, axis=-1, keepdims=True)
            parts.append(ctr * lax.rsqrt(var + EPS))
        hn = jnp.concatenate(parts, axis=1) * hg_ref[...]
        m = (hn * jax.nn.sigmoid(uo_ref[rs, :])).astype(BF16)
        return jnp.concatenate([p_ref[rs, :], m], axis=1)

    def finish(rs, mix):
        h1 = x_ref[rs, :] + mod_ref[2:3, :] * mix
        h1_o[rs, :] = h1
        ms = jnp.mean(h1 * h1, axis=-1, keepdims=True)
        fn = h1 * lax.rsqrt(ms + EPS) * n2g_ref[...]
        fn = fn * (1.0 + mod_ref[4:5, :]) + mod_ref[3:4, :]
        fh, fl = _split2(fn)
        fn_o[rs, :] = fh
        logits = _split_dot(fh, fl, wrp_ref[...], wrh_ref[...]) + br_ref[...]
        info_o[rs, :] = _route(logits)

    tm = x_ref.shape[0]
    subs = [slice(r0, r0 + OUTPROJ_SUB) for r0 in range(0, tm, OUTPROJ_SUB)]
    w = w_s[...]
    mixes = [_dot(mixer_input(rs), w) for rs in subs]
    for rs, mix in zip(subs, mixes):
        finish(rs, mix)


def _outproj(p, hf, hb, uo, x2d, w_out, mod, head_g, norm2_g, wr_pack, wr_hi, b_route, tm):
    t = x2d.shape[0]
    const = lambda i: (0, 0)
    row = lambda i: (i, 0)
    in_specs = ([pl.BlockSpec((tm, MIX_HALF), row)] * 4
                + [pl.BlockSpec((tm, D_MODEL), row),
                   pl.BlockSpec(memory_space=pl.ANY),
                   pl.BlockSpec(mod.shape, const),
                   pl.BlockSpec((1, MIX_HALF), const),
                   pl.BlockSpec((1, D_MODEL), const),
                   pl.BlockSpec((D_MODEL, LANES), const),
                   pl.BlockSpec((D_MODEL, LANES), const),
                   pl.BlockSpec((1, LANES), const)])
    return pl.pallas_call(
        _outproj_kernel,
        grid=(t // tm,),
        in_specs=in_specs,
        out_specs=[pl.BlockSpec((tm, D_MODEL), row), pl.BlockSpec((tm, D_MODEL), row),
                   pl.BlockSpec((tm, LANES), row)],
        out_shape=[jax.ShapeDtypeStruct((t, D_MODEL), F32),
                   jax.ShapeDtypeStruct((t, D_MODEL), BF16),
                   jax.ShapeDtypeStruct((t, LANES), F32)],
        scratch_shapes=[pltpu.VMEM(w_out.shape, BF16),
                        pltpu.VMEM((2, W_CHUNK // 2, w_out.shape[1]), F32),
                        pltpu.SemaphoreType.DMA((2,))],
        compiler_params=_cparams(("arbitrary",)),
        name="outproj_route",
    )(p, hf, hb, uo, x2d, w_out, mod, head_g, norm2_g, wr_pack, wr_hi, b_route)


DISPATCH_BLOCK = 512
GRANULE = 16
GRANULE_SHIFT = 4
LOCAL_CAP = 1536
LOCAL_GRANULES = LOCAL_CAP // GRANULE
FREE_GRANULES = 2
EXPERT_TILE = 256
TILE_GRANULES = EXPERT_TILE // GRANULE
PLAN_UNROLL = 4
PLAN_SLACK = 8
GATHER_DEPTH = 4


def _pair_rows(pos1, pos2):
    rows = lax.broadcasted_iota(jnp.int32, (pos1.shape[0], LOCAL_CAP), 1).astype(F32)
    return jnp.where(rows == pos1, 1.0, 0.0), jnp.where(rows == pos2, 1.0, 0.0)


def _dispatch_kernel(fn_ref, info_ref, xs_o, ws_o, pos_o, cnt_o):
    tb = DISPATCH_BLOCK
    info = info_ref[...]
    e1, e2 = info[:, 0:1], info[:, 1:2]
    w1c, w2c = info[:, 2:3], info[:, 3:4]
    lane = lax.broadcasted_iota(jnp.int32, (tb, LANES), 1).astype(F32)
    o1 = jnp.where(lane == e1, 1.0, 0.0)
    o2 = jnp.where(lane == e2, 1.0, 0.0)
    onehot = o1 + o2
    cnt = jnp.sum(onehot, axis=0, keepdims=True)
    gran = jnp.floor((cnt + (GRANULE - 1)) * (1.0 / GRANULE))
    a = lax.broadcasted_iota(jnp.int32, (LANES, LANES), 0)
    b = lax.broadcasted_iota(jnp.int32, (LANES, LANES), 1)
    upper = jnp.where(a < b, 1.0, 0.0).astype(BF16)
    seg_off = _dot(jnp.broadcast_to(gran, (8, LANES)).astype(BF16), upper)[0:1, :] * GRANULE
    r = lax.broadcasted_iota(jnp.int32, (tb, tb), 0)
    c = lax.broadcasted_iota(jnp.int32, (tb, tb), 1)
    strict = jnp.where(r > c, 1.0, 0.0).astype(BF16)
    rank = _dot(strict, onehot.astype(BF16))
    slot = rank + seg_off
    pos1 = jnp.sum(o1 * slot, axis=1, keepdims=True)
    pos2 = jnp.sum(o2 * slot, axis=1, keepdims=True)
    pt1, pt2 = _pair_rows(pos1, pos2)
    pos_o[...] = jnp.where(lane == 0.0, pos1, jnp.where(lane == 1.0, pos2, 0.0))
    perm = (pt1 + pt2).T.astype(BF16)
    w_slot = jnp.sum((pt1 * w1c + pt2 * w2c).T, axis=1, keepdims=True)
    for c0 in range(0, D_MODEL, 512):
        xs_o[:, c0:c0 + 512] = _dot(perm, fn_ref[:, c0:c0 + 512]).astype(BF16)
    ws_o[...] = jnp.broadcast_to(w_slot, (LOCAL_CAP, LANES))
    cnt_o[...] = jnp.broadcast_to(cnt, (8, LANES)).astype(jnp.int32)


def _dispatch(fn, info):
    t = fn.shape[0]
    nb = t // DISPATCH_BLOCK
    row = lambda i: (i, 0)
    return pl.pallas_call(
        _dispatch_kernel,
        grid=(nb,),
        in_specs=[pl.BlockSpec((DISPATCH_BLOCK, D_MODEL), row),
                  pl.BlockSpec((DISPATCH_BLOCK, LANES), row)],
        out_specs=[pl.BlockSpec((LOCAL_CAP, D_MODEL), row),
                   pl.BlockSpec((LOCAL_CAP, LANES), row),
                   pl.BlockSpec((DISPATCH_BLOCK, LANES), row),
                   pl.BlockSpec((8, LANES), row)],
        out_shape=[jax.ShapeDtypeStruct((nb * LOCAL_CAP, D_MODEL), BF16),
                   jax.ShapeDtypeStruct((nb * LOCAL_CAP, LANES), F32),
                   jax.ShapeDtypeStruct((t, LANES), F32),
                   jax.ShapeDtypeStruct((nb * 8, LANES), jnp.int32)],
        compiler_params=_cparams(("arbitrary",)),
        name="moe_dispatch",
    )(fn, info)


def _map_len(max_tiles):
    return (max_tiles + GATHER_DEPTH - 1) * TILE_GRANULES + PLAN_SLACK


def _free_granule(q):
    return ((q // FREE_GRANULES) * LOCAL_GRANULES + (LOCAL_GRANULES - FREE_GRANULES)
            + (q % FREE_GRANULES))


def _plan_kernel(cnt_ref, gsrc_o, gdst_o, texp_o, tend_o, ntile_o, lrun, *, nb, max_tiles):
    def init(b, c):
        lrun[b] = 0
        return c

    lax.fori_loop(0, nb, init, 0)

    def per_expert(e, carry):
        g0, last_e = carry

        def per_block(b, g):
            k = (cnt_ref[b, e] + (GRANULE - 1)) >> GRANULE_SHIFT
            lo = lrun[b]
            lrun[b] = lo + k

            base = b * LOCAL_GRANULES + lo

            for j in range(PLAN_UNROLL):
                gsrc_o[g + j] = base + j
                gdst_o[g + j] = base + j

            @pl.when(k > PLAN_UNROLL)
            def _():
                def put(j, c):
                    gsrc_o[g + j] = base + j
                    gdst_o[g + j] = base + j
                    return c

                lax.fori_loop(PLAN_UNROLL, k, put, 0)

            return g + k

        g1 = lax.fori_loop(0, nb, per_block, g0, unroll=True)
        pad = (-g1) & (TILE_GRANULES - 1)

        def put_pad(j, c):
            g = g1 + j
            parity = (g // TILE_GRANULES) & 1
            gsrc_o[g] = _free_granule(0)
            gdst_o[g] = _free_granule(1 + parity * (TILE_GRANULES - 1) + (g & (TILE_GRANULES - 1)))
            return c

        lax.fori_loop(0, pad, put_pad, 0)
        g2 = g1 + pad

        def put_tile(tt, c):
            texp_o[tt] = e
            return c

        lax.fori_loop(g0 // TILE_GRANULES, g2 // TILE_GRANULES, put_tile, 0)
        tend_o[e] = g2 // TILE_GRANULES
        return g2, jnp.where(g2 > g0, e, last_e)

    g_end, last_e = lax.fori_loop(0, N_EXPERTS, per_expert, (0, 0))
    n_tiles = g_end // TILE_GRANULES
    ntile_o[0] = n_tiles

    def fill(tt, c):
        texp_o[tt] = last_e
        return c

    lax.fori_loop(n_tiles, max_tiles, fill, 0)

    def fill_map(g, c):
        gsrc_o[g] = _free_granule(0)
        gdst_o[g] = _free_granule(0)
        return c

    lax.fori_loop(g_end, _map_len(max_tiles), fill_map, 0)


def _max_tiles(t):
    nb = t // DISPATCH_BLOCK
    worst_rows = 2 * t + nb * N_EXPERTS * (GRANULE - 1) + N_EXPERTS * (EXPERT_TILE - GRANULE)
    return -(-worst_rows // EXPERT_TILE)


def _plan(cnt, t):
    nb = cnt.shape[0]
    assert nb * FREE_GRANULES >= 2 + 2 * (TILE_GRANULES - 1)
    max_tiles = _max_tiles(t)
    smem = pl.BlockSpec(memory_space=pltpu.SMEM)
    n_map = _map_len(max_tiles)
    return pl.pallas_call(
        functools.partial(_plan_kernel, nb=nb, max_tiles=max_tiles),
        in_specs=[smem],
        out_specs=[smem, smem, smem, smem, smem],
        out_shape=[jax.ShapeDtypeStruct((n_map,), jnp.int32),
                   jax.ShapeDtypeStruct((n_map,), jnp.int32),
                   jax.ShapeDtypeStruct((max_tiles,), jnp.int32),
                   jax.ShapeDtypeStruct((N_EXPERTS,), jnp.int32),
                   jax.ShapeDtypeStruct((1,), jnp.int32)],
        scratch_shapes=[pltpu.SMEM((nb,), jnp.int32)],
        name="moe_plan",
    )(cnt)


def _experts_kernel(gsrc, gdst, texp, tend, ntile, xy_in, ws_in, w1_hbm, w3_hbm, w2_hbm, xy_out,
                    xbuf, wsbuf, ybuf, st1, st3, st2, gsem, ssem, wsem):
    nt = ntile[0]

    def weight_copies(e, ws):
        return (pltpu.make_async_copy(w1_hbm.at[e], st1.at[ws], wsem.at[ws]),
                pltpu.make_async_copy(w3_hbm.at[e], st3.at[ws], wsem.at[ws]),
                pltpu.make_async_copy(w2_hbm.at[e], st2.at[ws], wsem.at[ws]))

    def rows(i):
        return pl.ds(pl.multiple_of(i * GRANULE, GRANULE), GRANULE)

    def gather_copies(g, j, sl):
        return (pltpu.make_async_copy(xy_in.at[rows(g), :], xbuf.at[sl, rows(j), :], gsem.at[sl]),
                pltpu.make_async_copy(ws_in.at[rows(g), :], wsbuf.at[sl, rows(j), :], gsem.at[sl]))

    def scatter_copies(g, j, sl):
        return (pltpu.make_async_copy(ybuf.at[sl, rows(j), :], xy_out.at[rows(g), :], ssem.at[sl]),)

    def issue(tt, sl, gmap, copies):
        for j in range(TILE_GRANULES):
            for cp in copies(gmap[tt * TILE_GRANULES + j], j, sl):
                cp.start()

    def drain(sl, copies):
        full = pl.ds(0, EXPERT_TILE)
        if copies is gather_copies:
            pltpu.make_async_copy(xy_in.at[full, :], xbuf.at[sl], gsem.at[sl]).wait()
            pltpu.make_async_copy(ws_in.at[full, :], wsbuf.at[sl], gsem.at[sl]).wait()
        else:
            pltpu.make_async_copy(ybuf.at[sl], xy_out.at[full, :], ssem.at[sl]).wait()

    @pl.when(nt > 0)
    def _():
        last = nt - 1
        for cp in weight_copies(texp[0], 0):
            cp.start()
        for ahead in range(GATHER_DEPTH - 1):
            issue(ahead, ahead, gsrc, gather_copies)

        def tile(t, wslot):
            slot = t % 2
            gslot = t % GATHER_DEPTH
            e = texp[t]
            first = (t == 0) | (texp[jnp.maximum(t - 1, 0)] != e)
            wslot = jnp.where(first & (t > 0), 1 - wslot, wslot)

            drain(gslot, gather_copies)

            @pl.when(t >= 2)
            def _():
                drain(slot, scatter_copies)

            @pl.when(first)
            def _():
                for cp in weight_copies(e, wslot):
                    cp.wait()
                nxt = tend[e]

                @pl.when(nxt < nt)
                def _():
                    for cp in weight_copies(texp[jnp.minimum(nxt, last)], 1 - wslot):
                        cp.start()

            x = xbuf[gslot]
            w_row = wsbuf[gslot][:, 0:1]
            a = _dot(x, st1[wslot].astype(BF16))
            b = _dot(x, st3[wslot].astype(BF16))
            y = _dot((_silu(a) * b).astype(BF16), st2[wslot].astype(BF16)) * w_row
            ybuf[slot] = y.astype(BF16)
            issue(t, slot, gdst, scatter_copies)
            ahead = t + GATHER_DEPTH - 1
            issue(ahead, ahead % GATHER_DEPTH, gsrc, gather_copies)
            return wslot

        lax.fori_loop(0, nt, tile, 0)

        for k in range(GATHER_DEPTH - 1):
            drain((nt + k) % GATHER_DEPTH, gather_copies)
        drain(last % 2, scatter_copies)

        @pl.when(nt >= 2)
        def _():
            drain(nt % 2, scatter_copies)


def _experts(gsrc, gdst, texp, tend, ntile, xy, ws, w1, w3, w2):
    smem = pl.BlockSpec(memory_space=pltpu.SMEM)
    hbm = pl.BlockSpec(memory_space=pl.ANY)
    return pl.pallas_call(
        _experts_kernel,
        in_specs=[smem, smem, smem, smem, smem, hbm, hbm, hbm, hbm, hbm],
        out_specs=hbm,
        out_shape=jax.ShapeDtypeStruct(xy.shape, xy.dtype),
        scratch_shapes=[pltpu.VMEM((GATHER_DEPTH, EXPERT_TILE, D_MODEL), BF16),
                        pltpu.VMEM((GATHER_DEPTH, EXPERT_TILE, LANES), F32),
                        pltpu.VMEM((2, EXPERT_TILE, D_MODEL), BF16),
                        pltpu.VMEM((2, D_MODEL, D_EXPERT), F32),
                        pltpu.VMEM((2, D_MODEL, D_EXPERT), F32),
                        pltpu.VMEM((2, D_EXPERT, D_MODEL), F32),
                        pltpu.SemaphoreType.DMA((GATHER_DEPTH,)),
                        pltpu.SemaphoreType.DMA((2,)),
                        pltpu.SemaphoreType.DMA((2,))],
        input_output_aliases={5: 0},
        compiler_params=pltpu.CompilerParams(vmem_limit_bytes=VMEM_LIMIT),
        name="moe_experts",
    )(gsrc, gdst, texp, tend, ntile, xy, ws, w1, w3, w2)


def _combine_kernel(y_ref, pos_ref, h1_ref, g2_ref, fg_ref, o_ref):
    pos = pos_ref[...]
    pt1, pt2 = _pair_rows(pos[:, 0:1], pos[:, 1:2])
    moe = _dot((pt1 + pt2).astype(BF16), y_ref[...])
    h = h1_ref[...] + g2_ref[...] * moe
    ms = jnp.mean(h * h, axis=-1, keepdims=True)
    o_ref[...] = h * lax.rsqrt(ms + EPS) * fg_ref[...]


def _combine(xy, pos, h1, g2, final_g):
    t = h1.shape[0]
    row = lambda i: (i, 0)
    const = lambda i: (0, 0)
    return pl.pallas_call(
        _combine_kernel,
        grid=(t // DISPATCH_BLOCK,),
        in_specs=[pl.BlockSpec((LOCAL_CAP, D_MODEL), row),
                  pl.BlockSpec((DISPATCH_BLOCK, LANES), row),
                  pl.BlockSpec((DISPATCH_BLOCK, D_MODEL), row),
                  pl.BlockSpec((1, D_MODEL), const), pl.BlockSpec((1, D_MODEL), const)],
        out_specs=pl.BlockSpec((DISPATCH_BLOCK, D_MODEL), row),
        out_shape=jax.ShapeDtypeStruct((t, D_MODEL), F32),
        compiler_params=_cparams(("arbitrary",)),
        name="moe_combine_final",
    )(xy, pos, h1, g2, final_g)


def _pad_lanes(a):
    return jnp.pad(a, ((0, 0), (0, LANES - a.shape[1])))


def kernel(x, c, ctx, c_ctx, w_mod, b_mod, norm1_g, w_in, w_conv_q, w_conv_k, gate_bias, head_norm_g, w_pool, pool_scale, w_out, norm2_g, w_group, b_group, w_router, b_router, w1, w3, w2, final_g):
    assert x.shape[0] == 1 and w_mod.shape[0] == 1
    seq = x.shape[1]
    x2d = x[0]
    ctx2d = ctx[0]

    n_main = 5 * MIX_HALF
    w_in_t = jnp.transpose(w_in[0])
    w_gate = _pad_lanes(jnp.transpose(_take_rows(w_in_t, n_main, N_GATES))).astype(BF16)
    gate_bias_row = _pad_lanes(gate_bias[0].reshape(1, N_GATES))
    wr_pack, wr_hi = _split_pack(jnp.concatenate([w_group[0], w_router[0]], axis=1))
    b_route = _pad_lanes(jnp.concatenate([b_group[0], b_router[0]]).reshape(1, -1))
    norm1 = norm1_g[0].reshape(1, D_MODEL)

    c16 = jnp.concatenate([c, c_ctx[None, :], jnp.zeros((14, D_MODEL), F32)], axis=0)
    mods = _adaln(c16, w_mod[0], b_mod[0].reshape(1, -1))
    mod_lat = mods[0].reshape(6, D_MODEL)
    mod_ctx = mods[1].reshape(6, D_MODEL)

    mod_in = jnp.concatenate([mod_ctx[0:2], mod_lat[0:2]], axis=0)
    u_pool, q, uo, k, kt, v, col, rowi = _inproj(x2d, ctx2d, mod_in, norm1, w_in_t, w_gate,
                                                 gate_bias_row, w_conv_q[0], w_conv_k[0])
    nc = seq // CHUNK
    zeros_state = (jnp.zeros((2 * HEADS, HEAD_DIM, HEAD_DIM), F32),
                   jnp.zeros((2 * HEADS, 1, HEAD_DIM), F32),
                   jnp.zeros((2 * HEADS, 1, LANES), F32))
    s0, n0, m0 = _mlstm(None, k, kt, v, col, rowi, *zeros_state, need_out=False, first=nc, nc=1)

    hf, hb = _mlstm(q, k, kt, v, col, rowi, s0, n0, m0, need_out=True, first=0, nc=nc)
    p = _pool(u_pool, w_pool[0], pool_scale[0].reshape(1, -1), seq)
    h1, fn, info = _outproj(p, hf, hb, uo, x2d, w_out[0], mod_lat, head_norm_g[0].reshape(1, -1),
                            norm2_g[0].reshape(1, -1), wr_pack, wr_hi, b_route, tm=2 * OUTPROJ_SUB)

    xs, ws, pos, cnt = _dispatch(fn, info)
    gsrc, gdst, texp, tend, ntile = _plan(cnt[::8, :N_EXPERTS], seq)
    xy = _experts(gsrc, gdst, texp, tend, ntile, xs, ws, w1[0], w3[0], w2[0])
    out = _combine(xy, pos, h1, mod_lat[5:6], final_g.reshape(1, -1))
    return out.reshape(1, seq, D_MODEL)
```

```python
import functools

import jax
import jax.numpy as jnp
from jax import lax
from jax.experimental import pallas as pl
from jax.experimental.pallas import tpu as pltpu

F32 = jnp.float32
BF16 = jnp.bfloat16

D_MODEL = 2048
GRID_W = 64
GRID_SHIFT = 6
POOL_WINDOWS = (2, 4, 8, 16)
POOL_GROUP = 256
HEADS = 4
HEAD_DIM = 256
MIX_HALF = 1024
N_GATES = 16
N_GROUPS = 4
EXPERTS_PER_GROUP = 8
N_EXPERTS = 32
D_EXPERT = 512
EPS = 1e-6
LANES = 128
CHUNK = 256
ROUTE_LANE0 = N_GROUPS

VMEM_LIMIT = 60 * 1024 * 1024


def _cparams(sem, vmem=VMEM_LIMIT):
    return pltpu.CompilerParams(dimension_semantics=sem, vmem_limit_bytes=vmem)


def _split2(x):
    hi = x.astype(BF16)
    lo = (x - hi.astype(F32)).astype(BF16)
    return hi, lo


def _split3(x):
    hi = x.astype(BF16)
    r = x - hi.astype(F32)
    mid = r.astype(BF16)
    lo = (r - mid.astype(F32)).astype(BF16)
    return hi, mid, lo


def _dot(a, b):
    return jnp.dot(a, b, preferred_element_type=F32)


SPLIT_LANE = 64


def _split_pack(w):
    hi, lo = _split2(w)
    n = w.shape[1]
    gap = jnp.zeros((w.shape[0], SPLIT_LANE - n), BF16)
    rest = jnp.zeros((w.shape[0], LANES - n), BF16)
    return jnp.concatenate([hi, gap, lo, gap], axis=1), jnp.concatenate([hi, rest], axis=1)


def _split_dot(xh, xl, w_packed, w_hi):
    r = _dot(xh, w_packed)
    return r + pltpu.roll(r, SPLIT_LANE, 1) + _dot(xl, w_hi)


def _silu(x):
    return x * jax.nn.sigmoid(x)


def _log_sigmoid(x):
    return jnp.minimum(x, 0.0) - jnp.log(1.0 + jnp.exp(-jnp.abs(x)))


def _copy_kernel(w_ref, o_ref):
    o_ref[...] = w_ref[...]


def _take_rows(w, start, n):
    return pl.pallas_call(
        _copy_kernel,
        grid=(1,),
        in_specs=[pl.BlockSpec((n, w.shape[1]), lambda i: (start // n, 0))],
        out_specs=pl.BlockSpec((n, w.shape[1]), lambda i: (0, 0)),
        out_shape=jax.ShapeDtypeStruct((n, w.shape[1]), w.dtype),
        name="take_rows",
    )(w)


def _adaln_kernel(c_ref, w_ref, b_ref, o_ref):
    a = _silu(c_ref[...])
    a3 = jnp.concatenate(_split3(a), axis=0)
    w_hi, w_lo = _split2(w_ref[...])
    acc = _dot(a3, w_hi)
    acc_lo = _dot(a3[:32], w_lo)
    out = acc[0:16] + acc[16:32] + acc[32:48] + acc_lo[0:16] + acc_lo[16:32]
    o_ref[...] = out + b_ref[...]


ADALN_COLS = 1536


def _adaln(c16, w_mod, b_mod):
    n = w_mod.shape[1]
    tn = ADALN_COLS
    return pl.pallas_call(
        _adaln_kernel,
        grid=(n // tn,),
        in_specs=[pl.BlockSpec((16, D_MODEL), lambda j: (0, 0)),
                  pl.BlockSpec((D_MODEL, tn), lambda j: (0, j)),
                  pl.BlockSpec((1, tn), lambda j: (0, j))],
        out_specs=pl.BlockSpec((16, tn), lambda j: (0, j)),
        out_shape=jax.ShapeDtypeStruct((16, n), F32),
        compiler_params=_cparams(("arbitrary",)),
        name="adaln",
    )(c16, w_mod, b_mod)


HALO = 8
NEG_INF = float("-inf")


def _gate_scan_info(gates):
    n = gates.shape[0]
    lane = lax.broadcasted_iota(jnp.int32, gates.shape, 1)
    rows = lax.broadcasted_iota(jnp.int32, gates.shape, 0)
    lf = jnp.where(lane < N_GATES, _log_sigmoid(gates), 0.0)
    hi = lf.astype(BF16).astype(F32)
    rem = lf - hi
    mid = rem.astype(BF16).astype(F32)
    packed = (hi + pltpu.roll(mid, 32, 1) + pltpu.roll(rem - mid, 64, 1)).astype(BF16)
    r = lax.broadcasted_iota(jnp.int32, (n, n), 0)
    c = lax.broadcasted_iota(jnp.int32, (n, n), 1)
    pf = _dot(jnp.where(r >= c, 1.0, 0.0).astype(BF16), packed)
    pb = _dot(jnp.where(r <= c, 1.0, 0.0).astype(BF16), packed)
    bf = pf + pltpu.roll(pf, 96, 1) + pltpu.roll(pf, 64, 1)
    bb = pb + pltpu.roll(pb, 96, 1) + pltpu.roll(pb, 64, 1)
    b = jnp.where(lane < 8, bf, bb)
    cval = gates - pltpu.roll(b, LANES - 4, 1)
    pm = cval
    sm = cval
    step = 1
    while step < n:
        pm = jnp.maximum(pm, jnp.where(rows >= step, pltpu.roll(pm, step, 0), NEG_INF))
        sm = jnp.maximum(sm, jnp.where(rows < n - step, pltpu.roll(sm, n - step, 0), NEG_INF))
        step *= 2
    cm = jnp.where(lane < 8, pm, sm)
    return cval, jnp.where((lane & 4) == 0, cm, b)


W_CHUNK = 256


def _load_bf16(src_hbm, dst, stage, sem, *, n_chunks, transpose):
    rows = stage.shape[1]
    def chunk_copy(c, sl):
        return pltpu.make_async_copy(
            src_hbm.at[pl.ds(pl.multiple_of(c * rows, rows), rows), :], stage.at[sl], sem.at[sl])

    chunk_copy(0, 0).start()

    def convert(c, carry):
        sl = c % 2

        @pl.when(c + 1 < n_chunks)
        def _():
            chunk_copy(c + 1, 1 - sl).start()

        chunk_copy(c, sl).wait()
        span = pl.ds(pl.multiple_of(c * rows, rows), rows)
        if transpose:
            dst[:, span] = stage[sl].T.astype(BF16)
        else:
            dst[span, :] = stage[sl].astype(BF16)
        return carry

    lax.fori_loop(0, n_chunks, convert, 0)


def _inproj_kernel(x_ref, xp_ref, xn_ref, ctx_ref, mod_ref, g_ref, wt_hbm, wg_ref, gb_ref,
                   wcq_ref, wck_ref, pool_o, q_o, o_o, k_o, kt_o, v_o, col_o, row_o,
                   w_s, stage, sem, *, nt):
    i = pl.program_id(0)
    tm = x_ref.shape[0]

    @pl.when(i == 0)
    def _():
        _load_bf16(wt_hbm, w_s, stage, sem, n_chunks=w_s.shape[1] // W_CHUNK, transpose=True)

    is_ctx = i == nt
    x_main = jnp.where(is_ctx, ctx_ref[...], x_ref[...])
    x_all = jnp.concatenate([xp_ref[...], x_main, xn_ref[...]], axis=0)
    shift = jnp.where(is_ctx, mod_ref[0:1, :], mod_ref[2:3, :])
    scale = jnp.where(is_ctx, mod_ref[1:2, :], mod_ref[3:4, :])
    ms = jnp.mean(x_all * x_all, axis=-1, keepdims=True)
    y = x_all * lax.rsqrt(ms + EPS) * g_ref[...]
    xn_all = y * (1.0 + scale) + shift
    xh_all = xn_all.astype(BF16)
    xh = xn_all[HALO:HALO + tm].astype(BF16)

    def cols(ci):
        return w_s[:, ci * MIX_HALF:(ci + 1) * MIX_HALF]

    u_q = _dot(xh_all, cols(1))
    u_k = _dot(xh_all, cols(3))
    pool_o[...] = _dot(xh, cols(0))
    o_o[...] = _dot(xh, cols(2))
    v_o[...] = _dot(xh, cols(4)).astype(BF16)

    keep_prev = jnp.where(jnp.logical_or(i == 0, is_ctx), 0.0, 1.0)
    keep_next = jnp.where(jnp.logical_or(i == nt - 1, is_ctx), 0.0, 1.0)

    def conv_silu(u, wc_ref):
        n = u.shape[0]
        u = jnp.concatenate([u[:HALO] * keep_prev, u[HALO:HALO + tm], u[HALO + tm:] * keep_next], axis=0)
        up = pltpu.roll(u, 1, 0)[HALO:HALO + tm]
        un = pltpu.roll(u, n - 1, 0)[HALO:HALO + tm]
        return _silu(wc_ref[0:1, :] * up + wc_ref[1:2, :] * u[HALO:HALO + tm] + wc_ref[2:3, :] * un)

    q_o[...] = (conv_silu(u_q, wcq_ref) * (HEAD_DIM ** -0.5)).astype(BF16)
    k = conv_silu(u_k, wck_ref)
    k_o[...] = k.astype(BF16)
    kt_o[...] = k.astype(BF16).T
    gates = _dot(xh, wg_ref[...]) + gb_ref[...]
    cval, col = _gate_scan_info(gates)
    col_o[...] = col[:, :N_GATES]
    row_o[...] = cval.T[:N_GATES, :]


def _inproj(x2d, ctx2d, mod, g, w_in_t, w_gate, gate_bias_row, wcq, wck):
    t = x2d.shape[0]
    tm = CHUNK
    assert ctx2d.shape[0] == tm
    nt = t // tm
    r8 = tm // HALO
    last8 = t // HALO - 1
    n_main = 5 * MIX_HALF
    rows = t + tm
    const = lambda i: (0, 0)
    row = lambda i: (i, 0)
    f32_out = jax.ShapeDtypeStruct((rows, MIX_HALF), F32)
    bf16_out = jax.ShapeDtypeStruct((rows, MIX_HALF), BF16)
    seq = pl.BlockSpec((tm, MIX_HALF), row)
    return pl.pallas_call(
        functools.partial(_inproj_kernel, nt=nt),
        grid=(nt + 1,),
        in_specs=[pl.BlockSpec((tm, D_MODEL), lambda i: (jnp.minimum(i, nt - 1), 0)),
                  pl.BlockSpec((HALO, D_MODEL), lambda i: (jnp.clip(i * r8 - 1, 0, last8), 0)),
                  pl.BlockSpec((HALO, D_MODEL), lambda i: (jnp.minimum((i + 1) * r8, last8), 0)),
                  pl.BlockSpec((tm, D_MODEL), const),
                  pl.BlockSpec((4, D_MODEL), const),
                  pl.BlockSpec((1, D_MODEL), const),
                  pl.BlockSpec(memory_space=pl.ANY),
                  pl.BlockSpec((D_MODEL, LANES), const),
                  pl.BlockSpec((1, LANES), const),
                  pl.BlockSpec((3, MIX_HALF), const),
                  pl.BlockSpec((3, MIX_HALF), const)],
        out_specs=[seq, seq, seq, seq, pl.BlockSpec((MIX_HALF, tm), lambda i: (0, i)), seq,
                   pl.BlockSpec((tm, N_GATES), row), pl.BlockSpec((N_GATES, tm), lambda i: (0, i))],
        out_shape=[f32_out, bf16_out, f32_out, bf16_out,
                   jax.ShapeDtypeStruct((MIX_HALF, rows), BF16), bf16_out,
                   jax.ShapeDtypeStruct((rows, N_GATES), F32),
                   jax.ShapeDtypeStruct((N_GATES, rows), F32)],
        scratch_shapes=[pltpu.VMEM((D_MODEL, n_main), BF16),
                        pltpu.VMEM((2, W_CHUNK, D_MODEL), F32),
                        pltpu.SemaphoreType.DMA((2,))],
        compiler_params=_cparams(("arbitrary",)),
        name="inproj",
    )(x2d, x2d, x2d, ctx2d, mod, g, w_in_t, w_gate, gate_bias_row, wcq, wck)


def _mlstm_kernel(*refs, need_out):
    if need_out:
        (qf, kf, ktf, vf, colf, rowf, qb, kb, ktb, vb, colb, rowb, s0, n0, m0,
         hf_o, hb_o, s_s, n_s, m_s) = refs
        q_refs, h_outs = (qf, qb), (hf_o, hb_o)
    else:
        (kf, ktf, vf, colf, rowf, kb, ktb, vb, colb, rowb, s0, n0, m0,
         s_o, n_o, m_o, s_s, n_s, m_s) = refs
    k_refs, kt_refs, v_refs = (kf, kb), (ktf, ktb), (vf, vb)
    col_refs, row_refs = (colf, colb), (rowf, rowb)
    j = pl.program_id(0)

    @pl.when(j == 0)
    def _():
        s_s[...] = s0[...]
        n_s[...] = n0[...]
        m_s[...] = m0[...]

    L = CHUNK
    row = lax.broadcasted_iota(jnp.int32, (L, L), 0)
    col = lax.broadcasted_iota(jnp.int32, (L, L), 1)

    heads = [(d, h) for d in range(2) for h in range(HEADS)]

    def head_values(d, h):
        hd = d * HEADS + h
        lc, lb = 8 * d + h, 8 * d + 4 + h
        hs = slice(h * HEAD_DIM, (h + 1) * HEAD_DIM)
        edge = L - 1 if d == 0 else 0
        colv = col_refs[d][...]
        rowv = row_refs[d][...]
        v = dict(hd=hd, hs=hs, d=d)
        v["cm_c"], v["b_c"] = colv[:, lc:lc + 1], colv[:, lb:lb + 1]
        v["c_r"] = rowv[lc:lc + 1, :]
        v["g"] = v["b_c"][edge:edge + 1, :]
        v["m_old"] = m_s[hd][:, 0:1]
        v["s_old"] = s_s[hd]
        v["n_old"] = n_s[hd]
        v["m_x"] = jnp.maximum(v["m_old"], v["cm_c"][edge:edge + 1, :])
        v["decay"] = jnp.exp(v["m_old"] - v["m_x"])
        v["wk_r"] = jnp.exp(v["c_r"] - v["m_x"])
        v["v_h"] = v_refs[d][:, hs]
        v["kt_h"] = kt_refs[d][hs, :]
        return v

    def update_state(v):
        hd = v["hd"]
        n_s[hd] = v["decay"] * v["n_old"] + _dot(
            jnp.broadcast_to(v["wk_r"], (8, L)).astype(BF16), k_refs[v["d"]][:, v["hs"]])[0:1, :]
        kwt = (v["kt_h"].astype(F32) * v["wk_r"]).astype(BF16)
        s_s[hd] = v["decay"] * v["s_old"] + _dot(kwt, v["v_h"])
        m_s[hd] = jnp.broadcast_to(v["g"] + v["m_x"], (1, LANES))

    def weights(v):
        d, hs = v["d"], v["hs"]
        mask = (row >= col) if d == 0 else (row <= col)
        q_h = q_refs[d][:, hs]
        m_c = jnp.maximum(v["m_old"], v["cm_c"])
        w_inter = jnp.exp(v["m_old"] - m_c)
        v["e"] = jnp.where(mask, jnp.exp(v["c_r"] - m_c), 0.0)
        v["qw"] = (q_h.astype(F32) * w_inter).astype(BF16)
        qn = lax.dot_general(q_h, jnp.broadcast_to(v["n_old"], (8, HEAD_DIM)).astype(BF16),
                             (((1,), (1,)), ((), ())), preferred_element_type=F32)
        v["den_inter"] = w_inter * qn[:, 0:1]
        v["floor"] = jnp.exp(-(v["b_c"] + m_c))
        v["s_bf"] = v["s_old"].astype(BF16)

    def outputs(v, qk):
        p = (v["e"] * qk).astype(BF16)
        num = _dot(v["qw"], v["s_bf"]) + _dot(p, v["v_h"])
        p_sum = lax.dot_general(p, jnp.ones((8, L), BF16), (((1,), (1,)), ((), ())),
                                preferred_element_type=F32)
        den = v["den_inter"] + p_sum[:, 0:1]
        h_outs[v["d"]][:, v["hs"]] = num / jnp.maximum(jnp.abs(den), v["floor"])

    vals = [head_values(d, h) for d, h in heads]
    if need_out:
        qks = [_dot(q_refs[v["d"]][:, v["hs"]], v["kt_h"]) for v in vals]
        for v in vals:
            weights(v)
    for v in vals:
        update_state(v)
    if need_out:
        for v, qk in zip(vals, qks):
            outputs(v, qk)

    if not need_out:
        s_o[...] = s_s[...]
        n_o[...] = n_s[...]
        m_o[...] = m_s[...]


def _mlstm(q, k, kt, v, col, rowi, s0, n0, m0, need_out, first, nc):
    t = nc * CHUNK
    fwd = lambda j: (first + j, 0)
    bwd = lambda j: (first + nc - 1 - j, 0)
    fwd_t = lambda j: (0, first + j)
    bwd_t = lambda j: (0, first + nc - 1 - j)
    c3 = lambda j: (0, 0, 0)
    seq = lambda im: pl.BlockSpec((CHUNK, MIX_HALF), im)
    state_specs = [pl.BlockSpec(s0.shape, c3), pl.BlockSpec(n0.shape, c3), pl.BlockSpec(m0.shape, c3)]
    scratch = [pltpu.VMEM(s0.shape, F32), pltpu.VMEM(n0.shape, F32), pltpu.VMEM(m0.shape, F32)]

    def side(im, im_t):
        specs = ([seq(im)] if need_out else []) + [seq(im), pl.BlockSpec((MIX_HALF, CHUNK), im_t), seq(im)]
        return specs + [pl.BlockSpec((CHUNK, N_GATES), im), pl.BlockSpec((N_GATES, CHUNK), im_t)]

    in_specs = side(fwd, fwd_t) + side(bwd, bwd_t) + state_specs
    seq_in = ((q,) if need_out else ()) + (k, kt, v, col, rowi)
    args = seq_in + seq_in + (s0, n0, m0)
    if need_out:
        out_specs = [seq(fwd), seq(bwd)]
        out_shape = [jax.ShapeDtypeStruct((t, MIX_HALF), F32)] * 2
    else:
        out_specs = state_specs
        out_shape = [jax.ShapeDtypeStruct(a.shape, F32) for a in (s0, n0, m0)]
    return pl.pallas_call(
        functools.partial(_mlstm_kernel, need_out=need_out),
        grid=(nc,),
        in_specs=in_specs,
        out_specs=out_specs,
        out_shape=out_shape,
        scratch_shapes=scratch,
        compiler_params=_cparams(("arbitrary",)),
        name="mlstm_out" if need_out else "mlstm_state",
    )(*args)


POOL_PAD = 512
POOL_UNROLL = 4


def _pool_kernel(u_ref, w_ref, sc_ref, o_ref, pad_s, *, t):
    for gi, win in enumerate(POOL_WINDOWS):
        @pl.when(pl.program_id(0) == gi)
        def _():
            _pool_group(u_ref, w_ref, sc_ref, o_ref, pad_s, win=win, t=t)


def _pool_group(u_ref, w_ref, sc_ref, o_ref, pad_s, *, win, t):
    half = win // 2
    tile = 256
    zeros = jnp.zeros((POOL_PAD, POOL_GROUP), F32)
    pad_s[0:POOL_PAD, :] = zeros
    pad_s[POOL_PAD + t:POOL_PAD + t + POOL_PAD, :] = zeros

    def copy(r, carry):
        t0 = pl.multiple_of(r * tile, tile)
        pad_s[pl.ds(POOL_PAD + t0, tile), :] = u_ref[pl.ds(t0, tile), :]
        return carry

    lax.fori_loop(0, t // tile, copy, 0)

    row = lax.broadcasted_iota(jnp.int32, (tile, tile), 0)
    col = lax.broadcasted_iota(jnp.int32, (tile, tile), 1)
    same_row = (row >> GRID_SHIFT) == (col >> GRID_SHIFT)
    in_win = (col - row >= -half) & (col - row < half)
    band = jnp.where(same_row & in_win, 1.0, 0.0).astype(BF16)
    w = w_ref[...].astype(BF16)
    scale = sc_ref[...]
    n_rows = t // GRID_W

    tok0 = lax.broadcasted_iota(jnp.int32, (tile, POOL_GROUP), 0)
    gc = tok0 & (GRID_W - 1)
    inv_h = 1.0 / (jnp.minimum(gc + half, GRID_W) - jnp.maximum(gc - half, 0)).astype(F32)

    def body(r, carry):
        t0s = [pl.multiple_of((r * POOL_UNROLL + k) * tile, tile) for k in range(POOL_UNROLL)]
        pieces = []
        for t0 in t0s:
            acc = pad_s[pl.ds(POOL_PAD + t0 - GRID_W * half, tile), :]
            for dd in range(-half + 1, half):
                acc = acc + pad_s[pl.ds(POOL_PAD + t0 + GRID_W * dd, tile), :]
            gr = (t0 + tok0) >> GRID_SHIFT
            cnt_v = jnp.minimum(gr + half, n_rows) - jnp.maximum(gr - half, 0)
            pieces.append(_split2(acc / cnt_v.astype(F32)))
        means = [(_dot(band, hi) + _dot(band, lo)) * inv_h for hi, lo in pieces]
        diffs = [(m - pad_s[pl.ds(POOL_PAD + t0, tile), :]).astype(BF16) for m, t0 in zip(means, t0s)]
        for d, t0 in zip(diffs, t0s):
            o_ref[pl.ds(t0, tile), :] = (_dot(d, w) * scale).astype(BF16)
        return carry

    lax.fori_loop(0, t // (tile * POOL_UNROLL), body, 0)


def _pool(u_pool, w_pool, scale_row, t):
    return pl.pallas_call(
        functools.partial(_pool_kernel, t=t),
        grid=(len(POOL_WINDOWS),),
        in_specs=[pl.BlockSpec((t, POOL_GROUP), lambda g: (0, g)),
                  pl.BlockSpec((None, POOL_GROUP, POOL_GROUP), lambda g: (g, 0, 0)),
                  pl.BlockSpec((1, POOL_GROUP), lambda g: (0, g))],
        out_specs=pl.BlockSpec((t, POOL_GROUP), lambda g: (0, g)),
        out_shape=jax.ShapeDtypeStruct((t, MIX_HALF), BF16),
        scratch_shapes=[pltpu.VMEM((t + 2 * POOL_PAD, POOL_GROUP), F32)],
        compiler_params=_cparams(("arbitrary",)),
        name="pool_mix",
    )(u_pool, w_pool, scale_row)


OUTPROJ_SUB = 256


def _route(logits):
    lane = lax.broadcasted_iota(jnp.int32, logits.shape, 1).astype(F32)
    neg = -jnp.inf
    big = float(LANES)
    gl = jnp.where(lane < N_GROUPS, logits, neg)
    gmax = jnp.max(gl, axis=1, keepdims=True)
    gsel = jnp.min(jnp.where(gl == gmax, lane, big), axis=1, keepdims=True)
    p_grp = 1.0 / jnp.sum(jnp.exp(gl - gmax), axis=1, keepdims=True)
    lo = ROUTE_LANE0 + EXPERTS_PER_GROUP * gsel
    el = jnp.where((lane >= lo) & (lane < lo + EXPERTS_PER_GROUP), logits, neg)
    m1 = jnp.max(el, axis=1, keepdims=True)
    i1 = jnp.min(jnp.where(el == m1, lane, big), axis=1, keepdims=True)
    el2 = jnp.where(lane == i1, neg, el)
    m2 = jnp.max(el2, axis=1, keepdims=True)
    i2 = jnp.min(jnp.where(el2 == m2, lane, big), axis=1, keepdims=True)
    e2 = jnp.exp(m2 - m1)
    p1 = 1.0 / (1.0 + e2)
    p2 = e2 / (1.0 + e2)
    info = jnp.where(lane == 0.0, i1 - ROUTE_LANE0, 0.0)
    info = jnp.where(lane == 1.0, i2 - ROUTE_LANE0, info)
    info = jnp.where(lane == 2.0, p_grp * p1, info)
    return jnp.where(lane == 3.0, p_grp * p2, info)


def _outproj_kernel(p_ref, hf_ref, hb_ref, uo_ref, x_ref, wout_hbm, mod_ref,
                    hg_ref, n2g_ref, wrp_ref, wrh_ref, br_ref, h1_o, fn_o, info_o, w_s, stage, sem):
    @pl.when(pl.program_id(0) == 0)
    def _():
        _load_bf16(wout_hbm, w_s, stage, sem, n_chunks=w_s.shape[0] // stage.shape[1], transpose=False)

    def mixer_input(rs):
        h = hf_ref[rs, :] + hb_ref[rs, :]
        parts = []
        for hh in range(HEADS):
            hs = h[:, hh * HEAD_DIM:(hh + 1) * HEAD_DIM]
            mu = jnp.mean(hs, axis=-1, keepdims=True)
            ctr = hs - mu
            var = jnp.mean(ctr * ctr, axis=-1, keepdims=True)
            parts.append(ctr * lax.rsqrt(var + EPS))
        hn = jnp.concatenate(parts, axis=1) * hg_ref[...]
        m = (hn * jax.nn.sigmoid(uo_ref[rs, :])).astype(BF16)
        return jnp.concatenate([p_ref[rs, :], m], axis=1)

    def finish(rs, mix):
        h1 = x_ref[rs, :] + mod_ref[2:3, :] * mix
        h1_o[rs, :] = h1
        ms = jnp.mean(h1 * h1, axis=-1, keepdims=True)
        fn = h1 * lax.rsqrt(ms + EPS) * n2g_ref[...]
        fn = fn * (1.0 + mod_ref[4:5, :]) + mod_ref[3:4, :]
        fh, fl = _split2(fn)
        fn_o[rs, :] = fh
        logits = _split_dot(fh, fl, wrp_ref[...], wrh_ref[...]) + br_ref[...]
        info_o[rs, :] = _route(logits)

    tm = x_ref.shape[0]
    subs = [slice(r0, r0 + OUTPROJ_SUB) for r0 in range(0, tm, OUTPROJ_SUB)]
    w = w_s[...]
    mixes = [_dot(mixer_input(rs), w) for rs in subs]
    for rs, mix in zip(subs, mixes):
        finish(rs, mix)


def _outproj(p, hf, hb, uo, x2d, w_out, mod, head_g, norm2_g, wr_pack, wr_hi, b_route, tm):
    t = x2d.shape[0]
    const = lambda i: (0, 0)
    row = lambda i: (i, 0)
    in_specs = ([pl.BlockSpec((tm, MIX_HALF), row)] * 4
                + [pl.BlockSpec((tm, D_MODEL), row),
                   pl.BlockSpec(memory_space=pl.ANY),
                   pl.BlockSpec(mod.shape, const),
                   pl.BlockSpec((1, MIX_HALF), const),
                   pl.BlockSpec((1, D_MODEL), const),
                   pl.BlockSpec((D_MODEL, LANES), const),
                   pl.BlockSpec((D_MODEL, LANES), const),
                   pl.BlockSpec((1, LANES), const)])
    return pl.pallas_call(
        _outproj_kernel,
        grid=(t // tm,),
        in_specs=in_specs,
        out_specs=[pl.BlockSpec((tm, D_MODEL), row), pl.BlockSpec((tm, D_MODEL), row),
                   pl.BlockSpec((tm, LANES), row)],
        out_shape=[jax.ShapeDtypeStruct((t, D_MODEL), F32),
                   jax.ShapeDtypeStruct((t, D_MODEL), BF16),
                   jax.ShapeDtypeStruct((t, LANES), F32)],
        scratch_shapes=[pltpu.VMEM(w_out.shape, BF16),
                        pltpu.VMEM((2, W_CHUNK // 2, w_out.shape[1]), F32),
                        pltpu.SemaphoreType.DMA((2,))],
        compiler_params=_cparams(("arbitrary",)),
        name="outproj_route",
    )(p, hf, hb, uo, x2d, w_out, mod, head_g, norm2_g, wr_pack, wr_hi, b_route)


DISPATCH_BLOCK = 512
GRANULE = 16
GRANULE_SHIFT = 4
LOCAL_CAP = 1536
LOCAL_GRANULES = LOCAL_CAP // GRANULE
FREE_GRANULES = 2
EXPERT_TILE = 256
TILE_GRANULES = EXPERT_TILE // GRANULE
PLAN_UNROLL = 4
PLAN_SLACK = 8
GATHER_DEPTH = 4


def _pair_rows(pos1, pos2):
    rows = lax.broadcasted_iota(jnp.int32, (pos1.shape[0], LOCAL_CAP), 1).astype(F32)
    return jnp.where(rows == pos1, 1.0, 0.0), jnp.where(rows == pos2, 1.0, 0.0)


def _dispatch_kernel(fn_ref, info_ref, xs_o, ws_o, pos_o, cnt_o):
    tb = DISPATCH_BLOCK
    info = info_ref[...]
    e1, e2 = info[:, 0:1], info[:, 1:2]
    w1c, w2c = info[:, 2:3], info[:, 3:4]
    lane = lax.broadcasted_iota(jnp.int32, (tb, LANES), 1).astype(F32)
    o1 = jnp.where(lane == e1, 1.0, 0.0)
    o2 = jnp.where(lane == e2, 1.0, 0.0)
    onehot = o1 + o2
    cnt = jnp.sum(onehot, axis=0, keepdims=True)
    gran = jnp.floor((cnt + (GRANULE - 1)) * (1.0 / GRANULE))
    a = lax.broadcasted_iota(jnp.int32, (LANES, LANES), 0)
    b = lax.broadcasted_iota(jnp.int32, (LANES, LANES), 1)
    upper = jnp.where(a < b, 1.0, 0.0).astype(BF16)
    seg_off = _dot(jnp.broadcast_to(gran, (8, LANES)).astype(BF16), upper)[0:1, :] * GRANULE
    r = lax.broadcasted_iota(jnp.int32, (tb, tb), 0)
    c = lax.broadcasted_iota(jnp.int32, (tb, tb), 1)
    strict = jnp.where(r > c, 1.0, 0.0).astype(BF16)
    rank = _dot(strict, onehot.astype(BF16))
    slot = rank + seg_off
    pos1 = jnp.sum(o1 * slot, axis=1, keepdims=True)
    pos2 = jnp.sum(o2 * slot, axis=1, keepdims=True)
    pt1, pt2 = _pair_rows(pos1, pos2)
    pos_o[...] = jnp.where(lane == 0.0, pos1, jnp.where(lane == 1.0, pos2, 0.0))
    perm = (pt1 + pt2).T.astype(BF16)
    w_slot = jnp.sum((pt1 * w1c + pt2 * w2c).T, axis=1, keepdims=True)
    for c0 in range(0, D_MODEL, 512):
        xs_o[:, c0:c0 + 512] = _dot(perm, fn_ref[:, c0:c0 + 512]).astype(BF16)
    ws_o[...] = jnp.broadcast_to(w_slot, (LOCAL_CAP, LANES))
    cnt_o[...] = jnp.broadcast_to(cnt, (8, LANES)).astype(jnp.int32)


def _dispatch(fn, info):
    t = fn.shape[0]
    nb = t // DISPATCH_BLOCK
    row = lambda i: (i, 0)
    return pl.pallas_call(
        _dispatch_kernel,
        grid=(nb,),
        in_specs=[pl.BlockSpec((DISPATCH_BLOCK, D_MODEL), row),
                  pl.BlockSpec((DISPATCH_BLOCK, LANES), row)],
        out_specs=[pl.BlockSpec((LOCAL_CAP, D_MODEL), row),
                   pl.BlockSpec((LOCAL_CAP, LANES), row),
                   pl.BlockSpec((DISPATCH_BLOCK, LANES), row),
                   pl.BlockSpec((8, LANES), row)],
        out_shape=[jax.ShapeDtypeStruct((nb * LOCAL_CAP, D_MODEL), BF16),
                   jax.ShapeDtypeStruct((nb * LOCAL_CAP, LANES), F32),
                   jax.ShapeDtypeStruct((t, LANES), F32),
                   jax.ShapeDtypeStruct((nb * 8, LANES), jnp.int32)],
        compiler_params=_cparams(("arbitrary",)),
        name="moe_dispatch",
    )(fn, info)


def _map_len(max_tiles):
    return (max_tiles + GATHER_DEPTH - 1) * TILE_GRANULES + PLAN_SLACK


def _free_granule(q):
    return ((q // FREE_GRANULES) * LOCAL_GRANULES + (LOCAL_GRANULES - FREE_GRANULES)
            + (q % FREE_GRANULES))


def _plan_kernel(cnt_ref, gsrc_o, gdst_o, texp_o, tend_o, ntile_o, lrun, *, nb, max_tiles):
    def init(b, c):
        lrun[b] = 0
        return c

    lax.fori_loop(0, nb, init, 0)

    def per_expert(e, carry):
        g0, last_e = carry

        def per_block(b, g):
            k = (cnt_ref[b, e] + (GRANULE - 1)) >> GRANULE_SHIFT
            lo = lrun[b]
            lrun[b] = lo + k

            base = b * LOCAL_GRANULES + lo

            for j in range(PLAN_UNROLL):
                gsrc_o[g + j] = base + j
                gdst_o[g + j] = base + j

            @pl.when(k > PLAN_UNROLL)
            def _():
                def put(j, c):
                    gsrc_o[g + j] = base + j
                    gdst_o[g + j] = base + j
                    return c

                lax.fori_loop(PLAN_UNROLL, k, put, 0)

            return g + k

        g1 = lax.fori_loop(0, nb, per_block, g0, unroll=True)
        pad = (-g1) & (TILE_GRANULES - 1)

        def put_pad(j, c):
            g = g1 + j
            parity = (g // TILE_GRANULES) & 1
            gsrc_o[g] = _free_granule(0)
            gdst_o[g] = _free_granule(1 + parity * (TILE_GRANULES - 1) + (g & (TILE_GRANULES - 1)))
            return c

        lax.fori_loop(0, pad, put_pad, 0)
        g2 = g1 + pad

        def put_tile(tt, c):
            texp_o[tt] = e
            return c

        lax.fori_loop(g0 // TILE_GRANULES, g2 // TILE_GRANULES, put_tile, 0)
        tend_o[e] = g2 // TILE_GRANULES
        return g2, jnp.where(g2 > g0, e, last_e)

    g_end, last_e = lax.fori_loop(0, N_EXPERTS, per_expert, (0, 0))
    n_tiles = g_end // TILE_GRANULES
    ntile_o[0] = n_tiles

    def fill(tt, c):
        texp_o[tt] = last_e
        return c

    lax.fori_loop(n_tiles, max_tiles, fill, 0)

    def fill_map(g, c):
        gsrc_o[g] = _free_granule(0)
        gdst_o[g] = _free_granule(0)
        return c

    lax.fori_loop(g_end, _map_len(max_tiles), fill_map, 0)


def _max_tiles(t):
    nb = t // DISPATCH_BLOCK
    worst_rows = 2 * t + nb * N_EXPERTS * (GRANULE - 1) + N_EXPERTS * (EXPERT_TILE - GRANULE)
    return -(-worst_rows // EXPERT_TILE)


def _plan(cnt, t):
    nb = cnt.shape[0]
    assert nb * FREE_GRANULES >= 2 + 2 * (TILE_GRANULES - 1)
    max_tiles = _max_tiles(t)
    smem = pl.BlockSpec(memory_space=pltpu.SMEM)
    n_map = _map_len(max_tiles)
    return pl.pallas_call(
        functools.partial(_plan_kernel, nb=nb, max_tiles=max_tiles),
        in_specs=[smem],
        out_specs=[smem, smem, smem, smem, smem],
        out_shape=[jax.ShapeDtypeStruct((n_map,), jnp.int32),
                   jax.ShapeDtypeStruct((n_map,), jnp.int32),
                   jax.ShapeDtypeStruct((max_tiles,), jnp.int32),
                   jax.ShapeDtypeStruct((N_EXPERTS,), jnp.int32),
                   jax.ShapeDtypeStruct((1,), jnp.int32)],
        scratch_shapes=[pltpu.SMEM((nb,), jnp.int32)],
        name="moe_plan",
    )(cnt)


def _experts_kernel(gsrc, gdst, texp, tend, ntile, xy_in, ws_in, w1_hbm, w3_hbm, w2_hbm, xy_out,
                    xbuf, wsbuf, ybuf, st1, st3, st2, gsem, ssem, wsem):
    nt = ntile[0]

    def weight_copies(e, ws):
        return (pltpu.make_async_copy(w1_hbm.at[e], st1.at[ws], wsem.at[ws]),
                pltpu.make_async_copy(w3_hbm.at[e], st3.at[ws], wsem.at[ws]),
                pltpu.make_async_copy(w2_hbm.at[e], st2.at[ws], wsem.at[ws]))

    def rows(i):
        return pl.ds(pl.multiple_of(i * GRANULE, GRANULE), GRANULE)

    def gather_copies(g, j, sl):
        return (pltpu.make_async_copy(xy_in.at[rows(g), :], xbuf.at[sl, rows(j), :], gsem.at[sl]),
                pltpu.make_async_copy(ws_in.at[rows(g), :], wsbuf.at[sl, rows(j), :], gsem.at[sl]))

    def scatter_copies(g, j, sl):
        return (pltpu.make_async_copy(ybuf.at[sl, rows(j), :], xy_out.at[rows(g), :], ssem.at[sl]),)

    def issue(tt, sl, gmap, copies):
        for j in range(TILE_GRANULES):
            for cp in copies(gmap[tt * TILE_GRANULES + j], j, sl):
                cp.start()

    def drain(sl, copies):
        full = pl.ds(0, EXPERT_TILE)
        if copies is gather_copies:
            pltpu.make_async_copy(xy_in.at[full, :], xbuf.at[sl], gsem.at[sl]).wait()
            pltpu.make_async_copy(ws_in.at[full, :], wsbuf.at[sl], gsem.at[sl]).wait()
        else:
            pltpu.make_async_copy(ybuf.at[sl], xy_out.at[full, :], ssem.at[sl]).wait()

    @pl.when(nt > 0)
    def _():
        last = nt - 1
        for cp in weight_copies(texp[0], 0):
            cp.start()
        for ahead in range(GATHER_DEPTH - 1):
            issue(ahead, ahead, gsrc, gather_copies)

        def tile(t, wslot):
            slot = t % 2
            gslot = t % GATHER_DEPTH
            e = texp[t]
            first = (t == 0) | (texp[jnp.maximum(t - 1, 0)] != e)
            wslot = jnp.where(first & (t > 0), 1 - wslot, wslot)

            drain(gslot, gather_copies)

            @pl.when(t >= 2)
            def _():
                drain(slot, scatter_copies)

            @pl.when(first)
            def _():
                for cp in weight_copies(e, wslot):
                    cp.wait()
                nxt = tend[e]

                @pl.when(nxt < nt)
                def _():
                    for cp in weight_copies(texp[jnp.minimum(nxt, last)], 1 - wslot):
                        cp.start()

            x = xbuf[gslot]
            w_row = wsbuf[gslot][:, 0:1]
            a = _dot(x, st1[wslot].astype(BF16))
            b = _dot(x, st3[wslot].astype(BF16))
            y = _dot((_silu(a) * b).astype(BF16), st2[wslot].astype(BF16)) * w_row
            ybuf[slot] = y.astype(BF16)
            issue(t, slot, gdst, scatter_copies)
            ahead = t + GATHER_DEPTH - 1
            issue(ahead, ahead % GATHER_DEPTH, gsrc, gather_copies)
            return wslot

        lax.fori_loop(0, nt, tile, 0)

        for k in range(GATHER_DEPTH - 1):
            drain((nt + k) % GATHER_DEPTH, gather_copies)
        drain(last % 2, scatter_copies)

        @pl.when(nt >= 2)
        def _():
            drain(nt % 2, scatter_copies)


def _experts(gsrc, gdst, texp, tend, ntile, xy, ws, w1, w3, w2):
    smem = pl.BlockSpec(memory_space=pltpu.SMEM)
    hbm = pl.BlockSpec(memory_space=pl.ANY)
    return pl.pallas_call(
        _experts_kernel,
        in_specs=[smem, smem, smem, smem, smem, hbm, hbm, hbm, hbm, hbm],
        out_specs=hbm,
        out_shape=jax.ShapeDtypeStruct(xy.shape, xy.dtype),
        scratch_shapes=[pltpu.VMEM((GATHER_DEPTH, EXPERT_TILE, D_MODEL), BF16),
                        pltpu.VMEM((GATHER_DEPTH, EXPERT_TILE, LANES), F32),
                        pltpu.VMEM((2, EXPERT_TILE, D_MODEL), BF16),
                        pltpu.VMEM((2, D_MODEL, D_EXPERT), F32),
                        pltpu.VMEM((2, D_MODEL, D_EXPERT), F32),
                        pltpu.VMEM((2, D_EXPERT, D_MODEL), F32),
                        pltpu.SemaphoreType.DMA((GATHER_DEPTH,)),
                        pltpu.SemaphoreType.DMA((2,)),
                        pltpu.SemaphoreType.DMA((2,))],
        input_output_aliases={5: 0},
        compiler_params=pltpu.CompilerParams(vmem_limit_bytes=VMEM_LIMIT),
        name="moe_experts",
    )(gsrc, gdst, texp, tend, ntile, xy, ws, w1, w3, w2)


def _combine_kernel(y_ref, pos_ref, h1_ref, g2_ref, fg_ref, o_ref):
    pos = pos_ref[...]
    pt1, pt2 = _pair_rows(pos[:, 0:1], pos[:, 1:2])
    moe = _dot((pt1 + pt2).astype(BF16), y_ref[...])
    h = h1_ref[...] + g2_ref[...] * moe
    ms = jnp.mean(h * h, axis=-1, keepdims=True)
    o_ref[...] = h * lax.rsqrt(ms + EPS) * fg_ref[...]


def _combine(xy, pos, h1, g2, final_g):
    t = h1.shape[0]
    row = lambda i: (i, 0)
    const = lambda i: (0, 0)
    return pl.pallas_call(
        _combine_kernel,
        grid=(t // DISPATCH_BLOCK,),
        in_specs=[pl.BlockSpec((LOCAL_CAP, D_MODEL), row),
                  pl.BlockSpec((DISPATCH_BLOCK, LANES), row),
                  pl.BlockSpec((DISPATCH_BLOCK, D_MODEL), row),
                  pl.BlockSpec((1, D_MODEL), const), pl.BlockSpec((1, D_MODEL), const)],
        out_specs=pl.BlockSpec((DISPATCH_BLOCK, D_MODEL), row),
        out_shape=jax.ShapeDtypeStruct((t, D_MODEL), F32),
        compiler_params=_cparams(("arbitrary",)),
        name="moe_combine_final",
    )(xy, pos, h1, g2, final_g)


def _pad_lanes(a):
    return jnp.pad(a, ((0, 0), (0, LANES - a.shape[1])))


def kernel(x, c, ctx, c_ctx, w_mod, b_mod, norm1_g, w_in, w_conv_q, w_conv_k, gate_bias, head_norm_g, w_pool, pool_scale, w_out, norm2_g, w_group, b_group, w_router, b_router, w1, w3, w2, final_g):
    assert x.shape[0] == 1 and w_mod.shape[0] == 1
    seq = x.shape[1]
    x2d = x[0]
    ctx2d = ctx[0]

    n_main = 5 * MIX_HALF
    w_in_t = jnp.transpose(w_in[0])
    w_gate = _pad_lanes(jnp.transpose(_take_rows(w_in_t, n_main, N_GATES))).astype(BF16)
    gate_bias_row = _pad_lanes(gate_bias[0].reshape(1, N_GATES))
    wr_pack, wr_hi = _split_pack(jnp.concatenate([w_group[0], w_router[0]], axis=1))
    b_route = _pad_lanes(jnp.concatenate([b_group[0], b_router[0]]).reshape(1, -1))
    norm1 = norm1_g[0].reshape(1, D_MODEL)

    c16 = jnp.concatenate([c, c_ctx[None, :], jnp.zeros((14, D_MODEL), F32)], axis=0)
    mods = _adaln(c16, w_mod[0], b_mod[0].reshape(1, -1))
    mod_lat = mods[0].reshape(6, D_MODEL)
    mod_ctx = mods[1].reshape(6, D_MODEL)

    mod_in = jnp.concatenate([mod_ctx[0:2], mod_lat[0:2]], axis=0)
    u_pool, q, uo, k, kt, v, col, rowi = _inproj(x2d, ctx2d, mod_in, norm1, w_in_t, w_gate,
                                                 gate_bias_row, w_conv_q[0], w_conv_k[0])
    nc = seq // CHUNK
    zeros_state = (jnp.zeros((2 * HEADS, HEAD_DIM, HEAD_DIM), F32),
                   jnp.zeros((2 * HEADS, 1, HEAD_DIM), F32),
                   jnp.zeros((2 * HEADS, 1, LANES), F32))
    s0, n0, m0 = _mlstm(None, k, kt, v, col, rowi, *zeros_state, need_out=False, first=nc, nc=1)

    hf, hb = _mlstm(q, k, kt, v, col, rowi, s0, n0, m0, need_out=True, first=0, nc=nc)
    p = _pool(u_pool, w_pool[0], pool_scale[0].reshape(1, -1), seq)
    h1, fn, info = _outproj(p, hf, hb, uo, x2d, w_out[0], mod_lat, head_norm_g[0].reshape(1, -1),
                            norm2_g[0].reshape(1, -1), wr_pack, wr_hi, b_route, tm=2 * OUTPROJ_SUB)

    xs, ws, pos, cnt = _dispatch(fn, info)
    gsrc, gdst, texp, tend, ntile = _plan(cnt[::8, :N_EXPERTS], seq)
    xy = _experts(gsrc, gdst, texp, tend, ntile, xs, ws, w1[0], w3[0], w2[0])
    out = _combine(xy, pos, h1, mod_lat[5:6], final_g.reshape(1, -1))
    return out.reshape(1, seq, D_MODEL)
```

```python
import functools

import jax
import jax.numpy as jnp
from jax import lax
from jax.experimental import pallas as pl
from jax.experimental.pallas import tpu as pltpu

F32 = jnp.float32
BF16 = jnp.bfloat16

D_MODEL = 2048
GRID_W = 64
GRID_SHIFT = 6
POOL_WINDOWS = (2, 4, 8, 16)
POOL_GROUP = 256
HEADS = 4
HEAD_DIM = 256
MIX_HALF = 1024
N_GATES = 16
N_GROUPS = 4
EXPERTS_PER_GROUP = 8
N_EXPERTS = 32
D_EXPERT = 512
EPS = 1e-6
LANES = 128
CHUNK = 256
ROUTE_LANE0 = N_GROUPS

VMEM_LIMIT = 60 * 1024 * 1024


def _cparams(sem, vmem=VMEM_LIMIT):
    return pltpu.CompilerParams(dimension_semantics=sem, vmem_limit_bytes=vmem)


def _split2(x):
    hi = x.astype(BF16)
    lo = (x - hi.astype(F32)).astype(BF16)
    return hi, lo


def _split3(x):
    hi = x.astype(BF16)
    r = x - hi.astype(F32)
    mid = r.astype(BF16)
    lo = (r - mid.astype(F32)).astype(BF16)
    return hi, mid, lo


def _dot(a, b):
    return jnp.dot(a, b, preferred_element_type=F32)


SPLIT_LANE = 64


def _split_pack(w):
    hi, lo = _split2(w)
    n = w.shape[1]
    gap = jnp.zeros((w.shape[0], SPLIT_LANE - n), BF16)
    rest = jnp.zeros((w.shape[0], LANES - n), BF16)
    return jnp.concatenate([hi, gap, lo, gap], axis=1), jnp.concatenate([hi, rest], axis=1)


def _split_dot(xh, xl, w_packed, w_hi):
    r = _dot(xh, w_packed)
    return r + pltpu.roll(r, SPLIT_LANE, 1) + _dot(xl, w_hi)


def _silu(x):
    return x * jax.nn.sigmoid(x)


def _log_sigmoid(x):
    return jnp.minimum(x, 0.0) - jnp.log(1.0 + jnp.exp(-jnp.abs(x)))


def _copy_kernel(w_ref, o_ref):
    o_ref[...] = w_ref[...]


def _take_rows(w, start, n):
    return pl.pallas_call(
        _copy_kernel,
        grid=(1,),
        in_specs=[pl.BlockSpec((n, w.shape[1]), lambda i: (start // n, 0))],
        out_specs=pl.BlockSpec((n, w.shape[1]), lambda i: (0, 0)),
        out_shape=jax.ShapeDtypeStruct((n, w.shape[1]), w.dtype),
        name="take_rows",
    )(w)


def _adaln_kernel(c_ref, w_ref, b_ref, o_ref):
    a = _silu(c_ref[...])
    a3 = jnp.concatenate(_split3(a), axis=0)
    w_hi, w_lo = _split2(w_ref[...])
    acc = _dot(a3, w_hi)
    acc_lo = _dot(a3[:32], w_lo)
    out = acc[0:16] + acc[16:32] + acc[32:48] + acc_lo[0:16] + acc_lo[16:32]
    o_ref[...] = out + b_ref[...]


ADALN_COLS = 1536


def _adaln(c16, w_mod, b_mod):
    n = w_mod.shape[1]
    tn = ADALN_COLS
    return pl.pallas_call(
        _adaln_kernel,
        grid=(n // tn,),
        in_specs=[pl.BlockSpec((16, D_MODEL), lambda j: (0, 0)),
                  pl.BlockSpec((D_MODEL, tn), lambda j: (0, j)),
                  pl.BlockSpec((1, tn), lambda j: (0, j))],
        out_specs=pl.BlockSpec((16, tn), lambda j: (0, j)),
        out_shape=jax.ShapeDtypeStruct((16, n), F32),
        compiler_params=_cparams(("arbitrary",)),
        name="adaln",
    )(c16, w_mod, b_mod)


HALO = 8
NEG_INF = float("-inf")


def _gate_scan_info(gates):
    n = gates.shape[0]
    lane = lax.broadcasted_iota(jnp.int32, gates.shape, 1)
    rows = lax.broadcasted_iota(jnp.int32, gates.shape, 0)
    lf = jnp.where(lane < N_GATES, _log_sigmoid(gates), 0.0)
    hi = lf.astype(BF16).astype(F32)
    rem = lf - hi
    mid = rem.astype(BF16).astype(F32)
    packed = (hi + pltpu.roll(mid, 32, 1) + pltpu.roll(rem - mid, 64, 1)).astype(BF16)
    r = lax.broadcasted_iota(jnp.int32, (n, n), 0)
    c = lax.broadcasted_iota(jnp.int32, (n, n), 1)
    pf = _dot(jnp.where(r >= c, 1.0, 0.0).astype(BF16), packed)
    pb = _dot(jnp.where(r <= c, 1.0, 0.0).astype(BF16), packed)
    bf = pf + pltpu.roll(pf, 96, 1) + pltpu.roll(pf, 64, 1)
    bb = pb + pltpu.roll(pb, 96, 1) + pltpu.roll(pb, 64, 1)
    b = jnp.where(lane < 8, bf, bb)
    cval = gates - pltpu.roll(b, LANES - 4, 1)
    pm = cval
    sm = cval
    step = 1
    while step < n:
        pm = jnp.maximum(pm, jnp.where(rows >= step, pltpu.roll(pm, step, 0), NEG_INF))
        sm = jnp.maximum(sm, jnp.where(rows < n - step, pltpu.roll(sm, n - step, 0), NEG_INF))
        step *= 2
    cm = jnp.where(lane < 8, pm, sm)
    return cval, jnp.where((lane & 4) == 0, cm, b)


W_CHUNK = 256


def _load_bf16(src_hbm, dst, stage, sem, *, n_chunks, transpose):
    rows = stage.shape[1]
    def chunk_copy(c, sl):
        return pltpu.make_async_copy(
            src_hbm.at[pl.ds(pl.multiple_of(c * rows, rows), rows), :], stage.at[sl], sem.at[sl])

    chunk_copy(0, 0).start()

    def convert(c, carry):
        sl = c % 2

        @pl.when(c + 1 < n_chunks)
        def _():
            chunk_copy(c + 1, 1 - sl).start()

        chunk_copy(c, sl).wait()
        span = pl.ds(pl.multiple_of(c * rows, rows), rows)
        if transpose:
            dst[:, span] = stage[sl].T.astype(BF16)
        else:
            dst[span, :] = stage[sl].astype(BF16)
        return carry

    lax.fori_loop(0, n_chunks, convert, 0)


def _inproj_kernel(x_ref, xp_ref, xn_ref, ctx_ref, mod_ref, g_ref, wt_hbm, wg_ref, gb_ref,
                   wcq_ref, wck_ref, pool_o, q_o, o_o, k_o, kt_o, v_o, col_o, row_o,
                   w_s, stage, sem, *, nt):
    i = pl.program_id(0)
    tm = x_ref.shape[0]

    @pl.when(i == 0)
    def _():
        _load_bf16(wt_hbm, w_s, stage, sem, n_chunks=w_s.shape[1] // W_CHUNK, transpose=True)

    is_ctx = i == nt
    x_main = jnp.where(is_ctx, ctx_ref[...], x_ref[...])
    x_all = jnp.concatenate([xp_ref[...], x_main, xn_ref[...]], axis=0)
    shift = jnp.where(is_ctx, mod_ref[0:1, :], mod_ref[2:3, :])
    scale = jnp.where(is_ctx, mod_ref[1:2, :], mod_ref[3:4, :])
    ms = jnp.mean(x_all * x_all, axis=-1, keepdims=True)
    y = x_all * lax.rsqrt(ms + EPS) * g_ref[...]
    xn_all = y * (1.0 + scale) + shift
    xh_all = xn_all.astype(BF16)
    xh = xn_all[HALO:HALO + tm].astype(BF16)

    def cols(ci):
        return w_s[:, ci * MIX_HALF:(ci + 1) * MIX_HALF]

    u_q = _dot(xh_all, cols(1))
    u_k = _dot(xh_all, cols(3))
    pool_o[...] = _dot(xh, cols(0))
    o_o[...] = _dot(xh, cols(2))
    v_o[...] = _dot(xh, cols(4)).astype(BF16)

    keep_prev = jnp.where(jnp.logical_or(i == 0, is_ctx), 0.0, 1.0)
    keep_next = jnp.where(jnp.logical_or(i == nt - 1, is_ctx), 0.0, 1.0)

    def conv_silu(u, wc_ref):
        n = u.shape[0]
        u = jnp.concatenate([u[:HALO] * keep_prev, u[HALO:HALO + tm], u[HALO + tm:] * keep_next], axis=0)
        up = pltpu.roll(u, 1, 0)[HALO:HALO + tm]
        un = pltpu.roll(u, n - 1, 0)[HALO:HALO + tm]
        return _silu(wc_ref[0:1, :] * up + wc_ref[1:2, :] * u[HALO:HALO + tm] + wc_ref[2:3, :] * un)

    q_o[...] = (conv_silu(u_q, wcq_ref) * (HEAD_DIM ** -0.5)).astype(BF16)
    k = conv_silu(u_k, wck_ref)
    k_o[...] = k.astype(BF16)
    kt_o[...] = k.astype(BF16).T
    gates = _dot(xh, wg_ref[...]) + gb_ref[...]
    cval, col = _gate_scan_info(gates)
    col_o[...] = col[:, :N_GATES]
    row_o[...] = cval.T[:N_GATES, :]


def _inproj(x2d, ctx2d, mod, g, w_in_t, w_gate, gate_bias_row, wcq, wck):
    t = x2d.shape[0]
    tm = CHUNK
    assert ctx2d.shape[0] == tm
    nt = t // tm
    r8 = tm // HALO
    last8 = t // HALO - 1
    n_main = 5 * MIX_HALF
    rows = t + tm
    const = lambda i: (0, 0)
    row = lambda i: (i, 0)
    f32_out = jax.ShapeDtypeStruct((rows, MIX_HALF), F32)
    bf16_out = jax.ShapeDtypeStruct((rows, MIX_HALF), BF16)
    seq = pl.BlockSpec((tm, MIX_HALF), row)
    return pl.pallas_call(
        functools.partial(_inproj_kernel, nt=nt),
        grid=(nt + 1,),
        in_specs=[pl.BlockSpec((tm, D_MODEL), lambda i: (jnp.minimum(i, nt - 1), 0)),
                  pl.BlockSpec((HALO, D_MODEL), lambda i: (jnp.clip(i * r8 - 1, 0, last8), 0)),
                  pl.BlockSpec((HALO, D_MODEL), lambda i: (jnp.minimum((i + 1) * r8, last8), 0)),
                  pl.BlockSpec((tm, D_MODEL), const),
                  pl.BlockSpec((4, D_MODEL), const),
                  pl.BlockSpec((1, D_MODEL), const),
                  pl.BlockSpec(memory_space=pl.ANY),
                  pl.BlockSpec((D_MODEL, LANES), const),
                  pl.BlockSpec((1, LANES), const),
                  pl.BlockSpec((3, MIX_HALF), const),
                  pl.BlockSpec((3, MIX_HALF), const)],
        out_specs=[seq, seq, seq, seq, pl.BlockSpec((MIX_HALF, tm), lambda i: (0, i)), seq,
                   pl.BlockSpec((tm, N_GATES), row), pl.BlockSpec((N_GATES, tm), lambda i: (0, i))],
        out_shape=[f32_out, bf16_out, f32_out, bf16_out,
                   jax.ShapeDtypeStruct((MIX_HALF, rows), BF16), bf16_out,
                   jax.ShapeDtypeStruct((rows, N_GATES), F32),
                   jax.ShapeDtypeStruct((N_GATES, rows), F32)],
        scratch_shapes=[pltpu.VMEM((D_MODEL, n_main), BF16),
                        pltpu.VMEM((2, W_CHUNK, D_MODEL), F32),
                        pltpu.SemaphoreType.DMA((2,))],
        compiler_params=_cparams(("arbitrary",)),
        name="inproj",
    )(x2d, x2d, x2d, ctx2d, mod, g, w_in_t, w_gate, gate_bias_row, wcq, wck)


def _mlstm_kernel(*refs, need_out):
    if need_out:
        (qf, kf, ktf, vf, colf, rowf, qb, kb, ktb, vb, colb, rowb, s0, n0, m0,
         hf_o, hb_o, s_s, n_s, m_s) = refs
        q_refs, h_outs = (qf, qb), (hf_o, hb_o)
    else:
        (kf, ktf, vf, colf, rowf, kb, ktb, vb, colb, rowb, s0, n0, m0,
         s_o, n_o, m_o, s_s, n_s, m_s) = refs
    k_refs, kt_refs, v_refs = (kf, kb), (ktf, ktb), (vf, vb)
    col_refs, row_refs = (colf, colb), (rowf, rowb)
    j = pl.program_id(0)

    @pl.when(j == 0)
    def _():
        s_s[...] = s0[...]
        n_s[...] = n0[...]
        m_s[...] = m0[...]

    L = CHUNK
    row = lax.broadcasted_iota(jnp.int32, (L, L), 0)
    col = lax.broadcasted_iota(jnp.int32, (L, L), 1)

    heads = [(d, h) for d in range(2) for h in range(HEADS)]

    def head_values(d, h):
        hd = d * HEADS + h
        lc, lb = 8 * d + h, 8 * d + 4 + h
        hs = slice(h * HEAD_DIM, (h + 1) * HEAD_DIM)
        edge = L - 1 if d == 0 else 0
        colv = col_refs[d][...]
        rowv = row_refs[d][...]
        v = dict(hd=hd, hs=hs, d=d, lc=lc, colv=colv)
        v["cm_c"], v["b_c"] = colv[:, lc:lc + 1], colv[:, lb:lb + 1]
        v["c_r"] = rowv[lc:lc + 1, :]
        v["g"] = v["b_c"][edge:edge + 1, :]
        v["m_old"] = m_s[hd][:, 0:1]
        v["s_old"] = s_s[hd]
        v["n_old"] = n_s[hd]
        v["m_x"] = jnp.maximum(v["m_old"], v["cm_c"][edge:edge + 1, :])
        v["decay"] = jnp.exp(v["m_old"] - v["m_x"])
        v["wk_r"] = jnp.exp(v["c_r"] - v["m_x"])
        v["v_h"] = v_refs[d][:, hs]
        v["kt_h"] = kt_refs[d][hs, :]
        return v

    def update_state(v):
        hd = v["hd"]
        n_s[hd] = v["decay"] * v["n_old"] + _dot(
            jnp.broadcast_to(v["wk_r"], (8, L)).astype(BF16), k_refs[v["d"]][:, v["hs"]])[0:1, :]
        kwt = (v["kt_h"].astype(F32) * v["wk_r"]).astype(BF16)
        s_s[hd] = v["decay"] * v["s_old"] + _dot(kwt, v["v_h"])
        m_s[hd] = jnp.broadcast_to(v["g"] + v["m_x"], (1, LANES))

    def weights(v):
        d, hs = v["d"], v["hs"]
        mask = (row >= col) if d == 0 else (row <= col)
        q_h = q_refs[d][:, hs]
        m_b = m_s[v["hd"]][:, :N_GATES]
        m16 = jnp.maximum(m_b, v["colv"])
        m_c = m16[:, v["lc"]:v["lc"] + 1]
        w_inter = jnp.exp(m_b - m16)[:, v["lc"]:v["lc"] + 1]
        v["e"] = jnp.where(mask, jnp.exp(v["c_r"] - m_c), 0.0)
        v["qw"] = (q_h.astype(F32) * w_inter).astype(BF16)
        qn = lax.dot_general(q_h, jnp.broadcast_to(v["n_old"], (8, HEAD_DIM)).astype(BF16),
                             (((1,), (1,)), ((), ())), preferred_element_type=F32)
        v["den_inter"] = w_inter * qn[:, 0:1]
        v["floor"] = jnp.exp(-(v["b_c"] + m_c))
        v["s_bf"] = v["s_old"].astype(BF16)

    def outputs(v, qk):
        p = (v["e"] * qk).astype(BF16)
        num = _dot(v["qw"], v["s_bf"]) + _dot(p, v["v_h"])
        p_sum = lax.dot_general(p, jnp.ones((8, L), BF16), (((1,), (1,)), ((), ())),
                                preferred_element_type=F32)
        den = v["den_inter"] + p_sum[:, 0:1]
        h_outs[v["d"]][:, v["hs"]] = num / jnp.maximum(jnp.abs(den), v["floor"])

    vals = [head_values(d, h) for d, h in heads]
    if need_out:
        qks = [_dot(q_refs[v["d"]][:, v["hs"]], v["kt_h"]) for v in vals]
        for v in vals:
            weights(v)
    for v in vals:
        update_state(v)
    if need_out:
        for v, qk in zip(vals, qks):
            outputs(v, qk)

    if not need_out:
        s_o[...] = s_s[...]
        n_o[...] = n_s[...]
        m_o[...] = m_s[...]


def _mlstm(q, k, kt, v, col, rowi, s0, n0, m0, need_out, first, nc):
    t = nc * CHUNK
    fwd = lambda j: (first + j, 0)
    bwd = lambda j: (first + nc - 1 - j, 0)
    fwd_t = lambda j: (0, first + j)
    bwd_t = lambda j: (0, first + nc - 1 - j)
    c3 = lambda j: (0, 0, 0)
    seq = lambda im: pl.BlockSpec((CHUNK, MIX_HALF), im)
    state_specs = [pl.BlockSpec(s0.shape, c3), pl.BlockSpec(n0.shape, c3), pl.BlockSpec(m0.shape, c3)]
    scratch = [pltpu.VMEM(s0.shape, F32), pltpu.VMEM(n0.shape, F32), pltpu.VMEM(m0.shape, F32)]

    def side(im, im_t):
        specs = ([seq(im)] if need_out else []) + [seq(im), pl.BlockSpec((MIX_HALF, CHUNK), im_t), seq(im)]
        return specs + [pl.BlockSpec((CHUNK, N_GATES), im), pl.BlockSpec((N_GATES, CHUNK), im_t)]

    in_specs = side(fwd, fwd_t) + side(bwd, bwd_t) + state_specs
    seq_in = ((q,) if need_out else ()) + (k, kt, v, col, rowi)
    args = seq_in + seq_in + (s0, n0, m0)
    if need_out:
        out_specs = [seq(fwd), seq(bwd)]
        out_shape = [jax.ShapeDtypeStruct((t, MIX_HALF), F32)] * 2
    else:
        out_specs = state_specs
        out_shape = [jax.ShapeDtypeStruct(a.shape, F32) for a in (s0, n0, m0)]
    return pl.pallas_call(
        functools.partial(_mlstm_kernel, need_out=need_out),
        grid=(nc,),
        in_specs=in_specs,
        out_specs=out_specs,
        out_shape=out_shape,
        scratch_shapes=scratch,
        compiler_params=_cparams(("arbitrary",)),
        name="mlstm_out" if need_out else "mlstm_state",
    )(*args)


POOL_PAD = 512
POOL_UNROLL = 4


def _pool_kernel(u_ref, w_ref, sc_ref, o_ref, pad_s, *, t):
    for gi, win in enumerate(POOL_WINDOWS):
        @pl.when(pl.program_id(0) == gi)
        def _():
            _pool_group(u_ref, w_ref, sc_ref, o_ref, pad_s, win=win, t=t)


def _pool_group(u_ref, w_ref, sc_ref, o_ref, pad_s, *, win, t):
    half = win // 2
    tile = 256
    zeros = jnp.zeros((POOL_PAD, POOL_GROUP), F32)
    pad_s[0:POOL_PAD, :] = zeros
    pad_s[POOL_PAD + t:POOL_PAD + t + POOL_PAD, :] = zeros

    def copy(r, carry):
        t0 = pl.multiple_of(r * tile, tile)
        pad_s[pl.ds(POOL_PAD + t0, tile), :] = u_ref[pl.ds(t0, tile), :]
        return carry

    lax.fori_loop(0, t // tile, copy, 0)

    row = lax.broadcasted_iota(jnp.int32, (tile, tile), 0)
    col = lax.broadcasted_iota(jnp.int32, (tile, tile), 1)
    same_row = (row >> GRID_SHIFT) == (col >> GRID_SHIFT)
    in_win = (col - row >= -half) & (col - row < half)
    band = jnp.where(same_row & in_win, 1.0, 0.0).astype(BF16)
    w = w_ref[...].astype(BF16)
    scale = sc_ref[...]
    n_rows = t // GRID_W

    tok0 = lax.broadcasted_iota(jnp.int32, (tile, POOL_GROUP), 0)
    gc = tok0 & (GRID_W - 1)
    inv_h = 1.0 / (jnp.minimum(gc + half, GRID_W) - jnp.maximum(gc - half, 0)).astype(F32)

    def body(r, carry):
        t0s = [pl.multiple_of((r * POOL_UNROLL + k) * tile, tile) for k in range(POOL_UNROLL)]
        pieces = []
        for t0 in t0s:
            acc = pad_s[pl.ds(POOL_PAD + t0 - GRID_W * half, tile), :]
            for dd in range(-half + 1, half):
                acc = acc + pad_s[pl.ds(POOL_PAD + t0 + GRID_W * dd, tile), :]
            gr = (t0 + tok0) >> GRID_SHIFT
            cnt_v = jnp.minimum(gr + half, n_rows) - jnp.maximum(gr - half, 0)
            pieces.append(_split2(acc / cnt_v.astype(F32)))
        means = [(_dot(band, hi) + _dot(band, lo)) * inv_h for hi, lo in pieces]
        diffs = [(m - pad_s[pl.ds(POOL_PAD + t0, tile), :]).astype(BF16) for m, t0 in zip(means, t0s)]
        for d, t0 in zip(diffs, t0s):
            o_ref[pl.ds(t0, tile), :] = (_dot(d, w) * scale).astype(BF16)
        return carry

    lax.fori_loop(0, t // (tile * POOL_UNROLL), body, 0)


def _pool(u_pool, w_pool, scale_row, t):
    return pl.pallas_call(
        functools.partial(_pool_kernel, t=t),
        grid=(len(POOL_WINDOWS),),
        in_specs=[pl.BlockSpec((t, POOL_GROUP), lambda g: (0, g)),
                  pl.BlockSpec((None, POOL_GROUP, POOL_GROUP), lambda g: (g, 0, 0)),
                  pl.BlockSpec((1, POOL_GROUP), lambda g: (0, g))],
        out_specs=pl.BlockSpec((t, POOL_GROUP), lambda g: (0, g)),
        out_shape=jax.ShapeDtypeStruct((t, MIX_HALF), BF16),
        scratch_shapes=[pltpu.VMEM((t + 2 * POOL_PAD, POOL_GROUP), F32)],
        compiler_params=_cparams(("arbitrary",)),
        name="pool_mix",
    )(u_pool, w_pool, scale_row)


OUTPROJ_SUB = 256


def _route(logits):
    lane = lax.broadcasted_iota(jnp.int32, logits.shape, 1).astype(F32)
    neg = -jnp.inf
    big = float(LANES)
    gl = jnp.where(lane < N_GROUPS, logits, neg)
    gmax = jnp.max(gl, axis=1, keepdims=True)
    gsel = jnp.min(jnp.where(gl == gmax, lane, big), axis=1, keepdims=True)
    p_grp = 1.0 / jnp.sum(jnp.exp(gl - gmax), axis=1, keepdims=True)
    lo = ROUTE_LANE0 + EXPERTS_PER_GROUP * gsel
    el = jnp.where((lane >= lo) & (lane < lo + EXPERTS_PER_GROUP), logits, neg)
    m1 = jnp.max(el, axis=1, keepdims=True)
    i1 = jnp.min(jnp.where(el == m1, lane, big), axis=1, keepdims=True)
    el2 = jnp.where(lane == i1, neg, el)
    m2 = jnp.max(el2, axis=1, keepdims=True)
    i2 = jnp.min(jnp.where(el2 == m2, lane, big), axis=1, keepdims=True)
    e2 = jnp.exp(m2 - m1)
    p1 = 1.0 / (1.0 + e2)
    p2 = e2 / (1.0 + e2)
    info = jnp.where(lane == 0.0, i1 - ROUTE_LANE0, 0.0)
    info = jnp.where(lane == 1.0, i2 - ROUTE_LANE0, info)
    info = jnp.where(lane == 2.0, p_grp * p1, info)
    return jnp.where(lane == 3.0, p_grp * p2, info)


def _outproj_kernel(p_ref, hf_ref, hb_ref, uo_ref, x_ref, wout_hbm, mod_ref,
                    hg_ref, n2g_ref, wrp_ref, wrh_ref, br_ref, h1_o, fn_o, info_o, w_s, stage, sem):
    @pl.when(pl.program_id(0) == 0)
    def _():
        _load_bf16(wout_hbm, w_s, stage, sem, n_chunks=w_s.shape[0] // stage.shape[1], transpose=False)

    def mixer_input(rs):
        h = hf_ref[rs, :] + hb_ref[rs, :]
        parts = []
        for hh in range(HEADS):
            hs = h[:, hh * HEAD_DIM:(hh + 1) * HEAD_DIM]
            mu = jnp.mean(hs, axis=-1, keepdims=True)
            ctr = hs - mu
            var = jnp.mean(ctr * ctr, axis=-1, keepdims=True)
            parts.append(ctr * lax.rsqrt(var + EPS))
        hn = jnp.concatenate(parts, axis=1) * hg_ref[...]
        m = (hn * jax.nn.sigmoid(uo_ref[rs, :])).astype(BF16)
        return jnp.concatenate([p_ref[rs, :], m], axis=1)

    def finish(rs, mix):
        h1 = x_ref[rs, :] + mod_ref[2:3, :] * mix
        h1_o[rs, :] = h1
        ms = jnp.mean(h1 * h1, axis=-1, keepdims=True)
        fn = h1 * lax.rsqrt(ms + EPS) * n2g_ref[...]
        fn = fn * (1.0 + mod_ref[4:5, :]) + mod_ref[3:4, :]
        fh, fl = _split2(fn)
        fn_o[rs, :] = fh
        logits = _split_dot(fh, fl, wrp_ref[...], wrh_ref[...]) + br_ref[...]
        info_o[rs, :] = _route(logits)

    tm = x_ref.shape[0]
    subs = [slice(r0, r0 + OUTPROJ_SUB) for r0 in range(0, tm, OUTPROJ_SUB)]
    w = w_s[...]
    mixes = [_dot(mixer_input(rs), w) for rs in subs]
    for rs, mix in zip(subs, mixes):
        finish(rs, mix)


def _outproj(p, hf, hb, uo, x2d, w_out, mod, head_g, norm2_g, wr_pack, wr_hi, b_route, tm):
    t = x2d.shape[0]
    const = lambda i: (0, 0)
    row = lambda i: (i, 0)
    in_specs = ([pl.BlockSpec((tm, MIX_HALF), row)] * 4
                + [pl.BlockSpec((tm, D_MODEL), row),
                   pl.BlockSpec(memory_space=pl.ANY),
                   pl.BlockSpec(mod.shape, const),
                   pl.BlockSpec((1, MIX_HALF), const),
                   pl.BlockSpec((1, D_MODEL), const),
                   pl.BlockSpec((D_MODEL, LANES), const),
                   pl.BlockSpec((D_MODEL, LANES), const),
                   pl.BlockSpec((1, LANES), const)])
    return pl.pallas_call(
        _outproj_kernel,
        grid=(t // tm,),
        in_specs=in_specs,
        out_specs=[pl.BlockSpec((tm, D_MODEL), row), pl.BlockSpec((tm, D_MODEL), row),
                   pl.BlockSpec((tm, LANES), row)],
        out_shape=[jax.ShapeDtypeStruct((t, D_MODEL), F32),
                   jax.ShapeDtypeStruct((t, D_MODEL), BF16),
                   jax.ShapeDtypeStruct((t, LANES), F32)],
        scratch_shapes=[pltpu.VMEM(w_out.shape, BF16),
                        pltpu.VMEM((2, W_CHUNK // 2, w_out.shape[1]), F32),
                        pltpu.SemaphoreType.DMA((2,))],
        compiler_params=_cparams(("arbitrary",)),
        name="outproj_route",
    )(p, hf, hb, uo, x2d, w_out, mod, head_g, norm2_g, wr_pack, wr_hi, b_route)


DISPATCH_BLOCK = 512
GRANULE = 16
GRANULE_SHIFT = 4
LOCAL_CAP = 1536
LOCAL_GRANULES = LOCAL_CAP // GRANULE
FREE_GRANULES = 2
EXPERT_TILE = 256
TILE_GRANULES = EXPERT_TILE // GRANULE
PLAN_UNROLL = 4
PLAN_SLACK = 8
GATHER_DEPTH = 4


def _pair_rows(pos1, pos2):
    rows = lax.broadcasted_iota(jnp.int32, (pos1.shape[0], LOCAL_CAP), 1).astype(F32)
    return jnp.where(rows == pos1, 1.0, 0.0), jnp.where(rows == pos2, 1.0, 0.0)


def _dispatch_kernel(fn_ref, info_ref, xs_o, ws_o, pos_o, cnt_o):
    tb = DISPATCH_BLOCK
    info = info_ref[...]
    e1, e2 = info[:, 0:1], info[:, 1:2]
    w1c, w2c = info[:, 2:3], info[:, 3:4]
    lane = lax.broadcasted_iota(jnp.int32, (tb, LANES), 1).astype(F32)
    o1 = jnp.where(lane == e1, 1.0, 0.0)
    o2 = jnp.where(lane == e2, 1.0, 0.0)
    onehot = o1 + o2
    cnt = jnp.sum(onehot, axis=0, keepdims=True)
    gran = jnp.floor((cnt + (GRANULE - 1)) * (1.0 / GRANULE))
    a = lax.broadcasted_iota(jnp.int32, (LANES, LANES), 0)
    b = lax.broadcasted_iota(jnp.int32, (LANES, LANES), 1)
    upper = jnp.where(a < b, 1.0, 0.0).astype(BF16)
    seg_off = _dot(jnp.broadcast_to(gran, (8, LANES)).astype(BF16), upper)[0:1, :] * GRANULE
    r = lax.broadcasted_iota(jnp.int32, (tb, tb), 0)
    c = lax.broadcasted_iota(jnp.int32, (tb, tb), 1)
    strict = jnp.where(r > c, 1.0, 0.0).astype(BF16)
    rank = _dot(strict, onehot.astype(BF16))
    slot = rank + seg_off
    pos1 = jnp.sum(o1 * slot, axis=1, keepdims=True)
    pos2 = jnp.sum(o2 * slot, axis=1, keepdims=True)
    pt1, pt2 = _pair_rows(pos1, pos2)
    pos_o[...] = jnp.where(lane == 0.0, pos1, jnp.where(lane == 1.0, pos2, 0.0))
    perm = (pt1 + pt2).T.astype(BF16)
    w_slot = jnp.sum((pt1 * w1c + pt2 * w2c).T, axis=1, keepdims=True)
    for c0 in range(0, D_MODEL, 512):
        xs_o[:, c0:c0 + 512] = _dot(perm, fn_ref[:, c0:c0 + 512]).astype(BF16)
    ws_o[...] = jnp.broadcast_to(w_slot, (LOCAL_CAP, LANES))
    cnt_o[...] = jnp.broadcast_to(cnt, (8, LANES)).astype(jnp.int32)


def _dispatch(fn, info):
    t = fn.shape[0]
    nb = t // DISPATCH_BLOCK
    row = lambda i: (i, 0)
    return pl.pallas_call(
        _dispatch_kernel,
        grid=(nb,),
        in_specs=[pl.BlockSpec((DISPATCH_BLOCK, D_MODEL), row),
                  pl.BlockSpec((DISPATCH_BLOCK, LANES), row)],
        out_specs=[pl.BlockSpec((LOCAL_CAP, D_MODEL), row),
                   pl.BlockSpec((LOCAL_CAP, LANES), row),
                   pl.BlockSpec((DISPATCH_BLOCK, LANES), row),
                   pl.BlockSpec((8, LANES), row)],
        out_shape=[jax.ShapeDtypeStruct((nb * LOCAL_CAP, D_MODEL), BF16),
                   jax.ShapeDtypeStruct((nb * LOCAL_CAP, LANES), F32),
                   jax.ShapeDtypeStruct((t, LANES), F32),
                   jax.ShapeDtypeStruct((nb * 8, LANES), jnp.int32)],
        compiler_params=_cparams(("arbitrary",)),
        name="moe_dispatch",
    )(fn, info)


def _map_len(max_tiles):
    return (max_tiles + GATHER_DEPTH - 1) * TILE_GRANULES + PLAN_SLACK


def _free_granule(q):
    return ((q // FREE_GRANULES) * LOCAL_GRANULES + (LOCAL_GRANULES - FREE_GRANULES)
            + (q % FREE_GRANULES))


def _plan_kernel(cnt_ref, gsrc_o, gdst_o, texp_o, tend_o, ntile_o, lrun, *, nb, max_tiles):
    def init(b, c):
        lrun[b] = 0
        return c

    lax.fori_loop(0, nb, init, 0)

    def per_expert(e, carry):
        g0, last_e = carry

        def per_block(b, g):
            k = (cnt_ref[b, e] + (GRANULE - 1)) >> GRANULE_SHIFT
            lo = lrun[b]
            lrun[b] = lo + k

            base = b * LOCAL_GRANULES + lo

            for j in range(PLAN_UNROLL):
                gsrc_o[g + j] = base + j
                gdst_o[g + j] = base + j

            @pl.when(k > PLAN_UNROLL)
            def _():
                def put(j, c):
                    gsrc_o[g + j] = base + j
                    gdst_o[g + j] = base + j
                    return c

                lax.fori_loop(PLAN_UNROLL, k, put, 0)

            return g + k

        g1 = lax.fori_loop(0, nb, per_block, g0, unroll=True)
        pad = (-g1) & (TILE_GRANULES - 1)

        def put_pad(j, c):
            g = g1 + j
            parity = (g // TILE_GRANULES) & 1
            gsrc_o[g] = _free_granule(0)
            gdst_o[g] = _free_granule(1 + parity * (TILE_GRANULES - 1) + (g & (TILE_GRANULES - 1)))
            return c

        lax.fori_loop(0, pad, put_pad, 0)
        g2 = g1 + pad

        def put_tile(tt, c):
            texp_o[tt] = e
            return c

        lax.fori_loop(g0 // TILE_GRANULES, g2 // TILE_GRANULES, put_tile, 0)
        tend_o[e] = g2 // TILE_GRANULES
        return g2, jnp.where(g2 > g0, e, last_e)

    g_end, last_e = lax.fori_loop(0, N_EXPERTS, per_expert, (0, 0))
    n_tiles = g_end // TILE_GRANULES
    ntile_o[0] = n_tiles

    def fill(tt, c):
        texp_o[tt] = last_e
        return c

    lax.fori_loop(n_tiles, max_tiles, fill, 0)

    def fill_map(g, c):
        gsrc_o[g] = _free_granule(0)
        gdst_o[g] = _free_granule(0)
        return c

    lax.fori_loop(g_end, _map_len(max_tiles), fill_map, 0)


def _max_tiles(t):
    nb = t // DISPATCH_BLOCK
    worst_rows = 2 * t + nb * N_EXPERTS * (GRANULE - 1) + N_EXPERTS * (EXPERT_TILE - GRANULE)
    return -(-worst_rows // EXPERT_TILE)


def _plan(cnt, t):
    nb = cnt.shape[0]
    assert nb * FREE_GRANULES >= 2 + 2 * (TILE_GRANULES - 1)
    max_tiles = _max_tiles(t)
    smem = pl.BlockSpec(memory_space=pltpu.SMEM)
    n_map = _map_len(max_tiles)
    return pl.pallas_call(
        functools.partial(_plan_kernel, nb=nb, max_tiles=max_tiles),
        in_specs=[smem],
        out_specs=[smem, smem, smem, smem, smem],
        out_shape=[jax.ShapeDtypeStruct((n_map,), jnp.int32),
                   jax.ShapeDtypeStruct((n_map,), jnp.int32),
                   jax.ShapeDtypeStruct((max_tiles,), jnp.int32),
                   jax.ShapeDtypeStruct((N_EXPERTS,), jnp.int32),
                   jax.ShapeDtypeStruct((1,), jnp.int32)],
        scratch_shapes=[pltpu.SMEM((nb,), jnp.int32)],
        name="moe_plan",
    )(cnt)


def _experts_kernel(gsrc, gdst, texp, tend, ntile, xy_in, ws_in, w1_hbm, w3_hbm, w2_hbm, xy_out,
                    xbuf, wsbuf, ybuf, st1, st3, st2, gsem, ssem, wsem):
    nt = ntile[0]

    def weight_copies(e, ws):
        return (pltpu.make_async_copy(w1_hbm.at[e], st1.at[ws], wsem.at[ws]),
                pltpu.make_async_copy(w3_hbm.at[e], st3.at[ws], wsem.at[ws]),
                pltpu.make_async_copy(w2_hbm.at[e], st2.at[ws], wsem.at[ws]))

    def rows(i):
        return pl.ds(pl.multiple_of(i * GRANULE, GRANULE), GRANULE)

    def gather_copies(g, j, sl):
        return (pltpu.make_async_copy(xy_in.at[rows(g), :], xbuf.at[sl, rows(j), :], gsem.at[sl]),
                pltpu.make_async_copy(ws_in.at[rows(g), :], wsbuf.at[sl, rows(j), :], gsem.at[sl]))

    def scatter_copies(g, j, sl):
        return (pltpu.make_async_copy(ybuf.at[sl, rows(j), :], xy_out.at[rows(g), :], ssem.at[sl]),)

    def issue(tt, sl, gmap, copies):
        for j in range(TILE_GRANULES):
            for cp in copies(gmap[tt * TILE_GRANULES + j], j, sl):
                cp.start()

    def drain(sl, copies):
        full = pl.ds(0, EXPERT_TILE)
        if copies is gather_copies:
            pltpu.make_async_copy(xy_in.at[full, :], xbuf.at[sl], gsem.at[sl]).wait()
            pltpu.make_async_copy(ws_in.at[full, :], wsbuf.at[sl], gsem.at[sl]).wait()
        else:
            pltpu.make_async_copy(ybuf.at[sl], xy_out.at[full, :], ssem.at[sl]).wait()

    @pl.when(nt > 0)
    def _():
        last = nt - 1
        for cp in weight_copies(texp[0], 0):
            cp.start()
        for ahead in range(GATHER_DEPTH - 1):
            issue(ahead, ahead, gsrc, gather_copies)

        def tile(t, wslot):
            slot = t % 2
            gslot = t % GATHER_DEPTH
            e = texp[t]
            first = (t == 0) | (texp[jnp.maximum(t - 1, 0)] != e)
            wslot = jnp.where(first & (t > 0), 1 - wslot, wslot)

            drain(gslot, gather_copies)

            @pl.when(t >= 2)
            def _():
                drain(slot, scatter_copies)

            @pl.when(first)
            def _():
                for cp in weight_copies(e, wslot):
                    cp.wait()
                nxt = tend[e]

                @pl.when(nxt < nt)
                def _():
                    for cp in weight_copies(texp[jnp.minimum(nxt, last)], 1 - wslot):
                        cp.start()

            x = xbuf[gslot]
            w_row = wsbuf[gslot][:, 0:1]
            a = _dot(x, st1[wslot].astype(BF16))
            b = _dot(x, st3[wslot].astype(BF16))
            y = _dot((_silu(a) * b).astype(BF16), st2[wslot].astype(BF16)) * w_row
            ybuf[slot] = y.astype(BF16)
            issue(t, slot, gdst, scatter_copies)
            ahead = t + GATHER_DEPTH - 1
            issue(ahead, ahead % GATHER_DEPTH, gsrc, gather_copies)
            return wslot

        lax.fori_loop(0, nt, tile, 0)

        for k in range(GATHER_DEPTH - 1):
            drain((nt + k) % GATHER_DEPTH, gather_copies)
        drain(last % 2, scatter_copies)

        @pl.when(nt >= 2)
        def _():
            drain(nt % 2, scatter_copies)


def _experts(gsrc, gdst, texp, tend, ntile, xy, ws, w1, w3, w2):
    smem = pl.BlockSpec(memory_space=pltpu.SMEM)
    hbm = pl.BlockSpec(memory_space=pl.ANY)
    return pl.pallas_call(
        _experts_kernel,
        in_specs=[smem, smem, smem, smem, smem, hbm, hbm, hbm, hbm, hbm],
        out_specs=hbm,
        out_shape=jax.ShapeDtypeStruct(xy.shape, xy.dtype),
        scratch_shapes=[pltpu.VMEM((GATHER_DEPTH, EXPERT_TILE, D_MODEL), BF16),
                        pltpu.VMEM((GATHER_DEPTH, EXPERT_TILE, LANES), F32),
                        pltpu.VMEM((2, EXPERT_TILE, D_MODEL), BF16),
                        pltpu.VMEM((2, D_MODEL, D_EXPERT), F32),
                        pltpu.VMEM((2, D_MODEL, D_EXPERT), F32),
                        pltpu.VMEM((2, D_EXPERT, D_MODEL), F32),
                        pltpu.SemaphoreType.DMA((GATHER_DEPTH,)),
                        pltpu.SemaphoreType.DMA((2,)),
                        pltpu.SemaphoreType.DMA((2,))],
        input_output_aliases={5: 0},
        compiler_params=pltpu.CompilerParams(vmem_limit_bytes=VMEM_LIMIT),
        name="moe_experts",
    )(gsrc, gdst, texp, tend, ntile, xy, ws, w1, w3, w2)


def _combine_kernel(y_ref, pos_ref, h1_ref, g2_ref, fg_ref, o_ref):
    pos = pos_ref[...]
    pt1, pt2 = _pair_rows(pos[:, 0:1], pos[:, 1:2])
    moe = _dot((pt1 + pt2).astype(BF16), y_ref[...])
    h = h1_ref[...] + g2_ref[...] * moe
    ms = jnp.mean(h * h, axis=-1, keepdims=True)
    o_ref[...] = h * lax.rsqrt(ms + EPS) * fg_ref[...]


def _combine(xy, pos, h1, g2, final_g):
    t = h1.shape[0]
    row = lambda i: (i, 0)
    const = lambda i: (0, 0)
    return pl.pallas_call(
        _combine_kernel,
        grid=(t // DISPATCH_BLOCK,),
        in_specs=[pl.BlockSpec((LOCAL_CAP, D_MODEL), row),
                  pl.BlockSpec((DISPATCH_BLOCK, LANES), row),
                  pl.BlockSpec((DISPATCH_BLOCK, D_MODEL), row),
                  pl.BlockSpec((1, D_MODEL), const), pl.BlockSpec((1, D_MODEL), const)],
        out_specs=pl.BlockSpec((DISPATCH_BLOCK, D_MODEL), row),
        out_shape=jax.ShapeDtypeStruct((t, D_MODEL), F32),
        compiler_params=_cparams(("arbitrary",)),
        name="moe_combine_final",
    )(xy, pos, h1, g2, final_g)


def _pad_lanes(a):
    return jnp.pad(a, ((0, 0), (0, LANES - a.shape[1])))


def kernel(x, c, ctx, c_ctx, w_mod, b_mod, norm1_g, w_in, w_conv_q, w_conv_k, gate_bias, head_norm_g, w_pool, pool_scale, w_out, norm2_g, w_group, b_group, w_router, b_router, w1, w3, w2, final_g):
    assert x.shape[0] == 1 and w_mod.shape[0] == 1
    seq = x.shape[1]
    x2d = x[0]
    ctx2d = ctx[0]

    n_main = 5 * MIX_HALF
    w_in_t = jnp.transpose(w_in[0])
    w_gate = _pad_lanes(jnp.transpose(_take_rows(w_in_t, n_main, N_GATES))).astype(BF16)
    gate_bias_row = _pad_lanes(gate_bias[0].reshape(1, N_GATES))
    wr_pack, wr_hi = _split_pack(jnp.concatenate([w_group[0], w_router[0]], axis=1))
    b_route = _pad_lanes(jnp.concatenate([b_group[0], b_router[0]]).reshape(1, -1))
    norm1 = norm1_g[0].reshape(1, D_MODEL)

    c16 = jnp.concatenate([c, c_ctx[None, :], jnp.zeros((14, D_MODEL), F32)], axis=0)
    mods = _adaln(c16, w_mod[0], b_mod[0].reshape(1, -1))
    mod_lat = mods[0].reshape(6, D_MODEL)
    mod_ctx = mods[1].reshape(6, D_MODEL)

    mod_in = jnp.concatenate([mod_ctx[0:2], mod_lat[0:2]], axis=0)
    u_pool, q, uo, k, kt, v, col, rowi = _inproj(x2d, ctx2d, mod_in, norm1, w_in_t, w_gate,
                                                 gate_bias_row, w_conv_q[0], w_conv_k[0])
    nc = seq // CHUNK
    zeros_state = (jnp.zeros((2 * HEADS, HEAD_DIM, HEAD_DIM), F32),
                   jnp.zeros((2 * HEADS, 1, HEAD_DIM), F32),
                   jnp.zeros((2 * HEADS, 1, LANES), F32))
    s0, n0, m0 = _mlstm(None, k, kt, v, col, rowi, *zeros_state, need_out=False, first=nc, nc=1)

    hf, hb = _mlstm(q, k, kt, v, col, rowi, s0, n0, m0, need_out=True, first=0, nc=nc)
    p = _pool(u_pool, w_pool[0], pool_scale[0].reshape(1, -1), seq)
    h1, fn, info = _outproj(p, hf, hb, uo, x2d, w_out[0], mod_lat, head_norm_g[0].reshape(1, -1),
                            norm2_g[0].reshape(1, -1), wr_pack, wr_hi, b_route, tm=2 * OUTPROJ_SUB)

    xs, ws, pos, cnt = _dispatch(fn, info)
    gsrc, gdst, texp, tend, ntile = _plan(cnt[::8, :N_EXPERTS], seq)
    xy = _experts(gsrc, gdst, texp, tend, ntile, xs, ws, w1[0], w3[0], w2[0])
    out = _combine(xy, pos, h1, mod_lat[5:6], final_g.reshape(1, -1))
    return out.reshape(1, seq, D_MODEL)
```

```python
import functools

import jax
import jax.numpy as jnp
from jax import lax
from jax.experimental import pallas as pl
from jax.experimental.pallas import tpu as pltpu

F32 = jnp.float32
BF16 = jnp.bfloat16

D_MODEL = 2048
GRID_W = 64
GRID_SHIFT = 6
POOL_WINDOWS = (2, 4, 8, 16)
POOL_GROUP = 256
HEADS = 4
HEAD_DIM = 256
MIX_HALF = 1024
N_GATES = 16
N_GROUPS = 4
EXPERTS_PER_GROUP = 8
N_EXPERTS = 32
D_EXPERT = 512
EPS = 1e-6
LANES = 128
CHUNK = 256
ROUTE_LANE0 = N_GROUPS

VMEM_LIMIT = 60 * 1024 * 1024


def _cparams(sem, vmem=VMEM_LIMIT):
    return pltpu.CompilerParams(dimension_semantics=sem, vmem_limit_bytes=vmem)


def _split2(x):
    hi = x.astype(BF16)
    lo = (x - hi.astype(F32)).astype(BF16)
    return hi, lo


def _split3(x):
    hi = x.astype(BF16)
    r = x - hi.astype(F32)
    mid = r.astype(BF16)
    lo = (r - mid.astype(F32)).astype(BF16)
    return hi, mid, lo


def _dot(a, b):
    return jnp.dot(a, b, preferred_element_type=F32)


SPLIT_LANE = 64


def _split_pack(w):
    hi, lo = _split2(w)
    n = w.shape[1]
    gap = jnp.zeros((w.shape[0], SPLIT_LANE - n), BF16)
    rest = jnp.zeros((w.shape[0], LANES - n), BF16)
    return jnp.concatenate([hi, gap, lo, gap], axis=1), jnp.concatenate([hi, rest], axis=1)


def _split_dot(xh, xl, w_packed, w_hi):
    r = _dot(xh, w_packed)
    return r + pltpu.roll(r, SPLIT_LANE, 1) + _dot(xl, w_hi)


def _silu(x):
    return x * jax.nn.sigmoid(x)


def _log_sigmoid(x):
    return jnp.minimum(x, 0.0) - jnp.log(1.0 + jnp.exp(-jnp.abs(x)))


def _copy_kernel(w_ref, o_ref):
    o_ref[...] = w_ref[...]


def _take_rows(w, start, n):
    return pl.pallas_call(
        _copy_kernel,
        grid=(1,),
        in_specs=[pl.BlockSpec((n, w.shape[1]), lambda i: (start // n, 0))],
        out_specs=pl.BlockSpec((n, w.shape[1]), lambda i: (0, 0)),
        out_shape=jax.ShapeDtypeStruct((n, w.shape[1]), w.dtype),
        name="take_rows",
    )(w)


def _adaln_kernel(c_ref, w_ref, b_ref, o_ref):
    a = _silu(c_ref[...])
    a3 = jnp.concatenate(_split3(a), axis=0)
    w_hi, w_lo = _split2(w_ref[...])
    acc = _dot(a3, w_hi)
    acc_lo = _dot(a3[:32], w_lo)
    out = acc[0:16] + acc[16:32] + acc[32:48] + acc_lo[0:16] + acc_lo[16:32]
    o_ref[...] = out + b_ref[...]


ADALN_COLS = 1536


def _adaln(c16, w_mod, b_mod):
    n = w_mod.shape[1]
    tn = ADALN_COLS
    return pl.pallas_call(
        _adaln_kernel,
        grid=(n // tn,),
        in_specs=[pl.BlockSpec((16, D_MODEL), lambda j: (0, 0)),
                  pl.BlockSpec((D_MODEL, tn), lambda j: (0, j)),
                  pl.BlockSpec((1, tn), lambda j: (0, j))],
        out_specs=pl.BlockSpec((16, tn), lambda j: (0, j)),
        out_shape=jax.ShapeDtypeStruct((16, n), F32),
        compiler_params=_cparams(("arbitrary",)),
        name="adaln",
    )(c16, w_mod, b_mod)


HALO = 8
NEG_INF = float("-inf")


def _gate_scan_info(gates):
    n = gates.shape[0]
    lane = lax.broadcasted_iota(jnp.int32, gates.shape, 1)
    rows = lax.broadcasted_iota(jnp.int32, gates.shape, 0)
    lf = jnp.where(lane < N_GATES, _log_sigmoid(gates), 0.0)
    hi = lf.astype(BF16).astype(F32)
    rem = lf - hi
    mid = rem.astype(BF16).astype(F32)
    packed = (hi + pltpu.roll(mid, 32, 1) + pltpu.roll(rem - mid, 64, 1)).astype(BF16)
    r = lax.broadcasted_iota(jnp.int32, (n, n), 0)
    c = lax.broadcasted_iota(jnp.int32, (n, n), 1)
    pf = _dot(jnp.where(r >= c, 1.0, 0.0).astype(BF16), packed)
    pb = _dot(jnp.where(r <= c, 1.0, 0.0).astype(BF16), packed)
    bf = pf + pltpu.roll(pf, 96, 1) + pltpu.roll(pf, 64, 1)
    bb = pb + pltpu.roll(pb, 96, 1) + pltpu.roll(pb, 64, 1)
    b = jnp.where(lane < 8, bf, bb)
    cval = gates - pltpu.roll(b, LANES - 4, 1)
    pm = cval
    sm = cval
    step = 1
    while step < n:
        pm = jnp.maximum(pm, jnp.where(rows >= step, pltpu.roll(pm, step, 0), NEG_INF))
        sm = jnp.maximum(sm, jnp.where(rows < n - step, pltpu.roll(sm, n - step, 0), NEG_INF))
        step *= 2
    cm = jnp.where(lane < 8, pm, sm)
    return cval, jnp.where((lane & 4) == 0, cm, b)


W_CHUNK = 256


def _load_bf16(src_hbm, dst, stage, sem, *, n_chunks, transpose):
    rows = stage.shape[1]
    def chunk_copy(c, sl):
        return pltpu.make_async_copy(
            src_hbm.at[pl.ds(pl.multiple_of(c * rows, rows), rows), :], stage.at[sl], sem.at[sl])

    chunk_copy(0, 0).start()

    def convert(c, carry):
        sl = c % 2

        @pl.when(c + 1 < n_chunks)
        def _():
            chunk_copy(c + 1, 1 - sl).start()

        chunk_copy(c, sl).wait()
        span = pl.ds(pl.multiple_of(c * rows, rows), rows)
        if transpose:
            dst[:, span] = stage[sl].T.astype(BF16)
        else:
            dst[span, :] = stage[sl].astype(BF16)
        return carry

    lax.fori_loop(0, n_chunks, convert, 0)


def _inproj_kernel(x_ref, xp_ref, xn_ref, ctx_ref, mod_ref, g_ref, wt_hbm, wg_ref, gb_ref,
                   wcq_ref, wck_ref, pool_o, q_o, o_o, k_o, kt_o, v_o, col_o, row_o,
                   w_s, stage, sem, *, nt):
    i = pl.program_id(0)
    tm = x_ref.shape[0]

    @pl.when(i == 0)
    def _():
        _load_bf16(wt_hbm, w_s, stage, sem, n_chunks=w_s.shape[1] // W_CHUNK, transpose=True)

    is_ctx = i == nt
    x_main = jnp.where(is_ctx, ctx_ref[...], x_ref[...])
    x_all = jnp.concatenate([xp_ref[...], x_main, xn_ref[...]], axis=0)
    shift = jnp.where(is_ctx, mod_ref[0:1, :], mod_ref[2:3, :])
    scale = jnp.where(is_ctx, mod_ref[1:2, :], mod_ref[3:4, :])
    ms = jnp.mean(x_all * x_all, axis=-1, keepdims=True)
    y = x_all * lax.rsqrt(ms + EPS) * g_ref[...]
    xn_all = y * (1.0 + scale) + shift
    xh_all = xn_all.astype(BF16)
    xh = xn_all[HALO:HALO + tm].astype(BF16)

    def cols(ci):
        return w_s[:, ci * MIX_HALF:(ci + 1) * MIX_HALF]

    u_q = _dot(xh_all, cols(1))
    u_k = _dot(xh_all, cols(3))
    pool_o[...] = _dot(xh, cols(0))
    o_o[...] = _dot(xh, cols(2))
    v_o[...] = _dot(xh, cols(4)).astype(BF16)

    keep_prev = jnp.where(jnp.logical_or(i == 0, is_ctx), 0.0, 1.0)
    keep_next = jnp.where(jnp.logical_or(i == nt - 1, is_ctx), 0.0, 1.0)

    def conv_silu(u, wc_ref):
        n = u.shape[0]
        u = jnp.concatenate([u[:HALO] * keep_prev, u[HALO:HALO + tm], u[HALO + tm:] * keep_next], axis=0)
        up = pltpu.roll(u, 1, 0)[HALO:HALO + tm]
        un = pltpu.roll(u, n - 1, 0)[HALO:HALO + tm]
        return _silu(wc_ref[0:1, :] * up + wc_ref[1:2, :] * u[HALO:HALO + tm] + wc_ref[2:3, :] * un)

    q_o[...] = (conv_silu(u_q, wcq_ref) * (HEAD_DIM ** -0.5)).astype(BF16)
    k = conv_silu(u_k, wck_ref)
    k_o[...] = k.astype(BF16)
    kt_o[...] = k.astype(BF16).T
    gates = _dot(xh, wg_ref[...]) + gb_ref[...]
    cval, col = _gate_scan_info(gates)
    col_o[...] = col[:, :N_GATES]
    row_o[...] = cval.T[:N_GATES, :]


def _inproj(x2d, ctx2d, mod, g, w_in_t, w_gate, gate_bias_row, wcq, wck):
    t = x2d.shape[0]
    tm = CHUNK
    assert ctx2d.shape[0] == tm
    nt = t // tm
    r8 = tm // HALO
    last8 = t // HALO - 1
    n_main = 5 * MIX_HALF
    rows = t + tm
    const = lambda i: (0, 0)
    row = lambda i: (i, 0)
    f32_out = jax.ShapeDtypeStruct((rows, MIX_HALF), F32)
    bf16_out = jax.ShapeDtypeStruct((rows, MIX_HALF), BF16)
    seq = pl.BlockSpec((tm, MIX_HALF), row)
    return pl.pallas_call(
        functools.partial(_inproj_kernel, nt=nt),
        grid=(nt + 1,),
        in_specs=[pl.BlockSpec((tm, D_MODEL), lambda i: (jnp.minimum(i, nt - 1), 0)),
                  pl.BlockSpec((HALO, D_MODEL), lambda i: (jnp.clip(i * r8 - 1, 0, last8), 0)),
                  pl.BlockSpec((HALO, D_MODEL), lambda i: (jnp.minimum((i + 1) * r8, last8), 0)),
                  pl.BlockSpec((tm, D_MODEL), const),
                  pl.BlockSpec((4, D_MODEL), const),
                  pl.BlockSpec((1, D_MODEL), const),
                  pl.BlockSpec(memory_space=pl.ANY),
                  pl.BlockSpec((D_MODEL, LANES), const),
                  pl.BlockSpec((1, LANES), const),
                  pl.BlockSpec((3, MIX_HALF), const),
                  pl.BlockSpec((3, MIX_HALF), const)],
        out_specs=[seq, seq, seq, seq, pl.BlockSpec((MIX_HALF, tm), lambda i: (0, i)), seq,
                   pl.BlockSpec((tm, N_GATES), row), pl.BlockSpec((N_GATES, tm), lambda i: (0, i))],
        out_shape=[f32_out, bf16_out, f32_out, bf16_out,
                   jax.ShapeDtypeStruct((MIX_HALF, rows), BF16), bf16_out,
                   jax.ShapeDtypeStruct((rows, N_GATES), F32),
                   jax.ShapeDtypeStruct((N_GATES, rows), F32)],
        scratch_shapes=[pltpu.VMEM((D_MODEL, n_main), BF16),
                        pltpu.VMEM((2, W_CHUNK, D_MODEL), F32),
                        pltpu.SemaphoreType.DMA((2,))],
        compiler_params=_cparams(("arbitrary",)),
        name="inproj",
    )(x2d, x2d, x2d, ctx2d, mod, g, w_in_t, w_gate, gate_bias_row, wcq, wck)


def _mlstm_kernel(*refs, need_out):
    if need_out:
        (qf, kf, ktf, vf, colf, rowf, qb, kb, ktb, vb, colb, rowb, s0, n0, m0,
         hf_o, hb_o, s_s, n_s, m_s) = refs
        q_refs, h_outs = (qf, qb), (hf_o, hb_o)
    else:
        (kf, ktf, vf, colf, rowf, kb, ktb, vb, colb, rowb, s0, n0, m0,
         s_o, n_o, m_o, s_s, n_s, m_s) = refs
    k_refs, kt_refs, v_refs = (kf, kb), (ktf, ktb), (vf, vb)
    col_refs, row_refs = (colf, colb), (rowf, rowb)
    j = pl.program_id(0)

    @pl.when(j == 0)
    def _():
        s_s[...] = s0[...]
        n_s[...] = n0[...]
        m_s[...] = m0[...]

    L = CHUNK
    row = lax.broadcasted_iota(jnp.int32, (L, L), 0)
    col = lax.broadcasted_iota(jnp.int32, (L, L), 1)

    heads = [(d, h) for d in range(2) for h in range(HEADS)]

    def head_values(d, h):
        hd = d * HEADS + h
        lc, lb = 8 * d + h, 8 * d + 4 + h
        hs = slice(h * HEAD_DIM, (h + 1) * HEAD_DIM)
        edge = L - 1 if d == 0 else 0
        colv = col_refs[d][...]
        rowv = row_refs[d][...]
        v = dict(hd=hd, hs=hs, d=d, lc=lc, colv=colv)
        v["cm_c"], v["b_c"] = colv[:, lc:lc + 1], colv[:, lb:lb + 1]
        v["c_r"] = rowv[lc:lc + 1, :]
        v["g"] = v["b_c"][edge:edge + 1, :]
        v["m_old"] = m_s[hd][:, 0:1]
        v["s_old"] = s_s[hd]
        v["n_old"] = n_s[hd]
        v["m_x"] = jnp.maximum(v["m_old"], v["cm_c"][edge:edge + 1, :])
        v["decay"] = jnp.exp(v["m_old"] - v["m_x"])
        v["wk_r"] = jnp.exp(v["c_r"] - v["m_x"])
        v["v_h"] = v_refs[d][:, hs]
        v["kt_h"] = kt_refs[d][hs, :]
        return v

    def update_state(v):
        hd = v["hd"]
        n_s[hd] = v["decay"] * v["n_old"] + _dot(
            jnp.broadcast_to(v["wk_r"], (8, L)).astype(BF16), k_refs[v["d"]][:, v["hs"]])[0:1, :]
        kwt = (v["kt_h"].astype(F32) * v["wk_r"]).astype(BF16)
        s_s[hd] = v["decay"] * v["s_old"] + _dot(kwt, v["v_h"])
        m_s[hd] = jnp.broadcast_to(v["g"] + v["m_x"], (1, LANES))

    def weights(v):
        d, hs = v["d"], v["hs"]
        mask = (row >= col) if d == 0 else (row <= col)
        q_h = q_refs[d][:, hs]
        m_b = m_s[v["hd"]][:, :N_GATES]
        m16 = jnp.maximum(m_b, v["colv"])
        m_c = m16[:, v["lc"]:v["lc"] + 1]
        w_inter = jnp.exp(m_b - m16)[:, v["lc"]:v["lc"] + 1]
        v["e"] = jnp.where(mask, jnp.exp(v["c_r"] - m_c), 0.0)
        v["qw"] = (q_h.astype(F32) * w_inter).astype(BF16)
        qn = lax.dot_general(q_h, jnp.broadcast_to(v["n_old"], (N_GATES, HEAD_DIM)).astype(BF16),
                             (((1,), (1,)), ((), ())), preferred_element_type=F32)
        v["den_inter"] = w_inter * qn[:, v["lc"]:v["lc"] + 1]
        v["floor"] = jnp.exp(-(v["b_c"] + m_c))
        v["s_bf"] = v["s_old"].astype(BF16)

    def outputs(v, qk):
        p = (v["e"] * qk).astype(BF16)
        num = _dot(v["qw"], v["s_bf"]) + _dot(p, v["v_h"])
        p_sum = lax.dot_general(p, jnp.ones((N_GATES, L), BF16), (((1,), (1,)), ((), ())),
                                preferred_element_type=F32)
        den = v["den_inter"] + p_sum[:, v["lc"]:v["lc"] + 1]
        h_outs[v["d"]][:, v["hs"]] = num / jnp.maximum(jnp.abs(den), v["floor"])

    vals = [head_values(d, h) for d, h in heads]
    if need_out:
        qks = [_dot(q_refs[v["d"]][:, v["hs"]], v["kt_h"]) for v in vals]
        for v in vals:
            weights(v)
    for v in vals:
        update_state(v)
    if need_out:
        for v, qk in zip(vals, qks):
            outputs(v, qk)

    if not need_out:
        s_o[...] = s_s[...]
        n_o[...] = n_s[...]
        m_o[...] = m_s[...]


def _mlstm(q, k, kt, v, col, rowi, s0, n0, m0, need_out, first, nc):
    t = nc * CHUNK
    fwd = lambda j: (first + j, 0)
    bwd = lambda j: (first + nc - 1 - j, 0)
    fwd_t = lambda j: (0, first + j)
    bwd_t = lambda j: (0, first + nc - 1 - j)
    c3 = lambda j: (0, 0, 0)
    seq = lambda im: pl.BlockSpec((CHUNK, MIX_HALF), im)
    state_specs = [pl.BlockSpec(s0.shape, c3), pl.BlockSpec(n0.shape, c3), pl.BlockSpec(m0.shape, c3)]
    scratch = [pltpu.VMEM(s0.shape, F32), pltpu.VMEM(n0.shape, F32), pltpu.VMEM(m0.shape, F32)]

    def side(im, im_t):
        specs = ([seq(im)] if need_out else []) + [seq(im), pl.BlockSpec((MIX_HALF, CHUNK), im_t), seq(im)]
        return specs + [pl.BlockSpec((CHUNK, N_GATES), im), pl.BlockSpec((N_GATES, CHUNK), im_t)]

    in_specs = side(fwd, fwd_t) + side(bwd, bwd_t) + state_specs
    seq_in = ((q,) if need_out else ()) + (k, kt, v, col, rowi)
    args = seq_in + seq_in + (s0, n0, m0)
    if need_out:
        out_specs = [seq(fwd), seq(bwd)]
        out_shape = [jax.ShapeDtypeStruct((t, MIX_HALF), F32)] * 2
    else:
        out_specs = state_specs
        out_shape = [jax.ShapeDtypeStruct(a.shape, F32) for a in (s0, n0, m0)]
    return pl.pallas_call(
        functools.partial(_mlstm_kernel, need_out=need_out),
        grid=(nc,),
        in_specs=in_specs,
        out_specs=out_specs,
        out_shape=out_shape,
        scratch_shapes=scratch,
        compiler_params=_cparams(("arbitrary",)),
        name="mlstm_out" if need_out else "mlstm_state",
    )(*args)


POOL_PAD = 512
POOL_UNROLL = 4


def _pool_kernel(u_ref, w_ref, sc_ref, o_ref, pad_s, *, t):
    for gi, win in enumerate(POOL_WINDOWS):
        @pl.when(pl.program_id(0) == gi)
        def _():
            _pool_group(u_ref, w_ref, sc_ref, o_ref, pad_s, win=win, t=t)


def _pool_group(u_ref, w_ref, sc_ref, o_ref, pad_s, *, win, t):
    half = win // 2
    tile = 256
    zeros = jnp.zeros((POOL_PAD, POOL_GROUP), F32)
    pad_s[0:POOL_PAD, :] = zeros
    pad_s[POOL_PAD + t:POOL_PAD + t + POOL_PAD, :] = zeros

    def copy(r, carry):
        t0 = pl.multiple_of(r * tile, tile)
        pad_s[pl.ds(POOL_PAD + t0, tile), :] = u_ref[pl.ds(t0, tile), :]
        return carry

    lax.fori_loop(0, t // tile, copy, 0)

    row = lax.broadcasted_iota(jnp.int32, (tile, tile), 0)
    col = lax.broadcasted_iota(jnp.int32, (tile, tile), 1)
    same_row = (row >> GRID_SHIFT) == (col >> GRID_SHIFT)
    in_win = (col - row >= -half) & (col - row < half)
    band = jnp.where(same_row & in_win, 1.0, 0.0).astype(BF16)
    w = w_ref[...].astype(BF16)
    scale = sc_ref[...]
    n_rows = t // GRID_W

    tok0 = lax.broadcasted_iota(jnp.int32, (tile, POOL_GROUP), 0)
    gc = tok0 & (GRID_W - 1)
    inv_h = 1.0 / (jnp.minimum(gc + half, GRID_W) - jnp.maximum(gc - half, 0)).astype(F32)

    def body(r, carry):
        t0s = [pl.multiple_of((r * POOL_UNROLL + k) * tile, tile) for k in range(POOL_UNROLL)]
        pieces = []
        for t0 in t0s:
            acc = pad_s[pl.ds(POOL_PAD + t0 - GRID_W * half, tile), :]
            for dd in range(-half + 1, half):
                acc = acc + pad_s[pl.ds(POOL_PAD + t0 + GRID_W * dd, tile), :]
            gr = (t0 + tok0) >> GRID_SHIFT
            cnt_v = jnp.minimum(gr + half, n_rows) - jnp.maximum(gr - half, 0)
            pieces.append(_split2(acc / cnt_v.astype(F32)))
        means = [(_dot(band, hi) + _dot(band, lo)) * inv_h for hi, lo in pieces]
        diffs = [(m - pad_s[pl.ds(POOL_PAD + t0, tile), :]).astype(BF16) for m, t0 in zip(means, t0s)]
        for d, t0 in zip(diffs, t0s):
            o_ref[pl.ds(t0, tile), :] = (_dot(d, w) * scale).astype(BF16)
        return carry

    lax.fori_loop(0, t // (tile * POOL_UNROLL), body, 0)


def _pool(u_pool, w_pool, scale_row, t):
    return pl.pallas_call(
        functools.partial(_pool_kernel, t=t),
        grid=(len(POOL_WINDOWS),),
        in_specs=[pl.BlockSpec((t, POOL_GROUP), lambda g: (0, g)),
                  pl.BlockSpec((None, POOL_GROUP, POOL_GROUP), lambda g: (g, 0, 0)),
                  pl.BlockSpec((1, POOL_GROUP), lambda g: (0, g))],
        out_specs=pl.BlockSpec((t, POOL_GROUP), lambda g: (0, g)),
        out_shape=jax.ShapeDtypeStruct((t, MIX_HALF), BF16),
        scratch_shapes=[pltpu.VMEM((t + 2 * POOL_PAD, POOL_GROUP), F32)],
        compiler_params=_cparams(("arbitrary",)),
        name="pool_mix",
    )(u_pool, w_pool, scale_row)


OUTPROJ_SUB = 256


def _route(logits):
    lane = lax.broadcasted_iota(jnp.int32, logits.shape, 1).astype(F32)
    neg = -jnp.inf
    big = float(LANES)
    gl = jnp.where(lane < N_GROUPS, logits, neg)
    gmax = jnp.max(gl, axis=1, keepdims=True)
    gsel = jnp.min(jnp.where(gl == gmax, lane, big), axis=1, keepdims=True)
    p_grp = 1.0 / jnp.sum(jnp.exp(gl - gmax), axis=1, keepdims=True)
    lo = ROUTE_LANE0 + EXPERTS_PER_GROUP * gsel
    el = jnp.where((lane >= lo) & (lane < lo + EXPERTS_PER_GROUP), logits, neg)
    m1 = jnp.max(el, axis=1, keepdims=True)
    i1 = jnp.min(jnp.where(el == m1, lane, big), axis=1, keepdims=True)
    el2 = jnp.where(lane == i1, neg, el)
    m2 = jnp.max(el2, axis=1, keepdims=True)
    i2 = jnp.min(jnp.where(el2 == m2, lane, big), axis=1, keepdims=True)
    e2 = jnp.exp(m2 - m1)
    p1 = 1.0 / (1.0 + e2)
    p2 = e2 / (1.0 + e2)
    info = jnp.where(lane == 0.0, i1 - ROUTE_LANE0, 0.0)
    info = jnp.where(lane == 1.0, i2 - ROUTE_LANE0, info)
    info = jnp.where(lane == 2.0, p_grp * p1, info)
    return jnp.where(lane == 3.0, p_grp * p2, info)


def _outproj_kernel(p_ref, hf_ref, hb_ref, uo_ref, x_ref, wout_hbm, mod_ref,
                    hg_ref, n2g_ref, wrp_ref, wrh_ref, br_ref, h1_o, fn_o, info_o, w_s, stage, sem):
    @pl.when(pl.program_id(0) == 0)
    def _():
        _load_bf16(wout_hbm, w_s, stage, sem, n_chunks=w_s.shape[0] // stage.shape[1], transpose=False)

    def mixer_input(rs):
        h = hf_ref[rs, :] + hb_ref[rs, :]
        parts = []
        for hh in range(HEADS):
            hs = h[:, hh * HEAD_DIM:(hh + 1) * HEAD_DIM]
            mu = jnp.mean(hs, axis=-1, keepdims=True)
            ctr = hs - mu
            var = jnp.mean(ctr * ctr, axis=-1, keepdims=True)
            parts.append(ctr * lax.rsqrt(var + EPS))
        hn = jnp.concatenate(parts, axis=1) * hg_ref[...]
        m = (hn * jax.nn.sigmoid(uo_ref[rs, :])).astype(BF16)
        return jnp.concatenate([p_ref[rs, :], m], axis=1)

    def finish(rs, mix):
        h1 = x_ref[rs, :] + mod_ref[2:3, :] * mix
        h1_o[rs, :] = h1
        ms = jnp.mean(h1 * h1, axis=-1, keepdims=True)
        fn = h1 * lax.rsqrt(ms + EPS) * n2g_ref[...]
        fn = fn * (1.0 + mod_ref[4:5, :]) + mod_ref[3:4, :]
        fh, fl = _split2(fn)
        fn_o[rs, :] = fh
        logits = _split_dot(fh, fl, wrp_ref[...], wrh_ref[...]) + br_ref[...]
        info_o[rs, :] = _route(logits)

    tm = x_ref.shape[0]
    subs = [slice(r0, r0 + OUTPROJ_SUB) for r0 in range(0, tm, OUTPROJ_SUB)]
    w = w_s[...]
    mixes = [_dot(mixer_input(rs), w) for rs in subs]
    for rs, mix in zip(subs, mixes):
        finish(rs, mix)


def _outproj(p, hf, hb, uo, x2d, w_out, mod, head_g, norm2_g, wr_pack, wr_hi, b_route, tm):
    t = x2d.shape[0]
    const = lambda i: (0, 0)
    row = lambda i: (i, 0)
    in_specs = ([pl.BlockSpec((tm, MIX_HALF), row)] * 4
                + [pl.BlockSpec((tm, D_MODEL), row),
                   pl.BlockSpec(memory_space=pl.ANY),
                   pl.BlockSpec(mod.shape, const),
                   pl.BlockSpec((1, MIX_HALF), const),
                   pl.BlockSpec((1, D_MODEL), const),
                   pl.BlockSpec((D_MODEL, LANES), const),
                   pl.BlockSpec((D_MODEL, LANES), const),
                   pl.BlockSpec((1, LANES), const)])
    return pl.pallas_call(
        _outproj_kernel,
        grid=(t // tm,),
        in_specs=in_specs,
        out_specs=[pl.BlockSpec((tm, D_MODEL), row), pl.BlockSpec((tm, D_MODEL), row),
                   pl.BlockSpec((tm, LANES), row)],
        out_shape=[jax.ShapeDtypeStruct((t, D_MODEL), F32),
                   jax.ShapeDtypeStruct((t, D_MODEL), BF16),
                   jax.ShapeDtypeStruct((t, LANES), F32)],
        scratch_shapes=[pltpu.VMEM(w_out.shape, BF16),
                        pltpu.VMEM((2, W_CHUNK // 2, w_out.shape[1]), F32),
                        pltpu.SemaphoreType.DMA((2,))],
        compiler_params=_cparams(("arbitrary",)),
        name="outproj_route",
    )(p, hf, hb, uo, x2d, w_out, mod, head_g, norm2_g, wr_pack, wr_hi, b_route)


DISPATCH_BLOCK = 512
GRANULE = 16
GRANULE_SHIFT = 4
LOCAL_CAP = 1536
LOCAL_GRANULES = LOCAL_CAP // GRANULE
FREE_GRANULES = 2
EXPERT_TILE = 256
TILE_GRANULES = EXPERT_TILE // GRANULE
PLAN_UNROLL = 4
PLAN_SLACK = 8
GATHER_DEPTH = 4


def _pair_rows(pos1, pos2):
    rows = lax.broadcasted_iota(jnp.int32, (pos1.shape[0], LOCAL_CAP), 1).astype(F32)
    return jnp.where(rows == pos1, 1.0, 0.0), jnp.where(rows == pos2, 1.0, 0.0)


def _dispatch_kernel(fn_ref, info_ref, xs_o, ws_o, pos_o, cnt_o):
    tb = DISPATCH_BLOCK
    info = info_ref[...]
    e1, e2 = info[:, 0:1], info[:, 1:2]
    w1c, w2c = info[:, 2:3], info[:, 3:4]
    lane = lax.broadcasted_iota(jnp.int32, (tb, LANES), 1).astype(F32)
    o1 = jnp.where(lane == e1, 1.0, 0.0)
    o2 = jnp.where(lane == e2, 1.0, 0.0)
    onehot = o1 + o2
    cnt = jnp.sum(onehot, axis=0, keepdims=True)
    gran = jnp.floor((cnt + (GRANULE - 1)) * (1.0 / GRANULE))
    a = lax.broadcasted_iota(jnp.int32, (LANES, LANES), 0)
    b = lax.broadcasted_iota(jnp.int32, (LANES, LANES), 1)
    upper = jnp.where(a < b, 1.0, 0.0).astype(BF16)
    seg_off = _dot(jnp.broadcast_to(gran, (8, LANES)).astype(BF16), upper)[0:1, :] * GRANULE
    r = lax.broadcasted_iota(jnp.int32, (tb, tb), 0)
    c = lax.broadcasted_iota(jnp.int32, (tb, tb), 1)
    strict = jnp.where(r > c, 1.0, 0.0).astype(BF16)
    rank = _dot(strict, onehot.astype(BF16))
    slot = rank + seg_off
    pos1 = jnp.sum(o1 * slot, axis=1, keepdims=True)
    pos2 = jnp.sum(o2 * slot, axis=1, keepdims=True)
    pt1, pt2 = _pair_rows(pos1, pos2)
    pos_o[...] = jnp.where(lane == 0.0, pos1, jnp.where(lane == 1.0, pos2, 0.0))
    perm = (pt1 + pt2).T.astype(BF16)
    w_slot = jnp.sum((pt1 * w1c + pt2 * w2c).T, axis=1, keepdims=True)
    for c0 in range(0, D_MODEL, 512):
        xs_o[:, c0:c0 + 512] = _dot(perm, fn_ref[:, c0:c0 + 512]).astype(BF16)
    ws_o[...] = jnp.broadcast_to(w_slot, (LOCAL_CAP, LANES))
    cnt_o[...] = jnp.broadcast_to(cnt, (8, LANES)).astype(jnp.int32)


def _dispatch(fn, info):
    t = fn.shape[0]
    nb = t // DISPATCH_BLOCK
    row = lambda i: (i, 0)
    return pl.pallas_call(
        _dispatch_kernel,
        grid=(nb,),
        in_specs=[pl.BlockSpec((DISPATCH_BLOCK, D_MODEL), row),
                  pl.BlockSpec((DISPATCH_BLOCK, LANES), row)],
        out_specs=[pl.BlockSpec((LOCAL_CAP, D_MODEL), row),
                   pl.BlockSpec((LOCAL_CAP, LANES), row),
                   pl.BlockSpec((DISPATCH_BLOCK, LANES), row),
                   pl.BlockSpec((8, LANES), row)],
        out_shape=[jax.ShapeDtypeStruct((nb * LOCAL_CAP, D_MODEL), BF16),
                   jax.ShapeDtypeStruct((nb * LOCAL_CAP, LANES), F32),
                   jax.ShapeDtypeStruct((t, LANES), F32),
                   jax.ShapeDtypeStruct((nb * 8, LANES), jnp.int32)],
        compiler_params=_cparams(("arbitrary",)),
        name="moe_dispatch",
    )(fn, info)


def _map_len(max_tiles):
    return (max_tiles + GATHER_DEPTH - 1) * TILE_GRANULES + PLAN_SLACK


def _free_granule(q):
    return ((q // FREE_GRANULES) * LOCAL_GRANULES + (LOCAL_GRANULES - FREE_GRANULES)
            + (q % FREE_GRANULES))


def _plan_kernel(cnt_ref, gsrc_o, gdst_o, texp_o, tend_o, ntile_o, lrun, *, nb, max_tiles):
    def init(b, c):
        lrun[b] = 0
        return c

    lax.fori_loop(0, nb, init, 0)

    def per_expert(e, carry):
        g0, last_e = carry

        def per_block(b, g):
            k = (cnt_ref[b, e] + (GRANULE - 1)) >> GRANULE_SHIFT
            lo = lrun[b]
            lrun[b] = lo + k

            base = b * LOCAL_GRANULES + lo

            for j in range(PLAN_UNROLL):
                gsrc_o[g + j] = base + j
                gdst_o[g + j] = base + j

            @pl.when(k > PLAN_UNROLL)
            def _():
                def put(j, c):
                    gsrc_o[g + j] = base + j
                    gdst_o[g + j] = base + j
                    return c

                lax.fori_loop(PLAN_UNROLL, k, put, 0)

            return g + k

        g1 = lax.fori_loop(0, nb, per_block, g0, unroll=True)
        pad = (-g1) & (TILE_GRANULES - 1)

        def put_pad(j, c):
            g = g1 + j
            parity = (g // TILE_GRANULES) & 1
            gsrc_o[g] = _free_granule(0)
            gdst_o[g] = _free_granule(1 + parity * (TILE_GRANULES - 1) + (g & (TILE_GRANULES - 1)))
            return c

        lax.fori_loop(0, pad, put_pad, 0)
        g2 = g1 + pad

        def put_tile(tt, c):
            texp_o[tt] = e
            return c

        lax.fori_loop(g0 // TILE_GRANULES, g2 // TILE_GRANULES, put_tile, 0)
        tend_o[e] = g2 // TILE_GRANULES
        return g2, jnp.where(g2 > g0, e, last_e)

    g_end, last_e = lax.fori_loop(0, N_EXPERTS, per_expert, (0, 0))
    n_tiles = g_end // TILE_GRANULES
    ntile_o[0] = n_tiles

    def fill(tt, c):
        texp_o[tt] = last_e
        return c

    lax.fori_loop(n_tiles, max_tiles, fill, 0)

    def fill_map(g, c):
        gsrc_o[g] = _free_granule(0)
        gdst_o[g] = _free_granule(0)
        return c

    lax.fori_loop(g_end, _map_len(max_tiles), fill_map, 0)


def _max_tiles(t):
    nb = t // DISPATCH_BLOCK
    worst_rows = 2 * t + nb * N_EXPERTS * (GRANULE - 1) + N_EXPERTS * (EXPERT_TILE - GRANULE)
    return -(-worst_rows // EXPERT_TILE)


def _plan(cnt, t):
    nb = cnt.shape[0]
    assert nb * FREE_GRANULES >= 2 + 2 * (TILE_GRANULES - 1)
    max_tiles = _max_tiles(t)
    smem = pl.BlockSpec(memory_space=pltpu.SMEM)
    n_map = _map_len(max_tiles)
    return pl.pallas_call(
        functools.partial(_plan_kernel, nb=nb, max_tiles=max_tiles),
        in_specs=[smem],
        out_specs=[smem, smem, smem, smem, smem],
        out_shape=[jax.ShapeDtypeStruct((n_map,), jnp.int32),
                   jax.ShapeDtypeStruct((n_map,), jnp.int32),
                   jax.ShapeDtypeStruct((max_tiles,), jnp.int32),
                   jax.ShapeDtypeStruct((N_EXPERTS,), jnp.int32),
                   jax.ShapeDtypeStruct((1,), jnp.int32)],
        scratch_shapes=[pltpu.SMEM((nb,), jnp.int32)],
        name="moe_plan",
    )(cnt)


def _experts_kernel(gsrc, gdst, texp, tend, ntile, xy_in, ws_in, w1_hbm, w3_hbm, w2_hbm, xy_out,
                    xbuf, wsbuf, ybuf, st1, st3, st2, gsem, ssem, wsem):
    nt = ntile[0]

    def weight_copies(e, ws):
        return (pltpu.make_async_copy(w1_hbm.at[e], st1.at[ws], wsem.at[ws]),
                pltpu.make_async_copy(w3_hbm.at[e], st3.at[ws], wsem.at[ws]),
                pltpu.make_async_copy(w2_hbm.at[e], st2.at[ws], wsem.at[ws]))

    def rows(i):
        return pl.ds(pl.multiple_of(i * GRANULE, GRANULE), GRANULE)

    def gather_copies(g, j, sl):
        return (pltpu.make_async_copy(xy_in.at[rows(g), :], xbuf.at[sl, rows(j), :], gsem.at[sl]),
                pltpu.make_async_copy(ws_in.at[rows(g), :], wsbuf.at[sl, rows(j), :], gsem.at[sl]))

    def scatter_copies(g, j, sl):
        return (pltpu.make_async_copy(ybuf.at[sl, rows(j), :], xy_out.at[rows(g), :], ssem.at[sl]),)

    def issue(tt, sl, gmap, copies):
        for j in range(TILE_GRANULES):
            for cp in copies(gmap[tt * TILE_GRANULES + j], j, sl):
                cp.start()

    def drain(sl, copies):
        full = pl.ds(0, EXPERT_TILE)
        if copies is gather_copies:
            pltpu.make_async_copy(xy_in.at[full, :], xbuf.at[sl], gsem.at[sl]).wait()
            pltpu.make_async_copy(ws_in.at[full, :], wsbuf.at[sl], gsem.at[sl]).wait()
        else:
            pltpu.make_async_copy(ybuf.at[sl], xy_out.at[full, :], ssem.at[sl]).wait()

    @pl.when(nt > 0)
    def _():
        last = nt - 1
        for cp in weight_copies(texp[0], 0):
            cp.start()
        for ahead in range(GATHER_DEPTH - 1):
            issue(ahead, ahead, gsrc, gather_copies)

        def tile(t, wslot):
            slot = t % 2
            gslot = t % GATHER_DEPTH
            e = texp[t]
            first = (t == 0) | (texp[jnp.maximum(t - 1, 0)] != e)
            wslot = jnp.where(first & (t > 0), 1 - wslot, wslot)

            drain(gslot, gather_copies)

            @pl.when(t >= 2)
            def _():
                drain(slot, scatter_copies)

            @pl.when(first)
            def _():
                for cp in weight_copies(e, wslot):
                    cp.wait()
                nxt = tend[e]

                @pl.when(nxt < nt)
                def _():
                    for cp in weight_copies(texp[jnp.minimum(nxt, last)], 1 - wslot):
                        cp.start()

            x = xbuf[gslot]
            w_row = wsbuf[gslot][:, 0:1]
            a = _dot(x, st1[wslot].astype(BF16))
            b = _dot(x, st3[wslot].astype(BF16))
            y = _dot((_silu(a) * b).astype(BF16), st2[wslot].astype(BF16)) * w_row
            ybuf[slot] = y.astype(BF16)
            issue(t, slot, gdst, scatter_copies)
            ahead = t + GATHER_DEPTH - 1
            issue(ahead, ahead % GATHER_DEPTH, gsrc, gather_copies)
            return wslot

        lax.fori_loop(0, nt, tile, 0)

        for k in range(GATHER_DEPTH - 1):
            drain((nt + k) % GATHER_DEPTH, gather_copies)
        drain(last % 2, scatter_copies)

        @pl.when(nt >= 2)
        def _():
            drain(nt % 2, scatter_copies)


def _experts(gsrc, gdst, texp, tend, ntile, xy, ws, w1, w3, w2):
    smem = pl.BlockSpec(memory_space=pltpu.SMEM)
    hbm = pl.BlockSpec(memory_space=pl.ANY)
    return pl.pallas_call(
        _experts_kernel,
        in_specs=[smem, smem, smem, smem, smem, hbm, hbm, hbm, hbm, hbm],
        out_specs=hbm,
        out_shape=jax.ShapeDtypeStruct(xy.shape, xy.dtype),
        scratch_shapes=[pltpu.VMEM((GATHER_DEPTH, EXPERT_TILE, D_MODEL), BF16),
                        pltpu.VMEM((GATHER_DEPTH, EXPERT_TILE, LANES), F32),
                        pltpu.VMEM((2, EXPERT_TILE, D_MODEL), BF16),
                        pltpu.VMEM((2, D_MODEL, D_EXPERT), F32),
                        pltpu.VMEM((2, D_MODEL, D_EXPERT), F32),
                        pltpu.VMEM((2, D_EXPERT, D_MODEL), F32),
                        pltpu.SemaphoreType.DMA((GATHER_DEPTH,)),
                        pltpu.SemaphoreType.DMA((2,)),
                        pltpu.SemaphoreType.DMA((2,))],
        input_output_aliases={5: 0},
        compiler_params=pltpu.CompilerParams(vmem_limit_bytes=VMEM_LIMIT),
        name="moe_experts",
    )(gsrc, gdst, texp, tend, ntile, xy, ws, w1, w3, w2)


def _combine_kernel(y_ref, pos_ref, h1_ref, g2_ref, fg_ref, o_ref):
    pos = pos_ref[...]
    pt1, pt2 = _pair_rows(pos[:, 0:1], pos[:, 1:2])
    moe = _dot((pt1 + pt2).astype(BF16), y_ref[...])
    h = h1_ref[...] + g2_ref[...] * moe
    ms = jnp.mean(h * h, axis=-1, keepdims=True)
    o_ref[...] = h * lax.rsqrt(ms + EPS) * fg_ref[...]


def _combine(xy, pos, h1, g2, final_g):
    t = h1.shape[0]
    row = lambda i: (i, 0)
    const = lambda i: (0, 0)
    return pl.pallas_call(
        _combine_kernel,
        grid=(t // DISPATCH_BLOCK,),
        in_specs=[pl.BlockSpec((LOCAL_CAP, D_MODEL), row),
                  pl.BlockSpec((DISPATCH_BLOCK, LANES), row),
                  pl.BlockSpec((DISPATCH_BLOCK, D_MODEL), row),
                  pl.BlockSpec((1, D_MODEL), const), pl.BlockSpec((1, D_MODEL), const)],
        out_specs=pl.BlockSpec((DISPATCH_BLOCK, D_MODEL), row),
        out_shape=jax.ShapeDtypeStruct((t, D_MODEL), F32),
        compiler_params=_cparams(("arbitrary",)),
        name="moe_combine_final",
    )(xy, pos, h1, g2, final_g)


def _pad_lanes(a):
    return jnp.pad(a, ((0, 0), (0, LANES - a.shape[1])))


def kernel(x, c, ctx, c_ctx, w_mod, b_mod, norm1_g, w_in, w_conv_q, w_conv_k, gate_bias, head_norm_g, w_pool, pool_scale, w_out, norm2_g, w_group, b_group, w_router, b_router, w1, w3, w2, final_g):
    assert x.shape[0] == 1 and w_mod.shape[0] == 1
    seq = x.shape[1]
    x2d = x[0]
    ctx2d = ctx[0]

    n_main = 5 * MIX_HALF
    w_in_t = jnp.transpose(w_in[0])
    w_gate = _pad_lanes(jnp.transpose(_take_rows(w_in_t, n_main, N_GATES))).astype(BF16)
    gate_bias_row = _pad_lanes(gate_bias[0].reshape(1, N_GATES))
    wr_pack, wr_hi = _split_pack(jnp.concatenate([w_group[0], w_router[0]], axis=1))
    b_route = _pad_lanes(jnp.concatenate([b_group[0], b_router[0]]).reshape(1, -1))
    norm1 = norm1_g[0].reshape(1, D_MODEL)

    c16 = jnp.concatenate([c, c_ctx[None, :], jnp.zeros((14, D_MODEL), F32)], axis=0)
    mods = _adaln(c16, w_mod[0], b_mod[0].reshape(1, -1))
    mod_lat = mods[0].reshape(6, D_MODEL)
    mod_ctx = mods[1].reshape(6, D_MODEL)

    mod_in = jnp.concatenate([mod_ctx[0:2], mod_lat[0:2]], axis=0)
    u_pool, q, uo, k, kt, v, col, rowi = _inproj(x2d, ctx2d, mod_in, norm1, w_in_t, w_gate,
                                                 gate_bias_row, w_conv_q[0], w_conv_k[0])
    nc = seq // CHUNK
    zeros_state = (jnp.zeros((2 * HEADS, HEAD_DIM, HEAD_DIM), F32),
                   jnp.zeros((2 * HEADS, 1, HEAD_DIM), F32),
                   jnp.zeros((2 * HEADS, 1, LANES), F32))
    s0, n0, m0 = _mlstm(None, k, kt, v, col, rowi, *zeros_state, need_out=False, first=nc, nc=1)

    hf, hb = _mlstm(q, k, kt, v, col, rowi, s0, n0, m0, need_out=True, first=0, nc=nc)
    p = _pool(u_pool, w_pool[0], pool_scale[0].reshape(1, -1), seq)
    h1, fn, info = _outproj(p, hf, hb, uo, x2d, w_out[0], mod_lat, head_norm_g[0].reshape(1, -1),
                            norm2_g[0].reshape(1, -1), wr_pack, wr_hi, b_route, tm=2 * OUTPROJ_SUB)

    xs, ws, pos, cnt = _dispatch(fn, info)
    gsrc, gdst, texp, tend, ntile = _plan(cnt[::8, :N_EXPERTS], seq)
    xy = _experts(gsrc, gdst, texp, tend, ntile, xs, ws, w1[0], w3[0], w2[0])
    out = _combine(xy, pos, h1, mod_lat[5:6], final_g.reshape(1, -1))
    return out.reshape(1, seq, D_MODEL)
```

```python
import functools

import jax
import jax.numpy as jnp
from jax import lax
from jax.experimental import pallas as pl
from jax.experimental.pallas import tpu as pltpu

F32 = jnp.float32
BF16 = jnp.bfloat16

D_MODEL = 2048
GRID_W = 64
GRID_SHIFT = 6
POOL_WINDOWS = (2, 4, 8, 16)
POOL_GROUP = 256
HEADS = 4
HEAD_DIM = 256
MIX_HALF = 1024
N_GATES = 16
N_GROUPS = 4
EXPERTS_PER_GROUP = 8
N_EXPERTS = 32
D_EXPERT = 512
EPS = 1e-6
LANES = 128
CHUNK = 256
ROUTE_LANE0 = N_GROUPS

VMEM_LIMIT = 60 * 1024 * 1024


def _cparams(sem, vmem=VMEM_LIMIT):
    return pltpu.CompilerParams(dimension_semantics=sem, vmem_limit_bytes=vmem)


def _split2(x):
    hi = x.astype(BF16)
    lo = (x - hi.astype(F32)).astype(BF16)
    return hi, lo


def _split3(x):
    hi = x.astype(BF16)
    r = x - hi.astype(F32)
    mid = r.astype(BF16)
    lo = (r - mid.astype(F32)).astype(BF16)
    return hi, mid, lo


def _dot(a, b):
    return jnp.dot(a, b, preferred_element_type=F32)


SPLIT_LANE = 64


def _split_pack(w):
    hi, lo = _split2(w)
    n = w.shape[1]
    gap = jnp.zeros((w.shape[0], SPLIT_LANE - n), BF16)
    rest = jnp.zeros((w.shape[0], LANES - n), BF16)
    return jnp.concatenate([hi, gap, lo, gap], axis=1), jnp.concatenate([hi, rest], axis=1)


def _split_dot(xh, xl, w_packed, w_hi):
    r = _dot(xh, w_packed)
    return r + pltpu.roll(r, SPLIT_LANE, 1) + _dot(xl, w_hi)


def _silu(x):
    return x * jax.nn.sigmoid(x)


def _log_sigmoid(x):
    return jnp.minimum(x, 0.0) - jnp.log(1.0 + jnp.exp(-jnp.abs(x)))


def _copy_kernel(w_ref, o_ref):
    o_ref[...] = w_ref[...]


def _take_rows(w, start, n):
    return pl.pallas_call(
        _copy_kernel,
        grid=(1,),
        in_specs=[pl.BlockSpec((n, w.shape[1]), lambda i: (start // n, 0))],
        out_specs=pl.BlockSpec((n, w.shape[1]), lambda i: (0, 0)),
        out_shape=jax.ShapeDtypeStruct((n, w.shape[1]), w.dtype),
        name="take_rows",
    )(w)


def _adaln_kernel(c_ref, w_ref, b_ref, o_ref):
    a = _silu(c_ref[...])
    a3 = jnp.concatenate(_split3(a), axis=0)
    w_hi, w_lo = _split2(w_ref[...])
    acc = _dot(a3, w_hi)
    acc_lo = _dot(a3[:32], w_lo)
    out = acc[0:16] + acc[16:32] + acc[32:48] + acc_lo[0:16] + acc_lo[16:32]
    o_ref[...] = out + b_ref[...]


ADALN_COLS = 1536


def _adaln(c16, w_mod, b_mod):
    n = w_mod.shape[1]
    tn = ADALN_COLS
    return pl.pallas_call(
        _adaln_kernel,
        grid=(n // tn,),
        in_specs=[pl.BlockSpec((16, D_MODEL), lambda j: (0, 0)),
                  pl.BlockSpec((D_MODEL, tn), lambda j: (0, j)),
                  pl.BlockSpec((1, tn), lambda j: (0, j))],
        out_specs=pl.BlockSpec((16, tn), lambda j: (0, j)),
        out_shape=jax.ShapeDtypeStruct((16, n), F32),
        compiler_params=_cparams(("arbitrary",)),
        name="adaln",
    )(c16, w_mod, b_mod)


HALO = 8
NEG_INF = float("-inf")


def _gate_scan_info(gates):
    n = gates.shape[0]
    lane = lax.broadcasted_iota(jnp.int32, gates.shape, 1)
    rows = lax.broadcasted_iota(jnp.int32, gates.shape, 0)
    lf = jnp.where(lane < N_GATES, _log_sigmoid(gates), 0.0)
    hi = lf.astype(BF16).astype(F32)
    rem = lf - hi
    mid = rem.astype(BF16).astype(F32)
    packed = (hi + pltpu.roll(mid, 32, 1) + pltpu.roll(rem - mid, 64, 1)).astype(BF16)
    r = lax.broadcasted_iota(jnp.int32, (n, n), 0)
    c = lax.broadcasted_iota(jnp.int32, (n, n), 1)
    pf = _dot(jnp.where(r >= c, 1.0, 0.0).astype(BF16), packed)
    pb = _dot(jnp.where(r <= c, 1.0, 0.0).astype(BF16), packed)
    bf = pf + pltpu.roll(pf, 96, 1) + pltpu.roll(pf, 64, 1)
    bb = pb + pltpu.roll(pb, 96, 1) + pltpu.roll(pb, 64, 1)
    b = jnp.where(lane < 8, bf, bb)
    cval = gates - pltpu.roll(b, LANES - 4, 1)
    pm = cval
    sm = cval
    step = 1
    while step < n:
        pm = jnp.maximum(pm, jnp.where(rows >= step, pltpu.roll(pm, step, 0), NEG_INF))
        sm = jnp.maximum(sm, jnp.where(rows < n - step, pltpu.roll(sm, n - step, 0), NEG_INF))
        step *= 2
    cm = jnp.where(lane < 8, pm, sm)
    return cval, jnp.where((lane & 4) == 0, cm, b)


W_CHUNK = 256


def _load_bf16(src_hbm, dst, stage, sem, *, n_chunks, transpose):
    rows = stage.shape[1]
    def chunk_copy(c, sl):
        return pltpu.make_async_copy(
            src_hbm.at[pl.ds(pl.multiple_of(c * rows, rows), rows), :], stage.at[sl], sem.at[sl])

    chunk_copy(0, 0).start()

    def convert(c, carry):
        sl = c % 2

        @pl.when(c + 1 < n_chunks)
        def _():
            chunk_copy(c + 1, 1 - sl).start()

        chunk_copy(c, sl).wait()
        span = pl.ds(pl.multiple_of(c * rows, rows), rows)
        if transpose:
            dst[:, span] = stage[sl].T.astype(BF16)
        else:
            dst[span, :] = stage[sl].astype(BF16)
        return carry

    lax.fori_loop(0, n_chunks, convert, 0)


def _inproj_kernel(x_ref, xp_ref, xn_ref, ctx_ref, mod_ref, g_ref, wt_hbm, wg_ref, gb_ref,
                   wcq_ref, wck_ref, pool_o, q_o, o_o, k_o, kt_o, v_o, col_o, row_o,
                   w_s, stage, sem, *, nt):
    i = pl.program_id(0)
    tm = x_ref.shape[0]

    @pl.when(i == 0)
    def _():
        _load_bf16(wt_hbm, w_s, stage, sem, n_chunks=w_s.shape[1] // W_CHUNK, transpose=True)

    is_ctx = i == nt
    x_main = jnp.where(is_ctx, ctx_ref[...], x_ref[...])
    x_all = jnp.concatenate([xp_ref[...], x_main, xn_ref[...]], axis=0)
    shift = jnp.where(is_ctx, mod_ref[0:1, :], mod_ref[2:3, :])
    scale = jnp.where(is_ctx, mod_ref[1:2, :], mod_ref[3:4, :])
    ms = jnp.mean(x_all * x_all, axis=-1, keepdims=True)
    y = x_all * lax.rsqrt(ms + EPS) * g_ref[...]
    xn_all = y * (1.0 + scale) + shift
    xh_all = xn_all.astype(BF16)
    xh = xn_all[HALO:HALO + tm].astype(BF16)

    def cols(ci):
        return w_s[:, ci * MIX_HALF:(ci + 1) * MIX_HALF]

    u_q = _dot(xh_all, cols(1))
    u_k = _dot(xh_all, cols(3))
    pool_o[...] = _dot(xh, cols(0))
    o_o[...] = _dot(xh, cols(2))
    v_o[...] = _dot(xh, cols(4)).astype(BF16)

    keep_prev = jnp.where(jnp.logical_or(i == 0, is_ctx), 0.0, 1.0)
    keep_next = jnp.where(jnp.logical_or(i == nt - 1, is_ctx), 0.0, 1.0)

    def conv_silu(u, wc_ref):
        n = u.shape[0]
        u = jnp.concatenate([u[:HALO] * keep_prev, u[HALO:HALO + tm], u[HALO + tm:] * keep_next], axis=0)
        up = pltpu.roll(u, 1, 0)[HALO:HALO + tm]
        un = pltpu.roll(u, n - 1, 0)[HALO:HALO + tm]
        return _silu(wc_ref[0:1, :] * up + wc_ref[1:2, :] * u[HALO:HALO + tm] + wc_ref[2:3, :] * un)

    q_o[...] = (conv_silu(u_q, wcq_ref) * (HEAD_DIM ** -0.5)).astype(BF16)
    k = conv_silu(u_k, wck_ref)
    k_o[...] = k.astype(BF16)
    kt_o[...] = k.astype(BF16).T
    gates = _dot(xh, wg_ref[...]) + gb_ref[...]
    cval, col = _gate_scan_info(gates)
    col_o[...] = col[:, :N_GATES]
    row_o[...] = cval.T[:N_GATES, :]


def _inproj(x2d, ctx2d, mod, g, w_in_t, w_gate, gate_bias_row, wcq, wck):
    t = x2d.shape[0]
    tm = CHUNK
    assert ctx2d.shape[0] == tm
    nt = t // tm
    r8 = tm // HALO
    last8 = t // HALO - 1
    n_main = 5 * MIX_HALF
    rows = t + tm
    const = lambda i: (0, 0)
    row = lambda i: (i, 0)
    f32_out = jax.ShapeDtypeStruct((rows, MIX_HALF), F32)
    bf16_out = jax.ShapeDtypeStruct((rows, MIX_HALF), BF16)
    seq = pl.BlockSpec((tm, MIX_HALF), row)
    return pl.pallas_call(
        functools.partial(_inproj_kernel, nt=nt),
        grid=(nt + 1,),
        in_specs=[pl.BlockSpec((tm, D_MODEL), lambda i: (jnp.minimum(i, nt - 1), 0)),
                  pl.BlockSpec((HALO, D_MODEL), lambda i: (jnp.clip(i * r8 - 1, 0, last8), 0)),
                  pl.BlockSpec((HALO, D_MODEL), lambda i: (jnp.minimum((i + 1) * r8, last8), 0)),
                  pl.BlockSpec((tm, D_MODEL), const),
                  pl.BlockSpec((4, D_MODEL), const),
                  pl.BlockSpec((1, D_MODEL), const),
                  pl.BlockSpec(memory_space=pl.ANY),
                  pl.BlockSpec((D_MODEL, LANES), const),
                  pl.BlockSpec((1, LANES), const),
                  pl.BlockSpec((3, MIX_HALF), const),
                  pl.BlockSpec((3, MIX_HALF), const)],
        out_specs=[seq, seq, seq, seq, pl.BlockSpec((MIX_HALF, tm), lambda i: (0, i)), seq,
                   pl.BlockSpec((tm, N_GATES), row), pl.BlockSpec((N_GATES, tm), lambda i: (0, i))],
        out_shape=[f32_out, bf16_out, f32_out, bf16_out,
                   jax.ShapeDtypeStruct((MIX_HALF, rows), BF16), bf16_out,
                   jax.ShapeDtypeStruct((rows, N_GATES), F32),
                   jax.ShapeDtypeStruct((N_GATES, rows), F32)],
        scratch_shapes=[pltpu.VMEM((D_MODEL, n_main), BF16),
                        pltpu.VMEM((2, W_CHUNK, D_MODEL), F32),
                        pltpu.SemaphoreType.DMA((2,))],
        compiler_params=_cparams(("arbitrary",)),
        name="inproj",
    )(x2d, x2d, x2d, ctx2d, mod, g, w_in_t, w_gate, gate_bias_row, wcq, wck)


def _mlstm_kernel(*refs, need_out):
    if need_out:
        (qf, kf, ktf, vf, colf, rowf, qb, kb, ktb, vb, colb, rowb, s0, n0, m0,
         hf_o, hb_o, s_s, n_s, m_s) = refs
        q_refs, h_outs = (qf, qb), (hf_o, hb_o)
    else:
        (kf, ktf, vf, colf, rowf, kb, ktb, vb, colb, rowb, s0, n0, m0,
         s_o, n_o, m_o, s_s, n_s, m_s) = refs
    k_refs, kt_refs, v_refs = (kf, kb), (ktf, ktb), (vf, vb)
    col_refs, row_refs = (colf, colb), (rowf, rowb)
    j = pl.program_id(0)

    @pl.when(j == 0)
    def _():
        s_s[...] = s0[...]
        n_s[...] = n0[...]
        m_s[...] = m0[...]

    L = CHUNK
    row = lax.broadcasted_iota(jnp.int32, (L, L), 0)
    col = lax.broadcasted_iota(jnp.int32, (L, L), 1)

    heads = [(d, h) for d in range(2) for h in range(HEADS)]

    def head_values(d, h):
        hd = d * HEADS + h
        lc, lb = 8 * d + h, 8 * d + 4 + h
        hs = slice(h * HEAD_DIM, (h + 1) * HEAD_DIM)
        edge = L - 1 if d == 0 else 0
        colv = col_refs[d][...]
        rowv = row_refs[d][...]
        v = dict(hd=hd, hs=hs, d=d, lc=lc, colv=colv)
        v["cm_c"], v["b_c"] = colv[:, lc:lc + 1], colv[:, lb:lb + 1]
        v["c_r"] = rowv[lc:lc + 1, :]
        v["g"] = v["b_c"][edge:edge + 1, :]
        v["m_old"] = m_s[hd][:, 0:1]
        v["s_old"] = s_s[hd]
        v["n_old"] = n_s[hd]
        v["m_x"] = jnp.maximum(v["m_old"], v["cm_c"][edge:edge + 1, :])
        v["decay"] = jnp.exp(v["m_old"] - v["m_x"])
        v["wk_r"] = jnp.exp(v["c_r"] - v["m_x"])
        v["v_h"] = v_refs[d][:, hs]
        v["kt_h"] = kt_refs[d][hs, :]
        return v

    def update_state(v):
        hd = v["hd"]
        n_s[hd] = v["decay"] * v["n_old"] + _dot(
            jnp.broadcast_to(v["wk_r"], (8, L)).astype(BF16), k_refs[v["d"]][:, v["hs"]])[0:1, :]
        kwt = (v["kt_h"].astype(F32) * v["wk_r"]).astype(BF16)
        s_s[hd] = v["decay"] * v["s_old"] + _dot(kwt, v["v_h"])
        m_s[hd] = jnp.broadcast_to(v["g"] + v["m_x"], (1, LANES))

    def weights(v):
        d, hs = v["d"], v["hs"]
        mask = (row >= col) if d == 0 else (row <= col)
        q_h = q_refs[d][:, hs]
        m_b = m_s[v["hd"]][:, :N_GATES]
        m16 = jnp.maximum(m_b, v["colv"])
        m_c = m16[:, v["lc"]:v["lc"] + 1]
        w_inter = jnp.exp(m_b - m16)[:, v["lc"]:v["lc"] + 1]
        v["e"] = jnp.where(mask, jnp.exp(v["c_r"] - m_c), 0.0)
        v["qw"] = (q_h.astype(F32) * w_inter).astype(BF16)
        qn = lax.dot_general(q_h, jnp.broadcast_to(v["n_old"], (N_GATES, HEAD_DIM)).astype(BF16),
                             (((1,), (1,)), ((), ())), preferred_element_type=F32)
        v["den_inter"] = w_inter * qn[:, v["lc"]:v["lc"] + 1]
        v["floor"] = jnp.exp(-(jnp.roll(v["colv"], -4, axis=1) + m16))[:, v["lc"]:v["lc"] + 1]
        v["s_bf"] = v["s_old"].astype(BF16)

    def outputs(v, qk):
        p = (v["e"] * qk).astype(BF16)
        num = _dot(v["qw"], v["s_bf"]) + _dot(p, v["v_h"])
        p_sum = lax.dot_general(p, jnp.ones((N_GATES, L), BF16), (((1,), (1,)), ((), ())),
                                preferred_element_type=F32)
        den = v["den_inter"] + p_sum[:, v["lc"]:v["lc"] + 1]
        h_outs[v["d"]][:, v["hs"]] = num / jnp.maximum(jnp.abs(den), v["floor"])

    vals = [head_values(d, h) for d, h in heads]
    if need_out:
        qks = [_dot(q_refs[v["d"]][:, v["hs"]], v["kt_h"]) for v in vals]
        for v in vals:
            weights(v)
    for v in vals:
        update_state(v)
    if need_out:
        for v, qk in zip(vals, qks):
            outputs(v, qk)

    if not need_out:
        s_o[...] = s_s[...]
        n_o[...] = n_s[...]
        m_o[...] = m_s[...]


def _mlstm(q, k, kt, v, col, rowi, s0, n0, m0, need_out, first, nc):
    t = nc * CHUNK
    fwd = lambda j: (first + j, 0)
    bwd = lambda j: (first + nc - 1 - j, 0)
    fwd_t = lambda j: (0, first + j)
    bwd_t = lambda j: (0, first + nc - 1 - j)
    c3 = lambda j: (0, 0, 0)
    seq = lambda im: pl.BlockSpec((CHUNK, MIX_HALF), im)
    state_specs = [pl.BlockSpec(s0.shape, c3), pl.BlockSpec(n0.shape, c3), pl.BlockSpec(m0.shape, c3)]
    scratch = [pltpu.VMEM(s0.shape, F32), pltpu.VMEM(n0.shape, F32), pltpu.VMEM(m0.shape, F32)]

    def side(im, im_t):
        specs = ([seq(im)] if need_out else []) + [seq(im), pl.BlockSpec((MIX_HALF, CHUNK), im_t), seq(im)]
        return specs + [pl.BlockSpec((CHUNK, N_GATES), im), pl.BlockSpec((N_GATES, CHUNK), im_t)]

    in_specs = side(fwd, fwd_t) + side(bwd, bwd_t) + state_specs
    seq_in = ((q,) if need_out else ()) + (k, kt, v, col, rowi)
    args = seq_in + seq_in + (s0, n0, m0)
    if need_out:
        out_specs = [seq(fwd), seq(bwd)]
        out_shape = [jax.ShapeDtypeStruct((t, MIX_HALF), F32)] * 2
    else:
        out_specs = state_specs
        out_shape = [jax.ShapeDtypeStruct(a.shape, F32) for a in (s0, n0, m0)]
    return pl.pallas_call(
        functools.partial(_mlstm_kernel, need_out=need_out),
        grid=(nc,),
        in_specs=in_specs,
        out_specs=out_specs,
        out_shape=out_shape,
        scratch_shapes=scratch,
        compiler_params=_cparams(("arbitrary",)),
        name="mlstm_out" if need_out else "mlstm_state",
    )(*args)


POOL_PAD = 512
POOL_UNROLL = 4


def _pool_kernel(u_ref, w_ref, sc_ref, o_ref, pad_s, *, t):
    for gi, win in enumerate(POOL_WINDOWS):
        @pl.when(pl.program_id(0) == gi)
        def _():
            _pool_group(u_ref, w_ref, sc_ref, o_ref, pad_s, win=win, t=t)


def _pool_group(u_ref, w_ref, sc_ref, o_ref, pad_s, *, win, t):
    half = win // 2
    tile = 256
    zeros = jnp.zeros((POOL_PAD, POOL_GROUP), F32)
    pad_s[0:POOL_PAD, :] = zeros
    pad_s[POOL_PAD + t:POOL_PAD + t + POOL_PAD, :] = zeros

    def copy(r, carry):
        t0 = pl.multiple_of(r * tile, tile)
        pad_s[pl.ds(POOL_PAD + t0, tile), :] = u_ref[pl.ds(t0, tile), :]
        return carry

    lax.fori_loop(0, t // tile, copy, 0)

    row = lax.broadcasted_iota(jnp.int32, (tile, tile), 0)
    col = lax.broadcasted_iota(jnp.int32, (tile, tile), 1)
    same_row = (row >> GRID_SHIFT) == (col >> GRID_SHIFT)
    in_win = (col - row >= -half) & (col - row < half)
    band = jnp.where(same_row & in_win, 1.0, 0.0).astype(BF16)
    w = w_ref[...].astype(BF16)
    scale = sc_ref[...]
    n_rows = t // GRID_W

    tok0 = lax.broadcasted_iota(jnp.int32, (tile, POOL_GROUP), 0)
    gc = tok0 & (GRID_W - 1)
    inv_h = 1.0 / (jnp.minimum(gc + half, GRID_W) - jnp.maximum(gc - half, 0)).astype(F32)

    def body(r, carry):
        t0s = [pl.multiple_of((r * POOL_UNROLL + k) * tile, tile) for k in range(POOL_UNROLL)]
        pieces = []
        for t0 in t0s:
            acc = pad_s[pl.ds(POOL_PAD + t0 - GRID_W * half, tile), :]
            for dd in range(-half + 1, half):
                acc = acc + pad_s[pl.ds(POOL_PAD + t0 + GRID_W * dd, tile), :]
            gr = (t0 + tok0) >> GRID_SHIFT
            cnt_v = jnp.minimum(gr + half, n_rows) - jnp.maximum(gr - half, 0)
            pieces.append(_split2(acc / cnt_v.astype(F32)))
        means = [(_dot(band, hi) + _dot(band, lo)) * inv_h for hi, lo in pieces]
        diffs = [(m - pad_s[pl.ds(POOL_PAD + t0, tile), :]).astype(BF16) for m, t0 in zip(means, t0s)]
        for d, t0 in zip(diffs, t0s):
            o_ref[pl.ds(t0, tile), :] = (_dot(d, w) * scale).astype(BF16)
        return carry

    lax.fori_loop(0, t // (tile * POOL_UNROLL), body, 0)


def _pool(u_pool, w_pool, scale_row, t):
    return pl.pallas_call(
        functools.partial(_pool_kernel, t=t),
        grid=(len(POOL_WINDOWS),),
        in_specs=[pl.BlockSpec((t, POOL_GROUP), lambda g: (0, g)),
                  pl.BlockSpec((None, POOL_GROUP, POOL_GROUP), lambda g: (g, 0, 0)),
                  pl.BlockSpec((1, POOL_GROUP), lambda g: (0, g))],
        out_specs=pl.BlockSpec((t, POOL_GROUP), lambda g: (0, g)),
        out_shape=jax.ShapeDtypeStruct((t, MIX_HALF), BF16),
        scratch_shapes=[pltpu.VMEM((t + 2 * POOL_PAD, POOL_GROUP), F32)],
        compiler_params=_cparams(("arbitrary",)),
        name="pool_mix",
    )(u_pool, w_pool, scale_row)


OUTPROJ_SUB = 256


def _route(logits):
    lane = lax.broadcasted_iota(jnp.int32, logits.shape, 1).astype(F32)
    neg = -jnp.inf
    big = float(LANES)
    gl = jnp.where(lane < N_GROUPS, logits, neg)
    gmax = jnp.max(gl, axis=1, keepdims=True)
    gsel = jnp.min(jnp.where(gl == gmax, lane, big), axis=1, keepdims=True)
    p_grp = 1.0 / jnp.sum(jnp.exp(gl - gmax), axis=1, keepdims=True)
    lo = ROUTE_LANE0 + EXPERTS_PER_GROUP * gsel
    el = jnp.where((lane >= lo) & (lane < lo + EXPERTS_PER_GROUP), logits, neg)
    m1 = jnp.max(el, axis=1, keepdims=True)
    i1 = jnp.min(jnp.where(el == m1, lane, big), axis=1, keepdims=True)
    el2 = jnp.where(lane == i1, neg, el)
    m2 = jnp.max(el2, axis=1, keepdims=True)
    i2 = jnp.min(jnp.where(el2 == m2, lane, big), axis=1, keepdims=True)
    e2 = jnp.exp(m2 - m1)
    p1 = 1.0 / (1.0 + e2)
    p2 = e2 / (1.0 + e2)
    info = jnp.where(lane == 0.0, i1 - ROUTE_LANE0, 0.0)
    info = jnp.where(lane == 1.0, i2 - ROUTE_LANE0, info)
    info = jnp.where(lane == 2.0, p_grp * p1, info)
    return jnp.where(lane == 3.0, p_grp * p2, info)


def _outproj_kernel(p_ref, hf_ref, hb_ref, uo_ref, x_ref, wout_hbm, mod_ref,
                    hg_ref, n2g_ref, wrp_ref, wrh_ref, br_ref, h1_o, fn_o, info_o, w_s, stage, sem):
    @pl.when(pl.program_id(0) == 0)
    def _():
        _load_bf16(wout_hbm, w_s, stage, sem, n_chunks=w_s.shape[0] // stage.shape[1], transpose=False)

    def mixer_input(rs):
        h = hf_ref[rs, :] + hb_ref[rs, :]
        parts = []
        for hh in range(HEADS):
            hs = h[:, hh * HEAD_DIM:(hh + 1) * HEAD_DIM]
            mu = jnp.mean(hs, axis=-1, keepdims=True)
            ctr = hs - mu
            var = jnp.mean(ctr * ctr, axis=-1, keepdims=True)
            parts.append(ctr * lax.rsqrt(var + EPS))
        hn = jnp.concatenate(parts, axis=1) * hg_ref[...]
        m = (hn * jax.nn.sigmoid(uo_ref[rs, :])).astype(BF16)
        return jnp.concatenate([p_ref[rs, :], m], axis=1)

    def finish(rs, mix):
        h1 = x_ref[rs, :] + mod_ref[2:3, :] * mix
        h1_o[rs, :] = h1
        ms = jnp.mean(h1 * h1, axis=-1, keepdims=True)
        fn = h1 * lax.rsqrt(ms + EPS) * n2g_ref[...]
        fn = fn * (1.0 + mod_ref[4:5, :]) + mod_ref[3:4, :]
        fh, fl = _split2(fn)
        fn_o[rs, :] = fh
        logits = _split_dot(fh, fl, wrp_ref[...], wrh_ref[...]) + br_ref[...]
        info_o[rs, :] = _route(logits)

    tm = x_ref.shape[0]
    subs = [slice(r0, r0 + OUTPROJ_SUB) for r0 in range(0, tm, OUTPROJ_SUB)]
    w = w_s[...]
    mixes = [_dot(mixer_input(rs), w) for rs in subs]
    for rs, mix in zip(subs, mixes):
        finish(rs, mix)


def _outproj(p, hf, hb, uo, x2d, w_out, mod, head_g, norm2_g, wr_pack, wr_hi, b_route, tm):
    t = x2d.shape[0]
    const = lambda i: (0, 0)
    row = lambda i: (i, 0)
    in_specs = ([pl.BlockSpec((tm, MIX_HALF), row)] * 4
                + [pl.BlockSpec((tm, D_MODEL), row),
                   pl.BlockSpec(memory_space=pl.ANY),
                   pl.BlockSpec(mod.shape, const),
                   pl.BlockSpec((1, MIX_HALF), const),
                   pl.BlockSpec((1, D_MODEL), const),
                   pl.BlockSpec((D_MODEL, LANES), const),
                   pl.BlockSpec((D_MODEL, LANES), const),
                   pl.BlockSpec((1, LANES), const)])
    return pl.pallas_call(
        _outproj_kernel,
        grid=(t // tm,),
        in_specs=in_specs,
        out_specs=[pl.BlockSpec((tm, D_MODEL), row), pl.BlockSpec((tm, D_MODEL), row),
                   pl.BlockSpec((tm, LANES), row)],
        out_shape=[jax.ShapeDtypeStruct((t, D_MODEL), F32),
                   jax.ShapeDtypeStruct((t, D_MODEL), BF16),
                   jax.ShapeDtypeStruct((t, LANES), F32)],
        scratch_shapes=[pltpu.VMEM(w_out.shape, BF16),
                        pltpu.VMEM((2, W_CHUNK // 2, w_out.shape[1]), F32),
                        pltpu.SemaphoreType.DMA((2,))],
        compiler_params=_cparams(("arbitrary",)),
        name="outproj_route",
    )(p, hf, hb, uo, x2d, w_out, mod, head_g, norm2_g, wr_pack, wr_hi, b_route)


DISPATCH_BLOCK = 512
GRANULE = 16
GRANULE_SHIFT = 4
LOCAL_CAP = 1536
LOCAL_GRANULES = LOCAL_CAP // GRANULE
FREE_GRANULES = 2
EXPERT_TILE = 256
TILE_GRANULES = EXPERT_TILE // GRANULE
PLAN_UNROLL = 4
PLAN_SLACK = 8
GATHER_DEPTH = 4


def _pair_rows(pos1, pos2):
    rows = lax.broadcasted_iota(jnp.int32, (pos1.shape[0], LOCAL_CAP), 1).astype(F32)
    return jnp.where(rows == pos1, 1.0, 0.0), jnp.where(rows == pos2, 1.0, 0.0)


def _dispatch_kernel(fn_ref, info_ref, xs_o, ws_o, pos_o, cnt_o):
    tb = DISPATCH_BLOCK
    info = info_ref[...]
    e1, e2 = info[:, 0:1], info[:, 1:2]
    w1c, w2c = info[:, 2:3], info[:, 3:4]
    lane = lax.broadcasted_iota(jnp.int32, (tb, LANES), 1).astype(F32)
    o1 = jnp.where(lane == e1, 1.0, 0.0)
    o2 = jnp.where(lane == e2, 1.0, 0.0)
    onehot = o1 + o2
    cnt = jnp.sum(onehot, axis=0, keepdims=True)
    gran = jnp.floor((cnt + (GRANULE - 1)) * (1.0 / GRANULE))
    a = lax.broadcasted_iota(jnp.int32, (LANES, LANES), 0)
    b = lax.broadcasted_iota(jnp.int32, (LANES, LANES), 1)
    upper = jnp.where(a < b, 1.0, 0.0).astype(BF16)
    seg_off = _dot(jnp.broadcast_to(gran, (8, LANES)).astype(BF16), upper)[0:1, :] * GRANULE
    r = lax.broadcasted_iota(jnp.int32, (tb, tb), 0)
    c = lax.broadcasted_iota(jnp.int32, (tb, tb), 1)
    strict = jnp.where(r > c, 1.0, 0.0).astype(BF16)
    rank = _dot(strict, onehot.astype(BF16))
    slot = rank + seg_off
    pos1 = jnp.sum(o1 * slot, axis=1, keepdims=True)
    pos2 = jnp.sum(o2 * slot, axis=1, keepdims=True)
    pt1, pt2 = _pair_rows(pos1, pos2)
    pos_o[...] = jnp.where(lane == 0.0, pos1, jnp.where(lane == 1.0, pos2, 0.0))
    perm = (pt1 + pt2).T.astype(BF16)
    w_slot = jnp.sum((pt1 * w1c + pt2 * w2c).T, axis=1, keepdims=True)
    for c0 in range(0, D_MODEL, 512):
        xs_o[:, c0:c0 + 512] = _dot(perm, fn_ref[:, c0:c0 + 512]).astype(BF16)
    ws_o[...] = jnp.broadcast_to(w_slot, (LOCAL_CAP, LANES))
    cnt_o[...] = jnp.broadcast_to(cnt, (8, LANES)).astype(jnp.int32)


def _dispatch(fn, info):
    t = fn.shape[0]
    nb = t // DISPATCH_BLOCK
    row = lambda i: (i, 0)
    return pl.pallas_call(
        _dispatch_kernel,
        grid=(nb,),
        in_specs=[pl.BlockSpec((DISPATCH_BLOCK, D_MODEL), row),
                  pl.BlockSpec((DISPATCH_BLOCK, LANES), row)],
        out_specs=[pl.BlockSpec((LOCAL_CAP, D_MODEL), row),
                   pl.BlockSpec((LOCAL_CAP, LANES), row),
                   pl.BlockSpec((DISPATCH_BLOCK, LANES), row),
                   pl.BlockSpec((8, LANES), row)],
        out_shape=[jax.ShapeDtypeStruct((nb * LOCAL_CAP, D_MODEL), BF16),
                   jax.ShapeDtypeStruct((nb * LOCAL_CAP, LANES), F32),
                   jax.ShapeDtypeStruct((t, LANES), F32),
                   jax.ShapeDtypeStruct((nb * 8, LANES), jnp.int32)],
        compiler_params=_cparams(("arbitrary",)),
        name="moe_dispatch",
    )(fn, info)


def _map_len(max_tiles):
    return (max_tiles + GATHER_DEPTH - 1) * TILE_GRANULES + PLAN_SLACK


def _free_granule(q):
    return ((q // FREE_GRANULES) * LOCAL_GRANULES + (LOCAL_GRANULES - FREE_GRANULES)
            + (q % FREE_GRANULES))


def _plan_kernel(cnt_ref, gsrc_o, gdst_o, texp_o, tend_o, ntile_o, lrun, *, nb, max_tiles):
    def init(b, c):
        lrun[b] = 0
        return c

    lax.fori_loop(0, nb, init, 0)

    def per_expert(e, carry):
        g0, last_e = carry

        def per_block(b, g):
            k = (cnt_ref[b, e] + (GRANULE - 1)) >> GRANULE_SHIFT
            lo = lrun[b]
            lrun[b] = lo + k

            base = b * LOCAL_GRANULES + lo

            for j in range(PLAN_UNROLL):
                gsrc_o[g + j] = base + j
                gdst_o[g + j] = base + j

            @pl.when(k > PLAN_UNROLL)
            def _():
                def put(j, c):
                    gsrc_o[g + j] = base + j
                    gdst_o[g + j] = base + j
                    return c

                lax.fori_loop(PLAN_UNROLL, k, put, 0)

            return g + k

        g1 = lax.fori_loop(0, nb, per_block, g0, unroll=True)
        pad = (-g1) & (TILE_GRANULES - 1)

        def put_pad(j, c):
            g = g1 + j
            parity = (g // TILE_GRANULES) & 1
            gsrc_o[g] = _free_granule(0)
            gdst_o[g] = _free_granule(1 + parity * (TILE_GRANULES - 1) + (g & (TILE_GRANULES - 1)))
            return c

        lax.fori_loop(0, pad, put_pad, 0)
        g2 = g1 + pad

        def put_tile(tt, c):
            texp_o[tt] = e
            return c

        lax.fori_loop(g0 // TILE_GRANULES, g2 // TILE_GRANULES, put_tile, 0)
        tend_o[e] = g2 // TILE_GRANULES
        return g2, jnp.where(g2 > g0, e, last_e)

    g_end, last_e = lax.fori_loop(0, N_EXPERTS, per_expert, (0, 0))
    n_tiles = g_end // TILE_GRANULES
    ntile_o[0] = n_tiles

    def fill(tt, c):
        texp_o[tt] = last_e
        return c

    lax.fori_loop(n_tiles, max_tiles, fill, 0)

    def fill_map(g, c):
        gsrc_o[g] = _free_granule(0)
        gdst_o[g] = _free_granule(0)
        return c

    lax.fori_loop(g_end, _map_len(max_tiles), fill_map, 0)


def _max_tiles(t):
    nb = t // DISPATCH_BLOCK
    worst_rows = 2 * t + nb * N_EXPERTS * (GRANULE - 1) + N_EXPERTS * (EXPERT_TILE - GRANULE)
    return -(-worst_rows // EXPERT_TILE)


def _plan(cnt, t):
    nb = cnt.shape[0]
    assert nb * FREE_GRANULES >= 2 + 2 * (TILE_GRANULES - 1)
    max_tiles = _max_tiles(t)
    smem = pl.BlockSpec(memory_space=pltpu.SMEM)
    n_map = _map_len(max_tiles)
    return pl.pallas_call(
        functools.partial(_plan_kernel, nb=nb, max_tiles=max_tiles),
        in_specs=[smem],
        out_specs=[smem, smem, smem, smem, smem],
        out_shape=[jax.ShapeDtypeStruct((n_map,), jnp.int32),
                   jax.ShapeDtypeStruct((n_map,), jnp.int32),
                   jax.ShapeDtypeStruct((max_tiles,), jnp.int32),
                   jax.ShapeDtypeStruct((N_EXPERTS,), jnp.int32),
                   jax.ShapeDtypeStruct((1,), jnp.int32)],
        scratch_shapes=[pltpu.SMEM((nb,), jnp.int32)],
        name="moe_plan",
    )(cnt)


def _experts_kernel(gsrc, gdst, texp, tend, ntile, xy_in, ws_in, w1_hbm, w3_hbm, w2_hbm, xy_out,
                    xbuf, wsbuf, ybuf, st1, st3, st2, gsem, ssem, wsem):
    nt = ntile[0]

    def weight_copies(e, ws):
        return (pltpu.make_async_copy(w1_hbm.at[e], st1.at[ws], wsem.at[ws]),
                pltpu.make_async_copy(w3_hbm.at[e], st3.at[ws], wsem.at[ws]),
                pltpu.make_async_copy(w2_hbm.at[e], st2.at[ws], wsem.at[ws]))

    def rows(i):
        return pl.ds(pl.multiple_of(i * GRANULE, GRANULE), GRANULE)

    def gather_copies(g, j, sl):
        return (pltpu.make_async_copy(xy_in.at[rows(g), :], xbuf.at[sl, rows(j), :], gsem.at[sl]),
                pltpu.make_async_copy(ws_in.at[rows(g), :], wsbuf.at[sl, rows(j), :], gsem.at[sl]))

    def scatter_copies(g, j, sl):
        return (pltpu.make_async_copy(ybuf.at[sl, rows(j), :], xy_out.at[rows(g), :], ssem.at[sl]),)

    def issue(tt, sl, gmap, copies):
        for j in range(TILE_GRANULES):
            for cp in copies(gmap[tt * TILE_GRANULES + j], j, sl):
                cp.start()

    def drain(sl, copies):
        full = pl.ds(0, EXPERT_TILE)
        if copies is gather_copies:
            pltpu.make_async_copy(xy_in.at[full, :], xbuf.at[sl], gsem.at[sl]).wait()
            pltpu.make_async_copy(ws_in.at[full, :], wsbuf.at[sl], gsem.at[sl]).wait()
        else:
            pltpu.make_async_copy(ybuf.at[sl], xy_out.at[full, :], ssem.at[sl]).wait()

    @pl.when(nt > 0)
    def _():
        last = nt - 1
        for cp in weight_copies(texp[0], 0):
            cp.start()
        for ahead in range(GATHER_DEPTH - 1):
            issue(ahead, ahead, gsrc, gather_copies)

        def tile(t, wslot):
            slot = t % 2
            gslot = t % GATHER_DEPTH
            e = texp[t]
            first = (t == 0) | (texp[jnp.maximum(t - 1, 0)] != e)
            wslot = jnp.where(first & (t > 0), 1 - wslot, wslot)

            drain(gslot, gather_copies)

            @pl.when(t >= 2)
            def _():
                drain(slot, scatter_copies)

            @pl.when(first)
            def _():
                for cp in weight_copies(e, wslot):
                    cp.wait()
                nxt = tend[e]

                @pl.when(nxt < nt)
                def _():
                    for cp in weight_copies(texp[jnp.minimum(nxt, last)], 1 - wslot):
                        cp.start()

            x = xbuf[gslot]
            w_row = wsbuf[gslot][:, 0:1]
            a = _dot(x, st1[wslot].astype(BF16))
            b = _dot(x, st3[wslot].astype(BF16))
            y = _dot((_silu(a) * b).astype(BF16), st2[wslot].astype(BF16)) * w_row
            ybuf[slot] = y.astype(BF16)
            issue(t, slot, gdst, scatter_copies)
            ahead = t + GATHER_DEPTH - 1
            issue(ahead, ahead % GATHER_DEPTH, gsrc, gather_copies)
            return wslot

        lax.fori_loop(0, nt, tile, 0)

        for k in range(GATHER_DEPTH - 1):
            drain((nt + k) % GATHER_DEPTH, gather_copies)
        drain(last % 2, scatter_copies)

        @pl.when(nt >= 2)
        def _():
            drain(nt % 2, scatter_copies)


def _experts(gsrc, gdst, texp, tend, ntile, xy, ws, w1, w3, w2):
    smem = pl.BlockSpec(memory_space=pltpu.SMEM)
    hbm = pl.BlockSpec(memory_space=pl.ANY)
    return pl.pallas_call(
        _experts_kernel,
        in_specs=[smem, smem, smem, smem, smem, hbm, hbm, hbm, hbm, hbm],
        out_specs=hbm,
        out_shape=jax.ShapeDtypeStruct(xy.shape, xy.dtype),
        scratch_shapes=[pltpu.VMEM((GATHER_DEPTH, EXPERT_TILE, D_MODEL), BF16),
                        pltpu.VMEM((GATHER_DEPTH, EXPERT_TILE, LANES), F32),
                        pltpu.VMEM((2, EXPERT_TILE, D_MODEL), BF16),
                        pltpu.VMEM((2, D_MODEL, D_EXPERT), F32),
                        pltpu.VMEM((2, D_MODEL, D_EXPERT), F32),
                        pltpu.VMEM((2, D_EXPERT, D_MODEL), F32),
                        pltpu.SemaphoreType.DMA((GATHER_DEPTH,)),
                        pltpu.SemaphoreType.DMA((2,)),
                        pltpu.SemaphoreType.DMA((2,))],
        input_output_aliases={5: 0},
        compiler_params=pltpu.CompilerParams(vmem_limit_bytes=VMEM_LIMIT),
        name="moe_experts",
    )(gsrc, gdst, texp, tend, ntile, xy, ws, w1, w3, w2)


def _combine_kernel(y_ref, pos_ref, h1_ref, g2_ref, fg_ref, o_ref):
    pos = pos_ref[...]
    pt1, pt2 = _pair_rows(pos[:, 0:1], pos[:, 1:2])
    moe = _dot((pt1 + pt2).astype(BF16), y_ref[...])
    h = h1_ref[...] + g2_ref[...] * moe
    ms = jnp.mean(h * h, axis=-1, keepdims=True)
    o_ref[...] = h * lax.rsqrt(ms + EPS) * fg_ref[...]


def _combine(xy, pos, h1, g2, final_g):
    t = h1.shape[0]
    row = lambda i: (i, 0)
    const = lambda i: (0, 0)
    return pl.pallas_call(
        _combine_kernel,
        grid=(t // DISPATCH_BLOCK,),
        in_specs=[pl.BlockSpec((LOCAL_CAP, D_MODEL), row),
                  pl.BlockSpec((DISPATCH_BLOCK, LANES), row),
                  pl.BlockSpec((DISPATCH_BLOCK, D_MODEL), row),
                  pl.BlockSpec((1, D_MODEL), const), pl.BlockSpec((1, D_MODEL), const)],
        out_specs=pl.BlockSpec((DISPATCH_BLOCK, D_MODEL), row),
        out_shape=jax.ShapeDtypeStruct((t, D_MODEL), F32),
        compiler_params=_cparams(("arbitrary",)),
        name="moe_combine_final",
    )(xy, pos, h1, g2, final_g)


def _pad_lanes(a):
    return jnp.pad(a, ((0, 0), (0, LANES - a.shape[1])))


def kernel(x, c, ctx, c_ctx, w_mod, b_mod, norm1_g, w_in, w_conv_q, w_conv_k, gate_bias, head_norm_g, w_pool, pool_scale, w_out, norm2_g, w_group, b_group, w_router, b_router, w1, w3, w2, final_g):
    assert x.shape[0] == 1 and w_mod.shape[0] == 1
    seq = x.shape[1]
    x2d = x[0]
    ctx2d = ctx[0]

    n_main = 5 * MIX_HALF
    w_in_t = jnp.transpose(w_in[0])
    w_gate = _pad_lanes(jnp.transpose(_take_rows(w_in_t, n_main, N_GATES))).astype(BF16)
    gate_bias_row = _pad_lanes(gate_bias[0].reshape(1, N_GATES))
    wr_pack, wr_hi = _split_pack(jnp.concatenate([w_group[0], w_router[0]], axis=1))
    b_route = _pad_lanes(jnp.concatenate([b_group[0], b_router[0]]).reshape(1, -1))
    norm1 = norm1_g[0].reshape(1, D_MODEL)

    c16 = jnp.concatenate([c, c_ctx[None, :], jnp.zeros((14, D_MODEL), F32)], axis=0)
    mods = _adaln(c16, w_mod[0], b_mod[0].reshape(1, -1))
    mod_lat = mods[0].reshape(6, D_MODEL)
    mod_ctx = mods[1].reshape(6, D_MODEL)

    mod_in = jnp.concatenate([mod_ctx[0:2], mod_lat[0:2]], axis=0)
    u_pool, q, uo, k, kt, v, col, rowi = _inproj(x2d, ctx2d, mod_in, norm1, w_in_t, w_gate,
                                                 gate_bias_row, w_conv_q[0], w_conv_k[0])
    nc = seq // CHUNK
    zeros_state = (jnp.zeros((2 * HEADS, HEAD_DIM, HEAD_DIM), F32),
                   jnp.zeros((2 * HEADS, 1, HEAD_DIM), F32),
                   jnp.zeros((2 * HEADS, 1, LANES), F32))
    s0, n0, m0 = _mlstm(None, k, kt, v, col, rowi, *zeros_state, need_out=False, first=nc, nc=1)

    hf, hb = _mlstm(q, k, kt, v, col, rowi, s0, n0, m0, need_out=True, first=0, nc=nc)
    p = _pool(u_pool, w_pool[0], pool_scale[0].reshape(1, -1), seq)
    h1, fn, info = _outproj(p, hf, hb, uo, x2d, w_out[0], mod_lat, head_norm_g[0].reshape(1, -1),
                            norm2_g[0].reshape(1, -1), wr_pack, wr_hi, b_route, tm=2 * OUTPROJ_SUB)

    xs, ws, pos, cnt = _dispatch(fn, info)
    gsrc, gdst, texp, tend, ntile = _plan(cnt[::8, :N_EXPERTS], seq)
    xy = _experts(gsrc, gdst, texp, tend, ntile, xs, ws, w1[0], w3[0], w2[0])
    out = _combine(xy, pos, h1, mod_lat[5:6], final_g.reshape(1, -1))
    return out.reshape(1, seq, D_MODEL)
```
